```python
import jax, jax.numpy as jnp
from jax import lax
import numpy as np

D_MODEL = 1024
BATCH = 8
SEQ = 2048
DEPTH = 1

GRID_W = 64
CTX_LEN = 256
HEAD_DIM = 64
ATTN_Q_HEADS = 8
ATTN_KV_HEADS = 2
ATTN_GROUP = ATTN_Q_HEADS // ATTN_KV_HEADS
WINDOW = 128
ATTN_BLOCK = 128
ROPE_BASE = 10000.0
DN_HEADS = 8
DN_HEAD_DIM = 64
CONV_W = 3
CHUNK = 64
N_DIR = 2
D_FF = 4 * D_MODEL
EPS = 1e-6
NEG_INF = -1e30

ATTN_WIDTH = ATTN_Q_HEADS * HEAD_DIM
KV_WIDTH = ATTN_KV_HEADS * HEAD_DIM
DN_WIDTH = DN_HEADS * DN_HEAD_DIM
IN_SPLITS = (ATTN_WIDTH, KV_WIDTH, KV_WIDTH, DN_WIDTH, DN_WIDTH, DN_WIDTH, DN_WIDTH,
             N_DIR * DN_HEADS, N_DIR * DN_HEADS, D_MODEL, D_MODEL)
IN_WIDTH = 4 * DN_WIDTH + ATTN_WIDTH + 2 * KV_WIDTH + 2 * N_DIR * DN_HEADS + 2 * D_MODEL

kernel_name = "hybrid_swa_gdn_dit_layer"


def rms_norm(x, g):
    xf = x.astype(jnp.float32)
    return xf * lax.rsqrt(jnp.mean(xf * xf, axis=-1, keepdims=True) + EPS) * g.astype(jnp.float32)


def l2_normalize(x):
    return x * lax.rsqrt(jnp.sum(x * x, axis=-1, keepdims=True) + EPS)


def split_cols(p):
    offsets = [int(o) for o in np.cumsum(IN_SPLITS)[:-1]]
    return jnp.split(p, offsets, axis=-1)


def axial_rope(x, row, col):
    half = HEAD_DIM // 2
    n_freq = half // 2
    freqs = ROPE_BASE ** (-jnp.arange(n_freq, dtype=jnp.float32) / n_freq)

    def rot(xh, pos):
        ang = pos.astype(jnp.float32)[:, None] * freqs
        cos = jnp.cos(ang)[None, :, None, :]
        sin = jnp.sin(ang)[None, :, None, :]
        x1, x2 = xh[..., :n_freq], xh[..., n_freq:]
        return jnp.concatenate([x1 * cos - x2 * sin, x1 * sin + x2 * cos], axis=-1)

    return jnp.concatenate([rot(x[..., :half], row), rot(x[..., half:], col)], axis=-1)


def centred_short_conv(x, w):
    pad = CONV_W // 2
    T = x.shape[1]
    xp = jnp.pad(x, ((0, 0), (pad, CONV_W - 1 - pad), (0, 0)))
    y = sum(xp[:, tap:tap + T] * w[tap] for tap in range(CONV_W))
    return jax.nn.silu(y)


def gated_delta_chunked(q, k, v, g_log, beta, s0):
    B, T, H, dk = q.shape
    n = T // CHUNK

    def chunks(a):
        return jnp.moveaxis(a.reshape((B, n, CHUNK, H) + a.shape[3:]), 3, 1)

    qc, kc, vc = chunks(q), chunks(k), chunks(v)
    gc, bc = chunks(g_log), chunks(beta)
    G = jnp.cumsum(gc, axis=-1)
    idx = jnp.arange(CHUNK)
    strict = idx[:, None] > idx[None, :]
    incl = idx[:, None] >= idx[None, :]
    diff = G[..., :, None] - G[..., None, :]
    dec_strict = jnp.where(strict, jnp.exp(jnp.where(strict, diff, 0.0)), 0.0)
    dec_incl = jnp.where(incl, jnp.exp(jnp.where(incl, diff, 0.0)), 0.0)
    kk = jnp.einsum('bhnid,bhnjd->bhnij', kc, kc)
    t_mat = jnp.eye(CHUNK, dtype=jnp.float32) + bc[..., :, None] * dec_strict * kk
    eG = jnp.exp(G)
    w_mat = lax.linalg.triangular_solve(t_mat, (bc * eG)[..., None] * kc, left_side=True, lower=True,
                                        unit_diagonal=True)
    u_v = lax.linalg.triangular_solve(t_mat, bc[..., None] * vc, left_side=True, lower=True,
                                      unit_diagonal=True)
    intra = dec_incl * jnp.einsum('bhnid,bhnjd->bhnij', qc, kc)
    q_dec = eG[..., None] * qc
    g_last = G[..., -1]
    k_tail = jnp.exp(g_last[..., None] - G)[..., None] * kc

    def step(S, xs):
        w_c, uv_c, qd_c, intra_c, kt_c, gl_c = xs
        u = uv_c - jnp.einsum('bhck,bhkv->bhcv', w_c, S)
        o = jnp.einsum('bhck,bhkv->bhcv', qd_c, S) + jnp.einsum('bhij,bhjv->bhiv', intra_c, u)
        S = jnp.exp(gl_c)[..., None, None] * S + jnp.einsum('bhck,bhcv->bhkv', kt_c, u)
        return S, o

    xs = tuple(jnp.moveaxis(a, 2, 0) for a in (w_mat, u_v, q_dec, intra, k_tail, g_last))
    s_final, o = lax.scan(step, s0, xs)
    o = jnp.transpose(o, (1, 0, 3, 2, 4)).reshape(B, T, H, v.shape[-1])
    return o, s_final


def bidir_delta(q, k, v, g, beta, s0_f, s0_b):
    o_f, s_f = gated_delta_chunked(q, k, v, g[:, :, 0], beta[:, :, 0], s0_f)
    flip = lambda a: jnp.flip(a, axis=1)
    o_b, s_b = gated_delta_chunked(flip(q), flip(k), flip(v), flip(g[:, :, 1]), flip(beta[:, :, 1]), s0_b)
    return o_f + flip(o_b), s_f, s_b


def window_ctx_attention(q, k, v, k_ctx, v_ctx, sink):
    B, S = q.shape[:2]
    L = k_ctx.shape[1]
    nb = S // ATTN_BLOCK
    qb = q.reshape(B, nb, ATTN_BLOCK, ATTN_KV_HEADS, ATTN_GROUP, HEAD_DIM)

    def band(a):
        ap = jnp.pad(a, ((0, 0), (ATTN_BLOCK, ATTN_BLOCK), (0, 0), (0, 0)))
        ap = ap.reshape(B, nb + 2, ATTN_BLOCK, ATTN_KV_HEADS, HEAD_DIM)
        return jnp.concatenate([ap[:, :-2], ap[:, 1:-1], ap[:, 2:]], axis=2)

    kb, vb = band(k), band(v)
    blk = jnp.arange(nb)[:, None, None]
    qpos = blk * ATTN_BLOCK + jnp.arange(ATTN_BLOCK)[None, :, None]
    kpos = (blk - 1) * ATTN_BLOCK + jnp.arange(3 * ATTN_BLOCK)[None, None, :]
    valid = (jnp.abs(qpos - kpos) <= WINDOW) & (kpos >= 0) & (kpos < S)
    scale = HEAD_DIM ** -0.5
    s_loc = jnp.einsum('bnqgrd,bnkgd->bngrqk', qb, kb) * scale
    s_loc = jnp.where(valid[None, :, None, None], s_loc, NEG_INF)
    s_ctx = jnp.einsum('bnqgrd,bkgd->bngrqk', qb, k_ctx) * scale
    s_sink = jnp.broadcast_to(sink.reshape(ATTN_KV_HEADS, ATTN_GROUP)[None, None, :, :, None, None],
                              s_loc.shape[:-1] + (1,))
    p = jax.nn.softmax(jnp.concatenate([s_loc, s_ctx, s_sink], axis=-1).astype(jnp.float32), axis=-1)
    n_loc = 3 * ATTN_BLOCK
    o = (jnp.einsum('bngrqk,bnkgd->bnqgrd', p[..., :n_loc], vb)
         + jnp.einsum('bngrqk,bkgd->bnqgrd', p[..., n_loc:n_loc + L], v_ctx))
    return o.reshape(B, S, ATTN_WIDTH)


def ctx_self_attention(q, k, v, sink):
    B, L = q.shape[:2]
    qg = q.reshape(B, L, ATTN_KV_HEADS, ATTN_GROUP, HEAD_DIM)
    s = jnp.einsum('bqgrd,bkgd->bgrqk', qg, k) * HEAD_DIM ** -0.5
    s_sink = jnp.broadcast_to(sink.reshape(ATTN_KV_HEADS, ATTN_GROUP)[None, :, :, None, None], s.shape[:-1] + (1,))
    p = jax.nn.softmax(jnp.concatenate([s, s_sink], axis=-1).astype(jnp.float32), axis=-1)
    o = jnp.einsum('bgrqk,bkgd->bqgrd', p[..., :L], v)
    return o.reshape(B, L, ATTN_WIDTH)


def trunk_layer(x, ctx, c, c_ctx, w_ada, b_ada, g_norm1, w_in, q_norm_g, k_norm_g, attn_sink, conv_w,
                a_log, dt_bias, dn_norm_g, w_br_attn, w_br_dn, w_out, g_norm2, w_mlp1, w_mlp2, update_ctx):
    B, S = x.shape[:2]
    L = ctx.shape[1]
    rows = S // GRID_W
    row = jnp.broadcast_to(jnp.arange(rows)[:, None], (rows, GRID_W)).reshape(-1)
    col = jnp.broadcast_to(jnp.arange(GRID_W)[None, :], (rows, GRID_W)).reshape(-1)

    mod = jax.nn.silu(c.astype(jnp.float32)) @ w_ada + b_ada
    mod_c = jax.nn.silu(c_ctx.astype(jnp.float32)) @ w_ada + b_ada
    sh1, sc1, gt1, sh2, sc2, gt2 = [m[:, None] for m in jnp.split(mod, 6, axis=-1)]
    csh1, csc1, cgt1, csh2, csc2, cgt2 = jnp.split(mod_c, 6, axis=-1)

    h = rms_norm(x, g_norm1) * (1.0 + sc1) + sh1
    hc = rms_norm(ctx, g_norm1) * (1.0 + csc1) + csh1
    aq, ak, av, dq, dk, dv, dz, da, db, ga, gd = split_cols(h @ w_in)
    caq, cak, cav, cdq, cdk, cdv, cdz, cda, cdb, cga, cgd = split_cols(hc @ w_in)

    def heads(a, n_heads):
        return a.reshape(a.shape[0], a.shape[1], n_heads, HEAD_DIM)

    q_a = axial_rope(rms_norm(heads(aq, ATTN_Q_HEADS), q_norm_g), row, col)
    k_a = axial_rope(rms_norm(heads(ak, ATTN_KV_HEADS), k_norm_g), row, col)
    v_a = heads(av, ATTN_KV_HEADS)
    k_ac = rms_norm(heads(cak, ATTN_KV_HEADS), k_norm_g)
    v_ac = heads(cav, ATTN_KV_HEADS)
    y_attn = window_ctx_attention(q_a, k_a, v_a, k_ac, v_ac, attn_sink)

    def dn_inputs(pq, pk, pv, pa, pb):
        qkv = centred_short_conv(jnp.concatenate([pq, pk, pv], axis=-1), conv_w)
        q_d, k_d, v_d = jnp.split(qkv, 3, axis=-1)
        q_d = l2_normalize(heads(q_d, DN_HEADS)) * DN_HEAD_DIM ** -0.5
        k_d = l2_normalize(heads(k_d, DN_HEADS))
        v_d = heads(v_d, DN_HEADS)
        T = pq.shape[1]
        beta = jax.nn.sigmoid(pb.reshape(pb.shape[0], T, N_DIR, DN_HEADS))
        g = -jnp.exp(a_log) * jax.nn.softplus(pa.reshape(pa.shape[0], T, N_DIR, DN_HEADS) + dt_bias)
        return q_d, k_d, v_d, g, beta

    s_zero = jnp.zeros((B, DN_HEADS, DN_HEAD_DIM, DN_HEAD_DIM), jnp.float32)
    o_dc, s_cf, s_cb = bidir_delta(*dn_inputs(cdq, cdk, cdv, cda, cdb), s_zero, s_zero)
    o_dl, _, _ = bidir_delta(*dn_inputs(dq, dk, dv, da, db), s_cf, s_cb)

    def merge(y_a, o_d, z, gate_a, gate_d):
        y_d = (rms_norm(o_d, dn_norm_g) * jax.nn.silu(heads(z, DN_HEADS))).reshape(z.shape)
        y = jax.nn.sigmoid(gate_a) * (y_a @ w_br_attn) + jax.nn.sigmoid(gate_d) * (y_d @ w_br_dn)
        return y @ w_out

    def mlp(s, shift, scale_m):
        hm = rms_norm(s, g_norm2) * (1.0 + scale_m) + shift
        return jnp.square(jax.nn.relu(hm @ w_mlp1)) @ w_mlp2

    x_new = x.astype(jnp.float32) + gt1 * merge(y_attn, o_dl, dz, ga, gd)
    x_new = x_new + gt2 * mlp(x_new, sh2, sc2)

    if update_ctx:
        q_ac = rms_norm(heads(caq, ATTN_Q_HEADS), q_norm_g)
        y_attn_c = ctx_self_attention(q_ac, k_ac, v_ac, attn_sink)
        ctx = ctx.astype(jnp.float32) + cgt1 * merge(y_attn_c, o_dc, cdz, cga, cgd)
        ctx = ctx + cgt2 * mlp(ctx, csh2, csc2)
    return x_new, ctx


def setup_inputs(seed: int = 0) -> dict:
    key = jax.random.key(seed)
    ks = jax.random.split(key, 24)
    f32 = jnp.float32
    nrm = lambda k, shape, s: jax.random.normal(k, shape, f32) * s
    dt = jnp.exp(jax.random.uniform(ks[12], (DEPTH, N_DIR, DN_HEADS), f32, np.log(1e-3), np.log(1e-1)))
    return {
        "x": nrm(ks[0], (BATCH, SEQ, D_MODEL), 1.0),
        "c": nrm(ks[1], (BATCH, D_MODEL), 1.0),
        "ctx": nrm(ks[2], (BATCH, CTX_LEN, D_MODEL), 1.0),
        "c_ctx": nrm(ks[3], (D_MODEL,), 1.0),
        "w_ada": nrm(ks[4], (DEPTH, D_MODEL, 6 * D_MODEL), 0.5 * D_MODEL ** -0.5),
        "b_ada": nrm(ks[5], (DEPTH, 6 * D_MODEL), 0.02),
        "g_norm1": 1.0 + nrm(ks[6], (DEPTH, D_MODEL), 0.1),
        "w_in": nrm(ks[7], (DEPTH, D_MODEL, IN_WIDTH), D_MODEL ** -0.5),
        "q_norm_g": 1.0 + nrm(ks[8], (DEPTH, HEAD_DIM), 0.1),
        "k_norm_g": 1.0 + nrm(ks[9], (DEPTH, HEAD_DIM), 0.1),
        "attn_sink": nrm(ks[10], (DEPTH, ATTN_Q_HEADS), 0.5),
        "conv_w": nrm(ks[11], (DEPTH, CONV_W, 3 * DN_WIDTH), CONV_W ** -0.5),
        "a_log": jnp.log(jax.random.uniform(ks[13], (DEPTH, N_DIR, DN_HEADS), f32, 1.0, 16.0)),
        "dt_bias": dt + jnp.log(-jnp.expm1(-dt)),
        "dn_norm_g": 1.0 + nrm(ks[14], (DEPTH, DN_HEAD_DIM), 0.1),
        "w_br_attn": nrm(ks[15], (DEPTH, ATTN_WIDTH, D_MODEL), ATTN_WIDTH ** -0.5),
        "w_br_dn": nrm(ks[16], (DEPTH, DN_WIDTH, D_MODEL), DN_WIDTH ** -0.5),
        "w_out": nrm(ks[17], (DEPTH, D_MODEL, D_MODEL), D_MODEL ** -0.5),
        "g_norm2": 1.0 + nrm(ks[18], (DEPTH, D_MODEL), 0.1),
        "w_mlp1": nrm(ks[19], (DEPTH, D_MODEL, D_FF), D_MODEL ** -0.5),
        "w_mlp2": nrm(ks[20], (DEPTH, D_FF, D_MODEL), D_FF ** -0.5),
    }


def reference(x, c, ctx, c_ctx, w_ada, b_ada, g_norm1, w_in, q_norm_g, k_norm_g, attn_sink, conv_w,
              a_log, dt_bias, dn_norm_g, w_br_attn, w_br_dn, w_out, g_norm2, w_mlp1, w_mlp2):
    out_dtype = x.dtype
    for layer in range(DEPTH):
        x, ctx = trunk_layer(x, ctx, c, c_ctx, w_ada[layer], b_ada[layer], g_norm1[layer], w_in[layer],
                             q_norm_g[layer], k_norm_g[layer], attn_sink[layer], conv_w[layer],
                             a_log[layer], dt_bias[layer], dn_norm_g[layer], w_br_attn[layer],
                             w_br_dn[layer], w_out[layer], g_norm2[layer], w_mlp1[layer], w_mlp2[layer],
                             update_ctx=layer < DEPTH - 1)
    return x.astype(out_dtype)
```

```python
import functools

import numpy as np
import jax
import jax.numpy as jnp
from jax import lax
from jax.experimental import pallas as pl
from jax.experimental.pallas import tpu as pltpu

F32 = jnp.float32
BF16 = jnp.bfloat16

D_MODEL = 1024
GRID_W = 64
HEAD_DIM = 64
ATTN_Q_HEADS = 8
ATTN_KV_HEADS = 2
ATTN_GROUP = ATTN_Q_HEADS // ATTN_KV_HEADS
WINDOW = 128
ATTN_BLOCK = 128
ROPE_BASE = 10000.0
DN_HEADS = 8
CONV_W = 3
CHUNK = 64
N_DIR = 2
D_FF = 4 * D_MODEL
EPS = 1e-6
NEG_INF = -1e30

ATTN_WIDTH = ATTN_Q_HEADS * HEAD_DIM
KV_WIDTH = ATTN_KV_HEADS * HEAD_DIM
DN_WIDTH = DN_HEADS * HEAD_DIM
LANES = 128
MXU_TILE = 256
HEADS_PER_TILE = MXU_TILE // HEAD_DIM
VMEM_LIMIT = 56 * 1024 * 1024

_OFF_AQ = 0
_OFF_AK = _OFF_AQ + ATTN_WIDTH
_OFF_AV = _OFF_AK + KV_WIDTH
_OFF_DQ = _OFF_AV + KV_WIDTH
_OFF_DK = _OFF_DQ + DN_WIDTH
_OFF_DV = _OFF_DK + DN_WIDTH
_OFF_DZ = _OFF_DV + DN_WIDTH
_OFF_DA = _OFF_DZ + DN_WIDTH
_OFF_DB = _OFF_DA + N_DIR * DN_HEADS
_OFF_GA = _OFF_DB + N_DIR * DN_HEADS
_OFF_GD = _OFF_GA + D_MODEL
_IN_WIDTH = _OFF_GD + D_MODEL


def _sigmoid(x):
    return 1.0 / (1.0 + jnp.exp(-x))


def _dot(a, b):
    return jnp.dot(a, b, preferred_element_type=F32)


def _dot_nt(a, b):
    return lax.dot_general(a, b, (((1,), (1,)), ((), ())), preferred_element_type=F32)


def _dot_tn(a, b):
    return lax.dot_general(a, b, (((0,), (0,)), ((), ())), preferred_element_type=F32)


def _cparams(n_axes):
    return pltpu.CompilerParams(dimension_semantics=("arbitrary",) * n_axes, vmem_limit_bytes=VMEM_LIMIT)


def _ada_kernel(c_ref, w_ref, b_ref, o_ref):
    c = c_ref[...]
    s = c * _sigmoid(c)
    o_ref[...] = _dot(s.astype(BF16), w_ref[...].astype(BF16)) + b_ref[...]


def _ada(cc, w_ada, b_ada):
    rows, d = cc.shape
    n = w_ada.shape[1]
    tn = 1536
    return pl.pallas_call(
        _ada_kernel,
        grid=(n // tn,),
        in_specs=[pl.BlockSpec((rows, d), lambda j: (0, 0)),
                  pl.BlockSpec((d, tn), lambda j: (0, j)),
                  pl.BlockSpec((1, tn), lambda j: (0, j))],
        out_specs=pl.BlockSpec((rows, tn), lambda j: (0, j)),
        out_shape=jax.ShapeDtypeStruct((rows, n), F32),
        compiler_params=_cparams(1),
        name="ada",
    )(cc, w_ada, b_ada.reshape(1, n))


def _inproj_kernel(x_ref, mod_ref, g_ref, w_ref, *out_refs, segs):
    x = x_ref[0]
    ms = jnp.mean(x * x, axis=-1, keepdims=True)
    mod = mod_ref[0]
    scale = g_ref[...] * (1.0 + mod[1:2])
    h = (x * lax.rsqrt(ms + EPS) * scale + mod[0:1]).astype(BF16)
    for o_ref, (start, size) in zip(out_refs, segs):
        o_ref[0] = _dot(h, w_ref[:, start:start + size]).astype(o_ref.dtype)


def _inproj(x, mod3, mod_row, g_norm, w, segs, dtypes, tm):
    b, t, d = x.shape
    n = w.shape[1]
    if mod_row is None:
        mod_map = lambda bi, i: (bi, 0, 0)
    else:
        mod_map = lambda bi, i: (mod_row, 0, 0)
    out_shape = [jax.ShapeDtypeStruct((b, t, size), dt) for (_, size), dt in zip(segs, dtypes)]
    out_specs = [pl.BlockSpec((1, tm, size), lambda bi, i: (bi, i, 0)) for (_, size) in segs]
    return pl.pallas_call(
        functools.partial(_inproj_kernel, segs=segs),
        grid=(b, t // tm),
        in_specs=[pl.BlockSpec((1, tm, d), lambda bi, i: (bi, i, 0)),
                  pl.BlockSpec((1, 6, d), mod_map),
                  pl.BlockSpec((1, d), lambda bi, i: (0, 0)),
                  pl.BlockSpec((d, n), lambda bi, i: (0, 0))],
        out_specs=out_specs,
        out_shape=out_shape,
        compiler_params=_cparams(2),
        name="inproj",
    )(x, mod3, g_norm.reshape(1, d), w)


def _dnpre_kernel(x_ref, prev_ref, next_ref, cw_ref, hsum_ref, *out_refs, tm, halo, norm_flags, q_flags):
    i = pl.program_id(1)
    last = pl.num_programs(1) - 1
    x = x_ref[0].astype(F32)
    prev_row = prev_ref[0][halo - 1:halo, :].astype(F32) * (i > 0).astype(F32)
    next_row = next_ref[0][0:1, :].astype(F32) * (i < last).astype(F32)
    rows = lax.broadcasted_iota(jnp.int32, x.shape, 0)
    xm = jnp.where(rows == 0, prev_row, pltpu.roll(x, 1, 0))
    xp = jnp.where(rows == tm - 1, next_row, pltpu.roll(x, tm - 1, 0))
    cw = cw_ref[...]
    y = xm * cw[0:1] + x * cw[1:2] + xp * cw[2:3]
    y = y * _sigmoid(y)
    hsum = hsum_ref[...]
    for s, (o_ref, do_norm, is_q) in enumerate(zip(out_refs, norm_flags, q_flags)):
        for j in range(DN_WIDTH // MXU_TILE):
            lo = s * DN_WIDTH + j * MXU_TILE
            ys = y[:, lo:lo + MXU_TILE]
            if do_norm:
                ss = _dot((ys * ys).astype(BF16), hsum)
                ys = ys * lax.rsqrt(ss + EPS)
                if is_q:
                    ys = ys * (HEAD_DIM ** -0.5)
            o_ref[0, :, j * MXU_TILE:(j + 1) * MXU_TILE] = ys.astype(o_ref.dtype)


def _dnpre(xd, conv_w, hsum, norm_flags, q_flags, tm):
    b, t, c = xd.shape
    halo = 16
    nblk = t // halo
    r = tm // halo
    n_out = c // DN_WIDTH
    return pl.pallas_call(
        functools.partial(_dnpre_kernel, tm=tm, halo=halo, norm_flags=norm_flags, q_flags=q_flags),
        grid=(b, t // tm),
        in_specs=[pl.BlockSpec((1, tm, c), lambda bi, i: (bi, i, 0)),
                  pl.BlockSpec((1, halo, c), lambda bi, i: (bi, jnp.maximum(i * r - 1, 0), 0)),
                  pl.BlockSpec((1, halo, c), lambda bi, i: (bi, jnp.minimum((i + 1) * r, nblk - 1), 0)),
                  pl.BlockSpec((CONV_W, c), lambda bi, i: (0, 0)),
                  pl.BlockSpec((MXU_TILE, MXU_TILE), lambda bi, i: (0, 0))],
        out_specs=[pl.BlockSpec((1, tm, DN_WIDTH), lambda bi, i: (bi, i, 0))] * n_out,
        out_shape=[jax.ShapeDtypeStruct((b, t, DN_WIDTH), BF16)] * n_out,
        compiler_params=_cparams(2),
        name="dnpre",
    )(xd, xd, xd, conv_w, hsum)


def _rope(x, cos, sin, lane):
    swapped = jnp.where((lane % 32) < 16, pltpu.roll(x, LANES - 16, 1), pltpu.roll(x, 16, 1))
    return x * cos + swapped * sin


def _attn_kernel(a_ref, kvc_ref, cos_ref, sin_ref, qg_ref, kg_ref, sink_ref, havg_ref, tile_ref,
                 o_ref, k4_s, v4_s, *, seq, ctx):
    n = pl.program_id(1)
    havg = havg_ref[...]
    tile = tile_ref[...]
    prep_rows = 256

    @pl.when(n == 0)
    def _prep():
        kg = kg_ref[...]
        kc = kvc_ref[0, :, 0:KV_WIDTH].astype(F32)
        ms = _dot((kc * kc).astype(BF16), havg)
        kc = kc * lax.rsqrt(ms + EPS) * kg
        k4_s[0:ctx, :] = _dot(kc.astype(BF16), tile).astype(BF16)
        v4_s[0:ctx, :] = _dot(kvc_ref[0, :, KV_WIDTH:2 * KV_WIDTH], tile).astype(BF16)
        lane = lax.broadcasted_iota(jnp.int32, (prep_rows, LANES), 1)
        for r0 in range(0, seq, prep_rows):
            kx = a_ref[0, r0:r0 + prep_rows, _OFF_AK:_OFF_AK + KV_WIDTH].astype(F32)
            ms = _dot((kx * kx).astype(BF16), havg)
            kx = kx * lax.rsqrt(ms + EPS) * kg
            kx = _rope(kx, cos_ref[r0:r0 + prep_rows, :], sin_ref[r0:r0 + prep_rows, :], lane)
            k4_s[ctx + r0:ctx + r0 + prep_rows, :] = _dot(kx.astype(BF16), tile).astype(BF16)
            vx = a_ref[0, r0:r0 + prep_rows, _OFF_AV:_OFF_AV + KV_WIDTH]
            v4_s[ctx + r0:ctx + r0 + prep_rows, :] = _dot(vx, tile).astype(BF16)

    q0 = pl.multiple_of(n * ATTN_BLOCK, ATTN_BLOCK)
    lane = lax.broadcasted_iota(jnp.int32, (ATTN_BLOCK, LANES), 1)
    cos = cos_ref[pl.ds(q0, ATTN_BLOCK), :]
    sin = sin_ref[pl.ds(q0, ATTN_BLOCK), :]
    qg = qg_ref[...]
    q_parts = []
    for j in range(ATTN_WIDTH // LANES):
        qx = a_ref[0, pl.ds(q0, ATTN_BLOCK), j * LANES:(j + 1) * LANES].astype(F32)
        ms = _dot((qx * qx).astype(BF16), havg)
        qx = qx * lax.rsqrt(ms + EPS) * qg
        qx = _rope(qx, cos, sin, lane) * (HEAD_DIM ** -0.5)
        q_parts.append(qx.astype(BF16))

    n_win = 3 * ATTN_BLOCK
    start = pl.multiple_of(jnp.clip((n - 1) * ATTN_BLOCK, 0, seq - n_win), ATTN_BLOCK)
    qpos = q0 + lax.broadcasted_iota(jnp.int32, (ATTN_BLOCK, n_win), 0)
    kpos = start + lax.broadcasted_iota(jnp.int32, (ATTN_BLOCK, n_win), 1)
    valid = jnp.abs(qpos - kpos) <= WINDOW
    head_of_lane = lax.broadcasted_iota(jnp.int32, (1, MXU_TILE), 1) // HEAD_DIM

    outs = []
    for g in range(ATTN_KV_HEADS):
        gl = g * MXU_TILE
        qgrp = jnp.concatenate([q_parts[2 * g], q_parts[2 * g + 1]], axis=1)
        kwin = k4_s[pl.ds(ctx + start, n_win), gl:gl + MXU_TILE]
        vwin = v4_s[pl.ds(ctx + start, n_win), gl:gl + MXU_TILE]
        kctx = k4_s[0:ctx, gl:gl + MXU_TILE]
        vctx = v4_s[0:ctx, gl:gl + MXU_TILE]
        acc = jnp.zeros((ATTN_BLOCK, MXU_TILE), F32)
        for r in range(ATTN_GROUP):
            hmask = head_of_lane == r
            qm = jnp.where(hmask, qgrp, jnp.zeros_like(qgrp))
            s_loc = jnp.where(valid, _dot_nt(qm, kwin), NEG_INF)
            s_ctx = _dot_nt(qm, kctx)
            sink = sink_ref[g * ATTN_GROUP + r:g * ATTN_GROUP + r + 1, 0:1]
            m = jnp.maximum(jnp.maximum(jnp.max(s_loc, axis=-1, keepdims=True),
                                        jnp.max(s_ctx, axis=-1, keepdims=True)), sink)
            p_loc = jnp.exp(s_loc - m)
            p_ctx = jnp.exp(s_ctx - m)
            den = (jnp.sum(p_loc, axis=-1, keepdims=True) + jnp.sum(p_ctx, axis=-1, keepdims=True)
                   + jnp.exp(sink - m))
            o_full = _dot(p_loc.astype(BF16), vwin) + _dot(p_ctx.astype(BF16), vctx)
            acc = acc + jnp.where(hmask, o_full / den, 0.0)
        outs.append(acc)
    o_ref[0] = jnp.concatenate(outs, axis=1).astype(o_ref.dtype)


def _attention(a_lat, kv_ctx, cos, sin, qg, kg, sink, havg, tile):
    b, s, wa = a_lat.shape
    ctx = kv_ctx.shape[1]
    nb = s // ATTN_BLOCK
    const2 = lambda bi, n: (0, 0)
    return pl.pallas_call(
        functools.partial(_attn_kernel, seq=s, ctx=ctx),
        grid=(b, nb),
        in_specs=[pl.BlockSpec((1, s, wa), lambda bi, n: (bi, 0, 0)),
                  pl.BlockSpec((1, ctx, 2 * KV_WIDTH), lambda bi, n: (bi, 0, 0)),
                  pl.BlockSpec((s, LANES), const2),
                  pl.BlockSpec((s, LANES), const2),
                  pl.BlockSpec((1, LANES), const2),
                  pl.BlockSpec((1, LANES), const2),
                  pl.BlockSpec((ATTN_Q_HEADS, LANES), const2),
                  pl.BlockSpec((LANES, LANES), const2),
                  pl.BlockSpec((LANES, ATTN_KV_HEADS * MXU_TILE), const2)],
        out_specs=pl.BlockSpec((1, ATTN_BLOCK, ATTN_WIDTH), lambda bi, n: (bi, n, 0)),
        out_shape=jax.ShapeDtypeStruct((b, s, ATTN_WIDTH), BF16),
        scratch_shapes=[pltpu.VMEM((ctx + s, ATTN_KV_HEADS * MXU_TILE), BF16),
                        pltpu.VMEM((ctx + s, ATTN_KV_HEADS * MXU_TILE), BF16)],
        compiler_params=_cparams(2),
        name="attn",
    )(a_lat, kv_ctx, cos, sin, qg, kg, sink, havg, tile)


def _bdiag(x_lane, bmask):
    return jnp.where(bmask, jnp.concatenate([x_lane] * HEADS_PER_TILE, axis=0), jnp.zeros((), x_lane.dtype))


def _unit_tri_inverse(a, eye_l, bmask):
    p = eye_l - a
    pw = a.astype(BF16)
    n_levels = int(np.log2(CHUNK))
    for level in range(n_levels):
        rhs = _bdiag(pw, bmask)
        if level == 0:
            pw = _dot(pw, rhs).astype(BF16)
        elif level < n_levels - 1:
            both = _dot(jnp.concatenate([pw, p.astype(BF16)], axis=0), rhs)
            pw = both[0:CHUNK].astype(BF16)
            p = p + both[CHUNK:2 * CHUNK]
        else:
            p = p + _dot(p.astype(BF16), rhs)
    return p


def _dn_chain_step(d, want_o, q_l, k_l, v_l, be, gi, gj, eye_l, row, colj, bmask, s_ref, chain):
    lower = (row > colj) if d == 0 else (row < colj)
    last = CHUNK - 1 if d == 0 else 0
    dec = jnp.where(lower, jnp.exp(jnp.where(lower, gi - gj, 0.0)), 0.0)
    e_g = jnp.exp(gi)
    gl_row = gi[last:last + 1, :]
    kf = k_l.astype(F32)
    kbd = _bdiag(k_l, bmask)
    if want_o:
        kq = _dot_nt(jnp.concatenate([k_l, q_l], axis=0), kbd)
        kk = kq[0:CHUNK]
    else:
        kk = _dot_nt(k_l, kbd)
    tinv = _unit_tri_inverse(be * dec * kk, eye_l, bmask).astype(BF16)
    w = _dot(tinv, _bdiag((be * e_g * kf).astype(BF16), bmask))
    uv = _dot(tinv, _bdiag((be * v_l.astype(F32)).astype(BF16), bmask))
    s_old = s_ref[chain]
    s_bf = s_old.astype(BF16)
    if want_o:
        ws_qs = _dot(jnp.concatenate([w.astype(BF16), (e_g * q_l.astype(F32)).astype(BF16)], axis=0), s_bf)
        u = uv - ws_qs[0:CHUNK]
    else:
        u = uv - _dot(w.astype(BF16), s_bf)
    u_bf = u.astype(BF16)
    kt = (jnp.exp(gl_row - gi) * kf).astype(BF16)
    ds = jnp.where(bmask, _dot_tn(kt, u_bf), 0.0)
    s_ref[chain] = s_old * jnp.exp(gl_row) + ds
    if not want_o:
        return None
    intra = ((dec + eye_l) * kq[CHUNK:2 * CHUNK]).astype(BF16)
    return ws_qs[CHUNK:2 * CHUNK] + _dot(intra, _bdiag(u_bf, bmask))


def _dn_factors(ab, d, arow, dtrow, exp2_ref, tri_ref, eye_t):
    lane = lax.broadcasted_iota(jnp.int32, ab.shape, 1)
    z = ab + dtrow
    softplus = jnp.maximum(z, 0.0) + jnp.log(1.0 + jnp.exp(-jnp.abs(z)))
    is_g = (lane % 32) < 16
    x = jnp.where(is_g, -arow * softplus, _sigmoid(ab))
    x_hi = x.astype(BF16)
    x_lo = (x - x_hi.astype(F32)).astype(BF16)
    xhl = jnp.where(lane < 32, x_hi, x_lo)[:, 0:64]
    y = _dot(xhl, exp2_ref[d])
    ge = y[:, 0:DN_WIDTH]
    be = y[:, DN_WIDTH:2 * DN_WIDTH]
    ge_hi = ge.astype(BF16)
    ge_lo = (ge - ge_hi.astype(F32)).astype(BF16)
    gi = _dot(tri_ref[d], jnp.concatenate([ge_hi, ge_lo], axis=0))
    gj = jnp.sum(gi * eye_t, axis=0, keepdims=True)
    return be, gi, jnp.broadcast_to(gj, gi.shape)


def _dnscan_kernel(ql_ref, kl_ref, vl_ref, abl_ref, kc_ref, vc_ref, abc_ref, arow_ref, dtrow_ref,
                   exp2_ref, tri_ref, o_ref, s_ref, *, seq, ctx):
    n_tiles = DN_WIDTH // MXU_TILE
    s_ref[...] = jnp.zeros_like(s_ref)
    o_ref[...] = jnp.zeros_like(o_ref)
    row = lax.broadcasted_iota(jnp.int32, (CHUNK, MXU_TILE), 0)
    colj = lax.broadcasted_iota(jnp.int32, (CHUNK, MXU_TILE), 1) % HEAD_DIM
    eye_l = (row == colj).astype(F32)
    eye_t = jnp.concatenate([eye_l] * n_tiles, axis=1)
    bmask = (lax.broadcasted_iota(jnp.int32, (MXU_TILE, MXU_TILE), 0) // HEAD_DIM
             == lax.broadcasted_iota(jnp.int32, (MXU_TILE, MXU_TILE), 1) // HEAD_DIM)
    arow = arow_ref[...]
    dtrow = dtrow_ref[...]

    def step(t, n_chunks, refs, want_o):
        q_ref, k_ref, v_ref, ab_ref = refs
        for d in range(N_DIR):
            c = t if d == 0 else n_chunks - 1 - t
            r0 = pl.multiple_of(c * CHUNK, CHUNK)
            be, gi, gj = _dn_factors(ab_ref[0, pl.ds(r0, CHUNK), :], d, arow, dtrow, exp2_ref, tri_ref, eye_t)
            for g in range(n_tiles):
                gl = g * MXU_TILE
                q_l = q_ref[0, pl.ds(r0, CHUNK), gl:gl + MXU_TILE] if want_o else None
                k_l = k_ref[0, pl.ds(r0, CHUNK), gl:gl + MXU_TILE]
                v_l = v_ref[0, pl.ds(r0, CHUNK), gl:gl + MXU_TILE]
                o = _dn_chain_step(d, want_o, q_l, k_l, v_l, be[:, gl:gl + MXU_TILE], gi[:, gl:gl + MXU_TILE],
                                   gj[:, gl:gl + MXU_TILE], eye_l, row, colj, bmask, s_ref, d * n_tiles + g)
                if want_o:
                    o_ref[0, pl.ds(r0, CHUNK), gl:gl + MXU_TILE] += o

    def ctx_body(t, carry):
        step(t, ctx // CHUNK, (None, kc_ref, vc_ref, abc_ref), False)
        return carry

    def lat_body(t, carry):
        step(t, seq // CHUNK, (ql_ref, kl_ref, vl_ref, abl_ref), True)
        return carry

    lax.fori_loop(0, ctx // CHUNK, ctx_body, 0)
    lax.fori_loop(0, seq // CHUNK, lat_body, 0)


def _dnscan(q_l, k_l, v_l, ab_l, k_c, v_c, ab_c, arow, dtrow, exp2, tri):
    b, s, w = q_l.shape
    ctx = k_c.shape[1]
    const2 = lambda bi: (0, 0)
    per_b = lambda bi: (bi, 0, 0)
    return pl.pallas_call(
        functools.partial(_dnscan_kernel, seq=s, ctx=ctx),
        grid=(b,),
        in_specs=[pl.BlockSpec((1, s, w), per_b), pl.BlockSpec((1, s, w), per_b), pl.BlockSpec((1, s, w), per_b),
                  pl.BlockSpec((1, s, LANES), per_b),
                  pl.BlockSpec((1, ctx, w), per_b), pl.BlockSpec((1, ctx, w), per_b),
                  pl.BlockSpec((1, ctx, LANES), per_b),
                  pl.BlockSpec((1, LANES), const2), pl.BlockSpec((1, LANES), const2),
                  pl.BlockSpec(exp2.shape, lambda bi: (0, 0, 0)), pl.BlockSpec(tri.shape, lambda bi: (0, 0, 0))],
        out_specs=pl.BlockSpec((1, s, w), per_b),
        out_shape=jax.ShapeDtypeStruct((b, s, w), F32),
        scratch_shapes=[pltpu.VMEM((N_DIR * (w // MXU_TILE), MXU_TILE, MXU_TILE), F32)],
        compiler_params=_cparams(1),
        name="dnscan",
    )(q_l, k_l, v_l, ab_l, k_c, v_c, ab_c, arow, dtrow, exp2, tri)


def _merge_kernel(x_ref, mod_ref, ya_ref, od_ref, z_ref, gate_ref, dng_ref, havg_ref, wba_ref, wbd_ref, wo_ref,
                  gn2_ref, out1_ref, hm_ref):
    havg = havg_ref[...]
    dng = dng_ref[...]
    yd_parts = []
    for j in range(DN_WIDTH // MXU_TILE):
        sl = slice(j * MXU_TILE, (j + 1) * MXU_TILE)
        od = od_ref[0, :, sl]
        ms = _dot((od * od).astype(BF16), havg)
        z = z_ref[0, :, sl].astype(F32)
        yd_parts.append((od * lax.rsqrt(ms + EPS) * dng * (z * _sigmoid(z))).astype(BF16))
    yd = jnp.concatenate(yd_parts, axis=1)
    ga = gate_ref[0, :, 0:D_MODEL].astype(F32)
    gd = gate_ref[0, :, D_MODEL:2 * D_MODEL].astype(F32)
    y = _sigmoid(ga) * _dot(ya_ref[0], wba_ref[...]) + _sigmoid(gd) * _dot(yd, wbd_ref[...])
    br = _dot(y.astype(BF16), wo_ref[...])
    mod = mod_ref[0]
    out1 = x_ref[0] + mod[2:3] * br
    out1_ref[0] = out1
    ms2 = jnp.mean(out1 * out1, axis=-1, keepdims=True)
    hm = out1 * lax.rsqrt(ms2 + EPS) * (gn2_ref[...] * (1.0 + mod[4:5])) + mod[3:4]
    hm_ref[0] = hm.astype(BF16)


def _merge(x, mod3, y_attn, o_d, z, gates, dng, havg, wba, wbd, wo, gn2, tm):
    b, t, d = x.shape
    tok = lambda bi, i: (bi, i, 0)
    const2 = lambda bi, i: (0, 0)
    return pl.pallas_call(
        _merge_kernel,
        grid=(b, t // tm),
        in_specs=[pl.BlockSpec((1, tm, d), tok),
                  pl.BlockSpec((1, 6, d), lambda bi, i: (bi, 0, 0)),
                  pl.BlockSpec((1, tm, ATTN_WIDTH), tok),
                  pl.BlockSpec((1, tm, DN_WIDTH), tok),
                  pl.BlockSpec((1, tm, DN_WIDTH), tok),
                  pl.BlockSpec((1, tm, 2 * d), tok),
                  pl.BlockSpec((1, MXU_TILE), const2),
                  pl.BlockSpec((MXU_TILE, MXU_TILE), const2),
                  pl.BlockSpec(wba.shape, const2),
                  pl.BlockSpec(wbd.shape, const2),
                  pl.BlockSpec(wo.shape, const2),
                  pl.BlockSpec((1, d), const2)],
        out_specs=[pl.BlockSpec((1, tm, d), tok), pl.BlockSpec((1, tm, d), tok)],
        out_shape=[jax.ShapeDtypeStruct((b, t, d), F32), jax.ShapeDtypeStruct((b, t, d), BF16)],
        compiler_params=_cparams(2),
        name="merge",
    )(x, mod3, y_attn, o_d, z, gates, dng, havg, wba, wbd, wo, gn2)


def _mlp_kernel(out1_ref, hm_ref, mod_ref, w1_ref, w2_ref, o_ref, *, ff_chunk):
    hm = hm_ref[0]
    acc = None
    for j in range(D_FF // ff_chunk):
        a = jnp.maximum(_dot(hm, w1_ref[:, j * ff_chunk:(j + 1) * ff_chunk]), 0.0)
        part = _dot((a * a).astype(BF16), w2_ref[j * ff_chunk:(j + 1) * ff_chunk, :])
        acc = part if acc is None else acc + part
    o_ref[0] = out1_ref[0] + mod_ref[0][5:6] * acc


def _mlp(out1, hm, mod3, w1, w2, tm):
    b, t, d = out1.shape
    tok = lambda bi, i: (bi, i, 0)
    const2 = lambda bi, i: (0, 0)
    return pl.pallas_call(
        functools.partial(_mlp_kernel, ff_chunk=1024),
        grid=(b, t // tm),
        in_specs=[pl.BlockSpec((1, tm, d), tok),
                  pl.BlockSpec((1, tm, d), tok),
                  pl.BlockSpec((1, 6, d), lambda bi, i: (bi, 0, 0)),
                  pl.BlockSpec(w1.shape, const2),
                  pl.BlockSpec(w2.shape, const2)],
        out_specs=pl.BlockSpec((1, tm, d), tok),
        out_shape=jax.ShapeDtypeStruct((b, t, d), F32),
        compiler_params=_cparams(2),
        name="mlp",
    )(out1, hm, mod3, w1, w2)


def _head_avg(n, scale):
    idx = np.arange(n) // HEAD_DIM
    return jnp.asarray((idx[:, None] == idx[None, :]).astype(np.float32) * scale, BF16)


def _kv_tile_matrix():
    m = np.zeros((KV_WIDTH, ATTN_KV_HEADS * MXU_TILE), np.float32)
    for g in range(ATTN_KV_HEADS):
        for r in range(ATTN_GROUP):
            for dd in range(HEAD_DIM):
                m[g * HEAD_DIM + dd, g * MXU_TILE + r * HEAD_DIM + dd] = 1.0
    return jnp.asarray(m, BF16)


def _dn_expand_matrix():
    n = N_DIR * DN_HEADS
    m = np.zeros((N_DIR, 4 * n, 2 * DN_WIDTH), np.float32)
    for d in range(N_DIR):
        for part in range(2):
            for h in range(DN_HEADS):
                idx = d * DN_HEADS + h
                m[d, part * 2 * n + idx, h * HEAD_DIM:(h + 1) * HEAD_DIM] = 1.0
                m[d, part * 2 * n + n + idx, DN_WIDTH + h * HEAD_DIM:DN_WIDTH + (h + 1) * HEAD_DIM] = 1.0
    return jnp.asarray(m, BF16)


def _tri_matrices():
    i = np.arange(CHUNK)
    low = (i[:, None] >= i[None, :]).astype(np.float32)
    up = (i[:, None] <= i[None, :]).astype(np.float32)
    return jnp.asarray(np.stack([np.concatenate([low, low], axis=1), np.concatenate([up, up], axis=1)]), BF16)


def _rope_tables(seq):
    half = HEAD_DIM // 2
    n_freq = half // 2
    freqs = ROPE_BASE ** (-jnp.arange(n_freq, dtype=F32) / n_freq)
    pos = jnp.arange(seq)
    ang_r = (pos // GRID_W).astype(F32)[:, None] * freqs
    ang_c = (pos % GRID_W).astype(F32)[:, None] * freqs
    cos = jnp.concatenate([jnp.cos(ang_r)] * 2 + [jnp.cos(ang_c)] * 2, axis=1)
    sin = jnp.concatenate([-jnp.sin(ang_r), jnp.sin(ang_r), -jnp.sin(ang_c), jnp.sin(ang_c)], axis=1)
    reps = LANES // HEAD_DIM
    return jnp.tile(cos, (1, reps)), jnp.tile(sin, (1, reps))


def _pad_cols(w, n):
    return jnp.pad(w, ((0, 0), (0, n - w.shape[1])))


def kernel(x, c, ctx, c_ctx, w_ada, b_ada, g_norm1, w_in, q_norm_g, k_norm_g, attn_sink, conv_w, a_log, dt_bias,
           dn_norm_g, w_br_attn, w_br_dn, w_out, g_norm2, w_mlp1, w_mlp2):
    depth = w_ada.shape[0]
    assert depth == 1, "single-layer trunk only"
    b, s, d = x.shape
    n_ctx = ctx.shape[1]
    assert d == D_MODEL and w_in.shape[-1] == _IN_WIDTH
    assert s >= 3 * ATTN_BLOCK and s % ATTN_BLOCK == 0 and s % CHUNK == 0 and n_ctx % CHUNK == 0
    out_dtype = x.dtype
    w_in0 = w_in[0]

    mod_rows = 16
    cc = jnp.concatenate([c.astype(F32), c_ctx.astype(F32)[None], jnp.zeros((mod_rows - b - 1, d), F32)], axis=0)
    mod = _ada(cc, w_ada[0], b_ada[0])
    mod3 = mod.reshape(mod_rows, 6, d)

    ab_cols = jnp.concatenate([w_in0[:, _OFF_DA:_OFF_GA]] * 2, axis=1)
    w_lat = jnp.concatenate([w_in0[:, :_OFF_DA], w_in0[:, _OFF_GA:], _pad_cols(ab_cols, LANES)], axis=1).astype(BF16)
    segs_lat = ((0, _OFF_DQ), (_OFF_DQ, 3 * DN_WIDTH), (_OFF_DZ, DN_WIDTH), (_OFF_DA, 2 * D_MODEL),
                (_OFF_DA + 2 * D_MODEL, LANES))
    a_lat, d_lat, z_lat, gates, ab_lat = _inproj(x, mod3, None, g_norm1[0], w_lat, segs_lat,
                                                 (BF16, BF16, BF16, BF16, F32), tm=512)
    w_ctx = jnp.concatenate([w_in0[:, _OFF_AK:_OFF_DQ], w_in0[:, _OFF_DK:_OFF_DZ], _pad_cols(ab_cols, LANES)],
                            axis=1).astype(BF16)
    segs_ctx = ((0, 2 * KV_WIDTH), (2 * KV_WIDTH, 2 * DN_WIDTH), (2 * KV_WIDTH + 2 * DN_WIDTH, LANES))
    kv_ctx, d_ctx, ab_ctx = _inproj(ctx, mod3, b, g_norm1[0], w_ctx, segs_ctx, (BF16, BF16, F32), tm=n_ctx)

    hsum = _head_avg(MXU_TILE, 1.0)
    q_d, k_d, v_d = _dnpre(d_lat, conv_w[0], hsum, (True, True, False), (True, False, False), tm=512)
    k_dc, v_dc = _dnpre(d_ctx, conv_w[0][:, DN_WIDTH:], hsum, (True, False), (False, False), tm=n_ctx)

    cos, sin = _rope_tables(s)
    reps = LANES // HEAD_DIM
    y_attn = _attention(a_lat, kv_ctx, cos, sin,
                        jnp.tile(q_norm_g[0].astype(F32), reps)[None], jnp.tile(k_norm_g[0].astype(F32), reps)[None],
                        jnp.broadcast_to(attn_sink[0].astype(F32)[:, None], (ATTN_Q_HEADS, LANES)),
                        _head_avg(LANES, 1.0 / HEAD_DIM), _kv_tile_matrix())

    n_gate = N_DIR * DN_HEADS
    arow = _pad_cols(jnp.tile(jnp.concatenate([jnp.exp(a_log[0]).reshape(1, n_gate), jnp.zeros((1, n_gate), F32)],
                                              axis=1), (1, 2)), LANES)
    dtrow = _pad_cols(jnp.tile(jnp.concatenate([dt_bias[0].reshape(1, n_gate), jnp.zeros((1, n_gate), F32)],
                                               axis=1), (1, 2)), LANES)
    o_d = _dnscan(q_d, k_d, v_d, ab_lat, k_dc, v_dc, ab_ctx, arow, dtrow, _dn_expand_matrix(), _tri_matrices())

    out1, hm = _merge(x, mod3, y_attn, o_d, z_lat, gates,
                      jnp.tile(dn_norm_g[0].astype(F32), HEADS_PER_TILE)[None], _head_avg(MXU_TILE, 1.0 / HEAD_DIM),
                      w_br_attn[0].astype(BF16), w_br_dn[0].astype(BF16), w_out[0].astype(BF16),
                      g_norm2[0].reshape(1, d), tm=256)
    out = _mlp(out1, hm, mod3, w_mlp1[0].astype(BF16), w_mlp2[0].astype(BF16), tm=256)
    return out.astype(out_dtype)
```

```python
import functools

import numpy as np
import jax
import jax.numpy as jnp
from jax import lax
from jax.experimental import pallas as pl
from jax.experimental.pallas import tpu as pltpu

F32 = jnp.float32
BF16 = jnp.bfloat16

D_MODEL = 1024
GRID_W = 64
HEAD_DIM = 64
ATTN_Q_HEADS = 8
ATTN_KV_HEADS = 2
ATTN_GROUP = ATTN_Q_HEADS // ATTN_KV_HEADS
WINDOW = 128
ATTN_BLOCK = 128
ROPE_BASE = 10000.0
DN_HEADS = 8
CONV_W = 3
CHUNK = 64
N_DIR = 2
D_FF = 4 * D_MODEL
EPS = 1e-6
NEG_INF = -1e30

ATTN_WIDTH = ATTN_Q_HEADS * HEAD_DIM
KV_WIDTH = ATTN_KV_HEADS * HEAD_DIM
DN_WIDTH = DN_HEADS * HEAD_DIM
LANES = 128
MXU_TILE = 256
HEADS_PER_TILE = MXU_TILE // HEAD_DIM
DN_SUB = 16
VMEM_LIMIT = 56 * 1024 * 1024

_OFF_AQ = 0
_OFF_AK = _OFF_AQ + ATTN_WIDTH
_OFF_AV = _OFF_AK + KV_WIDTH
_OFF_DQ = _OFF_AV + KV_WIDTH
_OFF_DK = _OFF_DQ + DN_WIDTH
_OFF_DV = _OFF_DK + DN_WIDTH
_OFF_DZ = _OFF_DV + DN_WIDTH
_OFF_DA = _OFF_DZ + DN_WIDTH
_OFF_DB = _OFF_DA + N_DIR * DN_HEADS
_OFF_GA = _OFF_DB + N_DIR * DN_HEADS
_OFF_GD = _OFF_GA + D_MODEL
_IN_WIDTH = _OFF_GD + D_MODEL


def _sigmoid(x):
    return 1.0 / (1.0 + jnp.exp(-x))


def _dot(a, b):
    return jnp.dot(a, b, preferred_element_type=F32)


def _dot_nt(a, b):
    return lax.dot_general(a, b, (((1,), (1,)), ((), ())), preferred_element_type=F32)


def _dot_tn(a, b):
    return lax.dot_general(a, b, (((0,), (0,)), ((), ())), preferred_element_type=F32)


def _cparams(n_axes):
    return pltpu.CompilerParams(dimension_semantics=("arbitrary",) * n_axes, vmem_limit_bytes=VMEM_LIMIT)


def _ada_kernel(c_ref, w_ref, b_ref, o_ref):
    c = c_ref[...]
    s = c * _sigmoid(c)
    o_ref[...] = _dot(s.astype(BF16), w_ref[...].astype(BF16)) + b_ref[...]


def _ada(cc, w_ada, b_ada):
    rows, d = cc.shape
    n = w_ada.shape[1]
    tn = 1536
    return pl.pallas_call(
        _ada_kernel,
        grid=(n // tn,),
        in_specs=[pl.BlockSpec((rows, d), lambda j: (0, 0)),
                  pl.BlockSpec((d, tn), lambda j: (0, j)),
                  pl.BlockSpec((1, tn), lambda j: (0, j))],
        out_specs=pl.BlockSpec((rows, tn), lambda j: (0, j)),
        out_shape=jax.ShapeDtypeStruct((rows, n), F32),
        compiler_params=_cparams(1),
        name="ada",
    )(cc, w_ada, b_ada.reshape(1, n))


def _inproj_kernel(x_ref, mod_ref, g_ref, w_ref, *out_refs, segs):
    x = x_ref[0]
    ms = jnp.mean(x * x, axis=-1, keepdims=True)
    mod = mod_ref[0]
    scale = g_ref[...] * (1.0 + mod[1:2])
    h = (x * lax.rsqrt(ms + EPS) * scale + mod[0:1]).astype(BF16)
    for o_ref, (start, size) in zip(out_refs, segs):
        o_ref[0] = _dot(h, w_ref[:, start:start + size]).astype(o_ref.dtype)


def _inproj(x, mod3, mod_row, g_norm, w, segs, dtypes, tm):
    b, t, d = x.shape
    n = w.shape[1]
    if mod_row is None:
        mod_map = lambda bi, i: (bi, 0, 0)
    else:
        mod_map = lambda bi, i: (mod_row, 0, 0)
    out_shape = [jax.ShapeDtypeStruct((b, t, size), dt) for (_, size), dt in zip(segs, dtypes)]
    out_specs = [pl.BlockSpec((1, tm, size), lambda bi, i: (bi, i, 0)) for (_, size) in segs]
    return pl.pallas_call(
        functools.partial(_inproj_kernel, segs=segs),
        grid=(b, t // tm),
        in_specs=[pl.BlockSpec((1, tm, d), lambda bi, i: (bi, i, 0)),
                  pl.BlockSpec((1, 6, d), mod_map),
                  pl.BlockSpec((1, d), lambda bi, i: (0, 0)),
                  pl.BlockSpec((d, n), lambda bi, i: (0, 0))],
        out_specs=out_specs,
        out_shape=out_shape,
        compiler_params=_cparams(2),
        name="inproj",
    )(x, mod3, g_norm.reshape(1, d), w)


def _dnpre_kernel(x_ref, prev_ref, next_ref, cw_ref, hsum_ref, *out_refs, tm, halo, norm_flags, q_flags):
    i = pl.program_id(1)
    last = pl.num_programs(1) - 1
    x = x_ref[0].astype(F32)
    prev_row = prev_ref[0][halo - 1:halo, :].astype(F32) * (i > 0).astype(F32)
    next_row = next_ref[0][0:1, :].astype(F32) * (i < last).astype(F32)
    rows = lax.broadcasted_iota(jnp.int32, x.shape, 0)
    xm = jnp.where(rows == 0, prev_row, pltpu.roll(x, 1, 0))
    xp = jnp.where(rows == tm - 1, next_row, pltpu.roll(x, tm - 1, 0))
    cw = cw_ref[...]
    y = xm * cw[0:1] + x * cw[1:2] + xp * cw[2:3]
    y = y * _sigmoid(y)
    hsum = hsum_ref[...]
    for s, (o_ref, do_norm, is_q) in enumerate(zip(out_refs, norm_flags, q_flags)):
        for j in range(DN_WIDTH // MXU_TILE):
            lo = s * DN_WIDTH + j * MXU_TILE
            ys = y[:, lo:lo + MXU_TILE]
            if do_norm:
                ss = _dot((ys * ys).astype(BF16), hsum)
                ys = ys * lax.rsqrt(ss + EPS)
                if is_q:
                    ys = ys * (HEAD_DIM ** -0.5)
            o_ref[0, :, j * MXU_TILE:(j + 1) * MXU_TILE] = ys.astype(o_ref.dtype)


def _dnpre(xd, conv_w, hsum, norm_flags, q_flags, tm):
    b, t, c = xd.shape
    halo = 16
    nblk = t // halo
    r = tm // halo
    n_out = c // DN_WIDTH
    return pl.pallas_call(
        functools.partial(_dnpre_kernel, tm=tm, halo=halo, norm_flags=norm_flags, q_flags=q_flags),
        grid=(b, t // tm),
        in_specs=[pl.BlockSpec((1, tm, c), lambda bi, i: (bi, i, 0)),
                  pl.BlockSpec((1, halo, c), lambda bi, i: (bi, jnp.maximum(i * r - 1, 0), 0)),
                  pl.BlockSpec((1, halo, c), lambda bi, i: (bi, jnp.minimum((i + 1) * r, nblk - 1), 0)),
                  pl.BlockSpec((CONV_W, c), lambda bi, i: (0, 0)),
                  pl.BlockSpec((MXU_TILE, MXU_TILE), lambda bi, i: (0, 0))],
        out_specs=[pl.BlockSpec((1, tm, DN_WIDTH), lambda bi, i: (bi, i, 0))] * n_out,
        out_shape=[jax.ShapeDtypeStruct((b, t, DN_WIDTH), BF16)] * n_out,
        compiler_params=_cparams(2),
        name="dnpre",
    )(xd, xd, xd, conv_w, hsum)


def _rope(x, cos, sin, lane):
    swapped = jnp.where((lane % 32) < 16, pltpu.roll(x, LANES - 16, 1), pltpu.roll(x, 16, 1))
    return x * cos + swapped * sin


def _attn_kernel(a_ref, kvc_ref, cos_ref, sin_ref, qg_ref, kg_ref, sink_ref, havg_ref, tile_ref,
                 o_ref, k4_s, v4_s, *, seq, ctx):
    n = pl.program_id(1)
    havg = havg_ref[...]
    tile = tile_ref[...]
    prep_rows = 256

    @pl.when(n == 0)
    def _prep():
        kg = kg_ref[...]
        kc = kvc_ref[0, :, 0:KV_WIDTH].astype(F32)
        ms = _dot((kc * kc).astype(BF16), havg)
        kc = kc * lax.rsqrt(ms + EPS) * kg
        k4_s[0:ctx, :] = _dot(kc.astype(BF16), tile).astype(BF16)
        v4_s[0:ctx, :] = _dot(kvc_ref[0, :, KV_WIDTH:2 * KV_WIDTH], tile).astype(BF16)
        lane = lax.broadcasted_iota(jnp.int32, (prep_rows, LANES), 1)
        for r0 in range(0, seq, prep_rows):
            kx = a_ref[0, r0:r0 + prep_rows, _OFF_AK:_OFF_AK + KV_WIDTH].astype(F32)
            ms = _dot((kx * kx).astype(BF16), havg)
            kx = kx * lax.rsqrt(ms + EPS) * kg
            kx = _rope(kx, cos_ref[r0:r0 + prep_rows, :], sin_ref[r0:r0 + prep_rows, :], lane)
            k4_s[ctx + r0:ctx + r0 + prep_rows, :] = _dot(kx.astype(BF16), tile).astype(BF16)
            vx = a_ref[0, r0:r0 + prep_rows, _OFF_AV:_OFF_AV + KV_WIDTH]
            v4_s[ctx + r0:ctx + r0 + prep_rows, :] = _dot(vx, tile).astype(BF16)

    q0 = pl.multiple_of(n * ATTN_BLOCK, ATTN_BLOCK)
    lane = lax.broadcasted_iota(jnp.int32, (ATTN_BLOCK, LANES), 1)
    cos = cos_ref[pl.ds(q0, ATTN_BLOCK), :]
    sin = sin_ref[pl.ds(q0, ATTN_BLOCK), :]
    qg = qg_ref[...]
    q_parts = []
    for j in range(ATTN_WIDTH // LANES):
        qx = a_ref[0, pl.ds(q0, ATTN_BLOCK), j * LANES:(j + 1) * LANES].astype(F32)
        ms = _dot((qx * qx).astype(BF16), havg)
        qx = qx * lax.rsqrt(ms + EPS) * qg
        qx = _rope(qx, cos, sin, lane) * (HEAD_DIM ** -0.5)
        q_parts.append(qx.astype(BF16))

    n_win = 3 * ATTN_BLOCK
    start = pl.multiple_of(jnp.clip((n - 1) * ATTN_BLOCK, 0, seq - n_win), ATTN_BLOCK)
    qpos = q0 + lax.broadcasted_iota(jnp.int32, (ATTN_BLOCK, n_win), 0)
    kpos = start + lax.broadcasted_iota(jnp.int32, (ATTN_BLOCK, n_win), 1)
    valid = jnp.abs(qpos - kpos) <= WINDOW
    head_of_lane = lax.broadcasted_iota(jnp.int32, (1, MXU_TILE), 1) // HEAD_DIM

    outs = []
    for g in range(ATTN_KV_HEADS):
        gl = g * MXU_TILE
        qgrp = jnp.concatenate([q_parts[2 * g], q_parts[2 * g + 1]], axis=1)
        kwin = k4_s[pl.ds(ctx + start, n_win), gl:gl + MXU_TILE]
        vwin = v4_s[pl.ds(ctx + start, n_win), gl:gl + MXU_TILE]
        kctx = k4_s[0:ctx, gl:gl + MXU_TILE]
        vctx = v4_s[0:ctx, gl:gl + MXU_TILE]
        acc = jnp.zeros((ATTN_BLOCK, MXU_TILE), F32)
        for r in range(ATTN_GROUP):
            hmask = head_of_lane == r
            qm = jnp.where(hmask, qgrp, jnp.zeros_like(qgrp))
            s_loc = jnp.where(valid, _dot_nt(qm, kwin), NEG_INF)
            s_ctx = _dot_nt(qm, kctx)
            sink = sink_ref[g * ATTN_GROUP + r:g * ATTN_GROUP + r + 1, 0:1]
            m = jnp.maximum(jnp.maximum(jnp.max(s_loc, axis=-1, keepdims=True),
                                        jnp.max(s_ctx, axis=-1, keepdims=True)), sink)
            p_loc = jnp.exp(s_loc - m)
            p_ctx = jnp.exp(s_ctx - m)
            den = (jnp.sum(p_loc, axis=-1, keepdims=True) + jnp.sum(p_ctx, axis=-1, keepdims=True)
                   + jnp.exp(sink - m))
            o_full = _dot(p_loc.astype(BF16), vwin) + _dot(p_ctx.astype(BF16), vctx)
            acc = acc + jnp.where(hmask, o_full / den, 0.0)
        outs.append(acc)
    o_ref[0] = jnp.concatenate(outs, axis=1).astype(o_ref.dtype)


def _attention(a_lat, kv_ctx, cos, sin, qg, kg, sink, havg, tile):
    b, s, wa = a_lat.shape
    ctx = kv_ctx.shape[1]
    nb = s // ATTN_BLOCK
    const2 = lambda bi, n: (0, 0)
    return pl.pallas_call(
        functools.partial(_attn_kernel, seq=s, ctx=ctx),
        grid=(b, nb),
        in_specs=[pl.BlockSpec((1, s, wa), lambda bi, n: (bi, 0, 0)),
                  pl.BlockSpec((1, ctx, 2 * KV_WIDTH), lambda bi, n: (bi, 0, 0)),
                  pl.BlockSpec((s, LANES), const2),
                  pl.BlockSpec((s, LANES), const2),
                  pl.BlockSpec((1, LANES), const2),
                  pl.BlockSpec((1, LANES), const2),
                  pl.BlockSpec((ATTN_Q_HEADS, LANES), const2),
                  pl.BlockSpec((LANES, LANES), const2),
                  pl.BlockSpec((LANES, ATTN_KV_HEADS * MXU_TILE), const2)],
        out_specs=pl.BlockSpec((1, ATTN_BLOCK, ATTN_WIDTH), lambda bi, n: (bi, n, 0)),
        out_shape=jax.ShapeDtypeStruct((b, s, ATTN_WIDTH), BF16),
        scratch_shapes=[pltpu.VMEM((ctx + s, ATTN_KV_HEADS * MXU_TILE), BF16),
                        pltpu.VMEM((ctx + s, ATTN_KV_HEADS * MXU_TILE), BF16)],
        compiler_params=_cparams(2),
        name="attn",
    )(a_lat, kv_ctx, cos, sin, qg, kg, sink, havg, tile)


def _bdiag(x_lane, bmask):
    return jnp.where(bmask, jnp.concatenate([x_lane] * HEADS_PER_TILE, axis=0), jnp.zeros((), x_lane.dtype))


def _unit_tri_inverses(a_list, eye_l, sub_mask, bmask):
    ads = [jnp.where(sub_mask, a, 0.0) for a in a_list]
    aos = [jnp.where(sub_mask, 0.0, a).astype(BF16) for a in a_list]
    ps = [eye_l - ad for ad in ads]
    pws = [ad.astype(BF16) for ad in ads]
    n_levels = int(np.log2(DN_SUB))
    for level in range(n_levels):
        rhss = [_bdiag(pw, bmask) for pw in pws]
        if level == 0:
            pws = [_dot(pw, rhs).astype(BF16) for pw, rhs in zip(pws, rhss)]
        elif level < n_levels - 1:
            boths = [_dot(jnp.concatenate([pw, p.astype(BF16)], axis=0), rhs) for pw, p, rhs in zip(pws, ps, rhss)]
            pws = [both[0:CHUNK].astype(BF16) for both in boths]
            ps = [p + both[CHUNK:2 * CHUNK] for p, both in zip(ps, boths)]
        else:
            ps = [p + _dot(p.astype(BF16), rhs) for p, rhs in zip(ps, rhss)]
    xs = [p.astype(BF16) for p in ps]
    bs = [_dot(x, _bdiag(ao, bmask)) for x, ao in zip(xs, aos)]
    b_rhss = [_bdiag(b.astype(BF16), bmask) for b in bs]
    b2s = [_dot(b.astype(BF16), rhs) for b, rhs in zip(bs, b_rhss)]
    b3s = [_dot(b2.astype(BF16), rhs) for b2, rhs in zip(b2s, b_rhss)]
    assert CHUNK // DN_SUB == 4
    qs = [(eye_l - b + b2 - b3).astype(BF16) for b, b2, b3 in zip(bs, b2s, b3s)]
    return [_dot(q, _bdiag(x, bmask)) for q, x in zip(qs, xs)]


def _dn_factors(ab, d, arow, dtrow, exp2_ref, tri_ref, eye_t):
    lane = lax.broadcasted_iota(jnp.int32, ab.shape, 1)
    z = ab + dtrow
    softplus = jnp.maximum(z, 0.0) + jnp.log(1.0 + jnp.exp(-jnp.abs(z)))
    is_g = (lane % 32) < 16
    x = jnp.where(is_g, -arow * softplus, _sigmoid(ab))
    x_hi = x.astype(BF16)
    x_lo = (x - x_hi.astype(F32)).astype(BF16)
    xhl = jnp.where(lane < 32, x_hi, x_lo)[:, 0:64]
    y = _dot(xhl, exp2_ref[d])
    ge = y[:, 0:DN_WIDTH]
    be = y[:, DN_WIDTH:2 * DN_WIDTH]
    ge_hi = ge.astype(BF16)
    ge_lo = (ge - ge_hi.astype(F32)).astype(BF16)
    gi = _dot(tri_ref[d], jnp.concatenate([ge_hi, ge_lo], axis=0))
    gj = jnp.sum(gi * eye_t, axis=0, keepdims=True)
    return be, gi, jnp.broadcast_to(gj, gi.shape)


def _block_mask():
    return (lax.broadcasted_iota(jnp.int32, (MXU_TILE, MXU_TILE), 0) // HEAD_DIM
            == lax.broadcasted_iota(jnp.int32, (MXU_TILE, MXU_TILE), 1) // HEAD_DIM)


def _dnprep_kernel(*refs, cb, want_o):
    if want_o:
        q_ref, k_ref, v_ref, ab_ref = refs[:4]
        refs = refs[4:]
    else:
        q_ref = None
        k_ref, v_ref, ab_ref = refs[:3]
        refs = refs[3:]
    arow_ref, dtrow_ref, exp2_ref, tri_ref, w_o, uv_o, kt_o, dl_o = refs[:8]
    qd_o, in_o = refs[8:10] if want_o else (None, None)
    n_tiles = DN_WIDTH // MXU_TILE
    row = lax.broadcasted_iota(jnp.int32, (CHUNK, MXU_TILE), 0)
    colj = lax.broadcasted_iota(jnp.int32, (CHUNK, MXU_TILE), 1) % HEAD_DIM
    eye_l = (row == colj).astype(F32)
    eye_t = jnp.concatenate([eye_l] * n_tiles, axis=1)
    bmask = _block_mask()
    arow = arow_ref[...]
    dtrow = dtrow_ref[...]

    fac = {}
    shared = {}
    for c in range(cb):
        rows = slice(c * CHUNK, (c + 1) * CHUNK)
        ab = ab_ref[0, rows, :]
        for d in range(N_DIR):
            fac[(c, d)] = _dn_factors(ab, d, arow, dtrow, exp2_ref, tri_ref, eye_t)
        for g in range(n_tiles):
            lanes = slice(g * MXU_TILE, (g + 1) * MXU_TILE)
            k_l = k_ref[0, rows, lanes]
            kbd = _bdiag(k_l, bmask)
            if want_o:
                kq = _dot_nt(jnp.concatenate([k_l, q_ref[0, rows, lanes]], axis=0), kbd)
                shared[(c, g)] = (kq[0:CHUNK], kq[CHUNK:2 * CHUNK])
            else:
                shared[(c, g)] = (_dot_nt(k_l, kbd), None)

    units = [(c, g, d) for c in range(cb) for g in range(n_tiles) for d in range(N_DIR)]
    decs = []
    for c, g, d in units:
        lanes = slice(g * MXU_TILE, (g + 1) * MXU_TILE)
        be, gi, gj = fac[(c, d)]
        lower = (row > colj) if d == 0 else (row < colj)
        decs.append(jnp.where(lower, jnp.exp(jnp.where(lower, gi[:, lanes] - gj[:, lanes], 0.0)), 0.0))
    a_list = [fac[(c, d)][0][:, g * MXU_TILE:(g + 1) * MXU_TILE] * dec * shared[(c, g)][0]
              for (c, g, d), dec in zip(units, decs)]
    tinvs = _unit_tri_inverses(a_list, eye_l, (row // DN_SUB) == (colj // DN_SUB), bmask)

    for (c, g, d), dec, tinv in zip(units, decs, tinvs):
        rows = slice(c * CHUNK, (c + 1) * CHUNK)
        lanes = slice(g * MXU_TILE, (g + 1) * MXU_TILE)
        be, gi, _ = fac[(c, d)]
        be, gi = be[:, lanes], gi[:, lanes]
        last = CHUNK - 1 if d == 0 else 0
        e_g = jnp.exp(gi)
        gl_row = gi[last:last + 1, :]
        kf = k_ref[0, rows, lanes].astype(F32)
        vf = v_ref[0, rows, lanes].astype(F32)
        rhs = jnp.concatenate([_bdiag((be * e_g * kf).astype(BF16), bmask), _bdiag((be * vf).astype(BF16), bmask)],
                              axis=1)
        wu = _dot(tinv.astype(BF16), rhs)
        w_o[0, d, rows, lanes] = wu[:, 0:MXU_TILE].astype(BF16)
        uv_o[0, d, rows, lanes] = wu[:, MXU_TILE:2 * MXU_TILE]
        kt_o[0, d, rows, lanes] = (jnp.exp(gl_row - gi) * kf).astype(BF16)
        dl_o[0, d, c, :, lanes] = jnp.exp(gl_row)
        if want_o:
            qd_o[0, d, rows, lanes] = (e_g * q_ref[0, rows, lanes].astype(F32)).astype(BF16)
            in_o[0, d, rows, lanes] = ((dec + eye_l) * shared[(c, g)][1]).astype(BF16)


def _dnprep(q, k, v, ab, arow, dtrow, exp2, tri, cb):
    want_o = q is not None
    b, t, w = k.shape
    tb = cb * CHUNK
    tok = lambda bi, i: (bi, i, 0)
    const2 = lambda bi, i: (0, 0)
    const3 = lambda bi, i: (0, 0, 0)
    dir_tok = lambda bi, i: (bi, 0, i, 0)
    data = ([q] if want_o else []) + [k, v]
    in_specs = ([pl.BlockSpec((1, tb, w), tok)] * len(data) + [pl.BlockSpec((1, tb, LANES), tok),
                pl.BlockSpec((1, LANES), const2), pl.BlockSpec((1, LANES), const2),
                pl.BlockSpec(exp2.shape, const3), pl.BlockSpec(tri.shape, const3)])
    big = lambda dt: jax.ShapeDtypeStruct((b, N_DIR, t, w), dt)
    big_spec = pl.BlockSpec((1, N_DIR, tb, w), dir_tok)
    out_shape = [big(BF16), big(F32), big(BF16), jax.ShapeDtypeStruct((b, N_DIR, t // CHUNK, 1, w), F32)]
    out_specs = [big_spec, big_spec, big_spec, pl.BlockSpec((1, N_DIR, cb, 1, w), lambda bi, i: (bi, 0, i, 0, 0))]
    if want_o:
        out_shape += [big(BF16), big(BF16)]
        out_specs += [big_spec, big_spec]
    return pl.pallas_call(
        functools.partial(_dnprep_kernel, cb=cb, want_o=want_o),
        grid=(b, t // tb),
        in_specs=in_specs,
        out_specs=out_specs,
        out_shape=out_shape,
        compiler_params=_cparams(2),
        name="dnprep",
    )(*data, ab, arow, dtrow, exp2, tri)


def _dnscan_kernel(*refs, n_chunk, want_o, have_s0, want_s):
    n_in = 6 if want_o else 4
    dir_refs = [refs[0:n_in], refs[n_in:2 * n_in]]
    pos = 2 * n_in
    s0_ref = refs[pos] if have_s0 else None
    pos += int(have_s0)
    o_refs = refs[pos:pos + N_DIR] if want_o else None
    pos += N_DIR if want_o else 0
    sout_ref = refs[pos] if want_s else None
    pos += int(want_s)
    s_scr = refs[pos]
    n_tiles = DN_WIDTH // MXU_TILE
    i = pl.program_id(1)

    @pl.when(i == 0)
    def _init():
        if have_s0:
            s_scr[...] = s0_ref[0]
        else:
            s_scr[...] = jnp.zeros_like(s_scr)

    bmask = _block_mask()
    chains = [(d, g) for d in range(N_DIR) for g in range(n_tiles)]

    def body(j, carry):
        cidx = (j, n_chunk - 1 - j)
        r0s = [pl.multiple_of(cidx[d] * CHUNK, CHUNK) for d in range(N_DIR)]
        s_olds, r1s = [], []
        for d, g in chains:
            lanes = slice(g * MXU_TILE, (g + 1) * MXU_TILE)
            w = dir_refs[d][0][0, 0, pl.ds(r0s[d], CHUNK), lanes]
            if want_o:
                w = jnp.concatenate([w, dir_refs[d][4][0, 0, pl.ds(r0s[d], CHUNK), lanes]], axis=0)
            s_old = s_scr[d * n_tiles + g]
            s_olds.append(s_old)
            r1s.append(_dot(w, s_old.astype(BF16)))
        u_bfs = []
        for (d, g), r1 in zip(chains, r1s):
            lanes = slice(g * MXU_TILE, (g + 1) * MXU_TILE)
            u_bfs.append((dir_refs[d][1][0, 0, pl.ds(r0s[d], CHUNK), lanes] - r1[0:CHUNK]).astype(BF16))
        for (d, g), r1, u_bf, s_old in zip(chains, r1s, u_bfs, s_olds):
            lanes = slice(g * MXU_TILE, (g + 1) * MXU_TILE)
            kt = dir_refs[d][2][0, 0, pl.ds(r0s[d], CHUNK), lanes]
            ds = jnp.where(bmask, _dot_tn(kt, u_bf), 0.0)
            dl = dir_refs[d][3][0, 0, cidx[d]][:, lanes]
            s_scr[d * n_tiles + g] = s_old * dl + ds
            if want_o:
                intra = dir_refs[d][5][0, 0, pl.ds(r0s[d], CHUNK), lanes]
                o_refs[d][0, pl.ds(r0s[d], CHUNK), lanes] = r1[CHUNK:2 * CHUNK] + _dot(intra, _bdiag(u_bf, bmask))
        return carry

    lax.fori_loop(0, n_chunk, body, 0)

    if want_s:
        @pl.when(i == pl.num_programs(1) - 1)
        def _fin():
            sout_ref[0] = s_scr[...]


def _dnscan(prep, s0, tb, want_s):
    want_o = len(prep) == 6
    b, _, t, w = prep[0].shape
    n_t = t // tb
    n_chunk = tb // CHUNK
    n_chain = N_DIR * (w // MXU_TILE)

    def specs(d):
        blk = (lambda bi, i: i) if d == 0 else (lambda bi, i: n_t - 1 - i)
        big = pl.BlockSpec((1, 1, tb, w), lambda bi, i: (bi, d, blk(bi, i), 0))
        dl = pl.BlockSpec((1, 1, n_chunk, 1, w), lambda bi, i: (bi, d, blk(bi, i), 0, 0))
        return [big, big, big, dl] + ([big, big] if want_o else [])

    in_specs = specs(0) + specs(1)
    args = list(prep) + list(prep)
    state_spec = pl.BlockSpec((1, n_chain, MXU_TILE, MXU_TILE), lambda bi, i: (bi, 0, 0, 0))
    if s0 is not None:
        in_specs.append(state_spec)
        args.append(s0)
    out_shape, out_specs = [], []
    if want_o:
        out_shape += [jax.ShapeDtypeStruct((b, t, w), F32)] * N_DIR
        out_specs += [pl.BlockSpec((1, tb, w), lambda bi, i: (bi, i, 0)),
                      pl.BlockSpec((1, tb, w), lambda bi, i: (bi, n_t - 1 - i, 0))]
    if want_s:
        out_shape.append(jax.ShapeDtypeStruct((b, n_chain, MXU_TILE, MXU_TILE), F32))
        out_specs.append(state_spec)
    return pl.pallas_call(
        functools.partial(_dnscan_kernel, n_chunk=n_chunk, want_o=want_o, have_s0=s0 is not None, want_s=want_s),
        grid=(b, n_t),
        in_specs=in_specs,
        out_specs=out_specs,
        out_shape=out_shape,
        scratch_shapes=[pltpu.VMEM((n_chain, MXU_TILE, MXU_TILE), F32)],
        compiler_params=_cparams(2),
        name="dnscan",
    )(*args)


def _merge_kernel(x_ref, mod_ref, ya_ref, odf_ref, odb_ref, z_ref, gate_ref, dng_ref, havg_ref, wba_ref, wbd_ref,
                  wo_ref, gn2_ref, out1_ref, hm_ref):
    havg = havg_ref[...]
    dng = dng_ref[...]
    yd_parts = []
    for j in range(DN_WIDTH // MXU_TILE):
        sl = slice(j * MXU_TILE, (j + 1) * MXU_TILE)
        od = odf_ref[0, :, sl] + odb_ref[0, :, sl]
        ms = _dot((od * od).astype(BF16), havg)
        z = z_ref[0, :, sl].astype(F32)
        yd_parts.append((od * lax.rsqrt(ms + EPS) * dng * (z * _sigmoid(z))).astype(BF16))
    yd = jnp.concatenate(yd_parts, axis=1)
    ga = gate_ref[0, :, 0:D_MODEL].astype(F32)
    gd = gate_ref[0, :, D_MODEL:2 * D_MODEL].astype(F32)
    y = _sigmoid(ga) * _dot(ya_ref[0], wba_ref[...]) + _sigmoid(gd) * _dot(yd, wbd_ref[...])
    br = _dot(y.astype(BF16), wo_ref[...])
    mod = mod_ref[0]
    out1 = x_ref[0] + mod[2:3] * br
    out1_ref[0] = out1
    ms2 = jnp.mean(out1 * out1, axis=-1, keepdims=True)
    hm = out1 * lax.rsqrt(ms2 + EPS) * (gn2_ref[...] * (1.0 + mod[4:5])) + mod[3:4]
    hm_ref[0] = hm.astype(BF16)


def _merge(x, mod3, y_attn, o_df, o_db, z, gates, dng, havg, wba, wbd, wo, gn2, tm):
    b, t, d = x.shape
    tok = lambda bi, i: (bi, i, 0)
    const2 = lambda bi, i: (0, 0)
    return pl.pallas_call(
        _merge_kernel,
        grid=(b, t // tm),
        in_specs=[pl.BlockSpec((1, tm, d), tok),
                  pl.BlockSpec((1, 6, d), lambda bi, i: (bi, 0, 0)),
                  pl.BlockSpec((1, tm, ATTN_WIDTH), tok),
                  pl.BlockSpec((1, tm, DN_WIDTH), tok),
                  pl.BlockSpec((1, tm, DN_WIDTH), tok),
                  pl.BlockSpec((1, tm, DN_WIDTH), tok),
                  pl.BlockSpec((1, tm, 2 * d), tok),
                  pl.BlockSpec((1, MXU_TILE), const2),
                  pl.BlockSpec((MXU_TILE, MXU_TILE), const2),
                  pl.BlockSpec(wba.shape, const2),
                  pl.BlockSpec(wbd.shape, const2),
                  pl.BlockSpec(wo.shape, const2),
                  pl.BlockSpec((1, d), const2)],
        out_specs=[pl.BlockSpec((1, tm, d), tok), pl.BlockSpec((1, tm, d), tok)],
        out_shape=[jax.ShapeDtypeStruct((b, t, d), F32), jax.ShapeDtypeStruct((b, t, d), BF16)],
        compiler_params=_cparams(2),
        name="merge",
    )(x, mod3, y_attn, o_df, o_db, z, gates, dng, havg, wba, wbd, wo, gn2)


def _mlp_kernel(out1_ref, hm_ref, mod_ref, w1_ref, w2_ref, o_ref, *, ff_chunk):
    hm = hm_ref[0]
    acc = None
    for j in range(D_FF // ff_chunk):
        a = jnp.maximum(_dot(hm, w1_ref[:, j * ff_chunk:(j + 1) * ff_chunk]), 0.0)
        part = _dot((a * a).astype(BF16), w2_ref[j * ff_chunk:(j + 1) * ff_chunk, :])
        acc = part if acc is None else acc + part
    o_ref[0] = out1_ref[0] + mod_ref[0][5:6] * acc


def _mlp(out1, hm, mod3, w1, w2, tm):
    b, t, d = out1.shape
    tok = lambda bi, i: (bi, i, 0)
    const2 = lambda bi, i: (0, 0)
    return pl.pallas_call(
        functools.partial(_mlp_kernel, ff_chunk=1024),
        grid=(b, t // tm),
        in_specs=[pl.BlockSpec((1, tm, d), tok),
                  pl.BlockSpec((1, tm, d), tok),
                  pl.BlockSpec((1, 6, d), lambda bi, i: (bi, 0, 0)),
                  pl.BlockSpec(w1.shape, const2),
                  pl.BlockSpec(w2.shape, const2)],
        out_specs=pl.BlockSpec((1, tm, d), tok),
        out_shape=jax.ShapeDtypeStruct((b, t, d), F32),
        compiler_params=_cparams(2),
        name="mlp",
    )(out1, hm, mod3, w1, w2)


def _head_avg(n, scale):
    idx = np.arange(n) // HEAD_DIM
    return jnp.asarray((idx[:, None] == idx[None, :]).astype(np.float32) * scale, BF16)


def _kv_tile_matrix():
    m = np.zeros((KV_WIDTH, ATTN_KV_HEADS * MXU_TILE), np.float32)
    for g in range(ATTN_KV_HEADS):
        for r in range(ATTN_GROUP):
            for dd in range(HEAD_DIM):
                m[g * HEAD_DIM + dd, g * MXU_TILE + r * HEAD_DIM + dd] = 1.0
    return jnp.asarray(m, BF16)


def _dn_expand_matrix():
    n = N_DIR * DN_HEADS
    m = np.zeros((N_DIR, 4 * n, 2 * DN_WIDTH), np.float32)
    for d in range(N_DIR):
        for part in range(2):
            for h in range(DN_HEADS):
                idx = d * DN_HEADS + h
                m[d, part * 2 * n + idx, h * HEAD_DIM:(h + 1) * HEAD_DIM] = 1.0
                m[d, part * 2 * n + n + idx, DN_WIDTH + h * HEAD_DIM:DN_WIDTH + (h + 1) * HEAD_DIM] = 1.0
    return jnp.asarray(m, BF16)


def _tri_matrices():
    i = np.arange(CHUNK)
    low = (i[:, None] >= i[None, :]).astype(np.float32)
    up = (i[:, None] <= i[None, :]).astype(np.float32)
    return jnp.asarray(np.stack([np.concatenate([low, low], axis=1), np.concatenate([up, up], axis=1)]), BF16)


def _rope_tables(seq):
    half = HEAD_DIM // 2
    n_freq = half // 2
    freqs = ROPE_BASE ** (-jnp.arange(n_freq, dtype=F32) / n_freq)
    pos = jnp.arange(seq)
    ang_r = (pos // GRID_W).astype(F32)[:, None] * freqs
    ang_c = (pos % GRID_W).astype(F32)[:, None] * freqs
    cos = jnp.concatenate([jnp.cos(ang_r)] * 2 + [jnp.cos(ang_c)] * 2, axis=1)
    sin = jnp.concatenate([-jnp.sin(ang_r), jnp.sin(ang_r), -jnp.sin(ang_c), jnp.sin(ang_c)], axis=1)
    reps = LANES // HEAD_DIM
    return jnp.tile(cos, (1, reps)), jnp.tile(sin, (1, reps))


def _pad_cols(w, n):
    return jnp.pad(w, ((0, 0), (0, n - w.shape[1])))


def kernel(x, c, ctx, c_ctx, w_ada, b_ada, g_norm1, w_in, q_norm_g, k_norm_g, attn_sink, conv_w, a_log, dt_bias,
           dn_norm_g, w_br_attn, w_br_dn, w_out, g_norm2, w_mlp1, w_mlp2):
    depth = w_ada.shape[0]
    assert depth == 1, "single-layer trunk only"
    b, s, d = x.shape
    n_ctx = ctx.shape[1]
    assert d == D_MODEL and w_in.shape[-1] == _IN_WIDTH
    assert s >= 3 * ATTN_BLOCK and s % ATTN_BLOCK == 0 and s % CHUNK == 0 and n_ctx % CHUNK == 0
    out_dtype = x.dtype
    w_in0 = w_in[0]

    mod_rows = 16
    cc = jnp.concatenate([c.astype(F32), c_ctx.astype(F32)[None], jnp.zeros((mod_rows - b - 1, d), F32)], axis=0)
    mod = _ada(cc, w_ada[0], b_ada[0])
    mod3 = mod.reshape(mod_rows, 6, d)

    ab_cols = jnp.concatenate([w_in0[:, _OFF_DA:_OFF_GA]] * 2, axis=1)
    w_lat = jnp.concatenate([w_in0[:, :_OFF_DA], w_in0[:, _OFF_GA:], _pad_cols(ab_cols, LANES)], axis=1).astype(BF16)
    segs_lat = ((0, _OFF_DQ), (_OFF_DQ, 3 * DN_WIDTH), (_OFF_DZ, DN_WIDTH), (_OFF_DA, 2 * D_MODEL),
                (_OFF_DA + 2 * D_MODEL, LANES))
    a_lat, d_lat, z_lat, gates, ab_lat = _inproj(x, mod3, None, g_norm1[0], w_lat, segs_lat,
                                                 (BF16, BF16, BF16, BF16, F32), tm=512)
    w_ctx = jnp.concatenate([w_in0[:, _OFF_AK:_OFF_DQ], w_in0[:, _OFF_DK:_OFF_DZ], _pad_cols(ab_cols, LANES)],
                            axis=1).astype(BF16)
    segs_ctx = ((0, 2 * KV_WIDTH), (2 * KV_WIDTH, 2 * DN_WIDTH), (2 * KV_WIDTH + 2 * DN_WIDTH, LANES))
    kv_ctx, d_ctx, ab_ctx = _inproj(ctx, mod3, b, g_norm1[0], w_ctx, segs_ctx, (BF16, BF16, F32), tm=n_ctx)

    hsum = _head_avg(MXU_TILE, 1.0)
    q_d, k_d, v_d = _dnpre(d_lat, conv_w[0], hsum, (True, True, False), (True, False, False), tm=512)
    k_dc, v_dc = _dnpre(d_ctx, conv_w[0][:, DN_WIDTH:], hsum, (True, False), (False, False), tm=n_ctx)

    cos, sin = _rope_tables(s)
    reps = LANES // HEAD_DIM
    y_attn = _attention(a_lat, kv_ctx, cos, sin,
                        jnp.tile(q_norm_g[0].astype(F32), reps)[None], jnp.tile(k_norm_g[0].astype(F32), reps)[None],
                        jnp.broadcast_to(attn_sink[0].astype(F32)[:, None], (ATTN_Q_HEADS, LANES)),
                        _head_avg(LANES, 1.0 / HEAD_DIM), _kv_tile_matrix())

    n_gate = N_DIR * DN_HEADS
    arow = _pad_cols(jnp.tile(jnp.concatenate([jnp.exp(a_log[0]).reshape(1, n_gate), jnp.zeros((1, n_gate), F32)],
                                              axis=1), (1, 2)), LANES)
    dtrow = _pad_cols(jnp.tile(jnp.concatenate([dt_bias[0].reshape(1, n_gate), jnp.zeros((1, n_gate), F32)],
                                               axis=1), (1, 2)), LANES)
    exp2, tri = _dn_expand_matrix(), _tri_matrices()
    prep_ctx = _dnprep(None, k_dc, v_dc, ab_ctx, arow, dtrow, exp2, tri, cb=2)
    prep_lat = _dnprep(q_d, k_d, v_d, ab_lat, arow, dtrow, exp2, tri, cb=2)
    (s_ctx,) = _dnscan(prep_ctx, None, tb=n_ctx, want_s=True)
    o_df, o_db = _dnscan(prep_lat, s_ctx, tb=512, want_s=False)

    out1, hm = _merge(x, mod3, y_attn, o_df, o_db, z_lat, gates,
                      jnp.tile(dn_norm_g[0].astype(F32), HEADS_PER_TILE)[None], _head_avg(MXU_TILE, 1.0 / HEAD_DIM),
                      w_br_attn[0].astype(BF16), w_br_dn[0].astype(BF16), w_out[0].astype(BF16),
                      g_norm2[0].reshape(1, d), tm=256)
    out = _mlp(out1, hm, mod3, w_mlp1[0].astype(BF16), w_mlp2[0].astype(BF16), tm=256)
    return out.astype(out_dtype)
```

```python
import functools

import numpy as np
import jax
import jax.numpy as jnp
from jax import lax
from jax.experimental import pallas as pl
from jax.experimental.pallas import tpu as pltpu

F32 = jnp.float32
BF16 = jnp.bfloat16

D_MODEL = 1024
GRID_W = 64
HEAD_DIM = 64
ATTN_Q_HEADS = 8
ATTN_KV_HEADS = 2
ATTN_GROUP = ATTN_Q_HEADS // ATTN_KV_HEADS
WINDOW = 128
ATTN_BLOCK = 128
ROPE_BASE = 10000.0
DN_HEADS = 8
CONV_W = 3
CHUNK = 64
N_DIR = 2
D_FF = 4 * D_MODEL
EPS = 1e-6
NEG_INF = -1e30

ATTN_WIDTH = ATTN_Q_HEADS * HEAD_DIM
KV_WIDTH = ATTN_KV_HEADS * HEAD_DIM
DN_WIDTH = DN_HEADS * HEAD_DIM
LANES = 128
MXU_TILE = 256
HEADS_PER_TILE = MXU_TILE // HEAD_DIM
DN_SUB = 16
VMEM_LIMIT = 56 * 1024 * 1024

_OFF_AQ = 0
_OFF_AK = _OFF_AQ + ATTN_WIDTH
_OFF_AV = _OFF_AK + KV_WIDTH
_OFF_DQ = _OFF_AV + KV_WIDTH
_OFF_DK = _OFF_DQ + DN_WIDTH
_OFF_DV = _OFF_DK + DN_WIDTH
_OFF_DZ = _OFF_DV + DN_WIDTH
_OFF_DA = _OFF_DZ + DN_WIDTH
_OFF_DB = _OFF_DA + N_DIR * DN_HEADS
_OFF_GA = _OFF_DB + N_DIR * DN_HEADS
_OFF_GD = _OFF_GA + D_MODEL
_IN_WIDTH = _OFF_GD + D_MODEL


def _sigmoid(x):
    return 1.0 / (1.0 + jnp.exp(-x))


def _dot(a, b):
    return jnp.dot(a, b, preferred_element_type=F32)


def _dot_nt(a, b):
    return lax.dot_general(a, b, (((1,), (1,)), ((), ())), preferred_element_type=F32)


def _dot_tn(a, b):
    return lax.dot_general(a, b, (((0,), (0,)), ((), ())), preferred_element_type=F32)


def _cparams(n_axes):
    return pltpu.CompilerParams(dimension_semantics=("arbitrary",) * n_axes, vmem_limit_bytes=VMEM_LIMIT)


def _ada_kernel(c_ref, w_ref, b_ref, o_ref):
    c = c_ref[...]
    s = c * _sigmoid(c)
    o_ref[...] = _dot(s.astype(BF16), w_ref[...].astype(BF16)) + b_ref[...]


def _ada(cc, w_ada, b_ada):
    rows, d = cc.shape
    n = w_ada.shape[1]
    tn = 1536
    return pl.pallas_call(
        _ada_kernel,
        grid=(n // tn,),
        in_specs=[pl.BlockSpec((rows, d), lambda j: (0, 0)),
                  pl.BlockSpec((d, tn), lambda j: (0, j)),
                  pl.BlockSpec((1, tn), lambda j: (0, j))],
        out_specs=pl.BlockSpec((rows, tn), lambda j: (0, j)),
        out_shape=jax.ShapeDtypeStruct((rows, n), F32),
        compiler_params=_cparams(1),
        name="ada",
    )(cc, w_ada, b_ada.reshape(1, n))


def _inproj_kernel(x_ref, mod_ref, g_ref, w_ref, *out_refs, segs):
    x = x_ref[0]
    ms = jnp.mean(x * x, axis=-1, keepdims=True)
    mod = mod_ref[0]
    scale = g_ref[...] * (1.0 + mod[1:2])
    h = (x * lax.rsqrt(ms + EPS) * scale + mod[0:1]).astype(BF16)
    for o_ref, (start, size) in zip(out_refs, segs):
        o_ref[0] = _dot(h, w_ref[:, start:start + size]).astype(o_ref.dtype)


def _inproj(x, mod3, mod_row, g_norm, w, segs, dtypes, tm):
    b, t, d = x.shape
    n = w.shape[1]
    if mod_row is None:
        mod_map = lambda bi, i: (bi, 0, 0)
    else:
        mod_map = lambda bi, i: (mod_row, 0, 0)
    out_shape = [jax.ShapeDtypeStruct((b, t, size), dt) for (_, size), dt in zip(segs, dtypes)]
    out_specs = [pl.BlockSpec((1, tm, size), lambda bi, i: (bi, i, 0)) for (_, size) in segs]
    return pl.pallas_call(
        functools.partial(_inproj_kernel, segs=segs),
        grid=(b, t // tm),
        in_specs=[pl.BlockSpec((1, tm, d), lambda bi, i: (bi, i, 0)),
                  pl.BlockSpec((1, 6, d), mod_map),
                  pl.BlockSpec((1, d), lambda bi, i: (0, 0)),
                  pl.BlockSpec((d, n), lambda bi, i: (0, 0))],
        out_specs=out_specs,
        out_shape=out_shape,
        compiler_params=_cparams(2),
        name="inproj",
    )(x, mod3, g_norm.reshape(1, d), w)


def _dnpre_kernel(x_ref, prev_ref, next_ref, cw_ref, hsum_ref, *out_refs, tm, halo, norm_flags, q_flags):
    i = pl.program_id(1)
    last = pl.num_programs(1) - 1
    x = x_ref[0].astype(F32)
    prev_row = prev_ref[0][halo - 1:halo, :].astype(F32) * (i > 0).astype(F32)
    next_row = next_ref[0][0:1, :].astype(F32) * (i < last).astype(F32)
    rows = lax.broadcasted_iota(jnp.int32, x.shape, 0)
    xm = jnp.where(rows == 0, prev_row, pltpu.roll(x, 1, 0))
    xp = jnp.where(rows == tm - 1, next_row, pltpu.roll(x, tm - 1, 0))
    cw = cw_ref[...]
    y = xm * cw[0:1] + x * cw[1:2] + xp * cw[2:3]
    y = y * _sigmoid(y)
    hsum = hsum_ref[...]
    for s, (o_ref, do_norm, is_q) in enumerate(zip(out_refs, norm_flags, q_flags)):
        for j in range(DN_WIDTH // MXU_TILE):
            lo = s * DN_WIDTH + j * MXU_TILE
            ys = y[:, lo:lo + MXU_TILE]
            if do_norm:
                ss = _dot((ys * ys).astype(BF16), hsum)
                ys = ys * lax.rsqrt(ss + EPS)
                if is_q:
                    ys = ys * (HEAD_DIM ** -0.5)
            o_ref[0, :, j * MXU_TILE:(j + 1) * MXU_TILE] = ys.astype(o_ref.dtype)


def _dnpre(xd, conv_w, hsum, norm_flags, q_flags, tm):
    b, t, c = xd.shape
    halo = 16
    nblk = t // halo
    r = tm // halo
    n_out = c // DN_WIDTH
    return pl.pallas_call(
        functools.partial(_dnpre_kernel, tm=tm, halo=halo, norm_flags=norm_flags, q_flags=q_flags),
        grid=(b, t // tm),
        in_specs=[pl.BlockSpec((1, tm, c), lambda bi, i: (bi, i, 0)),
                  pl.BlockSpec((1, halo, c), lambda bi, i: (bi, jnp.maximum(i * r - 1, 0), 0)),
                  pl.BlockSpec((1, halo, c), lambda bi, i: (bi, jnp.minimum((i + 1) * r, nblk - 1), 0)),
                  pl.BlockSpec((CONV_W, c), lambda bi, i: (0, 0)),
                  pl.BlockSpec((MXU_TILE, MXU_TILE), lambda bi, i: (0, 0))],
        out_specs=[pl.BlockSpec((1, tm, DN_WIDTH), lambda bi, i: (bi, i, 0))] * n_out,
        out_shape=[jax.ShapeDtypeStruct((b, t, DN_WIDTH), BF16)] * n_out,
        compiler_params=_cparams(2),
        name="dnpre",
    )(xd, xd, xd, conv_w, hsum)


def _rope(x, cos, sin, lane):
    swapped = jnp.where((lane % 32) < 16, pltpu.roll(x, LANES - 16, 1), pltpu.roll(x, 16, 1))
    return x * cos + swapped * sin


def _attn_kernel(a_ref, kvc_ref, cos_ref, sin_ref, qg_ref, kg_ref, sink_ref, havg_ref, tile_ref,
                 o_ref, kt_s, v4_s, *, seq, ctx):
    n = pl.program_id(1)
    havg = havg_ref[...]
    tile = tile_ref[...]
    prep_rows = 256
    ctx_blocks = ctx // ATTN_BLOCK

    @pl.when(n == 0)
    def _prep():
        kg = kg_ref[...]
        kc = kvc_ref[0, :, 0:KV_WIDTH].astype(F32)
        ms = _dot((kc * kc).astype(BF16), havg)
        kc = kc * lax.rsqrt(ms + EPS) * kg
        for j in range(ctx_blocks):
            kt_s[j] = kc[j * ATTN_BLOCK:(j + 1) * ATTN_BLOCK, :].T.astype(BF16)
        v4_s[0:ctx, :] = _dot(kvc_ref[0, :, KV_WIDTH:2 * KV_WIDTH], tile).astype(BF16)
        lane = lax.broadcasted_iota(jnp.int32, (prep_rows, LANES), 1)
        for r0 in range(0, seq, prep_rows):
            kx = a_ref[0, r0:r0 + prep_rows, _OFF_AK:_OFF_AK + KV_WIDTH].astype(F32)
            ms = _dot((kx * kx).astype(BF16), havg)
            kx = kx * lax.rsqrt(ms + EPS) * kg
            kx = _rope(kx, cos_ref[r0:r0 + prep_rows, :], sin_ref[r0:r0 + prep_rows, :], lane)
            for j in range(prep_rows // ATTN_BLOCK):
                kt_s[ctx_blocks + r0 // ATTN_BLOCK + j] = kx[j * ATTN_BLOCK:(j + 1) * ATTN_BLOCK, :].T.astype(BF16)
            vx = a_ref[0, r0:r0 + prep_rows, _OFF_AV:_OFF_AV + KV_WIDTH]
            v4_s[ctx + r0:ctx + r0 + prep_rows, :] = _dot(vx, tile).astype(BF16)

    q0 = pl.multiple_of(n * ATTN_BLOCK, ATTN_BLOCK)
    lane = lax.broadcasted_iota(jnp.int32, (ATTN_BLOCK, LANES), 1)
    cos = cos_ref[pl.ds(q0, ATTN_BLOCK), :]
    sin = sin_ref[pl.ds(q0, ATTN_BLOCK), :]
    qg = qg_ref[...]
    q_parts = []
    for j in range(ATTN_WIDTH // LANES):
        qx = a_ref[0, pl.ds(q0, ATTN_BLOCK), j * LANES:(j + 1) * LANES].astype(F32)
        ms = _dot((qx * qx).astype(BF16), havg)
        qx = qx * lax.rsqrt(ms + EPS) * qg
        q_parts.append(_rope(qx, cos, sin, lane) * (HEAD_DIM ** -0.5))

    n_band = 3
    n_win = n_band * ATTN_BLOCK
    n_keys = n_win + ctx
    blk0 = jnp.clip(n - 1, 0, seq // ATTN_BLOCK - n_band)
    start = pl.multiple_of(blk0 * ATTN_BLOCK, ATTN_BLOCK)
    qpos = q0 + lax.broadcasted_iota(jnp.int32, (ATTN_BLOCK, n_keys), 0)
    kcol = lax.broadcasted_iota(jnp.int32, (ATTN_BLOCK, n_keys), 1)
    valid = (jnp.abs(qpos - (start + kcol)) <= WINDOW) | (kcol >= n_win)
    head_of_lane = lax.broadcasted_iota(jnp.int32, (1, MXU_TILE), 1) // HEAD_DIM
    kt_loc = kt_s[pl.ds(ctx_blocks + blk0, n_band)]
    kt_all = jnp.concatenate([kt_loc[j] for j in range(n_band)] + [kt_s[j] for j in range(ctx_blocks)], axis=1)

    outs = []
    for g in range(ATTN_KV_HEADS):
        gl = g * MXU_TILE
        v_all = jnp.concatenate([v4_s[pl.ds(ctx + start, n_win), gl:gl + MXU_TILE], v4_s[0:ctx, gl:gl + MXU_TILE]],
                                axis=0)
        qhs = []
        for r in range(ATTN_GROUP):
            h = g * ATTN_GROUP + r
            qx = q_parts[h // 2]
            if h % 2 != g:
                qx = pltpu.roll(qx, HEAD_DIM, 1)
            qhs.append(jnp.where(lane // HEAD_DIM == g, qx, 0.0).astype(BF16))
        ss = [jnp.where(valid, _dot(qh, kt_all), NEG_INF) for qh in qhs]
        sinks = [sink_ref[g * ATTN_GROUP + r:g * ATTN_GROUP + r + 1, 0:1] for r in range(ATTN_GROUP)]
        maxes = [jnp.maximum(jnp.max(s, axis=-1, keepdims=True), sink) for s, sink in zip(ss, sinks)]
        ps = [jnp.exp(s - m) for s, m in zip(ss, maxes)]
        invs = [1.0 / (jnp.sum(p, axis=-1, keepdims=True) + jnp.exp(sink - m))
                for p, sink, m in zip(ps, sinks, maxes)]
        o_fulls = [_dot(p.astype(BF16), v_all) for p in ps]
        acc = None
        for r, (o_full, inv) in enumerate(zip(o_fulls, invs)):
            term = jnp.where(head_of_lane == r, o_full * inv, 0.0)
            acc = term if acc is None else acc + term
        outs.append(acc)
    o_ref[0] = jnp.concatenate(outs, axis=1).astype(o_ref.dtype)


def _attention(a_lat, kv_ctx, cos, sin, qg, kg, sink, havg, tile):
    b, s, wa = a_lat.shape
    ctx = kv_ctx.shape[1]
    nb = s // ATTN_BLOCK
    const2 = lambda bi, n: (0, 0)
    return pl.pallas_call(
        functools.partial(_attn_kernel, seq=s, ctx=ctx),
        grid=(b, nb),
        in_specs=[pl.BlockSpec((1, s, wa), lambda bi, n: (bi, 0, 0)),
                  pl.BlockSpec((1, ctx, 2 * KV_WIDTH), lambda bi, n: (bi, 0, 0)),
                  pl.BlockSpec((s, LANES), const2),
                  pl.BlockSpec((s, LANES), const2),
                  pl.BlockSpec((1, LANES), const2),
                  pl.BlockSpec((1, LANES), const2),
                  pl.BlockSpec((ATTN_Q_HEADS, LANES), const2),
                  pl.BlockSpec((LANES, LANES), const2),
                  pl.BlockSpec((LANES, ATTN_KV_HEADS * MXU_TILE), const2)],
        out_specs=pl.BlockSpec((1, ATTN_BLOCK, ATTN_WIDTH), lambda bi, n: (bi, n, 0)),
        out_shape=jax.ShapeDtypeStruct((b, s, ATTN_WIDTH), BF16),
        scratch_shapes=[pltpu.VMEM(((ctx + s) // ATTN_BLOCK, KV_WIDTH, ATTN_BLOCK), BF16),
                        pltpu.VMEM((ctx + s, ATTN_KV_HEADS * MXU_TILE), BF16)],
        compiler_params=_cparams(2),
        name="attn",
    )(a_lat, kv_ctx, cos, sin, qg, kg, sink, havg, tile)


def _bdiag(x_lane, bmask):
    return jnp.where(bmask, jnp.concatenate([x_lane] * HEADS_PER_TILE, axis=0), jnp.zeros((), x_lane.dtype))


def _unit_tri_inverses(a_list, eye_l, sub_mask, bmask):
    ads = [jnp.where(sub_mask, a, 0.0) for a in a_list]
    aos = [jnp.where(sub_mask, 0.0, a).astype(BF16) for a in a_list]
    ps = [eye_l - ad for ad in ads]
    pws = [ad.astype(BF16) for ad in ads]
    n_levels = int(np.log2(DN_SUB))
    for level in range(n_levels):
        rhss = [_bdiag(pw, bmask) for pw in pws]
        if level == 0:
            pws = [_dot(pw, rhs).astype(BF16) for pw, rhs in zip(pws, rhss)]
        elif level < n_levels - 1:
            boths = [_dot(jnp.concatenate([pw, p.astype(BF16)], axis=0), rhs) for pw, p, rhs in zip(pws, ps, rhss)]
            pws = [both[0:CHUNK].astype(BF16) for both in boths]
            ps = [p + both[CHUNK:2 * CHUNK] for p, both in zip(ps, boths)]
        else:
            ps = [p + _dot(p.astype(BF16), rhs) for p, rhs in zip(ps, rhss)]
    xs = [p.astype(BF16) for p in ps]
    bs = [_dot(x, _bdiag(ao, bmask)) for x, ao in zip(xs, aos)]
    b_rhss = [_bdiag(b.astype(BF16), bmask) for b in bs]
    b2s = [_dot(b.astype(BF16), rhs) for b, rhs in zip(bs, b_rhss)]
    b3s = [_dot(b2.astype(BF16), rhs) for b2, rhs in zip(b2s, b_rhss)]
    assert CHUNK // DN_SUB == 4
    qs = [(eye_l - b + b2 - b3).astype(BF16) for b, b2, b3 in zip(bs, b2s, b3s)]
    return [_dot(q, _bdiag(x, bmask)) for q, x in zip(qs, xs)]


def _dn_factors(ab, d, arow, dtrow, exp2_ref, tri_ref, eye_t):
    lane = lax.broadcasted_iota(jnp.int32, ab.shape, 1)
    z = ab + dtrow
    softplus = jnp.maximum(z, 0.0) + jnp.log(1.0 + jnp.exp(-jnp.abs(z)))
    is_g = (lane % 32) < 16
    x = jnp.where(is_g, -arow * softplus, _sigmoid(ab))
    x_hi = x.astype(BF16)
    x_lo = (x - x_hi.astype(F32)).astype(BF16)
    xhl = jnp.where(lane < 32, x_hi, x_lo)[:, 0:64]
    y = _dot(xhl, exp2_ref[d])
    ge = y[:, 0:DN_WIDTH]
    be = y[:, DN_WIDTH:2 * DN_WIDTH]
    ge_hi = ge.astype(BF16)
    ge_lo = (ge - ge_hi.astype(F32)).astype(BF16)
    gi = _dot(tri_ref[d], jnp.concatenate([ge_hi, ge_lo], axis=0))
    gj = jnp.sum(gi * eye_t, axis=0, keepdims=True)
    return be, gi, jnp.broadcast_to(gj, gi.shape)


def _block_mask():
    return (lax.broadcasted_iota(jnp.int32, (MXU_TILE, MXU_TILE), 0) // HEAD_DIM
            == lax.broadcasted_iota(jnp.int32, (MXU_TILE, MXU_TILE), 1) // HEAD_DIM)


def _dnprep_kernel(*refs, cb, want_o):
    if want_o:
        q_ref, k_ref, v_ref, ab_ref = refs[:4]
        refs = refs[4:]
    else:
        q_ref = None
        k_ref, v_ref, ab_ref = refs[:3]
        refs = refs[3:]
    arow_ref, dtrow_ref, exp2_ref, tri_ref, w_o, uv_o, kt_o, dl_o = refs[:8]
    qd_o, in_o = refs[8:10] if want_o else (None, None)
    n_tiles = DN_WIDTH // MXU_TILE
    row = lax.broadcasted_iota(jnp.int32, (CHUNK, MXU_TILE), 0)
    colj = lax.broadcasted_iota(jnp.int32, (CHUNK, MXU_TILE), 1) % HEAD_DIM
    eye_l = (row == colj).astype(F32)
    eye_t = jnp.concatenate([eye_l] * n_tiles, axis=1)
    bmask = _block_mask()
    arow = arow_ref[...]
    dtrow = dtrow_ref[...]

    fac = {}
    shared = {}
    for c in range(cb):
        rows = slice(c * CHUNK, (c + 1) * CHUNK)
        ab = ab_ref[0, rows, :]
        for d in range(N_DIR):
            fac[(c, d)] = _dn_factors(ab, d, arow, dtrow, exp2_ref, tri_ref, eye_t)
        for g in range(n_tiles):
            lanes = slice(g * MXU_TILE, (g + 1) * MXU_TILE)
            k_l = k_ref[0, rows, lanes]
            kbd = _bdiag(k_l, bmask)
            if want_o:
                kq = _dot_nt(jnp.concatenate([k_l, q_ref[0, rows, lanes]], axis=0), kbd)
                shared[(c, g)] = (kq[0:CHUNK], kq[CHUNK:2 * CHUNK])
            else:
                shared[(c, g)] = (_dot_nt(k_l, kbd), None)

    units = [(c, g, d) for c in range(cb) for g in range(n_tiles) for d in range(N_DIR)]
    decs = []
    for c, g, d in units:
        lanes = slice(g * MXU_TILE, (g + 1) * MXU_TILE)
        be, gi, gj = fac[(c, d)]
        lower = (row > colj) if d == 0 else (row < colj)
        decs.append(jnp.where(lower, jnp.exp(jnp.where(lower, gi[:, lanes] - gj[:, lanes], 0.0)), 0.0))
    a_list = [fac[(c, d)][0][:, g * MXU_TILE:(g + 1) * MXU_TILE] * dec * shared[(c, g)][0]
              for (c, g, d), dec in zip(units, decs)]
    tinvs = _unit_tri_inverses(a_list, eye_l, (row // DN_SUB) == (colj // DN_SUB), bmask)

    for (c, g, d), dec, tinv in zip(units, decs, tinvs):
        rows = slice(c * CHUNK, (c + 1) * CHUNK)
        lanes = slice(g * MXU_TILE, (g + 1) * MXU_TILE)
        be, gi, _ = fac[(c, d)]
        be, gi = be[:, lanes], gi[:, lanes]
        last = CHUNK - 1 if d == 0 else 0
        e_g = jnp.exp(gi)
        gl_row = gi[last:last + 1, :]
        kf = k_ref[0, rows, lanes].astype(F32)
        vf = v_ref[0, rows, lanes].astype(F32)
        rhs = jnp.concatenate([_bdiag((be * e_g * kf).astype(BF16), bmask), _bdiag((be * vf).astype(BF16), bmask)],
                              axis=1)
        wu = _dot(tinv.astype(BF16), rhs)
        w_o[0, d, rows, lanes] = wu[:, 0:MXU_TILE].astype(BF16)
        uv_o[0, d, rows, lanes] = wu[:, MXU_TILE:2 * MXU_TILE]
        kt_o[0, d, rows, lanes] = (jnp.exp(gl_row - gi) * kf).astype(BF16)
        dl_o[0, d, c, :, lanes] = jnp.exp(gl_row)
        if want_o:
            qd_o[0, d, rows, lanes] = (e_g * q_ref[0, rows, lanes].astype(F32)).astype(BF16)
            in_o[0, d, rows, lanes] = ((dec + eye_l) * shared[(c, g)][1]).astype(BF16)


def _dnprep(q, k, v, ab, arow, dtrow, exp2, tri, cb):
    want_o = q is not None
    b, t, w = k.shape
    tb = cb * CHUNK
    tok = lambda bi, i: (bi, i, 0)
    const2 = lambda bi, i: (0, 0)
    const3 = lambda bi, i: (0, 0, 0)
    dir_tok = lambda bi, i: (bi, 0, i, 0)
    data = ([q] if want_o else []) + [k, v]
    in_specs = ([pl.BlockSpec((1, tb, w), tok)] * len(data) + [pl.BlockSpec((1, tb, LANES), tok),
                pl.BlockSpec((1, LANES), const2), pl.BlockSpec((1, LANES), const2),
                pl.BlockSpec(exp2.shape, const3), pl.BlockSpec(tri.shape, const3)])
    big = lambda dt: jax.ShapeDtypeStruct((b, N_DIR, t, w), dt)
    big_spec = pl.BlockSpec((1, N_DIR, tb, w), dir_tok)
    out_shape = [big(BF16), big(F32), big(BF16), jax.ShapeDtypeStruct((b, N_DIR, t // CHUNK, 1, w), F32)]
    out_specs = [big_spec, big_spec, big_spec, pl.BlockSpec((1, N_DIR, cb, 1, w), lambda bi, i: (bi, 0, i, 0, 0))]
    if want_o:
        out_shape += [big(BF16), big(BF16)]
        out_specs += [big_spec, big_spec]
    return pl.pallas_call(
        functools.partial(_dnprep_kernel, cb=cb, want_o=want_o),
        grid=(b, t // tb),
        in_specs=in_specs,
        out_specs=out_specs,
        out_shape=out_shape,
        compiler_params=_cparams(2),
        name="dnprep",
    )(*data, ab, arow, dtrow, exp2, tri)


def _dnscan_kernel(*refs, n_chunk, want_o, have_s0, want_s):
    n_in = 6 if want_o else 4
    dir_refs = [refs[0:n_in], refs[n_in:2 * n_in]]
    pos = 2 * n_in
    s0_ref = refs[pos] if have_s0 else None
    pos += int(have_s0)
    o_refs = refs[pos:pos + N_DIR] if want_o else None
    pos += N_DIR if want_o else 0
    sout_ref = refs[pos] if want_s else None
    pos += int(want_s)
    s_scr = refs[pos]
    n_tiles = DN_WIDTH // MXU_TILE
    i = pl.program_id(1)

    @pl.when(i == 0)
    def _init():
        if have_s0:
            s_scr[...] = s0_ref[0]
        else:
            s_scr[...] = jnp.zeros_like(s_scr)

    bmask = _block_mask()
    chains = [(d, g) for d in range(N_DIR) for g in range(n_tiles)]

    def body(j, carry):
        cidx = (j, n_chunk - 1 - j)
        r0s = [pl.multiple_of(cidx[d] * CHUNK, CHUNK) for d in range(N_DIR)]
        s_olds, r1s = [], []
        for d, g in chains:
            lanes = slice(g * MXU_TILE, (g + 1) * MXU_TILE)
            w = dir_refs[d][0][0, 0, pl.ds(r0s[d], CHUNK), lanes]
            if want_o:
                w = jnp.concatenate([w, dir_refs[d][4][0, 0, pl.ds(r0s[d], CHUNK), lanes]], axis=0)
            s_old = s_scr[d * n_tiles + g]
            s_olds.append(s_old)
            r1s.append(_dot(w, s_old.astype(BF16)))
        u_bfs = []
        for (d, g), r1 in zip(chains, r1s):
            lanes = slice(g * MXU_TILE, (g + 1) * MXU_TILE)
            u_bfs.append((dir_refs[d][1][0, 0, pl.ds(r0s[d], CHUNK), lanes] - r1[0:CHUNK]).astype(BF16))
        for (d, g), r1, u_bf, s_old in zip(chains, r1s, u_bfs, s_olds):
            lanes = slice(g * MXU_TILE, (g + 1) * MXU_TILE)
            kt = dir_refs[d][2][0, 0, pl.ds(r0s[d], CHUNK), lanes]
            ds = jnp.where(bmask, _dot_tn(kt, u_bf), 0.0)
            dl = dir_refs[d][3][0, 0, cidx[d]][:, lanes]
            s_scr[d * n_tiles + g] = s_old * dl + ds
            if want_o:
                intra = dir_refs[d][5][0, 0, pl.ds(r0s[d], CHUNK), lanes]
                o_refs[d][0, pl.ds(r0s[d], CHUNK), lanes] = r1[CHUNK:2 * CHUNK] + _dot(intra, _bdiag(u_bf, bmask))
        return carry

    lax.fori_loop(0, n_chunk, body, 0)

    if want_s:
        @pl.when(i == pl.num_programs(1) - 1)
        def _fin():
            sout_ref[0] = s_scr[...]


def _dnscan(prep, s0, tb, want_s):
    want_o = len(prep) == 6
    b, _, t, w = prep[0].shape
    n_t = t // tb
    n_chunk = tb // CHUNK
    n_chain = N_DIR * (w // MXU_TILE)

    def specs(d):
        blk = (lambda bi, i: i) if d == 0 else (lambda bi, i: n_t - 1 - i)
        big = pl.BlockSpec((1, 1, tb, w), lambda bi, i: (bi, d, blk(bi, i), 0))
        dl = pl.BlockSpec((1, 1, n_chunk, 1, w), lambda bi, i: (bi, d, blk(bi, i), 0, 0))
        return [big, big, big, dl] + ([big, big] if want_o else [])

    in_specs = specs(0) + specs(1)
    args = list(prep) + list(prep)
    state_spec = pl.BlockSpec((1, n_chain, MXU_TILE, MXU_TILE), lambda bi, i: (bi, 0, 0, 0))
    if s0 is not None:
        in_specs.append(state_spec)
        args.append(s0)
    out_shape, out_specs = [], []
    if want_o:
        out_shape += [jax.ShapeDtypeStruct((b, t, w), F32)] * N_DIR
        out_specs += [pl.BlockSpec((1, tb, w), lambda bi, i: (bi, i, 0)),
                      pl.BlockSpec((1, tb, w), lambda bi, i: (bi, n_t - 1 - i, 0))]
    if want_s:
        out_shape.append(jax.ShapeDtypeStruct((b, n_chain, MXU_TILE, MXU_TILE), F32))
        out_specs.append(state_spec)
    return pl.pallas_call(
        functools.partial(_dnscan_kernel, n_chunk=n_chunk, want_o=want_o, have_s0=s0 is not None, want_s=want_s),
        grid=(b, n_t),
        in_specs=in_specs,
        out_specs=out_specs,
        out_shape=out_shape,
        scratch_shapes=[pltpu.VMEM((n_chain, MXU_TILE, MXU_TILE), F32)],
        compiler_params=_cparams(2),
        name="dnscan",
    )(*args)


def _merge_kernel(x_ref, mod_ref, ya_ref, odf_ref, odb_ref, z_ref, gate_ref, dng_ref, havg_ref, wba_ref, wbd_ref,
                  wo_ref, gn2_ref, out1_ref, hm_ref):
    havg = havg_ref[...]
    dng = dng_ref[...]
    yd_parts = []
    for j in range(DN_WIDTH // MXU_TILE):
        sl = slice(j * MXU_TILE, (j + 1) * MXU_TILE)
        od = odf_ref[0, :, sl] + odb_ref[0, :, sl]
        ms = _dot((od * od).astype(BF16), havg)
        z = z_ref[0, :, sl].astype(F32)
        yd_parts.append((od * lax.rsqrt(ms + EPS) * dng * (z * _sigmoid(z))).astype(BF16))
    yd = jnp.concatenate(yd_parts, axis=1)
    ga = gate_ref[0, :, 0:D_MODEL].astype(F32)
    gd = gate_ref[0, :, D_MODEL:2 * D_MODEL].astype(F32)
    y = _sigmoid(ga) * _dot(ya_ref[0], wba_ref[...]) + _sigmoid(gd) * _dot(yd, wbd_ref[...])
    br = _dot(y.astype(BF16), wo_ref[...])
    mod = mod_ref[0]
    out1 = x_ref[0] + mod[2:3] * br
    out1_ref[0] = out1
    ms2 = jnp.mean(out1 * out1, axis=-1, keepdims=True)
    hm = out1 * lax.rsqrt(ms2 + EPS) * (gn2_ref[...] * (1.0 + mod[4:5])) + mod[3:4]
    hm_ref[0] = hm.astype(BF16)


def _merge(x, mod3, y_attn, o_df, o_db, z, gates, dng, havg, wba, wbd, wo, gn2, tm):
    b, t, d = x.shape
    tok = lambda bi, i: (bi, i, 0)
    const2 = lambda bi, i: (0, 0)
    return pl.pallas_call(
        _merge_kernel,
        grid=(b, t // tm),
        in_specs=[pl.BlockSpec((1, tm, d), tok),
                  pl.BlockSpec((1, 6, d), lambda bi, i: (bi, 0, 0)),
                  pl.BlockSpec((1, tm, ATTN_WIDTH), tok),
                  pl.BlockSpec((1, tm, DN_WIDTH), tok),
                  pl.BlockSpec((1, tm, DN_WIDTH), tok),
                  pl.BlockSpec((1, tm, DN_WIDTH), tok),
                  pl.BlockSpec((1, tm, 2 * d), tok),
                  pl.BlockSpec((1, MXU_TILE), const2),
                  pl.BlockSpec((MXU_TILE, MXU_TILE), const2),
                  pl.BlockSpec(wba.shape, const2),
                  pl.BlockSpec(wbd.shape, const2),
                  pl.BlockSpec(wo.shape, const2),
                  pl.BlockSpec((1, d), const2)],
        out_specs=[pl.BlockSpec((1, tm, d), tok), pl.BlockSpec((1, tm, d), tok)],
        out_shape=[jax.ShapeDtypeStruct((b, t, d), F32), jax.ShapeDtypeStruct((b, t, d), BF16)],
        compiler_params=_cparams(2),
        name="merge",
    )(x, mod3, y_attn, o_df, o_db, z, gates, dng, havg, wba, wbd, wo, gn2)


def _mlp_kernel(out1_ref, hm_ref, mod_ref, w1_ref, w2_ref, o_ref, *, ff_chunk):
    hm = hm_ref[0]
    acc = None
    for j in range(D_FF // ff_chunk):
        a = jnp.maximum(_dot(hm, w1_ref[:, j * ff_chunk:(j + 1) * ff_chunk]), 0.0)
        part = _dot((a * a).astype(BF16), w2_ref[j * ff_chunk:(j + 1) * ff_chunk, :])
        acc = part if acc is None else acc + part
    o_ref[0] = out1_ref[0] + mod_ref[0][5:6] * acc


def _mlp(out1, hm, mod3, w1, w2, tm):
    b, t, d = out1.shape
    tok = lambda bi, i: (bi, i, 0)
    const2 = lambda bi, i: (0, 0)
    return pl.pallas_call(
        functools.partial(_mlp_kernel, ff_chunk=1024),
        grid=(b, t // tm),
        in_specs=[pl.BlockSpec((1, tm, d), tok),
                  pl.BlockSpec((1, tm, d), tok),
                  pl.BlockSpec((1, 6, d), lambda bi, i: (bi, 0, 0)),
                  pl.BlockSpec(w1.shape, const2),
                  pl.BlockSpec(w2.shape, const2)],
        out_specs=pl.BlockSpec((1, tm, d), tok),
        out_shape=jax.ShapeDtypeStruct((b, t, d), F32),
        compiler_params=_cparams(2),
        name="mlp",
    )(out1, hm, mod3, w1, w2)


def _head_avg(n, scale):
    idx = np.arange(n) // HEAD_DIM
    return jnp.asarray((idx[:, None] == idx[None, :]).astype(np.float32) * scale, BF16)


def _kv_tile_matrix():
    m = np.zeros((KV_WIDTH, ATTN_KV_HEADS * MXU_TILE), np.float32)
    for g in range(ATTN_KV_HEADS):
        for r in range(ATTN_GROUP):
            for dd in range(HEAD_DIM):
                m[g * HEAD_DIM + dd, g * MXU_TILE + r * HEAD_DIM + dd] = 1.0
    return jnp.asarray(m, BF16)


def _dn_expand_matrix():
    n = N_DIR * DN_HEADS
    m = np.zeros((N_DIR, 4 * n, 2 * DN_WIDTH), np.float32)
    for d in range(N_DIR):
        for part in range(2):
            for h in range(DN_HEADS):
                idx = d * DN_HEADS + h
                m[d, part * 2 * n + idx, h * HEAD_DIM:(h + 1) * HEAD_DIM] = 1.0
                m[d, part * 2 * n + n + idx, DN_WIDTH + h * HEAD_DIM:DN_WIDTH + (h + 1) * HEAD_DIM] = 1.0
    return jnp.asarray(m, BF16)


def _tri_matrices():
    i = np.arange(CHUNK)
    low = (i[:, None] >= i[None, :]).astype(np.float32)
    up = (i[:, None] <= i[None, :]).astype(np.float32)
    return jnp.asarray(np.stack([np.concatenate([low, low], axis=1), np.concatenate([up, up], axis=1)]), BF16)


def _rope_tables(seq):
    half = HEAD_DIM // 2
    n_freq = half // 2
    freqs = ROPE_BASE ** (-jnp.arange(n_freq, dtype=F32) / n_freq)
    pos = jnp.arange(seq)
    ang_r = (pos // GRID_W).astype(F32)[:, None] * freqs
    ang_c = (pos % GRID_W).astype(F32)[:, None] * freqs
    cos = jnp.concatenate([jnp.cos(ang_r)] * 2 + [jnp.cos(ang_c)] * 2, axis=1)
    sin = jnp.concatenate([-jnp.sin(ang_r), jnp.sin(ang_r), -jnp.sin(ang_c), jnp.sin(ang_c)], axis=1)
    reps = LANES // HEAD_DIM
    return jnp.tile(cos, (1, reps)), jnp.tile(sin, (1, reps))


def _pad_cols(w, n):
    return jnp.pad(w, ((0, 0), (0, n - w.shape[1])))


def kernel(x, c, ctx, c_ctx, w_ada, b_ada, g_norm1, w_in, q_norm_g, k_norm_g, attn_sink, conv_w, a_log, dt_bias,
           dn_norm_g, w_br_attn, w_br_dn, w_out, g_norm2, w_mlp1, w_mlp2):
    depth = w_ada.shape[0]
    assert depth == 1, "single-layer trunk only"
    b, s, d = x.shape
    n_ctx = ctx.shape[1]
    assert d == D_MODEL and w_in.shape[-1] == _IN_WIDTH
    assert s >= 3 * ATTN_BLOCK and s % ATTN_BLOCK == 0 and s % CHUNK == 0 and n_ctx % CHUNK == 0
    out_dtype = x.dtype
    w_in0 = w_in[0]

    mod_rows = 16
    cc = jnp.concatenate([c.astype(F32), c_ctx.astype(F32)[None], jnp.zeros((mod_rows - b - 1, d), F32)], axis=0)
    mod = _ada(cc, w_ada[0], b_ada[0])
    mod3 = mod.reshape(mod_rows, 6, d)

    ab_cols = jnp.concatenate([w_in0[:, _OFF_DA:_OFF_GA]] * 2, axis=1)
    w_lat = jnp.concatenate([w_in0[:, :_OFF_DA], w_in0[:, _OFF_GA:], _pad_cols(ab_cols, LANES)], axis=1).astype(BF16)
    segs_lat = ((0, _OFF_DQ), (_OFF_DQ, 3 * DN_WIDTH), (_OFF_DZ, DN_WIDTH), (_OFF_DA, 2 * D_MODEL),
                (_OFF_DA + 2 * D_MODEL, LANES))
    a_lat, d_lat, z_lat, gates, ab_lat = _inproj(x, mod3, None, g_norm1[0], w_lat, segs_lat,
                                                 (BF16, BF16, BF16, BF16, F32), tm=512)
    w_ctx = jnp.concatenate([w_in0[:, _OFF_AK:_OFF_DQ], w_in0[:, _OFF_DK:_OFF_DZ], _pad_cols(ab_cols, LANES)],
                            axis=1).astype(BF16)
    segs_ctx = ((0, 2 * KV_WIDTH), (2 * KV_WIDTH, 2 * DN_WIDTH), (2 * KV_WIDTH + 2 * DN_WIDTH, LANES))
    kv_ctx, d_ctx, ab_ctx = _inproj(ctx, mod3, b, g_norm1[0], w_ctx, segs_ctx, (BF16, BF16, F32), tm=n_ctx)

    hsum = _head_avg(MXU_TILE, 1.0)
    q_d, k_d, v_d = _dnpre(d_lat, conv_w[0], hsum, (True, True, False), (True, False, False), tm=512)
    k_dc, v_dc = _dnpre(d_ctx, conv_w[0][:, DN_WIDTH:], hsum, (True, False), (False, False), tm=n_ctx)

    cos, sin = _rope_tables(s)
    reps = LANES // HEAD_DIM
    y_attn = _attention(a_lat, kv_ctx, cos, sin,
                        jnp.tile(q_norm_g[0].astype(F32), reps)[None], jnp.tile(k_norm_g[0].astype(F32), reps)[None],
                        jnp.broadcast_to(attn_sink[0].astype(F32)[:, None], (ATTN_Q_HEADS, LANES)),
                        _head_avg(LANES, 1.0 / HEAD_DIM), _kv_tile_matrix())

    n_gate = N_DIR * DN_HEADS
    arow = _pad_cols(jnp.tile(jnp.concatenate([jnp.exp(a_log[0]).reshape(1, n_gate), jnp.zeros((1, n_gate), F32)],
                                              axis=1), (1, 2)), LANES)
    dtrow = _pad_cols(jnp.tile(jnp.concatenate([dt_bias[0].reshape(1, n_gate), jnp.zeros((1, n_gate), F32)],
                                               axis=1), (1, 2)), LANES)
    exp2, tri = _dn_expand_matrix(), _tri_matrices()
    prep_ctx = _dnprep(None, k_dc, v_dc, ab_ctx, arow, dtrow, exp2, tri, cb=2)
    prep_lat = _dnprep(q_d, k_d, v_d, ab_lat, arow, dtrow, exp2, tri, cb=2)
    (s_ctx,) = _dnscan(prep_ctx, None, tb=n_ctx, want_s=True)
    o_df, o_db = _dnscan(prep_lat, s_ctx, tb=512, want_s=False)

    out1, hm = _merge(x, mod3, y_attn, o_df, o_db, z_lat, gates,
                      jnp.tile(dn_norm_g[0].astype(F32), HEADS_PER_TILE)[None], _head_avg(MXU_TILE, 1.0 / HEAD_DIM),
                      w_br_attn[0].astype(BF16), w_br_dn[0].astype(BF16), w_out[0].astype(BF16),
                      g_norm2[0].reshape(1, d), tm=256)
    out = _mlp(out1, hm, mod3, w_mlp1[0].astype(BF16), w_mlp2[0].astype(BF16), tm=256)
    return out.astype(out_dtype)
```

```python
import functools

import numpy as np
import jax
import jax.numpy as jnp
from jax import lax
from jax.experimental import pallas as pl
from jax.experimental.pallas import tpu as pltpu

F32 = jnp.float32
BF16 = jnp.bfloat16

D_MODEL = 1024
GRID_W = 64
HEAD_DIM = 64
ATTN_Q_HEADS = 8
ATTN_KV_HEADS = 2
ATTN_GROUP = ATTN_Q_HEADS // ATTN_KV_HEADS
WINDOW = 128
ATTN_BLOCK = 128
ROPE_BASE = 10000.0
DN_HEADS = 8
CONV_W = 3
CHUNK = 64
N_DIR = 2
D_FF = 4 * D_MODEL
EPS = 1e-6
NEG_INF = -1e30

ATTN_WIDTH = ATTN_Q_HEADS * HEAD_DIM
KV_WIDTH = ATTN_KV_HEADS * HEAD_DIM
DN_WIDTH = DN_HEADS * HEAD_DIM
LANES = 128
MXU_TILE = 256
HEADS_PER_TILE = MXU_TILE // HEAD_DIM
DN_SUB = 16
VMEM_LIMIT = 56 * 1024 * 1024

_OFF_AQ = 0
_OFF_AK = _OFF_AQ + ATTN_WIDTH
_OFF_AV = _OFF_AK + KV_WIDTH
_OFF_DQ = _OFF_AV + KV_WIDTH
_OFF_DK = _OFF_DQ + DN_WIDTH
_OFF_DV = _OFF_DK + DN_WIDTH
_OFF_DZ = _OFF_DV + DN_WIDTH
_OFF_DA = _OFF_DZ + DN_WIDTH
_OFF_DB = _OFF_DA + N_DIR * DN_HEADS
_OFF_GA = _OFF_DB + N_DIR * DN_HEADS
_OFF_GD = _OFF_GA + D_MODEL
_IN_WIDTH = _OFF_GD + D_MODEL


def _sigmoid(x):
    return 0.5 * jnp.tanh(0.5 * x) + 0.5


def _dot(a, b):
    return jnp.dot(a, b, preferred_element_type=F32)


def _dot_nt(a, b):
    return lax.dot_general(a, b, (((1,), (1,)), ((), ())), preferred_element_type=F32)


def _dot_tn(a, b):
    return lax.dot_general(a, b, (((0,), (0,)), ((), ())), preferred_element_type=F32)


def _cparams(n_axes):
    return pltpu.CompilerParams(dimension_semantics=("arbitrary",) * n_axes, vmem_limit_bytes=VMEM_LIMIT)


def _ada_kernel(c_ref, w_ref, b_ref, o_ref):
    c = c_ref[...]
    s = c * _sigmoid(c)
    o_ref[...] = _dot(s.astype(BF16), w_ref[...].astype(BF16)) + b_ref[...]


def _ada(cc, w_ada, b_ada):
    rows, d = cc.shape
    n = w_ada.shape[1]
    tn = 1536
    return pl.pallas_call(
        _ada_kernel,
        grid=(n // tn,),
        in_specs=[pl.BlockSpec((rows, d), lambda j: (0, 0)),
                  pl.BlockSpec((d, tn), lambda j: (0, j)),
                  pl.BlockSpec((1, tn), lambda j: (0, j))],
        out_specs=pl.BlockSpec((rows, tn), lambda j: (0, j)),
        out_shape=jax.ShapeDtypeStruct((rows, n), F32),
        compiler_params=_cparams(1),
        name="ada",
    )(cc, w_ada, b_ada.reshape(1, n))


def _inproj_kernel(x_ref, mod_ref, g_ref, w_ref, *out_refs, segs):
    x = x_ref[0]
    ms = jnp.mean(x * x, axis=-1, keepdims=True)
    mod = mod_ref[0]
    scale = g_ref[...] * (1.0 + mod[1:2])
    h = (x * lax.rsqrt(ms + EPS) * scale + mod[0:1]).astype(BF16)
    for o_ref, (start, size) in zip(out_refs, segs):
        o_ref[0] = _dot(h, w_ref[:, start:start + size]).astype(o_ref.dtype)


def _inproj(x, mod3, mod_row, g_norm, w, segs, dtypes, tm):
    b, t, d = x.shape
    n = w.shape[1]
    if mod_row is None:
        mod_map = lambda bi, i: (bi, 0, 0)
    else:
        mod_map = lambda bi, i: (mod_row, 0, 0)
    out_shape = [jax.ShapeDtypeStruct((b, t, size), dt) for (_, size), dt in zip(segs, dtypes)]
    out_specs = [pl.BlockSpec((1, tm, size), lambda bi, i: (bi, i, 0)) for (_, size) in segs]
    return pl.pallas_call(
        functools.partial(_inproj_kernel, segs=segs),
        grid=(b, t // tm),
        in_specs=[pl.BlockSpec((1, tm, d), lambda bi, i: (bi, i, 0)),
                  pl.BlockSpec((1, 6, d), mod_map),
                  pl.BlockSpec((1, d), lambda bi, i: (0, 0)),
                  pl.BlockSpec((d, n), lambda bi, i: (0, 0))],
        out_specs=out_specs,
        out_shape=out_shape,
        compiler_params=_cparams(2),
        name="inproj",
    )(x, mod3, g_norm.reshape(1, d), w)


def _dnpre_kernel(x_ref, prev_ref, next_ref, cw_ref, hsum_ref, *out_refs, tm, halo, norm_flags, q_flags):
    i = pl.program_id(1)
    last = pl.num_programs(1) - 1
    x = x_ref[0].astype(F32)
    prev_row = prev_ref[0][halo - 1:halo, :].astype(F32) * (i > 0).astype(F32)
    next_row = next_ref[0][0:1, :].astype(F32) * (i < last).astype(F32)
    rows = lax.broadcasted_iota(jnp.int32, x.shape, 0)
    xm = jnp.where(rows == 0, prev_row, pltpu.roll(x, 1, 0))
    xp = jnp.where(rows == tm - 1, next_row, pltpu.roll(x, tm - 1, 0))
    cw = cw_ref[...]
    y = xm * cw[0:1] + x * cw[1:2] + xp * cw[2:3]
    y = y * _sigmoid(y)
    hsum = hsum_ref[...]
    for s, (o_ref, do_norm, is_q) in enumerate(zip(out_refs, norm_flags, q_flags)):
        for j in range(DN_WIDTH // MXU_TILE):
            lo = s * DN_WIDTH + j * MXU_TILE
            ys = y[:, lo:lo + MXU_TILE]
            if do_norm:
                ss = _dot((ys * ys).astype(BF16), hsum)
                ys = ys * lax.rsqrt(ss + EPS)
                if is_q:
                    ys = ys * (HEAD_DIM ** -0.5)
            o_ref[0, :, j * MXU_TILE:(j + 1) * MXU_TILE] = ys.astype(o_ref.dtype)


def _dnpre(xd, conv_w, hsum, norm_flags, q_flags, tm):
    b, t, c = xd.shape
    halo = 16
    nblk = t // halo
    r = tm // halo
    n_out = c // DN_WIDTH
    return pl.pallas_call(
        functools.partial(_dnpre_kernel, tm=tm, halo=halo, norm_flags=norm_flags, q_flags=q_flags),
        grid=(b, t // tm),
        in_specs=[pl.BlockSpec((1, tm, c), lambda bi, i: (bi, i, 0)),
                  pl.BlockSpec((1, halo, c), lambda bi, i: (bi, jnp.maximum(i * r - 1, 0), 0)),
                  pl.BlockSpec((1, halo, c), lambda bi, i: (bi, jnp.minimum((i + 1) * r, nblk - 1), 0)),
                  pl.BlockSpec((CONV_W, c), lambda bi, i: (0, 0)),
                  pl.BlockSpec((MXU_TILE, MXU_TILE), lambda bi, i: (0, 0))],
        out_specs=[pl.BlockSpec((1, tm, DN_WIDTH), lambda bi, i: (bi, i, 0))] * n_out,
        out_shape=[jax.ShapeDtypeStruct((b, t, DN_WIDTH), BF16)] * n_out,
        compiler_params=_cparams(2),
        name="dnpre",
    )(xd, xd, xd, conv_w, hsum)


def _rope(x, cos, sin, lane):
    swapped = jnp.where((lane % 32) < 16, pltpu.roll(x, LANES - 16, 1), pltpu.roll(x, 16, 1))
    return x * cos + swapped * sin


def _attn_kernel(a_ref, kvc_ref, cos_ref, sin_ref, qg_ref, kg_ref, sink_ref, havg_ref, tile_ref,
                 o_ref, kt_s, v4_s, *, seq, ctx):
    n = pl.program_id(1)
    havg = havg_ref[...]
    tile = tile_ref[...]
    prep_rows = 256
    ctx_blocks = ctx // ATTN_BLOCK

    @pl.when(n == 0)
    def _prep():
        kg = kg_ref[...]
        kc = kvc_ref[0, :, 0:KV_WIDTH].astype(F32)
        ms = _dot((kc * kc).astype(BF16), havg)
        kc = kc * lax.rsqrt(ms + EPS) * kg
        for j in range(ctx_blocks):
            kt_s[j] = kc[j * ATTN_BLOCK:(j + 1) * ATTN_BLOCK, :].T.astype(BF16)
        v4_s[0:ctx, :] = _dot(kvc_ref[0, :, KV_WIDTH:2 * KV_WIDTH], tile).astype(BF16)
        lane = lax.broadcasted_iota(jnp.int32, (prep_rows, LANES), 1)
        for r0 in range(0, seq, prep_rows):
            kx = a_ref[0, r0:r0 + prep_rows, _OFF_AK:_OFF_AK + KV_WIDTH].astype(F32)
            ms = _dot((kx * kx).astype(BF16), havg)
            kx = kx * lax.rsqrt(ms + EPS) * kg
            kx = _rope(kx, cos_ref[r0:r0 + prep_rows, :], sin_ref[r0:r0 + prep_rows, :], lane)
            for j in range(prep_rows // ATTN_BLOCK):
                kt_s[ctx_blocks + r0 // ATTN_BLOCK + j] = kx[j * ATTN_BLOCK:(j + 1) * ATTN_BLOCK, :].T.astype(BF16)
            vx = a_ref[0, r0:r0 + prep_rows, _OFF_AV:_OFF_AV + KV_WIDTH]
            v4_s[ctx + r0:ctx + r0 + prep_rows, :] = _dot(vx, tile).astype(BF16)

    q0 = pl.multiple_of(n * ATTN_BLOCK, ATTN_BLOCK)
    lane = lax.broadcasted_iota(jnp.int32, (ATTN_BLOCK, LANES), 1)
    cos = cos_ref[pl.ds(q0, ATTN_BLOCK), :]
    sin = sin_ref[pl.ds(q0, ATTN_BLOCK), :]
    qg = qg_ref[...]
    q_parts = []
    for j in range(ATTN_WIDTH // LANES):
        qx = a_ref[0, pl.ds(q0, ATTN_BLOCK), j * LANES:(j + 1) * LANES].astype(F32)
        ms = _dot((qx * qx).astype(BF16), havg)
        qx = qx * lax.rsqrt(ms + EPS) * qg
        q_parts.append(_rope(qx, cos, sin, lane) * (HEAD_DIM ** -0.5))

    n_band = 3
    n_win = n_band * ATTN_BLOCK
    n_keys = n_win + ctx
    blk0 = jnp.clip(n - 1, 0, seq // ATTN_BLOCK - n_band)
    start = pl.multiple_of(blk0 * ATTN_BLOCK, ATTN_BLOCK)
    qpos = q0 + lax.broadcasted_iota(jnp.int32, (ATTN_BLOCK, n_keys), 0)
    kcol = lax.broadcasted_iota(jnp.int32, (ATTN_BLOCK, n_keys), 1)
    valid = (jnp.abs(qpos - (start + kcol)) <= WINDOW) | (kcol >= n_win)
    head_of_lane = lax.broadcasted_iota(jnp.int32, (1, MXU_TILE), 1) // HEAD_DIM
    kt_loc = kt_s[pl.ds(ctx_blocks + blk0, n_band)]
    kt_all = jnp.concatenate([kt_loc[j] for j in range(n_band)] + [kt_s[j] for j in range(ctx_blocks)], axis=1)

    v_alls = [jnp.concatenate([v4_s[pl.ds(ctx + start, n_win), g * MXU_TILE:(g + 1) * MXU_TILE],
                               v4_s[0:ctx, g * MXU_TILE:(g + 1) * MXU_TILE]], axis=0) for g in range(ATTN_KV_HEADS)]
    heads = range(ATTN_Q_HEADS)
    qhs = []
    for h in heads:
        g = h // ATTN_GROUP
        qx = q_parts[h // 2]
        if h % 2 != g:
            qx = pltpu.roll(qx, HEAD_DIM, 1)
        qhs.append(jnp.where(lane // HEAD_DIM == g, qx, 0.0).astype(BF16))
    ss = [jnp.where(valid, _dot(qh, kt_all), NEG_INF) for qh in qhs]
    sinks = [sink_ref[h:h + 1, 0:1] for h in heads]
    maxes = [jnp.maximum(jnp.max(s, axis=-1, keepdims=True), sink) for s, sink in zip(ss, sinks)]
    ps = [jnp.exp(s - m) for s, m in zip(ss, maxes)]
    invs = [1.0 / (jnp.sum(p, axis=-1, keepdims=True) + jnp.exp(sink - m)) for p, sink, m in zip(ps, sinks, maxes)]
    o_fulls = [_dot(p.astype(BF16), v_alls[h // ATTN_GROUP]) for h, p in zip(heads, ps)]
    outs = []
    for g in range(ATTN_KV_HEADS):
        acc = None
        for r in range(ATTN_GROUP):
            h = g * ATTN_GROUP + r
            term = jnp.where(head_of_lane == r, o_fulls[h] * invs[h], 0.0)
            acc = term if acc is None else acc + term
        outs.append(acc)
    o_ref[0] = jnp.concatenate(outs, axis=1).astype(o_ref.dtype)


def _attention(a_lat, kv_ctx, cos, sin, qg, kg, sink, havg, tile):
    b, s, wa = a_lat.shape
    ctx = kv_ctx.shape[1]
    nb = s // ATTN_BLOCK
    const2 = lambda bi, n: (0, 0)
    return pl.pallas_call(
        functools.partial(_attn_kernel, seq=s, ctx=ctx),
        grid=(b, nb),
        in_specs=[pl.BlockSpec((1, s, wa), lambda bi, n: (bi, 0, 0)),
                  pl.BlockSpec((1, ctx, 2 * KV_WIDTH), lambda bi, n: (bi, 0, 0)),
                  pl.BlockSpec((s, LANES), const2),
                  pl.BlockSpec((s, LANES), const2),
                  pl.BlockSpec((1, LANES), const2),
                  pl.BlockSpec((1, LANES), const2),
                  pl.BlockSpec((ATTN_Q_HEADS, LANES), const2),
                  pl.BlockSpec((LANES, LANES), const2),
                  pl.BlockSpec((LANES, ATTN_KV_HEADS * MXU_TILE), const2)],
        out_specs=pl.BlockSpec((1, ATTN_BLOCK, ATTN_WIDTH), lambda bi, n: (bi, n, 0)),
        out_shape=jax.ShapeDtypeStruct((b, s, ATTN_WIDTH), BF16),
        scratch_shapes=[pltpu.VMEM(((ctx + s) // ATTN_BLOCK, KV_WIDTH, ATTN_BLOCK), BF16),
                        pltpu.VMEM((ctx + s, ATTN_KV_HEADS * MXU_TILE), BF16)],
        compiler_params=_cparams(2),
        name="attn",
    )(a_lat, kv_ctx, cos, sin, qg, kg, sink, havg, tile)


def _bdiag(x_lane, bmask):
    return jnp.where(bmask, jnp.concatenate([x_lane] * HEADS_PER_TILE, axis=0), jnp.zeros((), x_lane.dtype))


def _unit_tri_inverses(a_list, eye_l, sub_mask, bmask):
    ads = [jnp.where(sub_mask, a, 0.0) for a in a_list]
    aos = [jnp.where(sub_mask, 0.0, a).astype(BF16) for a in a_list]
    ps = [eye_l - ad for ad in ads]
    pws = [ad.astype(BF16) for ad in ads]
    n_levels = int(np.log2(DN_SUB))
    for level in range(n_levels):
        rhss = [_bdiag(pw, bmask) for pw in pws]
        if level == 0:
            pws = [_dot(pw, rhs).astype(BF16) for pw, rhs in zip(pws, rhss)]
        elif level < n_levels - 1:
            boths = [_dot(jnp.concatenate([pw, p.astype(BF16)], axis=0), rhs) for pw, p, rhs in zip(pws, ps, rhss)]
            pws = [both[0:CHUNK].astype(BF16) for both in boths]
            ps = [p + both[CHUNK:2 * CHUNK] for p, both in zip(ps, boths)]
        else:
            ps = [p + _dot(p.astype(BF16), rhs) for p, rhs in zip(ps, rhss)]
    xs = [p.astype(BF16) for p in ps]
    bs = [_dot(x, _bdiag(ao, bmask)) for x, ao in zip(xs, aos)]
    b_rhss = [_bdiag(b.astype(BF16), bmask) for b in bs]
    b2s = [_dot(b.astype(BF16), rhs) for b, rhs in zip(bs, b_rhss)]
    b3s = [_dot(b2.astype(BF16), rhs) for b2, rhs in zip(b2s, b_rhss)]
    assert CHUNK // DN_SUB == 4
    qs = [(eye_l - b + b2 - b3).astype(BF16) for b, b2, b3 in zip(bs, b2s, b3s)]
    return [_dot(q, _bdiag(x, bmask)) for q, x in zip(qs, xs)]


def _dn_factors(ab, d, arow, dtrow, exp2_ref, tri_ref, eye_t):
    lane = lax.broadcasted_iota(jnp.int32, ab.shape, 1)
    z = ab + dtrow
    softplus = jnp.maximum(z, 0.0) + jnp.log(1.0 + jnp.exp(-jnp.abs(z)))
    is_g = (lane % 32) < 16
    x = jnp.where(is_g, -arow * softplus, _sigmoid(ab))
    x_hi = x.astype(BF16)
    x_lo = (x - x_hi.astype(F32)).astype(BF16)
    xhl = jnp.where(lane < 32, x_hi, x_lo)[:, 0:64]
    y = _dot(xhl, exp2_ref[d])
    ge = y[:, 0:DN_WIDTH]
    be = y[:, DN_WIDTH:2 * DN_WIDTH]
    ge_hi = ge.astype(BF16)
    ge_lo = (ge - ge_hi.astype(F32)).astype(BF16)
    gi = _dot(tri_ref[d], jnp.concatenate([ge_hi, ge_lo], axis=0))
    gj = jnp.sum(gi * eye_t, axis=0, keepdims=True)
    return be, gi, jnp.broadcast_to(gj, gi.shape)


def _block_mask():
    return (lax.broadcasted_iota(jnp.int32, (MXU_TILE, MXU_TILE), 0) // HEAD_DIM
            == lax.broadcasted_iota(jnp.int32, (MXU_TILE, MXU_TILE), 1) // HEAD_DIM)


def _dnprep_kernel(*refs, cb, want_o):
    if want_o:
        q_ref, k_ref, v_ref, ab_ref = refs[:4]
        refs = refs[4:]
    else:
        q_ref = None
        k_ref, v_ref, ab_ref = refs[:3]
        refs = refs[3:]
    arow_ref, dtrow_ref, exp2_ref, tri_ref, w_o, uv_o, kt_o, dl_o = refs[:8]
    qd_o, in_o = refs[8:10] if want_o else (None, None)
    n_tiles = DN_WIDTH // MXU_TILE
    row = lax.broadcasted_iota(jnp.int32, (CHUNK, MXU_TILE), 0)
    colj = lax.broadcasted_iota(jnp.int32, (CHUNK, MXU_TILE), 1) % HEAD_DIM
    eye_l = (row == colj).astype(F32)
    eye_t = jnp.concatenate([eye_l] * n_tiles, axis=1)
    bmask = _block_mask()
    arow = arow_ref[...]
    dtrow = dtrow_ref[...]

    fac = {}
    shared = {}
    for c in range(cb):
        rows = slice(c * CHUNK, (c + 1) * CHUNK)
        ab = ab_ref[0, rows, :]
        for d in range(N_DIR):
            fac[(c, d)] = _dn_factors(ab, d, arow, dtrow, exp2_ref, tri_ref, eye_t)
        for g in range(n_tiles):
            lanes = slice(g * MXU_TILE, (g + 1) * MXU_TILE)
            k_l = k_ref[0, rows, lanes]
            kbd = _bdiag(k_l, bmask)
            if want_o:
                kq = _dot_nt(jnp.concatenate([k_l, q_ref[0, rows, lanes]], axis=0), kbd)
                shared[(c, g)] = (kq[0:CHUNK], kq[CHUNK:2 * CHUNK])
            else:
                shared[(c, g)] = (_dot_nt(k_l, kbd), None)

    units = [(c, g, d) for c in range(cb) for g in range(n_tiles) for d in range(N_DIR)]
    decs = []
    for c, g, d in units:
        lanes = slice(g * MXU_TILE, (g + 1) * MXU_TILE)
        be, gi, gj = fac[(c, d)]
        lower = (row > colj) if d == 0 else (row < colj)
        decs.append(jnp.where(lower, jnp.exp(jnp.where(lower, gi[:, lanes] - gj[:, lanes], 0.0)), 0.0))
    a_list = [fac[(c, d)][0][:, g * MXU_TILE:(g + 1) * MXU_TILE] * dec * shared[(c, g)][0]
              for (c, g, d), dec in zip(units, decs)]
    tinvs = _unit_tri_inverses(a_list, eye_l, (row // DN_SUB) == (colj // DN_SUB), bmask)

    for (c, g, d), dec, tinv in zip(units, decs, tinvs):
        rows = slice(c * CHUNK, (c + 1) * CHUNK)
        lanes = slice(g * MXU_TILE, (g + 1) * MXU_TILE)
        be, gi, _ = fac[(c, d)]
        be, gi = be[:, lanes], gi[:, lanes]
        last = CHUNK - 1 if d == 0 else 0
        e_g = jnp.exp(gi)
        gl_row = gi[last:last + 1, :]
        kf = k_ref[0, rows, lanes].astype(F32)
        vf = v_ref[0, rows, lanes].astype(F32)
        rhs = jnp.concatenate([_bdiag((be * e_g * kf).astype(BF16), bmask), _bdiag((be * vf).astype(BF16), bmask)],
                              axis=1)
        wu = _dot(tinv.astype(BF16), rhs)
        w_o[0, d, rows, lanes] = wu[:, 0:MXU_TILE].astype(BF16)
        uv_o[0, d, rows, lanes] = wu[:, MXU_TILE:2 * MXU_TILE]
        kt_o[0, d, rows, lanes] = (jnp.exp(gl_row - gi) * kf).astype(BF16)
        dl_o[0, d, c, :, lanes] = jnp.exp(gl_row)
        if want_o:
            qd_o[0, d, rows, lanes] = (e_g * q_ref[0, rows, lanes].astype(F32)).astype(BF16)
            in_o[0, d, rows, lanes] = ((dec + eye_l) * shared[(c, g)][1]).astype(BF16)


def _dnprep(q, k, v, ab, arow, dtrow, exp2, tri, cb):
    want_o = q is not None
    b, t, w = k.shape
    tb = cb * CHUNK
    tok = lambda bi, i: (bi, i, 0)
    const2 = lambda bi, i: (0, 0)
    const3 = lambda bi, i: (0, 0, 0)
    dir_tok = lambda bi, i: (bi, 0, i, 0)
    data = ([q] if want_o else []) + [k, v]
    in_specs = ([pl.BlockSpec((1, tb, w), tok)] * len(data) + [pl.BlockSpec((1, tb, LANES), tok),
                pl.BlockSpec((1, LANES), const2), pl.BlockSpec((1, LANES), const2),
                pl.BlockSpec(exp2.shape, const3), pl.BlockSpec(tri.shape, const3)])
    big = lambda dt: jax.ShapeDtypeStruct((b, N_DIR, t, w), dt)
    big_spec = pl.BlockSpec((1, N_DIR, tb, w), dir_tok)
    out_shape = [big(BF16), big(F32), big(BF16), jax.ShapeDtypeStruct((b, N_DIR, t // CHUNK, 1, w), F32)]
    out_specs = [big_spec, big_spec, big_spec, pl.BlockSpec((1, N_DIR, cb, 1, w), lambda bi, i: (bi, 0, i, 0, 0))]
    if want_o:
        out_shape += [big(BF16), big(BF16)]
        out_specs += [big_spec, big_spec]
    return pl.pallas_call(
        functools.partial(_dnprep_kernel, cb=cb, want_o=want_o),
        grid=(b, t // tb),
        in_specs=in_specs,
        out_specs=out_specs,
        out_shape=out_shape,
        compiler_params=_cparams(2),
        name="dnprep",
    )(*data, ab, arow, dtrow, exp2, tri)


def _dnscan_kernel(*refs, n_chunk, want_o, have_s0, want_s):
    n_in = 6 if want_o else 4
    dir_refs = [refs[0:n_in], refs[n_in:2 * n_in]]
    pos = 2 * n_in
    s0_ref = refs[pos] if have_s0 else None
    pos += int(have_s0)
    o_refs = refs[pos:pos + N_DIR] if want_o else None
    pos += N_DIR if want_o else 0
    sout_ref = refs[pos] if want_s else None
    pos += int(want_s)
    s_scr = refs[pos]
    n_tiles = DN_WIDTH // MXU_TILE
    i = pl.program_id(1)

    @pl.when(i == 0)
    def _init():
        if have_s0:
            s_scr[...] = s0_ref[0]
        else:
            s_scr[...] = jnp.zeros_like(s_scr)

    bmask = _block_mask()
    chains = [(d, g) for d in range(N_DIR) for g in range(n_tiles)]

    def body(j, carry):
        cidx = (j, n_chunk - 1 - j)
        r0s = [pl.multiple_of(cidx[d] * CHUNK, CHUNK) for d in range(N_DIR)]
        s_olds, r1s = [], []
        for d, g in chains:
            lanes = slice(g * MXU_TILE, (g + 1) * MXU_TILE)
            w = dir_refs[d][0][0, 0, pl.ds(r0s[d], CHUNK), lanes]
            if want_o:
                w = jnp.concatenate([w, dir_refs[d][4][0, 0, pl.ds(r0s[d], CHUNK), lanes]], axis=0)
            s_old = s_scr[d * n_tiles + g]
            s_olds.append(s_old)
            r1s.append(_dot(w, s_old.astype(BF16)))
        u_bfs = []
        for (d, g), r1 in zip(chains, r1s):
            lanes = slice(g * MXU_TILE, (g + 1) * MXU_TILE)
            u_bfs.append((dir_refs[d][1][0, 0, pl.ds(r0s[d], CHUNK), lanes] - r1[0:CHUNK]).astype(BF16))
        for (d, g), r1, u_bf, s_old in zip(chains, r1s, u_bfs, s_olds):
            lanes = slice(g * MXU_TILE, (g + 1) * MXU_TILE)
            kt = dir_refs[d][2][0, 0, pl.ds(r0s[d], CHUNK), lanes]
            ds = jnp.where(bmask, _dot_tn(kt, u_bf), 0.0)
            dl = dir_refs[d][3][0, 0, cidx[d]][:, lanes]
            s_scr[d * n_tiles + g] = s_old * dl + ds
            if want_o:
                intra = dir_refs[d][5][0, 0, pl.ds(r0s[d], CHUNK), lanes]
                o_refs[d][0, pl.ds(r0s[d], CHUNK), lanes] = r1[CHUNK:2 * CHUNK] + _dot(intra, _bdiag(u_bf, bmask))
        return carry

    lax.fori_loop(0, n_chunk, body, 0)

    if want_s:
        @pl.when(i == pl.num_programs(1) - 1)
        def _fin():
            sout_ref[0] = s_scr[...]


def _dnscan(prep, s0, tb, want_s):
    want_o = len(prep) == 6
    b, _, t, w = prep[0].shape
    n_t = t // tb
    n_chunk = tb // CHUNK
    n_chain = N_DIR * (w // MXU_TILE)

    def specs(d):
        blk = (lambda bi, i: i) if d == 0 else (lambda bi, i: n_t - 1 - i)
        big = pl.BlockSpec((1, 1, tb, w), lambda bi, i: (bi, d, blk(bi, i), 0))
        dl = pl.BlockSpec((1, 1, n_chunk, 1, w), lambda bi, i: (bi, d, blk(bi, i), 0, 0))
        return [big, big, big, dl] + ([big, big] if want_o else [])

    in_specs = specs(0) + specs(1)
    args = list(prep) + list(prep)
    state_spec = pl.BlockSpec((1, n_chain, MXU_TILE, MXU_TILE), lambda bi, i: (bi, 0, 0, 0))
    if s0 is not None:
        in_specs.append(state_spec)
        args.append(s0)
    out_shape, out_specs = [], []
    if want_o:
        out_shape += [jax.ShapeDtypeStruct((b, t, w), F32)] * N_DIR
        out_specs += [pl.BlockSpec((1, tb, w), lambda bi, i: (bi, i, 0)),
                      pl.BlockSpec((1, tb, w), lambda bi, i: (bi, n_t - 1 - i, 0))]
    if want_s:
        out_shape.append(jax.ShapeDtypeStruct((b, n_chain, MXU_TILE, MXU_TILE), F32))
        out_specs.append(state_spec)
    return pl.pallas_call(
        functools.partial(_dnscan_kernel, n_chunk=n_chunk, want_o=want_o, have_s0=s0 is not None, want_s=want_s),
        grid=(b, n_t),
        in_specs=in_specs,
        out_specs=out_specs,
        out_shape=out_shape,
        scratch_shapes=[pltpu.VMEM((n_chain, MXU_TILE, MXU_TILE), F32)],
        compiler_params=_cparams(2),
        name="dnscan",
    )(*args)


def _merge_kernel(x_ref, mod_ref, ya_ref, odf_ref, odb_ref, z_ref, gate_ref, dng_ref, havg_ref, wba_ref, wbd_ref,
                  wo_ref, gn2_ref, out1_ref, hm_ref):
    havg = havg_ref[...]
    dng = dng_ref[...]
    yd_parts = []
    for j in range(DN_WIDTH // MXU_TILE):
        sl = slice(j * MXU_TILE, (j + 1) * MXU_TILE)
        od = odf_ref[0, :, sl] + odb_ref[0, :, sl]
        ms = _dot((od * od).astype(BF16), havg)
        z = z_ref[0, :, sl].astype(F32)
        yd_parts.append((od * lax.rsqrt(ms + EPS) * dng * (z * _sigmoid(z))).astype(BF16))
    yd = jnp.concatenate(yd_parts, axis=1)
    ga = gate_ref[0, :, 0:D_MODEL].astype(F32)
    gd = gate_ref[0, :, D_MODEL:2 * D_MODEL].astype(F32)
    y = _sigmoid(ga) * _dot(ya_ref[0], wba_ref[...]) + _sigmoid(gd) * _dot(yd, wbd_ref[...])
    br = _dot(y.astype(BF16), wo_ref[...])
    mod = mod_ref[0]
    out1 = x_ref[0] + mod[2:3] * br
    out1_ref[0] = out1
    ms2 = jnp.mean(out1 * out1, axis=-1, keepdims=True)
    hm = out1 * lax.rsqrt(ms2 + EPS) * (gn2_ref[...] * (1.0 + mod[4:5])) + mod[3:4]
    hm_ref[0] = hm.astype(BF16)


def _merge(x, mod3, y_attn, o_df, o_db, z, gates, dng, havg, wba, wbd, wo, gn2, tm):
    b, t, d = x.shape
    tok = lambda bi, i: (bi, i, 0)
    const2 = lambda bi, i: (0, 0)
    return pl.pallas_call(
        _merge_kernel,
        grid=(b, t // tm),
        in_specs=[pl.BlockSpec((1, tm, d), tok),
                  pl.BlockSpec((1, 6, d), lambda bi, i: (bi, 0, 0)),
                  pl.BlockSpec((1, tm, ATTN_WIDTH), tok),
                  pl.BlockSpec((1, tm, DN_WIDTH), tok),
                  pl.BlockSpec((1, tm, DN_WIDTH), tok),
                  pl.BlockSpec((1, tm, DN_WIDTH), tok),
                  pl.BlockSpec((1, tm, 2 * d), tok),
                  pl.BlockSpec((1, MXU_TILE), const2),
                  pl.BlockSpec((MXU_TILE, MXU_TILE), const2),
                  pl.BlockSpec(wba.shape, const2),
                  pl.BlockSpec(wbd.shape, const2),
                  pl.BlockSpec(wo.shape, const2),
                  pl.BlockSpec((1, d), const2)],
        out_specs=[pl.BlockSpec((1, tm, d), tok), pl.BlockSpec((1, tm, d), tok)],
        out_shape=[jax.ShapeDtypeStruct((b, t, d), F32), jax.ShapeDtypeStruct((b, t, d), BF16)],
        compiler_params=_cparams(2),
        name="merge",
    )(x, mod3, y_attn, o_df, o_db, z, gates, dng, havg, wba, wbd, wo, gn2)


def _mlp_kernel(out1_ref, hm_ref, mod_ref, w1_ref, w2_ref, o_ref, *, ff_chunk):
    hm = hm_ref[0]
    acc = None
    for j in range(D_FF // ff_chunk):
        a = jnp.maximum(_dot(hm, w1_ref[:, j * ff_chunk:(j + 1) * ff_chunk]), 0.0)
        part = _dot((a * a).astype(BF16), w2_ref[j * ff_chunk:(j + 1) * ff_chunk, :])
        acc = part if acc is None else acc + part
    o_ref[0] = out1_ref[0] + mod_ref[0][5:6] * acc


def _mlp(out1, hm, mod3, w1, w2, tm):
    b, t, d = out1.shape
    tok = lambda bi, i: (bi, i, 0)
    const2 = lambda bi, i: (0, 0)
    return pl.pallas_call(
        functools.partial(_mlp_kernel, ff_chunk=1024),
        grid=(b, t // tm),
        in_specs=[pl.BlockSpec((1, tm, d), tok),
                  pl.BlockSpec((1, tm, d), tok),
                  pl.BlockSpec((1, 6, d), lambda bi, i: (bi, 0, 0)),
                  pl.BlockSpec(w1.shape, const2),
                  pl.BlockSpec(w2.shape, const2)],
        out_specs=pl.BlockSpec((1, tm, d), tok),
        out_shape=jax.ShapeDtypeStruct((b, t, d), F32),
        compiler_params=_cparams(2),
        name="mlp",
    )(out1, hm, mod3, w1, w2)


def _head_avg(n, scale):
    idx = np.arange(n) // HEAD_DIM
    return jnp.asarray((idx[:, None] == idx[None, :]).astype(np.float32) * scale, BF16)


def _kv_tile_matrix():
    m = np.zeros((KV_WIDTH, ATTN_KV_HEADS * MXU_TILE), np.float32)
    for g in range(ATTN_KV_HEADS):
        for r in range(ATTN_GROUP):
            for dd in range(HEAD_DIM):
                m[g * HEAD_DIM + dd, g * MXU_TILE + r * HEAD_DIM + dd] = 1.0
    return jnp.asarray(m, BF16)


def _dn_expand_matrix():
    n = N_DIR * DN_HEADS
    m = np.zeros((N_DIR, 4 * n, 2 * DN_WIDTH), np.float32)
    for d in range(N_DIR):
        for part in range(2):
            for h in range(DN_HEADS):
                idx = d * DN_HEADS + h
                m[d, part * 2 * n + idx, h * HEAD_DIM:(h + 1) * HEAD_DIM] = 1.0
                m[d, part * 2 * n + n + idx, DN_WIDTH + h * HEAD_DIM:DN_WIDTH + (h + 1) * HEAD_DIM] = 1.0
    return jnp.asarray(m, BF16)


def _tri_matrices():
    i = np.arange(CHUNK)
    low = (i[:, None] >= i[None, :]).astype(np.float32)
    up = (i[:, None] <= i[None, :]).astype(np.float32)
    return jnp.asarray(np.stack([np.concatenate([low, low], axis=1), np.concatenate([up, up], axis=1)]), BF16)


def _rope_tables(seq):
    half = HEAD_DIM // 2
    n_freq = half // 2
    freqs = ROPE_BASE ** (-jnp.arange(n_freq, dtype=F32) / n_freq)
    pos = jnp.arange(seq)
    ang_r = (pos // GRID_W).astype(F32)[:, None] * freqs
    ang_c = (pos % GRID_W).astype(F32)[:, None] * freqs
    cos = jnp.concatenate([jnp.cos(ang_r)] * 2 + [jnp.cos(ang_c)] * 2, axis=1)
    sin = jnp.concatenate([-jnp.sin(ang_r), jnp.sin(ang_r), -jnp.sin(ang_c), jnp.sin(ang_c)], axis=1)
    reps = LANES // HEAD_DIM
    return jnp.tile(cos, (1, reps)), jnp.tile(sin, (1, reps))


def _pad_cols(w, n):
    return jnp.pad(w, ((0, 0), (0, n - w.shape[1])))


def kernel(x, c, ctx, c_ctx, w_ada, b_ada, g_norm1, w_in, q_norm_g, k_norm_g, attn_sink, conv_w, a_log, dt_bias,
           dn_norm_g, w_br_attn, w_br_dn, w_out, g_norm2, w_mlp1, w_mlp2):
    depth = w_ada.shape[0]
    assert depth == 1, "single-layer trunk only"
    b, s, d = x.shape
    n_ctx = ctx.shape[1]
    assert d == D_MODEL and w_in.shape[-1] == _IN_WIDTH
    assert s >= 3 * ATTN_BLOCK and s % ATTN_BLOCK == 0 and s % CHUNK == 0 and n_ctx % CHUNK == 0
    out_dtype = x.dtype
    w_in0 = w_in[0]

    mod_rows = 16
    cc = jnp.concatenate([c.astype(F32), c_ctx.astype(F32)[None], jnp.zeros((mod_rows - b - 1, d), F32)], axis=0)
    mod = _ada(cc, w_ada[0], b_ada[0])
    mod3 = mod.reshape(mod_rows, 6, d)

    ab_cols = jnp.concatenate([w_in0[:, _OFF_DA:_OFF_GA]] * 2, axis=1)
    w_lat = jnp.concatenate([w_in0[:, :_OFF_DA], w_in0[:, _OFF_GA:], _pad_cols(ab_cols, LANES)], axis=1).astype(BF16)
    segs_lat = ((0, _OFF_DQ), (_OFF_DQ, 3 * DN_WIDTH), (_OFF_DZ, DN_WIDTH), (_OFF_DA, 2 * D_MODEL),
                (_OFF_DA + 2 * D_MODEL, LANES))
    a_lat, d_lat, z_lat, gates, ab_lat = _inproj(x, mod3, None, g_norm1[0], w_lat, segs_lat,
                                                 (BF16, BF16, BF16, BF16, F32), tm=512)
    w_ctx = jnp.concatenate([w_in0[:, _OFF_AK:_OFF_DQ], w_in0[:, _OFF_DK:_OFF_DZ], _pad_cols(ab_cols, LANES)],
                            axis=1).astype(BF16)
    segs_ctx = ((0, 2 * KV_WIDTH), (2 * KV_WIDTH, 2 * DN_WIDTH), (2 * KV_WIDTH + 2 * DN_WIDTH, LANES))
    kv_ctx, d_ctx, ab_ctx = _inproj(ctx, mod3, b, g_norm1[0], w_ctx, segs_ctx, (BF16, BF16, F32), tm=n_ctx)

    hsum = _head_avg(MXU_TILE, 1.0)
    q_d, k_d, v_d = _dnpre(d_lat, conv_w[0], hsum, (True, True, False), (True, False, False), tm=512)
    k_dc, v_dc = _dnpre(d_ctx, conv_w[0][:, DN_WIDTH:], hsum, (True, False), (False, False), tm=n_ctx)

    cos, sin = _rope_tables(s)
    reps = LANES // HEAD_DIM
    y_attn = _attention(a_lat, kv_ctx, cos, sin,
                        jnp.tile(q_norm_g[0].astype(F32), reps)[None], jnp.tile(k_norm_g[0].astype(F32), reps)[None],
                        jnp.broadcast_to(attn_sink[0].astype(F32)[:, None], (ATTN_Q_HEADS, LANES)),
                        _head_avg(LANES, 1.0 / HEAD_DIM), _kv_tile_matrix())

    n_gate = N_DIR * DN_HEADS
    arow = _pad_cols(jnp.tile(jnp.concatenate([jnp.exp(a_log[0]).reshape(1, n_gate), jnp.zeros((1, n_gate), F32)],
                                              axis=1), (1, 2)), LANES)
    dtrow = _pad_cols(jnp.tile(jnp.concatenate([dt_bias[0].reshape(1, n_gate), jnp.zeros((1, n_gate), F32)],
                                               axis=1), (1, 2)), LANES)
    exp2, tri = _dn_expand_matrix(), _tri_matrices()
    prep_ctx = _dnprep(None, k_dc, v_dc, ab_ctx, arow, dtrow, exp2, tri, cb=4)
    prep_lat = _dnprep(q_d, k_d, v_d, ab_lat, arow, dtrow, exp2, tri, cb=4)
    (s_ctx,) = _dnscan(prep_ctx, None, tb=n_ctx, want_s=True)
    o_df, o_db = _dnscan(prep_lat, s_ctx, tb=512, want_s=False)

    out1, hm = _merge(x, mod3, y_attn, o_df, o_db, z_lat, gates,
                      jnp.tile(dn_norm_g[0].astype(F32), HEADS_PER_TILE)[None], _head_avg(MXU_TILE, 1.0 / HEAD_DIM),
                      w_br_attn[0].astype(BF16), w_br_dn[0].astype(BF16), w_out[0].astype(BF16),
                      g_norm2[0].reshape(1, d), tm=256)
    out = _mlp(out1, hm, mod3, w_mlp1[0].astype(BF16), w_mlp2[0].astype(BF16), tm=256)
    return out.astype(out_dtype)
```

```python
import functools

import numpy as np
import jax
import jax.numpy as jnp
from jax import lax
from jax.experimental import pallas as pl
from jax.experimental.pallas import tpu as pltpu

F32 = jnp.float32
BF16 = jnp.bfloat16

D_MODEL = 1024
GRID_W = 64
HEAD_DIM = 64
ATTN_Q_HEADS = 8
ATTN_KV_HEADS = 2
ATTN_GROUP = ATTN_Q_HEADS // ATTN_KV_HEADS
WINDOW = 128
ATTN_BLOCK = 128
ROPE_BASE = 10000.0
DN_HEADS = 8
CONV_W = 3
CHUNK = 64
N_DIR = 2
D_FF = 4 * D_MODEL
EPS = 1e-6
NEG_INF = -1e30

ATTN_WIDTH = ATTN_Q_HEADS * HEAD_DIM
KV_WIDTH = ATTN_KV_HEADS * HEAD_DIM
DN_WIDTH = DN_HEADS * HEAD_DIM
LANES = 128
MXU_TILE = 256
HEADS_PER_TILE = MXU_TILE // HEAD_DIM
DN_SUB = 16
VMEM_LIMIT = 56 * 1024 * 1024

_OFF_AQ = 0
_OFF_AK = _OFF_AQ + ATTN_WIDTH
_OFF_AV = _OFF_AK + KV_WIDTH
_OFF_DQ = _OFF_AV + KV_WIDTH
_OFF_DK = _OFF_DQ + DN_WIDTH
_OFF_DV = _OFF_DK + DN_WIDTH
_OFF_DZ = _OFF_DV + DN_WIDTH
_OFF_DA = _OFF_DZ + DN_WIDTH
_OFF_DB = _OFF_DA + N_DIR * DN_HEADS
_OFF_GA = _OFF_DB + N_DIR * DN_HEADS
_OFF_GD = _OFF_GA + D_MODEL
_IN_WIDTH = _OFF_GD + D_MODEL


def _sigmoid(x):
    return 0.5 * jnp.tanh(0.5 * x) + 0.5


def _dot(a, b):
    return jnp.dot(a, b, preferred_element_type=F32)


def _dot_nt(a, b):
    return lax.dot_general(a, b, (((1,), (1,)), ((), ())), preferred_element_type=F32)


def _dot_tn(a, b):
    return lax.dot_general(a, b, (((0,), (0,)), ((), ())), preferred_element_type=F32)


def _cparams(n_axes):
    return pltpu.CompilerParams(dimension_semantics=("arbitrary",) * n_axes, vmem_limit_bytes=VMEM_LIMIT)


def _ada_kernel(c_ref, w_ref, b_ref, o_ref):
    c = c_ref[...]
    s = c * _sigmoid(c)
    o_ref[...] = _dot(s.astype(BF16), w_ref[...].astype(BF16)) + b_ref[...]


def _ada(cc, w_ada, b_ada):
    rows, d = cc.shape
    n = w_ada.shape[1]
    tn = 1536
    return pl.pallas_call(
        _ada_kernel,
        grid=(n // tn,),
        in_specs=[pl.BlockSpec((rows, d), lambda j: (0, 0)),
                  pl.BlockSpec((d, tn), lambda j: (0, j)),
                  pl.BlockSpec((1, tn), lambda j: (0, j))],
        out_specs=pl.BlockSpec((rows, tn), lambda j: (0, j)),
        out_shape=jax.ShapeDtypeStruct((rows, n), F32),
        compiler_params=_cparams(1),
        name="ada",
    )(cc, w_ada, b_ada.reshape(1, n))


def _inproj_kernel(x_ref, mod_ref, g_ref, w_ref, *out_refs, segs):
    x = x_ref[0]
    ms = jnp.mean(x * x, axis=-1, keepdims=True)
    mod = mod_ref[0]
    scale = g_ref[...] * (1.0 + mod[1:2])
    h = (x * lax.rsqrt(ms + EPS) * scale + mod[0:1]).astype(BF16)
    for o_ref, (start, size) in zip(out_refs, segs):
        o_ref[0] = _dot(h, w_ref[:, start:start + size]).astype(o_ref.dtype)


def _inproj(x, mod3, mod_row, g_norm, w, segs, dtypes, tm):
    b, t, d = x.shape
    n = w.shape[1]
    if mod_row is None:
        mod_map = lambda bi, i: (bi, 0, 0)
    else:
        mod_map = lambda bi, i: (mod_row, 0, 0)
    out_shape = [jax.ShapeDtypeStruct((b, t, size), dt) for (_, size), dt in zip(segs, dtypes)]
    out_specs = [pl.BlockSpec((1, tm, size), lambda bi, i: (bi, i, 0)) for (_, size) in segs]
    return pl.pallas_call(
        functools.partial(_inproj_kernel, segs=segs),
        grid=(b, t // tm),
        in_specs=[pl.BlockSpec((1, tm, d), lambda bi, i: (bi, i, 0)),
                  pl.BlockSpec((1, 6, d), mod_map),
                  pl.BlockSpec((1, d), lambda bi, i: (0, 0)),
                  pl.BlockSpec((d, n), lambda bi, i: (0, 0))],
        out_specs=out_specs,
        out_shape=out_shape,
        compiler_params=_cparams(2),
        name="inproj",
    )(x, mod3, g_norm.reshape(1, d), w)


def _dnpre_kernel(x_ref, prev_ref, next_ref, cw_ref, hsum_ref, *out_refs, tm, halo, norm_flags, q_flags):
    i = pl.program_id(1)
    last = pl.num_programs(1) - 1
    x = x_ref[0].astype(F32)
    prev_row = prev_ref[0][halo - 1:halo, :].astype(F32) * (i > 0).astype(F32)
    next_row = next_ref[0][0:1, :].astype(F32) * (i < last).astype(F32)
    rows = lax.broadcasted_iota(jnp.int32, x.shape, 0)
    xm = jnp.where(rows == 0, prev_row, pltpu.roll(x, 1, 0))
    xp = jnp.where(rows == tm - 1, next_row, pltpu.roll(x, tm - 1, 0))
    cw = cw_ref[...]
    y = xm * cw[0:1] + x * cw[1:2] + xp * cw[2:3]
    y = y * _sigmoid(y)
    hsum = hsum_ref[...]
    for s, (o_ref, do_norm, is_q) in enumerate(zip(out_refs, norm_flags, q_flags)):
        for j in range(DN_WIDTH // MXU_TILE):
            lo = s * DN_WIDTH + j * MXU_TILE
            ys = y[:, lo:lo + MXU_TILE]
            if do_norm:
                ss = _dot((ys * ys).astype(BF16), hsum)
                ys = ys * lax.rsqrt(ss + EPS)
                if is_q:
                    ys = ys * (HEAD_DIM ** -0.5)
            o_ref[0, :, j * MXU_TILE:(j + 1) * MXU_TILE] = ys.astype(o_ref.dtype)


def _dnpre(xd, conv_w, hsum, norm_flags, q_flags, tm):
    b, t, c = xd.shape
    halo = 16
    nblk = t // halo
    r = tm // halo
    n_out = c // DN_WIDTH
    return pl.pallas_call(
        functools.partial(_dnpre_kernel, tm=tm, halo=halo, norm_flags=norm_flags, q_flags=q_flags),
        grid=(b, t // tm),
        in_specs=[pl.BlockSpec((1, tm, c), lambda bi, i: (bi, i, 0)),
                  pl.BlockSpec((1, halo, c), lambda bi, i: (bi, jnp.maximum(i * r - 1, 0), 0)),
                  pl.BlockSpec((1, halo, c), lambda bi, i: (bi, jnp.minimum((i + 1) * r, nblk - 1), 0)),
                  pl.BlockSpec((CONV_W, c), lambda bi, i: (0, 0)),
                  pl.BlockSpec((MXU_TILE, MXU_TILE), lambda bi, i: (0, 0))],
        out_specs=[pl.BlockSpec((1, tm, DN_WIDTH), lambda bi, i: (bi, i, 0))] * n_out,
        out_shape=[jax.ShapeDtypeStruct((b, t, DN_WIDTH), BF16)] * n_out,
        compiler_params=_cparams(2),
        name="dnpre",
    )(xd, xd, xd, conv_w, hsum)


def _rope(x, cos, sin, lane):
    swapped = jnp.where((lane % 32) < 16, pltpu.roll(x, LANES - 16, 1), pltpu.roll(x, 16, 1))
    return x * cos + swapped * sin


def _attn_kernel(a_ref, kvc_ref, cos_ref, sin_ref, qg_ref, kg_ref, sink_ref, havg_ref,
                 o_ref, k_s, vt_s, *, seq, ctx):
    n = pl.program_id(1)
    havg = havg_ref[...]
    prep_rows = 256
    ctx_blocks = ctx // ATTN_BLOCK

    @pl.when(n == 0)
    def _prep():
        kg = kg_ref[...]
        kc = kvc_ref[0, :, 0:KV_WIDTH].astype(F32)
        ms = _dot((kc * kc).astype(BF16), havg)
        k_s[0:ctx, :] = (kc * lax.rsqrt(ms + EPS) * kg).astype(BF16)
        vc = kvc_ref[0, :, KV_WIDTH:2 * KV_WIDTH].astype(F32)
        for j in range(ctx_blocks):
            vt_s[j] = vc[j * ATTN_BLOCK:(j + 1) * ATTN_BLOCK, :].T.astype(BF16)
        lane = lax.broadcasted_iota(jnp.int32, (prep_rows, LANES), 1)
        for r0 in range(0, seq, prep_rows):
            kx = a_ref[0, r0:r0 + prep_rows, _OFF_AK:_OFF_AK + KV_WIDTH].astype(F32)
            ms = _dot((kx * kx).astype(BF16), havg)
            kx = kx * lax.rsqrt(ms + EPS) * kg
            kx = _rope(kx, cos_ref[r0:r0 + prep_rows, :], sin_ref[r0:r0 + prep_rows, :], lane)
            k_s[ctx + r0:ctx + r0 + prep_rows, :] = kx.astype(BF16)
            vx = a_ref[0, r0:r0 + prep_rows, _OFF_AV:_OFF_AV + KV_WIDTH].astype(F32)
            for j in range(prep_rows // ATTN_BLOCK):
                vt_s[ctx_blocks + r0 // ATTN_BLOCK + j] = vx[j * ATTN_BLOCK:(j + 1) * ATTN_BLOCK, :].T.astype(BF16)

    q0 = pl.multiple_of(n * ATTN_BLOCK, ATTN_BLOCK)
    lane = lax.broadcasted_iota(jnp.int32, (ATTN_BLOCK, LANES), 1)
    cos = cos_ref[pl.ds(q0, ATTN_BLOCK), :]
    sin = sin_ref[pl.ds(q0, ATTN_BLOCK), :]
    qg = qg_ref[...]
    q_parts = []
    for j in range(ATTN_WIDTH // LANES):
        qx = a_ref[0, pl.ds(q0, ATTN_BLOCK), j * LANES:(j + 1) * LANES].astype(F32)
        ms = _dot((qx * qx).astype(BF16), havg)
        qx = qx * lax.rsqrt(ms + EPS) * qg
        q_parts.append(_rope(qx, cos, sin, lane) * (HEAD_DIM ** -0.5))

    qts = [qp.T.astype(BF16) for qp in q_parts]

    n_band = 3
    n_win = n_band * ATTN_BLOCK
    n_keys = n_win + ctx
    blk0 = jnp.clip(n - 1, 0, seq // ATTN_BLOCK - n_band)
    start = pl.multiple_of(blk0 * ATTN_BLOCK, ATTN_BLOCK)
    key_row = lax.broadcasted_iota(jnp.int32, (n_keys, ATTN_BLOCK), 0)
    qpos = q0 + lax.broadcasted_iota(jnp.int32, (n_keys, ATTN_BLOCK), 1)
    valid = (jnp.abs(qpos - (start + key_row)) <= WINDOW) | (key_row >= n_win)
    valid = jnp.concatenate([valid] * ATTN_GROUP, axis=1)
    k_all = jnp.concatenate([k_s[pl.ds(ctx + start, n_win), :], k_s[0:ctx, :]], axis=0)
    vt_loc = vt_s[pl.ds(ctx_blocks + blk0, n_band)]
    vt_all = jnp.concatenate([vt_loc[j] for j in range(n_band)] + [vt_s[j] for j in range(ctx_blocks)], axis=1)
    zeros = jnp.zeros((HEAD_DIM, ATTN_BLOCK), BF16)

    groups = range(ATTN_KV_HEADS)
    rhss, sink_rows = [], []
    for g in groups:
        cols = []
        for r in range(ATTN_GROUP):
            h = g * ATTN_GROUP + r
            piece = qts[h // 2][(h % 2) * HEAD_DIM:(h % 2 + 1) * HEAD_DIM, :]
            cols.append(jnp.concatenate([piece, zeros] if g == 0 else [zeros, piece], axis=0))
        rhss.append(jnp.concatenate(cols, axis=1))
        sink_rows.append(jnp.concatenate([sink_ref[g * ATTN_GROUP + r:g * ATTN_GROUP + r + 1, :]
                                          for r in range(ATTN_GROUP)], axis=1))
    ss = [jnp.where(valid, _dot(k_all, rhs), NEG_INF) for rhs in rhss]
    maxes = [jnp.maximum(jnp.max(s, axis=0, keepdims=True), sink) for s, sink in zip(ss, sink_rows)]
    ps = [jnp.exp(s - m) for s, m in zip(ss, maxes)]
    invs = [1.0 / (jnp.sum(p, axis=0, keepdims=True) + jnp.exp(sink - m)) for p, sink, m in zip(ps, sink_rows, maxes)]
    p_bfs = [p.astype(BF16) for p in ps]
    pieces = []
    for g in groups:
        vt_g = vt_all[g * HEAD_DIM:(g + 1) * HEAD_DIM, :]
        for pair in range(ATTN_GROUP // 2):
            lanes = slice(pair * MXU_TILE, (pair + 1) * MXU_TILE)
            ot = _dot(vt_g, p_bfs[g][:, lanes]) * invs[g][:, lanes]
            pieces += [ot[:, 0:ATTN_BLOCK], ot[:, ATTN_BLOCK:2 * ATTN_BLOCK]]
    outs = [jnp.concatenate([pieces[2 * j], pieces[2 * j + 1]], axis=0).T for j in range(ATTN_Q_HEADS // 2)]
    o_ref[0] = jnp.concatenate(outs, axis=1).astype(o_ref.dtype)


def _attention(a_lat, kv_ctx, cos, sin, qg, kg, sink, havg):
    b, s, wa = a_lat.shape
    ctx = kv_ctx.shape[1]
    nb = s // ATTN_BLOCK
    const2 = lambda bi, n: (0, 0)
    return pl.pallas_call(
        functools.partial(_attn_kernel, seq=s, ctx=ctx),
        grid=(b, nb),
        in_specs=[pl.BlockSpec((1, s, wa), lambda bi, n: (bi, 0, 0)),
                  pl.BlockSpec((1, ctx, 2 * KV_WIDTH), lambda bi, n: (bi, 0, 0)),
                  pl.BlockSpec((s, LANES), const2),
                  pl.BlockSpec((s, LANES), const2),
                  pl.BlockSpec((1, LANES), const2),
                  pl.BlockSpec((1, LANES), const2),
                  pl.BlockSpec((ATTN_Q_HEADS, LANES), const2),
                  pl.BlockSpec((LANES, LANES), const2)],
        out_specs=pl.BlockSpec((1, ATTN_BLOCK, ATTN_WIDTH), lambda bi, n: (bi, n, 0)),
        out_shape=jax.ShapeDtypeStruct((b, s, ATTN_WIDTH), BF16),
        scratch_shapes=[pltpu.VMEM((ctx + s, KV_WIDTH), BF16),
                        pltpu.VMEM(((ctx + s) // ATTN_BLOCK, KV_WIDTH, ATTN_BLOCK), BF16)],
        compiler_params=_cparams(2),
        name="attn",
    )(a_lat, kv_ctx, cos, sin, qg, kg, sink, havg)


def _bdiag(x_lane, bmask):
    return jnp.where(bmask, jnp.concatenate([x_lane] * HEADS_PER_TILE, axis=0), jnp.zeros((), x_lane.dtype))


def _unit_tri_inverses(a_list, eye_l, sub_mask, bmask):
    ads = [jnp.where(sub_mask, a, 0.0) for a in a_list]
    aos = [jnp.where(sub_mask, 0.0, a).astype(BF16) for a in a_list]
    ps = [eye_l - ad for ad in ads]
    pws = [ad.astype(BF16) for ad in ads]
    n_levels = int(np.log2(DN_SUB))
    for level in range(n_levels):
        rhss = [_bdiag(pw, bmask) for pw in pws]
        if level == 0:
            pws = [_dot(pw, rhs).astype(BF16) for pw, rhs in zip(pws, rhss)]
        elif level < n_levels - 1:
            boths = [_dot(jnp.concatenate([pw, p.astype(BF16)], axis=0), rhs) for pw, p, rhs in zip(pws, ps, rhss)]
            pws = [both[0:CHUNK].astype(BF16) for both in boths]
            ps = [p + both[CHUNK:2 * CHUNK] for p, both in zip(ps, boths)]
        else:
            ps = [p + _dot(p.astype(BF16), rhs) for p, rhs in zip(ps, rhss)]
    xs = [p.astype(BF16) for p in ps]
    bs = [_dot(x, _bdiag(ao, bmask)) for x, ao in zip(xs, aos)]
    b_rhss = [_bdiag(b.astype(BF16), bmask) for b in bs]
    b2s = [_dot(b.astype(BF16), rhs) for b, rhs in zip(bs, b_rhss)]
    b3s = [_dot(b2.astype(BF16), rhs) for b2, rhs in zip(b2s, b_rhss)]
    assert CHUNK // DN_SUB == 4
    qs = [(eye_l - b + b2 - b3).astype(BF16) for b, b2, b3 in zip(bs, b2s, b3s)]
    return [_dot(q, _bdiag(x, bmask)) for q, x in zip(qs, xs)]


def _dn_factors(ab, d, arow, dtrow, exp2_ref, tri_ref, eye_t):
    lane = lax.broadcasted_iota(jnp.int32, ab.shape, 1)
    z = ab + dtrow
    softplus = jnp.maximum(z, 0.0) + jnp.log(1.0 + jnp.exp(-jnp.abs(z)))
    is_g = (lane % 32) < 16
    x = jnp.where(is_g, -arow * softplus, _sigmoid(ab))
    x_hi = x.astype(BF16)
    x_lo = (x - x_hi.astype(F32)).astype(BF16)
    xhl = jnp.where(lane < 32, x_hi, x_lo)[:, 0:64]
    y = _dot(xhl, exp2_ref[d])
    ge = y[:, 0:DN_WIDTH]
    be = y[:, DN_WIDTH:2 * DN_WIDTH]
    ge_hi = ge.astype(BF16)
    ge_lo = (ge - ge_hi.astype(F32)).astype(BF16)
    gi = _dot(tri_ref[d], jnp.concatenate([ge_hi, ge_lo], axis=0))
    gj = jnp.sum(gi * eye_t, axis=0, keepdims=True)
    return be, gi, jnp.broadcast_to(gj, gi.shape)


def _block_mask():
    return (lax.broadcasted_iota(jnp.int32, (MXU_TILE, MXU_TILE), 0) // HEAD_DIM
            == lax.broadcasted_iota(jnp.int32, (MXU_TILE, MXU_TILE), 1) // HEAD_DIM)


def _dnprep_kernel(*refs, cb, want_o):
    if want_o:
        q_ref, k_ref, v_ref, ab_ref = refs[:4]
        refs = refs[4:]
    else:
        q_ref = None
        k_ref, v_ref, ab_ref = refs[:3]
        refs = refs[3:]
    arow_ref, dtrow_ref, exp2_ref, tri_ref, w_o, uv_o, kt_o, dl_o = refs[:8]
    qd_o, in_o = refs[8:10] if want_o else (None, None)
    n_tiles = DN_WIDTH // MXU_TILE
    row = lax.broadcasted_iota(jnp.int32, (CHUNK, MXU_TILE), 0)
    colj = lax.broadcasted_iota(jnp.int32, (CHUNK, MXU_TILE), 1) % HEAD_DIM
    eye_l = (row == colj).astype(F32)
    eye_t = jnp.concatenate([eye_l] * n_tiles, axis=1)
    bmask = _block_mask()
    arow = arow_ref[...]
    dtrow = dtrow_ref[...]

    fac = {}
    shared = {}
    for c in range(cb):
        rows = slice(c * CHUNK, (c + 1) * CHUNK)
        ab = ab_ref[0, rows, :]
        for d in range(N_DIR):
            fac[(c, d)] = _dn_factors(ab, d, arow, dtrow, exp2_ref, tri_ref, eye_t)
        for g in range(n_tiles):
            lanes = slice(g * MXU_TILE, (g + 1) * MXU_TILE)
            k_l = k_ref[0, rows, lanes]
            kbd = _bdiag(k_l, bmask)
            if want_o:
                kq = _dot_nt(jnp.concatenate([k_l, q_ref[0, rows, lanes]], axis=0), kbd)
                shared[(c, g)] = (kq[0:CHUNK], kq[CHUNK:2 * CHUNK])
            else:
                shared[(c, g)] = (_dot_nt(k_l, kbd), None)

    units = [(c, g, d) for c in range(cb) for g in range(n_tiles) for d in range(N_DIR)]
    decs = []
    for c, g, d in units:
        lanes = slice(g * MXU_TILE, (g + 1) * MXU_TILE)
        be, gi, gj = fac[(c, d)]
        lower = (row > colj) if d == 0 else (row < colj)
        decs.append(jnp.where(lower, jnp.exp(jnp.where(lower, gi[:, lanes] - gj[:, lanes], 0.0)), 0.0))
    a_list = [fac[(c, d)][0][:, g * MXU_TILE:(g + 1) * MXU_TILE] * dec * shared[(c, g)][0]
              for (c, g, d), dec in zip(units, decs)]
    tinvs = _unit_tri_inverses(a_list, eye_l, (row // DN_SUB) == (colj // DN_SUB), bmask)

    for (c, g, d), dec, tinv in zip(units, decs, tinvs):
        rows = slice(c * CHUNK, (c + 1) * CHUNK)
        lanes = slice(g * MXU_TILE, (g + 1) * MXU_TILE)
        be, gi, _ = fac[(c, d)]
        be, gi = be[:, lanes], gi[:, lanes]
        last = CHUNK - 1 if d == 0 else 0
        e_g = jnp.exp(gi)
        gl_row = gi[last:last + 1, :]
        kf = k_ref[0, rows, lanes].astype(F32)
        vf = v_ref[0, rows, lanes].astype(F32)
        rhs = jnp.concatenate([_bdiag((be * e_g * kf).astype(BF16), bmask), _bdiag((be * vf).astype(BF16), bmask)],
                              axis=1)
        wu = _dot(tinv.astype(BF16), rhs)
        w_o[0, d, rows, lanes] = wu[:, 0:MXU_TILE].astype(BF16)
        uv_o[0, d, rows, lanes] = wu[:, MXU_TILE:2 * MXU_TILE]
        kt_o[0, d, rows, lanes] = (jnp.exp(gl_row - gi) * kf).astype(BF16)
        dl_o[0, d, c, :, lanes] = jnp.exp(gl_row)
        if want_o:
            qd_o[0, d, rows, lanes] = (e_g * q_ref[0, rows, lanes].astype(F32)).astype(BF16)
            in_o[0, d, rows, lanes] = ((dec + eye_l) * shared[(c, g)][1]).astype(BF16)


def _dnprep(q, k, v, ab, arow, dtrow, exp2, tri, cb):
    want_o = q is not None
    b, t, w = k.shape
    tb = cb * CHUNK
    tok = lambda bi, i: (bi, i, 0)
    const2 = lambda bi, i: (0, 0)
    const3 = lambda bi, i: (0, 0, 0)
    dir_tok = lambda bi, i: (bi, 0, i, 0)
    data = ([q] if want_o else []) + [k, v]
    in_specs = ([pl.BlockSpec((1, tb, w), tok)] * len(data) + [pl.BlockSpec((1, tb, LANES), tok),
                pl.BlockSpec((1, LANES), const2), pl.BlockSpec((1, LANES), const2),
                pl.BlockSpec(exp2.shape, const3), pl.BlockSpec(tri.shape, const3)])
    big = lambda dt: jax.ShapeDtypeStruct((b, N_DIR, t, w), dt)
    big_spec = pl.BlockSpec((1, N_DIR, tb, w), dir_tok)
    out_shape = [big(BF16), big(F32), big(BF16), jax.ShapeDtypeStruct((b, N_DIR, t // CHUNK, 1, w), F32)]
    out_specs = [big_spec, big_spec, big_spec, pl.BlockSpec((1, N_DIR, cb, 1, w), lambda bi, i: (bi, 0, i, 0, 0))]
    if want_o:
        out_shape += [big(BF16), big(BF16)]
        out_specs += [big_spec, big_spec]
    return pl.pallas_call(
        functools.partial(_dnprep_kernel, cb=cb, want_o=want_o),
        grid=(b, t // tb),
        in_specs=in_specs,
        out_specs=out_specs,
        out_shape=out_shape,
        compiler_params=_cparams(2),
        name="dnprep",
    )(*data, ab, arow, dtrow, exp2, tri)


def _dnscan_kernel(*refs, n_chunk, bb, want_o, have_s0, want_s):
    n_in = 6 if want_o else 4
    dir_refs = [refs[0:n_in], refs[n_in:2 * n_in]]
    pos = 2 * n_in
    s0_ref = refs[pos] if have_s0 else None
    pos += int(have_s0)
    o_refs = refs[pos:pos + N_DIR] if want_o else None
    pos += N_DIR if want_o else 0
    sout_ref = refs[pos] if want_s else None
    pos += int(want_s)
    s_scr = refs[pos]
    n_tiles = DN_WIDTH // MXU_TILE
    i = pl.program_id(1)

    @pl.when(i == 0)
    def _init():
        if have_s0:
            s_scr[...] = s0_ref[...]
        else:
            s_scr[...] = jnp.zeros_like(s_scr)

    bmask = _block_mask()
    chains = [(bi, d, g) for bi in range(bb) for d in range(N_DIR) for g in range(n_tiles)]

    def body(j, carry):
        cidx = (j, n_chunk - 1 - j)
        r0s = [pl.multiple_of(cidx[d] * CHUNK, CHUNK) for d in range(N_DIR)]
        s_olds, r1s = [], []
        for bi, d, g in chains:
            lanes = slice(g * MXU_TILE, (g + 1) * MXU_TILE)
            w = dir_refs[d][0][bi, 0, pl.ds(r0s[d], CHUNK), lanes]
            if want_o:
                w = jnp.concatenate([w, dir_refs[d][4][bi, 0, pl.ds(r0s[d], CHUNK), lanes]], axis=0)
            s_old = s_scr[bi, d * n_tiles + g]
            s_olds.append(s_old)
            r1s.append(_dot(w, s_old.astype(BF16)))
        u_bfs = []
        for (bi, d, g), r1 in zip(chains, r1s):
            lanes = slice(g * MXU_TILE, (g + 1) * MXU_TILE)
            u_bfs.append((dir_refs[d][1][bi, 0, pl.ds(r0s[d], CHUNK), lanes] - r1[0:CHUNK]).astype(BF16))
        for (bi, d, g), r1, u_bf, s_old in zip(chains, r1s, u_bfs, s_olds):
            lanes = slice(g * MXU_TILE, (g + 1) * MXU_TILE)
            kt = dir_refs[d][2][bi, 0, pl.ds(r0s[d], CHUNK), lanes]
            ds = jnp.where(bmask, _dot_tn(kt, u_bf), 0.0)
            dl = dir_refs[d][3][bi, 0, cidx[d]][:, lanes]
            s_scr[bi, d * n_tiles + g] = s_old * dl + ds
            if want_o:
                intra = dir_refs[d][5][bi, 0, pl.ds(r0s[d], CHUNK), lanes]
                o_refs[d][bi, pl.ds(r0s[d], CHUNK), lanes] = r1[CHUNK:2 * CHUNK] + _dot(intra, _bdiag(u_bf, bmask))
        return carry

    lax.fori_loop(0, n_chunk, body, 0)

    if want_s:
        @pl.when(i == pl.num_programs(1) - 1)
        def _fin():
            sout_ref[...] = s_scr[...]


def _dnscan(prep, s0, tb, bb, want_s):
    want_o = len(prep) == 6
    b, _, t, w = prep[0].shape
    n_t = t // tb
    n_chunk = tb // CHUNK
    n_chain = N_DIR * (w // MXU_TILE)

    def specs(d):
        blk = (lambda bi, i: i) if d == 0 else (lambda bi, i: n_t - 1 - i)
        big = pl.BlockSpec((bb, 1, tb, w), lambda bi, i: (bi, d, blk(bi, i), 0))
        dl = pl.BlockSpec((bb, 1, n_chunk, 1, w), lambda bi, i: (bi, d, blk(bi, i), 0, 0))
        return [big, big, big, dl] + ([big, big] if want_o else [])

    in_specs = specs(0) + specs(1)
    args = list(prep) + list(prep)
    state_spec = pl.BlockSpec((bb, n_chain, MXU_TILE, MXU_TILE), lambda bi, i: (bi, 0, 0, 0))
    if s0 is not None:
        in_specs.append(state_spec)
        args.append(s0)
    out_shape, out_specs = [], []
    if want_o:
        out_shape += [jax.ShapeDtypeStruct((b, t, w), F32)] * N_DIR
        out_specs += [pl.BlockSpec((bb, tb, w), lambda bi, i: (bi, i, 0)),
                      pl.BlockSpec((bb, tb, w), lambda bi, i: (bi, n_t - 1 - i, 0))]
    if want_s:
        out_shape.append(jax.ShapeDtypeStruct((b, n_chain, MXU_TILE, MXU_TILE), F32))
        out_specs.append(state_spec)
    return pl.pallas_call(
        functools.partial(_dnscan_kernel, n_chunk=n_chunk, bb=bb, want_o=want_o, have_s0=s0 is not None,
                          want_s=want_s),
        grid=(b // bb, n_t),
        in_specs=in_specs,
        out_specs=out_specs,
        out_shape=out_shape,
        scratch_shapes=[pltpu.VMEM((bb, n_chain, MXU_TILE, MXU_TILE), F32)],
        compiler_params=_cparams(2),
        name="dnscan",
    )(*args)


def _merge_kernel(x_ref, mod_ref, ya_ref, odf_ref, odb_ref, z_ref, gate_ref, dng_ref, havg_ref, wba_ref, wbd_ref,
                  wo_ref, gn2_ref, out1_ref, hm_ref):
    havg = havg_ref[...]
    dng = dng_ref[...]
    yd_parts = []
    for j in range(DN_WIDTH // MXU_TILE):
        sl = slice(j * MXU_TILE, (j + 1) * MXU_TILE)
        od = odf_ref[0, :, sl] + odb_ref[0, :, sl]
        ms = _dot((od * od).astype(BF16), havg)
        z = z_ref[0, :, sl].astype(F32)
        yd_parts.append((od * lax.rsqrt(ms + EPS) * dng * (z * _sigmoid(z))).astype(BF16))
    yd = jnp.concatenate(yd_parts, axis=1)
    ga = gate_ref[0, :, 0:D_MODEL].astype(F32)
    gd = gate_ref[0, :, D_MODEL:2 * D_MODEL].astype(F32)
    y = _sigmoid(ga) * _dot(ya_ref[0], wba_ref[...]) + _sigmoid(gd) * _dot(yd, wbd_ref[...])
    br = _dot(y.astype(BF16), wo_ref[...])
    mod = mod_ref[0]
    out1 = x_ref[0] + mod[2:3] * br
    out1_ref[0] = out1
    ms2 = jnp.mean(out1 * out1, axis=-1, keepdims=True)
    hm = out1 * lax.rsqrt(ms2 + EPS) * (gn2_ref[...] * (1.0 + mod[4:5])) + mod[3:4]
    hm_ref[0] = hm.astype(BF16)


def _merge(x, mod3, y_attn, o_df, o_db, z, gates, dng, havg, wba, wbd, wo, gn2, tm):
    b, t, d = x.shape
    tok = lambda bi, i: (bi, i, 0)
    const2 = lambda bi, i: (0, 0)
    return pl.pallas_call(
        _merge_kernel,
        grid=(b, t // tm),
        in_specs=[pl.BlockSpec((1, tm, d), tok),
                  pl.BlockSpec((1, 6, d), lambda bi, i: (bi, 0, 0)),
                  pl.BlockSpec((1, tm, ATTN_WIDTH), tok),
                  pl.BlockSpec((1, tm, DN_WIDTH), tok),
                  pl.BlockSpec((1, tm, DN_WIDTH), tok),
                  pl.BlockSpec((1, tm, DN_WIDTH), tok),
                  pl.BlockSpec((1, tm, 2 * d), tok),
                  pl.BlockSpec((1, MXU_TILE), const2),
                  pl.BlockSpec((MXU_TILE, MXU_TILE), const2),
                  pl.BlockSpec(wba.shape, const2),
                  pl.BlockSpec(wbd.shape, const2),
                  pl.BlockSpec(wo.shape, const2),
                  pl.BlockSpec((1, d), const2)],
        out_specs=[pl.BlockSpec((1, tm, d), tok), pl.BlockSpec((1, tm, d), tok)],
        out_shape=[jax.ShapeDtypeStruct((b, t, d), F32), jax.ShapeDtypeStruct((b, t, d), BF16)],
        compiler_params=_cparams(2),
        name="merge",
    )(x, mod3, y_attn, o_df, o_db, z, gates, dng, havg, wba, wbd, wo, gn2)


def _mlp_kernel(out1_ref, hm_ref, mod_ref, w1_ref, w2_ref, o_ref, *, ff_chunk):
    hm = hm_ref[0]
    acc = None
    for j in range(D_FF // ff_chunk):
        a = jnp.maximum(_dot(hm, w1_ref[:, j * ff_chunk:(j + 1) * ff_chunk]), 0.0)
        part = _dot((a * a).astype(BF16), w2_ref[j * ff_chunk:(j + 1) * ff_chunk, :])
        acc = part if acc is None else acc + part
    o_ref[0] = out1_ref[0] + mod_ref[0][5:6] * acc


def _mlp(out1, hm, mod3, w1, w2, tm):
    b, t, d = out1.shape
    tok = lambda bi, i: (bi, i, 0)
    const2 = lambda bi, i: (0, 0)
    return pl.pallas_call(
        functools.partial(_mlp_kernel, ff_chunk=1024),
        grid=(b, t // tm),
        in_specs=[pl.BlockSpec((1, tm, d), tok),
                  pl.BlockSpec((1, tm, d), tok),
                  pl.BlockSpec((1, 6, d), lambda bi, i: (bi, 0, 0)),
                  pl.BlockSpec(w1.shape, const2),
                  pl.BlockSpec(w2.shape, const2)],
        out_specs=pl.BlockSpec((1, tm, d), tok),
        out_shape=jax.ShapeDtypeStruct((b, t, d), F32),
        compiler_params=_cparams(2),
        name="mlp",
    )(out1, hm, mod3, w1, w2)


def _head_avg(n, scale):
    idx = np.arange(n) // HEAD_DIM
    return jnp.asarray((idx[:, None] == idx[None, :]).astype(np.float32) * scale, BF16)


def _dn_expand_matrix():
    n = N_DIR * DN_HEADS
    m = np.zeros((N_DIR, 4 * n, 2 * DN_WIDTH), np.float32)
    for d in range(N_DIR):
        for part in range(2):
            for h in range(DN_HEADS):
                idx = d * DN_HEADS + h
                m[d, part * 2 * n + idx, h * HEAD_DIM:(h + 1) * HEAD_DIM] = 1.0
                m[d, part * 2 * n + n + idx, DN_WIDTH + h * HEAD_DIM:DN_WIDTH + (h + 1) * HEAD_DIM] = 1.0
    return jnp.asarray(m, BF16)


def _tri_matrices():
    i = np.arange(CHUNK)
    low = (i[:, None] >= i[None, :]).astype(np.float32)
    up = (i[:, None] <= i[None, :]).astype(np.float32)
    return jnp.asarray(np.stack([np.concatenate([low, low], axis=1), np.concatenate([up, up], axis=1)]), BF16)


def _rope_tables(seq):
    half = HEAD_DIM // 2
    n_freq = half // 2
    freqs = ROPE_BASE ** (-jnp.arange(n_freq, dtype=F32) / n_freq)
    pos = jnp.arange(seq)
    ang_r = (pos // GRID_W).astype(F32)[:, None] * freqs
    ang_c = (pos % GRID_W).astype(F32)[:, None] * freqs
    cos = jnp.concatenate([jnp.cos(ang_r)] * 2 + [jnp.cos(ang_c)] * 2, axis=1)
    sin = jnp.concatenate([-jnp.sin(ang_r), jnp.sin(ang_r), -jnp.sin(ang_c), jnp.sin(ang_c)], axis=1)
    reps = LANES // HEAD_DIM
    return jnp.tile(cos, (1, reps)), jnp.tile(sin, (1, reps))


def _pad_cols(w, n):
    return jnp.pad(w, ((0, 0), (0, n - w.shape[1])))


def kernel(x, c, ctx, c_ctx, w_ada, b_ada, g_norm1, w_in, q_norm_g, k_norm_g, attn_sink, conv_w, a_log, dt_bias,
           dn_norm_g, w_br_attn, w_br_dn, w_out, g_norm2, w_mlp1, w_mlp2):
    depth = w_ada.shape[0]
    assert depth == 1, "single-layer trunk only"
    b, s, d = x.shape
    n_ctx = ctx.shape[1]
    assert d == D_MODEL and w_in.shape[-1] == _IN_WIDTH
    assert s >= 3 * ATTN_BLOCK and s % ATTN_BLOCK == 0 and s % CHUNK == 0 and n_ctx % CHUNK == 0
    out_dtype = x.dtype
    w_in0 = w_in[0]

    mod_rows = 16
    cc = jnp.concatenate([c.astype(F32), c_ctx.astype(F32)[None], jnp.zeros((mod_rows - b - 1, d), F32)], axis=0)
    mod = _ada(cc, w_ada[0], b_ada[0])
    mod3 = mod.reshape(mod_rows, 6, d)

    ab_cols = jnp.concatenate([w_in0[:, _OFF_DA:_OFF_GA]] * 2, axis=1)
    w_lat = jnp.concatenate([w_in0[:, :_OFF_DA], w_in0[:, _OFF_GA:], _pad_cols(ab_cols, LANES)], axis=1).astype(BF16)
    segs_lat = ((0, _OFF_DQ), (_OFF_DQ, 3 * DN_WIDTH), (_OFF_DZ, DN_WIDTH), (_OFF_DA, 2 * D_MODEL),
                (_OFF_DA + 2 * D_MODEL, LANES))
    a_lat, d_lat, z_lat, gates, ab_lat = _inproj(x, mod3, None, g_norm1[0], w_lat, segs_lat,
                                                 (BF16, BF16, BF16, BF16, F32), tm=512)
    w_ctx = jnp.concatenate([w_in0[:, _OFF_AK:_OFF_DQ], w_in0[:, _OFF_DK:_OFF_DZ], _pad_cols(ab_cols, LANES)],
                            axis=1).astype(BF16)
    segs_ctx = ((0, 2 * KV_WIDTH), (2 * KV_WIDTH, 2 * DN_WIDTH), (2 * KV_WIDTH + 2 * DN_WIDTH, LANES))
    kv_ctx, d_ctx, ab_ctx = _inproj(ctx, mod3, b, g_norm1[0], w_ctx, segs_ctx, (BF16, BF16, F32), tm=n_ctx)

    hsum = _head_avg(MXU_TILE, 1.0)
    q_d, k_d, v_d = _dnpre(d_lat, conv_w[0], hsum, (True, True, False), (True, False, False), tm=512)
    k_dc, v_dc = _dnpre(d_ctx, conv_w[0][:, DN_WIDTH:], hsum, (True, False), (False, False), tm=n_ctx)

    cos, sin = _rope_tables(s)
    reps = LANES // HEAD_DIM
    y_attn = _attention(a_lat, kv_ctx, cos, sin,
                        jnp.tile(q_norm_g[0].astype(F32), reps)[None], jnp.tile(k_norm_g[0].astype(F32), reps)[None],
                        jnp.broadcast_to(attn_sink[0].astype(F32)[:, None], (ATTN_Q_HEADS, LANES)),
                        _head_avg(LANES, 1.0 / HEAD_DIM))

    n_gate = N_DIR * DN_HEADS
    arow = _pad_cols(jnp.tile(jnp.concatenate([jnp.exp(a_log[0]).reshape(1, n_gate), jnp.zeros((1, n_gate), F32)],
                                              axis=1), (1, 2)), LANES)
    dtrow = _pad_cols(jnp.tile(jnp.concatenate([dt_bias[0].reshape(1, n_gate), jnp.zeros((1, n_gate), F32)],
                                               axis=1), (1, 2)), LANES)
    exp2, tri = _dn_expand_matrix(), _tri_matrices()
    prep_ctx = _dnprep(None, k_dc, v_dc, ab_ctx, arow, dtrow, exp2, tri, cb=4)
    prep_lat = _dnprep(q_d, k_d, v_d, ab_lat, arow, dtrow, exp2, tri, cb=4)
    bb = 2 if b % 2 == 0 else 1
    (s_ctx,) = _dnscan(prep_ctx, None, tb=n_ctx, bb=bb, want_s=True)
    o_df, o_db = _dnscan(prep_lat, s_ctx, tb=512, bb=bb, want_s=False)

    out1, hm = _merge(x, mod3, y_attn, o_df, o_db, z_lat, gates,
                      jnp.tile(dn_norm_g[0].astype(F32), HEADS_PER_TILE)[None], _head_avg(MXU_TILE, 1.0 / HEAD_DIM),
                      w_br_attn[0].astype(BF16), w_br_dn[0].astype(BF16), w_out[0].astype(BF16),
                      g_norm2[0].reshape(1, d), tm=256)
    out = _mlp(out1, hm, mod3, w_mlp1[0].astype(BF16), w_mlp2[0].astype(BF16), tm=256)
    return out.astype(out_dtype)
```

```python
import functools

import numpy as np
import jax
import jax.numpy as jnp
from jax import lax
from jax.experimental import pallas as pl
from jax.experimental.pallas import tpu as pltpu

F32 = jnp.float32
BF16 = jnp.bfloat16

D_MODEL = 1024
GRID_W = 64
HEAD_DIM = 64
ATTN_Q_HEADS = 8
ATTN_KV_HEADS = 2
ATTN_GROUP = ATTN_Q_HEADS // ATTN_KV_HEADS
WINDOW = 128
ATTN_BLOCK = 128
ROPE_BASE = 10000.0
DN_HEADS = 8
CONV_W = 3
CHUNK = 64
N_DIR = 2
D_FF = 4 * D_MODEL
EPS = 1e-6
NEG_INF = -1e30

ATTN_WIDTH = ATTN_Q_HEADS * HEAD_DIM
KV_WIDTH = ATTN_KV_HEADS * HEAD_DIM
DN_WIDTH = DN_HEADS * HEAD_DIM
LANES = 128
MXU_TILE = 256
HEADS_PER_TILE = MXU_TILE // HEAD_DIM
DN_SUB = 16
VMEM_LIMIT = 56 * 1024 * 1024

_OFF_AQ = 0
_OFF_AK = _OFF_AQ + ATTN_WIDTH
_OFF_AV = _OFF_AK + KV_WIDTH
_OFF_DQ = _OFF_AV + KV_WIDTH
_OFF_DK = _OFF_DQ + DN_WIDTH
_OFF_DV = _OFF_DK + DN_WIDTH
_OFF_DZ = _OFF_DV + DN_WIDTH
_OFF_DA = _OFF_DZ + DN_WIDTH
_OFF_DB = _OFF_DA + N_DIR * DN_HEADS
_OFF_GA = _OFF_DB + N_DIR * DN_HEADS
_OFF_GD = _OFF_GA + D_MODEL
_IN_WIDTH = _OFF_GD + D_MODEL


def _sigmoid(x):
    return 0.5 * jnp.tanh(0.5 * x) + 0.5


def _dot(a, b):
    return jnp.dot(a, b, preferred_element_type=F32)


def _dot_nt(a, b):
    return lax.dot_general(a, b, (((1,), (1,)), ((), ())), preferred_element_type=F32)


def _dot_tn(a, b):
    return lax.dot_general(a, b, (((0,), (0,)), ((), ())), preferred_element_type=F32)


def _cparams(n_axes):
    return pltpu.CompilerParams(dimension_semantics=("arbitrary",) * n_axes, vmem_limit_bytes=VMEM_LIMIT)


def _ada_kernel(c_ref, w_ref, b_ref, o_ref):
    c = c_ref[...]
    s = c * _sigmoid(c)
    o_ref[...] = _dot(s.astype(BF16), w_ref[...].astype(BF16)) + b_ref[...]


def _ada(cc, w_ada, b_ada):
    rows, d = cc.shape
    n = w_ada.shape[1]
    tn = 1536
    return pl.pallas_call(
        _ada_kernel,
        grid=(n // tn,),
        in_specs=[pl.BlockSpec((rows, d), lambda j: (0, 0)),
                  pl.BlockSpec((d, tn), lambda j: (0, j)),
                  pl.BlockSpec((1, tn), lambda j: (0, j))],
        out_specs=pl.BlockSpec((rows, tn), lambda j: (0, j)),
        out_shape=jax.ShapeDtypeStruct((rows, n), F32),
        compiler_params=_cparams(1),
        name="ada",
    )(cc, w_ada, b_ada.reshape(1, n))


def _inproj_kernel(x_ref, xprev_ref, xnext_ref, mod_ref, g_ref, w_ref, cw_ref, hsum_ref, *out_refs, segs, tm, halo):
    i = pl.program_id(1)
    last = pl.num_programs(1) - 1
    mod = mod_ref[0]
    scale = g_ref[...] * (1.0 + mod[1:2])

    def norm_mod(v):
        ms = jnp.mean(v * v, axis=-1, keepdims=True)
        return (v * lax.rsqrt(ms + EPS) * scale + mod[0:1]).astype(BF16)

    h = norm_mod(x_ref[0])
    h_halo = norm_mod(jnp.concatenate([xprev_ref[0], xnext_ref[0]], axis=0))
    keep_prev = (i > 0).astype(F32)
    keep_next = (i < last).astype(F32)
    rows = lax.broadcasted_iota(jnp.int32, (tm, MXU_TILE), 0)
    hsum = hsum_ref[...]
    def plain_piece(o_ref, start, lo, width):
        o_ref[0, :, lo:lo + width] = _dot(h, w_ref[:, start + lo:start + lo + width]).astype(o_ref.dtype)

    def conv_piece(o_ref, start, lo, kind):
        conv_col, do_norm, is_q = kind
        w_cols = w_ref[:, start + lo:start + lo + MXU_TILE]
        p = _dot(h, w_cols)
        p_halo = _dot(h_halo, w_cols)
        p_prev = jnp.where(rows == 0, p_halo[halo - 1:halo] * keep_prev, pltpu.roll(p, 1, 0))
        p_next = jnp.where(rows == tm - 1, p_halo[halo:halo + 1] * keep_next, pltpu.roll(p, tm - 1, 0))
        cw = cw_ref[:, conv_col + lo:conv_col + lo + MXU_TILE]
        y = p_prev * cw[0:1] + p * cw[1:2] + p_next * cw[2:3]
        y = y * _sigmoid(y)
        if do_norm:
            y = y * lax.rsqrt(_dot((y * y).astype(BF16), hsum) + EPS)
            if is_q:
                y = y * (HEAD_DIM ** -0.5)
        o_ref[0, :, lo:lo + MXU_TILE] = y.astype(o_ref.dtype)

    plain, conv = [], []
    for o_ref, (start, size, kind) in zip(out_refs, segs):
        if kind is None:
            plain += [functools.partial(plain_piece, o_ref, start, lo, min(2 * MXU_TILE, size - lo))
                      for lo in range(0, size, 2 * MXU_TILE)]
        else:
            conv += [functools.partial(conv_piece, o_ref, start, lo, kind) for lo in range(0, size, MXU_TILE)]
    while plain or conv:
        if conv:
            conv.pop(0)()
        if plain:
            plain.pop(0)()


def _inproj(x, mod3, mod_row, g_norm, w, conv_w, hsum, segs, dtypes, tm):
    b, t, d = x.shape
    n = w.shape[1]
    halo = 8
    r = tm // halo
    nblk = t // halo
    if mod_row is None:
        mod_map = lambda bi, i: (bi, 0, 0)
    else:
        mod_map = lambda bi, i: (mod_row, 0, 0)
    const2 = lambda bi, i: (0, 0)
    out_shape = [jax.ShapeDtypeStruct((b, t, size), dt) for (_, size, _), dt in zip(segs, dtypes)]
    out_specs = [pl.BlockSpec((1, tm, size), lambda bi, i: (bi, i, 0)) for (_, size, _) in segs]
    return pl.pallas_call(
        functools.partial(_inproj_kernel, segs=segs, tm=tm, halo=halo),
        grid=(b, t // tm),
        in_specs=[pl.BlockSpec((1, tm, d), lambda bi, i: (bi, i, 0)),
                  pl.BlockSpec((1, halo, d), lambda bi, i: (bi, jnp.maximum(i * r - 1, 0), 0)),
                  pl.BlockSpec((1, halo, d), lambda bi, i: (bi, jnp.minimum((i + 1) * r, nblk - 1), 0)),
                  pl.BlockSpec((1, 6, d), mod_map),
                  pl.BlockSpec((1, d), const2),
                  pl.BlockSpec((d, n), const2),
                  pl.BlockSpec(conv_w.shape, const2),
                  pl.BlockSpec(hsum.shape, const2)],
        out_specs=out_specs,
        out_shape=out_shape,
        compiler_params=_cparams(2),
        name="inproj",
    )(x, x, x, mod3, g_norm.reshape(1, d), w, conv_w, hsum)


def _rope(x, cos, sin, lane):
    swapped = jnp.where((lane % 32) < 16, pltpu.roll(x, LANES - 16, 1), pltpu.roll(x, 16, 1))
    return x * cos + swapped * sin


def _attn_kernel(a_ref, kvc_ref, cos_ref, sin_ref, qg_ref, kg_ref, sink_ref, havg_ref,
                 o_ref, k_s, vt_s, *, seq, ctx):
    n = pl.program_id(1)
    havg = havg_ref[...]
    prep_rows = 256
    ctx_blocks = ctx // ATTN_BLOCK

    @pl.when(n == 0)
    def _prep():
        kg = kg_ref[...]
        kc = kvc_ref[0, :, 0:KV_WIDTH].astype(F32)
        ms = _dot((kc * kc).astype(BF16), havg)
        k_s[0:ctx, :] = (kc * lax.rsqrt(ms + EPS) * kg).astype(BF16)
        vc = kvc_ref[0, :, KV_WIDTH:2 * KV_WIDTH].astype(F32)
        for j in range(ctx_blocks):
            vt_s[j] = vc[j * ATTN_BLOCK:(j + 1) * ATTN_BLOCK, :].T.astype(BF16)
        lane = lax.broadcasted_iota(jnp.int32, (prep_rows, LANES), 1)
        for r0 in range(0, seq, prep_rows):
            kx = a_ref[0, r0:r0 + prep_rows, _OFF_AK:_OFF_AK + KV_WIDTH].astype(F32)
            ms = _dot((kx * kx).astype(BF16), havg)
            kx = kx * lax.rsqrt(ms + EPS) * kg
            kx = _rope(kx, cos_ref[r0:r0 + prep_rows, :], sin_ref[r0:r0 + prep_rows, :], lane)
            k_s[ctx + r0:ctx + r0 + prep_rows, :] = kx.astype(BF16)
            vx = a_ref[0, r0:r0 + prep_rows, _OFF_AV:_OFF_AV + KV_WIDTH].astype(F32)
            for j in range(prep_rows // ATTN_BLOCK):
                vt_s[ctx_blocks + r0 // ATTN_BLOCK + j] = vx[j * ATTN_BLOCK:(j + 1) * ATTN_BLOCK, :].T.astype(BF16)

    q0 = pl.multiple_of(n * ATTN_BLOCK, ATTN_BLOCK)
    lane = lax.broadcasted_iota(jnp.int32, (ATTN_BLOCK, LANES), 1)
    cos = cos_ref[pl.ds(q0, ATTN_BLOCK), :]
    sin = sin_ref[pl.ds(q0, ATTN_BLOCK), :]
    qg = qg_ref[...]
    q_parts = []
    for j in range(ATTN_WIDTH // LANES):
        qx = a_ref[0, pl.ds(q0, ATTN_BLOCK), j * LANES:(j + 1) * LANES].astype(F32)
        ms = _dot((qx * qx).astype(BF16), havg)
        qx = qx * lax.rsqrt(ms + EPS) * qg
        q_parts.append(_rope(qx, cos, sin, lane) * (HEAD_DIM ** -0.5))

    qts = [qp.T.astype(BF16) for qp in q_parts]

    n_band = 3
    n_win = n_band * ATTN_BLOCK
    n_keys = n_win + ctx
    blk0 = jnp.clip(n - 1, 0, seq // ATTN_BLOCK - n_band)
    start = pl.multiple_of(blk0 * ATTN_BLOCK, ATTN_BLOCK)
    key_row = lax.broadcasted_iota(jnp.int32, (n_keys, ATTN_BLOCK), 0)
    qpos = q0 + lax.broadcasted_iota(jnp.int32, (n_keys, ATTN_BLOCK), 1)
    valid = (jnp.abs(qpos - (start + key_row)) <= WINDOW) | (key_row >= n_win)
    valid = jnp.concatenate([valid] * ATTN_GROUP, axis=1)
    k_all = jnp.concatenate([k_s[pl.ds(ctx + start, n_win), :], k_s[0:ctx, :]], axis=0)
    vt_loc = vt_s[pl.ds(ctx_blocks + blk0, n_band)]
    vt_all = jnp.concatenate([vt_loc[j] for j in range(n_band)] + [vt_s[j] for j in range(ctx_blocks)], axis=1)
    zeros = jnp.zeros((HEAD_DIM, ATTN_BLOCK), BF16)

    groups = range(ATTN_KV_HEADS)
    rhss, sink_rows = [], []
    for g in groups:
        cols = []
        for r in range(ATTN_GROUP):
            h = g * ATTN_GROUP + r
            piece = qts[h // 2][(h % 2) * HEAD_DIM:(h % 2 + 1) * HEAD_DIM, :]
            cols.append(jnp.concatenate([piece, zeros] if g == 0 else [zeros, piece], axis=0))
        rhss.append(jnp.concatenate(cols, axis=1))
        sink_rows.append(jnp.concatenate([sink_ref[g * ATTN_GROUP + r:g * ATTN_GROUP + r + 1, :]
                                          for r in range(ATTN_GROUP)], axis=1))
    ss = [jnp.where(valid, _dot(k_all, rhs), NEG_INF) for rhs in rhss]
    maxes = [jnp.maximum(jnp.max(s, axis=0, keepdims=True), sink) for s, sink in zip(ss, sink_rows)]
    ps = [jnp.exp(s - m) for s, m in zip(ss, maxes)]
    invs = [1.0 / (jnp.sum(p, axis=0, keepdims=True) + jnp.exp(sink - m)) for p, sink, m in zip(ps, sink_rows, maxes)]
    p_bfs = [p.astype(BF16) for p in ps]
    pieces = []
    for g in groups:
        vt_g = vt_all[g * HEAD_DIM:(g + 1) * HEAD_DIM, :]
        for pair in range(ATTN_GROUP // 2):
            lanes = slice(pair * MXU_TILE, (pair + 1) * MXU_TILE)
            ot = _dot(vt_g, p_bfs[g][:, lanes]) * invs[g][:, lanes]
            pieces += [ot[:, 0:ATTN_BLOCK], ot[:, ATTN_BLOCK:2 * ATTN_BLOCK]]
    outs = [jnp.concatenate([pieces[2 * j], pieces[2 * j + 1]], axis=0).T for j in range(ATTN_Q_HEADS // 2)]
    o_ref[0] = jnp.concatenate(outs, axis=1).astype(o_ref.dtype)


def _attention(a_lat, kv_ctx, cos, sin, qg, kg, sink, havg):
    b, s, wa = a_lat.shape
    ctx = kv_ctx.shape[1]
    nb = s // ATTN_BLOCK
    const2 = lambda bi, n: (0, 0)
    return pl.pallas_call(
        functools.partial(_attn_kernel, seq=s, ctx=ctx),
        grid=(b, nb),
        in_specs=[pl.BlockSpec((1, s, wa), lambda bi, n: (bi, 0, 0)),
                  pl.BlockSpec((1, ctx, 2 * KV_WIDTH), lambda bi, n: (bi, 0, 0)),
                  pl.BlockSpec((s, LANES), const2),
                  pl.BlockSpec((s, LANES), const2),
                  pl.BlockSpec((1, LANES), const2),
                  pl.BlockSpec((1, LANES), const2),
                  pl.BlockSpec((ATTN_Q_HEADS, LANES), const2),
                  pl.BlockSpec((LANES, LANES), const2)],
        out_specs=pl.BlockSpec((1, ATTN_BLOCK, ATTN_WIDTH), lambda bi, n: (bi, n, 0)),
        out_shape=jax.ShapeDtypeStruct((b, s, ATTN_WIDTH), BF16),
        scratch_shapes=[pltpu.VMEM((ctx + s, KV_WIDTH), BF16),
                        pltpu.VMEM(((ctx + s) // ATTN_BLOCK, KV_WIDTH, ATTN_BLOCK), BF16)],
        compiler_params=_cparams(2),
        name="attn",
    )(a_lat, kv_ctx, cos, sin, qg, kg, sink, havg)


def _bdiag(x_lane, bmask):
    return jnp.where(bmask, jnp.concatenate([x_lane] * HEADS_PER_TILE, axis=0), jnp.zeros((), x_lane.dtype))


def _unit_tri_inverses(a_list, eye_l, sub_mask, bmask):
    ads = [jnp.where(sub_mask, a, 0.0) for a in a_list]
    aos = [jnp.where(sub_mask, 0.0, a).astype(BF16) for a in a_list]
    ps = [eye_l - ad for ad in ads]
    pws = [ad.astype(BF16) for ad in ads]
    n_levels = int(np.log2(DN_SUB))
    for level in range(n_levels):
        rhss = [_bdiag(pw, bmask) for pw in pws]
        if level == 0:
            pws = [_dot(pw, rhs).astype(BF16) for pw, rhs in zip(pws, rhss)]
        elif level < n_levels - 1:
            boths = [_dot(jnp.concatenate([pw, p.astype(BF16)], axis=0), rhs) for pw, p, rhs in zip(pws, ps, rhss)]
            pws = [both[0:CHUNK].astype(BF16) for both in boths]
            ps = [p + both[CHUNK:2 * CHUNK] for p, both in zip(ps, boths)]
        else:
            ps = [p + _dot(p.astype(BF16), rhs) for p, rhs in zip(ps, rhss)]
    xs = [p.astype(BF16) for p in ps]
    bs = [_dot(x, _bdiag(ao, bmask)) for x, ao in zip(xs, aos)]
    b_rhss = [_bdiag(b.astype(BF16), bmask) for b in bs]
    b2s = [_dot(b.astype(BF16), rhs) for b, rhs in zip(bs, b_rhss)]
    b3s = [_dot(b2.astype(BF16), rhs) for b2, rhs in zip(b2s, b_rhss)]
    assert CHUNK // DN_SUB == 4
    qs = [(eye_l - b + b2 - b3).astype(BF16) for b, b2, b3 in zip(bs, b2s, b3s)]
    return [_dot(q, _bdiag(x, bmask)) for q, x in zip(qs, xs)]


def _dn_factors(ab, d, arow, dtrow, exp2_ref, tri_ref, eye_t):
    lane = lax.broadcasted_iota(jnp.int32, ab.shape, 1)
    z = ab + dtrow
    softplus = jnp.maximum(z, 0.0) + jnp.log(1.0 + jnp.exp(-jnp.abs(z)))
    is_g = (lane % 32) < 16
    x = jnp.where(is_g, -arow * softplus, _sigmoid(ab))
    x_hi = x.astype(BF16)
    x_lo = (x - x_hi.astype(F32)).astype(BF16)
    xhl = jnp.where(lane < 32, x_hi, x_lo)[:, 0:64]
    y = _dot(xhl, exp2_ref[d])
    ge = y[:, 0:DN_WIDTH]
    be = y[:, DN_WIDTH:2 * DN_WIDTH]
    ge_hi = ge.astype(BF16)
    ge_lo = (ge - ge_hi.astype(F32)).astype(BF16)
    gi = _dot(tri_ref[d], jnp.concatenate([ge_hi, ge_lo], axis=0))
    gj = jnp.sum(gi * eye_t, axis=0, keepdims=True)
    return be, gi, jnp.broadcast_to(gj, gi.shape)


def _block_mask():
    return (lax.broadcasted_iota(jnp.int32, (MXU_TILE, MXU_TILE), 0) // HEAD_DIM
            == lax.broadcasted_iota(jnp.int32, (MXU_TILE, MXU_TILE), 1) // HEAD_DIM)


def _dnprep_kernel(*refs, cb, want_o):
    if want_o:
        q_ref, k_ref, v_ref, ab_ref = refs[:4]
        refs = refs[4:]
    else:
        q_ref = None
        k_ref, v_ref, ab_ref = refs[:3]
        refs = refs[3:]
    arow_ref, dtrow_ref, exp2_ref, tri_ref, w_o, uv_o, kt_o, dl_o = refs[:8]
    qd_o, in_o = refs[8:10] if want_o else (None, None)
    n_tiles = DN_WIDTH // MXU_TILE
    row = lax.broadcasted_iota(jnp.int32, (CHUNK, MXU_TILE), 0)
    colj = lax.broadcasted_iota(jnp.int32, (CHUNK, MXU_TILE), 1) % HEAD_DIM
    eye_l = (row == colj).astype(F32)
    eye_t = jnp.concatenate([eye_l] * n_tiles, axis=1)
    bmask = _block_mask()
    arow = arow_ref[...]
    dtrow = dtrow_ref[...]

    fac = {}
    shared = {}
    for c in range(cb):
        rows = slice(c * CHUNK, (c + 1) * CHUNK)
        ab = ab_ref[0, rows, :]
        for d in range(N_DIR):
            fac[(c, d)] = _dn_factors(ab, d, arow, dtrow, exp2_ref, tri_ref, eye_t)
        for g in range(n_tiles):
            lanes = slice(g * MXU_TILE, (g + 1) * MXU_TILE)
            k_l = k_ref[0, rows, lanes]
            kbd = _bdiag(k_l, bmask)
            if want_o:
                kq = _dot_nt(jnp.concatenate([k_l, q_ref[0, rows, lanes]], axis=0), kbd)
                shared[(c, g)] = (kq[0:CHUNK], kq[CHUNK:2 * CHUNK])
            else:
                shared[(c, g)] = (_dot_nt(k_l, kbd), None)

    units = [(c, g, d) for c in range(cb) for g in range(n_tiles) for d in range(N_DIR)]
    decs = []
    for c, g, d in units:
        lanes = slice(g * MXU_TILE, (g + 1) * MXU_TILE)
        be, gi, gj = fac[(c, d)]
        lower = (row > colj) if d == 0 else (row < colj)
        decs.append(jnp.where(lower, jnp.exp(jnp.where(lower, gi[:, lanes] - gj[:, lanes], 0.0)), 0.0))
    a_list = [fac[(c, d)][0][:, g * MXU_TILE:(g + 1) * MXU_TILE] * dec * shared[(c, g)][0]
              for (c, g, d), dec in zip(units, decs)]
    tinvs = _unit_tri_inverses(a_list, eye_l, (row // DN_SUB) == (colj // DN_SUB), bmask)

    for (c, g, d), dec, tinv in zip(units, decs, tinvs):
        rows = slice(c * CHUNK, (c + 1) * CHUNK)
        lanes = slice(g * MXU_TILE, (g + 1) * MXU_TILE)
        be, gi, _ = fac[(c, d)]
        be, gi = be[:, lanes], gi[:, lanes]
        last = CHUNK - 1 if d == 0 else 0
        e_g = jnp.exp(gi)
        gl_row = gi[last:last + 1, :]
        kf = k_ref[0, rows, lanes].astype(F32)
        vf = v_ref[0, rows, lanes].astype(F32)
        rhs = jnp.concatenate([_bdiag((be * e_g * kf).astype(BF16), bmask), _bdiag((be * vf).astype(BF16), bmask)],
                              axis=1)
        wu = _dot(tinv.astype(BF16), rhs)
        w_o[0, d, rows, lanes] = wu[:, 0:MXU_TILE].astype(BF16)
        uv_o[0, d, rows, lanes] = wu[:, MXU_TILE:2 * MXU_TILE]
        kt_o[0, d, rows, lanes] = (jnp.exp(gl_row - gi) * kf).astype(BF16)
        dl_o[0, d, c, :, lanes] = jnp.exp(gl_row)
        if want_o:
            qd_o[0, d, rows, lanes] = (e_g * q_ref[0, rows, lanes].astype(F32)).astype(BF16)
            in_o[0, d, rows, lanes] = ((dec + eye_l) * shared[(c, g)][1]).astype(BF16)


def _dnprep(q, k, v, ab, arow, dtrow, exp2, tri, cb):
    want_o = q is not None
    b, t, w = k.shape
    tb = cb * CHUNK
    tok = lambda bi, i: (bi, i, 0)
    const2 = lambda bi, i: (0, 0)
    const3 = lambda bi, i: (0, 0, 0)
    dir_tok = lambda bi, i: (bi, 0, i, 0)
    data = ([q] if want_o else []) + [k, v]
    in_specs = ([pl.BlockSpec((1, tb, w), tok)] * len(data) + [pl.BlockSpec((1, tb, LANES), tok),
                pl.BlockSpec((1, LANES), const2), pl.BlockSpec((1, LANES), const2),
                pl.BlockSpec(exp2.shape, const3), pl.BlockSpec(tri.shape, const3)])
    big = lambda dt: jax.ShapeDtypeStruct((b, N_DIR, t, w), dt)
    big_spec = pl.BlockSpec((1, N_DIR, tb, w), dir_tok)
    out_shape = [big(BF16), big(F32), big(BF16), jax.ShapeDtypeStruct((b, N_DIR, t // CHUNK, 1, w), F32)]
    out_specs = [big_spec, big_spec, big_spec, pl.BlockSpec((1, N_DIR, cb, 1, w), lambda bi, i: (bi, 0, i, 0, 0))]
    if want_o:
        out_shape += [big(BF16), big(BF16)]
        out_specs += [big_spec, big_spec]
    return pl.pallas_call(
        functools.partial(_dnprep_kernel, cb=cb, want_o=want_o),
        grid=(b, t // tb),
        in_specs=in_specs,
        out_specs=out_specs,
        out_shape=out_shape,
        compiler_params=_cparams(2),
        name="dnprep",
    )(*data, ab, arow, dtrow, exp2, tri)


def _dnscan_kernel(*refs, n_chunk, bb, want_o, have_s0, want_s):
    n_in = 6 if want_o else 4
    dir_refs = [refs[0:n_in], refs[n_in:2 * n_in]]
    pos = 2 * n_in
    s0_ref = refs[pos] if have_s0 else None
    pos += int(have_s0)
    o_refs = refs[pos:pos + N_DIR] if want_o else None
    pos += N_DIR if want_o else 0
    sout_ref = refs[pos] if want_s else None
    pos += int(want_s)
    s_scr = refs[pos]
    n_tiles = DN_WIDTH // MXU_TILE
    i = pl.program_id(1)

    @pl.when(i == 0)
    def _init():
        if have_s0:
            s_scr[...] = s0_ref[...]
        else:
            s_scr[...] = jnp.zeros_like(s_scr)

    bmask = _block_mask()
    chains = [(bi, d, g) for bi in range(bb) for d in range(N_DIR) for g in range(n_tiles)]

    def body(j, carry):
        cidx = (j, n_chunk - 1 - j)
        r0s = [pl.multiple_of(cidx[d] * CHUNK, CHUNK) for d in range(N_DIR)]
        s_olds, r1s = [], []
        for bi, d, g in chains:
            lanes = slice(g * MXU_TILE, (g + 1) * MXU_TILE)
            w = dir_refs[d][0][bi, 0, pl.ds(r0s[d], CHUNK), lanes]
            if want_o:
                w = jnp.concatenate([w, dir_refs[d][4][bi, 0, pl.ds(r0s[d], CHUNK), lanes]], axis=0)
            s_old = s_scr[bi, d * n_tiles + g]
            s_olds.append(s_old)
            r1s.append(_dot(w, s_old.astype(BF16)))
        u_bfs = []
        for (bi, d, g), r1 in zip(chains, r1s):
            lanes = slice(g * MXU_TILE, (g + 1) * MXU_TILE)
            u_bfs.append((dir_refs[d][1][bi, 0, pl.ds(r0s[d], CHUNK), lanes] - r1[0:CHUNK]).astype(BF16))
        for (bi, d, g), r1, u_bf, s_old in zip(chains, r1s, u_bfs, s_olds):
            lanes = slice(g * MXU_TILE, (g + 1) * MXU_TILE)
            kt = dir_refs[d][2][bi, 0, pl.ds(r0s[d], CHUNK), lanes]
            ds = jnp.where(bmask, _dot_tn(kt, u_bf), 0.0)
            dl = dir_refs[d][3][bi, 0, cidx[d]][:, lanes]
            s_scr[bi, d * n_tiles + g] = s_old * dl + ds
            if want_o:
                intra = dir_refs[d][5][bi, 0, pl.ds(r0s[d], CHUNK), lanes]
                o_refs[d][bi, pl.ds(r0s[d], CHUNK), lanes] = r1[CHUNK:2 * CHUNK] + _dot(intra, _bdiag(u_bf, bmask))
        return carry

    lax.fori_loop(0, n_chunk, body, 0)

    if want_s:
        @pl.when(i == pl.num_programs(1) - 1)
        def _fin():
            sout_ref[...] = s_scr[...]


def _dnscan(prep, s0, tb, bb, want_s):
    want_o = len(prep) == 6
    b, _, t, w = prep[0].shape
    n_t = t // tb
    n_chunk = tb // CHUNK
    n_chain = N_DIR * (w // MXU_TILE)

    def specs(d):
        blk = (lambda bi, i: i) if d == 0 else (lambda bi, i: n_t - 1 - i)
        big = pl.BlockSpec((bb, 1, tb, w), lambda bi, i: (bi, d, blk(bi, i), 0))
        dl = pl.BlockSpec((bb, 1, n_chunk, 1, w), lambda bi, i: (bi, d, blk(bi, i), 0, 0))
        return [big, big, big, dl] + ([big, big] if want_o else [])

    in_specs = specs(0) + specs(1)
    args = list(prep) + list(prep)
    state_spec = pl.BlockSpec((bb, n_chain, MXU_TILE, MXU_TILE), lambda bi, i: (bi, 0, 0, 0))
    if s0 is not None:
        in_specs.append(state_spec)
        args.append(s0)
    out_shape, out_specs = [], []
    if want_o:
        out_shape += [jax.ShapeDtypeStruct((b, t, w), F32)] * N_DIR
        out_specs += [pl.BlockSpec((bb, tb, w), lambda bi, i: (bi, i, 0)),
                      pl.BlockSpec((bb, tb, w), lambda bi, i: (bi, n_t - 1 - i, 0))]
    if want_s:
        out_shape.append(jax.ShapeDtypeStruct((b, n_chain, MXU_TILE, MXU_TILE), F32))
        out_specs.append(state_spec)
    return pl.pallas_call(
        functools.partial(_dnscan_kernel, n_chunk=n_chunk, bb=bb, want_o=want_o, have_s0=s0 is not None,
                          want_s=want_s),
        grid=(b // bb, n_t),
        in_specs=in_specs,
        out_specs=out_specs,
        out_shape=out_shape,
        scratch_shapes=[pltpu.VMEM((bb, n_chain, MXU_TILE, MXU_TILE), F32)],
        compiler_params=_cparams(2),
        name="dnscan",
    )(*args)


def _tail_kernel(x_ref, mod_ref, ya_ref, odf_ref, odb_ref, z_ref, gate_ref, dng_ref, havg_ref, wba_ref, wbd_ref,
                 wo_ref, gn2_ref, w1_ref, w2_ref, o_ref, *, ff_chunk):
    havg = havg_ref[...]
    dng = dng_ref[...]
    yd_parts = []
    for j in range(DN_WIDTH // MXU_TILE):
        sl = slice(j * MXU_TILE, (j + 1) * MXU_TILE)
        od = odf_ref[0, :, sl] + odb_ref[0, :, sl]
        ms = _dot((od * od).astype(BF16), havg)
        z = z_ref[0, :, sl].astype(F32)
        yd_parts.append((od * lax.rsqrt(ms + EPS) * dng * (z * _sigmoid(z))).astype(BF16))
    yd = jnp.concatenate(yd_parts, axis=1)
    ga = gate_ref[0, :, 0:D_MODEL].astype(F32)
    gd = gate_ref[0, :, D_MODEL:2 * D_MODEL].astype(F32)
    y = _sigmoid(ga) * _dot(ya_ref[0], wba_ref[...]) + _sigmoid(gd) * _dot(yd, wbd_ref[...])
    br = _dot(y.astype(BF16), wo_ref[...])
    mod = mod_ref[0]
    out1 = x_ref[0] + mod[2:3] * br
    ms2 = jnp.mean(out1 * out1, axis=-1, keepdims=True)
    hm = (out1 * lax.rsqrt(ms2 + EPS) * (gn2_ref[...] * (1.0 + mod[4:5])) + mod[3:4]).astype(BF16)
    acc = None
    for j in range(D_FF // ff_chunk):
        a = jnp.maximum(_dot(hm, w1_ref[:, j * ff_chunk:(j + 1) * ff_chunk]), 0.0)
        part = _dot((a * a).astype(BF16), w2_ref[j * ff_chunk:(j + 1) * ff_chunk, :])
        acc = part if acc is None else acc + part
    o_ref[0] = out1 + mod[5:6] * acc


def _resident(shape):
    return pl.BlockSpec(shape, lambda bi, i: (0,) * len(shape), pipeline_mode=pl.Buffered(1))


def _tail(x, mod3, y_attn, o_df, o_db, z, gates, dng, havg, wba, wbd, wo, gn2, w1, w2, tm):
    b, t, d = x.shape
    tok = lambda bi, i: (bi, i, 0)
    return pl.pallas_call(
        functools.partial(_tail_kernel, ff_chunk=1024),
        grid=(b, t // tm),
        in_specs=[pl.BlockSpec((1, tm, d), tok),
                  pl.BlockSpec((1, 6, d), lambda bi, i: (bi, 0, 0)),
                  pl.BlockSpec((1, tm, ATTN_WIDTH), tok),
                  pl.BlockSpec((1, tm, DN_WIDTH), tok),
                  pl.BlockSpec((1, tm, DN_WIDTH), tok),
                  pl.BlockSpec((1, tm, DN_WIDTH), tok),
                  pl.BlockSpec((1, tm, 2 * d), tok),
                  _resident((1, MXU_TILE)),
                  _resident((MXU_TILE, MXU_TILE)),
                  _resident(wba.shape),
                  _resident(wbd.shape),
                  _resident(wo.shape),
                  _resident((1, d)),
                  _resident(w1.shape),
                  _resident(w2.shape)],
        out_specs=pl.BlockSpec((1, tm, d), tok),
        out_shape=jax.ShapeDtypeStruct((b, t, d), F32),
        compiler_params=_cparams(2),
        name="tail",
    )(x, mod3, y_attn, o_df, o_db, z, gates, dng, havg, wba, wbd, wo, gn2, w1, w2)


def _head_avg(n, scale):
    idx = np.arange(n) // HEAD_DIM
    return jnp.asarray((idx[:, None] == idx[None, :]).astype(np.float32) * scale, BF16)


def _dn_expand_matrix():
    n = N_DIR * DN_HEADS
    m = np.zeros((N_DIR, 4 * n, 2 * DN_WIDTH), np.float32)
    for d in range(N_DIR):
        for part in range(2):
            for h in range(DN_HEADS):
                idx = d * DN_HEADS + h
                m[d, part * 2 * n + idx, h * HEAD_DIM:(h + 1) * HEAD_DIM] = 1.0
                m[d, part * 2 * n + n + idx, DN_WIDTH + h * HEAD_DIM:DN_WIDTH + (h + 1) * HEAD_DIM] = 1.0
    return jnp.asarray(m, BF16)


def _tri_matrices():
    i = np.arange(CHUNK)
    low = (i[:, None] >= i[None, :]).astype(np.float32)
    up = (i[:, None] <= i[None, :]).astype(np.float32)
    return jnp.asarray(np.stack([np.concatenate([low, low], axis=1), np.concatenate([up, up], axis=1)]), BF16)


def _rope_tables(seq):
    half = HEAD_DIM // 2
    n_freq = half // 2
    freqs = ROPE_BASE ** (-jnp.arange(n_freq, dtype=F32) / n_freq)
    pos = jnp.arange(seq)
    ang_r = (pos // GRID_W).astype(F32)[:, None] * freqs
    ang_c = (pos % GRID_W).astype(F32)[:, None] * freqs
    cos = jnp.concatenate([jnp.cos(ang_r)] * 2 + [jnp.cos(ang_c)] * 2, axis=1)
    sin = jnp.concatenate([-jnp.sin(ang_r), jnp.sin(ang_r), -jnp.sin(ang_c), jnp.sin(ang_c)], axis=1)
    reps = LANES // HEAD_DIM
    return jnp.tile(cos, (1, reps)), jnp.tile(sin, (1, reps))


def _pad_cols(w, n):
    return jnp.pad(w, ((0, 0), (0, n - w.shape[1])))


def kernel(x, c, ctx, c_ctx, w_ada, b_ada, g_norm1, w_in, q_norm_g, k_norm_g, attn_sink, conv_w, a_log, dt_bias,
           dn_norm_g, w_br_attn, w_br_dn, w_out, g_norm2, w_mlp1, w_mlp2):
    depth = w_ada.shape[0]
    assert depth == 1, "single-layer trunk only"
    b, s, d = x.shape
    n_ctx = ctx.shape[1]
    assert d == D_MODEL and w_in.shape[-1] == _IN_WIDTH
    assert s >= 3 * ATTN_BLOCK and s % ATTN_BLOCK == 0 and s % CHUNK == 0 and n_ctx % CHUNK == 0
    out_dtype = x.dtype
    w_in0 = w_in[0]

    mod_rows = 16
    cc = jnp.concatenate([c.astype(F32), c_ctx.astype(F32)[None], jnp.zeros((mod_rows - b - 1, d), F32)], axis=0)
    mod = _ada(cc, w_ada[0], b_ada[0])
    mod3 = mod.reshape(mod_rows, 6, d)

    ab_cols = jnp.concatenate([w_in0[:, _OFF_DA:_OFF_GA]] * 2, axis=1)
    w_lat = jnp.concatenate([w_in0[:, :_OFF_DA], w_in0[:, _OFF_GA:], _pad_cols(ab_cols, LANES)], axis=1).astype(BF16)
    hsum = _head_avg(MXU_TILE, 1.0)
    segs_lat = ((0, _OFF_DQ, None),
                (_OFF_DQ, DN_WIDTH, (0, True, True)), (_OFF_DK, DN_WIDTH, (DN_WIDTH, True, False)),
                (_OFF_DV, DN_WIDTH, (2 * DN_WIDTH, False, False)),
                (_OFF_DZ, DN_WIDTH, None), (_OFF_DA, 2 * D_MODEL, None), (_OFF_DA + 2 * D_MODEL, LANES, None))
    a_lat, q_d, k_d, v_d, z_lat, gates, ab_lat = _inproj(x, mod3, None, g_norm1[0], w_lat, conv_w[0], hsum, segs_lat,
                                                         (BF16, BF16, BF16, BF16, BF16, BF16, F32), tm=512)
    w_ctx = jnp.concatenate([w_in0[:, _OFF_AK:_OFF_DQ], w_in0[:, _OFF_DK:_OFF_DZ], _pad_cols(ab_cols, LANES)],
                            axis=1).astype(BF16)
    segs_ctx = ((0, 2 * KV_WIDTH, None),
                (2 * KV_WIDTH, DN_WIDTH, (0, True, False)), (2 * KV_WIDTH + DN_WIDTH, DN_WIDTH, (DN_WIDTH, False, False)),
                (2 * KV_WIDTH + 2 * DN_WIDTH, LANES, None))
    kv_ctx, k_dc, v_dc, ab_ctx = _inproj(ctx, mod3, b, g_norm1[0], w_ctx, conv_w[0][:, DN_WIDTH:], hsum, segs_ctx,
                                         (BF16, BF16, BF16, F32), tm=n_ctx)

    cos, sin = _rope_tables(s)
    reps = LANES // HEAD_DIM
    y_attn = _attention(a_lat, kv_ctx, cos, sin,
                        jnp.tile(q_norm_g[0].astype(F32), reps)[None], jnp.tile(k_norm_g[0].astype(F32), reps)[None],
                        jnp.broadcast_to(attn_sink[0].astype(F32)[:, None], (ATTN_Q_HEADS, LANES)),
                        _head_avg(LANES, 1.0 / HEAD_DIM))

    n_gate = N_DIR * DN_HEADS
    arow = _pad_cols(jnp.tile(jnp.concatenate([jnp.exp(a_log[0]).reshape(1, n_gate), jnp.zeros((1, n_gate), F32)],
                                              axis=1), (1, 2)), LANES)
    dtrow = _pad_cols(jnp.tile(jnp.concatenate([dt_bias[0].reshape(1, n_gate), jnp.zeros((1, n_gate), F32)],
                                               axis=1), (1, 2)), LANES)
    exp2, tri = _dn_expand_matrix(), _tri_matrices()
    prep_ctx = _dnprep(None, k_dc, v_dc, ab_ctx, arow, dtrow, exp2, tri, cb=4)
    prep_lat = _dnprep(q_d, k_d, v_d, ab_lat, arow, dtrow, exp2, tri, cb=4)
    bb = 2 if b % 2 == 0 else 1
    (s_ctx,) = _dnscan(prep_ctx, None, tb=n_ctx, bb=bb, want_s=True)
    o_df, o_db = _dnscan(prep_lat, s_ctx, tb=512, bb=bb, want_s=False)

    out = _tail(x, mod3, y_attn, o_df, o_db, z_lat, gates,
                jnp.tile(dn_norm_g[0].astype(F32), HEADS_PER_TILE)[None], _head_avg(MXU_TILE, 1.0 / HEAD_DIM),
                w_br_attn[0].astype(BF16), w_br_dn[0].astype(BF16), w_out[0].astype(BF16),
                g_norm2[0].reshape(1, d), w_mlp1[0].astype(BF16), w_mlp2[0].astype(BF16), tm=256)
    return out.astype(out_dtype)
```

```python
import functools

import numpy as np
import jax
import jax.numpy as jnp
from jax import lax
from jax.experimental import pallas as pl
from jax.experimental.pallas import tpu as pltpu

F32 = jnp.float32
BF16 = jnp.bfloat16

D_MODEL = 1024
GRID_W = 64
HEAD_DIM = 64
ATTN_Q_HEADS = 8
ATTN_KV_HEADS = 2
ATTN_GROUP = ATTN_Q_HEADS // ATTN_KV_HEADS
WINDOW = 128
ATTN_BLOCK = 128
ROPE_BASE = 10000.0
DN_HEADS = 8
CONV_W = 3
CHUNK = 64
N_DIR = 2
D_FF = 4 * D_MODEL
EPS = 1e-6
NEG_INF = -1e30
LOG2_E = float(np.log2(np.e))

ATTN_WIDTH = ATTN_Q_HEADS * HEAD_DIM
KV_WIDTH = ATTN_KV_HEADS * HEAD_DIM
DN_WIDTH = DN_HEADS * HEAD_DIM
LANES = 128
MXU_TILE = 256
HEADS_PER_TILE = MXU_TILE // HEAD_DIM
DN_SUB = 16
VMEM_LIMIT = 56 * 1024 * 1024

_OFF_AQ = 0
_OFF_AK = _OFF_AQ + ATTN_WIDTH
_OFF_AV = _OFF_AK + KV_WIDTH
_OFF_DQ = _OFF_AV + KV_WIDTH
_OFF_DK = _OFF_DQ + DN_WIDTH
_OFF_DV = _OFF_DK + DN_WIDTH
_OFF_DZ = _OFF_DV + DN_WIDTH
_OFF_DA = _OFF_DZ + DN_WIDTH
_OFF_DB = _OFF_DA + N_DIR * DN_HEADS
_OFF_GA = _OFF_DB + N_DIR * DN_HEADS
_OFF_GD = _OFF_GA + D_MODEL
_IN_WIDTH = _OFF_GD + D_MODEL


def _sigmoid(x):
    return 0.5 * jnp.tanh(0.5 * x) + 0.5


def _dot(a, b):
    return jnp.dot(a, b, preferred_element_type=F32)


def _dot_nt(a, b):
    return lax.dot_general(a, b, (((1,), (1,)), ((), ())), preferred_element_type=F32)


def _dot_tn(a, b):
    return lax.dot_general(a, b, (((0,), (0,)), ((), ())), preferred_element_type=F32)


def _cparams(n_axes):
    return pltpu.CompilerParams(dimension_semantics=("arbitrary",) * n_axes, vmem_limit_bytes=VMEM_LIMIT)


def _ada_kernel(c_ref, w_ref, b_ref, o_ref):
    c = c_ref[...]
    s = c * _sigmoid(c)
    o_ref[...] = _dot(s.astype(BF16), w_ref[...].astype(BF16)) + b_ref[...]


def _ada(cc, w_ada, b_ada):
    rows, d = cc.shape
    n = w_ada.shape[1]
    tn = 1536
    return pl.pallas_call(
        _ada_kernel,
        grid=(n // tn,),
        in_specs=[pl.BlockSpec((rows, d), lambda j: (0, 0)),
                  pl.BlockSpec((d, tn), lambda j: (0, j)),
                  pl.BlockSpec((1, tn), lambda j: (0, j))],
        out_specs=pl.BlockSpec((rows, tn), lambda j: (0, j)),
        out_shape=jax.ShapeDtypeStruct((rows, n), F32),
        compiler_params=_cparams(1),
        name="ada",
    )(cc, w_ada, b_ada.reshape(1, n))


def _inproj_kernel(x_ref, xprev_ref, xnext_ref, mod_ref, g_ref, w_ref, cw_ref, hsum_ref, *out_refs, segs, tm, halo):
    i = pl.program_id(1)
    last = pl.num_programs(1) - 1
    mod = mod_ref[0]
    scale = g_ref[...] * (1.0 + mod[1:2])

    def norm_mod(v):
        ms = jnp.mean(v * v, axis=-1, keepdims=True)
        return (v * lax.rsqrt(ms + EPS) * scale + mod[0:1]).astype(BF16)

    h = norm_mod(x_ref[0])
    h_halo = norm_mod(jnp.concatenate([xprev_ref[0], xnext_ref[0]], axis=0))
    keep_prev = (i > 0).astype(F32)
    keep_next = (i < last).astype(F32)
    rows = lax.broadcasted_iota(jnp.int32, (tm, MXU_TILE), 0)
    hsum = hsum_ref[...]
    def plain_piece(o_ref, start, lo, width):
        o_ref[0, :, lo:lo + width] = _dot(h, w_ref[:, start + lo:start + lo + width]).astype(o_ref.dtype)

    def conv_piece(o_ref, start, lo, kind):
        conv_col, do_norm, is_q = kind
        w_cols = w_ref[:, start + lo:start + lo + MXU_TILE]
        p = _dot(h, w_cols)
        p_halo = _dot(h_halo, w_cols)
        p_prev = jnp.where(rows == 0, p_halo[halo - 1:halo] * keep_prev, pltpu.roll(p, 1, 0))
        p_next = jnp.where(rows == tm - 1, p_halo[halo:halo + 1] * keep_next, pltpu.roll(p, tm - 1, 0))
        cw = cw_ref[:, conv_col + lo:conv_col + lo + MXU_TILE]
        y = p_prev * cw[0:1] + p * cw[1:2] + p_next * cw[2:3]
        y = y * _sigmoid(y)
        if do_norm:
            y = y * lax.rsqrt(_dot((y * y).astype(BF16), hsum) + EPS)
            if is_q:
                y = y * (HEAD_DIM ** -0.5)
        o_ref[0, :, lo:lo + MXU_TILE] = y.astype(o_ref.dtype)

    plain, conv = [], []
    for o_ref, (start, size, kind) in zip(out_refs, segs):
        if kind is None:
            plain += [functools.partial(plain_piece, o_ref, start, lo, min(2 * MXU_TILE, size - lo))
                      for lo in range(0, size, 2 * MXU_TILE)]
        else:
            conv += [functools.partial(conv_piece, o_ref, start, lo, kind) for lo in range(0, size, MXU_TILE)]
    while plain or conv:
        if conv:
            conv.pop(0)()
        if plain:
            plain.pop(0)()


def _inproj(x, mod3, mod_row, g_norm, w, conv_w, hsum, segs, dtypes, tm):
    b, t, d = x.shape
    n = w.shape[1]
    halo = 8
    r = tm // halo
    nblk = t // halo
    if mod_row is None:
        mod_map = lambda bi, i: (bi, 0, 0)
    else:
        mod_map = lambda bi, i: (mod_row, 0, 0)
    const2 = lambda bi, i: (0, 0)
    out_shape = [jax.ShapeDtypeStruct((b, t, size), dt) for (_, size, _), dt in zip(segs, dtypes)]
    out_specs = [pl.BlockSpec((1, tm, size), lambda bi, i: (bi, i, 0)) for (_, size, _) in segs]
    return pl.pallas_call(
        functools.partial(_inproj_kernel, segs=segs, tm=tm, halo=halo),
        grid=(b, t // tm),
        in_specs=[pl.BlockSpec((1, tm, d), lambda bi, i: (bi, i, 0)),
                  pl.BlockSpec((1, halo, d), lambda bi, i: (bi, jnp.maximum(i * r - 1, 0), 0)),
                  pl.BlockSpec((1, halo, d), lambda bi, i: (bi, jnp.minimum((i + 1) * r, nblk - 1), 0)),
                  pl.BlockSpec((1, 6, d), mod_map),
                  pl.BlockSpec((1, d), const2),
                  pl.BlockSpec((d, n), const2),
                  pl.BlockSpec(conv_w.shape, const2),
                  pl.BlockSpec(hsum.shape, const2)],
        out_specs=out_specs,
        out_shape=out_shape,
        compiler_params=_cparams(2),
        name="inproj",
    )(x, x, x, mod3, g_norm.reshape(1, d), w, conv_w, hsum)


def _rope(x, cos, sin, lane):
    swapped = jnp.where((lane % 32) < 16, pltpu.roll(x, LANES - 16, 1), pltpu.roll(x, 16, 1))
    return x * cos + swapped * sin


def _attn_kernel(a_ref, kvc_ref, cos_ref, sin_ref, qg_ref, kg_ref, sink_ref, havg_ref,
                 o_ref, k_s, vt_s, *, seq, ctx, qb):
    n = pl.program_id(1)
    havg = havg_ref[...]
    prep_rows = 256
    ctx_blocks = ctx // ATTN_BLOCK

    @pl.when(n == 0)
    def _prep():
        kg = kg_ref[...]
        kc = kvc_ref[0, :, 0:KV_WIDTH].astype(F32)
        ms = _dot((kc * kc).astype(BF16), havg)
        k_s[0:ctx, :] = (kc * lax.rsqrt(ms + EPS) * kg).astype(BF16)
        vc = kvc_ref[0, :, KV_WIDTH:2 * KV_WIDTH].astype(F32)
        for j in range(ctx_blocks):
            vt_s[j] = vc[j * ATTN_BLOCK:(j + 1) * ATTN_BLOCK, :].T.astype(BF16)
        lane = lax.broadcasted_iota(jnp.int32, (prep_rows, LANES), 1)
        for r0 in range(0, seq, prep_rows):
            kx = a_ref[0, r0:r0 + prep_rows, _OFF_AK:_OFF_AK + KV_WIDTH].astype(F32)
            ms = _dot((kx * kx).astype(BF16), havg)
            kx = kx * lax.rsqrt(ms + EPS) * kg
            kx = _rope(kx, cos_ref[r0:r0 + prep_rows, :], sin_ref[r0:r0 + prep_rows, :], lane)
            k_s[ctx + r0:ctx + r0 + prep_rows, :] = kx.astype(BF16)
            vx = a_ref[0, r0:r0 + prep_rows, _OFF_AV:_OFF_AV + KV_WIDTH].astype(F32)
            for j in range(prep_rows // ATTN_BLOCK):
                vt_s[ctx_blocks + r0 // ATTN_BLOCK + j] = vx[j * ATTN_BLOCK:(j + 1) * ATTN_BLOCK, :].T.astype(BF16)

    lane = lax.broadcasted_iota(jnp.int32, (ATTN_BLOCK, LANES), 1)
    qg = qg_ref[...]
    n_band = 3
    n_win = n_band * ATTN_BLOCK
    n_keys = n_win + ctx
    key_row = lax.broadcasted_iota(jnp.int32, (n_win, ATTN_BLOCK), 0)
    q_lane = lax.broadcasted_iota(jnp.int32, (n_win, ATTN_BLOCK), 1)
    zeros = jnp.zeros((HEAD_DIM, ATTN_BLOCK), BF16)
    groups = range(ATTN_KV_HEADS)
    sink_rows = [jnp.concatenate([sink_ref[g * ATTN_GROUP + r:g * ATTN_GROUP + r + 1, :] for r in range(ATTN_GROUP)],
                                 axis=1) * LOG2_E for g in groups]

    units = []
    k_alls, vt_alls, valids, rhss = [], [], [], {}
    for sb in range(qb):
        nq = n * qb + sb
        q0 = pl.multiple_of(nq * ATTN_BLOCK, ATTN_BLOCK)
        cos = cos_ref[pl.ds(q0, ATTN_BLOCK), :]
        sin = sin_ref[pl.ds(q0, ATTN_BLOCK), :]
        qts = []
        for j in range(ATTN_WIDTH // LANES):
            qx = a_ref[0, pl.ds(q0, ATTN_BLOCK), j * LANES:(j + 1) * LANES].astype(F32)
            ms = _dot((qx * qx).astype(BF16), havg)
            qx = _rope(qx * lax.rsqrt(ms + EPS) * qg, cos, sin, lane) * (HEAD_DIM ** -0.5 * LOG2_E)
            qts.append(qx.T.astype(BF16))
        blk0 = jnp.clip(nq - 1, 0, seq // ATTN_BLOCK - n_band)
        start = pl.multiple_of(blk0 * ATTN_BLOCK, ATTN_BLOCK)
        valid = jnp.abs(q0 + q_lane - (start + key_row)) <= WINDOW
        valids.append(jnp.concatenate([valid] * ATTN_GROUP, axis=1))
        k_alls.append(jnp.concatenate([k_s[pl.ds(ctx + start, n_win), :], k_s[0:ctx, :]], axis=0))
        vt_loc = vt_s[pl.ds(ctx_blocks + blk0, n_band)]
        vt_alls.append(jnp.concatenate([vt_loc[j] for j in range(n_band)] + [vt_s[j] for j in range(ctx_blocks)],
                                       axis=1))
        for g in groups:
            cols = []
            for r in range(ATTN_GROUP):
                h = g * ATTN_GROUP + r
                piece = qts[h // 2][(h % 2) * HEAD_DIM:(h % 2 + 1) * HEAD_DIM, :]
                cols.append(jnp.concatenate([piece, zeros] if g == 0 else [zeros, piece], axis=0))
            rhss[(sb, g)] = jnp.concatenate(cols, axis=1)
            units.append((sb, g))

    ss = []
    for sb, g in units:
        s = _dot(k_alls[sb], rhss[(sb, g)])
        ss.append(jnp.concatenate([jnp.where(valids[sb], s[0:n_win], NEG_INF), s[n_win:n_keys]], axis=0))
    maxes = [jnp.maximum(jnp.max(s, axis=0, keepdims=True), sink_rows[g]) for s, (sb, g) in zip(ss, units)]
    ps = [jnp.exp2(s - m) for s, m in zip(ss, maxes)]
    invs = [1.0 / (jnp.sum(p, axis=0, keepdims=True) + jnp.exp2(sink_rows[g] - m))
            for p, m, (sb, g) in zip(ps, maxes, units)]
    p_bfs = [p.astype(BF16) for p in ps]
    pieces = {sb: [] for sb in range(qb)}
    for (sb, g), p_bf, inv in zip(units, p_bfs, invs):
        vt_g = vt_alls[sb][g * HEAD_DIM:(g + 1) * HEAD_DIM, :]
        for pair in range(ATTN_GROUP // 2):
            lanes = slice(pair * MXU_TILE, (pair + 1) * MXU_TILE)
            ot = _dot(vt_g, p_bf[:, lanes]) * inv[:, lanes]
            pieces[sb] += [ot[:, 0:ATTN_BLOCK], ot[:, ATTN_BLOCK:2 * ATTN_BLOCK]]
    for sb in range(qb):
        outs = [jnp.concatenate([pieces[sb][2 * j], pieces[sb][2 * j + 1]], axis=0).T
                for j in range(ATTN_Q_HEADS // 2)]
        o_ref[0, sb * ATTN_BLOCK:(sb + 1) * ATTN_BLOCK, :] = jnp.concatenate(outs, axis=1).astype(o_ref.dtype)


def _attention(a_lat, kv_ctx, cos, sin, qg, kg, sink, havg):
    b, s, wa = a_lat.shape
    ctx = kv_ctx.shape[1]
    nb = s // ATTN_BLOCK
    qb = max(q for q in (4, 2, 1) if nb % q == 0)
    const2 = lambda bi, n: (0, 0)
    return pl.pallas_call(
        functools.partial(_attn_kernel, seq=s, ctx=ctx, qb=qb),
        grid=(b, nb // qb),
        in_specs=[pl.BlockSpec((1, s, wa), lambda bi, n: (bi, 0, 0)),
                  pl.BlockSpec((1, ctx, 2 * KV_WIDTH), lambda bi, n: (bi, 0, 0)),
                  pl.BlockSpec((s, LANES), const2),
                  pl.BlockSpec((s, LANES), const2),
                  pl.BlockSpec((1, LANES), const2),
                  pl.BlockSpec((1, LANES), const2),
                  pl.BlockSpec((ATTN_Q_HEADS, LANES), const2),
                  pl.BlockSpec((LANES, LANES), const2)],
        out_specs=pl.BlockSpec((1, qb * ATTN_BLOCK, ATTN_WIDTH), lambda bi, n: (bi, n, 0)),
        out_shape=jax.ShapeDtypeStruct((b, s, ATTN_WIDTH), BF16),
        scratch_shapes=[pltpu.VMEM((ctx + s, KV_WIDTH), BF16),
                        pltpu.VMEM(((ctx + s) // ATTN_BLOCK, KV_WIDTH, ATTN_BLOCK), BF16)],
        compiler_params=_cparams(2),
        name="attn",
    )(a_lat, kv_ctx, cos, sin, qg, kg, sink, havg)


def _bdiag(x_lane, bmask):
    return jnp.where(bmask, jnp.concatenate([x_lane] * HEADS_PER_TILE, axis=0), jnp.zeros((), x_lane.dtype))


def _unit_tri_inverses(a_list, eye_l, sub_mask, bmask):
    ads = [jnp.where(sub_mask, a, 0.0) for a in a_list]
    aos = [jnp.where(sub_mask, 0.0, a).astype(BF16) for a in a_list]
    ps = [eye_l - ad for ad in ads]
    pws = [ad.astype(BF16) for ad in ads]
    n_levels = int(np.log2(DN_SUB))
    for level in range(n_levels):
        rhss = [_bdiag(pw, bmask) for pw in pws]
        if level == 0:
            pws = [_dot(pw, rhs).astype(BF16) for pw, rhs in zip(pws, rhss)]
        elif level < n_levels - 1:
            boths = [_dot(jnp.concatenate([pw, p.astype(BF16)], axis=0), rhs) for pw, p, rhs in zip(pws, ps, rhss)]
            pws = [both[0:CHUNK].astype(BF16) for both in boths]
            ps = [p + both[CHUNK:2 * CHUNK] for p, both in zip(ps, boths)]
        else:
            ps = [p + _dot(p.astype(BF16), rhs) for p, rhs in zip(ps, rhss)]
    xs = [p.astype(BF16) for p in ps]
    bs = [_dot(x, _bdiag(ao, bmask)) for x, ao in zip(xs, aos)]
    b_rhss = [_bdiag(b.astype(BF16), bmask) for b in bs]
    b2s = [_dot(b.astype(BF16), rhs) for b, rhs in zip(bs, b_rhss)]
    b3s = [_dot(b2.astype(BF16), rhs) for b2, rhs in zip(b2s, b_rhss)]
    assert CHUNK // DN_SUB == 4
    qs = [(eye_l - b + b2 - b3).astype(BF16) for b, b2, b3 in zip(bs, b2s, b3s)]
    return [_dot(q, _bdiag(x, bmask)) for q, x in zip(qs, xs)]


def _dn_factors(ab, d, arow, dtrow, exp2_ref, tri_ref, eye_t):
    lane = lax.broadcasted_iota(jnp.int32, ab.shape, 1)
    z = ab + dtrow
    softplus = jnp.maximum(z, 0.0) + jnp.log(1.0 + jnp.exp(-jnp.abs(z)))
    is_g = (lane % 32) < 16
    x = jnp.where(is_g, -arow * softplus, _sigmoid(ab))
    x_hi = x.astype(BF16)
    x_lo = (x - x_hi.astype(F32)).astype(BF16)
    xhl = jnp.where(lane < 32, x_hi, x_lo)[:, 0:64]
    y = _dot(xhl, exp2_ref[d])
    ge = y[:, 0:DN_WIDTH]
    be = y[:, DN_WIDTH:2 * DN_WIDTH]
    ge_hi = ge.astype(BF16)
    ge_lo = (ge - ge_hi.astype(F32)).astype(BF16)
    gi = _dot(tri_ref[d], jnp.concatenate([ge_hi, ge_lo], axis=0))
    gj = jnp.sum(gi * eye_t, axis=0, keepdims=True)
    return be, gi, jnp.broadcast_to(gj, gi.shape)


def _block_mask():
    return (lax.broadcasted_iota(jnp.int32, (MXU_TILE, MXU_TILE), 0) // HEAD_DIM
            == lax.broadcasted_iota(jnp.int32, (MXU_TILE, MXU_TILE), 1) // HEAD_DIM)


def _dnprep_kernel(*refs, cb, want_o):
    if want_o:
        q_ref, k_ref, v_ref, ab_ref = refs[:4]
        refs = refs[4:]
    else:
        q_ref = None
        k_ref, v_ref, ab_ref = refs[:3]
        refs = refs[3:]
    arow_ref, dtrow_ref, exp2_ref, tri_ref, w_o, uv_o, kt_o, dl_o = refs[:8]
    qd_o, in_o = refs[8:10] if want_o else (None, None)
    n_tiles = DN_WIDTH // MXU_TILE
    row = lax.broadcasted_iota(jnp.int32, (CHUNK, MXU_TILE), 0)
    colj = lax.broadcasted_iota(jnp.int32, (CHUNK, MXU_TILE), 1) % HEAD_DIM
    eye_l = (row == colj).astype(F32)
    eye_t = jnp.concatenate([eye_l] * n_tiles, axis=1)
    bmask = _block_mask()
    arow = arow_ref[...]
    dtrow = dtrow_ref[...]

    fac = {}
    shared = {}
    for c in range(cb):
        rows = slice(c * CHUNK, (c + 1) * CHUNK)
        ab = ab_ref[0, rows, :]
        for d in range(N_DIR):
            fac[(c, d)] = _dn_factors(ab, d, arow, dtrow, exp2_ref, tri_ref, eye_t)
        for g in range(n_tiles):
            lanes = slice(g * MXU_TILE, (g + 1) * MXU_TILE)
            k_l = k_ref[0, rows, lanes]
            kbd = _bdiag(k_l, bmask)
            if want_o:
                kq = _dot_nt(jnp.concatenate([k_l, q_ref[0, rows, lanes]], axis=0), kbd)
                shared[(c, g)] = (kq[0:CHUNK], kq[CHUNK:2 * CHUNK])
            else:
                shared[(c, g)] = (_dot_nt(k_l, kbd), None)

    units = [(c, g, d) for c in range(cb) for g in range(n_tiles) for d in range(N_DIR)]
    decs = []
    for c, g, d in units:
        lanes = slice(g * MXU_TILE, (g + 1) * MXU_TILE)
        be, gi, gj = fac[(c, d)]
        lower = (row > colj) if d == 0 else (row < colj)
        decs.append(jnp.where(lower, jnp.exp(jnp.where(lower, gi[:, lanes] - gj[:, lanes], 0.0)), 0.0))
    a_list = [fac[(c, d)][0][:, g * MXU_TILE:(g + 1) * MXU_TILE] * dec * shared[(c, g)][0]
              for (c, g, d), dec in zip(units, decs)]
    tinvs = _unit_tri_inverses(a_list, eye_l, (row // DN_SUB) == (colj // DN_SUB), bmask)

    for (c, g, d), dec, tinv in zip(units, decs, tinvs):
        rows = slice(c * CHUNK, (c + 1) * CHUNK)
        lanes = slice(g * MXU_TILE, (g + 1) * MXU_TILE)
        be, gi, _ = fac[(c, d)]
        be, gi = be[:, lanes], gi[:, lanes]
        last = CHUNK - 1 if d == 0 else 0
        e_g = jnp.exp(gi)
        gl_row = gi[last:last + 1, :]
        kf = k_ref[0, rows, lanes].astype(F32)
        vf = v_ref[0, rows, lanes].astype(F32)
        rhs = jnp.concatenate([_bdiag((be * e_g * kf).astype(BF16), bmask), _bdiag((be * vf).astype(BF16), bmask)],
                              axis=1)
        wu = _dot(tinv.astype(BF16), rhs)
        w_o[0, d, rows, lanes] = wu[:, 0:MXU_TILE].astype(BF16)
        uv_o[0, d, rows, lanes] = wu[:, MXU_TILE:2 * MXU_TILE]
        kt_o[0, d, rows, lanes] = (jnp.exp(gl_row - gi) * kf).astype(BF16)
        dl_o[0, d, c, :, lanes] = jnp.exp(gl_row)
        if want_o:
            qd_o[0, d, rows, lanes] = (e_g * q_ref[0, rows, lanes].astype(F32)).astype(BF16)
            in_o[0, d, rows, lanes] = ((dec + eye_l) * shared[(c, g)][1]).astype(BF16)


def _dnprep(q, k, v, ab, arow, dtrow, exp2, tri, cb):
    want_o = q is not None
    b, t, w = k.shape
    tb = cb * CHUNK
    tok = lambda bi, i: (bi, i, 0)
    const2 = lambda bi, i: (0, 0)
    const3 = lambda bi, i: (0, 0, 0)
    dir_tok = lambda bi, i: (bi, 0, i, 0)
    data = ([q] if want_o else []) + [k, v]
    in_specs = ([pl.BlockSpec((1, tb, w), tok)] * len(data) + [pl.BlockSpec((1, tb, LANES), tok),
                pl.BlockSpec((1, LANES), const2), pl.BlockSpec((1, LANES), const2),
                pl.BlockSpec(exp2.shape, const3), pl.BlockSpec(tri.shape, const3)])
    big = lambda dt: jax.ShapeDtypeStruct((b, N_DIR, t, w), dt)
    big_spec = pl.BlockSpec((1, N_DIR, tb, w), dir_tok)
    out_shape = [big(BF16), big(F32), big(BF16), jax.ShapeDtypeStruct((b, N_DIR, t // CHUNK, 1, w), F32)]
    out_specs = [big_spec, big_spec, big_spec, pl.BlockSpec((1, N_DIR, cb, 1, w), lambda bi, i: (bi, 0, i, 0, 0))]
    if want_o:
        out_shape += [big(BF16), big(BF16)]
        out_specs += [big_spec, big_spec]
    return pl.pallas_call(
        functools.partial(_dnprep_kernel, cb=cb, want_o=want_o),
        grid=(b, t // tb),
        in_specs=in_specs,
        out_specs=out_specs,
        out_shape=out_shape,
        compiler_params=_cparams(2),
        name="dnprep",
    )(*data, ab, arow, dtrow, exp2, tri)


def _dnscan_kernel(*refs, n_chunk, bb, want_o, have_s0, want_s):
    n_in = 6 if want_o else 4
    dir_refs = [refs[0:n_in], refs[n_in:2 * n_in]]
    pos = 2 * n_in
    s0_ref = refs[pos] if have_s0 else None
    pos += int(have_s0)
    o_refs = refs[pos:pos + N_DIR] if want_o else None
    pos += N_DIR if want_o else 0
    sout_ref = refs[pos] if want_s else None
    pos += int(want_s)
    s_scr = refs[pos]
    n_tiles = DN_WIDTH // MXU_TILE
    i = pl.program_id(1)

    @pl.when(i == 0)
    def _init():
        if have_s0:
            s_scr[...] = s0_ref[...]
        else:
            s_scr[...] = jnp.zeros_like(s_scr)

    bmask = _block_mask()
    chains = [(bi, d, g) for bi in range(bb) for d in range(N_DIR) for g in range(n_tiles)]

    def body(j, carry):
        cidx = (j, n_chunk - 1 - j)
        r0s = [pl.multiple_of(cidx[d] * CHUNK, CHUNK) for d in range(N_DIR)]
        s_olds, r1s = [], []
        for bi, d, g in chains:
            lanes = slice(g * MXU_TILE, (g + 1) * MXU_TILE)
            w = dir_refs[d][0][bi, 0, pl.ds(r0s[d], CHUNK), lanes]
            if want_o:
                w = jnp.concatenate([w, dir_refs[d][4][bi, 0, pl.ds(r0s[d], CHUNK), lanes]], axis=0)
            s_old = s_scr[bi, d * n_tiles + g]
            s_olds.append(s_old)
            r1s.append(_dot(w, s_old.astype(BF16)))
        u_bfs = []
        for (bi, d, g), r1 in zip(chains, r1s):
            lanes = slice(g * MXU_TILE, (g + 1) * MXU_TILE)
            u_bfs.append((dir_refs[d][1][bi, 0, pl.ds(r0s[d], CHUNK), lanes] - r1[0:CHUNK]).astype(BF16))
        for (bi, d, g), r1, u_bf, s_old in zip(chains, r1s, u_bfs, s_olds):
            lanes = slice(g * MXU_TILE, (g + 1) * MXU_TILE)
            kt = dir_refs[d][2][bi, 0, pl.ds(r0s[d], CHUNK), lanes]
            ds = jnp.where(bmask, _dot_tn(kt, u_bf), 0.0)
            dl = dir_refs[d][3][bi, 0, cidx[d]][:, lanes]
            s_scr[bi, d * n_tiles + g] = s_old * dl + ds
            if want_o:
                intra = dir_refs[d][5][bi, 0, pl.ds(r0s[d], CHUNK), lanes]
                o_refs[d][bi, pl.ds(r0s[d], CHUNK), lanes] = r1[CHUNK:2 * CHUNK] + _dot(intra, _bdiag(u_bf, bmask))
        return carry

    lax.fori_loop(0, n_chunk, body, 0)

    if want_s:
        @pl.when(i == pl.num_programs(1) - 1)
        def _fin():
            sout_ref[...] = s_scr[...]


def _dnscan(prep, s0, tb, bb, want_s):
    want_o = len(prep) == 6
    b, _, t, w = prep[0].shape
    n_t = t // tb
    n_chunk = tb // CHUNK
    n_chain = N_DIR * (w // MXU_TILE)

    def specs(d):
        blk = (lambda bi, i: i) if d == 0 else (lambda bi, i: n_t - 1 - i)
        big = pl.BlockSpec((bb, 1, tb, w), lambda bi, i: (bi, d, blk(bi, i), 0))
        dl = pl.BlockSpec((bb, 1, n_chunk, 1, w), lambda bi, i: (bi, d, blk(bi, i), 0, 0))
        return [big, big, big, dl] + ([big, big] if want_o else [])

    in_specs = specs(0) + specs(1)
    args = list(prep) + list(prep)
    state_spec = pl.BlockSpec((bb, n_chain, MXU_TILE, MXU_TILE), lambda bi, i: (bi, 0, 0, 0))
    if s0 is not None:
        in_specs.append(state_spec)
        args.append(s0)
    out_shape, out_specs = [], []
    if want_o:
        out_shape += [jax.ShapeDtypeStruct((b, t, w), F32)] * N_DIR
        out_specs += [pl.BlockSpec((bb, tb, w), lambda bi, i: (bi, i, 0)),
                      pl.BlockSpec((bb, tb, w), lambda bi, i: (bi, n_t - 1 - i, 0))]
    if want_s:
        out_shape.append(jax.ShapeDtypeStruct((b, n_chain, MXU_TILE, MXU_TILE), F32))
        out_specs.append(state_spec)
    return pl.pallas_call(
        functools.partial(_dnscan_kernel, n_chunk=n_chunk, bb=bb, want_o=want_o, have_s0=s0 is not None,
                          want_s=want_s),
        grid=(b // bb, n_t),
        in_specs=in_specs,
        out_specs=out_specs,
        out_shape=out_shape,
        scratch_shapes=[pltpu.VMEM((bb, n_chain, MXU_TILE, MXU_TILE), F32)],
        compiler_params=_cparams(2),
        name="dnscan",
    )(*args)


def _tail_kernel(x_ref, mod_ref, ya_ref, odf_ref, odb_ref, z_ref, gate_ref, dng_ref, havg_ref, wba_ref, wbd_ref,
                 wo_ref, gn2_ref, w1_ref, w2_ref, o_ref, *, ff_chunk):
    havg = havg_ref[...]
    dng = dng_ref[...]
    yd_parts = []
    for j in range(DN_WIDTH // MXU_TILE):
        sl = slice(j * MXU_TILE, (j + 1) * MXU_TILE)
        od = odf_ref[0, :, sl] + odb_ref[0, :, sl]
        ms = _dot((od * od).astype(BF16), havg)
        z = z_ref[0, :, sl].astype(F32)
        yd_parts.append((od * lax.rsqrt(ms + EPS) * dng * (z * _sigmoid(z))).astype(BF16))
    yd = jnp.concatenate(yd_parts, axis=1)
    ga = gate_ref[0, :, 0:D_MODEL].astype(F32)
    gd = gate_ref[0, :, D_MODEL:2 * D_MODEL].astype(F32)
    y = _sigmoid(ga) * _dot(ya_ref[0], wba_ref[...]) + _sigmoid(gd) * _dot(yd, wbd_ref[...])
    br = _dot(y.astype(BF16), wo_ref[...])
    mod = mod_ref[0]
    out1 = x_ref[0] + mod[2:3] * br
    ms2 = jnp.mean(out1 * out1, axis=-1, keepdims=True)
    hm = (out1 * lax.rsqrt(ms2 + EPS) * (gn2_ref[...] * (1.0 + mod[4:5])) + mod[3:4]).astype(BF16)
    acc = None
    for j in range(D_FF // ff_chunk):
        a = jnp.maximum(_dot(hm, w1_ref[:, j * ff_chunk:(j + 1) * ff_chunk]), 0.0)
        part = _dot((a * a).astype(BF16), w2_ref[j * ff_chunk:(j + 1) * ff_chunk, :])
        acc = part if acc is None else acc + part
    o_ref[0] = out1 + mod[5:6] * acc


def _resident(shape):
    return pl.BlockSpec(shape, lambda bi, i: (0,) * len(shape), pipeline_mode=pl.Buffered(1))


def _tail(x, mod3, y_attn, o_df, o_db, z, gates, dng, havg, wba, wbd, wo, gn2, w1, w2, tm):
    b, t, d = x.shape
    tok = lambda bi, i: (bi, i, 0)
    return pl.pallas_call(
        functools.partial(_tail_kernel, ff_chunk=1024),
        grid=(b, t // tm),
        in_specs=[pl.BlockSpec((1, tm, d), tok),
                  pl.BlockSpec((1, 6, d), lambda bi, i: (bi, 0, 0)),
                  pl.BlockSpec((1, tm, ATTN_WIDTH), tok),
                  pl.BlockSpec((1, tm, DN_WIDTH), tok),
                  pl.BlockSpec((1, tm, DN_WIDTH), tok),
                  pl.BlockSpec((1, tm, DN_WIDTH), tok),
                  pl.BlockSpec((1, tm, 2 * d), tok),
                  _resident((1, MXU_TILE)),
                  _resident((MXU_TILE, MXU_TILE)),
                  _resident(wba.shape),
                  _resident(wbd.shape),
                  _resident(wo.shape),
                  _resident((1, d)),
                  _resident(w1.shape),
                  _resident(w2.shape)],
        out_specs=pl.BlockSpec((1, tm, d), tok),
        out_shape=jax.ShapeDtypeStruct((b, t, d), F32),
        compiler_params=_cparams(2),
        name="tail",
    )(x, mod3, y_attn, o_df, o_db, z, gates, dng, havg, wba, wbd, wo, gn2, w1, w2)


def _head_avg(n, scale):
    idx = np.arange(n) // HEAD_DIM
    return jnp.asarray((idx[:, None] == idx[None, :]).astype(np.float32) * scale, BF16)


def _dn_expand_matrix():
    n = N_DIR * DN_HEADS
    m = np.zeros((N_DIR, 4 * n, 2 * DN_WIDTH), np.float32)
    for d in range(N_DIR):
        for part in range(2):
            for h in range(DN_HEADS):
                idx = d * DN_HEADS + h
                m[d, part * 2 * n + idx, h * HEAD_DIM:(h + 1) * HEAD_DIM] = 1.0
                m[d, part * 2 * n + n + idx, DN_WIDTH + h * HEAD_DIM:DN_WIDTH + (h + 1) * HEAD_DIM] = 1.0
    return jnp.asarray(m, BF16)


def _tri_matrices():
    i = np.arange(CHUNK)
    low = (i[:, None] >= i[None, :]).astype(np.float32)
    up = (i[:, None] <= i[None, :]).astype(np.float32)
    return jnp.asarray(np.stack([np.concatenate([low, low], axis=1), np.concatenate([up, up], axis=1)]), BF16)


def _rope_tables(seq):
    half = HEAD_DIM // 2
    n_freq = half // 2
    freqs = ROPE_BASE ** (-jnp.arange(n_freq, dtype=F32) / n_freq)
    pos = jnp.arange(seq)
    ang_r = (pos // GRID_W).astype(F32)[:, None] * freqs
    ang_c = (pos % GRID_W).astype(F32)[:, None] * freqs
    cos = jnp.concatenate([jnp.cos(ang_r)] * 2 + [jnp.cos(ang_c)] * 2, axis=1)
    sin = jnp.concatenate([-jnp.sin(ang_r), jnp.sin(ang_r), -jnp.sin(ang_c), jnp.sin(ang_c)], axis=1)
    reps = LANES // HEAD_DIM
    return jnp.tile(cos, (1, reps)), jnp.tile(sin, (1, reps))


def _pad_cols(w, n):
    return jnp.pad(w, ((0, 0), (0, n - w.shape[1])))


def kernel(x, c, ctx, c_ctx, w_ada, b_ada, g_norm1, w_in, q_norm_g, k_norm_g, attn_sink, conv_w, a_log, dt_bias,
           dn_norm_g, w_br_attn, w_br_dn, w_out, g_norm2, w_mlp1, w_mlp2):
    depth = w_ada.shape[0]
    assert depth == 1, "single-layer trunk only"
    b, s, d = x.shape
    n_ctx = ctx.shape[1]
    assert d == D_MODEL and w_in.shape[-1] == _IN_WIDTH
    assert s >= 3 * ATTN_BLOCK and s % ATTN_BLOCK == 0 and s % CHUNK == 0 and n_ctx % CHUNK == 0
    out_dtype = x.dtype
    w_in0 = w_in[0]

    mod_rows = 16
    cc = jnp.concatenate([c.astype(F32), c_ctx.astype(F32)[None], jnp.zeros((mod_rows - b - 1, d), F32)], axis=0)
    mod = _ada(cc, w_ada[0], b_ada[0])
    mod3 = mod.reshape(mod_rows, 6, d)

    ab_cols = jnp.concatenate([w_in0[:, _OFF_DA:_OFF_GA]] * 2, axis=1)
    w_lat = jnp.concatenate([w_in0[:, :_OFF_DA], w_in0[:, _OFF_GA:], _pad_cols(ab_cols, LANES)], axis=1).astype(BF16)
    hsum = _head_avg(MXU_TILE, 1.0)
    segs_lat = ((0, _OFF_DQ, None),
                (_OFF_DQ, DN_WIDTH, (0, True, True)), (_OFF_DK, DN_WIDTH, (DN_WIDTH, True, False)),
                (_OFF_DV, DN_WIDTH, (2 * DN_WIDTH, False, False)),
                (_OFF_DZ, DN_WIDTH, None), (_OFF_DA, 2 * D_MODEL, None), (_OFF_DA + 2 * D_MODEL, LANES, None))
    a_lat, q_d, k_d, v_d, z_lat, gates, ab_lat = _inproj(x, mod3, None, g_norm1[0], w_lat, conv_w[0], hsum, segs_lat,
                                                         (BF16, BF16, BF16, BF16, BF16, BF16, F32), tm=512)
    w_ctx = jnp.concatenate([w_in0[:, _OFF_AK:_OFF_DQ], w_in0[:, _OFF_DK:_OFF_DZ], _pad_cols(ab_cols, LANES)],
                            axis=1).astype(BF16)
    segs_ctx = ((0, 2 * KV_WIDTH, None),
                (2 * KV_WIDTH, DN_WIDTH, (0, True, False)), (2 * KV_WIDTH + DN_WIDTH, DN_WIDTH, (DN_WIDTH, False, False)),
                (2 * KV_WIDTH + 2 * DN_WIDTH, LANES, None))
    kv_ctx, k_dc, v_dc, ab_ctx = _inproj(ctx, mod3, b, g_norm1[0], w_ctx, conv_w[0][:, DN_WIDTH:], hsum, segs_ctx,
                                         (BF16, BF16, BF16, F32), tm=n_ctx)

    cos, sin = _rope_tables(s)
    reps = LANES // HEAD_DIM
    y_attn = _attention(a_lat, kv_ctx, cos, sin,
                        jnp.tile(q_norm_g[0].astype(F32), reps)[None], jnp.tile(k_norm_g[0].astype(F32), reps)[None],
                        jnp.broadcast_to(attn_sink[0].astype(F32)[:, None], (ATTN_Q_HEADS, LANES)),
                        _head_avg(LANES, 1.0 / HEAD_DIM))

    n_gate = N_DIR * DN_HEADS
    arow = _pad_cols(jnp.tile(jnp.concatenate([jnp.exp(a_log[0]).reshape(1, n_gate), jnp.zeros((1, n_gate), F32)],
                                              axis=1), (1, 2)), LANES)
    dtrow = _pad_cols(jnp.tile(jnp.concatenate([dt_bias[0].reshape(1, n_gate), jnp.zeros((1, n_gate), F32)],
                                               axis=1), (1, 2)), LANES)
    exp2, tri = _dn_expand_matrix(), _tri_matrices()
    prep_ctx = _dnprep(None, k_dc, v_dc, ab_ctx, arow, dtrow, exp2, tri, cb=4)
    prep_lat = _dnprep(q_d, k_d, v_d, ab_lat, arow, dtrow, exp2, tri, cb=4)
    bb = 2 if b % 2 == 0 else 1
    (s_ctx,) = _dnscan(prep_ctx, None, tb=n_ctx, bb=bb, want_s=True)
    o_df, o_db = _dnscan(prep_lat, s_ctx, tb=512, bb=bb, want_s=False)

    out = _tail(x, mod3, y_attn, o_df, o_db, z_lat, gates,
                jnp.tile(dn_norm_g[0].astype(F32), HEADS_PER_TILE)[None], _head_avg(MXU_TILE, 1.0 / HEAD_DIM),
                w_br_attn[0].astype(BF16), w_br_dn[0].astype(BF16), w_out[0].astype(BF16),
                g_norm2[0].reshape(1, d), w_mlp1[0].astype(BF16), w_mlp2[0].astype(BF16), tm=256)
    return out.astype(out_dtype)
```

```python
import functools

import numpy as np
import jax
import jax.numpy as jnp
from jax import lax
from jax.experimental import pallas as pl
from jax.experimental.pallas import tpu as pltpu

F32 = jnp.float32
BF16 = jnp.bfloat16

D_MODEL = 1024
GRID_W = 64
HEAD_DIM = 64
ATTN_Q_HEADS = 8
ATTN_KV_HEADS = 2
ATTN_GROUP = ATTN_Q_HEADS // ATTN_KV_HEADS
WINDOW = 128
ATTN_BLOCK = 128
ROPE_BASE = 10000.0
DN_HEADS = 8
CONV_W = 3
CHUNK = 64
N_DIR = 2
D_FF = 4 * D_MODEL
EPS = 1e-6
NEG_INF = -1e30
LOG2_E = float(np.log2(np.e))

ATTN_WIDTH = ATTN_Q_HEADS * HEAD_DIM
KV_WIDTH = ATTN_KV_HEADS * HEAD_DIM
DN_WIDTH = DN_HEADS * HEAD_DIM
LANES = 128
MXU_TILE = 256
HEADS_PER_TILE = MXU_TILE // HEAD_DIM
DN_SUB = 16
VMEM_LIMIT = 56 * 1024 * 1024

_OFF_AQ = 0
_OFF_AK = _OFF_AQ + ATTN_WIDTH
_OFF_AV = _OFF_AK + KV_WIDTH
_OFF_DQ = _OFF_AV + KV_WIDTH
_OFF_DK = _OFF_DQ + DN_WIDTH
_OFF_DV = _OFF_DK + DN_WIDTH
_OFF_DZ = _OFF_DV + DN_WIDTH
_OFF_DA = _OFF_DZ + DN_WIDTH
_OFF_DB = _OFF_DA + N_DIR * DN_HEADS
_OFF_GA = _OFF_DB + N_DIR * DN_HEADS
_OFF_GD = _OFF_GA + D_MODEL
_IN_WIDTH = _OFF_GD + D_MODEL


def _sigmoid(x):
    return 0.5 * jnp.tanh(0.5 * x) + 0.5


def _dot(a, b):
    return jnp.dot(a, b, preferred_element_type=F32)


def _dot_nt(a, b):
    return lax.dot_general(a, b, (((1,), (1,)), ((), ())), preferred_element_type=F32)


def _dot_tn(a, b):
    return lax.dot_general(a, b, (((0,), (0,)), ((), ())), preferred_element_type=F32)


def _cparams(n_axes):
    return pltpu.CompilerParams(dimension_semantics=("arbitrary",) * n_axes, vmem_limit_bytes=VMEM_LIMIT)


def _ada_kernel(c_ref, w_ref, b_ref, o_ref):
    c = c_ref[...]
    s = c * _sigmoid(c)
    o_ref[...] = _dot(s.astype(BF16), w_ref[...].astype(BF16)) + b_ref[...]


def _ada(cc, w_ada, b_ada):
    rows, d = cc.shape
    n = w_ada.shape[1]
    tn = 1536
    return pl.pallas_call(
        _ada_kernel,
        grid=(n // tn,),
        in_specs=[pl.BlockSpec((rows, d), lambda j: (0, 0)),
                  pl.BlockSpec((d, tn), lambda j: (0, j)),
                  pl.BlockSpec((1, tn), lambda j: (0, j))],
        out_specs=pl.BlockSpec((rows, tn), lambda j: (0, j)),
        out_shape=jax.ShapeDtypeStruct((rows, n), F32),
        compiler_params=_cparams(1),
        name="ada",
    )(cc, w_ada, b_ada.reshape(1, n))


def _inproj_kernel(x_ref, xprev_ref, xnext_ref, mod_ref, g_ref, w_ref, cw_ref, hsum_ref, *out_refs, segs, tm, halo):
    i = pl.program_id(1)
    last = pl.num_programs(1) - 1
    mod = mod_ref[0]
    scale = g_ref[...] * (1.0 + mod[1:2])

    def norm_mod(v):
        ms = jnp.mean(v * v, axis=-1, keepdims=True)
        return (v * lax.rsqrt(ms + EPS) * scale + mod[0:1]).astype(BF16)

    h = norm_mod(x_ref[0])
    h_halo = norm_mod(jnp.concatenate([xprev_ref[0], xnext_ref[0]], axis=0))
    keep_prev = (i > 0).astype(F32)
    keep_next = (i < last).astype(F32)
    rows = lax.broadcasted_iota(jnp.int32, (tm, MXU_TILE), 0)
    hsum = hsum_ref[...]
    def plain_piece(o_ref, start, lo, width):
        o_ref[0, :, lo:lo + width] = _dot(h, w_ref[:, start + lo:start + lo + width]).astype(o_ref.dtype)

    def conv_piece(o_ref, start, lo, kind):
        conv_col, do_norm, is_q = kind
        w_cols = w_ref[:, start + lo:start + lo + MXU_TILE]
        p = _dot(h, w_cols)
        p_halo = _dot(h_halo, w_cols)
        p_prev = jnp.where(rows == 0, p_halo[halo - 1:halo] * keep_prev, pltpu.roll(p, 1, 0))
        p_next = jnp.where(rows == tm - 1, p_halo[halo:halo + 1] * keep_next, pltpu.roll(p, tm - 1, 0))
        cw = cw_ref[:, conv_col + lo:conv_col + lo + MXU_TILE]
        y = p_prev * cw[0:1] + p * cw[1:2] + p_next * cw[2:3]
        y = y * _sigmoid(y)
        if do_norm:
            y = y * lax.rsqrt(_dot((y * y).astype(BF16), hsum) + EPS)
            if is_q:
                y = y * (HEAD_DIM ** -0.5)
        o_ref[0, :, lo:lo + MXU_TILE] = y.astype(o_ref.dtype)

    plain, conv = [], []
    for o_ref, (start, size, kind) in zip(out_refs, segs):
        if kind is None:
            plain += [functools.partial(plain_piece, o_ref, start, lo, min(2 * MXU_TILE, size - lo))
                      for lo in range(0, size, 2 * MXU_TILE)]
        else:
            conv += [functools.partial(conv_piece, o_ref, start, lo, kind) for lo in range(0, size, MXU_TILE)]
    while plain or conv:
        if conv:
            conv.pop(0)()
        if plain:
            plain.pop(0)()


def _inproj(x, mod3, mod_row, g_norm, w, conv_w, hsum, segs, dtypes, tm):
    b, t, d = x.shape
    n = w.shape[1]
    halo = 8
    r = tm // halo
    nblk = t // halo
    if mod_row is None:
        mod_map = lambda bi, i: (bi, 0, 0)
    else:
        mod_map = lambda bi, i: (mod_row, 0, 0)
    const2 = lambda bi, i: (0, 0)
    out_shape = [jax.ShapeDtypeStruct((b, t, size), dt) for (_, size, _), dt in zip(segs, dtypes)]
    out_specs = [pl.BlockSpec((1, tm, size), lambda bi, i: (bi, i, 0)) for (_, size, _) in segs]
    return pl.pallas_call(
        functools.partial(_inproj_kernel, segs=segs, tm=tm, halo=halo),
        grid=(b, t // tm),
        in_specs=[pl.BlockSpec((1, tm, d), lambda bi, i: (bi, i, 0)),
                  pl.BlockSpec((1, halo, d), lambda bi, i: (bi, jnp.maximum(i * r - 1, 0), 0)),
                  pl.BlockSpec((1, halo, d), lambda bi, i: (bi, jnp.minimum((i + 1) * r, nblk - 1), 0)),
                  pl.BlockSpec((1, 6, d), mod_map),
                  pl.BlockSpec((1, d), const2),
                  pl.BlockSpec((d, n), const2),
                  pl.BlockSpec(conv_w.shape, const2),
                  pl.BlockSpec(hsum.shape, const2)],
        out_specs=out_specs,
        out_shape=out_shape,
        compiler_params=_cparams(2),
        name="inproj",
    )(x, x, x, mod3, g_norm.reshape(1, d), w, conv_w, hsum)


def _rope(x, cos, sin, lane):
    swapped = jnp.where((lane % 32) < 16, pltpu.roll(x, LANES - 16, 1), pltpu.roll(x, 16, 1))
    return x * cos + swapped * sin


def _attn_kernel(a_ref, kvc_ref, cos_ref, sin_ref, qg_ref, kg_ref, sink_ref, havg_ref,
                 o_ref, k_s, vt_s, *, seq, ctx, qb):
    n = pl.program_id(1)
    havg = havg_ref[...]
    prep_rows = 256
    ctx_blocks = ctx // ATTN_BLOCK

    @pl.when(n == 0)
    def _prep():
        kg = kg_ref[...]
        kc = kvc_ref[0, :, 0:KV_WIDTH].astype(F32)
        ms = _dot((kc * kc).astype(BF16), havg)
        k_s[0:ctx, :] = (kc * lax.rsqrt(ms + EPS) * kg).astype(BF16)
        vc = kvc_ref[0, :, KV_WIDTH:2 * KV_WIDTH].astype(F32)
        for j in range(ctx_blocks):
            vt_s[j] = vc[j * ATTN_BLOCK:(j + 1) * ATTN_BLOCK, :].T.astype(BF16)
        lane = lax.broadcasted_iota(jnp.int32, (prep_rows, LANES), 1)
        for r0 in range(0, seq, prep_rows):
            kx = a_ref[0, r0:r0 + prep_rows, _OFF_AK:_OFF_AK + KV_WIDTH].astype(F32)
            ms = _dot((kx * kx).astype(BF16), havg)
            kx = kx * lax.rsqrt(ms + EPS) * kg
            kx = _rope(kx, cos_ref[r0:r0 + prep_rows, :], sin_ref[r0:r0 + prep_rows, :], lane)
            k_s[ctx + r0:ctx + r0 + prep_rows, :] = kx.astype(BF16)
            vx = a_ref[0, r0:r0 + prep_rows, _OFF_AV:_OFF_AV + KV_WIDTH].astype(F32)
            for j in range(prep_rows // ATTN_BLOCK):
                vt_s[ctx_blocks + r0 // ATTN_BLOCK + j] = vx[j * ATTN_BLOCK:(j + 1) * ATTN_BLOCK, :].T.astype(BF16)

    lane = lax.broadcasted_iota(jnp.int32, (ATTN_BLOCK, LANES), 1)
    qg = qg_ref[...]
    n_band = 3
    n_win = n_band * ATTN_BLOCK
    n_keys = n_win + ctx
    key_row = lax.broadcasted_iota(jnp.int32, (n_win, ATTN_BLOCK), 0)
    q_lane = lax.broadcasted_iota(jnp.int32, (n_win, ATTN_BLOCK), 1)
    zeros = jnp.zeros((HEAD_DIM, ATTN_BLOCK), BF16)
    groups = range(ATTN_KV_HEADS)
    sink_rows = [jnp.concatenate([sink_ref[g * ATTN_GROUP + r:g * ATTN_GROUP + r + 1, :] for r in range(ATTN_GROUP)],
                                 axis=1) * LOG2_E for g in groups]

    units = []
    k_alls, vt_alls, valids, rhss = [], [], [], {}
    for sb in range(qb):
        nq = n * qb + sb
        q0 = pl.multiple_of(nq * ATTN_BLOCK, ATTN_BLOCK)
        cos = cos_ref[pl.ds(q0, ATTN_BLOCK), :]
        sin = sin_ref[pl.ds(q0, ATTN_BLOCK), :]
        qts = []
        for j in range(ATTN_WIDTH // LANES):
            qx = a_ref[0, pl.ds(q0, ATTN_BLOCK), j * LANES:(j + 1) * LANES].astype(F32)
            ms = _dot((qx * qx).astype(BF16), havg)
            qx = _rope(qx * lax.rsqrt(ms + EPS) * qg, cos, sin, lane) * (HEAD_DIM ** -0.5 * LOG2_E)
            qts.append(qx.T.astype(BF16))
        blk0 = jnp.clip(nq - 1, 0, seq // ATTN_BLOCK - n_band)
        start = pl.multiple_of(blk0 * ATTN_BLOCK, ATTN_BLOCK)
        valid = jnp.abs(q0 + q_lane - (start + key_row)) <= WINDOW
        valids.append(jnp.concatenate([valid] * ATTN_GROUP, axis=1))
        k_alls.append(jnp.concatenate([k_s[pl.ds(ctx + start, n_win), :], k_s[0:ctx, :]], axis=0))
        vt_loc = vt_s[pl.ds(ctx_blocks + blk0, n_band)]
        vt_alls.append(jnp.concatenate([vt_loc[j] for j in range(n_band)] + [vt_s[j] for j in range(ctx_blocks)],
                                       axis=1))
        for g in groups:
            cols = []
            for r in range(ATTN_GROUP):
                h = g * ATTN_GROUP + r
                piece = qts[h // 2][(h % 2) * HEAD_DIM:(h % 2 + 1) * HEAD_DIM, :]
                cols.append(jnp.concatenate([piece, zeros] if g == 0 else [zeros, piece], axis=0))
            rhss[(sb, g)] = jnp.concatenate(cols, axis=1)
            units.append((sb, g))
        yield

    ss = []
    for sb, g in units:
        s = _dot(k_alls[sb], rhss[(sb, g)])
        ss.append(jnp.concatenate([jnp.where(valids[sb], s[0:n_win], NEG_INF), s[n_win:n_keys]], axis=0))
        yield
    p_bfs, invs = [], []
    for s, (sb, g) in zip(ss, units):
        m = jnp.maximum(jnp.max(s, axis=0, keepdims=True), sink_rows[g])
        p = jnp.exp2(s - m)
        invs.append(1.0 / (jnp.sum(p, axis=0, keepdims=True) + jnp.exp2(sink_rows[g] - m)))
        p_bfs.append(p.astype(BF16))
        yield
    pieces = {sb: [] for sb in range(qb)}
    for (sb, g), p_bf, inv in zip(units, p_bfs, invs):
        vt_g = vt_alls[sb][g * HEAD_DIM:(g + 1) * HEAD_DIM, :]
        for pair in range(ATTN_GROUP // 2):
            lanes = slice(pair * MXU_TILE, (pair + 1) * MXU_TILE)
            ot = _dot(vt_g, p_bf[:, lanes]) * inv[:, lanes]
            pieces[sb] += [ot[:, 0:ATTN_BLOCK], ot[:, ATTN_BLOCK:2 * ATTN_BLOCK]]
        yield
    for sb in range(qb):
        outs = [jnp.concatenate([pieces[sb][2 * j], pieces[sb][2 * j + 1]], axis=0).T
                for j in range(ATTN_Q_HEADS // 2)]
        o_ref[0, sb * ATTN_BLOCK:(sb + 1) * ATTN_BLOCK, :] = jnp.concatenate(outs, axis=1).astype(o_ref.dtype)


def _run_parts(parts, b, name):
    steps = parts[0]["steps"]
    assert all(p["steps"] == steps for p in parts)
    counts = [(len(p["args"]), len(p["out_shape"]), len(p["scratch_shapes"])) for p in parts]
    n_in = sum(c[0] for c in counts)
    n_out = sum(c[1] for c in counts)

    def kern(*refs):
        i_pos, o_pos, s_pos = 0, n_in, n_in + n_out
        bodies = []
        for p, (ci, co, cs) in zip(parts, counts):
            bodies.append(p["kernel"](*refs[i_pos:i_pos + ci], *refs[o_pos:o_pos + co], *refs[s_pos:s_pos + cs]))
            i_pos, o_pos, s_pos = i_pos + ci, o_pos + co, s_pos + cs
        while bodies:
            for body in list(bodies):
                try:
                    next(body)
                except StopIteration:
                    bodies.remove(body)

    outs = pl.pallas_call(
        kern,
        grid=(b, steps),
        in_specs=[s for p in parts for s in p["in_specs"]],
        out_specs=[s for p in parts for s in p["out_specs"]],
        out_shape=[s for p in parts for s in p["out_shape"]],
        scratch_shapes=[s for p in parts for s in p["scratch_shapes"]],
        compiler_params=_cparams(2),
        name=name,
    )(*[a for p in parts for a in p["args"]])
    res, pos = [], 0
    for _, co, _ in counts:
        res.append(outs[pos:pos + co])
        pos += co
    return res


def _attn_parts(a_lat, kv_ctx, cos, sin, qg, kg, sink, havg, qb):
    b, s, wa = a_lat.shape
    ctx = kv_ctx.shape[1]
    nb = s // ATTN_BLOCK
    const2 = lambda bi, n: (0, 0)
    return dict(
        kernel=functools.partial(_attn_kernel, seq=s, ctx=ctx, qb=qb),
        steps=nb // qb,
        in_specs=[pl.BlockSpec((1, s, wa), lambda bi, n: (bi, 0, 0)),
                  pl.BlockSpec((1, ctx, 2 * KV_WIDTH), lambda bi, n: (bi, 0, 0)),
                  pl.BlockSpec((s, LANES), const2),
                  pl.BlockSpec((s, LANES), const2),
                  pl.BlockSpec((1, LANES), const2),
                  pl.BlockSpec((1, LANES), const2),
                  pl.BlockSpec((ATTN_Q_HEADS, LANES), const2),
                  pl.BlockSpec((LANES, LANES), const2)],
        out_specs=[pl.BlockSpec((1, qb * ATTN_BLOCK, ATTN_WIDTH), lambda bi, n: (bi, n, 0))],
        out_shape=[jax.ShapeDtypeStruct((b, s, ATTN_WIDTH), BF16)],
        scratch_shapes=[pltpu.VMEM((ctx + s, KV_WIDTH), BF16),
                        pltpu.VMEM(((ctx + s) // ATTN_BLOCK, KV_WIDTH, ATTN_BLOCK), BF16)],
        args=[a_lat, kv_ctx, cos, sin, qg, kg, sink, havg])


def _bdiag(x_lane, bmask):
    return jnp.where(bmask, jnp.concatenate([x_lane] * HEADS_PER_TILE, axis=0), jnp.zeros((), x_lane.dtype))


def _unit_tri_inverses(a_list, eye_l, sub_mask, bmask):
    ads = [jnp.where(sub_mask, a, 0.0) for a in a_list]
    aos = [jnp.where(sub_mask, 0.0, a).astype(BF16) for a in a_list]
    ps = [eye_l - ad for ad in ads]
    pws = [ad.astype(BF16) for ad in ads]
    n_levels = int(np.log2(DN_SUB))
    for level in range(n_levels):
        rhss = [_bdiag(pw, bmask) for pw in pws]
        if level == 0:
            pws = [_dot(pw, rhs).astype(BF16) for pw, rhs in zip(pws, rhss)]
        elif level < n_levels - 1:
            boths = [_dot(jnp.concatenate([pw, p.astype(BF16)], axis=0), rhs) for pw, p, rhs in zip(pws, ps, rhss)]
            pws = [both[0:CHUNK].astype(BF16) for both in boths]
            ps = [p + both[CHUNK:2 * CHUNK] for p, both in zip(ps, boths)]
        else:
            ps = [p + _dot(p.astype(BF16), rhs) for p, rhs in zip(ps, rhss)]
        yield
    xs = [p.astype(BF16) for p in ps]
    bs = [_dot(x, _bdiag(ao, bmask)) for x, ao in zip(xs, aos)]
    yield
    b_rhss = [_bdiag(b.astype(BF16), bmask) for b in bs]
    b2s = [_dot(b.astype(BF16), rhs) for b, rhs in zip(bs, b_rhss)]
    yield
    b3s = [_dot(b2.astype(BF16), rhs) for b2, rhs in zip(b2s, b_rhss)]
    yield
    assert CHUNK // DN_SUB == 4
    qs = [(eye_l - b + b2 - b3).astype(BF16) for b, b2, b3 in zip(bs, b2s, b3s)]
    return [_dot(q, _bdiag(x, bmask)) for q, x in zip(qs, xs)]


def _dn_factors(ab, d, arow, dtrow, exp2_ref, tri_ref, eye_t):
    lane = lax.broadcasted_iota(jnp.int32, ab.shape, 1)
    z = ab + dtrow
    softplus = jnp.maximum(z, 0.0) + jnp.log(1.0 + jnp.exp(-jnp.abs(z)))
    is_g = (lane % 32) < 16
    x = jnp.where(is_g, -arow * softplus, _sigmoid(ab))
    x_hi = x.astype(BF16)
    x_lo = (x - x_hi.astype(F32)).astype(BF16)
    xhl = jnp.where(lane < 32, x_hi, x_lo)[:, 0:64]
    y = _dot(xhl, exp2_ref[d])
    ge = y[:, 0:DN_WIDTH]
    be = y[:, DN_WIDTH:2 * DN_WIDTH]
    ge_hi = ge.astype(BF16)
    ge_lo = (ge - ge_hi.astype(F32)).astype(BF16)
    gi = _dot(tri_ref[d], jnp.concatenate([ge_hi, ge_lo], axis=0))
    gj = jnp.sum(gi * eye_t, axis=0, keepdims=True)
    return be, gi, jnp.broadcast_to(gj, gi.shape)


def _block_mask():
    return (lax.broadcasted_iota(jnp.int32, (MXU_TILE, MXU_TILE), 0) // HEAD_DIM
            == lax.broadcasted_iota(jnp.int32, (MXU_TILE, MXU_TILE), 1) // HEAD_DIM)


def _dnprep_kernel(*refs, cb, want_o):
    if want_o:
        q_ref, k_ref, v_ref, ab_ref = refs[:4]
        refs = refs[4:]
    else:
        q_ref = None
        k_ref, v_ref, ab_ref = refs[:3]
        refs = refs[3:]
    arow_ref, dtrow_ref, exp2_ref, tri_ref, w_o, uv_o, kt_o, dl_o = refs[:8]
    qd_o, in_o = refs[8:10] if want_o else (None, None)
    n_tiles = DN_WIDTH // MXU_TILE
    row = lax.broadcasted_iota(jnp.int32, (CHUNK, MXU_TILE), 0)
    colj = lax.broadcasted_iota(jnp.int32, (CHUNK, MXU_TILE), 1) % HEAD_DIM
    eye_l = (row == colj).astype(F32)
    eye_t = jnp.concatenate([eye_l] * n_tiles, axis=1)
    bmask = _block_mask()
    arow = arow_ref[...]
    dtrow = dtrow_ref[...]

    fac = {}
    shared = {}
    for c in range(cb):
        rows = slice(c * CHUNK, (c + 1) * CHUNK)
        ab = ab_ref[0, rows, :]
        for d in range(N_DIR):
            fac[(c, d)] = _dn_factors(ab, d, arow, dtrow, exp2_ref, tri_ref, eye_t)
        for g in range(n_tiles):
            lanes = slice(g * MXU_TILE, (g + 1) * MXU_TILE)
            k_l = k_ref[0, rows, lanes]
            kbd = _bdiag(k_l, bmask)
            if want_o:
                kq = _dot_nt(jnp.concatenate([k_l, q_ref[0, rows, lanes]], axis=0), kbd)
                shared[(c, g)] = (kq[0:CHUNK], kq[CHUNK:2 * CHUNK])
            else:
                shared[(c, g)] = (_dot_nt(k_l, kbd), None)
        yield

    units = [(c, g, d) for c in range(cb) for g in range(n_tiles) for d in range(N_DIR)]
    decs = []
    for c, g, d in units:
        lanes = slice(g * MXU_TILE, (g + 1) * MXU_TILE)
        be, gi, gj = fac[(c, d)]
        lower = (row > colj) if d == 0 else (row < colj)
        decs.append(jnp.where(lower, jnp.exp(jnp.where(lower, gi[:, lanes] - gj[:, lanes], 0.0)), 0.0))
    a_list = [fac[(c, d)][0][:, g * MXU_TILE:(g + 1) * MXU_TILE] * dec * shared[(c, g)][0]
              for (c, g, d), dec in zip(units, decs)]
    yield
    tinvs = yield from _unit_tri_inverses(a_list, eye_l, (row // DN_SUB) == (colj // DN_SUB), bmask)

    for idx, ((c, g, d), dec, tinv) in enumerate(zip(units, decs, tinvs)):
        if idx % (n_tiles * N_DIR) == 0:
            yield
        rows = slice(c * CHUNK, (c + 1) * CHUNK)
        lanes = slice(g * MXU_TILE, (g + 1) * MXU_TILE)
        be, gi, _ = fac[(c, d)]
        be, gi = be[:, lanes], gi[:, lanes]
        last = CHUNK - 1 if d == 0 else 0
        e_g = jnp.exp(gi)
        gl_row = gi[last:last + 1, :]
        kf = k_ref[0, rows, lanes].astype(F32)
        vf = v_ref[0, rows, lanes].astype(F32)
        rhs = jnp.concatenate([_bdiag((be * e_g * kf).astype(BF16), bmask), _bdiag((be * vf).astype(BF16), bmask)],
                              axis=1)
        wu = _dot(tinv.astype(BF16), rhs)
        w_o[0, d, rows, lanes] = wu[:, 0:MXU_TILE].astype(BF16)
        uv_o[0, d, rows, lanes] = wu[:, MXU_TILE:2 * MXU_TILE]
        kt_o[0, d, rows, lanes] = (jnp.exp(gl_row - gi) * kf).astype(BF16)
        dl_o[0, d, c, :, lanes] = jnp.exp(gl_row)
        if want_o:
            qd_o[0, d, rows, lanes] = (e_g * q_ref[0, rows, lanes].astype(F32)).astype(BF16)
            in_o[0, d, rows, lanes] = ((dec + eye_l) * shared[(c, g)][1]).astype(BF16)


def _dnprep_parts(q, k, v, ab, arow, dtrow, exp2, tri, cb):
    want_o = q is not None
    b, t, w = k.shape
    tb = cb * CHUNK
    tok = lambda bi, i: (bi, i, 0)
    const2 = lambda bi, i: (0, 0)
    const3 = lambda bi, i: (0, 0, 0)
    dir_tok = lambda bi, i: (bi, 0, i, 0)
    data = ([q] if want_o else []) + [k, v]
    in_specs = ([pl.BlockSpec((1, tb, w), tok)] * len(data) + [pl.BlockSpec((1, tb, LANES), tok),
                pl.BlockSpec((1, LANES), const2), pl.BlockSpec((1, LANES), const2),
                pl.BlockSpec(exp2.shape, const3), pl.BlockSpec(tri.shape, const3)])
    big = lambda dt: jax.ShapeDtypeStruct((b, N_DIR, t, w), dt)
    big_spec = pl.BlockSpec((1, N_DIR, tb, w), dir_tok)
    out_shape = [big(BF16), big(F32), big(BF16), jax.ShapeDtypeStruct((b, N_DIR, t // CHUNK, 1, w), F32)]
    out_specs = [big_spec, big_spec, big_spec, pl.BlockSpec((1, N_DIR, cb, 1, w), lambda bi, i: (bi, 0, i, 0, 0))]
    if want_o:
        out_shape += [big(BF16), big(BF16)]
        out_specs += [big_spec, big_spec]
    return dict(kernel=functools.partial(_dnprep_kernel, cb=cb, want_o=want_o), steps=t // tb, in_specs=in_specs,
                out_specs=out_specs, out_shape=out_shape, scratch_shapes=[],
                args=data + [ab, arow, dtrow, exp2, tri])


def _dnscan_kernel(*refs, n_chunk, bb, want_o, have_s0, want_s):
    n_in = 6 if want_o else 4
    dir_refs = [refs[0:n_in], refs[n_in:2 * n_in]]
    pos = 2 * n_in
    s0_ref = refs[pos] if have_s0 else None
    pos += int(have_s0)
    o_refs = refs[pos:pos + N_DIR] if want_o else None
    pos += N_DIR if want_o else 0
    sout_ref = refs[pos] if want_s else None
    pos += int(want_s)
    s_scr = refs[pos]
    n_tiles = DN_WIDTH // MXU_TILE
    i = pl.program_id(1)

    @pl.when(i == 0)
    def _init():
        if have_s0:
            s_scr[...] = s0_ref[...]
        else:
            s_scr[...] = jnp.zeros_like(s_scr)

    bmask = _block_mask()
    chains = [(bi, d, g) for bi in range(bb) for d in range(N_DIR) for g in range(n_tiles)]

    def body(j, carry):
        cidx = (j, n_chunk - 1 - j)
        r0s = [pl.multiple_of(cidx[d] * CHUNK, CHUNK) for d in range(N_DIR)]
        s_olds, r1s = [], []
        for bi, d, g in chains:
            lanes = slice(g * MXU_TILE, (g + 1) * MXU_TILE)
            w = dir_refs[d][0][bi, 0, pl.ds(r0s[d], CHUNK), lanes]
            if want_o:
                w = jnp.concatenate([w, dir_refs[d][4][bi, 0, pl.ds(r0s[d], CHUNK), lanes]], axis=0)
            s_old = s_scr[bi, d * n_tiles + g]
            s_olds.append(s_old)
            r1s.append(_dot(w, s_old.astype(BF16)))
        u_bfs = []
        for (bi, d, g), r1 in zip(chains, r1s):
            lanes = slice(g * MXU_TILE, (g + 1) * MXU_TILE)
            u_bfs.append((dir_refs[d][1][bi, 0, pl.ds(r0s[d], CHUNK), lanes] - r1[0:CHUNK]).astype(BF16))
        for (bi, d, g), r1, u_bf, s_old in zip(chains, r1s, u_bfs, s_olds):
            lanes = slice(g * MXU_TILE, (g + 1) * MXU_TILE)
            kt = dir_refs[d][2][bi, 0, pl.ds(r0s[d], CHUNK), lanes]
            ds = jnp.where(bmask, _dot_tn(kt, u_bf), 0.0)
            dl = dir_refs[d][3][bi, 0, cidx[d]][:, lanes]
            s_scr[bi, d * n_tiles + g] = s_old * dl + ds
            if want_o:
                intra = dir_refs[d][5][bi, 0, pl.ds(r0s[d], CHUNK), lanes]
                o_refs[d][bi, pl.ds(r0s[d], CHUNK), lanes] = r1[CHUNK:2 * CHUNK] + _dot(intra, _bdiag(u_bf, bmask))
        return carry

    lax.fori_loop(0, n_chunk, body, 0)

    if want_s:
        @pl.when(i == pl.num_programs(1) - 1)
        def _fin():
            sout_ref[...] = s_scr[...]


def _dnscan(prep, s0, tb, bb, want_s):
    want_o = len(prep) == 6
    b, _, t, w = prep[0].shape
    n_t = t // tb
    n_chunk = tb // CHUNK
    n_chain = N_DIR * (w // MXU_TILE)

    def specs(d):
        blk = (lambda bi, i: i) if d == 0 else (lambda bi, i: n_t - 1 - i)
        big = pl.BlockSpec((bb, 1, tb, w), lambda bi, i: (bi, d, blk(bi, i), 0))
        dl = pl.BlockSpec((bb, 1, n_chunk, 1, w), lambda bi, i: (bi, d, blk(bi, i), 0, 0))
        return [big, big, big, dl] + ([big, big] if want_o else [])

    in_specs = specs(0) + specs(1)
    args = list(prep) + list(prep)
    state_spec = pl.BlockSpec((bb, n_chain, MXU_TILE, MXU_TILE), lambda bi, i: (bi, 0, 0, 0))
    if s0 is not None:
        in_specs.append(state_spec)
        args.append(s0)
    out_shape, out_specs = [], []
    if want_o:
        out_shape += [jax.ShapeDtypeStruct((b, t, w), F32)] * N_DIR
        out_specs += [pl.BlockSpec((bb, tb, w), lambda bi, i: (bi, i, 0)),
                      pl.BlockSpec((bb, tb, w), lambda bi, i: (bi, n_t - 1 - i, 0))]
    if want_s:
        out_shape.append(jax.ShapeDtypeStruct((b, n_chain, MXU_TILE, MXU_TILE), F32))
        out_specs.append(state_spec)
    return pl.pallas_call(
        functools.partial(_dnscan_kernel, n_chunk=n_chunk, bb=bb, want_o=want_o, have_s0=s0 is not None,
                          want_s=want_s),
        grid=(b // bb, n_t),
        in_specs=in_specs,
        out_specs=out_specs,
        out_shape=out_shape,
        scratch_shapes=[pltpu.VMEM((bb, n_chain, MXU_TILE, MXU_TILE), F32)],
        compiler_params=_cparams(2),
        name="dnscan",
    )(*args)


def _tail_kernel(x_ref, mod_ref, ya_ref, odf_ref, odb_ref, z_ref, gate_ref, dng_ref, havg_ref, wba_ref, wbd_ref,
                 wo_ref, gn2_ref, w1_ref, w2_ref, o_ref, *, ff_chunk):
    havg = havg_ref[...]
    dng = dng_ref[...]
    yd_parts = []
    for j in range(DN_WIDTH // MXU_TILE):
        sl = slice(j * MXU_TILE, (j + 1) * MXU_TILE)
        od = odf_ref[0, :, sl] + odb_ref[0, :, sl]
        ms = _dot((od * od).astype(BF16), havg)
        z = z_ref[0, :, sl].astype(F32)
        yd_parts.append((od * lax.rsqrt(ms + EPS) * dng * (z * _sigmoid(z))).astype(BF16))
    yd = jnp.concatenate(yd_parts, axis=1)
    ga = gate_ref[0, :, 0:D_MODEL].astype(F32)
    gd = gate_ref[0, :, D_MODEL:2 * D_MODEL].astype(F32)
    y = _sigmoid(ga) * _dot(ya_ref[0], wba_ref[...]) + _sigmoid(gd) * _dot(yd, wbd_ref[...])
    br = _dot(y.astype(BF16), wo_ref[...])
    mod = mod_ref[0]
    out1 = x_ref[0] + mod[2:3] * br
    ms2 = jnp.mean(out1 * out1, axis=-1, keepdims=True)
    hm = (out1 * lax.rsqrt(ms2 + EPS) * (gn2_ref[...] * (1.0 + mod[4:5])) + mod[3:4]).astype(BF16)
    acc = None
    for j in range(D_FF // ff_chunk):
        a = jnp.maximum(_dot(hm, w1_ref[:, j * ff_chunk:(j + 1) * ff_chunk]), 0.0)
        part = _dot((a * a).astype(BF16), w2_ref[j * ff_chunk:(j + 1) * ff_chunk, :])
        acc = part if acc is None else acc + part
    o_ref[0] = out1 + mod[5:6] * acc


def _resident(shape):
    return pl.BlockSpec(shape, lambda bi, i: (0,) * len(shape), pipeline_mode=pl.Buffered(1))


def _tail(x, mod3, y_attn, o_df, o_db, z, gates, dng, havg, wba, wbd, wo, gn2, w1, w2, tm):
    b, t, d = x.shape
    tok = lambda bi, i: (bi, i, 0)
    return pl.pallas_call(
        functools.partial(_tail_kernel, ff_chunk=1024),
        grid=(b, t // tm),
        in_specs=[pl.BlockSpec((1, tm, d), tok),
                  pl.BlockSpec((1, 6, d), lambda bi, i: (bi, 0, 0)),
                  pl.BlockSpec((1, tm, ATTN_WIDTH), tok),
                  pl.BlockSpec((1, tm, DN_WIDTH), tok),
                  pl.BlockSpec((1, tm, DN_WIDTH), tok),
                  pl.BlockSpec((1, tm, DN_WIDTH), tok),
                  pl.BlockSpec((1, tm, 2 * d), tok),
                  _resident((1, MXU_TILE)),
                  _resident((MXU_TILE, MXU_TILE)),
                  _resident(wba.shape),
                  _resident(wbd.shape),
                  _resident(wo.shape),
                  _resident((1, d)),
                  _resident(w1.shape),
                  _resident(w2.shape)],
        out_specs=pl.BlockSpec((1, tm, d), tok),
        out_shape=jax.ShapeDtypeStruct((b, t, d), F32),
        compiler_params=_cparams(2),
        name="tail",
    )(x, mod3, y_attn, o_df, o_db, z, gates, dng, havg, wba, wbd, wo, gn2, w1, w2)


def _head_avg(n, scale):
    idx = np.arange(n) // HEAD_DIM
    return jnp.asarray((idx[:, None] == idx[None, :]).astype(np.float32) * scale, BF16)


def _dn_expand_matrix():
    n = N_DIR * DN_HEADS
    m = np.zeros((N_DIR, 4 * n, 2 * DN_WIDTH), np.float32)
    for d in range(N_DIR):
        for part in range(2):
            for h in range(DN_HEADS):
                idx = d * DN_HEADS + h
                m[d, part * 2 * n + idx, h * HEAD_DIM:(h + 1) * HEAD_DIM] = 1.0
                m[d, part * 2 * n + n + idx, DN_WIDTH + h * HEAD_DIM:DN_WIDTH + (h + 1) * HEAD_DIM] = 1.0
    return jnp.asarray(m, BF16)


def _tri_matrices():
    i = np.arange(CHUNK)
    low = (i[:, None] >= i[None, :]).astype(np.float32)
    up = (i[:, None] <= i[None, :]).astype(np.float32)
    return jnp.asarray(np.stack([np.concatenate([low, low], axis=1), np.concatenate([up, up], axis=1)]), BF16)


def _rope_tables(seq):
    half = HEAD_DIM // 2
    n_freq = half // 2
    freqs = ROPE_BASE ** (-jnp.arange(n_freq, dtype=F32) / n_freq)
    pos = jnp.arange(seq)
    ang_r = (pos // GRID_W).astype(F32)[:, None] * freqs
    ang_c = (pos % GRID_W).astype(F32)[:, None] * freqs
    cos = jnp.concatenate([jnp.cos(ang_r)] * 2 + [jnp.cos(ang_c)] * 2, axis=1)
    sin = jnp.concatenate([-jnp.sin(ang_r), jnp.sin(ang_r), -jnp.sin(ang_c), jnp.sin(ang_c)], axis=1)
    reps = LANES // HEAD_DIM
    return jnp.tile(cos, (1, reps)), jnp.tile(sin, (1, reps))


def _pad_cols(w, n):
    return jnp.pad(w, ((0, 0), (0, n - w.shape[1])))


def kernel(x, c, ctx, c_ctx, w_ada, b_ada, g_norm1, w_in, q_norm_g, k_norm_g, attn_sink, conv_w, a_log, dt_bias,
           dn_norm_g, w_br_attn, w_br_dn, w_out, g_norm2, w_mlp1, w_mlp2):
    depth = w_ada.shape[0]
    assert depth == 1, "single-layer trunk only"
    b, s, d = x.shape
    n_ctx = ctx.shape[1]
    assert d == D_MODEL and w_in.shape[-1] == _IN_WIDTH
    assert s >= 3 * ATTN_BLOCK and s % ATTN_BLOCK == 0 and s % CHUNK == 0 and n_ctx % CHUNK == 0
    out_dtype = x.dtype
    w_in0 = w_in[0]

    mod_rows = 16
    cc = jnp.concatenate([c.astype(F32), c_ctx.astype(F32)[None], jnp.zeros((mod_rows - b - 1, d), F32)], axis=0)
    mod = _ada(cc, w_ada[0], b_ada[0])
    mod3 = mod.reshape(mod_rows, 6, d)

    ab_cols = jnp.concatenate([w_in0[:, _OFF_DA:_OFF_GA]] * 2, axis=1)
    w_lat = jnp.concatenate([w_in0[:, :_OFF_DA], w_in0[:, _OFF_GA:], _pad_cols(ab_cols, LANES)], axis=1).astype(BF16)
    hsum = _head_avg(MXU_TILE, 1.0)
    segs_lat = ((0, _OFF_DQ, None),
                (_OFF_DQ, DN_WIDTH, (0, True, True)), (_OFF_DK, DN_WIDTH, (DN_WIDTH, True, False)),
                (_OFF_DV, DN_WIDTH, (2 * DN_WIDTH, False, False)),
                (_OFF_DZ, DN_WIDTH, None), (_OFF_DA, 2 * D_MODEL, None), (_OFF_DA + 2 * D_MODEL, LANES, None))
    a_lat, q_d, k_d, v_d, z_lat, gates, ab_lat = _inproj(x, mod3, None, g_norm1[0], w_lat, conv_w[0], hsum, segs_lat,
                                                         (BF16, BF16, BF16, BF16, BF16, BF16, F32), tm=512)
    w_ctx = jnp.concatenate([w_in0[:, _OFF_AK:_OFF_DQ], w_in0[:, _OFF_DK:_OFF_DZ], _pad_cols(ab_cols, LANES)],
                            axis=1).astype(BF16)
    segs_ctx = ((0, 2 * KV_WIDTH, None),
                (2 * KV_WIDTH, DN_WIDTH, (0, True, False)), (2 * KV_WIDTH + DN_WIDTH, DN_WIDTH, (DN_WIDTH, False, False)),
                (2 * KV_WIDTH + 2 * DN_WIDTH, LANES, None))
    kv_ctx, k_dc, v_dc, ab_ctx = _inproj(ctx, mod3, b, g_norm1[0], w_ctx, conv_w[0][:, DN_WIDTH:], hsum, segs_ctx,
                                         (BF16, BF16, BF16, F32), tm=n_ctx)

    cos, sin = _rope_tables(s)
    reps = LANES // HEAD_DIM
    n_gate = N_DIR * DN_HEADS
    arow = _pad_cols(jnp.tile(jnp.concatenate([jnp.exp(a_log[0]).reshape(1, n_gate), jnp.zeros((1, n_gate), F32)],
                                              axis=1), (1, 2)), LANES)
    dtrow = _pad_cols(jnp.tile(jnp.concatenate([dt_bias[0].reshape(1, n_gate), jnp.zeros((1, n_gate), F32)],
                                               axis=1), (1, 2)), LANES)
    exp2, tri = _dn_expand_matrix(), _tri_matrices()
    dn_cb = 4
    qb = (s // ATTN_BLOCK) // (s // (dn_cb * CHUNK))
    attn_parts = _attn_parts(a_lat, kv_ctx, cos, sin,
                             jnp.tile(q_norm_g[0].astype(F32), reps)[None],
                             jnp.tile(k_norm_g[0].astype(F32), reps)[None],
                             jnp.broadcast_to(attn_sink[0].astype(F32)[:, None], (ATTN_Q_HEADS, LANES)),
                             _head_avg(LANES, 1.0 / HEAD_DIM), qb)
    (y_attn,), prep_lat = _run_parts(
        [attn_parts, _dnprep_parts(q_d, k_d, v_d, ab_lat, arow, dtrow, exp2, tri, dn_cb)], b, "attn_dnprep")
    (prep_ctx,) = _run_parts([_dnprep_parts(None, k_dc, v_dc, ab_ctx, arow, dtrow, exp2, tri, dn_cb)], b, "dnprep")

    bb = 2 if b % 2 == 0 else 1
    (s_ctx,) = _dnscan(prep_ctx, None, tb=n_ctx, bb=bb, want_s=True)
    o_df, o_db = _dnscan(prep_lat, s_ctx, tb=512, bb=bb, want_s=False)

    out = _tail(x, mod3, y_attn, o_df, o_db, z_lat, gates,
                jnp.tile(dn_norm_g[0].astype(F32), HEADS_PER_TILE)[None], _head_avg(MXU_TILE, 1.0 / HEAD_DIM),
                w_br_attn[0].astype(BF16), w_br_dn[0].astype(BF16), w_out[0].astype(BF16),
                g_norm2[0].reshape(1, d), w_mlp1[0].astype(BF16), w_mlp2[0].astype(BF16), tm=256)
    return out.astype(out_dtype)
```

```python
import functools

import numpy as np
import jax
import jax.numpy as jnp
from jax import lax
from jax.experimental import pallas as pl
from jax.experimental.pallas import tpu as pltpu

F32 = jnp.float32
BF16 = jnp.bfloat16

D_MODEL = 1024
GRID_W = 64
HEAD_DIM = 64
ATTN_Q_HEADS = 8
ATTN_KV_HEADS = 2
ATTN_GROUP = ATTN_Q_HEADS // ATTN_KV_HEADS
WINDOW = 128
ATTN_BLOCK = 128
ROPE_BASE = 10000.0
DN_HEADS = 8
CONV_W = 3
CHUNK = 64
N_DIR = 2
D_FF = 4 * D_MODEL
EPS = 1e-6
NEG_INF = -1e30
LOG2_E = float(np.log2(np.e))

ATTN_WIDTH = ATTN_Q_HEADS * HEAD_DIM
KV_WIDTH = ATTN_KV_HEADS * HEAD_DIM
DN_WIDTH = DN_HEADS * HEAD_DIM
LANES = 128
MXU_TILE = 256
HEADS_PER_TILE = MXU_TILE // HEAD_DIM
DN_SUB = 16
VMEM_LIMIT = 56 * 1024 * 1024

_OFF_AQ = 0
_OFF_AK = _OFF_AQ + ATTN_WIDTH
_OFF_AV = _OFF_AK + KV_WIDTH
_OFF_DQ = _OFF_AV + KV_WIDTH
_OFF_DK = _OFF_DQ + DN_WIDTH
_OFF_DV = _OFF_DK + DN_WIDTH
_OFF_DZ = _OFF_DV + DN_WIDTH
_OFF_DA = _OFF_DZ + DN_WIDTH
_OFF_DB = _OFF_DA + N_DIR * DN_HEADS
_OFF_GA = _OFF_DB + N_DIR * DN_HEADS
_OFF_GD = _OFF_GA + D_MODEL
_IN_WIDTH = _OFF_GD + D_MODEL


def _sigmoid(x):
    return 0.5 * jnp.tanh(0.5 * x) + 0.5


def _dot(a, b):
    return jnp.dot(a, b, preferred_element_type=F32)


def _dot_nt(a, b):
    return lax.dot_general(a, b, (((1,), (1,)), ((), ())), preferred_element_type=F32)


def _dot_tn(a, b):
    return lax.dot_general(a, b, (((0,), (0,)), ((), ())), preferred_element_type=F32)


def _cparams(n_axes):
    return pltpu.CompilerParams(dimension_semantics=("arbitrary",) * n_axes, vmem_limit_bytes=VMEM_LIMIT)


def _ada_kernel(c_ref, w_ref, b_ref, o_ref):
    c = c_ref[...]
    s = c * _sigmoid(c)
    o_ref[...] = _dot(s.astype(BF16), w_ref[...].astype(BF16)) + b_ref[...]


def _ada(cc, w_ada, b_ada):
    rows, d = cc.shape
    n = w_ada.shape[1]
    tn = 1536
    return pl.pallas_call(
        _ada_kernel,
        grid=(n // tn,),
        in_specs=[pl.BlockSpec((rows, d), lambda j: (0, 0)),
                  pl.BlockSpec((d, tn), lambda j: (0, j)),
                  pl.BlockSpec((1, tn), lambda j: (0, j))],
        out_specs=pl.BlockSpec((rows, tn), lambda j: (0, j)),
        out_shape=jax.ShapeDtypeStruct((rows, n), F32),
        compiler_params=_cparams(1),
        name="ada",
    )(cc, w_ada, b_ada.reshape(1, n))


def _inproj_kernel(x_ref, xprev_ref, xnext_ref, mod_ref, g_ref, w_ref, cw_ref, hsum_ref, *out_refs, segs, tm, halo):
    i = pl.program_id(1)
    last = pl.num_programs(1) - 1
    mod = mod_ref[0]
    scale = g_ref[...] * (1.0 + mod[1:2])

    def norm_mod(v):
        ms = jnp.mean(v * v, axis=-1, keepdims=True)
        return (v * lax.rsqrt(ms + EPS) * scale + mod[0:1]).astype(BF16)

    h = norm_mod(x_ref[0])
    h_halo = norm_mod(jnp.concatenate([xprev_ref[0], xnext_ref[0]], axis=0))
    keep_prev = (i > 0).astype(F32)
    keep_next = (i < last).astype(F32)
    rows = lax.broadcasted_iota(jnp.int32, (tm, MXU_TILE), 0)
    hsum = hsum_ref[...]
    def plain_piece(o_ref, start, lo, width):
        o_ref[0, :, lo:lo + width] = _dot(h, w_ref[:, start + lo:start + lo + width]).astype(o_ref.dtype)

    def conv_piece(o_ref, start, lo, kind):
        conv_col, do_norm, is_q = kind
        w_cols = w_ref[:, start + lo:start + lo + MXU_TILE]
        p = _dot(h, w_cols)
        p_halo = _dot(h_halo, w_cols)
        p_prev = jnp.where(rows == 0, p_halo[halo - 1:halo] * keep_prev, pltpu.roll(p, 1, 0))
        p_next = jnp.where(rows == tm - 1, p_halo[halo:halo + 1] * keep_next, pltpu.roll(p, tm - 1, 0))
        cw = cw_ref[:, conv_col + lo:conv_col + lo + MXU_TILE]
        y = p_prev * cw[0:1] + p * cw[1:2] + p_next * cw[2:3]
        y = y * _sigmoid(y)
        if do_norm:
            y = y * lax.rsqrt(_dot((y * y).astype(BF16), hsum) + EPS)
            if is_q:
                y = y * (HEAD_DIM ** -0.5)
        o_ref[0, :, lo:lo + MXU_TILE] = y.astype(o_ref.dtype)

    plain, conv = [], []
    for o_ref, (start, size, kind) in zip(out_refs, segs):
        if kind is None:
            plain += [functools.partial(plain_piece, o_ref, start, lo, min(2 * MXU_TILE, size - lo))
                      for lo in range(0, size, 2 * MXU_TILE)]
        else:
            conv += [functools.partial(conv_piece, o_ref, start, lo, kind) for lo in range(0, size, MXU_TILE)]
    while plain or conv:
        if conv:
            conv.pop(0)()
        if plain:
            plain.pop(0)()


def _inproj(x, mod3, mod_row, g_norm, w, conv_w, hsum, segs, dtypes, tm):
    b, t, d = x.shape
    n = w.shape[1]
    halo = 8
    r = tm // halo
    nblk = t // halo
    if mod_row is None:
        mod_map = lambda bi, i: (bi, 0, 0)
    else:
        mod_map = lambda bi, i: (mod_row, 0, 0)
    const2 = lambda bi, i: (0, 0)
    out_shape = [jax.ShapeDtypeStruct((b, t, size), dt) for (_, size, _), dt in zip(segs, dtypes)]
    out_specs = [pl.BlockSpec((1, tm, size), lambda bi, i: (bi, i, 0)) for (_, size, _) in segs]
    return pl.pallas_call(
        functools.partial(_inproj_kernel, segs=segs, tm=tm, halo=halo),
        grid=(b, t // tm),
        in_specs=[pl.BlockSpec((1, tm, d), lambda bi, i: (bi, i, 0)),
                  pl.BlockSpec((1, halo, d), lambda bi, i: (bi, jnp.maximum(i * r - 1, 0), 0)),
                  pl.BlockSpec((1, halo, d), lambda bi, i: (bi, jnp.minimum((i + 1) * r, nblk - 1), 0)),
                  pl.BlockSpec((1, 6, d), mod_map),
                  pl.BlockSpec((1, d), const2),
                  pl.BlockSpec((d, n), const2),
                  pl.BlockSpec(conv_w.shape, const2),
                  pl.BlockSpec(hsum.shape, const2)],
        out_specs=out_specs,
        out_shape=out_shape,
        compiler_params=_cparams(2),
        name="inproj",
    )(x, x, x, mod3, g_norm.reshape(1, d), w, conv_w, hsum)


def _rope(x, cos, sin, lane):
    swapped = jnp.where((lane % 32) < 16, pltpu.roll(x, LANES - 16, 1), pltpu.roll(x, 16, 1))
    return x * cos + swapped * sin


def _attn_kernel(a_ref, kvc_ref, cos_ref, sin_ref, qg_ref, kg_ref, sink_ref, havg_ref,
                 o_ref, k_s, vt_s, *, seq, ctx, qb):
    n = pl.program_id(1)
    havg = havg_ref[...]
    prep_rows = 256
    ctx_blocks = ctx // ATTN_BLOCK

    @pl.when(n == 0)
    def _prep():
        kg = kg_ref[...]
        kc = kvc_ref[0, :, 0:KV_WIDTH].astype(F32)
        ms = _dot((kc * kc).astype(BF16), havg)
        k_s[0:ctx, :] = (kc * lax.rsqrt(ms + EPS) * kg).astype(BF16)
        vc = kvc_ref[0, :, KV_WIDTH:2 * KV_WIDTH].astype(F32)
        for j in range(ctx_blocks):
            vt_s[j] = vc[j * ATTN_BLOCK:(j + 1) * ATTN_BLOCK, :].T.astype(BF16)
        lane = lax.broadcasted_iota(jnp.int32, (prep_rows, LANES), 1)
        for r0 in range(0, seq, prep_rows):
            kx = a_ref[0, r0:r0 + prep_rows, _OFF_AK:_OFF_AK + KV_WIDTH].astype(F32)
            ms = _dot((kx * kx).astype(BF16), havg)
            kx = kx * lax.rsqrt(ms + EPS) * kg
            kx = _rope(kx, cos_ref[r0:r0 + prep_rows, :], sin_ref[r0:r0 + prep_rows, :], lane)
            k_s[ctx + r0:ctx + r0 + prep_rows, :] = kx.astype(BF16)
            vx = a_ref[0, r0:r0 + prep_rows, _OFF_AV:_OFF_AV + KV_WIDTH].astype(F32)
            for j in range(prep_rows // ATTN_BLOCK):
                vt_s[ctx_blocks + r0 // ATTN_BLOCK + j] = vx[j * ATTN_BLOCK:(j + 1) * ATTN_BLOCK, :].T.astype(BF16)

    lane = lax.broadcasted_iota(jnp.int32, (ATTN_BLOCK, LANES), 1)
    qg = qg_ref[...]
    n_band = 3
    n_win = n_band * ATTN_BLOCK
    n_keys = n_win + ctx
    key_row = lax.broadcasted_iota(jnp.int32, (n_win, ATTN_BLOCK), 0)
    q_lane = lax.broadcasted_iota(jnp.int32, (n_win, ATTN_BLOCK), 1)
    zeros = jnp.zeros((HEAD_DIM, ATTN_BLOCK), BF16)
    groups = range(ATTN_KV_HEADS)
    sink_rows = [jnp.concatenate([sink_ref[g * ATTN_GROUP + r:g * ATTN_GROUP + r + 1, :] for r in range(ATTN_GROUP)],
                                 axis=1) * LOG2_E for g in groups]

    units = []
    k_alls, vt_alls, valids, rhss = [], [], [], {}
    for sb in range(qb):
        nq = n * qb + sb
        q0 = pl.multiple_of(nq * ATTN_BLOCK, ATTN_BLOCK)
        cos = cos_ref[pl.ds(q0, ATTN_BLOCK), :]
        sin = sin_ref[pl.ds(q0, ATTN_BLOCK), :]
        qts = []
        for j in range(ATTN_WIDTH // LANES):
            qx = a_ref[0, pl.ds(q0, ATTN_BLOCK), j * LANES:(j + 1) * LANES].astype(F32)
            ms = _dot((qx * qx).astype(BF16), havg)
            qx = _rope(qx * lax.rsqrt(ms + EPS) * qg, cos, sin, lane) * (HEAD_DIM ** -0.5 * LOG2_E)
            qts.append(qx.T.astype(BF16))
        blk0 = jnp.clip(nq - 1, 0, seq // ATTN_BLOCK - n_band)
        start = pl.multiple_of(blk0 * ATTN_BLOCK, ATTN_BLOCK)
        valid = jnp.abs(q0 + q_lane - (start + key_row)) <= WINDOW
        valids.append(jnp.concatenate([valid] * ATTN_GROUP, axis=1))
        k_alls.append(jnp.concatenate([k_s[pl.ds(ctx + start, n_win), :], k_s[0:ctx, :]], axis=0))
        vt_loc = vt_s[pl.ds(ctx_blocks + blk0, n_band)]
        vt_alls.append(jnp.concatenate([vt_loc[j] for j in range(n_band)] + [vt_s[j] for j in range(ctx_blocks)],
                                       axis=1))
        for g in groups:
            cols = []
            for r in range(ATTN_GROUP):
                h = g * ATTN_GROUP + r
                piece = qts[h // 2][(h % 2) * HEAD_DIM:(h % 2 + 1) * HEAD_DIM, :]
                cols.append(jnp.concatenate([piece, zeros] if g == 0 else [zeros, piece], axis=0))
            rhss[(sb, g)] = jnp.concatenate(cols, axis=1)
            units.append((sb, g))
        yield

    ss = []
    for sb, g in units:
        s = _dot(k_alls[sb], rhss[(sb, g)])
        ss.append(jnp.concatenate([jnp.where(valids[sb], s[0:n_win], NEG_INF), s[n_win:n_keys]], axis=0))
        yield
    p_bfs, invs = [], []
    for s, (sb, g) in zip(ss, units):
        m = jnp.maximum(jnp.max(s, axis=0, keepdims=True), sink_rows[g])
        p = jnp.exp2(s - m)
        invs.append(1.0 / (jnp.sum(p, axis=0, keepdims=True) + jnp.exp2(sink_rows[g] - m)))
        p_bfs.append(p.astype(BF16))
        yield
    pieces = {sb: [] for sb in range(qb)}
    for (sb, g), p_bf, inv in zip(units, p_bfs, invs):
        vt_g = vt_alls[sb][g * HEAD_DIM:(g + 1) * HEAD_DIM, :]
        for pair in range(ATTN_GROUP // 2):
            lanes = slice(pair * MXU_TILE, (pair + 1) * MXU_TILE)
            ot = _dot(vt_g, p_bf[:, lanes]) * inv[:, lanes]
            pieces[sb] += [ot[:, 0:ATTN_BLOCK], ot[:, ATTN_BLOCK:2 * ATTN_BLOCK]]
        yield
    for sb in range(qb):
        outs = [jnp.concatenate([pieces[sb][2 * j], pieces[sb][2 * j + 1]], axis=0).T
                for j in range(ATTN_Q_HEADS // 2)]
        o_ref[0, sb * ATTN_BLOCK:(sb + 1) * ATTN_BLOCK, :] = jnp.concatenate(outs, axis=1).astype(o_ref.dtype)


def _run_parts(parts, b, name):
    steps = parts[0]["steps"]
    assert all(p["steps"] == steps for p in parts)
    counts = [(len(p["args"]), len(p["out_shape"]), len(p["scratch_shapes"])) for p in parts]
    n_in = sum(c[0] for c in counts)
    n_out = sum(c[1] for c in counts)

    def kern(*refs):
        i_pos, o_pos, s_pos = 0, n_in, n_in + n_out
        bodies = []
        for p, (ci, co, cs) in zip(parts, counts):
            bodies.append(p["kernel"](*refs[i_pos:i_pos + ci], *refs[o_pos:o_pos + co], *refs[s_pos:s_pos + cs]))
            i_pos, o_pos, s_pos = i_pos + ci, o_pos + co, s_pos + cs
        while bodies:
            for body in list(bodies):
                try:
                    next(body)
                except StopIteration:
                    bodies.remove(body)

    outs = pl.pallas_call(
        kern,
        grid=(b, steps),
        in_specs=[s for p in parts for s in p["in_specs"]],
        out_specs=[s for p in parts for s in p["out_specs"]],
        out_shape=[s for p in parts for s in p["out_shape"]],
        scratch_shapes=[s for p in parts for s in p["scratch_shapes"]],
        compiler_params=_cparams(2),
        name=name,
    )(*[a for p in parts for a in p["args"]])
    res, pos = [], 0
    for _, co, _ in counts:
        res.append(outs[pos:pos + co])
        pos += co
    return res


def _attn_parts(a_lat, kv_ctx, cos, sin, qg, kg, sink, havg, qb):
    b, s, wa = a_lat.shape
    ctx = kv_ctx.shape[1]
    nb = s // ATTN_BLOCK
    const2 = lambda bi, n: (0, 0)
    return dict(
        kernel=functools.partial(_attn_kernel, seq=s, ctx=ctx, qb=qb),
        steps=nb // qb,
        in_specs=[pl.BlockSpec((1, s, wa), lambda bi, n: (bi, 0, 0)),
                  pl.BlockSpec((1, ctx, 2 * KV_WIDTH), lambda bi, n: (bi, 0, 0)),
                  pl.BlockSpec((s, LANES), const2),
                  pl.BlockSpec((s, LANES), const2),
                  pl.BlockSpec((1, LANES), const2),
                  pl.BlockSpec((1, LANES), const2),
                  pl.BlockSpec((ATTN_Q_HEADS, LANES), const2),
                  pl.BlockSpec((LANES, LANES), const2)],
        out_specs=[pl.BlockSpec((1, qb * ATTN_BLOCK, ATTN_WIDTH), lambda bi, n: (bi, n, 0))],
        out_shape=[jax.ShapeDtypeStruct((b, s, ATTN_WIDTH), BF16)],
        scratch_shapes=[pltpu.VMEM((ctx + s, KV_WIDTH), BF16),
                        pltpu.VMEM(((ctx + s) // ATTN_BLOCK, KV_WIDTH, ATTN_BLOCK), BF16)],
        args=[a_lat, kv_ctx, cos, sin, qg, kg, sink, havg])


def _bdiag(x_lane, bmask):
    return jnp.where(bmask, jnp.concatenate([x_lane] * HEADS_PER_TILE, axis=0), jnp.zeros((), x_lane.dtype))


def _unit_tri_inverses(a_list, eye_l, sub_mask, bmask):
    ads = [jnp.where(sub_mask, a, 0.0) for a in a_list]
    ys = [jnp.where(sub_mask, 0.0, a) for a in a_list]
    ps = [eye_l - ad for ad in ads]
    pws = [ad.astype(BF16) for ad in ads]
    n_levels = int(np.log2(DN_SUB))
    c1, c2, c3 = CHUNK, 2 * CHUNK, 3 * CHUNK
    for level in range(n_levels):
        rhss = [_bdiag(pw, bmask) for pw in pws]
        if level == 0:
            boths = [_dot(jnp.concatenate([pw, y.astype(BF16)], axis=0), rhs) for pw, y, rhs in zip(pws, ys, rhss)]
            pws = [both[0:c1].astype(BF16) for both in boths]
            ys = [y - both[c1:c2] for y, both in zip(ys, boths)]
        elif level < n_levels - 1:
            boths = [_dot(jnp.concatenate([pw, p.astype(BF16), y.astype(BF16)], axis=0), rhs)
                     for pw, p, y, rhs in zip(pws, ps, ys, rhss)]
            pws = [both[0:c1].astype(BF16) for both in boths]
            ps = [p + both[c1:c2] for p, both in zip(ps, boths)]
            ys = [y + both[c2:c3] for y, both in zip(ys, boths)]
        else:
            boths = [_dot(jnp.concatenate([p.astype(BF16), y.astype(BF16)], axis=0), rhs)
                     for p, y, rhs in zip(ps, ys, rhss)]
            ps = [p + both[0:c1] for p, both in zip(ps, boths)]
            ys = [y + both[c1:c2] for y, both in zip(ys, boths)]
        yield
    assert CHUNK // DN_SUB == 4
    b_bfs = [y.astype(BF16) for y in ys]
    boths = [_dot(jnp.concatenate([b, p.astype(BF16)], axis=0), _bdiag(b, bmask)) for b, p in zip(b_bfs, ps)]
    yield
    zs = [p - both[c1:c2] for p, both in zip(ps, boths)]
    return [z + _dot(z.astype(BF16), _bdiag(both[0:c1].astype(BF16), bmask)) for z, both in zip(zs, boths)]


def _dn_factors(ab, d, arow, dtrow, exp2_ref, tri_ref, eye_t):
    lane = lax.broadcasted_iota(jnp.int32, ab.shape, 1)
    z = ab + dtrow
    softplus = jnp.maximum(z, 0.0) + jnp.log(1.0 + jnp.exp(-jnp.abs(z)))
    is_g = (lane % 32) < 16
    x = jnp.where(is_g, -arow * softplus, _sigmoid(ab))
    x_hi = x.astype(BF16)
    x_lo = (x - x_hi.astype(F32)).astype(BF16)
    xhl = jnp.where(lane < 32, x_hi, x_lo)[:, 0:64]
    y = _dot(xhl, exp2_ref[d])
    ge = y[:, 0:DN_WIDTH]
    be = y[:, DN_WIDTH:2 * DN_WIDTH]
    ge_hi = ge.astype(BF16)
    ge_lo = (ge - ge_hi.astype(F32)).astype(BF16)
    gi = _dot(tri_ref[d], jnp.concatenate([ge_hi, ge_lo], axis=0))
    gj = jnp.sum(gi * eye_t, axis=0, keepdims=True)
    return be, gi, jnp.broadcast_to(gj, gi.shape)


def _block_mask():
    return (lax.broadcasted_iota(jnp.int32, (MXU_TILE, MXU_TILE), 0) // HEAD_DIM
            == lax.broadcasted_iota(jnp.int32, (MXU_TILE, MXU_TILE), 1) // HEAD_DIM)


def _dnprep_kernel(*refs, cb, want_o):
    if want_o:
        q_ref, k_ref, v_ref, ab_ref = refs[:4]
        refs = refs[4:]
    else:
        q_ref = None
        k_ref, v_ref, ab_ref = refs[:3]
        refs = refs[3:]
    arow_ref, dtrow_ref, exp2_ref, tri_ref, w_o, uv_o, kt_o, dl_o = refs[:8]
    qd_o, in_o = refs[8:10] if want_o else (None, None)
    n_tiles = DN_WIDTH // MXU_TILE
    row = lax.broadcasted_iota(jnp.int32, (CHUNK, MXU_TILE), 0)
    colj = lax.broadcasted_iota(jnp.int32, (CHUNK, MXU_TILE), 1) % HEAD_DIM
    eye_l = (row == colj).astype(F32)
    eye_t = jnp.concatenate([eye_l] * n_tiles, axis=1)
    bmask = _block_mask()
    arow = arow_ref[...]
    dtrow = dtrow_ref[...]

    fac = {}
    shared = {}
    for c in range(cb):
        rows = slice(c * CHUNK, (c + 1) * CHUNK)
        ab = ab_ref[0, rows, :]
        for d in range(N_DIR):
            fac[(c, d)] = _dn_factors(ab, d, arow, dtrow, exp2_ref, tri_ref, eye_t)
        for g in range(n_tiles):
            lanes = slice(g * MXU_TILE, (g + 1) * MXU_TILE)
            k_l = k_ref[0, rows, lanes]
            kbd = _bdiag(k_l, bmask)
            if want_o:
                kq = _dot_nt(jnp.concatenate([k_l, q_ref[0, rows, lanes]], axis=0), kbd)
                shared[(c, g)] = (kq[0:CHUNK], kq[CHUNK:2 * CHUNK])
            else:
                shared[(c, g)] = (_dot_nt(k_l, kbd), None)
        yield

    units = [(c, g, d) for c in range(cb) for g in range(n_tiles) for d in range(N_DIR)]
    decs = []
    for c, g, d in units:
        lanes = slice(g * MXU_TILE, (g + 1) * MXU_TILE)
        be, gi, gj = fac[(c, d)]
        lower = (row > colj) if d == 0 else (row < colj)
        decs.append(jnp.where(lower, jnp.exp(jnp.where(lower, gi[:, lanes] - gj[:, lanes], 0.0)), 0.0))
    a_list = [fac[(c, d)][0][:, g * MXU_TILE:(g + 1) * MXU_TILE] * dec * shared[(c, g)][0]
              for (c, g, d), dec in zip(units, decs)]
    yield
    tinvs = yield from _unit_tri_inverses(a_list, eye_l, (row // DN_SUB) == (colj // DN_SUB), bmask)

    for idx, ((c, g, d), dec, tinv) in enumerate(zip(units, decs, tinvs)):
        if idx % (n_tiles * N_DIR) == 0:
            yield
        rows = slice(c * CHUNK, (c + 1) * CHUNK)
        lanes = slice(g * MXU_TILE, (g + 1) * MXU_TILE)
        be, gi, _ = fac[(c, d)]
        be, gi = be[:, lanes], gi[:, lanes]
        last = CHUNK - 1 if d == 0 else 0
        e_g = jnp.exp(gi)
        gl_row = gi[last:last + 1, :]
        kf = k_ref[0, rows, lanes].astype(F32)
        vf = v_ref[0, rows, lanes].astype(F32)
        rhs = jnp.concatenate([_bdiag((be * e_g * kf).astype(BF16), bmask), _bdiag((be * vf).astype(BF16), bmask)],
                              axis=1)
        wu = _dot(tinv.astype(BF16), rhs)
        w_o[0, d, rows, lanes] = wu[:, 0:MXU_TILE].astype(BF16)
        uv_o[0, d, rows, lanes] = wu[:, MXU_TILE:2 * MXU_TILE]
        kt_o[0, d, rows, lanes] = (jnp.exp(gl_row - gi) * kf).astype(BF16)
        dl_o[0, d, c, :, lanes] = jnp.exp(gl_row)
        if want_o:
            qd_o[0, d, rows, lanes] = (e_g * q_ref[0, rows, lanes].astype(F32)).astype(BF16)
            in_o[0, d, rows, lanes] = ((dec + eye_l) * shared[(c, g)][1]).astype(BF16)


def _dnprep_parts(q, k, v, ab, arow, dtrow, exp2, tri, cb):
    want_o = q is not None
    b, t, w = k.shape
    tb = cb * CHUNK
    tok = lambda bi, i: (bi, i, 0)
    const2 = lambda bi, i: (0, 0)
    const3 = lambda bi, i: (0, 0, 0)
    dir_tok = lambda bi, i: (bi, 0, i, 0)
    data = ([q] if want_o else []) + [k, v]
    in_specs = ([pl.BlockSpec((1, tb, w), tok)] * len(data) + [pl.BlockSpec((1, tb, LANES), tok),
                pl.BlockSpec((1, LANES), const2), pl.BlockSpec((1, LANES), const2),
                pl.BlockSpec(exp2.shape, const3), pl.BlockSpec(tri.shape, const3)])
    big = lambda dt: jax.ShapeDtypeStruct((b, N_DIR, t, w), dt)
    big_spec = pl.BlockSpec((1, N_DIR, tb, w), dir_tok)
    out_shape = [big(BF16), big(F32), big(BF16), jax.ShapeDtypeStruct((b, N_DIR, t // CHUNK, 1, w), F32)]
    out_specs = [big_spec, big_spec, big_spec, pl.BlockSpec((1, N_DIR, cb, 1, w), lambda bi, i: (bi, 0, i, 0, 0))]
    if want_o:
        out_shape += [big(BF16), big(BF16)]
        out_specs += [big_spec, big_spec]
    return dict(kernel=functools.partial(_dnprep_kernel, cb=cb, want_o=want_o), steps=t // tb, in_specs=in_specs,
                out_specs=out_specs, out_shape=out_shape, scratch_shapes=[],
                args=data + [ab, arow, dtrow, exp2, tri])


def _dnscan_kernel(*refs, n_chunk, bb, want_o, have_s0, want_s):
    n_in = 6 if want_o else 4
    dir_refs = [refs[0:n_in], refs[n_in:2 * n_in]]
    pos = 2 * n_in
    s0_ref = refs[pos] if have_s0 else None
    pos += int(have_s0)
    o_refs = refs[pos:pos + N_DIR] if want_o else None
    pos += N_DIR if want_o else 0
    sout_ref = refs[pos] if want_s else None
    pos += int(want_s)
    s_scr = refs[pos]
    n_tiles = DN_WIDTH // MXU_TILE
    i = pl.program_id(1)

    @pl.when(i == 0)
    def _init():
        if have_s0:
            s_scr[...] = s0_ref[...]
        else:
            s_scr[...] = jnp.zeros_like(s_scr)

    bmask = _block_mask()
    chains = [(bi, d, g) for bi in range(bb) for d in range(N_DIR) for g in range(n_tiles)]

    def body(j, carry):
        cidx = (j, n_chunk - 1 - j)
        r0s = [pl.multiple_of(cidx[d] * CHUNK, CHUNK) for d in range(N_DIR)]
        s_olds, r1s = [], []
        for bi, d, g in chains:
            lanes = slice(g * MXU_TILE, (g + 1) * MXU_TILE)
            w = dir_refs[d][0][bi, 0, pl.ds(r0s[d], CHUNK), lanes]
            if want_o:
                w = jnp.concatenate([w, dir_refs[d][4][bi, 0, pl.ds(r0s[d], CHUNK), lanes]], axis=0)
            s_old = s_scr[bi, d * n_tiles + g]
            s_olds.append(s_old)
            r1s.append(_dot(w, s_old.astype(BF16)))
        u_bfs = []
        for (bi, d, g), r1 in zip(chains, r1s):
            lanes = slice(g * MXU_TILE, (g + 1) * MXU_TILE)
            u_bfs.append((dir_refs[d][1][bi, 0, pl.ds(r0s[d], CHUNK), lanes] - r1[0:CHUNK]).astype(BF16))
        for (bi, d, g), r1, u_bf, s_old in zip(chains, r1s, u_bfs, s_olds):
            lanes = slice(g * MXU_TILE, (g + 1) * MXU_TILE)
            kt = dir_refs[d][2][bi, 0, pl.ds(r0s[d], CHUNK), lanes]
            ds = jnp.where(bmask, _dot_tn(kt, u_bf), 0.0)
            dl = dir_refs[d][3][bi, 0, cidx[d]][:, lanes]
            s_scr[bi, d * n_tiles + g] = s_old * dl + ds
            if want_o:
                intra = dir_refs[d][5][bi, 0, pl.ds(r0s[d], CHUNK), lanes]
                o_refs[d][bi, pl.ds(r0s[d], CHUNK), lanes] = r1[CHUNK:2 * CHUNK] + _dot(intra, _bdiag(u_bf, bmask))
        return carry

    lax.fori_loop(0, n_chunk, body, 0)

    if want_s:
        @pl.when(i == pl.num_programs(1) - 1)
        def _fin():
            sout_ref[...] = s_scr[...]


def _dnscan(prep, s0, tb, bb, want_s):
    want_o = len(prep) == 6
    b, _, t, w = prep[0].shape
    n_t = t // tb
    n_chunk = tb // CHUNK
    n_chain = N_DIR * (w // MXU_TILE)

    def specs(d):
        blk = (lambda bi, i: i) if d == 0 else (lambda bi, i: n_t - 1 - i)
        big = pl.BlockSpec((bb, 1, tb, w), lambda bi, i: (bi, d, blk(bi, i), 0))
        dl = pl.BlockSpec((bb, 1, n_chunk, 1, w), lambda bi, i: (bi, d, blk(bi, i), 0, 0))
        return [big, big, big, dl] + ([big, big] if want_o else [])

    in_specs = specs(0) + specs(1)
    args = list(prep) + list(prep)
    state_spec = pl.BlockSpec((bb, n_chain, MXU_TILE, MXU_TILE), lambda bi, i: (bi, 0, 0, 0))
    if s0 is not None:
        in_specs.append(state_spec)
        args.append(s0)
    out_shape, out_specs = [], []
    if want_o:
        out_shape += [jax.ShapeDtypeStruct((b, t, w), F32)] * N_DIR
        out_specs += [pl.BlockSpec((bb, tb, w), lambda bi, i: (bi, i, 0)),
                      pl.BlockSpec((bb, tb, w), lambda bi, i: (bi, n_t - 1 - i, 0))]
    if want_s:
        out_shape.append(jax.ShapeDtypeStruct((b, n_chain, MXU_TILE, MXU_TILE), F32))
        out_specs.append(state_spec)
    return pl.pallas_call(
        functools.partial(_dnscan_kernel, n_chunk=n_chunk, bb=bb, want_o=want_o, have_s0=s0 is not None,
                          want_s=want_s),
        grid=(b // bb, n_t),
        in_specs=in_specs,
        out_specs=out_specs,
        out_shape=out_shape,
        scratch_shapes=[pltpu.VMEM((bb, n_chain, MXU_TILE, MXU_TILE), F32)],
        compiler_params=_cparams(2),
        name="dnscan",
    )(*args)


def _tail_kernel(x_ref, mod_ref, ya_ref, odf_ref, odb_ref, z_ref, gate_ref, dng_ref, havg_ref, wba_ref, wbd_ref,
                 wo_ref, gn2_ref, w1_ref, w2_ref, o_ref, *, ff_chunk):
    havg = havg_ref[...]
    dng = dng_ref[...]
    yd_parts = []
    for j in range(DN_WIDTH // MXU_TILE):
        sl = slice(j * MXU_TILE, (j + 1) * MXU_TILE)
        od = odf_ref[0, :, sl] + odb_ref[0, :, sl]
        ms = _dot((od * od).astype(BF16), havg)
        z = z_ref[0, :, sl].astype(F32)
        yd_parts.append((od * lax.rsqrt(ms + EPS) * dng * (z * _sigmoid(z))).astype(BF16))
    yd = jnp.concatenate(yd_parts, axis=1)
    ga = gate_ref[0, :, 0:D_MODEL].astype(F32)
    gd = gate_ref[0, :, D_MODEL:2 * D_MODEL].astype(F32)
    y = _sigmoid(ga) * _dot(ya_ref[0], wba_ref[...]) + _sigmoid(gd) * _dot(yd, wbd_ref[...])
    br = _dot(y.astype(BF16), wo_ref[...])
    mod = mod_ref[0]
    out1 = x_ref[0] + mod[2:3] * br
    ms2 = jnp.mean(out1 * out1, axis=-1, keepdims=True)
    hm = (out1 * lax.rsqrt(ms2 + EPS) * (gn2_ref[...] * (1.0 + mod[4:5])) + mod[3:4]).astype(BF16)
    acc = None
    for j in range(D_FF // ff_chunk):
        a = jnp.maximum(_dot(hm, w1_ref[:, j * ff_chunk:(j + 1) * ff_chunk]), 0.0)
        part = _dot((a * a).astype(BF16), w2_ref[j * ff_chunk:(j + 1) * ff_chunk, :])
        acc = part if acc is None else acc + part
    o_ref[0] = out1 + mod[5:6] * acc


def _resident(shape):
    return pl.BlockSpec(shape, lambda bi, i: (0,) * len(shape), pipeline_mode=pl.Buffered(1))


def _tail(x, mod3, y_attn, o_df, o_db, z, gates, dng, havg, wba, wbd, wo, gn2, w1, w2, tm):
    b, t, d = x.shape
    tok = lambda bi, i: (bi, i, 0)
    return pl.pallas_call(
        functools.partial(_tail_kernel, ff_chunk=1024),
        grid=(b, t // tm),
        in_specs=[pl.BlockSpec((1, tm, d), tok),
                  pl.BlockSpec((1, 6, d), lambda bi, i: (bi, 0, 0)),
                  pl.BlockSpec((1, tm, ATTN_WIDTH), tok),
                  pl.BlockSpec((1, tm, DN_WIDTH), tok),
                  pl.BlockSpec((1, tm, DN_WIDTH), tok),
                  pl.BlockSpec((1, tm, DN_WIDTH), tok),
                  pl.BlockSpec((1, tm, 2 * d), tok),
                  _resident((1, MXU_TILE)),
                  _resident((MXU_TILE, MXU_TILE)),
                  _resident(wba.shape),
                  _resident(wbd.shape),
                  _resident(wo.shape),
                  _resident((1, d)),
                  _resident(w1.shape),
                  _resident(w2.shape)],
        out_specs=pl.BlockSpec((1, tm, d), tok),
        out_shape=jax.ShapeDtypeStruct((b, t, d), F32),
        compiler_params=_cparams(2),
        name="tail",
    )(x, mod3, y_attn, o_df, o_db, z, gates, dng, havg, wba, wbd, wo, gn2, w1, w2)


def _head_avg(n, scale):
    idx = np.arange(n) // HEAD_DIM
    return jnp.asarray((idx[:, None] == idx[None, :]).astype(np.float32) * scale, BF16)


def _dn_expand_matrix():
    n = N_DIR * DN_HEADS
    m = np.zeros((N_DIR, 4 * n, 2 * DN_WIDTH), np.float32)
    for d in range(N_DIR):
        for part in range(2):
            for h in range(DN_HEADS):
                idx = d * DN_HEADS + h
                m[d, part * 2 * n + idx, h * HEAD_DIM:(h + 1) * HEAD_DIM] = 1.0
                m[d, part * 2 * n + n + idx, DN_WIDTH + h * HEAD_DIM:DN_WIDTH + (h + 1) * HEAD_DIM] = 1.0
    return jnp.asarray(m, BF16)


def _tri_matrices():
    i = np.arange(CHUNK)
    low = (i[:, None] >= i[None, :]).astype(np.float32)
    up = (i[:, None] <= i[None, :]).astype(np.float32)
    return jnp.asarray(np.stack([np.concatenate([low, low], axis=1), np.concatenate([up, up], axis=1)]), BF16)


def _rope_tables(seq):
    half = HEAD_DIM // 2
    n_freq = half // 2
    freqs = ROPE_BASE ** (-jnp.arange(n_freq, dtype=F32) / n_freq)
    pos = jnp.arange(seq)
    ang_r = (pos // GRID_W).astype(F32)[:, None] * freqs
    ang_c = (pos % GRID_W).astype(F32)[:, None] * freqs
    cos = jnp.concatenate([jnp.cos(ang_r)] * 2 + [jnp.cos(ang_c)] * 2, axis=1)
    sin = jnp.concatenate([-jnp.sin(ang_r), jnp.sin(ang_r), -jnp.sin(ang_c), jnp.sin(ang_c)], axis=1)
    reps = LANES // HEAD_DIM
    return jnp.tile(cos, (1, reps)), jnp.tile(sin, (1, reps))


def _pad_cols(w, n):
    return jnp.pad(w, ((0, 0), (0, n - w.shape[1])))


def kernel(x, c, ctx, c_ctx, w_ada, b_ada, g_norm1, w_in, q_norm_g, k_norm_g, attn_sink, conv_w, a_log, dt_bias,
           dn_norm_g, w_br_attn, w_br_dn, w_out, g_norm2, w_mlp1, w_mlp2):
    depth = w_ada.shape[0]
    assert depth == 1, "single-layer trunk only"
    b, s, d = x.shape
    n_ctx = ctx.shape[1]
    assert d == D_MODEL and w_in.shape[-1] == _IN_WIDTH
    assert s >= 3 * ATTN_BLOCK and s % ATTN_BLOCK == 0 and s % CHUNK == 0 and n_ctx % CHUNK == 0
    out_dtype = x.dtype
    w_in0 = w_in[0]

    mod_rows = 16
    cc = jnp.concatenate([c.astype(F32), c_ctx.astype(F32)[None], jnp.zeros((mod_rows - b - 1, d), F32)], axis=0)
    mod = _ada(cc, w_ada[0], b_ada[0])
    mod3 = mod.reshape(mod_rows, 6, d)

    ab_cols = jnp.concatenate([w_in0[:, _OFF_DA:_OFF_GA]] * 2, axis=1)
    w_lat = jnp.concatenate([w_in0[:, :_OFF_DA], w_in0[:, _OFF_GA:], _pad_cols(ab_cols, LANES)], axis=1).astype(BF16)
    hsum = _head_avg(MXU_TILE, 1.0)
    segs_lat = ((0, _OFF_DQ, None),
                (_OFF_DQ, DN_WIDTH, (0, True, True)), (_OFF_DK, DN_WIDTH, (DN_WIDTH, True, False)),
                (_OFF_DV, DN_WIDTH, (2 * DN_WIDTH, False, False)),
                (_OFF_DZ, DN_WIDTH, None), (_OFF_DA, 2 * D_MODEL, None), (_OFF_DA + 2 * D_MODEL, LANES, None))
    a_lat, q_d, k_d, v_d, z_lat, gates, ab_lat = _inproj(x, mod3, None, g_norm1[0], w_lat, conv_w[0], hsum, segs_lat,
                                                         (BF16, BF16, BF16, BF16, BF16, BF16, F32), tm=512)
    w_ctx = jnp.concatenate([w_in0[:, _OFF_AK:_OFF_DQ], w_in0[:, _OFF_DK:_OFF_DZ], _pad_cols(ab_cols, LANES)],
                            axis=1).astype(BF16)
    segs_ctx = ((0, 2 * KV_WIDTH, None),
                (2 * KV_WIDTH, DN_WIDTH, (0, True, False)), (2 * KV_WIDTH + DN_WIDTH, DN_WIDTH, (DN_WIDTH, False, False)),
                (2 * KV_WIDTH + 2 * DN_WIDTH, LANES, None))
    kv_ctx, k_dc, v_dc, ab_ctx = _inproj(ctx, mod3, b, g_norm1[0], w_ctx, conv_w[0][:, DN_WIDTH:], hsum, segs_ctx,
                                         (BF16, BF16, BF16, F32), tm=n_ctx)

    cos, sin = _rope_tables(s)
    reps = LANES // HEAD_DIM
    n_gate = N_DIR * DN_HEADS
    arow = _pad_cols(jnp.tile(jnp.concatenate([jnp.exp(a_log[0]).reshape(1, n_gate), jnp.zeros((1, n_gate), F32)],
                                              axis=1), (1, 2)), LANES)
    dtrow = _pad_cols(jnp.tile(jnp.concatenate([dt_bias[0].reshape(1, n_gate), jnp.zeros((1, n_gate), F32)],
                                               axis=1), (1, 2)), LANES)
    exp2, tri = _dn_expand_matrix(), _tri_matrices()
    dn_cb = 4
    qb = (s // ATTN_BLOCK) // (s // (dn_cb * CHUNK))
    attn_parts = _attn_parts(a_lat, kv_ctx, cos, sin,
                             jnp.tile(q_norm_g[0].astype(F32), reps)[None],
                             jnp.tile(k_norm_g[0].astype(F32), reps)[None],
                             jnp.broadcast_to(attn_sink[0].astype(F32)[:, None], (ATTN_Q_HEADS, LANES)),
                             _head_avg(LANES, 1.0 / HEAD_DIM), qb)
    (y_attn,), prep_lat = _run_parts(
        [attn_parts, _dnprep_parts(q_d, k_d, v_d, ab_lat, arow, dtrow, exp2, tri, dn_cb)], b, "attn_dnprep")
    (prep_ctx,) = _run_parts([_dnprep_parts(None, k_dc, v_dc, ab_ctx, arow, dtrow, exp2, tri, dn_cb)], b, "dnprep")

    bb = max(q for q in (4, 2, 1) if b % q == 0)
    (s_ctx,) = _dnscan(prep_ctx, None, tb=n_ctx, bb=bb, want_s=True)
    o_df, o_db = _dnscan(prep_lat, s_ctx, tb=256, bb=bb, want_s=False)

    out = _tail(x, mod3, y_attn, o_df, o_db, z_lat, gates,
                jnp.tile(dn_norm_g[0].astype(F32), HEADS_PER_TILE)[None], _head_avg(MXU_TILE, 1.0 / HEAD_DIM),
                w_br_attn[0].astype(BF16), w_br_dn[0].astype(BF16), w_out[0].astype(BF16),
                g_norm2[0].reshape(1, d), w_mlp1[0].astype(BF16), w_mlp2[0].astype(BF16), tm=512)
    return out.astype(out_dtype)
```

```python
import functools

import numpy as np
import jax
import jax.numpy as jnp
from jax import lax
from jax.experimental import pallas as pl
from jax.experimental.pallas import tpu as pltpu

F32 = jnp.float32
BF16 = jnp.bfloat16

D_MODEL = 1024
GRID_W = 64
HEAD_DIM = 64
ATTN_Q_HEADS = 8
ATTN_KV_HEADS = 2
ATTN_GROUP = ATTN_Q_HEADS // ATTN_KV_HEADS
WINDOW = 128
ATTN_BLOCK = 128
ROPE_BASE = 10000.0
DN_HEADS = 8
CONV_W = 3
CHUNK = 64
N_DIR = 2
D_FF = 4 * D_MODEL
EPS = 1e-6
NEG_INF = -1e30
LOG2_E = float(np.log2(np.e))

ATTN_WIDTH = ATTN_Q_HEADS * HEAD_DIM
KV_WIDTH = ATTN_KV_HEADS * HEAD_DIM
DN_WIDTH = DN_HEADS * HEAD_DIM
LANES = 128
MXU_TILE = 256
HEADS_PER_TILE = MXU_TILE // HEAD_DIM
DN_SUB = 16
VMEM_LIMIT = 56 * 1024 * 1024

_OFF_AQ = 0
_OFF_AK = _OFF_AQ + ATTN_WIDTH
_OFF_AV = _OFF_AK + KV_WIDTH
_OFF_DQ = _OFF_AV + KV_WIDTH
_OFF_DK = _OFF_DQ + DN_WIDTH
_OFF_DV = _OFF_DK + DN_WIDTH
_OFF_DZ = _OFF_DV + DN_WIDTH
_OFF_DA = _OFF_DZ + DN_WIDTH
_OFF_DB = _OFF_DA + N_DIR * DN_HEADS
_OFF_GA = _OFF_DB + N_DIR * DN_HEADS
_OFF_GD = _OFF_GA + D_MODEL
_IN_WIDTH = _OFF_GD + D_MODEL


def _sigmoid(x):
    return 0.5 * jnp.tanh(0.5 * x) + 0.5


def _dot(a, b):
    return jnp.dot(a, b, preferred_element_type=F32)


def _dot_nt(a, b):
    return lax.dot_general(a, b, (((1,), (1,)), ((), ())), preferred_element_type=F32)


def _dot_tn(a, b):
    return lax.dot_general(a, b, (((0,), (0,)), ((), ())), preferred_element_type=F32)


def _cparams(n_axes):
    return pltpu.CompilerParams(dimension_semantics=("arbitrary",) * n_axes, vmem_limit_bytes=VMEM_LIMIT)


def _ada_kernel(c_ref, w_ref, b_ref, o_ref):
    c = c_ref[...]
    s = c * _sigmoid(c)
    o_ref[...] = _dot(s.astype(BF16), w_ref[...].astype(BF16)) + b_ref[...]


def _ada(cc, w_ada, b_ada):
    rows, d = cc.shape
    n = w_ada.shape[1]
    tn = 1536
    return pl.pallas_call(
        _ada_kernel,
        grid=(n // tn,),
        in_specs=[pl.BlockSpec((rows, d), lambda j: (0, 0)),
                  pl.BlockSpec((d, tn), lambda j: (0, j)),
                  pl.BlockSpec((1, tn), lambda j: (0, j))],
        out_specs=pl.BlockSpec((rows, tn), lambda j: (0, j)),
        out_shape=jax.ShapeDtypeStruct((rows, n), F32),
        compiler_params=_cparams(1),
        name="ada",
    )(cc, w_ada, b_ada.reshape(1, n))


def _inproj_kernel(x_ref, xprev_ref, xnext_ref, mod_ref, g_ref, w_ref, cw_ref, hsum_ref, *out_refs, segs, tm, halo):
    i = pl.program_id(1)
    last = pl.num_programs(1) - 1
    mod = mod_ref[0]
    scale = g_ref[...] * (1.0 + mod[1:2])

    def norm_mod(v):
        ms = jnp.mean(v * v, axis=-1, keepdims=True)
        return (v * lax.rsqrt(ms + EPS) * scale + mod[0:1]).astype(BF16)

    h = norm_mod(x_ref[0])
    h_halo = norm_mod(jnp.concatenate([xprev_ref[0], xnext_ref[0]], axis=0))
    keep_prev = (i > 0).astype(F32)
    keep_next = (i < last).astype(F32)
    rows = lax.broadcasted_iota(jnp.int32, (tm, MXU_TILE), 0)
    hsum = hsum_ref[...]
    def plain_piece(o_ref, start, lo, width):
        o_ref[0, :, lo:lo + width] = _dot(h, w_ref[:, start + lo:start + lo + width]).astype(o_ref.dtype)

    def conv_piece(o_ref, start, lo, kind):
        conv_col, do_norm, is_q = kind
        w_cols = w_ref[:, start + lo:start + lo + MXU_TILE]
        p = _dot(h, w_cols)
        p_halo = _dot(h_halo, w_cols)
        yield
        p_prev = jnp.where(rows == 0, p_halo[halo - 1:halo] * keep_prev, pltpu.roll(p, 1, 0))
        p_next = jnp.where(rows == tm - 1, p_halo[halo:halo + 1] * keep_next, pltpu.roll(p, tm - 1, 0))
        cw = cw_ref[:, conv_col + lo:conv_col + lo + MXU_TILE]
        y = p_prev * cw[0:1] + p * cw[1:2] + p_next * cw[2:3]
        y = y * _sigmoid(y)
        if do_norm:
            y = y * lax.rsqrt(_dot((y * y).astype(BF16), hsum) + EPS)
            if is_q:
                y = y * (HEAD_DIM ** -0.5)
        o_ref[0, :, lo:lo + MXU_TILE] = y.astype(o_ref.dtype)

    plain, conv = [], []
    for o_ref, (start, size, kind) in zip(out_refs, segs):
        if kind is None:
            plain += [functools.partial(plain_piece, o_ref, start, lo, min(MXU_TILE, size - lo))
                      for lo in range(0, size, MXU_TILE)]
        else:
            conv += [functools.partial(conv_piece, o_ref, start, lo, kind) for lo in range(0, size, MXU_TILE)]
    pending = None
    while plain or conv or pending is not None:
        started = conv.pop(0)() if conv else None
        if started is not None:
            next(started)
        if pending is not None:
            for _ in pending:
                pass
        pending = started
        for _ in range(-(-len(plain) // (len(conv) + 1)) if plain else 0):
            plain.pop(0)()


def _inproj(x, mod3, mod_row, g_norm, w, conv_w, hsum, segs, dtypes, tm):
    b, t, d = x.shape
    n = w.shape[1]
    halo = 8
    r = tm // halo
    nblk = t // halo
    if mod_row is None:
        mod_map = lambda bi, i: (bi, 0, 0)
    else:
        mod_map = lambda bi, i: (mod_row, 0, 0)
    const2 = lambda bi, i: (0, 0)
    out_shape = [jax.ShapeDtypeStruct((b, t, size), dt) for (_, size, _), dt in zip(segs, dtypes)]
    out_specs = [pl.BlockSpec((1, tm, size), lambda bi, i: (bi, i, 0)) for (_, size, _) in segs]
    return pl.pallas_call(
        functools.partial(_inproj_kernel, segs=segs, tm=tm, halo=halo),
        grid=(b, t // tm),
        in_specs=[pl.BlockSpec((1, tm, d), lambda bi, i: (bi, i, 0)),
                  pl.BlockSpec((1, halo, d), lambda bi, i: (bi, jnp.maximum(i * r - 1, 0), 0)),
                  pl.BlockSpec((1, halo, d), lambda bi, i: (bi, jnp.minimum((i + 1) * r, nblk - 1), 0)),
                  pl.BlockSpec((1, 6, d), mod_map),
                  pl.BlockSpec((1, d), const2),
                  pl.BlockSpec((d, n), const2),
                  pl.BlockSpec(conv_w.shape, const2),
                  pl.BlockSpec(hsum.shape, const2)],
        out_specs=out_specs,
        out_shape=out_shape,
        compiler_params=_cparams(2),
        name="inproj",
    )(x, x, x, mod3, g_norm.reshape(1, d), w, conv_w, hsum)


def _rope(x, cos, sin, lane):
    swapped = jnp.where((lane % 32) < 16, pltpu.roll(x, LANES - 16, 1), pltpu.roll(x, 16, 1))
    return x * cos + swapped * sin


def _attn_kernel(a_ref, kvc_ref, cos_ref, sin_ref, qg_ref, kg_ref, sink_ref, havg_ref,
                 o_ref, k_s, vt_s, *, seq, ctx, qb):
    n = pl.program_id(1)
    havg = havg_ref[...]
    prep_rows = 256
    ctx_blocks = ctx // ATTN_BLOCK

    @pl.when(n == 0)
    def _prep():
        kg = kg_ref[...]
        kc = kvc_ref[0, :, 0:KV_WIDTH].astype(F32)
        ms = _dot((kc * kc).astype(BF16), havg)
        k_s[0:ctx, :] = (kc * lax.rsqrt(ms + EPS) * kg).astype(BF16)
        vc = kvc_ref[0, :, KV_WIDTH:2 * KV_WIDTH].astype(F32)
        for j in range(ctx_blocks):
            vt_s[j] = vc[j * ATTN_BLOCK:(j + 1) * ATTN_BLOCK, :].T.astype(BF16)
        lane = lax.broadcasted_iota(jnp.int32, (prep_rows, LANES), 1)
        for r0 in range(0, seq, prep_rows):
            kx = a_ref[0, r0:r0 + prep_rows, _OFF_AK:_OFF_AK + KV_WIDTH].astype(F32)
            ms = _dot((kx * kx).astype(BF16), havg)
            kx = kx * lax.rsqrt(ms + EPS) * kg
            kx = _rope(kx, cos_ref[r0:r0 + prep_rows, :], sin_ref[r0:r0 + prep_rows, :], lane)
            k_s[ctx + r0:ctx + r0 + prep_rows, :] = kx.astype(BF16)
            vx = a_ref[0, r0:r0 + prep_rows, _OFF_AV:_OFF_AV + KV_WIDTH].astype(F32)
            for j in range(prep_rows // ATTN_BLOCK):
                vt_s[ctx_blocks + r0 // ATTN_BLOCK + j] = vx[j * ATTN_BLOCK:(j + 1) * ATTN_BLOCK, :].T.astype(BF16)

    lane = lax.broadcasted_iota(jnp.int32, (ATTN_BLOCK, LANES), 1)
    qg = qg_ref[...]
    n_band = 3
    n_win = n_band * ATTN_BLOCK
    n_keys = n_win + ctx
    key_row = lax.broadcasted_iota(jnp.int32, (n_win, ATTN_BLOCK), 0)
    q_lane = lax.broadcasted_iota(jnp.int32, (n_win, ATTN_BLOCK), 1)
    zeros = jnp.zeros((HEAD_DIM, ATTN_BLOCK), BF16)
    groups = range(ATTN_KV_HEADS)
    sink_rows = [jnp.concatenate([sink_ref[g * ATTN_GROUP + r:g * ATTN_GROUP + r + 1, :] for r in range(ATTN_GROUP)],
                                 axis=1) * LOG2_E for g in groups]

    units = []
    k_alls, vt_alls, valids, rhss = [], [], [], {}
    for sb in range(qb):
        nq = n * qb + sb
        q0 = pl.multiple_of(nq * ATTN_BLOCK, ATTN_BLOCK)
        cos = cos_ref[pl.ds(q0, ATTN_BLOCK), :]
        sin = sin_ref[pl.ds(q0, ATTN_BLOCK), :]
        qts = []
        for j in range(ATTN_WIDTH // LANES):
            qx = a_ref[0, pl.ds(q0, ATTN_BLOCK), j * LANES:(j + 1) * LANES].astype(F32)
            ms = _dot((qx * qx).astype(BF16), havg)
            qx = _rope(qx * lax.rsqrt(ms + EPS) * qg, cos, sin, lane) * (HEAD_DIM ** -0.5 * LOG2_E)
            qts.append(qx.T.astype(BF16))
        blk0 = jnp.clip(nq - 1, 0, seq // ATTN_BLOCK - n_band)
        start = pl.multiple_of(blk0 * ATTN_BLOCK, ATTN_BLOCK)
        valid = jnp.abs(q0 + q_lane - (start + key_row)) <= WINDOW
        valids.append(jnp.concatenate([valid] * ATTN_GROUP, axis=1))
        k_alls.append(jnp.concatenate([k_s[pl.ds(ctx + start, n_win), :], k_s[0:ctx, :]], axis=0))
        vt_loc = vt_s[pl.ds(ctx_blocks + blk0, n_band)]
        vt_alls.append(jnp.concatenate([vt_loc[j] for j in range(n_band)] + [vt_s[j] for j in range(ctx_blocks)],
                                       axis=1))
        for g in groups:
            cols = []
            for r in range(ATTN_GROUP):
                h = g * ATTN_GROUP + r
                piece = qts[h // 2][(h % 2) * HEAD_DIM:(h % 2 + 1) * HEAD_DIM, :]
                cols.append(jnp.concatenate([piece, zeros] if g == 0 else [zeros, piece], axis=0))
            rhss[(sb, g)] = jnp.concatenate(cols, axis=1)
            units.append((sb, g))
        yield

    ss = []
    for sb, g in units:
        s = _dot(k_alls[sb], rhss[(sb, g)])
        ss.append(jnp.concatenate([jnp.where(valids[sb], s[0:n_win], NEG_INF), s[n_win:n_keys]], axis=0))
        yield
    p_bfs, invs = [], []
    for s, (sb, g) in zip(ss, units):
        m = jnp.maximum(jnp.max(s, axis=0, keepdims=True), sink_rows[g])
        p = jnp.exp2(s - m)
        invs.append(1.0 / (jnp.sum(p, axis=0, keepdims=True) + jnp.exp2(sink_rows[g] - m)))
        p_bfs.append(p.astype(BF16))
        yield
    pieces = {sb: [] for sb in range(qb)}
    for (sb, g), p_bf, inv in zip(units, p_bfs, invs):
        vt_g = vt_alls[sb][g * HEAD_DIM:(g + 1) * HEAD_DIM, :]
        for pair in range(ATTN_GROUP // 2):
            lanes = slice(pair * MXU_TILE, (pair + 1) * MXU_TILE)
            ot = _dot(vt_g, p_bf[:, lanes]) * inv[:, lanes]
            pieces[sb] += [ot[:, 0:ATTN_BLOCK], ot[:, ATTN_BLOCK:2 * ATTN_BLOCK]]
        yield
    for sb in range(qb):
        outs = [jnp.concatenate([pieces[sb][2 * j], pieces[sb][2 * j + 1]], axis=0).T
                for j in range(ATTN_Q_HEADS // 2)]
        o_ref[0, sb * ATTN_BLOCK:(sb + 1) * ATTN_BLOCK, :] = jnp.concatenate(outs, axis=1).astype(o_ref.dtype)


def _run_parts(parts, b, name):
    steps = parts[0]["steps"]
    assert all(p["steps"] == steps for p in parts)
    counts = [(len(p["args"]), len(p["out_shape"]), len(p["scratch_shapes"])) for p in parts]
    n_in = sum(c[0] for c in counts)
    n_out = sum(c[1] for c in counts)

    def kern(*refs):
        i_pos, o_pos, s_pos = 0, n_in, n_in + n_out
        bodies = []
        for p, (ci, co, cs) in zip(parts, counts):
            bodies.append(p["kernel"](*refs[i_pos:i_pos + ci], *refs[o_pos:o_pos + co], *refs[s_pos:s_pos + cs]))
            i_pos, o_pos, s_pos = i_pos + ci, o_pos + co, s_pos + cs
        while bodies:
            for body in list(bodies):
                try:
                    next(body)
                except StopIteration:
                    bodies.remove(body)

    outs = pl.pallas_call(
        kern,
        grid=(b, steps),
        in_specs=[s for p in parts for s in p["in_specs"]],
        out_specs=[s for p in parts for s in p["out_specs"]],
        out_shape=[s for p in parts for s in p["out_shape"]],
        scratch_shapes=[s for p in parts for s in p["scratch_shapes"]],
        compiler_params=_cparams(2),
        name=name,
    )(*[a for p in parts for a in p["args"]])
    res, pos = [], 0
    for _, co, _ in counts:
        res.append(outs[pos:pos + co])
        pos += co
    return res


def _attn_parts(a_lat, kv_ctx, cos, sin, qg, kg, sink, havg, qb):
    b, s, wa = a_lat.shape
    ctx = kv_ctx.shape[1]
    nb = s // ATTN_BLOCK
    const2 = lambda bi, n: (0, 0)
    return dict(
        kernel=functools.partial(_attn_kernel, seq=s, ctx=ctx, qb=qb),
        steps=nb // qb,
        in_specs=[pl.BlockSpec((1, s, wa), lambda bi, n: (bi, 0, 0)),
                  pl.BlockSpec((1, ctx, 2 * KV_WIDTH), lambda bi, n: (bi, 0, 0)),
                  pl.BlockSpec((s, LANES), const2),
                  pl.BlockSpec((s, LANES), const2),
                  pl.BlockSpec((1, LANES), const2),
                  pl.BlockSpec((1, LANES), const2),
                  pl.BlockSpec((ATTN_Q_HEADS, LANES), const2),
                  pl.BlockSpec((LANES, LANES), const2)],
        out_specs=[pl.BlockSpec((1, qb * ATTN_BLOCK, ATTN_WIDTH), lambda bi, n: (bi, n, 0))],
        out_shape=[jax.ShapeDtypeStruct((b, s, ATTN_WIDTH), BF16)],
        scratch_shapes=[pltpu.VMEM((ctx + s, KV_WIDTH), BF16),
                        pltpu.VMEM(((ctx + s) // ATTN_BLOCK, KV_WIDTH, ATTN_BLOCK), BF16)],
        args=[a_lat, kv_ctx, cos, sin, qg, kg, sink, havg])


def _bdiag(x_lane, bmask):
    return jnp.where(bmask, jnp.concatenate([x_lane] * HEADS_PER_TILE, axis=0), jnp.zeros((), x_lane.dtype))


def _unit_tri_inverses(a_list, eye_l, sub_mask, bmask):
    ads = [jnp.where(sub_mask, a, 0.0) for a in a_list]
    ys = [jnp.where(sub_mask, 0.0, a) for a in a_list]
    ps = [eye_l - ad for ad in ads]
    pws = [ad.astype(BF16) for ad in ads]
    n_levels = int(np.log2(DN_SUB))
    c1, c2, c3 = CHUNK, 2 * CHUNK, 3 * CHUNK
    for level in range(n_levels):
        rhss = [_bdiag(pw, bmask) for pw in pws]
        if level == 0:
            boths = [_dot(jnp.concatenate([pw, y.astype(BF16)], axis=0), rhs) for pw, y, rhs in zip(pws, ys, rhss)]
            pws = [both[0:c1].astype(BF16) for both in boths]
            ys = [y - both[c1:c2] for y, both in zip(ys, boths)]
        elif level < n_levels - 1:
            boths = [_dot(jnp.concatenate([pw, p.astype(BF16), y.astype(BF16)], axis=0), rhs)
                     for pw, p, y, rhs in zip(pws, ps, ys, rhss)]
            pws = [both[0:c1].astype(BF16) for both in boths]
            ps = [p + both[c1:c2] for p, both in zip(ps, boths)]
            ys = [y + both[c2:c3] for y, both in zip(ys, boths)]
        else:
            boths = [_dot(jnp.concatenate([p.astype(BF16), y.astype(BF16)], axis=0), rhs)
                     for p, y, rhs in zip(ps, ys, rhss)]
            ps = [p + both[0:c1] for p, both in zip(ps, boths)]
            ys = [y + both[c1:c2] for y, both in zip(ys, boths)]
        yield
    assert CHUNK // DN_SUB == 4
    b_bfs = [y.astype(BF16) for y in ys]
    boths = [_dot(jnp.concatenate([b, p.astype(BF16)], axis=0), _bdiag(b, bmask)) for b, p in zip(b_bfs, ps)]
    yield
    zs = [p - both[c1:c2] for p, both in zip(ps, boths)]
    return [z + _dot(z.astype(BF16), _bdiag(both[0:c1].astype(BF16), bmask)) for z, both in zip(zs, boths)]


def _dn_factors(ab, d, arow, dtrow, exp2_ref, tri_ref, eye_t):
    lane = lax.broadcasted_iota(jnp.int32, ab.shape, 1)
    z = ab + dtrow
    softplus = jnp.maximum(z, 0.0) + jnp.log(1.0 + jnp.exp(-jnp.abs(z)))
    is_g = (lane % 32) < 16
    x = jnp.where(is_g, -arow * softplus, _sigmoid(ab))
    def hi_lo(v):
        hi = v.astype(BF16)
        return hi, (v - hi.astype(F32)).astype(BF16)

    x = jnp.where(is_g, _dot(tri_ref[d], jnp.concatenate(hi_lo(x), axis=0)), x)
    x_hi, x_lo = hi_lo(x)
    xhl = jnp.where(lane < 32, x_hi, x_lo)[:, 0:64]
    y = _dot(xhl, exp2_ref[d])
    gi = y[:, 0:DN_WIDTH]
    be = y[:, DN_WIDTH:2 * DN_WIDTH]
    gj = jnp.sum(gi * eye_t, axis=0, keepdims=True)
    return be, gi, jnp.broadcast_to(gj, gi.shape)


def _block_mask():
    return (lax.broadcasted_iota(jnp.int32, (MXU_TILE, MXU_TILE), 0) // HEAD_DIM
            == lax.broadcasted_iota(jnp.int32, (MXU_TILE, MXU_TILE), 1) // HEAD_DIM)


def _dnprep_kernel(*refs, cb, want_o):
    if want_o:
        q_ref, k_ref, v_ref, ab_ref = refs[:4]
        refs = refs[4:]
    else:
        q_ref = None
        k_ref, v_ref, ab_ref = refs[:3]
        refs = refs[3:]
    arow_ref, dtrow_ref, exp2_ref, tri_ref, w_o, uv_o, kt_o, dl_o = refs[:8]
    qd_o, in_o = refs[8:10] if want_o else (None, None)
    n_tiles = DN_WIDTH // MXU_TILE
    row = lax.broadcasted_iota(jnp.int32, (CHUNK, MXU_TILE), 0)
    colj = lax.broadcasted_iota(jnp.int32, (CHUNK, MXU_TILE), 1) % HEAD_DIM
    eye_l = (row == colj).astype(F32)
    eye_t = jnp.concatenate([eye_l] * n_tiles, axis=1)
    bmask = _block_mask()
    arow = arow_ref[...]
    dtrow = dtrow_ref[...]

    fac = {}
    shared = {}
    for c in range(cb):
        rows = slice(c * CHUNK, (c + 1) * CHUNK)
        ab = ab_ref[0, rows, :]
        for d in range(N_DIR):
            fac[(c, d)] = _dn_factors(ab, d, arow, dtrow, exp2_ref, tri_ref, eye_t)
        for g in range(n_tiles):
            lanes = slice(g * MXU_TILE, (g + 1) * MXU_TILE)
            k_l = k_ref[0, rows, lanes]
            kbd = _bdiag(k_l, bmask)
            if want_o:
                kq = _dot_nt(jnp.concatenate([k_l, q_ref[0, rows, lanes]], axis=0), kbd)
                shared[(c, g)] = (kq[0:CHUNK], kq[CHUNK:2 * CHUNK])
            else:
                shared[(c, g)] = (_dot_nt(k_l, kbd), None)
        yield

    units = [(c, g, d) for c in range(cb) for g in range(n_tiles) for d in range(N_DIR)]
    decs = []
    for c, g, d in units:
        lanes = slice(g * MXU_TILE, (g + 1) * MXU_TILE)
        be, gi, gj = fac[(c, d)]
        lower = (row > colj) if d == 0 else (row < colj)
        decs.append(jnp.where(lower, jnp.exp(jnp.where(lower, gi[:, lanes] - gj[:, lanes], 0.0)), 0.0))
    a_list = [fac[(c, d)][0][:, g * MXU_TILE:(g + 1) * MXU_TILE] * dec * shared[(c, g)][0]
              for (c, g, d), dec in zip(units, decs)]
    yield
    tinvs = yield from _unit_tri_inverses(a_list, eye_l, (row // DN_SUB) == (colj // DN_SUB), bmask)

    for idx, ((c, g, d), dec, tinv) in enumerate(zip(units, decs, tinvs)):
        if idx % (n_tiles * N_DIR) == 0:
            yield
        rows = slice(c * CHUNK, (c + 1) * CHUNK)
        lanes = slice(g * MXU_TILE, (g + 1) * MXU_TILE)
        be, gi, _ = fac[(c, d)]
        be, gi = be[:, lanes], gi[:, lanes]
        last = CHUNK - 1 if d == 0 else 0
        e_g = jnp.exp(gi)
        gl_row = gi[last:last + 1, :]
        kf = k_ref[0, rows, lanes].astype(F32)
        vf = v_ref[0, rows, lanes].astype(F32)
        rhs = jnp.concatenate([_bdiag((be * e_g * kf).astype(BF16), bmask), _bdiag((be * vf).astype(BF16), bmask)],
                              axis=1)
        wu = _dot(tinv.astype(BF16), rhs)
        w_o[0, d, rows, lanes] = wu[:, 0:MXU_TILE].astype(BF16)
        uv_o[0, d, rows, lanes] = wu[:, MXU_TILE:2 * MXU_TILE]
        kt_o[0, d, rows, lanes] = (jnp.exp(gl_row - gi) * kf).astype(BF16)
        dl_o[0, d, c, :, lanes] = jnp.exp(gl_row)
        if want_o:
            qd_o[0, d, rows, lanes] = (e_g * q_ref[0, rows, lanes].astype(F32)).astype(BF16)
            in_o[0, d, rows, lanes] = ((dec + eye_l) * shared[(c, g)][1]).astype(BF16)


def _dnprep_parts(q, k, v, ab, arow, dtrow, exp2, tri, cb):
    want_o = q is not None
    b, t, w = k.shape
    tb = cb * CHUNK
    tok = lambda bi, i: (bi, i, 0)
    const2 = lambda bi, i: (0, 0)
    const3 = lambda bi, i: (0, 0, 0)
    dir_tok = lambda bi, i: (bi, 0, i, 0)
    data = ([q] if want_o else []) + [k, v]
    in_specs = ([pl.BlockSpec((1, tb, w), tok)] * len(data) + [pl.BlockSpec((1, tb, LANES), tok),
                pl.BlockSpec((1, LANES), const2), pl.BlockSpec((1, LANES), const2),
                pl.BlockSpec(exp2.shape, const3), pl.BlockSpec(tri.shape, const3)])
    big = lambda dt: jax.ShapeDtypeStruct((b, N_DIR, t, w), dt)
    big_spec = pl.BlockSpec((1, N_DIR, tb, w), dir_tok)
    out_shape = [big(BF16), big(F32), big(BF16), jax.ShapeDtypeStruct((b, N_DIR, t // CHUNK, 1, w), F32)]
    out_specs = [big_spec, big_spec, big_spec, pl.BlockSpec((1, N_DIR, cb, 1, w), lambda bi, i: (bi, 0, i, 0, 0))]
    if want_o:
        out_shape += [big(BF16), big(BF16)]
        out_specs += [big_spec, big_spec]
    return dict(kernel=functools.partial(_dnprep_kernel, cb=cb, want_o=want_o), steps=t // tb, in_specs=in_specs,
                out_specs=out_specs, out_shape=out_shape, scratch_shapes=[],
                args=data + [ab, arow, dtrow, exp2, tri])


def _dnscan_kernel(*refs, n_chunk, bb, want_o, have_s0, want_s):
    n_in = 6 if want_o else 4
    dir_refs = [refs[0:n_in], refs[n_in:2 * n_in]]
    pos = 2 * n_in
    s0_ref = refs[pos] if have_s0 else None
    pos += int(have_s0)
    o_refs = refs[pos:pos + N_DIR] if want_o else None
    pos += N_DIR if want_o else 0
    sout_ref = refs[pos] if want_s else None
    pos += int(want_s)
    s_scr = refs[pos]
    n_tiles = DN_WIDTH // MXU_TILE
    i = pl.program_id(1)

    @pl.when(i == 0)
    def _init():
        if have_s0:
            s_scr[...] = s0_ref[...]
        else:
            s_scr[...] = jnp.zeros_like(s_scr)

    bmask = _block_mask()
    chains = [(bi, d, g) for bi in range(bb) for d in range(N_DIR) for g in range(n_tiles)]

    def body(j, carry):
        cidx = (j, n_chunk - 1 - j)
        r0s = [pl.multiple_of(cidx[d] * CHUNK, CHUNK) for d in range(N_DIR)]
        s_olds, r1s = [], []
        for bi, d, g in chains:
            lanes = slice(g * MXU_TILE, (g + 1) * MXU_TILE)
            w = dir_refs[d][0][bi, 0, pl.ds(r0s[d], CHUNK), lanes]
            if want_o:
                w = jnp.concatenate([w, dir_refs[d][4][bi, 0, pl.ds(r0s[d], CHUNK), lanes]], axis=0)
            s_old = s_scr[bi, d * n_tiles + g]
            s_olds.append(s_old)
            r1s.append(_dot(w, s_old.astype(BF16)))
        u_bfs = []
        for (bi, d, g), r1 in zip(chains, r1s):
            lanes = slice(g * MXU_TILE, (g + 1) * MXU_TILE)
            u_bfs.append((dir_refs[d][1][bi, 0, pl.ds(r0s[d], CHUNK), lanes] - r1[0:CHUNK]).astype(BF16))
        for (bi, d, g), r1, u_bf, s_old in zip(chains, r1s, u_bfs, s_olds):
            lanes = slice(g * MXU_TILE, (g + 1) * MXU_TILE)
            kt = dir_refs[d][2][bi, 0, pl.ds(r0s[d], CHUNK), lanes]
            ds = jnp.where(bmask, _dot_tn(kt, u_bf), 0.0)
            dl = dir_refs[d][3][bi, 0, cidx[d]][:, lanes]
            s_scr[bi, d * n_tiles + g] = s_old * dl + ds
            if want_o:
                intra = dir_refs[d][5][bi, 0, pl.ds(r0s[d], CHUNK), lanes]
                o_refs[d][bi, pl.ds(r0s[d], CHUNK), lanes] = r1[CHUNK:2 * CHUNK] + _dot(intra, _bdiag(u_bf, bmask))
        return carry

    lax.fori_loop(0, n_chunk, body, 0)

    if want_s:
        @pl.when(i == pl.num_programs(1) - 1)
        def _fin():
            sout_ref[...] = s_scr[...]


def _dnscan(prep, s0, tb, bb, want_s):
    want_o = len(prep) == 6
    b, _, t, w = prep[0].shape
    n_t = t // tb
    n_chunk = tb // CHUNK
    n_chain = N_DIR * (w // MXU_TILE)

    def specs(d):
        blk = (lambda bi, i: i) if d == 0 else (lambda bi, i: n_t - 1 - i)
        big = pl.BlockSpec((bb, 1, tb, w), lambda bi, i: (bi, d, blk(bi, i), 0))
        dl = pl.BlockSpec((bb, 1, n_chunk, 1, w), lambda bi, i: (bi, d, blk(bi, i), 0, 0))
        return [big, big, big, dl] + ([big, big] if want_o else [])

    in_specs = specs(0) + specs(1)
    args = list(prep) + list(prep)
    state_spec = pl.BlockSpec((bb, n_chain, MXU_TILE, MXU_TILE), lambda bi, i: (bi, 0, 0, 0))
    if s0 is not None:
        in_specs.append(state_spec)
        args.append(s0)
    out_shape, out_specs = [], []
    if want_o:
        out_shape += [jax.ShapeDtypeStruct((b, t, w), F32)] * N_DIR
        out_specs += [pl.BlockSpec((bb, tb, w), lambda bi, i: (bi, i, 0)),
                      pl.BlockSpec((bb, tb, w), lambda bi, i: (bi, n_t - 1 - i, 0))]
    if want_s:
        out_shape.append(jax.ShapeDtypeStruct((b, n_chain, MXU_TILE, MXU_TILE), F32))
        out_specs.append(state_spec)
    return pl.pallas_call(
        functools.partial(_dnscan_kernel, n_chunk=n_chunk, bb=bb, want_o=want_o, have_s0=s0 is not None,
                          want_s=want_s),
        grid=(b // bb, n_t),
        in_specs=in_specs,
        out_specs=out_specs,
        out_shape=out_shape,
        scratch_shapes=[pltpu.VMEM((bb, n_chain, MXU_TILE, MXU_TILE), F32)],
        compiler_params=_cparams(2),
        name="dnscan",
    )(*args)


def _tail_kernel(x_ref, mod_ref, ya_ref, odf_ref, odb_ref, z_ref, gate_ref, dng_ref, havg_ref, wba_ref, wbd_ref,
                 wo_ref, gn2_ref, w1_ref, w2_ref, o_ref, *, ff_chunk):
    havg = havg_ref[...]
    dng = dng_ref[...]
    yd_parts = []
    for j in range(DN_WIDTH // MXU_TILE):
        sl = slice(j * MXU_TILE, (j + 1) * MXU_TILE)
        od = odf_ref[0, :, sl] + odb_ref[0, :, sl]
        ms = _dot((od * od).astype(BF16), havg)
        z = z_ref[0, :, sl].astype(F32)
        yd_parts.append((od * lax.rsqrt(ms + EPS) * dng * (z * _sigmoid(z))).astype(BF16))
    yd = jnp.concatenate(yd_parts, axis=1)
    ga = gate_ref[0, :, 0:D_MODEL].astype(F32)
    gd = gate_ref[0, :, D_MODEL:2 * D_MODEL].astype(F32)
    y = _sigmoid(ga) * _dot(ya_ref[0], wba_ref[...]) + _sigmoid(gd) * _dot(yd, wbd_ref[...])
    br = _dot(y.astype(BF16), wo_ref[...])
    mod = mod_ref[0]
    out1 = x_ref[0] + mod[2:3] * br
    ms2 = jnp.mean(out1 * out1, axis=-1, keepdims=True)
    hm = (out1 * lax.rsqrt(ms2 + EPS) * (gn2_ref[...] * (1.0 + mod[4:5])) + mod[3:4]).astype(BF16)
    acc = None
    for j in range(D_FF // ff_chunk):
        a = jnp.maximum(_dot(hm, w1_ref[:, j * ff_chunk:(j + 1) * ff_chunk]), 0.0)
        part = _dot((a * a).astype(BF16), w2_ref[j * ff_chunk:(j + 1) * ff_chunk, :])
        acc = part if acc is None else acc + part
    o_ref[0] = out1 + mod[5:6] * acc


def _resident(shape):
    return pl.BlockSpec(shape, lambda bi, i: (0,) * len(shape), pipeline_mode=pl.Buffered(1))


def _tail(x, mod3, y_attn, o_df, o_db, z, gates, dng, havg, wba, wbd, wo, gn2, w1, w2, tm):
    b, t, d = x.shape
    tok = lambda bi, i: (bi, i, 0)
    return pl.pallas_call(
        functools.partial(_tail_kernel, ff_chunk=1024),
        grid=(b, t // tm),
        in_specs=[pl.BlockSpec((1, tm, d), tok),
                  pl.BlockSpec((1, 6, d), lambda bi, i: (bi, 0, 0)),
                  pl.BlockSpec((1, tm, ATTN_WIDTH), tok),
                  pl.BlockSpec((1, tm, DN_WIDTH), tok),
                  pl.BlockSpec((1, tm, DN_WIDTH), tok),
                  pl.BlockSpec((1, tm, DN_WIDTH), tok),
                  pl.BlockSpec((1, tm, 2 * d), tok),
                  _resident((1, MXU_TILE)),
                  _resident((MXU_TILE, MXU_TILE)),
                  _resident(wba.shape),
                  _resident(wbd.shape),
                  _resident(wo.shape),
                  _resident((1, d)),
                  _resident(w1.shape),
                  _resident(w2.shape)],
        out_specs=pl.BlockSpec((1, tm, d), tok),
        out_shape=jax.ShapeDtypeStruct((b, t, d), F32),
        compiler_params=_cparams(2),
        name="tail",
    )(x, mod3, y_attn, o_df, o_db, z, gates, dng, havg, wba, wbd, wo, gn2, w1, w2)


def _head_avg(n, scale):
    idx = np.arange(n) // HEAD_DIM
    return jnp.asarray((idx[:, None] == idx[None, :]).astype(np.float32) * scale, BF16)


def _dn_expand_matrix():
    n = N_DIR * DN_HEADS
    m = np.zeros((N_DIR, 4 * n, 2 * DN_WIDTH), np.float32)
    for d in range(N_DIR):
        for part in range(2):
            for h in range(DN_HEADS):
                idx = d * DN_HEADS + h
                m[d, part * 2 * n + idx, h * HEAD_DIM:(h + 1) * HEAD_DIM] = 1.0
                m[d, part * 2 * n + n + idx, DN_WIDTH + h * HEAD_DIM:DN_WIDTH + (h + 1) * HEAD_DIM] = 1.0
    return jnp.asarray(m, BF16)


def _tri_matrices():
    i = np.arange(CHUNK)
    low = (i[:, None] >= i[None, :]).astype(np.float32)
    up = (i[:, None] <= i[None, :]).astype(np.float32)
    return jnp.asarray(np.stack([np.concatenate([low, low], axis=1), np.concatenate([up, up], axis=1)]), BF16)


def _rope_tables(seq):
    half = HEAD_DIM // 2
    n_freq = half // 2
    freqs = ROPE_BASE ** (-jnp.arange(n_freq, dtype=F32) / n_freq)
    pos = jnp.arange(seq)
    ang_r = (pos // GRID_W).astype(F32)[:, None] * freqs
    ang_c = (pos % GRID_W).astype(F32)[:, None] * freqs
    cos = jnp.concatenate([jnp.cos(ang_r)] * 2 + [jnp.cos(ang_c)] * 2, axis=1)
    sin = jnp.concatenate([-jnp.sin(ang_r), jnp.sin(ang_r), -jnp.sin(ang_c), jnp.sin(ang_c)], axis=1)
    reps = LANES // HEAD_DIM
    return jnp.tile(cos, (1, reps)), jnp.tile(sin, (1, reps))


def _pad_cols(w, n):
    return jnp.pad(w, ((0, 0), (0, n - w.shape[1])))


def kernel(x, c, ctx, c_ctx, w_ada, b_ada, g_norm1, w_in, q_norm_g, k_norm_g, attn_sink, conv_w, a_log, dt_bias,
           dn_norm_g, w_br_attn, w_br_dn, w_out, g_norm2, w_mlp1, w_mlp2):
    depth = w_ada.shape[0]
    assert depth == 1, "single-layer trunk only"
    b, s, d = x.shape
    n_ctx = ctx.shape[1]
    assert d == D_MODEL and w_in.shape[-1] == _IN_WIDTH
    assert s >= 3 * ATTN_BLOCK and s % ATTN_BLOCK == 0 and s % CHUNK == 0 and n_ctx % CHUNK == 0
    out_dtype = x.dtype
    w_in0 = w_in[0]

    mod_rows = 16
    cc = jnp.concatenate([c.astype(F32), c_ctx.astype(F32)[None], jnp.zeros((mod_rows - b - 1, d), F32)], axis=0)
    mod = _ada(cc, w_ada[0], b_ada[0])
    mod3 = mod.reshape(mod_rows, 6, d)

    ab_cols = jnp.concatenate([w_in0[:, _OFF_DA:_OFF_GA]] * 2, axis=1)
    w_lat = jnp.concatenate([w_in0[:, :_OFF_DA], w_in0[:, _OFF_GA:], _pad_cols(ab_cols, LANES)], axis=1).astype(BF16)
    hsum = _head_avg(MXU_TILE, 1.0)
    segs_lat = ((0, _OFF_DQ, None),
                (_OFF_DQ, DN_WIDTH, (0, True, True)), (_OFF_DK, DN_WIDTH, (DN_WIDTH, True, False)),
                (_OFF_DV, DN_WIDTH, (2 * DN_WIDTH, False, False)),
                (_OFF_DZ, DN_WIDTH, None), (_OFF_DA, 2 * D_MODEL, None), (_OFF_DA + 2 * D_MODEL, LANES, None))
    a_lat, q_d, k_d, v_d, z_lat, gates, ab_lat = _inproj(x, mod3, None, g_norm1[0], w_lat, conv_w[0], hsum, segs_lat,
                                                         (BF16, BF16, BF16, BF16, BF16, BF16, F32), tm=512)
    w_ctx = jnp.concatenate([w_in0[:, _OFF_AK:_OFF_DQ], w_in0[:, _OFF_DK:_OFF_DZ], _pad_cols(ab_cols, LANES)],
                            axis=1).astype(BF16)
    segs_ctx = ((0, 2 * KV_WIDTH, None),
                (2 * KV_WIDTH, DN_WIDTH, (0, True, False)), (2 * KV_WIDTH + DN_WIDTH, DN_WIDTH, (DN_WIDTH, False, False)),
                (2 * KV_WIDTH + 2 * DN_WIDTH, LANES, None))
    kv_ctx, k_dc, v_dc, ab_ctx = _inproj(ctx, mod3, b, g_norm1[0], w_ctx, conv_w[0][:, DN_WIDTH:], hsum, segs_ctx,
                                         (BF16, BF16, BF16, F32), tm=n_ctx)

    cos, sin = _rope_tables(s)
    reps = LANES // HEAD_DIM
    n_gate = N_DIR * DN_HEADS
    arow = _pad_cols(jnp.tile(jnp.concatenate([jnp.exp(a_log[0]).reshape(1, n_gate), jnp.zeros((1, n_gate), F32)],
                                              axis=1), (1, 2)), LANES)
    dtrow = _pad_cols(jnp.tile(jnp.concatenate([dt_bias[0].reshape(1, n_gate), jnp.zeros((1, n_gate), F32)],
                                               axis=1), (1, 2)), LANES)
    exp2, tri = _dn_expand_matrix(), _tri_matrices()
    dn_cb = 4
    qb = (s // ATTN_BLOCK) // (s // (dn_cb * CHUNK))
    attn_parts = _attn_parts(a_lat, kv_ctx, cos, sin,
                             jnp.tile(q_norm_g[0].astype(F32), reps)[None],
                             jnp.tile(k_norm_g[0].astype(F32), reps)[None],
                             jnp.broadcast_to(attn_sink[0].astype(F32)[:, None], (ATTN_Q_HEADS, LANES)),
                             _head_avg(LANES, 1.0 / HEAD_DIM), qb)
    (y_attn,), prep_lat = _run_parts(
        [attn_parts, _dnprep_parts(q_d, k_d, v_d, ab_lat, arow, dtrow, exp2, tri, dn_cb)], b, "attn_dnprep")
    (prep_ctx,) = _run_parts([_dnprep_parts(None, k_dc, v_dc, ab_ctx, arow, dtrow, exp2, tri, dn_cb)], b, "dnprep")

    bb = max(q for q in (4, 2, 1) if b % q == 0)
    (s_ctx,) = _dnscan(prep_ctx, None, tb=n_ctx, bb=bb, want_s=True)
    o_df, o_db = _dnscan(prep_lat, s_ctx, tb=256, bb=bb, want_s=False)

    out = _tail(x, mod3, y_attn, o_df, o_db, z_lat, gates,
                jnp.tile(dn_norm_g[0].astype(F32), HEADS_PER_TILE)[None], _head_avg(MXU_TILE, 1.0 / HEAD_DIM),
                w_br_attn[0].astype(BF16), w_br_dn[0].astype(BF16), w_out[0].astype(BF16),
                g_norm2[0].reshape(1, d), w_mlp1[0].astype(BF16), w_mlp2[0].astype(BF16), tm=512)
    return out.astype(out_dtype)
```

```python
import functools

import numpy as np
import jax
import jax.numpy as jnp
from jax import lax
from jax.experimental import pallas as pl
from jax.experimental.pallas import tpu as pltpu

F32 = jnp.float32
BF16 = jnp.bfloat16

D_MODEL = 1024
GRID_W = 64
HEAD_DIM = 64
ATTN_Q_HEADS = 8
ATTN_KV_HEADS = 2
ATTN_GROUP = ATTN_Q_HEADS // ATTN_KV_HEADS
WINDOW = 128
ATTN_BLOCK = 128
ROPE_BASE = 10000.0
DN_HEADS = 8
CONV_W = 3
CHUNK = 64
N_DIR = 2
D_FF = 4 * D_MODEL
EPS = 1e-6
NEG_INF = -1e30
LOG2_E = float(np.log2(np.e))

ATTN_WIDTH = ATTN_Q_HEADS * HEAD_DIM
KV_WIDTH = ATTN_KV_HEADS * HEAD_DIM
DN_WIDTH = DN_HEADS * HEAD_DIM
LANES = 128
MXU_TILE = 256
HEADS_PER_TILE = MXU_TILE // HEAD_DIM
DN_SUB = 16
VMEM_LIMIT = 56 * 1024 * 1024

_OFF_AQ = 0
_OFF_AK = _OFF_AQ + ATTN_WIDTH
_OFF_AV = _OFF_AK + KV_WIDTH
_OFF_DQ = _OFF_AV + KV_WIDTH
_OFF_DK = _OFF_DQ + DN_WIDTH
_OFF_DV = _OFF_DK + DN_WIDTH
_OFF_DZ = _OFF_DV + DN_WIDTH
_OFF_DA = _OFF_DZ + DN_WIDTH
_OFF_DB = _OFF_DA + N_DIR * DN_HEADS
_OFF_GA = _OFF_DB + N_DIR * DN_HEADS
_OFF_GD = _OFF_GA + D_MODEL
_IN_WIDTH = _OFF_GD + D_MODEL


def _sigmoid(x):
    return 0.5 * jnp.tanh(0.5 * x) + 0.5


def _dot(a, b):
    return jnp.dot(a, b, preferred_element_type=F32)


def _dot_nt(a, b):
    return lax.dot_general(a, b, (((1,), (1,)), ((), ())), preferred_element_type=F32)


def _dot_tn(a, b):
    return lax.dot_general(a, b, (((0,), (0,)), ((), ())), preferred_element_type=F32)


def _cparams(n_axes):
    return pltpu.CompilerParams(dimension_semantics=("arbitrary",) * n_axes, vmem_limit_bytes=VMEM_LIMIT)


def _ada_kernel(c_ref, w_ref, b_ref, o_ref):
    c = c_ref[...]
    s = c * _sigmoid(c)
    o_ref[...] = _dot(s.astype(BF16), w_ref[...].astype(BF16)) + b_ref[...]


def _ada(cc, w_ada, b_ada):
    rows, d = cc.shape
    n = w_ada.shape[1]
    tn = 1536
    return pl.pallas_call(
        _ada_kernel,
        grid=(n // tn,),
        in_specs=[pl.BlockSpec((rows, d), lambda j: (0, 0)),
                  pl.BlockSpec((d, tn), lambda j: (0, j)),
                  pl.BlockSpec((1, tn), lambda j: (0, j))],
        out_specs=pl.BlockSpec((rows, tn), lambda j: (0, j)),
        out_shape=jax.ShapeDtypeStruct((rows, n), F32),
        compiler_params=_cparams(1),
        name="ada",
    )(cc, w_ada, b_ada.reshape(1, n))


def _inproj_kernel(x_ref, xprev_ref, xnext_ref, mod_ref, g_ref, w_ref, cw_ref, hsum_ref, *out_refs, segs, tm, halo):
    i = pl.program_id(1)
    last = pl.num_programs(1) - 1
    mod = mod_ref[0]
    scale = g_ref[...] * (1.0 + mod[1:2])

    def norm_mod(v):
        ms = jnp.mean(v * v, axis=-1, keepdims=True)
        return (v * lax.rsqrt(ms + EPS) * scale + mod[0:1]).astype(BF16)

    h = norm_mod(x_ref[0])
    h_halo = norm_mod(jnp.concatenate([xprev_ref[0], xnext_ref[0]], axis=0))
    keep_prev = (i > 0).astype(F32)
    keep_next = (i < last).astype(F32)
    rows = lax.broadcasted_iota(jnp.int32, (tm, MXU_TILE), 0)
    hsum = hsum_ref[...]
    def plain_piece(o_ref, start, lo, width):
        o_ref[0, :, lo:lo + width] = _dot(h, w_ref[:, start + lo:start + lo + width]).astype(o_ref.dtype)

    def conv_piece(o_ref, start, lo, kind):
        conv_col, do_norm, is_q = kind
        w_cols = w_ref[:, start + lo:start + lo + MXU_TILE]
        p = _dot(h, w_cols)
        p_halo = _dot(h_halo, w_cols)
        yield
        p_prev = jnp.where(rows == 0, p_halo[halo - 1:halo] * keep_prev, pltpu.roll(p, 1, 0))
        p_next = jnp.where(rows == tm - 1, p_halo[halo:halo + 1] * keep_next, pltpu.roll(p, tm - 1, 0))
        cw = cw_ref[:, conv_col + lo:conv_col + lo + MXU_TILE]
        y = p_prev * cw[0:1] + p * cw[1:2] + p_next * cw[2:3]
        y = y * _sigmoid(y)
        if do_norm:
            y = y * lax.rsqrt(_dot((y * y).astype(BF16), hsum) + EPS)
            if is_q:
                y = y * (HEAD_DIM ** -0.5)
        o_ref[0, :, lo:lo + MXU_TILE] = y.astype(o_ref.dtype)

    plain, conv = [], []
    for o_ref, (start, size, kind) in zip(out_refs, segs):
        if kind is None:
            plain += [functools.partial(plain_piece, o_ref, start, lo, min(MXU_TILE, size - lo))
                      for lo in range(0, size, MXU_TILE)]
        else:
            conv += [functools.partial(conv_piece, o_ref, start, lo, kind) for lo in range(0, size, MXU_TILE)]
    pending = None
    while plain or conv or pending is not None:
        started = conv.pop(0)() if conv else None
        if started is not None:
            next(started)
        if pending is not None:
            for _ in pending:
                pass
        pending = started
        for _ in range(-(-len(plain) // (len(conv) + 1)) if plain else 0):
            plain.pop(0)()


def _inproj(x, mod3, mod_row, g_norm, w, conv_w, hsum, segs, dtypes, tm):
    b, t, d = x.shape
    n = w.shape[1]
    halo = 8
    r = tm // halo
    nblk = t // halo
    if mod_row is None:
        mod_map = lambda bi, i: (bi, 0, 0)
    else:
        mod_map = lambda bi, i: (mod_row, 0, 0)
    const2 = lambda bi, i: (0, 0)
    out_shape = [jax.ShapeDtypeStruct((b, t, size), dt) for (_, size, _), dt in zip(segs, dtypes)]
    out_specs = [pl.BlockSpec((1, tm, size), lambda bi, i: (bi, i, 0)) for (_, size, _) in segs]
    return pl.pallas_call(
        functools.partial(_inproj_kernel, segs=segs, tm=tm, halo=halo),
        grid=(b, t // tm),
        in_specs=[pl.BlockSpec((1, tm, d), lambda bi, i: (bi, i, 0)),
                  pl.BlockSpec((1, halo, d), lambda bi, i: (bi, jnp.maximum(i * r - 1, 0), 0)),
                  pl.BlockSpec((1, halo, d), lambda bi, i: (bi, jnp.minimum((i + 1) * r, nblk - 1), 0)),
                  pl.BlockSpec((1, 6, d), mod_map),
                  pl.BlockSpec((1, d), const2),
                  pl.BlockSpec((d, n), const2),
                  pl.BlockSpec(conv_w.shape, const2),
                  pl.BlockSpec(hsum.shape, const2)],
        out_specs=out_specs,
        out_shape=out_shape,
        compiler_params=_cparams(2),
        name="inproj",
    )(x, x, x, mod3, g_norm.reshape(1, d), w, conv_w, hsum)


def _rope(x, cos, sin, lane):
    swapped = jnp.where((lane % 32) < 16, pltpu.roll(x, LANES - 16, 1), pltpu.roll(x, 16, 1))
    return x * cos + swapped * sin


def _attn_kernel(a_ref, kvc_ref, cos_ref, sin_ref, qg_ref, kg_ref, sink_ref, havg_ref,
                 o_ref, k_s, vt_s, *, seq, ctx, qb):
    n = pl.program_id(1)
    havg = havg_ref[...]
    prep_rows = 256
    ctx_blocks = ctx // ATTN_BLOCK

    @pl.when(n == 0)
    def _prep():
        kg = kg_ref[...]
        kc = kvc_ref[0, :, 0:KV_WIDTH].astype(F32)
        ms = _dot((kc * kc).astype(BF16), havg)
        k_s[0:ctx, :] = (kc * lax.rsqrt(ms + EPS) * kg).astype(BF16)
        vc = kvc_ref[0, :, KV_WIDTH:2 * KV_WIDTH].astype(F32)
        for j in range(ctx_blocks):
            vt_s[j] = vc[j * ATTN_BLOCK:(j + 1) * ATTN_BLOCK, :].T.astype(BF16)
        lane = lax.broadcasted_iota(jnp.int32, (prep_rows, LANES), 1)
        for r0 in range(0, seq, prep_rows):
            kx = a_ref[0, r0:r0 + prep_rows, _OFF_AK:_OFF_AK + KV_WIDTH].astype(F32)
            ms = _dot((kx * kx).astype(BF16), havg)
            kx = kx * lax.rsqrt(ms + EPS) * kg
            kx = _rope(kx, cos_ref[r0:r0 + prep_rows, :], sin_ref[r0:r0 + prep_rows, :], lane)
            k_s[ctx + r0:ctx + r0 + prep_rows, :] = kx.astype(BF16)
            vx = a_ref[0, r0:r0 + prep_rows, _OFF_AV:_OFF_AV + KV_WIDTH].astype(F32)
            for j in range(prep_rows // ATTN_BLOCK):
                vt_s[ctx_blocks + r0 // ATTN_BLOCK + j] = vx[j * ATTN_BLOCK:(j + 1) * ATTN_BLOCK, :].T.astype(BF16)

    lane = lax.broadcasted_iota(jnp.int32, (ATTN_BLOCK, LANES), 1)
    qg = qg_ref[...]
    n_band = 3
    n_win = n_band * ATTN_BLOCK
    n_keys = n_win + ctx
    key_row = lax.broadcasted_iota(jnp.int32, (n_win, ATTN_BLOCK), 0)
    q_lane = lax.broadcasted_iota(jnp.int32, (n_win, ATTN_BLOCK), 1)
    zeros = jnp.zeros((HEAD_DIM, ATTN_BLOCK), BF16)
    groups = range(ATTN_KV_HEADS)
    sink_rows = [jnp.concatenate([sink_ref[g * ATTN_GROUP + r:g * ATTN_GROUP + r + 1, :] for r in range(ATTN_GROUP)],
                                 axis=1) * LOG2_E for g in groups]

    units = []
    k_alls, vt_alls, valids, rhss = [], [], [], {}
    for sb in range(qb):
        nq = n * qb + sb
        q0 = pl.multiple_of(nq * ATTN_BLOCK, ATTN_BLOCK)
        cos = cos_ref[pl.ds(q0, ATTN_BLOCK), :]
        sin = sin_ref[pl.ds(q0, ATTN_BLOCK), :]
        qts = []
        for j in range(ATTN_WIDTH // LANES):
            qx = a_ref[0, pl.ds(q0, ATTN_BLOCK), j * LANES:(j + 1) * LANES].astype(F32)
            ms = _dot((qx * qx).astype(BF16), havg)
            qx = _rope(qx * lax.rsqrt(ms + EPS) * qg, cos, sin, lane) * (HEAD_DIM ** -0.5 * LOG2_E)
            qts.append(qx.T.astype(BF16))
        blk0 = jnp.clip(nq - 1, 0, seq // ATTN_BLOCK - n_band)
        start = pl.multiple_of(blk0 * ATTN_BLOCK, ATTN_BLOCK)
        valid = jnp.abs(q0 + q_lane - (start + key_row)) <= WINDOW
        valids.append(jnp.concatenate([valid] * ATTN_GROUP, axis=1))
        k_alls.append(jnp.concatenate([k_s[pl.ds(ctx + start, n_win), :], k_s[0:ctx, :]], axis=0))
        vt_loc = vt_s[pl.ds(ctx_blocks + blk0, n_band)]
        vt_alls.append(jnp.concatenate([vt_loc[j] for j in range(n_band)] + [vt_s[j] for j in range(ctx_blocks)],
                                       axis=1))
        for g in groups:
            cols = []
            for r in range(ATTN_GROUP):
                h = g * ATTN_GROUP + r
                piece = qts[h // 2][(h % 2) * HEAD_DIM:(h % 2 + 1) * HEAD_DIM, :]
                cols.append(jnp.concatenate([piece, zeros] if g == 0 else [zeros, piece], axis=0))
            rhss[(sb, g)] = jnp.concatenate(cols, axis=1)
            units.append((sb, g))
        yield

    ss = []
    for sb, g in units:
        s = _dot(k_alls[sb], rhss[(sb, g)])
        ss.append(jnp.concatenate([jnp.where(valids[sb], s[0:n_win], NEG_INF), s[n_win:n_keys]], axis=0))
        yield
    p_bfs, invs = [], []
    for s, (sb, g) in zip(ss, units):
        m = jnp.maximum(jnp.max(s, axis=0, keepdims=True), sink_rows[g])
        p = jnp.exp2(s - m)
        invs.append(1.0 / (jnp.sum(p, axis=0, keepdims=True) + jnp.exp2(sink_rows[g] - m)))
        p_bfs.append(p.astype(BF16))
        yield
    pieces = {sb: [] for sb in range(qb)}
    for (sb, g), p_bf, inv in zip(units, p_bfs, invs):
        vt_g = vt_alls[sb][g * HEAD_DIM:(g + 1) * HEAD_DIM, :]
        for pair in range(ATTN_GROUP // 2):
            lanes = slice(pair * MXU_TILE, (pair + 1) * MXU_TILE)
            ot = _dot(vt_g, p_bf[:, lanes]) * inv[:, lanes]
            pieces[sb] += [ot[:, 0:ATTN_BLOCK], ot[:, ATTN_BLOCK:2 * ATTN_BLOCK]]
        yield
    for sb in range(qb):
        outs = [jnp.concatenate([pieces[sb][2 * j], pieces[sb][2 * j + 1]], axis=0).T
                for j in range(ATTN_Q_HEADS // 2)]
        o_ref[0, sb * ATTN_BLOCK:(sb + 1) * ATTN_BLOCK, :] = jnp.concatenate(outs, axis=1).astype(o_ref.dtype)


def _run_parts(parts, b, name):
    steps = parts[0]["steps"]
    assert all(p["steps"] == steps for p in parts)
    counts = [(len(p["args"]), len(p["out_shape"]), len(p["scratch_shapes"])) for p in parts]
    n_in = sum(c[0] for c in counts)
    n_out = sum(c[1] for c in counts)

    def kern(*refs):
        i_pos, o_pos, s_pos = 0, n_in, n_in + n_out
        bodies = []
        for p, (ci, co, cs) in zip(parts, counts):
            bodies.append(p["kernel"](*refs[i_pos:i_pos + ci], *refs[o_pos:o_pos + co], *refs[s_pos:s_pos + cs]))
            i_pos, o_pos, s_pos = i_pos + ci, o_pos + co, s_pos + cs
        while bodies:
            for body in list(bodies):
                try:
                    next(body)
                except StopIteration:
                    bodies.remove(body)

    outs = pl.pallas_call(
        kern,
        grid=(b, steps),
        in_specs=[s for p in parts for s in p["in_specs"]],
        out_specs=[s for p in parts for s in p["out_specs"]],
        out_shape=[s for p in parts for s in p["out_shape"]],
        scratch_shapes=[s for p in parts for s in p["scratch_shapes"]],
        compiler_params=_cparams(2),
        name=name,
    )(*[a for p in parts for a in p["args"]])
    res, pos = [], 0
    for _, co, _ in counts:
        res.append(outs[pos:pos + co])
        pos += co
    return res


def _attn_parts(a_lat, kv_ctx, cos, sin, qg, kg, sink, havg, qb):
    b, s, wa = a_lat.shape
    ctx = kv_ctx.shape[1]
    nb = s // ATTN_BLOCK
    const2 = lambda bi, n: (0, 0)
    return dict(
        kernel=functools.partial(_attn_kernel, seq=s, ctx=ctx, qb=qb),
        steps=nb // qb,
        in_specs=[pl.BlockSpec((1, s, wa), lambda bi, n: (bi, 0, 0)),
                  pl.BlockSpec((1, ctx, 2 * KV_WIDTH), lambda bi, n: (bi, 0, 0)),
                  pl.BlockSpec((s, LANES), const2),
                  pl.BlockSpec((s, LANES), const2),
                  pl.BlockSpec((1, LANES), const2),
                  pl.BlockSpec((1, LANES), const2),
                  pl.BlockSpec((ATTN_Q_HEADS, LANES), const2),
                  pl.BlockSpec((LANES, LANES), const2)],
        out_specs=[pl.BlockSpec((1, qb * ATTN_BLOCK, ATTN_WIDTH), lambda bi, n: (bi, n, 0))],
        out_shape=[jax.ShapeDtypeStruct((b, s, ATTN_WIDTH), BF16)],
        scratch_shapes=[pltpu.VMEM((ctx + s, KV_WIDTH), BF16),
                        pltpu.VMEM(((ctx + s) // ATTN_BLOCK, KV_WIDTH, ATTN_BLOCK), BF16)],
        args=[a_lat, kv_ctx, cos, sin, qg, kg, sink, havg])


def _half_mask():
    return lax.broadcasted_iota(jnp.int32, (CHUNK, LANES), 1) < HEAD_DIM


def _bdiag(x_lane, lo_half):
    zeros = jnp.zeros((CHUNK, LANES), x_lane.dtype)
    blocks = []
    for h in range(HEADS_PER_TILE):
        col = (h * HEAD_DIM) // LANES
        piece = x_lane[:, col * LANES:(col + 1) * LANES]
        piece = jnp.where(lo_half if (h * HEAD_DIM) % LANES == 0 else jnp.logical_not(lo_half), piece, zeros)
        blocks.append(jnp.concatenate([piece if c == col else zeros for c in range(MXU_TILE // LANES)], axis=1))
    return jnp.concatenate(blocks, axis=0)


def _unit_tri_inverses(a_list, eye_l, sub_mask, lo_half):
    ads = [jnp.where(sub_mask, a, 0.0) for a in a_list]
    ys = [jnp.where(sub_mask, 0.0, a) for a in a_list]
    ps = [eye_l - ad for ad in ads]
    pws = [ad.astype(BF16) for ad in ads]
    n_levels = int(np.log2(DN_SUB))
    c1, c2, c3 = CHUNK, 2 * CHUNK, 3 * CHUNK
    for level in range(n_levels):
        rhss = [_bdiag(pw, lo_half) for pw in pws]
        if level == 0:
            boths = [_dot(jnp.concatenate([pw, y.astype(BF16)], axis=0), rhs) for pw, y, rhs in zip(pws, ys, rhss)]
            pws = [both[0:c1].astype(BF16) for both in boths]
            ys = [y - both[c1:c2] for y, both in zip(ys, boths)]
        elif level < n_levels - 1:
            boths = [_dot(jnp.concatenate([pw, p.astype(BF16), y.astype(BF16)], axis=0), rhs)
                     for pw, p, y, rhs in zip(pws, ps, ys, rhss)]
            pws = [both[0:c1].astype(BF16) for both in boths]
            ps = [p + both[c1:c2] for p, both in zip(ps, boths)]
            ys = [y + both[c2:c3] for y, both in zip(ys, boths)]
        else:
            boths = [_dot(jnp.concatenate([p.astype(BF16), y.astype(BF16)], axis=0), rhs)
                     for p, y, rhs in zip(ps, ys, rhss)]
            ps = [p + both[0:c1] for p, both in zip(ps, boths)]
            ys = [y + both[c1:c2] for y, both in zip(ys, boths)]
        yield
    assert CHUNK // DN_SUB == 4
    b_bfs = [y.astype(BF16) for y in ys]
    boths = [_dot(jnp.concatenate([b, p.astype(BF16)], axis=0), _bdiag(b, lo_half)) for b, p in zip(b_bfs, ps)]
    yield
    zs = [p - both[c1:c2] for p, both in zip(ps, boths)]
    return [z + _dot(z.astype(BF16), _bdiag(both[0:c1].astype(BF16), lo_half)) for z, both in zip(zs, boths)]


def _dn_factors(ab, d, arow, dtrow, exp2_ref, tri_ref, eye_t):
    lane = lax.broadcasted_iota(jnp.int32, ab.shape, 1)
    z = ab + dtrow
    softplus = jnp.maximum(z, 0.0) + jnp.log(1.0 + jnp.exp(-jnp.abs(z)))
    is_g = (lane % 32) < 16
    x = jnp.where(is_g, -arow * softplus, _sigmoid(ab))
    def hi_lo(v):
        hi = v.astype(BF16)
        return hi, (v - hi.astype(F32)).astype(BF16)

    x = jnp.where(is_g, _dot(tri_ref[d], jnp.concatenate(hi_lo(x), axis=0)), x)
    x_hi, x_lo = hi_lo(x)
    xhl = jnp.where(lane < 32, x_hi, x_lo)[:, 0:64]
    y = _dot(xhl, exp2_ref[d])
    gi = y[:, 0:DN_WIDTH]
    be = y[:, DN_WIDTH:2 * DN_WIDTH]
    gj = jnp.sum(gi * eye_t, axis=0, keepdims=True)
    return be, gi, jnp.broadcast_to(gj, gi.shape)


def _block_mask():
    return (lax.broadcasted_iota(jnp.int32, (MXU_TILE, MXU_TILE), 0) // HEAD_DIM
            == lax.broadcasted_iota(jnp.int32, (MXU_TILE, MXU_TILE), 1) // HEAD_DIM)


def _dnprep_kernel(*refs, cb, want_o):
    if want_o:
        q_ref, k_ref, v_ref, ab_ref = refs[:4]
        refs = refs[4:]
    else:
        q_ref = None
        k_ref, v_ref, ab_ref = refs[:3]
        refs = refs[3:]
    arow_ref, dtrow_ref, exp2_ref, tri_ref, w_o, uv_o, kt_o, dl_o = refs[:8]
    qd_o, in_o = refs[8:10] if want_o else (None, None)
    n_tiles = DN_WIDTH // MXU_TILE
    row = lax.broadcasted_iota(jnp.int32, (CHUNK, MXU_TILE), 0)
    colj = lax.broadcasted_iota(jnp.int32, (CHUNK, MXU_TILE), 1) % HEAD_DIM
    eye_l = (row == colj).astype(F32)
    eye_t = jnp.concatenate([eye_l] * n_tiles, axis=1)
    lo_half = _half_mask()
    arow = arow_ref[...]
    dtrow = dtrow_ref[...]

    fac = {}
    shared = {}
    for c in range(cb):
        rows = slice(c * CHUNK, (c + 1) * CHUNK)
        ab = ab_ref[0, rows, :]
        for d in range(N_DIR):
            fac[(c, d)] = _dn_factors(ab, d, arow, dtrow, exp2_ref, tri_ref, eye_t)
        for g in range(n_tiles):
            lanes = slice(g * MXU_TILE, (g + 1) * MXU_TILE)
            k_l = k_ref[0, rows, lanes]
            kbd = _bdiag(k_l, lo_half)
            if want_o:
                kq = _dot_nt(jnp.concatenate([k_l, q_ref[0, rows, lanes]], axis=0), kbd)
                shared[(c, g)] = (kq[0:CHUNK], kq[CHUNK:2 * CHUNK])
            else:
                shared[(c, g)] = (_dot_nt(k_l, kbd), None)
        yield

    units = [(c, g, d) for c in range(cb) for g in range(n_tiles) for d in range(N_DIR)]
    decs = []
    for c, g, d in units:
        lanes = slice(g * MXU_TILE, (g + 1) * MXU_TILE)
        be, gi, gj = fac[(c, d)]
        lower = (row > colj) if d == 0 else (row < colj)
        decs.append(jnp.where(lower, jnp.exp(jnp.where(lower, gi[:, lanes] - gj[:, lanes], 0.0)), 0.0))
    a_list = [fac[(c, d)][0][:, g * MXU_TILE:(g + 1) * MXU_TILE] * dec * shared[(c, g)][0]
              for (c, g, d), dec in zip(units, decs)]
    yield
    tinvs = yield from _unit_tri_inverses(a_list, eye_l, (row // DN_SUB) == (colj // DN_SUB), lo_half)

    for idx, ((c, g, d), dec, tinv) in enumerate(zip(units, decs, tinvs)):
        if idx % (n_tiles * N_DIR) == 0:
            yield
        rows = slice(c * CHUNK, (c + 1) * CHUNK)
        lanes = slice(g * MXU_TILE, (g + 1) * MXU_TILE)
        be, gi, _ = fac[(c, d)]
        be, gi = be[:, lanes], gi[:, lanes]
        last = CHUNK - 1 if d == 0 else 0
        e_g = jnp.exp(gi)
        gl_row = gi[last:last + 1, :]
        kf = k_ref[0, rows, lanes].astype(F32)
        vf = v_ref[0, rows, lanes].astype(F32)
        rhs = jnp.concatenate([_bdiag((be * e_g * kf).astype(BF16), lo_half), _bdiag((be * vf).astype(BF16), lo_half)],
                              axis=1)
        wu = _dot(tinv.astype(BF16), rhs)
        w_o[0, d, rows, lanes] = wu[:, 0:MXU_TILE].astype(BF16)
        uv_o[0, d, rows, lanes] = wu[:, MXU_TILE:2 * MXU_TILE]
        kt_o[0, d, rows, lanes] = (jnp.exp(gl_row - gi) * kf).astype(BF16)
        dl_o[0, d, c, :, lanes] = jnp.exp(gl_row)
        if want_o:
            qd_o[0, d, rows, lanes] = (e_g * q_ref[0, rows, lanes].astype(F32)).astype(BF16)
            in_o[0, d, rows, lanes] = ((dec + eye_l) * shared[(c, g)][1]).astype(BF16)


def _dnprep_parts(q, k, v, ab, arow, dtrow, exp2, tri, cb):
    want_o = q is not None
    b, t, w = k.shape
    tb = cb * CHUNK
    tok = lambda bi, i: (bi, i, 0)
    const2 = lambda bi, i: (0, 0)
    const3 = lambda bi, i: (0, 0, 0)
    dir_tok = lambda bi, i: (bi, 0, i, 0)
    data = ([q] if want_o else []) + [k, v]
    in_specs = ([pl.BlockSpec((1, tb, w), tok)] * len(data) + [pl.BlockSpec((1, tb, LANES), tok),
                pl.BlockSpec((1, LANES), const2), pl.BlockSpec((1, LANES), const2),
                pl.BlockSpec(exp2.shape, const3), pl.BlockSpec(tri.shape, const3)])
    big = lambda dt: jax.ShapeDtypeStruct((b, N_DIR, t, w), dt)
    big_spec = pl.BlockSpec((1, N_DIR, tb, w), dir_tok)
    out_shape = [big(BF16), big(F32), big(BF16), jax.ShapeDtypeStruct((b, N_DIR, t // CHUNK, 1, w), F32)]
    out_specs = [big_spec, big_spec, big_spec, pl.BlockSpec((1, N_DIR, cb, 1, w), lambda bi, i: (bi, 0, i, 0, 0))]
    if want_o:
        out_shape += [big(BF16), big(BF16)]
        out_specs += [big_spec, big_spec]
    return dict(kernel=functools.partial(_dnprep_kernel, cb=cb, want_o=want_o), steps=t // tb, in_specs=in_specs,
                out_specs=out_specs, out_shape=out_shape, scratch_shapes=[],
                args=data + [ab, arow, dtrow, exp2, tri])


def _dnscan_kernel(*refs, n_chunk, bb, want_o, have_s0, want_s):
    n_in = 6 if want_o else 4
    dir_refs = [refs[0:n_in], refs[n_in:2 * n_in]]
    pos = 2 * n_in
    s0_ref = refs[pos] if have_s0 else None
    pos += int(have_s0)
    o_refs = refs[pos:pos + N_DIR] if want_o else None
    pos += N_DIR if want_o else 0
    sout_ref = refs[pos] if want_s else None
    pos += int(want_s)
    s_scr = refs[pos]
    n_tiles = DN_WIDTH // MXU_TILE
    i = pl.program_id(1)

    @pl.when(i == 0)
    def _init():
        if have_s0:
            s_scr[...] = s0_ref[...]
        else:
            s_scr[...] = jnp.zeros_like(s_scr)

    bmask = _block_mask()
    lo_half = _half_mask()
    chains = [(bi, d, g) for bi in range(bb) for d in range(N_DIR) for g in range(n_tiles)]

    def body(j, carry):
        cidx = (j, n_chunk - 1 - j)
        r0s = [pl.multiple_of(cidx[d] * CHUNK, CHUNK) for d in range(N_DIR)]
        s_olds, r1s = [], []
        for bi, d, g in chains:
            lanes = slice(g * MXU_TILE, (g + 1) * MXU_TILE)
            w = dir_refs[d][0][bi, 0, pl.ds(r0s[d], CHUNK), lanes]
            if want_o:
                w = jnp.concatenate([w, dir_refs[d][4][bi, 0, pl.ds(r0s[d], CHUNK), lanes]], axis=0)
            s_old = s_scr[bi, d * n_tiles + g]
            s_olds.append(s_old)
            r1s.append(_dot(w, s_old.astype(BF16)))
        u_bfs = []
        for (bi, d, g), r1 in zip(chains, r1s):
            lanes = slice(g * MXU_TILE, (g + 1) * MXU_TILE)
            u_bfs.append((dir_refs[d][1][bi, 0, pl.ds(r0s[d], CHUNK), lanes] - r1[0:CHUNK]).astype(BF16))
        for (bi, d, g), r1, u_bf, s_old in zip(chains, r1s, u_bfs, s_olds):
            lanes = slice(g * MXU_TILE, (g + 1) * MXU_TILE)
            kt = dir_refs[d][2][bi, 0, pl.ds(r0s[d], CHUNK), lanes]
            ds = jnp.where(bmask, _dot_tn(kt, u_bf), 0.0)
            dl = dir_refs[d][3][bi, 0, cidx[d]][:, lanes]
            s_scr[bi, d * n_tiles + g] = s_old * dl + ds
            if want_o:
                intra = dir_refs[d][5][bi, 0, pl.ds(r0s[d], CHUNK), lanes]
                o_refs[d][bi, pl.ds(r0s[d], CHUNK), lanes] = r1[CHUNK:2 * CHUNK] + _dot(intra, _bdiag(u_bf, lo_half))
        return carry

    lax.fori_loop(0, n_chunk, body, 0)

    if want_s:
        @pl.when(i == pl.num_programs(1) - 1)
        def _fin():
            sout_ref[...] = s_scr[...]


def _dnscan(prep, s0, tb, bb, want_s):
    want_o = len(prep) == 6
    b, _, t, w = prep[0].shape
    n_t = t // tb
    n_chunk = tb // CHUNK
    n_chain = N_DIR * (w // MXU_TILE)

    def specs(d):
        blk = (lambda bi, i: i) if d == 0 else (lambda bi, i: n_t - 1 - i)
        big = pl.BlockSpec((bb, 1, tb, w), lambda bi, i: (bi, d, blk(bi, i), 0))
        dl = pl.BlockSpec((bb, 1, n_chunk, 1, w), lambda bi, i: (bi, d, blk(bi, i), 0, 0))
        return [big, big, big, dl] + ([big, big] if want_o else [])

    in_specs = specs(0) + specs(1)
    args = list(prep) + list(prep)
    state_spec = pl.BlockSpec((bb, n_chain, MXU_TILE, MXU_TILE), lambda bi, i: (bi, 0, 0, 0))
    if s0 is not None:
        in_specs.append(state_spec)
        args.append(s0)
    out_shape, out_specs = [], []
    if want_o:
        out_shape += [jax.ShapeDtypeStruct((b, t, w), F32)] * N_DIR
        out_specs += [pl.BlockSpec((bb, tb, w), lambda bi, i: (bi, i, 0)),
                      pl.BlockSpec((bb, tb, w), lambda bi, i: (bi, n_t - 1 - i, 0))]
    if want_s:
        out_shape.append(jax.ShapeDtypeStruct((b, n_chain, MXU_TILE, MXU_TILE), F32))
        out_specs.append(state_spec)
    return pl.pallas_call(
        functools.partial(_dnscan_kernel, n_chunk=n_chunk, bb=bb, want_o=want_o, have_s0=s0 is not None,
                          want_s=want_s),
        grid=(b // bb, n_t),
        in_specs=in_specs,
        out_specs=out_specs,
        out_shape=out_shape,
        scratch_shapes=[pltpu.VMEM((bb, n_chain, MXU_TILE, MXU_TILE), F32)],
        compiler_params=_cparams(2),
        name="dnscan",
    )(*args)


def _tail_kernel(x_ref, mod_ref, ya_ref, odf_ref, odb_ref, z_ref, gate_ref, dng_ref, havg_ref, wba_ref, wbd_ref,
                 wo_ref, gn2_ref, w1_ref, w2_ref, o_ref, *, ff_chunk):
    havg = havg_ref[...]
    dng = dng_ref[...]
    yd_parts = []
    for j in range(DN_WIDTH // MXU_TILE):
        sl = slice(j * MXU_TILE, (j + 1) * MXU_TILE)
        od = odf_ref[0, :, sl] + odb_ref[0, :, sl]
        ms = _dot((od * od).astype(BF16), havg)
        z = z_ref[0, :, sl].astype(F32)
        yd_parts.append((od * lax.rsqrt(ms + EPS) * dng * (z * _sigmoid(z))).astype(BF16))
    yd = jnp.concatenate(yd_parts, axis=1)
    ga = gate_ref[0, :, 0:D_MODEL].astype(F32)
    gd = gate_ref[0, :, D_MODEL:2 * D_MODEL].astype(F32)
    y = _sigmoid(ga) * _dot(ya_ref[0], wba_ref[...]) + _sigmoid(gd) * _dot(yd, wbd_ref[...])
    br = _dot(y.astype(BF16), wo_ref[...])
    mod = mod_ref[0]
    out1 = x_ref[0] + mod[2:3] * br
    ms2 = jnp.mean(out1 * out1, axis=-1, keepdims=True)
    hm = (out1 * lax.rsqrt(ms2 + EPS) * (gn2_ref[...] * (1.0 + mod[4:5])) + mod[3:4]).astype(BF16)
    acc = None
    for j in range(D_FF // ff_chunk):
        a = jnp.maximum(_dot(hm, w1_ref[:, j * ff_chunk:(j + 1) * ff_chunk]), 0.0)
        part = _dot((a * a).astype(BF16), w2_ref[j * ff_chunk:(j + 1) * ff_chunk, :])
        acc = part if acc is None else acc + part
    o_ref[0] = out1 + mod[5:6] * acc


def _resident(shape):
    return pl.BlockSpec(shape, lambda bi, i: (0,) * len(shape), pipeline_mode=pl.Buffered(1))


def _tail(x, mod3, y_attn, o_df, o_db, z, gates, dng, havg, wba, wbd, wo, gn2, w1, w2, tm):
    b, t, d = x.shape
    tok = lambda bi, i: (bi, i, 0)
    return pl.pallas_call(
        functools.partial(_tail_kernel, ff_chunk=1024),
        grid=(b, t // tm),
        in_specs=[pl.BlockSpec((1, tm, d), tok),
                  pl.BlockSpec((1, 6, d), lambda bi, i: (bi, 0, 0)),
                  pl.BlockSpec((1, tm, ATTN_WIDTH), tok),
                  pl.BlockSpec((1, tm, DN_WIDTH), tok),
                  pl.BlockSpec((1, tm, DN_WIDTH), tok),
                  pl.BlockSpec((1, tm, DN_WIDTH), tok),
                  pl.BlockSpec((1, tm, 2 * d), tok),
                  _resident((1, MXU_TILE)),
                  _resident((MXU_TILE, MXU_TILE)),
                  _resident(wba.shape),
                  _resident(wbd.shape),
                  _resident(wo.shape),
                  _resident((1, d)),
                  _resident(w1.shape),
                  _resident(w2.shape)],
        out_specs=pl.BlockSpec((1, tm, d), tok),
        out_shape=jax.ShapeDtypeStruct((b, t, d), F32),
        compiler_params=_cparams(2),
        name="tail",
    )(x, mod3, y_attn, o_df, o_db, z, gates, dng, havg, wba, wbd, wo, gn2, w1, w2)


def _head_avg(n, scale):
    idx = np.arange(n) // HEAD_DIM
    return jnp.asarray((idx[:, None] == idx[None, :]).astype(np.float32) * scale, BF16)


def _dn_expand_matrix():
    n = N_DIR * DN_HEADS
    m = np.zeros((N_DIR, 4 * n, 2 * DN_WIDTH), np.float32)
    for d in range(N_DIR):
        for part in range(2):
            for h in range(DN_HEADS):
                idx = d * DN_HEADS + h
                m[d, part * 2 * n + idx, h * HEAD_DIM:(h + 1) * HEAD_DIM] = 1.0
                m[d, part * 2 * n + n + idx, DN_WIDTH + h * HEAD_DIM:DN_WIDTH + (h + 1) * HEAD_DIM] = 1.0
    return jnp.asarray(m, BF16)


def _tri_matrices():
    i = np.arange(CHUNK)
    low = (i[:, None] >= i[None, :]).astype(np.float32)
    up = (i[:, None] <= i[None, :]).astype(np.float32)
    return jnp.asarray(np.stack([np.concatenate([low, low], axis=1), np.concatenate([up, up], axis=1)]), BF16)


def _rope_tables(seq):
    half = HEAD_DIM // 2
    n_freq = half // 2
    freqs = ROPE_BASE ** (-jnp.arange(n_freq, dtype=F32) / n_freq)
    pos = jnp.arange(seq)
    ang_r = (pos // GRID_W).astype(F32)[:, None] * freqs
    ang_c = (pos % GRID_W).astype(F32)[:, None] * freqs
    cos = jnp.concatenate([jnp.cos(ang_r)] * 2 + [jnp.cos(ang_c)] * 2, axis=1)
    sin = jnp.concatenate([-jnp.sin(ang_r), jnp.sin(ang_r), -jnp.sin(ang_c), jnp.sin(ang_c)], axis=1)
    reps = LANES // HEAD_DIM
    return jnp.tile(cos, (1, reps)), jnp.tile(sin, (1, reps))


def _pad_cols(w, n):
    return jnp.pad(w, ((0, 0), (0, n - w.shape[1])))


def kernel(x, c, ctx, c_ctx, w_ada, b_ada, g_norm1, w_in, q_norm_g, k_norm_g, attn_sink, conv_w, a_log, dt_bias,
           dn_norm_g, w_br_attn, w_br_dn, w_out, g_norm2, w_mlp1, w_mlp2):
    depth = w_ada.shape[0]
    assert depth == 1, "single-layer trunk only"
    b, s, d = x.shape
    n_ctx = ctx.shape[1]
    assert d == D_MODEL and w_in.shape[-1] == _IN_WIDTH
    assert s >= 3 * ATTN_BLOCK and s % ATTN_BLOCK == 0 and s % CHUNK == 0 and n_ctx % CHUNK == 0
    out_dtype = x.dtype
    w_in0 = w_in[0].astype(BF16)

    mod_rows = 16
    cc = jnp.concatenate([c.astype(F32), c_ctx.astype(F32)[None], jnp.zeros((mod_rows - b - 1, d), F32)], axis=0)
    mod = _ada(cc, w_ada[0], b_ada[0])
    mod3 = mod.reshape(mod_rows, 6, d)

    ab_cols = jnp.concatenate([w_in0[:, _OFF_DA:_OFF_GA]] * 2, axis=1)
    w_lat = jnp.concatenate([w_in0[:, :_OFF_DA], w_in0[:, _OFF_GA:], _pad_cols(ab_cols, LANES)], axis=1)
    hsum = _head_avg(MXU_TILE, 1.0)
    segs_lat = ((0, _OFF_DQ, None),
                (_OFF_DQ, DN_WIDTH, (0, True, True)), (_OFF_DK, DN_WIDTH, (DN_WIDTH, True, False)),
                (_OFF_DV, DN_WIDTH, (2 * DN_WIDTH, False, False)),
                (_OFF_DZ, DN_WIDTH, None), (_OFF_DA, 2 * D_MODEL, None), (_OFF_DA + 2 * D_MODEL, LANES, None))
    a_lat, q_d, k_d, v_d, z_lat, gates, ab_lat = _inproj(x, mod3, None, g_norm1[0], w_lat, conv_w[0], hsum, segs_lat,
                                                         (BF16, BF16, BF16, BF16, BF16, BF16, F32), tm=512)
    w_ctx = jnp.concatenate([w_in0[:, _OFF_AK:_OFF_DQ], w_in0[:, _OFF_DK:_OFF_DZ], _pad_cols(ab_cols, LANES)],
                            axis=1)
    segs_ctx = ((0, 2 * KV_WIDTH, None),
                (2 * KV_WIDTH, DN_WIDTH, (0, True, False)), (2 * KV_WIDTH + DN_WIDTH, DN_WIDTH, (DN_WIDTH, False, False)),
                (2 * KV_WIDTH + 2 * DN_WIDTH, LANES, None))
    kv_ctx, k_dc, v_dc, ab_ctx = _inproj(ctx, mod3, b, g_norm1[0], w_ctx, conv_w[0][:, DN_WIDTH:], hsum, segs_ctx,
                                         (BF16, BF16, BF16, F32), tm=n_ctx)

    cos, sin = _rope_tables(s)
    reps = LANES // HEAD_DIM
    n_gate = N_DIR * DN_HEADS
    arow = _pad_cols(jnp.tile(jnp.concatenate([jnp.exp(a_log[0]).reshape(1, n_gate), jnp.zeros((1, n_gate), F32)],
                                              axis=1), (1, 2)), LANES)
    dtrow = _pad_cols(jnp.tile(jnp.concatenate([dt_bias[0].reshape(1, n_gate), jnp.zeros((1, n_gate), F32)],
                                               axis=1), (1, 2)), LANES)
    exp2, tri = _dn_expand_matrix(), _tri_matrices()
    dn_cb = 4
    qb = (s // ATTN_BLOCK) // (s // (dn_cb * CHUNK))
    attn_parts = _attn_parts(a_lat, kv_ctx, cos, sin,
                             jnp.tile(q_norm_g[0].astype(F32), reps)[None],
                             jnp.tile(k_norm_g[0].astype(F32), reps)[None],
                             jnp.broadcast_to(attn_sink[0].astype(F32)[:, None], (ATTN_Q_HEADS, LANES)),
                             _head_avg(LANES, 1.0 / HEAD_DIM), qb)
    (y_attn,), prep_lat = _run_parts(
        [attn_parts, _dnprep_parts(q_d, k_d, v_d, ab_lat, arow, dtrow, exp2, tri, dn_cb)], b, "attn_dnprep")
    (prep_ctx,) = _run_parts([_dnprep_parts(None, k_dc, v_dc, ab_ctx, arow, dtrow, exp2, tri, dn_cb)], b, "dnprep")

    bb = max(q for q in (4, 2, 1) if b % q == 0)
    (s_ctx,) = _dnscan(prep_ctx, None, tb=n_ctx, bb=bb, want_s=True)
    o_df, o_db = _dnscan(prep_lat, s_ctx, tb=256, bb=bb, want_s=False)

    out = _tail(x, mod3, y_attn, o_df, o_db, z_lat, gates,
                jnp.tile(dn_norm_g[0].astype(F32), HEADS_PER_TILE)[None], _head_avg(MXU_TILE, 1.0 / HEAD_DIM),
                w_br_attn[0].astype(BF16), w_br_dn[0].astype(BF16), w_out[0].astype(BF16),
                g_norm2[0].reshape(1, d), w_mlp1[0].astype(BF16), w_mlp2[0].astype(BF16), tm=512)
    return out.astype(out_dtype)
```

```python
import functools

import numpy as np
import jax
import jax.numpy as jnp
from jax import lax
from jax.experimental import pallas as pl
from jax.experimental.pallas import tpu as pltpu

F32 = jnp.float32
BF16 = jnp.bfloat16

D_MODEL = 1024
GRID_W = 64
HEAD_DIM = 64
ATTN_Q_HEADS = 8
ATTN_KV_HEADS = 2
ATTN_GROUP = ATTN_Q_HEADS // ATTN_KV_HEADS
WINDOW = 128
ATTN_BLOCK = 128
ROPE_BASE = 10000.0
DN_HEADS = 8
CONV_W = 3
CHUNK = 64
N_DIR = 2
D_FF = 4 * D_MODEL
EPS = 1e-6
NEG_INF = -1e30
LOG2_E = float(np.log2(np.e))

ATTN_WIDTH = ATTN_Q_HEADS * HEAD_DIM
KV_WIDTH = ATTN_KV_HEADS * HEAD_DIM
DN_WIDTH = DN_HEADS * HEAD_DIM
LANES = 128
MXU_TILE = 256
HEADS_PER_TILE = MXU_TILE // HEAD_DIM
DN_SUB = 16
VMEM_LIMIT = 56 * 1024 * 1024

_OFF_AQ = 0
_OFF_AK = _OFF_AQ + ATTN_WIDTH
_OFF_AV = _OFF_AK + KV_WIDTH
_OFF_DQ = _OFF_AV + KV_WIDTH
_OFF_DK = _OFF_DQ + DN_WIDTH
_OFF_DV = _OFF_DK + DN_WIDTH
_OFF_DZ = _OFF_DV + DN_WIDTH
_OFF_DA = _OFF_DZ + DN_WIDTH
_OFF_DB = _OFF_DA + N_DIR * DN_HEADS
_OFF_GA = _OFF_DB + N_DIR * DN_HEADS
_OFF_GD = _OFF_GA + D_MODEL
_IN_WIDTH = _OFF_GD + D_MODEL


def _sigmoid(x):
    return 0.5 * jnp.tanh(0.5 * x) + 0.5


def _dot(a, b):
    return jnp.dot(a, b, preferred_element_type=F32)


def _dot_nt(a, b):
    return lax.dot_general(a, b, (((1,), (1,)), ((), ())), preferred_element_type=F32)


def _dot_tn(a, b):
    return lax.dot_general(a, b, (((0,), (0,)), ((), ())), preferred_element_type=F32)


def _cparams(n_axes):
    return pltpu.CompilerParams(dimension_semantics=("arbitrary",) * n_axes, vmem_limit_bytes=VMEM_LIMIT)


def _ada_kernel(c_ref, w_ref, b_ref, o_ref):
    c = c_ref[...]
    s = c * _sigmoid(c)
    o_ref[...] = _dot(s.astype(BF16), w_ref[...].astype(BF16)) + b_ref[...]


def _ada(cc, w_ada, b_ada):
    rows, d = cc.shape
    n = w_ada.shape[1]
    tn = 1536
    return pl.pallas_call(
        _ada_kernel,
        grid=(n // tn,),
        in_specs=[pl.BlockSpec((rows, d), lambda j: (0, 0)),
                  pl.BlockSpec((d, tn), lambda j: (0, j)),
                  pl.BlockSpec((1, tn), lambda j: (0, j))],
        out_specs=pl.BlockSpec((rows, tn), lambda j: (0, j)),
        out_shape=jax.ShapeDtypeStruct((rows, n), F32),
        compiler_params=_cparams(1),
        name="ada",
    )(cc, w_ada, b_ada.reshape(1, n))


def _inproj_kernel(x_ref, xprev_ref, xnext_ref, mod_ref, g_ref, wt_ref, wabt_ref, cw_ref, hsum_ref, *out_refs,
                   segs, tm, halo):
    w_refs = (wt_ref, wabt_ref)
    i = pl.program_id(1)
    last = pl.num_programs(1) - 1
    mod = mod_ref[0]
    scale = g_ref[...] * (1.0 + mod[1:2])

    def norm_mod(v):
        ms = jnp.mean(v * v, axis=-1, keepdims=True)
        return (v * lax.rsqrt(ms + EPS) * scale + mod[0:1]).astype(BF16)

    h = norm_mod(x_ref[0])
    h_halo = norm_mod(jnp.concatenate([xprev_ref[0], xnext_ref[0]], axis=0))
    keep_prev = (i > 0).astype(F32)
    keep_next = (i < last).astype(F32)
    rows = lax.broadcasted_iota(jnp.int32, (tm, MXU_TILE), 0)
    hsum = hsum_ref[...]
    def plain_piece(o_ref, src, start, lo, width):
        o_ref[0, :, lo:lo + width] = _dot_nt(h, w_refs[src][start + lo:start + lo + width, :]).astype(o_ref.dtype)

    def conv_piece(o_ref, src, start, lo, kind):
        conv_col, do_norm, is_q = kind
        w_rows = w_refs[src][start + lo:start + lo + MXU_TILE, :]
        p = _dot_nt(h, w_rows)
        p_halo = _dot_nt(h_halo, w_rows)
        yield
        p_prev = jnp.where(rows == 0, p_halo[halo - 1:halo] * keep_prev, pltpu.roll(p, 1, 0))
        p_next = jnp.where(rows == tm - 1, p_halo[halo:halo + 1] * keep_next, pltpu.roll(p, tm - 1, 0))
        cw = cw_ref[:, conv_col + lo:conv_col + lo + MXU_TILE]
        y = p_prev * cw[0:1] + p * cw[1:2] + p_next * cw[2:3]
        y = y * _sigmoid(y)
        if do_norm:
            y = y * lax.rsqrt(_dot((y * y).astype(BF16), hsum) + EPS)
            if is_q:
                y = y * (HEAD_DIM ** -0.5)
        o_ref[0, :, lo:lo + MXU_TILE] = y.astype(o_ref.dtype)

    plain, conv = [], []
    for o_ref, (src, start, size, kind) in zip(out_refs, segs):
        if kind is None:
            plain += [functools.partial(plain_piece, o_ref, src, start, lo, min(MXU_TILE, size - lo))
                      for lo in range(0, size, MXU_TILE)]
        else:
            conv += [functools.partial(conv_piece, o_ref, src, start, lo, kind) for lo in range(0, size, MXU_TILE)]
    pending = None
    while plain or conv or pending is not None:
        started = conv.pop(0)() if conv else None
        if started is not None:
            next(started)
        if pending is not None:
            for _ in pending:
                pass
        pending = started
        for _ in range(-(-len(plain) // (len(conv) + 1)) if plain else 0):
            plain.pop(0)()


def _inproj(x, mod3, mod_row, g_norm, w_t, wab_t, conv_w, hsum, segs, dtypes, tm):
    b, t, d = x.shape
    halo = 8
    r = tm // halo
    nblk = t // halo
    if mod_row is None:
        mod_map = lambda bi, i: (bi, 0, 0)
    else:
        mod_map = lambda bi, i: (mod_row, 0, 0)
    const2 = lambda bi, i: (0, 0)
    out_shape = [jax.ShapeDtypeStruct((b, t, size), dt) for (_, _, size, _), dt in zip(segs, dtypes)]
    out_specs = [pl.BlockSpec((1, tm, size), lambda bi, i: (bi, i, 0)) for (_, _, size, _) in segs]
    return pl.pallas_call(
        functools.partial(_inproj_kernel, segs=segs, tm=tm, halo=halo),
        grid=(b, t // tm),
        in_specs=[pl.BlockSpec((1, tm, d), lambda bi, i: (bi, i, 0)),
                  pl.BlockSpec((1, halo, d), lambda bi, i: (bi, jnp.maximum(i * r - 1, 0), 0)),
                  pl.BlockSpec((1, halo, d), lambda bi, i: (bi, jnp.minimum((i + 1) * r, nblk - 1), 0)),
                  pl.BlockSpec((1, 6, d), mod_map),
                  pl.BlockSpec((1, d), const2),
                  pl.BlockSpec(w_t.shape, const2),
                  pl.BlockSpec(wab_t.shape, const2),
                  pl.BlockSpec(conv_w.shape, const2),
                  pl.BlockSpec(hsum.shape, const2)],
        out_specs=out_specs,
        out_shape=out_shape,
        compiler_params=_cparams(2),
        name="inproj",
    )(x, x, x, mod3, g_norm.reshape(1, d), w_t, wab_t, conv_w, hsum)


def _rope(x, cos, sin, lane):
    swapped = jnp.where((lane % 32) < 16, pltpu.roll(x, LANES - 16, 1), pltpu.roll(x, 16, 1))
    return x * cos + swapped * sin


def _attn_kernel(a_ref, kvc_ref, cos_ref, sin_ref, qg_ref, kg_ref, sink_ref, havg_ref,
                 o_ref, k_s, vt_s, *, seq, ctx, qb):
    n = pl.program_id(1)
    havg = havg_ref[...]
    prep_rows = 256
    ctx_blocks = ctx // ATTN_BLOCK

    @pl.when(n == 0)
    def _prep():
        kg = kg_ref[...]
        kc = kvc_ref[0, :, 0:KV_WIDTH].astype(F32)
        ms = _dot((kc * kc).astype(BF16), havg)
        k_s[0:ctx, :] = (kc * lax.rsqrt(ms + EPS) * kg).astype(BF16)
        vc = kvc_ref[0, :, KV_WIDTH:2 * KV_WIDTH].astype(F32)
        for j in range(ctx_blocks):
            vt_s[j] = vc[j * ATTN_BLOCK:(j + 1) * ATTN_BLOCK, :].T.astype(BF16)
        lane = lax.broadcasted_iota(jnp.int32, (prep_rows, LANES), 1)
        for r0 in range(0, seq, prep_rows):
            kx = a_ref[0, r0:r0 + prep_rows, _OFF_AK:_OFF_AK + KV_WIDTH].astype(F32)
            ms = _dot((kx * kx).astype(BF16), havg)
            kx = kx * lax.rsqrt(ms + EPS) * kg
            kx = _rope(kx, cos_ref[r0:r0 + prep_rows, :], sin_ref[r0:r0 + prep_rows, :], lane)
            k_s[ctx + r0:ctx + r0 + prep_rows, :] = kx.astype(BF16)
            vx = a_ref[0, r0:r0 + prep_rows, _OFF_AV:_OFF_AV + KV_WIDTH].astype(F32)
            for j in range(prep_rows // ATTN_BLOCK):
                vt_s[ctx_blocks + r0 // ATTN_BLOCK + j] = vx[j * ATTN_BLOCK:(j + 1) * ATTN_BLOCK, :].T.astype(BF16)

    lane = lax.broadcasted_iota(jnp.int32, (ATTN_BLOCK, LANES), 1)
    qg = qg_ref[...]
    n_band = 3
    n_win = n_band * ATTN_BLOCK
    n_keys = n_win + ctx
    key_row = lax.broadcasted_iota(jnp.int32, (n_win, ATTN_BLOCK), 0)
    q_lane = lax.broadcasted_iota(jnp.int32, (n_win, ATTN_BLOCK), 1)
    zeros = jnp.zeros((HEAD_DIM, ATTN_BLOCK), BF16)
    groups = range(ATTN_KV_HEADS)
    sink_rows = [jnp.concatenate([sink_ref[g * ATTN_GROUP + r:g * ATTN_GROUP + r + 1, :] for r in range(ATTN_GROUP)],
                                 axis=1) * LOG2_E for g in groups]

    units = []
    k_alls, vt_alls, valids, rhss = [], [], [], {}
    for sb in range(qb):
        nq = n * qb + sb
        q0 = pl.multiple_of(nq * ATTN_BLOCK, ATTN_BLOCK)
        cos = cos_ref[pl.ds(q0, ATTN_BLOCK), :]
        sin = sin_ref[pl.ds(q0, ATTN_BLOCK), :]
        qts = []
        for j in range(ATTN_WIDTH // LANES):
            qx = a_ref[0, pl.ds(q0, ATTN_BLOCK), j * LANES:(j + 1) * LANES].astype(F32)
            ms = _dot((qx * qx).astype(BF16), havg)
            qx = _rope(qx * lax.rsqrt(ms + EPS) * qg, cos, sin, lane) * (HEAD_DIM ** -0.5 * LOG2_E)
            qts.append(qx.T.astype(BF16))
        blk0 = jnp.clip(nq - 1, 0, seq // ATTN_BLOCK - n_band)
        start = pl.multiple_of(blk0 * ATTN_BLOCK, ATTN_BLOCK)
        valid = jnp.abs(q0 + q_lane - (start + key_row)) <= WINDOW
        valids.append(jnp.concatenate([valid] * ATTN_GROUP, axis=1))
        k_alls.append(jnp.concatenate([k_s[pl.ds(ctx + start, n_win), :], k_s[0:ctx, :]], axis=0))
        vt_loc = vt_s[pl.ds(ctx_blocks + blk0, n_band)]
        vt_alls.append(jnp.concatenate([vt_loc[j] for j in range(n_band)] + [vt_s[j] for j in range(ctx_blocks)],
                                       axis=1))
        for g in groups:
            cols = []
            for r in range(ATTN_GROUP):
                h = g * ATTN_GROUP + r
                piece = qts[h // 2][(h % 2) * HEAD_DIM:(h % 2 + 1) * HEAD_DIM, :]
                cols.append(jnp.concatenate([piece, zeros] if g == 0 else [zeros, piece], axis=0))
            rhss[(sb, g)] = jnp.concatenate(cols, axis=1)
            units.append((sb, g))
        yield

    ss = []
    for sb, g in units:
        s = _dot(k_alls[sb], rhss[(sb, g)])
        ss.append(jnp.concatenate([jnp.where(valids[sb], s[0:n_win], NEG_INF), s[n_win:n_keys]], axis=0))
        yield
    p_bfs, invs = [], []
    for s, (sb, g) in zip(ss, units):
        m = jnp.maximum(jnp.max(s, axis=0, keepdims=True), sink_rows[g])
        p = jnp.exp2(s - m)
        invs.append(1.0 / (jnp.sum(p, axis=0, keepdims=True) + jnp.exp2(sink_rows[g] - m)))
        p_bfs.append(p.astype(BF16))
        yield
    pieces = {sb: [] for sb in range(qb)}
    for (sb, g), p_bf, inv in zip(units, p_bfs, invs):
        vt_g = vt_alls[sb][g * HEAD_DIM:(g + 1) * HEAD_DIM, :]
        for pair in range(ATTN_GROUP // 2):
            lanes = slice(pair * MXU_TILE, (pair + 1) * MXU_TILE)
            ot = _dot(vt_g, p_bf[:, lanes]) * inv[:, lanes]
            pieces[sb] += [ot[:, 0:ATTN_BLOCK], ot[:, ATTN_BLOCK:2 * ATTN_BLOCK]]
        yield
    for sb in range(qb):
        outs = [jnp.concatenate([pieces[sb][2 * j], pieces[sb][2 * j + 1]], axis=0).T
                for j in range(ATTN_Q_HEADS // 2)]
        o_ref[0, sb * ATTN_BLOCK:(sb + 1) * ATTN_BLOCK, :] = jnp.concatenate(outs, axis=1).astype(o_ref.dtype)


def _run_parts(parts, b, name):
    steps = parts[0]["steps"]
    assert all(p["steps"] == steps for p in parts)
    counts = [(len(p["args"]), len(p["out_shape"]), len(p["scratch_shapes"])) for p in parts]
    n_in = sum(c[0] for c in counts)
    n_out = sum(c[1] for c in counts)

    def kern(*refs):
        i_pos, o_pos, s_pos = 0, n_in, n_in + n_out
        bodies = []
        for p, (ci, co, cs) in zip(parts, counts):
            bodies.append(p["kernel"](*refs[i_pos:i_pos + ci], *refs[o_pos:o_pos + co], *refs[s_pos:s_pos + cs]))
            i_pos, o_pos, s_pos = i_pos + ci, o_pos + co, s_pos + cs
        while bodies:
            for body in list(bodies):
                try:
                    next(body)
                except StopIteration:
                    bodies.remove(body)

    outs = pl.pallas_call(
        kern,
        grid=(b, steps),
        in_specs=[s for p in parts for s in p["in_specs"]],
        out_specs=[s for p in parts for s in p["out_specs"]],
        out_shape=[s for p in parts for s in p["out_shape"]],
        scratch_shapes=[s for p in parts for s in p["scratch_shapes"]],
        compiler_params=_cparams(2),
        name=name,
    )(*[a for p in parts for a in p["args"]])
    res, pos = [], 0
    for _, co, _ in counts:
        res.append(outs[pos:pos + co])
        pos += co
    return res


def _attn_parts(a_lat, kv_ctx, cos, sin, qg, kg, sink, havg, qb):
    b, s, wa = a_lat.shape
    ctx = kv_ctx.shape[1]
    nb = s // ATTN_BLOCK
    const2 = lambda bi, n: (0, 0)
    return dict(
        kernel=functools.partial(_attn_kernel, seq=s, ctx=ctx, qb=qb),
        steps=nb // qb,
        in_specs=[pl.BlockSpec((1, s, wa), lambda bi, n: (bi, 0, 0)),
                  pl.BlockSpec((1, ctx, 2 * KV_WIDTH), lambda bi, n: (bi, 0, 0)),
                  pl.BlockSpec((s, LANES), const2),
                  pl.BlockSpec((s, LANES), const2),
                  pl.BlockSpec((1, LANES), const2),
                  pl.BlockSpec((1, LANES), const2),
                  pl.BlockSpec((ATTN_Q_HEADS, LANES), const2),
                  pl.BlockSpec((LANES, LANES), const2)],
        out_specs=[pl.BlockSpec((1, qb * ATTN_BLOCK, ATTN_WIDTH), lambda bi, n: (bi, n, 0))],
        out_shape=[jax.ShapeDtypeStruct((b, s, ATTN_WIDTH), BF16)],
        scratch_shapes=[pltpu.VMEM((ctx + s, KV_WIDTH), BF16),
                        pltpu.VMEM(((ctx + s) // ATTN_BLOCK, KV_WIDTH, ATTN_BLOCK), BF16)],
        args=[a_lat, kv_ctx, cos, sin, qg, kg, sink, havg])


def _half_mask():
    return lax.broadcasted_iota(jnp.int32, (CHUNK, LANES), 1) < HEAD_DIM


def _bdiag(x_lane, lo_half):
    zeros = jnp.zeros((CHUNK, LANES), x_lane.dtype)
    blocks = []
    for h in range(HEADS_PER_TILE):
        col = (h * HEAD_DIM) // LANES
        piece = x_lane[:, col * LANES:(col + 1) * LANES]
        piece = jnp.where(lo_half if (h * HEAD_DIM) % LANES == 0 else jnp.logical_not(lo_half), piece, zeros)
        blocks.append(jnp.concatenate([piece if c == col else zeros for c in range(MXU_TILE // LANES)], axis=1))
    return jnp.concatenate(blocks, axis=0)


def _unit_tri_inverses(a_list, eye_l, sub_mask, lo_half):
    ads = [jnp.where(sub_mask, a, 0.0) for a in a_list]
    ys = [jnp.where(sub_mask, 0.0, a) for a in a_list]
    ps = [eye_l - ad for ad in ads]
    pws = [ad.astype(BF16) for ad in ads]
    n_levels = int(np.log2(DN_SUB))
    c1, c2, c3 = CHUNK, 2 * CHUNK, 3 * CHUNK
    for level in range(n_levels):
        rhss = [_bdiag(pw, lo_half) for pw in pws]
        if level == 0:
            boths = [_dot(jnp.concatenate([pw, y.astype(BF16)], axis=0), rhs) for pw, y, rhs in zip(pws, ys, rhss)]
            pws = [both[0:c1].astype(BF16) for both in boths]
            ys = [y - both[c1:c2] for y, both in zip(ys, boths)]
        elif level < n_levels - 1:
            boths = [_dot(jnp.concatenate([pw, p.astype(BF16), y.astype(BF16)], axis=0), rhs)
                     for pw, p, y, rhs in zip(pws, ps, ys, rhss)]
            pws = [both[0:c1].astype(BF16) for both in boths]
            ps = [p + both[c1:c2] for p, both in zip(ps, boths)]
            ys = [y + both[c2:c3] for y, both in zip(ys, boths)]
        else:
            boths = [_dot(jnp.concatenate([p.astype(BF16), y.astype(BF16)], axis=0), rhs)
                     for p, y, rhs in zip(ps, ys, rhss)]
            ps = [p + both[0:c1] for p, both in zip(ps, boths)]
            ys = [y + both[c1:c2] for y, both in zip(ys, boths)]
        yield
    assert CHUNK // DN_SUB == 4
    b_bfs = [y.astype(BF16) for y in ys]
    boths = [_dot(jnp.concatenate([b, p.astype(BF16)], axis=0), _bdiag(b, lo_half)) for b, p in zip(b_bfs, ps)]
    yield
    zs = [p - both[c1:c2] for p, both in zip(ps, boths)]
    return [z + _dot(z.astype(BF16), _bdiag(both[0:c1].astype(BF16), lo_half)) for z, both in zip(zs, boths)]


def _dn_factors(ab, d, arow, dtrow, exp2_ref, tri_ref, eye_t):
    lane = lax.broadcasted_iota(jnp.int32, ab.shape, 1)
    z = ab + dtrow
    softplus = jnp.maximum(z, 0.0) + jnp.log(1.0 + jnp.exp(-jnp.abs(z)))
    is_g = (lane % 32) < 16
    x = jnp.where(is_g, -arow * softplus, _sigmoid(ab))
    def hi_lo(v):
        hi = v.astype(BF16)
        return hi, (v - hi.astype(F32)).astype(BF16)

    x = jnp.where(is_g, _dot(tri_ref[d], jnp.concatenate(hi_lo(x), axis=0)), x)
    x_hi, x_lo = hi_lo(x)
    xhl = jnp.where(lane < 32, x_hi, x_lo)[:, 0:64]
    y = _dot(xhl, exp2_ref[d])
    gi = y[:, 0:DN_WIDTH]
    be = y[:, DN_WIDTH:2 * DN_WIDTH]
    gj = jnp.sum(gi * eye_t, axis=0, keepdims=True)
    return be, gi, jnp.broadcast_to(gj, gi.shape)


def _block_mask():
    return (lax.broadcasted_iota(jnp.int32, (MXU_TILE, MXU_TILE), 0) // HEAD_DIM
            == lax.broadcasted_iota(jnp.int32, (MXU_TILE, MXU_TILE), 1) // HEAD_DIM)


def _dnprep_kernel(*refs, cb, want_o):
    if want_o:
        q_ref, k_ref, v_ref, ab_ref = refs[:4]
        refs = refs[4:]
    else:
        q_ref = None
        k_ref, v_ref, ab_ref = refs[:3]
        refs = refs[3:]
    arow_ref, dtrow_ref, exp2_ref, tri_ref, w_o, uv_o, kt_o, dl_o = refs[:8]
    qd_o, in_o = refs[8:10] if want_o else (None, None)
    n_tiles = DN_WIDTH // MXU_TILE
    row = lax.broadcasted_iota(jnp.int32, (CHUNK, MXU_TILE), 0)
    colj = lax.broadcasted_iota(jnp.int32, (CHUNK, MXU_TILE), 1) % HEAD_DIM
    eye_l = (row == colj).astype(F32)
    eye_t = jnp.concatenate([eye_l] * n_tiles, axis=1)
    lo_half = _half_mask()
    arow = arow_ref[...]
    dtrow = dtrow_ref[...]

    fac = {}
    shared = {}
    for c in range(cb):
        rows = slice(c * CHUNK, (c + 1) * CHUNK)
        ab = ab_ref[0, rows, :]
        for d in range(N_DIR):
            fac[(c, d)] = _dn_factors(ab, d, arow, dtrow, exp2_ref, tri_ref, eye_t)
        for g in range(n_tiles):
            lanes = slice(g * MXU_TILE, (g + 1) * MXU_TILE)
            k_l = k_ref[0, rows, lanes]
            kbd = _bdiag(k_l, lo_half)
            if want_o:
                kq = _dot_nt(jnp.concatenate([k_l, q_ref[0, rows, lanes]], axis=0), kbd)
                shared[(c, g)] = (kq[0:CHUNK], kq[CHUNK:2 * CHUNK])
            else:
                shared[(c, g)] = (_dot_nt(k_l, kbd), None)
        yield

    units = [(c, g, d) for c in range(cb) for g in range(n_tiles) for d in range(N_DIR)]
    decs = []
    for c, g, d in units:
        lanes = slice(g * MXU_TILE, (g + 1) * MXU_TILE)
        be, gi, gj = fac[(c, d)]
        lower = (row > colj) if d == 0 else (row < colj)
        decs.append(jnp.where(lower, jnp.exp(jnp.where(lower, gi[:, lanes] - gj[:, lanes], 0.0)), 0.0))
    a_list = [fac[(c, d)][0][:, g * MXU_TILE:(g + 1) * MXU_TILE] * dec * shared[(c, g)][0]
              for (c, g, d), dec in zip(units, decs)]
    yield
    tinvs = yield from _unit_tri_inverses(a_list, eye_l, (row // DN_SUB) == (colj // DN_SUB), lo_half)

    for idx, ((c, g, d), dec, tinv) in enumerate(zip(units, decs, tinvs)):
        if idx % (n_tiles * N_DIR) == 0:
            yield
        rows = slice(c * CHUNK, (c + 1) * CHUNK)
        lanes = slice(g * MXU_TILE, (g + 1) * MXU_TILE)
        be, gi, _ = fac[(c, d)]
        be, gi = be[:, lanes], gi[:, lanes]
        last = CHUNK - 1 if d == 0 else 0
        e_g = jnp.exp(gi)
        gl_row = gi[last:last + 1, :]
        kf = k_ref[0, rows, lanes].astype(F32)
        vf = v_ref[0, rows, lanes].astype(F32)
        rhs = jnp.concatenate([_bdiag((be * e_g * kf).astype(BF16), lo_half), _bdiag((be * vf).astype(BF16), lo_half)],
                              axis=1)
        wu = _dot(tinv.astype(BF16), rhs)
        w_o[0, d, rows, lanes] = wu[:, 0:MXU_TILE].astype(BF16)
        uv_o[0, d, rows, lanes] = wu[:, MXU_TILE:2 * MXU_TILE].astype(BF16)
        kt_o[0, d, rows, lanes] = (jnp.exp(gl_row - gi) * kf).astype(BF16)
        dl_o[0, d, c, :, lanes] = jnp.exp(gl_row)
        if want_o:
            qd_o[0, d, rows, lanes] = (e_g * q_ref[0, rows, lanes].astype(F32)).astype(BF16)
            in_o[0, d, rows, lanes] = ((dec + eye_l) * shared[(c, g)][1]).astype(BF16)


def _dnprep_parts(q, k, v, ab, arow, dtrow, exp2, tri, cb):
    want_o = q is not None
    b, t, w = k.shape
    tb = cb * CHUNK
    tok = lambda bi, i: (bi, i, 0)
    const2 = lambda bi, i: (0, 0)
    const3 = lambda bi, i: (0, 0, 0)
    dir_tok = lambda bi, i: (bi, 0, i, 0)
    data = ([q] if want_o else []) + [k, v]
    in_specs = ([pl.BlockSpec((1, tb, w), tok)] * len(data) + [pl.BlockSpec((1, tb, LANES), tok),
                pl.BlockSpec((1, LANES), const2), pl.BlockSpec((1, LANES), const2),
                pl.BlockSpec(exp2.shape, const3), pl.BlockSpec(tri.shape, const3)])
    big = lambda dt: jax.ShapeDtypeStruct((b, N_DIR, t, w), dt)
    big_spec = pl.BlockSpec((1, N_DIR, tb, w), dir_tok)
    out_shape = [big(BF16), big(BF16), big(BF16), jax.ShapeDtypeStruct((b, N_DIR, t // CHUNK, 1, w), F32)]
    out_specs = [big_spec, big_spec, big_spec, pl.BlockSpec((1, N_DIR, cb, 1, w), lambda bi, i: (bi, 0, i, 0, 0))]
    if want_o:
        out_shape += [big(BF16), big(BF16)]
        out_specs += [big_spec, big_spec]
    return dict(kernel=functools.partial(_dnprep_kernel, cb=cb, want_o=want_o), steps=t // tb, in_specs=in_specs,
                out_specs=out_specs, out_shape=out_shape, scratch_shapes=[],
                args=data + [ab, arow, dtrow, exp2, tri])


def _dnscan_kernel(*refs, n_chunk, bb, want_o, have_s0, want_s):
    n_in = 6 if want_o else 4
    dir_refs = [refs[0:n_in], refs[n_in:2 * n_in]]
    pos = 2 * n_in
    s0_ref = refs[pos] if have_s0 else None
    pos += int(have_s0)
    o_refs = refs[pos:pos + N_DIR] if want_o else None
    pos += N_DIR if want_o else 0
    sout_ref = refs[pos] if want_s else None
    pos += int(want_s)
    s_scr = refs[pos]
    n_tiles = DN_WIDTH // MXU_TILE
    i = pl.program_id(1)

    @pl.when(i == 0)
    def _init():
        if have_s0:
            s_scr[...] = s0_ref[...]
        else:
            s_scr[...] = jnp.zeros_like(s_scr)

    bmask = _block_mask()
    lo_half = _half_mask()
    chains = [(bi, d, g) for bi in range(bb) for d in range(N_DIR) for g in range(n_tiles)]

    def body(j, carry):
        cidx = (j, n_chunk - 1 - j)
        r0s = [pl.multiple_of(cidx[d] * CHUNK, CHUNK) for d in range(N_DIR)]
        s_olds, r1s = [], []
        for bi, d, g in chains:
            lanes = slice(g * MXU_TILE, (g + 1) * MXU_TILE)
            w = dir_refs[d][0][bi, 0, pl.ds(r0s[d], CHUNK), lanes]
            if want_o:
                w = jnp.concatenate([w, dir_refs[d][4][bi, 0, pl.ds(r0s[d], CHUNK), lanes]], axis=0)
            s_old = s_scr[bi, d * n_tiles + g]
            s_olds.append(s_old)
            r1s.append(_dot(w, s_old.astype(BF16)))
        u_bfs = []
        for (bi, d, g), r1 in zip(chains, r1s):
            lanes = slice(g * MXU_TILE, (g + 1) * MXU_TILE)
            u_bfs.append((dir_refs[d][1][bi, 0, pl.ds(r0s[d], CHUNK), lanes] - r1[0:CHUNK]).astype(BF16))
        for (bi, d, g), r1, u_bf, s_old in zip(chains, r1s, u_bfs, s_olds):
            lanes = slice(g * MXU_TILE, (g + 1) * MXU_TILE)
            kt = dir_refs[d][2][bi, 0, pl.ds(r0s[d], CHUNK), lanes]
            ds = jnp.where(bmask, _dot_tn(kt, u_bf), 0.0)
            dl = dir_refs[d][3][bi, 0, cidx[d]][:, lanes]
            s_scr[bi, d * n_tiles + g] = s_old * dl + ds
            if want_o:
                intra = dir_refs[d][5][bi, 0, pl.ds(r0s[d], CHUNK), lanes]
                o = r1[CHUNK:2 * CHUNK] + _dot(intra, _bdiag(u_bf, lo_half))
                o_refs[d][bi, pl.ds(r0s[d], CHUNK), lanes] = o.astype(BF16)
        return carry

    lax.fori_loop(0, n_chunk, body, 0)

    if want_s:
        @pl.when(i == pl.num_programs(1) - 1)
        def _fin():
            sout_ref[...] = s_scr[...]


def _dnscan(prep, s0, tb, bb, want_s):
    want_o = len(prep) == 6
    b, _, t, w = prep[0].shape
    n_t = t // tb
    n_chunk = tb // CHUNK
    n_chain = N_DIR * (w // MXU_TILE)

    def specs(d):
        blk = (lambda bi, i: i) if d == 0 else (lambda bi, i: n_t - 1 - i)
        big = pl.BlockSpec((bb, 1, tb, w), lambda bi, i: (bi, d, blk(bi, i), 0))
        dl = pl.BlockSpec((bb, 1, n_chunk, 1, w), lambda bi, i: (bi, d, blk(bi, i), 0, 0))
        return [big, big, big, dl] + ([big, big] if want_o else [])

    in_specs = specs(0) + specs(1)
    args = list(prep) + list(prep)
    state_spec = pl.BlockSpec((bb, n_chain, MXU_TILE, MXU_TILE), lambda bi, i: (bi, 0, 0, 0))
    if s0 is not None:
        in_specs.append(state_spec)
        args.append(s0)
    out_shape, out_specs = [], []
    if want_o:
        out_shape += [jax.ShapeDtypeStruct((b, t, w), BF16)] * N_DIR
        out_specs += [pl.BlockSpec((bb, tb, w), lambda bi, i: (bi, i, 0)),
                      pl.BlockSpec((bb, tb, w), lambda bi, i: (bi, n_t - 1 - i, 0))]
    if want_s:
        out_shape.append(jax.ShapeDtypeStruct((b, n_chain, MXU_TILE, MXU_TILE), F32))
        out_specs.append(state_spec)
    return pl.pallas_call(
        functools.partial(_dnscan_kernel, n_chunk=n_chunk, bb=bb, want_o=want_o, have_s0=s0 is not None,
                          want_s=want_s),
        grid=(b // bb, n_t),
        in_specs=in_specs,
        out_specs=out_specs,
        out_shape=out_shape,
        scratch_shapes=[pltpu.VMEM((bb, n_chain, MXU_TILE, MXU_TILE), F32)],
        compiler_params=_cparams(2),
        name="dnscan",
    )(*args)


def _tail_kernel(x_ref, mod_ref, ya_ref, odf_ref, odb_ref, z_ref, gate_ref, dng_ref, havg_ref, wba_ref, wbd_ref,
                 wo_ref, gn2_ref, w1_ref, w2_ref, o_ref, *, ff_chunk):
    havg = havg_ref[...]
    dng = dng_ref[...]
    yd_parts = []
    for j in range(DN_WIDTH // MXU_TILE):
        sl = slice(j * MXU_TILE, (j + 1) * MXU_TILE)
        od = odf_ref[0, :, sl].astype(F32) + odb_ref[0, :, sl].astype(F32)
        ms = _dot((od * od).astype(BF16), havg)
        z = z_ref[0, :, sl].astype(F32)
        yd_parts.append((od * lax.rsqrt(ms + EPS) * dng * (z * _sigmoid(z))).astype(BF16))
    yd = jnp.concatenate(yd_parts, axis=1)
    ga = gate_ref[0, :, 0:D_MODEL].astype(F32)
    gd = gate_ref[0, :, D_MODEL:2 * D_MODEL].astype(F32)
    y = _sigmoid(ga) * _dot(ya_ref[0], wba_ref[...]) + _sigmoid(gd) * _dot(yd, wbd_ref[...])
    br = _dot(y.astype(BF16), wo_ref[...])
    mod = mod_ref[0]
    out1 = x_ref[0] + mod[2:3] * br
    ms2 = jnp.mean(out1 * out1, axis=-1, keepdims=True)
    hm = (out1 * lax.rsqrt(ms2 + EPS) * (gn2_ref[...] * (1.0 + mod[4:5])) + mod[3:4]).astype(BF16)
    acc = None
    for j in range(D_FF // ff_chunk):
        a = jnp.maximum(_dot(hm, w1_ref[:, j * ff_chunk:(j + 1) * ff_chunk]), 0.0)
        part = _dot((a * a).astype(BF16), w2_ref[j * ff_chunk:(j + 1) * ff_chunk, :])
        acc = part if acc is None else acc + part
    o_ref[0] = out1 + mod[5:6] * acc


def _resident(shape):
    return pl.BlockSpec(shape, lambda bi, i: (0,) * len(shape), pipeline_mode=pl.Buffered(1))


def _tail(x, mod3, y_attn, o_df, o_db, z, gates, dng, havg, wba, wbd, wo, gn2, w1, w2, tm):
    b, t, d = x.shape
    tok = lambda bi, i: (bi, i, 0)
    return pl.pallas_call(
        functools.partial(_tail_kernel, ff_chunk=1024),
        grid=(b, t // tm),
        in_specs=[pl.BlockSpec((1, tm, d), tok),
                  pl.BlockSpec((1, 6, d), lambda bi, i: (bi, 0, 0)),
                  pl.BlockSpec((1, tm, ATTN_WIDTH), tok),
                  pl.BlockSpec((1, tm, DN_WIDTH), tok),
                  pl.BlockSpec((1, tm, DN_WIDTH), tok),
                  pl.BlockSpec((1, tm, DN_WIDTH), tok),
                  pl.BlockSpec((1, tm, 2 * d), tok),
                  _resident((1, MXU_TILE)),
                  _resident((MXU_TILE, MXU_TILE)),
                  _resident(wba.shape),
                  _resident(wbd.shape),
                  _resident(wo.shape),
                  _resident((1, d)),
                  _resident(w1.shape),
                  _resident(w2.shape)],
        out_specs=pl.BlockSpec((1, tm, d), tok),
        out_shape=jax.ShapeDtypeStruct((b, t, d), F32),
        compiler_params=_cparams(2),
        name="tail",
    )(x, mod3, y_attn, o_df, o_db, z, gates, dng, havg, wba, wbd, wo, gn2, w1, w2)


def _head_avg(n, scale):
    idx = np.arange(n) // HEAD_DIM
    return jnp.asarray((idx[:, None] == idx[None, :]).astype(np.float32) * scale, BF16)


def _dn_expand_matrix():
    n = N_DIR * DN_HEADS
    m = np.zeros((N_DIR, 4 * n, 2 * DN_WIDTH), np.float32)
    for d in range(N_DIR):
        for part in range(2):
            for h in range(DN_HEADS):
                idx = d * DN_HEADS + h
                m[d, part * 2 * n + idx, h * HEAD_DIM:(h + 1) * HEAD_DIM] = 1.0
                m[d, part * 2 * n + n + idx, DN_WIDTH + h * HEAD_DIM:DN_WIDTH + (h + 1) * HEAD_DIM] = 1.0
    return jnp.asarray(m, BF16)


def _tri_matrices():
    i = np.arange(CHUNK)
    low = (i[:, None] >= i[None, :]).astype(np.float32)
    up = (i[:, None] <= i[None, :]).astype(np.float32)
    return jnp.asarray(np.stack([np.concatenate([low, low], axis=1), np.concatenate([up, up], axis=1)]), BF16)


def _rope_tables(seq):
    half = HEAD_DIM // 2
    n_freq = half // 2
    freqs = ROPE_BASE ** (-jnp.arange(n_freq, dtype=F32) / n_freq)
    pos = jnp.arange(seq)
    ang_r = (pos // GRID_W).astype(F32)[:, None] * freqs
    ang_c = (pos % GRID_W).astype(F32)[:, None] * freqs
    cos = jnp.concatenate([jnp.cos(ang_r)] * 2 + [jnp.cos(ang_c)] * 2, axis=1)
    sin = jnp.concatenate([-jnp.sin(ang_r), jnp.sin(ang_r), -jnp.sin(ang_c), jnp.sin(ang_c)], axis=1)
    reps = LANES // HEAD_DIM
    return jnp.tile(cos, (1, reps)), jnp.tile(sin, (1, reps))


def _pad_cols(w, n):
    return jnp.pad(w, ((0, 0), (0, n - w.shape[1])))


def kernel(x, c, ctx, c_ctx, w_ada, b_ada, g_norm1, w_in, q_norm_g, k_norm_g, attn_sink, conv_w, a_log, dt_bias,
           dn_norm_g, w_br_attn, w_br_dn, w_out, g_norm2, w_mlp1, w_mlp2):
    depth = w_ada.shape[0]
    assert depth == 1, "single-layer trunk only"
    b, s, d = x.shape
    n_ctx = ctx.shape[1]
    assert d == D_MODEL and w_in.shape[-1] == _IN_WIDTH
    assert s >= 3 * ATTN_BLOCK and s % ATTN_BLOCK == 0 and s % CHUNK == 0 and n_ctx % CHUNK == 0
    out_dtype = x.dtype
    w_in_t = jnp.swapaxes(w_in[0], 0, 1).astype(BF16)

    mod_rows = 16
    cc = jnp.concatenate([c.astype(F32), c_ctx.astype(F32)[None], jnp.zeros((mod_rows - b - 1, d), F32)], axis=0)
    mod = _ada(cc, w_ada[0], b_ada[0])
    mod3 = mod.reshape(mod_rows, 6, d)

    ab_rows = w_in_t[_OFF_DA:_OFF_GA]
    wab_t = jnp.concatenate([ab_rows, ab_rows, jnp.zeros((LANES - 2 * (_OFF_GA - _OFF_DA), d), BF16)], axis=0)
    hsum = _head_avg(MXU_TILE, 1.0)
    segs_lat = ((0, 0, _OFF_DQ, None),
                (0, _OFF_DQ, DN_WIDTH, (0, True, True)), (0, _OFF_DK, DN_WIDTH, (DN_WIDTH, True, False)),
                (0, _OFF_DV, DN_WIDTH, (2 * DN_WIDTH, False, False)),
                (0, _OFF_DZ, DN_WIDTH, None), (0, _OFF_GA, 2 * D_MODEL, None), (1, 0, LANES, None))
    a_lat, q_d, k_d, v_d, z_lat, gates, ab_lat = _inproj(x, mod3, None, g_norm1[0], w_in_t, wab_t, conv_w[0], hsum,
                                                         segs_lat, (BF16, BF16, BF16, BF16, BF16, BF16, F32), tm=512)
    segs_ctx = ((0, _OFF_AK, 2 * KV_WIDTH, None),
                (0, _OFF_DK, DN_WIDTH, (0, True, False)), (0, _OFF_DV, DN_WIDTH, (DN_WIDTH, False, False)),
                (1, 0, LANES, None))
    kv_ctx, k_dc, v_dc, ab_ctx = _inproj(ctx, mod3, b, g_norm1[0], w_in_t, wab_t, conv_w[0][:, DN_WIDTH:], hsum,
                                         segs_ctx, (BF16, BF16, BF16, F32), tm=n_ctx)

    cos, sin = _rope_tables(s)
    reps = LANES // HEAD_DIM
    n_gate = N_DIR * DN_HEADS
    arow = _pad_cols(jnp.tile(jnp.concatenate([jnp.exp(a_log[0]).reshape(1, n_gate), jnp.zeros((1, n_gate), F32)],
                                              axis=1), (1, 2)), LANES)
    dtrow = _pad_cols(jnp.tile(jnp.concatenate([dt_bias[0].reshape(1, n_gate), jnp.zeros((1, n_gate), F32)],
                                               axis=1), (1, 2)), LANES)
    exp2, tri = _dn_expand_matrix(), _tri_matrices()
    dn_cb = 4
    qb = (s // ATTN_BLOCK) // (s // (dn_cb * CHUNK))
    attn_parts = _attn_parts(a_lat, kv_ctx, cos, sin,
                             jnp.tile(q_norm_g[0].astype(F32), reps)[None],
                             jnp.tile(k_norm_g[0].astype(F32), reps)[None],
                             jnp.broadcast_to(attn_sink[0].astype(F32)[:, None], (ATTN_Q_HEADS, LANES)),
                             _head_avg(LANES, 1.0 / HEAD_DIM), qb)
    (y_attn,), prep_lat = _run_parts(
        [attn_parts, _dnprep_parts(q_d, k_d, v_d, ab_lat, arow, dtrow, exp2, tri, dn_cb)], b, "attn_dnprep")
    (prep_ctx,) = _run_parts([_dnprep_parts(None, k_dc, v_dc, ab_ctx, arow, dtrow, exp2, tri, dn_cb)], b, "dnprep")

    bb = max(q for q in (4, 2, 1) if b % q == 0)
    (s_ctx,) = _dnscan(prep_ctx, None, tb=n_ctx, bb=bb, want_s=True)
    o_df, o_db = _dnscan(prep_lat, s_ctx, tb=256, bb=bb, want_s=False)

    out = _tail(x, mod3, y_attn, o_df, o_db, z_lat, gates,
                jnp.tile(dn_norm_g[0].astype(F32), HEADS_PER_TILE)[None], _head_avg(MXU_TILE, 1.0 / HEAD_DIM),
                w_br_attn[0].astype(BF16), w_br_dn[0].astype(BF16), w_out[0].astype(BF16),
                g_norm2[0].reshape(1, d), w_mlp1[0].astype(BF16), w_mlp2[0].astype(BF16), tm=512)
    return out.astype(out_dtype)
```

```python
import functools

import numpy as np
import jax
import jax.numpy as jnp
from jax import lax
from jax.experimental import pallas as pl
from jax.experimental.pallas import tpu as pltpu

F32 = jnp.float32
BF16 = jnp.bfloat16

D_MODEL = 1024
GRID_W = 64
HEAD_DIM = 64
ATTN_Q_HEADS = 8
ATTN_KV_HEADS = 2
ATTN_GROUP = ATTN_Q_HEADS // ATTN_KV_HEADS
WINDOW = 128
ATTN_BLOCK = 128
ROPE_BASE = 10000.0
DN_HEADS = 8
CONV_W = 3
CHUNK = 64
N_DIR = 2
D_FF = 4 * D_MODEL
EPS = 1e-6
NEG_INF = -1e30
LOG2_E = float(np.log2(np.e))

ATTN_WIDTH = ATTN_Q_HEADS * HEAD_DIM
KV_WIDTH = ATTN_KV_HEADS * HEAD_DIM
DN_WIDTH = DN_HEADS * HEAD_DIM
LANES = 128
MXU_TILE = 256
HEADS_PER_TILE = MXU_TILE // HEAD_DIM
DN_SUB = 16
VMEM_LIMIT = 56 * 1024 * 1024

_OFF_AQ = 0
_OFF_AK = _OFF_AQ + ATTN_WIDTH
_OFF_AV = _OFF_AK + KV_WIDTH
_OFF_DQ = _OFF_AV + KV_WIDTH
_OFF_DK = _OFF_DQ + DN_WIDTH
_OFF_DV = _OFF_DK + DN_WIDTH
_OFF_DZ = _OFF_DV + DN_WIDTH
_OFF_DA = _OFF_DZ + DN_WIDTH
_OFF_DB = _OFF_DA + N_DIR * DN_HEADS
_OFF_GA = _OFF_DB + N_DIR * DN_HEADS
_OFF_GD = _OFF_GA + D_MODEL
_IN_WIDTH = _OFF_GD + D_MODEL


def _sigmoid(x):
    return 0.5 * jnp.tanh(0.5 * x) + 0.5


def _dot(a, b):
    return jnp.dot(a, b, preferred_element_type=F32)


def _dot_nt(a, b):
    return lax.dot_general(a, b, (((1,), (1,)), ((), ())), preferred_element_type=F32)


def _dot_tn(a, b):
    return lax.dot_general(a, b, (((0,), (0,)), ((), ())), preferred_element_type=F32)


def _cparams(n_axes):
    return pltpu.CompilerParams(dimension_semantics=("arbitrary",) * n_axes, vmem_limit_bytes=VMEM_LIMIT)


def _ada_kernel(c_ref, w_ref, b_ref, o_ref):
    c = c_ref[...]
    s = c * _sigmoid(c)
    o_ref[...] = _dot(s.astype(BF16), w_ref[...].astype(BF16)) + b_ref[...]


def _ada(cc, w_ada, b_ada):
    rows, d = cc.shape
    n = w_ada.shape[1]
    tn = 1536
    return pl.pallas_call(
        _ada_kernel,
        grid=(n // tn,),
        in_specs=[pl.BlockSpec((rows, d), lambda j: (0, 0)),
                  pl.BlockSpec((d, tn), lambda j: (0, j)),
                  pl.BlockSpec((1, tn), lambda j: (0, j))],
        out_specs=pl.BlockSpec((rows, tn), lambda j: (0, j)),
        out_shape=jax.ShapeDtypeStruct((rows, n), F32),
        compiler_params=_cparams(1),
        name="ada",
    )(cc, w_ada, b_ada.reshape(1, n))


def _inproj_kernel(x_ref, xprev_ref, xnext_ref, mod_ref, g_ref, wt_ref, wabt_ref, cw_ref, hsum_ref, *out_refs,
                   segs, tm, halo):
    w_refs = (wt_ref, wabt_ref)
    i = pl.program_id(1)
    last = pl.num_programs(1) - 1
    mod = mod_ref[0]
    scale = g_ref[...] * (1.0 + mod[1:2])

    def norm_mod(v):
        ms = jnp.mean(v * v, axis=-1, keepdims=True)
        return (v * lax.rsqrt(ms + EPS) * scale + mod[0:1]).astype(BF16)

    h = norm_mod(x_ref[0])
    h_ext = jnp.concatenate([h, norm_mod(jnp.concatenate([xprev_ref[0], xnext_ref[0]], axis=0))], axis=0)
    keep_prev = (i > 0).astype(F32)
    keep_next = (i < last).astype(F32)
    rows = lax.broadcasted_iota(jnp.int32, (tm, MXU_TILE), 0)
    hsum = hsum_ref[...]
    def plain_piece(o_ref, src, start, lo, width):
        o_ref[0, :, lo:lo + width] = _dot_nt(h, w_refs[src][start + lo:start + lo + width, :]).astype(o_ref.dtype)

    def conv_piece(o_ref, src, start, lo, kind):
        conv_col, do_norm, is_q = kind
        w_rows = w_refs[src][start + lo:start + lo + MXU_TILE, :]
        p_ext = _dot_nt(h_ext, w_rows)
        p, p_halo = p_ext[0:tm], p_ext[tm:tm + 2 * halo]
        yield
        p_prev = jnp.where(rows == 0, p_halo[halo - 1:halo] * keep_prev, pltpu.roll(p, 1, 0))
        p_next = jnp.where(rows == tm - 1, p_halo[halo:halo + 1] * keep_next, pltpu.roll(p, tm - 1, 0))
        cw = cw_ref[:, conv_col + lo:conv_col + lo + MXU_TILE]
        y = p_prev * cw[0:1] + p * cw[1:2] + p_next * cw[2:3]
        y = y * _sigmoid(y)
        if do_norm:
            y = y * lax.rsqrt(_dot((y * y).astype(BF16), hsum) + EPS)
            if is_q:
                y = y * (HEAD_DIM ** -0.5)
        o_ref[0, :, lo:lo + MXU_TILE] = y.astype(o_ref.dtype)

    plain, conv = [], []
    for o_ref, (src, start, size, kind) in zip(out_refs, segs):
        if kind is None:
            plain += [functools.partial(plain_piece, o_ref, src, start, lo, min(MXU_TILE, size - lo))
                      for lo in range(0, size, MXU_TILE)]
        else:
            conv += [functools.partial(conv_piece, o_ref, src, start, lo, kind) for lo in range(0, size, MXU_TILE)]
    pending = None
    while plain or conv or pending is not None:
        started = conv.pop(0)() if conv else None
        if started is not None:
            next(started)
        if pending is not None:
            for _ in pending:
                pass
        pending = started
        for _ in range(-(-len(plain) // (len(conv) + 1)) if plain else 0):
            plain.pop(0)()


def _inproj(x, mod3, mod_row, g_norm, w_t, wab_t, conv_w, hsum, segs, dtypes, tm):
    b, t, d = x.shape
    halo = 8
    r = tm // halo
    nblk = t // halo
    if mod_row is None:
        mod_map = lambda bi, i: (bi, 0, 0)
    else:
        mod_map = lambda bi, i: (mod_row, 0, 0)
    const2 = lambda bi, i: (0, 0)
    out_shape = [jax.ShapeDtypeStruct((b, t, size), dt) for (_, _, size, _), dt in zip(segs, dtypes)]
    out_specs = [pl.BlockSpec((1, tm, size), lambda bi, i: (bi, i, 0)) for (_, _, size, _) in segs]
    return pl.pallas_call(
        functools.partial(_inproj_kernel, segs=segs, tm=tm, halo=halo),
        grid=(b, t // tm),
        in_specs=[pl.BlockSpec((1, tm, d), lambda bi, i: (bi, i, 0)),
                  pl.BlockSpec((1, halo, d), lambda bi, i: (bi, jnp.maximum(i * r - 1, 0), 0)),
                  pl.BlockSpec((1, halo, d), lambda bi, i: (bi, jnp.minimum((i + 1) * r, nblk - 1), 0)),
                  pl.BlockSpec((1, 6, d), mod_map),
                  pl.BlockSpec((1, d), const2),
                  pl.BlockSpec(w_t.shape, const2),
                  pl.BlockSpec(wab_t.shape, const2),
                  pl.BlockSpec(conv_w.shape, const2),
                  pl.BlockSpec(hsum.shape, const2)],
        out_specs=out_specs,
        out_shape=out_shape,
        compiler_params=_cparams(2),
        name="inproj",
    )(x, x, x, mod3, g_norm.reshape(1, d), w_t, wab_t, conv_w, hsum)


def _rope(x, cos, sin, lane):
    swapped = jnp.where((lane % 32) < 16, pltpu.roll(x, LANES - 16, 1), pltpu.roll(x, 16, 1))
    return x * cos + swapped * sin


def _attn_kernel(a_ref, kvc_ref, cos_ref, sin_ref, qg_ref, kg_ref, sink_ref, havg_ref,
                 o_ref, k_s, vt_s, *, seq, ctx, qb):
    n = pl.program_id(1)
    havg = havg_ref[...]
    prep_rows = 256
    ctx_blocks = ctx // ATTN_BLOCK

    @pl.when(n == 0)
    def _prep():
        kg = kg_ref[...]
        kc = kvc_ref[0, :, 0:KV_WIDTH].astype(F32)
        ms = _dot((kc * kc).astype(BF16), havg)
        k_s[0:ctx, :] = (kc * lax.rsqrt(ms + EPS) * kg).astype(BF16)
        vc = kvc_ref[0, :, KV_WIDTH:2 * KV_WIDTH].astype(F32)
        for j in range(ctx_blocks):
            vt_s[j] = vc[j * ATTN_BLOCK:(j + 1) * ATTN_BLOCK, :].T.astype(BF16)
        lane = lax.broadcasted_iota(jnp.int32, (prep_rows, LANES), 1)
        for r0 in range(0, seq, prep_rows):
            kx = a_ref[0, r0:r0 + prep_rows, _OFF_AK:_OFF_AK + KV_WIDTH].astype(F32)
            ms = _dot((kx * kx).astype(BF16), havg)
            kx = kx * lax.rsqrt(ms + EPS) * kg
            kx = _rope(kx, cos_ref[r0:r0 + prep_rows, :], sin_ref[r0:r0 + prep_rows, :], lane)
            k_s[ctx + r0:ctx + r0 + prep_rows, :] = kx.astype(BF16)
            vx = a_ref[0, r0:r0 + prep_rows, _OFF_AV:_OFF_AV + KV_WIDTH].astype(F32)
            for j in range(prep_rows // ATTN_BLOCK):
                vt_s[ctx_blocks + r0 // ATTN_BLOCK + j] = vx[j * ATTN_BLOCK:(j + 1) * ATTN_BLOCK, :].T.astype(BF16)

    lane = lax.broadcasted_iota(jnp.int32, (ATTN_BLOCK, LANES), 1)
    qg = qg_ref[...]
    n_band = 3
    n_win = n_band * ATTN_BLOCK
    n_keys = n_win + ctx
    key_row = lax.broadcasted_iota(jnp.int32, (n_win, ATTN_BLOCK), 0)
    q_lane = lax.broadcasted_iota(jnp.int32, (n_win, ATTN_BLOCK), 1)
    zeros = jnp.zeros((HEAD_DIM, ATTN_BLOCK), BF16)
    groups = range(ATTN_KV_HEADS)
    sink_rows = [jnp.concatenate([sink_ref[g * ATTN_GROUP + r:g * ATTN_GROUP + r + 1, :] for r in range(ATTN_GROUP)],
                                 axis=1) * LOG2_E for g in groups]

    units = []
    k_alls, vt_alls, valids, rhss = [], [], [], {}
    for sb in range(qb):
        nq = n * qb + sb
        q0 = pl.multiple_of(nq * ATTN_BLOCK, ATTN_BLOCK)
        cos = cos_ref[pl.ds(q0, ATTN_BLOCK), :]
        sin = sin_ref[pl.ds(q0, ATTN_BLOCK), :]
        qts = []
        for j in range(ATTN_WIDTH // LANES):
            qx = a_ref[0, pl.ds(q0, ATTN_BLOCK), j * LANES:(j + 1) * LANES].astype(F32)
            ms = _dot((qx * qx).astype(BF16), havg)
            qx = _rope(qx * lax.rsqrt(ms + EPS) * qg, cos, sin, lane) * (HEAD_DIM ** -0.5 * LOG2_E)
            qts.append(qx.T.astype(BF16))
        blk0 = jnp.clip(nq - 1, 0, seq // ATTN_BLOCK - n_band)
        start = pl.multiple_of(blk0 * ATTN_BLOCK, ATTN_BLOCK)
        valid = jnp.abs(q0 + q_lane - (start + key_row)) <= WINDOW
        valids.append(jnp.concatenate([valid] * ATTN_GROUP, axis=1))
        k_alls.append(jnp.concatenate([k_s[pl.ds(ctx + start, n_win), :], k_s[0:ctx, :]], axis=0))
        vt_loc = vt_s[pl.ds(ctx_blocks + blk0, n_band)]
        vt_alls.append(jnp.concatenate([vt_loc[j] for j in range(n_band)] + [vt_s[j] for j in range(ctx_blocks)],
                                       axis=1))
        for g in groups:
            cols = []
            for r in range(ATTN_GROUP):
                h = g * ATTN_GROUP + r
                piece = qts[h // 2][(h % 2) * HEAD_DIM:(h % 2 + 1) * HEAD_DIM, :]
                cols.append(jnp.concatenate([piece, zeros] if g == 0 else [zeros, piece], axis=0))
            rhss[(sb, g)] = jnp.concatenate(cols, axis=1)
            units.append((sb, g))
        yield

    ss = []
    for sb, g in units:
        s = _dot(k_alls[sb], rhss[(sb, g)])
        ss.append(jnp.concatenate([jnp.where(valids[sb], s[0:n_win], NEG_INF), s[n_win:n_keys]], axis=0))
        yield
    p_bfs, invs = [], []
    for s, (sb, g) in zip(ss, units):
        m = jnp.maximum(jnp.max(s, axis=0, keepdims=True), sink_rows[g])
        p = jnp.exp2(s - m)
        invs.append(1.0 / (jnp.sum(p, axis=0, keepdims=True) + jnp.exp2(sink_rows[g] - m)))
        p_bfs.append(p.astype(BF16))
        yield
    pieces = {sb: [] for sb in range(qb)}
    for (sb, g), p_bf, inv in zip(units, p_bfs, invs):
        vt_g = vt_alls[sb][g * HEAD_DIM:(g + 1) * HEAD_DIM, :]
        for pair in range(ATTN_GROUP // 2):
            lanes = slice(pair * MXU_TILE, (pair + 1) * MXU_TILE)
            ot = _dot(vt_g, p_bf[:, lanes]) * inv[:, lanes]
            pieces[sb] += [ot[:, 0:ATTN_BLOCK], ot[:, ATTN_BLOCK:2 * ATTN_BLOCK]]
        yield
    for sb in range(qb):
        outs = [jnp.concatenate([pieces[sb][2 * j], pieces[sb][2 * j + 1]], axis=0).T
                for j in range(ATTN_Q_HEADS // 2)]
        o_ref[0, sb * ATTN_BLOCK:(sb + 1) * ATTN_BLOCK, :] = jnp.concatenate(outs, axis=1).astype(o_ref.dtype)


def _run_parts(parts, b, name):
    steps = parts[0]["steps"]
    assert all(p["steps"] == steps for p in parts)
    counts = [(len(p["args"]), len(p["out_shape"]), len(p["scratch_shapes"])) for p in parts]
    n_in = sum(c[0] for c in counts)
    n_out = sum(c[1] for c in counts)

    def kern(*refs):
        i_pos, o_pos, s_pos = 0, n_in, n_in + n_out
        bodies = []
        for p, (ci, co, cs) in zip(parts, counts):
            bodies.append(p["kernel"](*refs[i_pos:i_pos + ci], *refs[o_pos:o_pos + co], *refs[s_pos:s_pos + cs]))
            i_pos, o_pos, s_pos = i_pos + ci, o_pos + co, s_pos + cs
        while bodies:
            for body in list(bodies):
                try:
                    next(body)
                except StopIteration:
                    bodies.remove(body)

    outs = pl.pallas_call(
        kern,
        grid=(b, steps),
        in_specs=[s for p in parts for s in p["in_specs"]],
        out_specs=[s for p in parts for s in p["out_specs"]],
        out_shape=[s for p in parts for s in p["out_shape"]],
        scratch_shapes=[s for p in parts for s in p["scratch_shapes"]],
        compiler_params=_cparams(2),
        name=name,
    )(*[a for p in parts for a in p["args"]])
    res, pos = [], 0
    for _, co, _ in counts:
        res.append(outs[pos:pos + co])
        pos += co
    return res


def _attn_parts(a_lat, kv_ctx, cos, sin, qg, kg, sink, havg, qb):
    b, s, wa = a_lat.shape
    ctx = kv_ctx.shape[1]
    nb = s // ATTN_BLOCK
    const2 = lambda bi, n: (0, 0)
    return dict(
        kernel=functools.partial(_attn_kernel, seq=s, ctx=ctx, qb=qb),
        steps=nb // qb,
        in_specs=[pl.BlockSpec((1, s, wa), lambda bi, n: (bi, 0, 0)),
                  pl.BlockSpec((1, ctx, 2 * KV_WIDTH), lambda bi, n: (bi, 0, 0)),
                  pl.BlockSpec((s, LANES), const2),
                  pl.BlockSpec((s, LANES), const2),
                  pl.BlockSpec((1, LANES), const2),
                  pl.BlockSpec((1, LANES), const2),
                  pl.BlockSpec((ATTN_Q_HEADS, LANES), const2),
                  pl.BlockSpec((LANES, LANES), const2)],
        out_specs=[pl.BlockSpec((1, qb * ATTN_BLOCK, ATTN_WIDTH), lambda bi, n: (bi, n, 0))],
        out_shape=[jax.ShapeDtypeStruct((b, s, ATTN_WIDTH), BF16)],
        scratch_shapes=[pltpu.VMEM((ctx + s, KV_WIDTH), BF16),
                        pltpu.VMEM(((ctx + s) // ATTN_BLOCK, KV_WIDTH, ATTN_BLOCK), BF16)],
        args=[a_lat, kv_ctx, cos, sin, qg, kg, sink, havg])


def _half_mask():
    return lax.broadcasted_iota(jnp.int32, (CHUNK, LANES), 1) < HEAD_DIM


def _bdiag(x_lane, lo_half):
    zeros = jnp.zeros((CHUNK, LANES), x_lane.dtype)
    blocks = []
    for h in range(HEADS_PER_TILE):
        col = (h * HEAD_DIM) // LANES
        piece = x_lane[:, col * LANES:(col + 1) * LANES]
        piece = jnp.where(lo_half if (h * HEAD_DIM) % LANES == 0 else jnp.logical_not(lo_half), piece, zeros)
        blocks.append(jnp.concatenate([piece if c == col else zeros for c in range(MXU_TILE // LANES)], axis=1))
    return jnp.concatenate(blocks, axis=0)


def _unit_tri_inverses(a_list, eye_l, sub_mask, lo_half):
    ads = [jnp.where(sub_mask, a, 0.0) for a in a_list]
    ys = [jnp.where(sub_mask, 0.0, a) for a in a_list]
    ps = [eye_l - ad for ad in ads]
    pws = [ad.astype(BF16) for ad in ads]
    n_levels = int(np.log2(DN_SUB))
    c1, c2, c3 = CHUNK, 2 * CHUNK, 3 * CHUNK
    for level in range(n_levels):
        rhss = [_bdiag(pw, lo_half) for pw in pws]
        if level == 0:
            boths = [_dot(jnp.concatenate([pw, y.astype(BF16)], axis=0), rhs) for pw, y, rhs in zip(pws, ys, rhss)]
            pws = [both[0:c1].astype(BF16) for both in boths]
            ys = [y - both[c1:c2] for y, both in zip(ys, boths)]
        elif level < n_levels - 1:
            boths = [_dot(jnp.concatenate([pw, p.astype(BF16), y.astype(BF16)], axis=0), rhs)
                     for pw, p, y, rhs in zip(pws, ps, ys, rhss)]
            pws = [both[0:c1].astype(BF16) for both in boths]
            ps = [p + both[c1:c2] for p, both in zip(ps, boths)]
            ys = [y + both[c2:c3] for y, both in zip(ys, boths)]
        else:
            boths = [_dot(jnp.concatenate([p.astype(BF16), y.astype(BF16)], axis=0), rhs)
                     for p, y, rhs in zip(ps, ys, rhss)]
            ps = [p + both[0:c1] for p, both in zip(ps, boths)]
            ys = [y + both[c1:c2] for y, both in zip(ys, boths)]
        yield
    assert CHUNK // DN_SUB == 4
    b_bfs = [y.astype(BF16) for y in ys]
    boths = [_dot(jnp.concatenate([b, p.astype(BF16)], axis=0), _bdiag(b, lo_half)) for b, p in zip(b_bfs, ps)]
    yield
    zs = [p - both[c1:c2] for p, both in zip(ps, boths)]
    return [z + _dot(z.astype(BF16), _bdiag(both[0:c1].astype(BF16), lo_half)) for z, both in zip(zs, boths)]


def _dn_factors(abs_, d, arow, dtrow, exp2_ref, tri_ref, eye_t):
    def hi_lo(v):
        hi = v.astype(BF16)
        return hi, (v - hi.astype(F32)).astype(BF16)

    lane = lax.broadcasted_iota(jnp.int32, abs_[0].shape, 1)
    is_g = (lane % 32) < 16
    xhls = []
    for ab in abs_:
        z = ab + dtrow
        softplus = jnp.maximum(z, 0.0) + jnp.log(1.0 + jnp.exp(-jnp.abs(z)))
        x = jnp.where(is_g, -arow * softplus, _sigmoid(ab))
        x = jnp.where(is_g, _dot(tri_ref[d], jnp.concatenate(hi_lo(x), axis=0)), x)
        x_hi, x_lo = hi_lo(x)
        xhls.append(jnp.where(lane < 32, x_hi, x_lo)[:, 0:64])
    y = _dot(jnp.concatenate(xhls, axis=0), exp2_ref[d])
    out = []
    for c in range(len(abs_)):
        gi = y[c * CHUNK:(c + 1) * CHUNK, 0:DN_WIDTH]
        be = y[c * CHUNK:(c + 1) * CHUNK, DN_WIDTH:2 * DN_WIDTH]
        gj = jnp.sum(gi * eye_t, axis=0, keepdims=True)
        out.append((be, gi, jnp.broadcast_to(gj, gi.shape)))
    return out


def _block_mask():
    return (lax.broadcasted_iota(jnp.int32, (MXU_TILE, MXU_TILE), 0) // HEAD_DIM
            == lax.broadcasted_iota(jnp.int32, (MXU_TILE, MXU_TILE), 1) // HEAD_DIM)


def _dnprep_kernel(*refs, cb, want_o):
    if want_o:
        q_ref, k_ref, v_ref, ab_ref = refs[:4]
        refs = refs[4:]
    else:
        q_ref = None
        k_ref, v_ref, ab_ref = refs[:3]
        refs = refs[3:]
    arow_ref, dtrow_ref, exp2_ref, tri_ref, w_o, uv_o, kt_o, dl_o = refs[:8]
    qd_o, in_o = refs[8:10] if want_o else (None, None)
    n_tiles = DN_WIDTH // MXU_TILE
    row = lax.broadcasted_iota(jnp.int32, (CHUNK, MXU_TILE), 0)
    colj = lax.broadcasted_iota(jnp.int32, (CHUNK, MXU_TILE), 1) % HEAD_DIM
    eye_l = (row == colj).astype(F32)
    eye_t = jnp.concatenate([eye_l] * n_tiles, axis=1)
    lo_half = _half_mask()
    arow = arow_ref[...]
    dtrow = dtrow_ref[...]

    fac = {}
    shared = {}
    abs_ = [ab_ref[0, c * CHUNK:(c + 1) * CHUNK, :] for c in range(cb)]
    for d in range(N_DIR):
        for c, f in enumerate(_dn_factors(abs_, d, arow, dtrow, exp2_ref, tri_ref, eye_t)):
            fac[(c, d)] = f
    yield
    for c in range(cb):
        rows = slice(c * CHUNK, (c + 1) * CHUNK)
        for g in range(n_tiles):
            lanes = slice(g * MXU_TILE, (g + 1) * MXU_TILE)
            k_l = k_ref[0, rows, lanes]
            kbd = _bdiag(k_l, lo_half)
            if want_o:
                kq = _dot_nt(jnp.concatenate([k_l, q_ref[0, rows, lanes]], axis=0), kbd)
                shared[(c, g)] = (kq[0:CHUNK], kq[CHUNK:2 * CHUNK])
            else:
                shared[(c, g)] = (_dot_nt(k_l, kbd), None)
        yield

    units = [(c, g, d) for c in range(cb) for g in range(n_tiles) for d in range(N_DIR)]
    decs = []
    for c, g, d in units:
        lanes = slice(g * MXU_TILE, (g + 1) * MXU_TILE)
        be, gi, gj = fac[(c, d)]
        lower = (row > colj) if d == 0 else (row < colj)
        decs.append(jnp.where(lower, jnp.exp(jnp.where(lower, gi[:, lanes] - gj[:, lanes], 0.0)), 0.0))
    a_list = [fac[(c, d)][0][:, g * MXU_TILE:(g + 1) * MXU_TILE] * dec * shared[(c, g)][0]
              for (c, g, d), dec in zip(units, decs)]
    yield
    tinvs = yield from _unit_tri_inverses(a_list, eye_l, (row // DN_SUB) == (colj // DN_SUB), lo_half)

    for idx, ((c, g, d), dec, tinv) in enumerate(zip(units, decs, tinvs)):
        if idx % (n_tiles * N_DIR) == 0:
            yield
        rows = slice(c * CHUNK, (c + 1) * CHUNK)
        lanes = slice(g * MXU_TILE, (g + 1) * MXU_TILE)
        be, gi, _ = fac[(c, d)]
        be, gi = be[:, lanes], gi[:, lanes]
        last = CHUNK - 1 if d == 0 else 0
        e_g = jnp.exp(gi)
        gl_row = gi[last:last + 1, :]
        kf = k_ref[0, rows, lanes].astype(F32)
        vf = v_ref[0, rows, lanes].astype(F32)
        rhs = jnp.concatenate([_bdiag((be * e_g * kf).astype(BF16), lo_half), _bdiag((be * vf).astype(BF16), lo_half)],
                              axis=1)
        wu = _dot(tinv.astype(BF16), rhs)
        w_o[0, d, rows, lanes] = wu[:, 0:MXU_TILE].astype(BF16)
        uv_o[0, d, rows, lanes] = wu[:, MXU_TILE:2 * MXU_TILE].astype(BF16)
        kt_o[0, d, rows, lanes] = (jnp.exp(gl_row - gi) * kf).astype(BF16)
        dl_o[0, d, c, :, lanes] = jnp.exp(gl_row)
        if want_o:
            qd_o[0, d, rows, lanes] = (e_g * q_ref[0, rows, lanes].astype(F32)).astype(BF16)
            in_o[0, d, rows, lanes] = ((dec + eye_l) * shared[(c, g)][1]).astype(BF16)


def _dnprep_parts(q, k, v, ab, arow, dtrow, exp2, tri, cb):
    want_o = q is not None
    b, t, w = k.shape
    tb = cb * CHUNK
    tok = lambda bi, i: (bi, i, 0)
    const2 = lambda bi, i: (0, 0)
    const3 = lambda bi, i: (0, 0, 0)
    dir_tok = lambda bi, i: (bi, 0, i, 0)
    data = ([q] if want_o else []) + [k, v]
    in_specs = ([pl.BlockSpec((1, tb, w), tok)] * len(data) + [pl.BlockSpec((1, tb, LANES), tok),
                pl.BlockSpec((1, LANES), const2), pl.BlockSpec((1, LANES), const2),
                pl.BlockSpec(exp2.shape, const3), pl.BlockSpec(tri.shape, const3)])
    big = lambda dt: jax.ShapeDtypeStruct((b, N_DIR, t, w), dt)
    big_spec = pl.BlockSpec((1, N_DIR, tb, w), dir_tok)
    out_shape = [big(BF16), big(BF16), big(BF16), jax.ShapeDtypeStruct((b, N_DIR, t // CHUNK, 1, w), F32)]
    out_specs = [big_spec, big_spec, big_spec, pl.BlockSpec((1, N_DIR, cb, 1, w), lambda bi, i: (bi, 0, i, 0, 0))]
    if want_o:
        out_shape += [big(BF16), big(BF16)]
        out_specs += [big_spec, big_spec]
    return dict(kernel=functools.partial(_dnprep_kernel, cb=cb, want_o=want_o), steps=t // tb, in_specs=in_specs,
                out_specs=out_specs, out_shape=out_shape, scratch_shapes=[],
                args=data + [ab, arow, dtrow, exp2, tri])


def _dnscan_kernel(*refs, n_chunk, bb, want_o, have_s0, want_s):
    n_in = 6 if want_o else 4
    dir_refs = [refs[0:n_in], refs[n_in:2 * n_in]]
    pos = 2 * n_in
    s0_ref = refs[pos] if have_s0 else None
    pos += int(have_s0)
    o_refs = refs[pos:pos + N_DIR] if want_o else None
    pos += N_DIR if want_o else 0
    sout_ref = refs[pos] if want_s else None
    pos += int(want_s)
    s_scr = refs[pos]
    n_tiles = DN_WIDTH // MXU_TILE
    i = pl.program_id(1)

    @pl.when(i == 0)
    def _init():
        if have_s0:
            s_scr[...] = s0_ref[...]
        else:
            s_scr[...] = jnp.zeros_like(s_scr)

    bmask = _block_mask()
    lo_half = _half_mask()
    chains = [(bi, d, g) for bi in range(bb) for d in range(N_DIR) for g in range(n_tiles)]

    def body(j, carry):
        cidx = (j, n_chunk - 1 - j)
        r0s = [pl.multiple_of(cidx[d] * CHUNK, CHUNK) for d in range(N_DIR)]
        s_olds, r1s = [], []
        for bi, d, g in chains:
            lanes = slice(g * MXU_TILE, (g + 1) * MXU_TILE)
            w = dir_refs[d][0][bi, 0, pl.ds(r0s[d], CHUNK), lanes]
            if want_o:
                w = jnp.concatenate([w, dir_refs[d][4][bi, 0, pl.ds(r0s[d], CHUNK), lanes]], axis=0)
            s_old = s_scr[bi, d * n_tiles + g]
            s_olds.append(s_old)
            r1s.append(_dot(w, s_old.astype(BF16)))
        u_bfs = []
        for (bi, d, g), r1 in zip(chains, r1s):
            lanes = slice(g * MXU_TILE, (g + 1) * MXU_TILE)
            u_bfs.append((dir_refs[d][1][bi, 0, pl.ds(r0s[d], CHUNK), lanes] - r1[0:CHUNK]).astype(BF16))
        for (bi, d, g), r1, u_bf, s_old in zip(chains, r1s, u_bfs, s_olds):
            lanes = slice(g * MXU_TILE, (g + 1) * MXU_TILE)
            kt = dir_refs[d][2][bi, 0, pl.ds(r0s[d], CHUNK), lanes]
            ds = jnp.where(bmask, _dot_tn(kt, u_bf), 0.0)
            dl = dir_refs[d][3][bi, 0, cidx[d]][:, lanes]
            s_scr[bi, d * n_tiles + g] = s_old * dl + ds
            if want_o:
                intra = dir_refs[d][5][bi, 0, pl.ds(r0s[d], CHUNK), lanes]
                o = r1[CHUNK:2 * CHUNK] + _dot(intra, _bdiag(u_bf, lo_half))
                o_refs[d][bi, pl.ds(r0s[d], CHUNK), lanes] = o.astype(BF16)
        return carry

    lax.fori_loop(0, n_chunk, body, 0)

    if want_s:
        @pl.when(i == pl.num_programs(1) - 1)
        def _fin():
            sout_ref[...] = s_scr[...]


def _dnscan(prep, s0, tb, bb, want_s):
    want_o = len(prep) == 6
    b, _, t, w = prep[0].shape
    n_t = t // tb
    n_chunk = tb // CHUNK
    n_chain = N_DIR * (w // MXU_TILE)

    def specs(d):
        blk = (lambda bi, i: i) if d == 0 else (lambda bi, i: n_t - 1 - i)
        big = pl.BlockSpec((bb, 1, tb, w), lambda bi, i: (bi, d, blk(bi, i), 0))
        dl = pl.BlockSpec((bb, 1, n_chunk, 1, w), lambda bi, i: (bi, d, blk(bi, i), 0, 0))
        return [big, big, big, dl] + ([big, big] if want_o else [])

    in_specs = specs(0) + specs(1)
    args = list(prep) + list(prep)
    state_spec = pl.BlockSpec((bb, n_chain, MXU_TILE, MXU_TILE), lambda bi, i: (bi, 0, 0, 0))
    if s0 is not None:
        in_specs.append(state_spec)
        args.append(s0)
    out_shape, out_specs = [], []
    if want_o:
        out_shape += [jax.ShapeDtypeStruct((b, t, w), BF16)] * N_DIR
        out_specs += [pl.BlockSpec((bb, tb, w), lambda bi, i: (bi, i, 0)),
                      pl.BlockSpec((bb, tb, w), lambda bi, i: (bi, n_t - 1 - i, 0))]
    if want_s:
        out_shape.append(jax.ShapeDtypeStruct((b, n_chain, MXU_TILE, MXU_TILE), F32))
        out_specs.append(state_spec)
    return pl.pallas_call(
        functools.partial(_dnscan_kernel, n_chunk=n_chunk, bb=bb, want_o=want_o, have_s0=s0 is not None,
                          want_s=want_s),
        grid=(b // bb, n_t),
        in_specs=in_specs,
        out_specs=out_specs,
        out_shape=out_shape,
        scratch_shapes=[pltpu.VMEM((bb, n_chain, MXU_TILE, MXU_TILE), F32)],
        compiler_params=_cparams(2),
        name="dnscan",
    )(*args)


def _tail_kernel(x_ref, mod_ref, ya_ref, odf_ref, odb_ref, z_ref, gate_ref, dng_ref, havg_ref, wba_ref, wbd_ref,
                 wo_ref, gn2_ref, w1_ref, w2_ref, o_ref, *, ff_chunk):
    havg = havg_ref[...]
    dng = dng_ref[...]
    yd_parts = []
    for j in range(DN_WIDTH // MXU_TILE):
        sl = slice(j * MXU_TILE, (j + 1) * MXU_TILE)
        od = odf_ref[0, :, sl].astype(F32) + odb_ref[0, :, sl].astype(F32)
        ms = _dot((od * od).astype(BF16), havg)
        z = z_ref[0, :, sl].astype(F32)
        yd_parts.append((od * lax.rsqrt(ms + EPS) * dng * (z * _sigmoid(z))).astype(BF16))
    yd = jnp.concatenate(yd_parts, axis=1)
    ga = gate_ref[0, :, 0:D_MODEL].astype(F32)
    gd = gate_ref[0, :, D_MODEL:2 * D_MODEL].astype(F32)
    y = _sigmoid(ga) * _dot(ya_ref[0], wba_ref[...]) + _sigmoid(gd) * _dot(yd, wbd_ref[...])
    br = _dot(y.astype(BF16), wo_ref[...])
    mod = mod_ref[0]
    out1 = x_ref[0] + mod[2:3] * br
    ms2 = jnp.mean(out1 * out1, axis=-1, keepdims=True)
    hm = (out1 * lax.rsqrt(ms2 + EPS) * (gn2_ref[...] * (1.0 + mod[4:5])) + mod[3:4]).astype(BF16)
    acc = None
    for j in range(D_FF // ff_chunk):
        a = jnp.maximum(_dot(hm, w1_ref[:, j * ff_chunk:(j + 1) * ff_chunk]), 0.0)
        part = _dot((a * a).astype(BF16), w2_ref[j * ff_chunk:(j + 1) * ff_chunk, :])
        acc = part if acc is None else acc + part
    o_ref[0] = out1 + mod[5:6] * acc


def _resident(shape):
    return pl.BlockSpec(shape, lambda bi, i: (0,) * len(shape), pipeline_mode=pl.Buffered(1))


def _tail(x, mod3, y_attn, o_df, o_db, z, gates, dng, havg, wba, wbd, wo, gn2, w1, w2, tm):
    b, t, d = x.shape
    tok = lambda bi, i: (bi, i, 0)
    return pl.pallas_call(
        functools.partial(_tail_kernel, ff_chunk=1024),
        grid=(b, t // tm),
        in_specs=[pl.BlockSpec((1, tm, d), tok),
                  pl.BlockSpec((1, 6, d), lambda bi, i: (bi, 0, 0)),
                  pl.BlockSpec((1, tm, ATTN_WIDTH), tok),
                  pl.BlockSpec((1, tm, DN_WIDTH), tok),
                  pl.BlockSpec((1, tm, DN_WIDTH), tok),
                  pl.BlockSpec((1, tm, DN_WIDTH), tok),
                  pl.BlockSpec((1, tm, 2 * d), tok),
                  _resident((1, MXU_TILE)),
                  _resident((MXU_TILE, MXU_TILE)),
                  _resident(wba.shape),
                  _resident(wbd.shape),
                  _resident(wo.shape),
                  _resident((1, d)),
                  _resident(w1.shape),
                  _resident(w2.shape)],
        out_specs=pl.BlockSpec((1, tm, d), tok),
        out_shape=jax.ShapeDtypeStruct((b, t, d), F32),
        compiler_params=_cparams(2),
        name="tail",
    )(x, mod3, y_attn, o_df, o_db, z, gates, dng, havg, wba, wbd, wo, gn2, w1, w2)


def _head_avg(n, scale):
    idx = np.arange(n) // HEAD_DIM
    return jnp.asarray((idx[:, None] == idx[None, :]).astype(np.float32) * scale, BF16)


def _dn_expand_matrix():
    n = N_DIR * DN_HEADS
    m = np.zeros((N_DIR, 4 * n, 2 * DN_WIDTH), np.float32)
    for d in range(N_DIR):
        for part in range(2):
            for h in range(DN_HEADS):
                idx = d * DN_HEADS + h
                m[d, part * 2 * n + idx, h * HEAD_DIM:(h + 1) * HEAD_DIM] = 1.0
                m[d, part * 2 * n + n + idx, DN_WIDTH + h * HEAD_DIM:DN_WIDTH + (h + 1) * HEAD_DIM] = 1.0
    return jnp.asarray(m, BF16)


def _tri_matrices():
    i = np.arange(CHUNK)
    low = (i[:, None] >= i[None, :]).astype(np.float32)
    up = (i[:, None] <= i[None, :]).astype(np.float32)
    return jnp.asarray(np.stack([np.concatenate([low, low], axis=1), np.concatenate([up, up], axis=1)]), BF16)


def _rope_tables(seq):
    half = HEAD_DIM // 2
    n_freq = half // 2
    freqs = ROPE_BASE ** (-jnp.arange(n_freq, dtype=F32) / n_freq)
    pos = jnp.arange(seq)
    ang_r = (pos // GRID_W).astype(F32)[:, None] * freqs
    ang_c = (pos % GRID_W).astype(F32)[:, None] * freqs
    cos = jnp.concatenate([jnp.cos(ang_r)] * 2 + [jnp.cos(ang_c)] * 2, axis=1)
    sin = jnp.concatenate([-jnp.sin(ang_r), jnp.sin(ang_r), -jnp.sin(ang_c), jnp.sin(ang_c)], axis=1)
    reps = LANES // HEAD_DIM
    return jnp.tile(cos, (1, reps)), jnp.tile(sin, (1, reps))


def _pad_cols(w, n):
    return jnp.pad(w, ((0, 0), (0, n - w.shape[1])))


def kernel(x, c, ctx, c_ctx, w_ada, b_ada, g_norm1, w_in, q_norm_g, k_norm_g, attn_sink, conv_w, a_log, dt_bias,
           dn_norm_g, w_br_attn, w_br_dn, w_out, g_norm2, w_mlp1, w_mlp2):
    depth = w_ada.shape[0]
    assert depth == 1, "single-layer trunk only"
    b, s, d = x.shape
    n_ctx = ctx.shape[1]
    assert d == D_MODEL and w_in.shape[-1] == _IN_WIDTH
    assert s >= 3 * ATTN_BLOCK and s % ATTN_BLOCK == 0 and s % CHUNK == 0 and n_ctx % CHUNK == 0
    out_dtype = x.dtype
    w_in_t = jnp.swapaxes(w_in[0], 0, 1).astype(BF16)

    mod_rows = 16
    cc = jnp.concatenate([c.astype(F32), c_ctx.astype(F32)[None], jnp.zeros((mod_rows - b - 1, d), F32)], axis=0)
    mod = _ada(cc, w_ada[0], b_ada[0])
    mod3 = mod.reshape(mod_rows, 6, d)

    ab_rows = w_in_t[_OFF_DA:_OFF_GA]
    wab_t = jnp.concatenate([ab_rows, ab_rows, jnp.zeros((LANES - 2 * (_OFF_GA - _OFF_DA), d), BF16)], axis=0)
    hsum = _head_avg(MXU_TILE, 1.0)
    segs_lat = ((0, 0, _OFF_DQ, None),
                (0, _OFF_DQ, DN_WIDTH, (0, True, True)), (0, _OFF_DK, DN_WIDTH, (DN_WIDTH, True, False)),
                (0, _OFF_DV, DN_WIDTH, (2 * DN_WIDTH, False, False)),
                (0, _OFF_DZ, DN_WIDTH, None), (0, _OFF_GA, 2 * D_MODEL, None), (1, 0, LANES, None))
    a_lat, q_d, k_d, v_d, z_lat, gates, ab_lat = _inproj(x, mod3, None, g_norm1[0], w_in_t, wab_t, conv_w[0], hsum,
                                                         segs_lat, (BF16, BF16, BF16, BF16, BF16, BF16, F32), tm=512)
    segs_ctx = ((0, _OFF_AK, 2 * KV_WIDTH, None),
                (0, _OFF_DK, DN_WIDTH, (0, True, False)), (0, _OFF_DV, DN_WIDTH, (DN_WIDTH, False, False)),
                (1, 0, LANES, None))
    kv_ctx, k_dc, v_dc, ab_ctx = _inproj(ctx, mod3, b, g_norm1[0], w_in_t, wab_t, conv_w[0][:, DN_WIDTH:], hsum,
                                         segs_ctx, (BF16, BF16, BF16, F32), tm=n_ctx)

    cos, sin = _rope_tables(s)
    reps = LANES // HEAD_DIM
    n_gate = N_DIR * DN_HEADS
    arow = _pad_cols(jnp.tile(jnp.concatenate([jnp.exp(a_log[0]).reshape(1, n_gate), jnp.zeros((1, n_gate), F32)],
                                              axis=1), (1, 2)), LANES)
    dtrow = _pad_cols(jnp.tile(jnp.concatenate([dt_bias[0].reshape(1, n_gate), jnp.zeros((1, n_gate), F32)],
                                               axis=1), (1, 2)), LANES)
    exp2, tri = _dn_expand_matrix(), _tri_matrices()
    dn_cb = 4
    qb = (s // ATTN_BLOCK) // (s // (dn_cb * CHUNK))
    attn_parts = _attn_parts(a_lat, kv_ctx, cos, sin,
                             jnp.tile(q_norm_g[0].astype(F32), reps)[None],
                             jnp.tile(k_norm_g[0].astype(F32), reps)[None],
                             jnp.broadcast_to(attn_sink[0].astype(F32)[:, None], (ATTN_Q_HEADS, LANES)),
                             _head_avg(LANES, 1.0 / HEAD_DIM), qb)
    (y_attn,), prep_lat = _run_parts(
        [attn_parts, _dnprep_parts(q_d, k_d, v_d, ab_lat, arow, dtrow, exp2, tri, dn_cb)], b, "attn_dnprep")
    (prep_ctx,) = _run_parts([_dnprep_parts(None, k_dc, v_dc, ab_ctx, arow, dtrow, exp2, tri, dn_cb)], b, "dnprep")

    bb = max(q for q in (4, 2, 1) if b % q == 0)
    (s_ctx,) = _dnscan(prep_ctx, None, tb=n_ctx, bb=bb, want_s=True)
    o_df, o_db = _dnscan(prep_lat, s_ctx, tb=256, bb=bb, want_s=False)

    out = _tail(x, mod3, y_attn, o_df, o_db, z_lat, gates,
                jnp.tile(dn_norm_g[0].astype(F32), HEADS_PER_TILE)[None], _head_avg(MXU_TILE, 1.0 / HEAD_DIM),
                w_br_attn[0].astype(BF16), w_br_dn[0].astype(BF16), w_out[0].astype(BF16),
                g_norm2[0].reshape(1, d), w_mlp1[0].astype(BF16), w_mlp2[0].astype(BF16), tm=512)
    return out.astype(out_dtype)
```

```python
import functools

import numpy as np
import jax
import jax.numpy as jnp
from jax import lax
from jax.experimental import pallas as pl
from jax.experimental.pallas import tpu as pltpu

F32 = jnp.float32
BF16 = jnp.bfloat16

D_MODEL = 1024
GRID_W = 64
HEAD_DIM = 64
ATTN_Q_HEADS = 8
ATTN_KV_HEADS = 2
ATTN_GROUP = ATTN_Q_HEADS // ATTN_KV_HEADS
WINDOW = 128
ATTN_BLOCK = 128
ROPE_BASE = 10000.0
DN_HEADS = 8
CONV_W = 3
CHUNK = 64
N_DIR = 2
D_FF = 4 * D_MODEL
EPS = 1e-6
NEG_INF = -1e30
LOG2_E = float(np.log2(np.e))

ATTN_WIDTH = ATTN_Q_HEADS * HEAD_DIM
KV_WIDTH = ATTN_KV_HEADS * HEAD_DIM
DN_WIDTH = DN_HEADS * HEAD_DIM
LANES = 128
MXU_TILE = 256
HEADS_PER_TILE = MXU_TILE // HEAD_DIM
DN_SUB = 16
VMEM_LIMIT = 56 * 1024 * 1024

TOKEN_TILE = 512
ADA_COL_TILE = 1536
MLP_FF_CHUNK = 1024
ATTN_PREP_ROWS = 256
DN_CHUNKS_PER_STEP = 4
SCAN_TOKEN_TILE = 256
SCAN_BATCH_ROWS = 4

_OFF_AQ = 0
_OFF_AK = _OFF_AQ + ATTN_WIDTH
_OFF_AV = _OFF_AK + KV_WIDTH
_OFF_DQ = _OFF_AV + KV_WIDTH
_OFF_DK = _OFF_DQ + DN_WIDTH
_OFF_DV = _OFF_DK + DN_WIDTH
_OFF_DZ = _OFF_DV + DN_WIDTH
_OFF_DA = _OFF_DZ + DN_WIDTH
_OFF_DB = _OFF_DA + N_DIR * DN_HEADS
_OFF_GA = _OFF_DB + N_DIR * DN_HEADS
_OFF_GD = _OFF_GA + D_MODEL
_IN_WIDTH = _OFF_GD + D_MODEL


def _sigmoid(x):
    return 0.5 * jnp.tanh(0.5 * x) + 0.5


def _dot(a, b):
    return jnp.dot(a, b, preferred_element_type=F32)


def _dot_nt(a, b):
    return lax.dot_general(a, b, (((1,), (1,)), ((), ())), preferred_element_type=F32)


def _dot_tn(a, b):
    return lax.dot_general(a, b, (((0,), (0,)), ((), ())), preferred_element_type=F32)


def _cparams(n_axes):
    return pltpu.CompilerParams(dimension_semantics=("arbitrary",) * n_axes, vmem_limit_bytes=VMEM_LIMIT)


def _ada_kernel(c_ref, w_ref, b_ref, o_ref):
    c = c_ref[...]
    s = c * _sigmoid(c)
    o_ref[...] = _dot(s.astype(BF16), w_ref[...].astype(BF16)) + b_ref[...]


def _ada(cc, w_ada, b_ada):
    rows, d = cc.shape
    n = w_ada.shape[1]
    tn = ADA_COL_TILE
    return pl.pallas_call(
        _ada_kernel,
        grid=(n // tn,),
        in_specs=[pl.BlockSpec((rows, d), lambda j: (0, 0)),
                  pl.BlockSpec((d, tn), lambda j: (0, j)),
                  pl.BlockSpec((1, tn), lambda j: (0, j))],
        out_specs=pl.BlockSpec((rows, tn), lambda j: (0, j)),
        out_shape=jax.ShapeDtypeStruct((rows, n), F32),
        compiler_params=_cparams(1),
        name="ada",
    )(cc, w_ada, b_ada.reshape(1, n))


def _inproj_kernel(x_ref, xprev_ref, xnext_ref, mod_ref, g_ref, wt_ref, wabt_ref, cw_ref, hsum_ref, *out_refs,
                   segs, tm, halo):
    w_refs = (wt_ref, wabt_ref)
    i = pl.program_id(1)
    last = pl.num_programs(1) - 1
    mod = mod_ref[0]
    scale = g_ref[...] * (1.0 + mod[1:2])

    def norm_mod(v):
        ms = jnp.mean(v * v, axis=-1, keepdims=True)
        return (v * lax.rsqrt(ms + EPS) * scale + mod[0:1]).astype(BF16)

    h = norm_mod(x_ref[0])
    h_ext = jnp.concatenate([h, norm_mod(jnp.concatenate([xprev_ref[0], xnext_ref[0]], axis=0))], axis=0)
    keep_prev = (i > 0).astype(F32)
    keep_next = (i < last).astype(F32)
    rows = lax.broadcasted_iota(jnp.int32, (tm, MXU_TILE), 0)
    hsum = hsum_ref[...]
    def plain_piece(o_ref, src, start, lo, width):
        o_ref[0, :, lo:lo + width] = _dot_nt(h, w_refs[src][start + lo:start + lo + width, :]).astype(o_ref.dtype)

    def conv_piece(o_ref, src, start, lo, kind):
        conv_col, do_norm, is_q = kind
        w_rows = w_refs[src][start + lo:start + lo + MXU_TILE, :]
        p_ext = _dot_nt(h_ext, w_rows)
        p, p_halo = p_ext[0:tm], p_ext[tm:tm + 2 * halo]
        yield
        p_prev = jnp.where(rows == 0, p_halo[halo - 1:halo] * keep_prev, pltpu.roll(p, 1, 0))
        p_next = jnp.where(rows == tm - 1, p_halo[halo:halo + 1] * keep_next, pltpu.roll(p, tm - 1, 0))
        cw = cw_ref[:, conv_col + lo:conv_col + lo + MXU_TILE]
        y = p_prev * cw[0:1] + p * cw[1:2] + p_next * cw[2:3]
        y = y * _sigmoid(y)
        if do_norm:
            y = y * lax.rsqrt(_dot((y * y).astype(BF16), hsum) + EPS)
            if is_q:
                y = y * (HEAD_DIM ** -0.5)
        o_ref[0, :, lo:lo + MXU_TILE] = y.astype(o_ref.dtype)

    plain, conv = [], []
    for o_ref, (src, start, size, kind) in zip(out_refs, segs):
        if kind is None:
            plain += [functools.partial(plain_piece, o_ref, src, start, lo, min(MXU_TILE, size - lo))
                      for lo in range(0, size, MXU_TILE)]
        else:
            conv += [functools.partial(conv_piece, o_ref, src, start, lo, kind) for lo in range(0, size, MXU_TILE)]
    pending = None
    while plain or conv or pending is not None:
        started = conv.pop(0)() if conv else None
        if started is not None:
            next(started)
        if pending is not None:
            for _ in pending:
                pass
        pending = started
        for _ in range(-(-len(plain) // (len(conv) + 1)) if plain else 0):
            plain.pop(0)()


def _inproj(x, mod3, mod_row, g_norm, w_t, wab_t, conv_w, hsum, segs, dtypes, tm):
    b, t, d = x.shape
    halo = 8
    r = tm // halo
    nblk = t // halo
    if mod_row is None:
        mod_map = lambda bi, i: (bi, 0, 0)
    else:
        mod_map = lambda bi, i: (mod_row, 0, 0)
    const2 = lambda bi, i: (0, 0)
    out_shape = [jax.ShapeDtypeStruct((b, t, size), dt) for (_, _, size, _), dt in zip(segs, dtypes)]
    out_specs = [pl.BlockSpec((1, tm, size), lambda bi, i: (bi, i, 0)) for (_, _, size, _) in segs]
    return pl.pallas_call(
        functools.partial(_inproj_kernel, segs=segs, tm=tm, halo=halo),
        grid=(b, t // tm),
        in_specs=[pl.BlockSpec((1, tm, d), lambda bi, i: (bi, i, 0)),
                  pl.BlockSpec((1, halo, d), lambda bi, i: (bi, jnp.maximum(i * r - 1, 0), 0)),
                  pl.BlockSpec((1, halo, d), lambda bi, i: (bi, jnp.minimum((i + 1) * r, nblk - 1), 0)),
                  pl.BlockSpec((1, 6, d), mod_map),
                  pl.BlockSpec((1, d), const2),
                  pl.BlockSpec(w_t.shape, const2),
                  pl.BlockSpec(wab_t.shape, const2),
                  pl.BlockSpec(conv_w.shape, const2),
                  pl.BlockSpec(hsum.shape, const2)],
        out_specs=out_specs,
        out_shape=out_shape,
        compiler_params=_cparams(2),
        name="inproj",
    )(x, x, x, mod3, g_norm.reshape(1, d), w_t, wab_t, conv_w, hsum)


def _rope(x, cos, sin, lane):
    swapped = jnp.where((lane % 32) < 16, pltpu.roll(x, LANES - 16, 1), pltpu.roll(x, 16, 1))
    return x * cos + swapped * sin


def _attn_kernel(a_ref, kvc_ref, cos_ref, sin_ref, qg_ref, kg_ref, sink_ref, havg_ref,
                 o_ref, k_s, vt_s, *, seq, ctx, qb):
    n = pl.program_id(1)
    havg = havg_ref[...]
    prep_rows = ATTN_PREP_ROWS
    ctx_blocks = ctx // ATTN_BLOCK

    @pl.when(n == 0)
    def _prep():
        kg = kg_ref[...]
        kc = kvc_ref[0, :, 0:KV_WIDTH].astype(F32)
        ms = _dot((kc * kc).astype(BF16), havg)
        k_s[0:ctx, :] = (kc * lax.rsqrt(ms + EPS) * kg).astype(BF16)
        vc = kvc_ref[0, :, KV_WIDTH:2 * KV_WIDTH].astype(F32)
        for j in range(ctx_blocks):
            vt_s[j] = vc[j * ATTN_BLOCK:(j + 1) * ATTN_BLOCK, :].T.astype(BF16)
        lane = lax.broadcasted_iota(jnp.int32, (prep_rows, LANES), 1)
        for r0 in range(0, seq, prep_rows):
            kx = a_ref[0, r0:r0 + prep_rows, _OFF_AK:_OFF_AK + KV_WIDTH].astype(F32)
            ms = _dot((kx * kx).astype(BF16), havg)
            kx = kx * lax.rsqrt(ms + EPS) * kg
            kx = _rope(kx, cos_ref[r0:r0 + prep_rows, :], sin_ref[r0:r0 + prep_rows, :], lane)
            k_s[ctx + r0:ctx + r0 + prep_rows, :] = kx.astype(BF16)
            vx = a_ref[0, r0:r0 + prep_rows, _OFF_AV:_OFF_AV + KV_WIDTH].astype(F32)
            for j in range(prep_rows // ATTN_BLOCK):
                vt_s[ctx_blocks + r0 // ATTN_BLOCK + j] = vx[j * ATTN_BLOCK:(j + 1) * ATTN_BLOCK, :].T.astype(BF16)

    lane = lax.broadcasted_iota(jnp.int32, (ATTN_BLOCK, LANES), 1)
    qg = qg_ref[...]
    n_band = 3
    n_win = n_band * ATTN_BLOCK
    n_keys = n_win + ctx
    key_row = lax.broadcasted_iota(jnp.int32, (n_win, ATTN_BLOCK), 0)
    q_lane = lax.broadcasted_iota(jnp.int32, (n_win, ATTN_BLOCK), 1)
    zeros = jnp.zeros((HEAD_DIM, ATTN_BLOCK), BF16)
    groups = range(ATTN_KV_HEADS)
    sink_rows = [jnp.concatenate([sink_ref[g * ATTN_GROUP + r:g * ATTN_GROUP + r + 1, :] for r in range(ATTN_GROUP)],
                                 axis=1) * LOG2_E for g in groups]

    units = []
    k_alls, vt_alls, valids, rhss = [], [], [], {}
    for sb in range(qb):
        nq = n * qb + sb
        q0 = pl.multiple_of(nq * ATTN_BLOCK, ATTN_BLOCK)
        cos = cos_ref[pl.ds(q0, ATTN_BLOCK), :]
        sin = sin_ref[pl.ds(q0, ATTN_BLOCK), :]
        qts = []
        for j in range(ATTN_WIDTH // LANES):
            qx = a_ref[0, pl.ds(q0, ATTN_BLOCK), j * LANES:(j + 1) * LANES].astype(F32)
            ms = _dot((qx * qx).astype(BF16), havg)
            qx = _rope(qx * lax.rsqrt(ms + EPS) * qg, cos, sin, lane) * (HEAD_DIM ** -0.5 * LOG2_E)
            qts.append(qx.T.astype(BF16))
        blk0 = jnp.clip(nq - 1, 0, seq // ATTN_BLOCK - n_band)
        start = pl.multiple_of(blk0 * ATTN_BLOCK, ATTN_BLOCK)
        valid = jnp.abs(q0 + q_lane - (start + key_row)) <= WINDOW
        valids.append(jnp.concatenate([valid] * ATTN_GROUP, axis=1))
        k_alls.append(jnp.concatenate([k_s[pl.ds(ctx + start, n_win), :], k_s[0:ctx, :]], axis=0))
        vt_loc = vt_s[pl.ds(ctx_blocks + blk0, n_band)]
        vt_alls.append(jnp.concatenate([vt_loc[j] for j in range(n_band)] + [vt_s[j] for j in range(ctx_blocks)],
                                       axis=1))
        for g in groups:
            cols = []
            for r in range(ATTN_GROUP):
                h = g * ATTN_GROUP + r
                piece = qts[h // 2][(h % 2) * HEAD_DIM:(h % 2 + 1) * HEAD_DIM, :]
                cols.append(jnp.concatenate([piece, zeros] if g == 0 else [zeros, piece], axis=0))
            rhss[(sb, g)] = jnp.concatenate(cols, axis=1)
            units.append((sb, g))
        yield

    ss = []
    for sb, g in units:
        s = _dot(k_alls[sb], rhss[(sb, g)])
        ss.append(jnp.concatenate([jnp.where(valids[sb], s[0:n_win], NEG_INF), s[n_win:n_keys]], axis=0))
        yield
    p_bfs, invs = [], []
    for s, (sb, g) in zip(ss, units):
        m = jnp.maximum(jnp.max(s, axis=0, keepdims=True), sink_rows[g])
        p = jnp.exp2(s - m)
        invs.append(1.0 / (jnp.sum(p, axis=0, keepdims=True) + jnp.exp2(sink_rows[g] - m)))
        p_bfs.append(p.astype(BF16))
        yield
    pieces = {sb: [] for sb in range(qb)}
    for (sb, g), p_bf, inv in zip(units, p_bfs, invs):
        vt_g = vt_alls[sb][g * HEAD_DIM:(g + 1) * HEAD_DIM, :]
        for pair in range(ATTN_GROUP // 2):
            lanes = slice(pair * MXU_TILE, (pair + 1) * MXU_TILE)
            ot = _dot(vt_g, p_bf[:, lanes]) * inv[:, lanes]
            pieces[sb] += [ot[:, 0:ATTN_BLOCK], ot[:, ATTN_BLOCK:2 * ATTN_BLOCK]]
        yield
    for sb in range(qb):
        outs = [jnp.concatenate([pieces[sb][2 * j], pieces[sb][2 * j + 1]], axis=0).T
                for j in range(ATTN_Q_HEADS // 2)]
        o_ref[0, sb * ATTN_BLOCK:(sb + 1) * ATTN_BLOCK, :] = jnp.concatenate(outs, axis=1).astype(o_ref.dtype)


def _run_parts(parts, b, name):
    steps = parts[0]["steps"]
    assert all(p["steps"] == steps for p in parts)
    counts = [(len(p["args"]), len(p["out_shape"]), len(p["scratch_shapes"])) for p in parts]
    n_in = sum(c[0] for c in counts)
    n_out = sum(c[1] for c in counts)

    def kern(*refs):
        i_pos, o_pos, s_pos = 0, n_in, n_in + n_out
        bodies = []
        for p, (ci, co, cs) in zip(parts, counts):
            bodies.append(p["kernel"](*refs[i_pos:i_pos + ci], *refs[o_pos:o_pos + co], *refs[s_pos:s_pos + cs]))
            i_pos, o_pos, s_pos = i_pos + ci, o_pos + co, s_pos + cs
        while bodies:
            for body in list(bodies):
                try:
                    next(body)
                except StopIteration:
                    bodies.remove(body)

    outs = pl.pallas_call(
        kern,
        grid=(b, steps),
        in_specs=[s for p in parts for s in p["in_specs"]],
        out_specs=[s for p in parts for s in p["out_specs"]],
        out_shape=[s for p in parts for s in p["out_shape"]],
        scratch_shapes=[s for p in parts for s in p["scratch_shapes"]],
        compiler_params=_cparams(2),
        name=name,
    )(*[a for p in parts for a in p["args"]])
    res, pos = [], 0
    for _, co, _ in counts:
        res.append(outs[pos:pos + co])
        pos += co
    return res


def _attn_parts(a_lat, kv_ctx, cos, sin, qg, kg, sink, havg, qb):
    b, s, wa = a_lat.shape
    ctx = kv_ctx.shape[1]
    nb = s // ATTN_BLOCK
    const2 = lambda bi, n: (0, 0)
    return dict(
        kernel=functools.partial(_attn_kernel, seq=s, ctx=ctx, qb=qb),
        steps=nb // qb,
        in_specs=[pl.BlockSpec((1, s, wa), lambda bi, n: (bi, 0, 0)),
                  pl.BlockSpec((1, ctx, 2 * KV_WIDTH), lambda bi, n: (bi, 0, 0)),
                  pl.BlockSpec((s, LANES), const2),
                  pl.BlockSpec((s, LANES), const2),
                  pl.BlockSpec((1, LANES), const2),
                  pl.BlockSpec((1, LANES), const2),
                  pl.BlockSpec((ATTN_Q_HEADS, LANES), const2),
                  pl.BlockSpec((LANES, LANES), const2)],
        out_specs=[pl.BlockSpec((1, qb * ATTN_BLOCK, ATTN_WIDTH), lambda bi, n: (bi, n, 0))],
        out_shape=[jax.ShapeDtypeStruct((b, s, ATTN_WIDTH), BF16)],
        scratch_shapes=[pltpu.VMEM((ctx + s, KV_WIDTH), BF16),
                        pltpu.VMEM(((ctx + s) // ATTN_BLOCK, KV_WIDTH, ATTN_BLOCK), BF16)],
        args=[a_lat, kv_ctx, cos, sin, qg, kg, sink, havg])


def _half_mask():
    return lax.broadcasted_iota(jnp.int32, (CHUNK, LANES), 1) < HEAD_DIM


def _bdiag(x_lane, lo_half):
    zeros = jnp.zeros((CHUNK, LANES), x_lane.dtype)
    blocks = []
    for h in range(HEADS_PER_TILE):
        col = (h * HEAD_DIM) // LANES
        piece = x_lane[:, col * LANES:(col + 1) * LANES]
        piece = jnp.where(lo_half if (h * HEAD_DIM) % LANES == 0 else jnp.logical_not(lo_half), piece, zeros)
        blocks.append(jnp.concatenate([piece if c == col else zeros for c in range(MXU_TILE // LANES)], axis=1))
    return jnp.concatenate(blocks, axis=0)


def _unit_tri_inverses(a_list, eye_l, sub_mask, lo_half):
    ads = [jnp.where(sub_mask, a, 0.0) for a in a_list]
    ys = [jnp.where(sub_mask, 0.0, a) for a in a_list]
    ps = [eye_l - ad for ad in ads]
    pws = [ad.astype(BF16) for ad in ads]
    n_levels = int(np.log2(DN_SUB))
    c1, c2, c3 = CHUNK, 2 * CHUNK, 3 * CHUNK
    for level in range(n_levels):
        rhss = [_bdiag(pw, lo_half) for pw in pws]
        if level == 0:
            boths = [_dot(jnp.concatenate([pw, y.astype(BF16)], axis=0), rhs) for pw, y, rhs in zip(pws, ys, rhss)]
            pws = [both[0:c1].astype(BF16) for both in boths]
            ys = [y - both[c1:c2] for y, both in zip(ys, boths)]
        elif level < n_levels - 1:
            boths = [_dot(jnp.concatenate([pw, p.astype(BF16), y.astype(BF16)], axis=0), rhs)
                     for pw, p, y, rhs in zip(pws, ps, ys, rhss)]
            pws = [both[0:c1].astype(BF16) for both in boths]
            ps = [p + both[c1:c2] for p, both in zip(ps, boths)]
            ys = [y + both[c2:c3] for y, both in zip(ys, boths)]
        else:
            boths = [_dot(jnp.concatenate([p.astype(BF16), y.astype(BF16)], axis=0), rhs)
                     for p, y, rhs in zip(ps, ys, rhss)]
            ps = [p + both[0:c1] for p, both in zip(ps, boths)]
            ys = [y + both[c1:c2] for y, both in zip(ys, boths)]
        yield
    assert CHUNK // DN_SUB == 4
    b_bfs = [y.astype(BF16) for y in ys]
    boths = [_dot(jnp.concatenate([b, p.astype(BF16)], axis=0), _bdiag(b, lo_half)) for b, p in zip(b_bfs, ps)]
    yield
    zs = [p - both[c1:c2] for p, both in zip(ps, boths)]
    return [z + _dot(z.astype(BF16), _bdiag(both[0:c1].astype(BF16), lo_half)) for z, both in zip(zs, boths)]


def _dn_factors(abs_, d, arow, dtrow, exp2_ref, tri_ref, eye_t):
    def hi_lo(v):
        hi = v.astype(BF16)
        return hi, (v - hi.astype(F32)).astype(BF16)

    lane = lax.broadcasted_iota(jnp.int32, abs_[0].shape, 1)
    is_g = (lane % 32) < 16
    xhls = []
    for ab in abs_:
        z = ab + dtrow
        softplus = jnp.maximum(z, 0.0) + jnp.log(1.0 + jnp.exp(-jnp.abs(z)))
        x = jnp.where(is_g, -arow * softplus, _sigmoid(ab))
        x = jnp.where(is_g, _dot(tri_ref[d], jnp.concatenate(hi_lo(x), axis=0)), x)
        x_hi, x_lo = hi_lo(x)
        xhls.append(jnp.where(lane < 32, x_hi, x_lo)[:, 0:64])
    y = _dot(jnp.concatenate(xhls, axis=0), exp2_ref[d])
    out = []
    for c in range(len(abs_)):
        gi = y[c * CHUNK:(c + 1) * CHUNK, 0:DN_WIDTH]
        be = y[c * CHUNK:(c + 1) * CHUNK, DN_WIDTH:2 * DN_WIDTH]
        gj = jnp.sum(gi * eye_t, axis=0, keepdims=True)
        out.append((be, gi, jnp.broadcast_to(gj, gi.shape)))
    return out


def _block_mask():
    return (lax.broadcasted_iota(jnp.int32, (MXU_TILE, MXU_TILE), 0) // HEAD_DIM
            == lax.broadcasted_iota(jnp.int32, (MXU_TILE, MXU_TILE), 1) // HEAD_DIM)


def _dnprep_kernel(*refs, cb, want_o):
    if want_o:
        q_ref, k_ref, v_ref, ab_ref = refs[:4]
        refs = refs[4:]
    else:
        q_ref = None
        k_ref, v_ref, ab_ref = refs[:3]
        refs = refs[3:]
    arow_ref, dtrow_ref, exp2_ref, tri_ref, w_o, uv_o, kt_o, dl_o = refs[:8]
    qd_o, in_o = refs[8:10] if want_o else (None, None)
    n_tiles = DN_WIDTH // MXU_TILE
    row = lax.broadcasted_iota(jnp.int32, (CHUNK, MXU_TILE), 0)
    colj = lax.broadcasted_iota(jnp.int32, (CHUNK, MXU_TILE), 1) % HEAD_DIM
    eye_l = (row == colj).astype(F32)
    eye_t = jnp.concatenate([eye_l] * n_tiles, axis=1)
    lo_half = _half_mask()
    arow = arow_ref[...]
    dtrow = dtrow_ref[...]

    fac = {}
    shared = {}
    abs_ = [ab_ref[0, c * CHUNK:(c + 1) * CHUNK, :] for c in range(cb)]
    for d in range(N_DIR):
        for c, f in enumerate(_dn_factors(abs_, d, arow, dtrow, exp2_ref, tri_ref, eye_t)):
            fac[(c, d)] = f
    yield
    for c in range(cb):
        rows = slice(c * CHUNK, (c + 1) * CHUNK)
        for g in range(n_tiles):
            lanes = slice(g * MXU_TILE, (g + 1) * MXU_TILE)
            k_l = k_ref[0, rows, lanes]
            kbd = _bdiag(k_l, lo_half)
            if want_o:
                kq = _dot_nt(jnp.concatenate([k_l, q_ref[0, rows, lanes]], axis=0), kbd)
                shared[(c, g)] = (kq[0:CHUNK], kq[CHUNK:2 * CHUNK])
            else:
                shared[(c, g)] = (_dot_nt(k_l, kbd), None)
        yield

    units = [(c, g, d) for c in range(cb) for g in range(n_tiles) for d in range(N_DIR)]
    decs = []
    for c, g, d in units:
        lanes = slice(g * MXU_TILE, (g + 1) * MXU_TILE)
        be, gi, gj = fac[(c, d)]
        lower = (row > colj) if d == 0 else (row < colj)
        decs.append(jnp.where(lower, jnp.exp(jnp.where(lower, gi[:, lanes] - gj[:, lanes], 0.0)), 0.0))
    a_list = [fac[(c, d)][0][:, g * MXU_TILE:(g + 1) * MXU_TILE] * dec * shared[(c, g)][0]
              for (c, g, d), dec in zip(units, decs)]
    yield
    tinvs = yield from _unit_tri_inverses(a_list, eye_l, (row // DN_SUB) == (colj // DN_SUB), lo_half)

    for idx, ((c, g, d), dec, tinv) in enumerate(zip(units, decs, tinvs)):
        if idx % (n_tiles * N_DIR) == 0:
            yield
        rows = slice(c * CHUNK, (c + 1) * CHUNK)
        lanes = slice(g * MXU_TILE, (g + 1) * MXU_TILE)
        be, gi, _ = fac[(c, d)]
        be, gi = be[:, lanes], gi[:, lanes]
        last = CHUNK - 1 if d == 0 else 0
        e_g = jnp.exp(gi)
        gl_row = gi[last:last + 1, :]
        kf = k_ref[0, rows, lanes].astype(F32)
        vf = v_ref[0, rows, lanes].astype(F32)
        rhs = jnp.concatenate([_bdiag((be * e_g * kf).astype(BF16), lo_half), _bdiag((be * vf).astype(BF16), lo_half)],
                              axis=1)
        wu = _dot(tinv.astype(BF16), rhs)
        w_o[0, d, rows, lanes] = wu[:, 0:MXU_TILE].astype(BF16)
        uv_o[0, d, rows, lanes] = wu[:, MXU_TILE:2 * MXU_TILE].astype(BF16)
        kt_o[0, d, rows, lanes] = (jnp.exp(gl_row - gi) * kf).astype(BF16)
        dl_o[0, d, c, :, lanes] = jnp.exp(gl_row)
        if want_o:
            qd_o[0, d, rows, lanes] = (e_g * q_ref[0, rows, lanes].astype(F32)).astype(BF16)
            in_o[0, d, rows, lanes] = ((dec + eye_l) * shared[(c, g)][1]).astype(BF16)


def _dnprep_parts(q, k, v, ab, arow, dtrow, exp2, tri, cb):
    want_o = q is not None
    b, t, w = k.shape
    tb = cb * CHUNK
    tok = lambda bi, i: (bi, i, 0)
    const2 = lambda bi, i: (0, 0)
    const3 = lambda bi, i: (0, 0, 0)
    dir_tok = lambda bi, i: (bi, 0, i, 0)
    data = ([q] if want_o else []) + [k, v]
    in_specs = ([pl.BlockSpec((1, tb, w), tok)] * len(data) + [pl.BlockSpec((1, tb, LANES), tok),
                pl.BlockSpec((1, LANES), const2), pl.BlockSpec((1, LANES), const2),
                pl.BlockSpec(exp2.shape, const3), pl.BlockSpec(tri.shape, const3)])
    big = lambda dt: jax.ShapeDtypeStruct((b, N_DIR, t, w), dt)
    big_spec = pl.BlockSpec((1, N_DIR, tb, w), dir_tok)
    out_shape = [big(BF16), big(BF16), big(BF16), jax.ShapeDtypeStruct((b, N_DIR, t // CHUNK, 1, w), F32)]
    out_specs = [big_spec, big_spec, big_spec, pl.BlockSpec((1, N_DIR, cb, 1, w), lambda bi, i: (bi, 0, i, 0, 0))]
    if want_o:
        out_shape += [big(BF16), big(BF16)]
        out_specs += [big_spec, big_spec]
    return dict(kernel=functools.partial(_dnprep_kernel, cb=cb, want_o=want_o), steps=t // tb, in_specs=in_specs,
                out_specs=out_specs, out_shape=out_shape, scratch_shapes=[],
                args=data + [ab, arow, dtrow, exp2, tri])


def _dnscan_kernel(*refs, n_chunk, bb, want_o, have_s0, want_s):
    n_in = 6 if want_o else 4
    dir_refs = [refs[0:n_in], refs[n_in:2 * n_in]]
    pos = 2 * n_in
    s0_ref = refs[pos] if have_s0 else None
    pos += int(have_s0)
    o_refs = refs[pos:pos + N_DIR] if want_o else None
    pos += N_DIR if want_o else 0
    sout_ref = refs[pos] if want_s else None
    pos += int(want_s)
    s_scr = refs[pos]
    n_tiles = DN_WIDTH // MXU_TILE
    i = pl.program_id(1)

    @pl.when(i == 0)
    def _init():
        if have_s0:
            s_scr[...] = s0_ref[...]
        else:
            s_scr[...] = jnp.zeros_like(s_scr)

    bmask = _block_mask()
    lo_half = _half_mask()
    chains = [(bi, d, g) for bi in range(bb) for d in range(N_DIR) for g in range(n_tiles)]

    def body(j, carry):
        cidx = (j, n_chunk - 1 - j)
        r0s = [pl.multiple_of(cidx[d] * CHUNK, CHUNK) for d in range(N_DIR)]
        s_olds, r1s = [], []
        for bi, d, g in chains:
            lanes = slice(g * MXU_TILE, (g + 1) * MXU_TILE)
            w = dir_refs[d][0][bi, 0, pl.ds(r0s[d], CHUNK), lanes]
            if want_o:
                w = jnp.concatenate([w, dir_refs[d][4][bi, 0, pl.ds(r0s[d], CHUNK), lanes]], axis=0)
            s_old = s_scr[bi, d * n_tiles + g]
            s_olds.append(s_old)
            r1s.append(_dot(w, s_old.astype(BF16)))
        u_bfs = []
        for (bi, d, g), r1 in zip(chains, r1s):
            lanes = slice(g * MXU_TILE, (g + 1) * MXU_TILE)
            u_bfs.append((dir_refs[d][1][bi, 0, pl.ds(r0s[d], CHUNK), lanes] - r1[0:CHUNK]).astype(BF16))
        for (bi, d, g), r1, u_bf, s_old in zip(chains, r1s, u_bfs, s_olds):
            lanes = slice(g * MXU_TILE, (g + 1) * MXU_TILE)
            kt = dir_refs[d][2][bi, 0, pl.ds(r0s[d], CHUNK), lanes]
            ds = jnp.where(bmask, _dot_tn(kt, u_bf), 0.0)
            dl = dir_refs[d][3][bi, 0, cidx[d]][:, lanes]
            s_scr[bi, d * n_tiles + g] = s_old * dl + ds
            if want_o:
                intra = dir_refs[d][5][bi, 0, pl.ds(r0s[d], CHUNK), lanes]
                o = r1[CHUNK:2 * CHUNK] + _dot(intra, _bdiag(u_bf, lo_half))
                o_refs[d][bi, pl.ds(r0s[d], CHUNK), lanes] = o.astype(BF16)
        return carry

    lax.fori_loop(0, n_chunk, body, 0)

    if want_s:
        @pl.when(i == pl.num_programs(1) - 1)
        def _fin():
            sout_ref[...] = s_scr[...]


def _dnscan(prep, s0, tb, bb, want_s):
    want_o = len(prep) == 6
    b, _, t, w = prep[0].shape
    n_t = t // tb
    n_chunk = tb // CHUNK
    n_chain = N_DIR * (w // MXU_TILE)

    def specs(d):
        blk = (lambda bi, i: i) if d == 0 else (lambda bi, i: n_t - 1 - i)
        big = pl.BlockSpec((bb, 1, tb, w), lambda bi, i: (bi, d, blk(bi, i), 0))
        dl = pl.BlockSpec((bb, 1, n_chunk, 1, w), lambda bi, i: (bi, d, blk(bi, i), 0, 0))
        return [big, big, big, dl] + ([big, big] if want_o else [])

    in_specs = specs(0) + specs(1)
    args = list(prep) + list(prep)
    state_spec = pl.BlockSpec((bb, n_chain, MXU_TILE, MXU_TILE), lambda bi, i: (bi, 0, 0, 0))
    if s0 is not None:
        in_specs.append(state_spec)
        args.append(s0)
    out_shape, out_specs = [], []
    if want_o:
        out_shape += [jax.ShapeDtypeStruct((b, t, w), BF16)] * N_DIR
        out_specs += [pl.BlockSpec((bb, tb, w), lambda bi, i: (bi, i, 0)),
                      pl.BlockSpec((bb, tb, w), lambda bi, i: (bi, n_t - 1 - i, 0))]
    if want_s:
        out_shape.append(jax.ShapeDtypeStruct((b, n_chain, MXU_TILE, MXU_TILE), F32))
        out_specs.append(state_spec)
    return pl.pallas_call(
        functools.partial(_dnscan_kernel, n_chunk=n_chunk, bb=bb, want_o=want_o, have_s0=s0 is not None,
                          want_s=want_s),
        grid=(b // bb, n_t),
        in_specs=in_specs,
        out_specs=out_specs,
        out_shape=out_shape,
        scratch_shapes=[pltpu.VMEM((bb, n_chain, MXU_TILE, MXU_TILE), F32)],
        compiler_params=_cparams(2),
        name="dnscan",
    )(*args)


def _tail_kernel(x_ref, mod_ref, ya_ref, odf_ref, odb_ref, z_ref, gate_ref, dng_ref, havg_ref, wba_ref, wbd_ref,
                 wo_ref, gn2_ref, w1_ref, w2_ref, o_ref, *, ff_chunk):
    havg = havg_ref[...]
    dng = dng_ref[...]
    yd_parts = []
    for j in range(DN_WIDTH // MXU_TILE):
        sl = slice(j * MXU_TILE, (j + 1) * MXU_TILE)
        od = odf_ref[0, :, sl].astype(F32) + odb_ref[0, :, sl].astype(F32)
        ms = _dot((od * od).astype(BF16), havg)
        z = z_ref[0, :, sl].astype(F32)
        yd_parts.append((od * lax.rsqrt(ms + EPS) * dng * (z * _sigmoid(z))).astype(BF16))
    yd = jnp.concatenate(yd_parts, axis=1)
    ga = gate_ref[0, :, 0:D_MODEL].astype(F32)
    gd = gate_ref[0, :, D_MODEL:2 * D_MODEL].astype(F32)
    y = _sigmoid(ga) * _dot(ya_ref[0], wba_ref[...]) + _sigmoid(gd) * _dot(yd, wbd_ref[...])
    br = _dot(y.astype(BF16), wo_ref[...])
    mod = mod_ref[0]
    out1 = x_ref[0] + mod[2:3] * br
    ms2 = jnp.mean(out1 * out1, axis=-1, keepdims=True)
    hm = (out1 * lax.rsqrt(ms2 + EPS) * (gn2_ref[...] * (1.0 + mod[4:5])) + mod[3:4]).astype(BF16)
    acc = None
    for j in range(D_FF // ff_chunk):
        a = jnp.maximum(_dot(hm, w1_ref[:, j * ff_chunk:(j + 1) * ff_chunk]), 0.0)
        part = _dot((a * a).astype(BF16), w2_ref[j * ff_chunk:(j + 1) * ff_chunk, :])
        acc = part if acc is None else acc + part
    o_ref[0] = out1 + mod[5:6] * acc


def _resident(shape):
    return pl.BlockSpec(shape, lambda bi, i: (0,) * len(shape), pipeline_mode=pl.Buffered(1))


def _tail(x, mod3, y_attn, o_df, o_db, z, gates, dng, havg, wba, wbd, wo, gn2, w1, w2, tm):
    b, t, d = x.shape
    tok = lambda bi, i: (bi, i, 0)
    return pl.pallas_call(
        functools.partial(_tail_kernel, ff_chunk=MLP_FF_CHUNK),
        grid=(b, t // tm),
        in_specs=[pl.BlockSpec((1, tm, d), tok),
                  pl.BlockSpec((1, 6, d), lambda bi, i: (bi, 0, 0)),
                  pl.BlockSpec((1, tm, ATTN_WIDTH), tok),
                  pl.BlockSpec((1, tm, DN_WIDTH), tok),
                  pl.BlockSpec((1, tm, DN_WIDTH), tok),
                  pl.BlockSpec((1, tm, DN_WIDTH), tok),
                  pl.BlockSpec((1, tm, 2 * d), tok),
                  _resident((1, MXU_TILE)),
                  _resident((MXU_TILE, MXU_TILE)),
                  _resident(wba.shape),
                  _resident(wbd.shape),
                  _resident(wo.shape),
                  _resident((1, d)),
                  _resident(w1.shape),
                  _resident(w2.shape)],
        out_specs=pl.BlockSpec((1, tm, d), tok),
        out_shape=jax.ShapeDtypeStruct((b, t, d), F32),
        compiler_params=_cparams(2),
        name="tail",
    )(x, mod3, y_attn, o_df, o_db, z, gates, dng, havg, wba, wbd, wo, gn2, w1, w2)


def _head_avg(n, scale):
    idx = np.arange(n) // HEAD_DIM
    return jnp.asarray((idx[:, None] == idx[None, :]).astype(np.float32) * scale, BF16)


def _dn_expand_matrix():
    n = N_DIR * DN_HEADS
    m = np.zeros((N_DIR, 4 * n, 2 * DN_WIDTH), np.float32)
    for d in range(N_DIR):
        for part in range(2):
            for h in range(DN_HEADS):
                idx = d * DN_HEADS + h
                m[d, part * 2 * n + idx, h * HEAD_DIM:(h + 1) * HEAD_DIM] = 1.0
                m[d, part * 2 * n + n + idx, DN_WIDTH + h * HEAD_DIM:DN_WIDTH + (h + 1) * HEAD_DIM] = 1.0
    return jnp.asarray(m, BF16)


def _tri_matrices():
    i = np.arange(CHUNK)
    low = (i[:, None] >= i[None, :]).astype(np.float32)
    up = (i[:, None] <= i[None, :]).astype(np.float32)
    return jnp.asarray(np.stack([np.concatenate([low, low], axis=1), np.concatenate([up, up], axis=1)]), BF16)


def _rope_tables(seq):
    half = HEAD_DIM // 2
    n_freq = half // 2
    freqs = ROPE_BASE ** (-jnp.arange(n_freq, dtype=F32) / n_freq)
    pos = jnp.arange(seq)
    ang_r = (pos // GRID_W).astype(F32)[:, None] * freqs
    ang_c = (pos % GRID_W).astype(F32)[:, None] * freqs
    cos = jnp.concatenate([jnp.cos(ang_r)] * 2 + [jnp.cos(ang_c)] * 2, axis=1)
    sin = jnp.concatenate([-jnp.sin(ang_r), jnp.sin(ang_r), -jnp.sin(ang_c), jnp.sin(ang_c)], axis=1)
    reps = LANES // HEAD_DIM
    return jnp.tile(cos, (1, reps)), jnp.tile(sin, (1, reps))


def _pad_cols(w, n):
    return jnp.pad(w, ((0, 0), (0, n - w.shape[1])))


def kernel(x, c, ctx, c_ctx, w_ada, b_ada, g_norm1, w_in, q_norm_g, k_norm_g, attn_sink, conv_w, a_log, dt_bias,
           dn_norm_g, w_br_attn, w_br_dn, w_out, g_norm2, w_mlp1, w_mlp2):
    depth = w_ada.shape[0]
    assert depth == 1, "single-layer trunk only"
    b, s, d = x.shape
    n_ctx = ctx.shape[1]
    assert d == D_MODEL and w_in.shape[-1] == _IN_WIDTH
    assert s >= 3 * ATTN_BLOCK and s % ATTN_BLOCK == 0 and s % CHUNK == 0 and n_ctx % CHUNK == 0
    out_dtype = x.dtype
    w_in_t = jnp.swapaxes(w_in[0], 0, 1).astype(BF16)

    mod_rows = 16
    cc = jnp.concatenate([c.astype(F32), c_ctx.astype(F32)[None], jnp.zeros((mod_rows - b - 1, d), F32)], axis=0)
    mod = _ada(cc, w_ada[0], b_ada[0])
    mod3 = mod.reshape(mod_rows, 6, d)

    ab_rows = w_in_t[_OFF_DA:_OFF_GA]
    wab_t = jnp.concatenate([ab_rows, ab_rows, jnp.zeros((LANES - 2 * (_OFF_GA - _OFF_DA), d), BF16)], axis=0)
    hsum = _head_avg(MXU_TILE, 1.0)
    segs_lat = ((0, 0, _OFF_DQ, None),
                (0, _OFF_DQ, DN_WIDTH, (0, True, True)), (0, _OFF_DK, DN_WIDTH, (DN_WIDTH, True, False)),
                (0, _OFF_DV, DN_WIDTH, (2 * DN_WIDTH, False, False)),
                (0, _OFF_DZ, DN_WIDTH, None), (0, _OFF_GA, 2 * D_MODEL, None), (1, 0, LANES, None))
    a_lat, q_d, k_d, v_d, z_lat, gates, ab_lat = _inproj(x, mod3, None, g_norm1[0], w_in_t, wab_t, conv_w[0], hsum,
                                                         segs_lat, (BF16, BF16, BF16, BF16, BF16, BF16, F32),
                                                         tm=TOKEN_TILE)
    segs_ctx = ((0, _OFF_AK, 2 * KV_WIDTH, None),
                (0, _OFF_DK, DN_WIDTH, (0, True, False)), (0, _OFF_DV, DN_WIDTH, (DN_WIDTH, False, False)),
                (1, 0, LANES, None))
    kv_ctx, k_dc, v_dc, ab_ctx = _inproj(ctx, mod3, b, g_norm1[0], w_in_t, wab_t, conv_w[0][:, DN_WIDTH:], hsum,
                                         segs_ctx, (BF16, BF16, BF16, F32), tm=n_ctx)

    cos, sin = _rope_tables(s)
    reps = LANES // HEAD_DIM
    n_gate = N_DIR * DN_HEADS
    arow = _pad_cols(jnp.tile(jnp.concatenate([jnp.exp(a_log[0]).reshape(1, n_gate), jnp.zeros((1, n_gate), F32)],
                                              axis=1), (1, 2)), LANES)
    dtrow = _pad_cols(jnp.tile(jnp.concatenate([dt_bias[0].reshape(1, n_gate), jnp.zeros((1, n_gate), F32)],
                                               axis=1), (1, 2)), LANES)
    exp2, tri = _dn_expand_matrix(), _tri_matrices()
    dn_cb = DN_CHUNKS_PER_STEP
    qb = (s // ATTN_BLOCK) // (s // (dn_cb * CHUNK))
    attn_parts = _attn_parts(a_lat, kv_ctx, cos, sin,
                             jnp.tile(q_norm_g[0].astype(F32), reps)[None],
                             jnp.tile(k_norm_g[0].astype(F32), reps)[None],
                             jnp.broadcast_to(attn_sink[0].astype(F32)[:, None], (ATTN_Q_HEADS, LANES)),
                             _head_avg(LANES, 1.0 / HEAD_DIM), qb)
    (y_attn,), prep_lat = _run_parts(
        [attn_parts, _dnprep_parts(q_d, k_d, v_d, ab_lat, arow, dtrow, exp2, tri, dn_cb)], b, "attn_dnprep")
    (prep_ctx,) = _run_parts([_dnprep_parts(None, k_dc, v_dc, ab_ctx, arow, dtrow, exp2, tri, dn_cb)], b, "dnprep")

    bb = max(q for q in range(1, SCAN_BATCH_ROWS + 1) if b % q == 0)
    (s_ctx,) = _dnscan(prep_ctx, None, tb=n_ctx, bb=bb, want_s=True)
    o_df, o_db = _dnscan(prep_lat, s_ctx, tb=SCAN_TOKEN_TILE, bb=bb, want_s=False)

    out = _tail(x, mod3, y_attn, o_df, o_db, z_lat, gates,
                jnp.tile(dn_norm_g[0].astype(F32), HEADS_PER_TILE)[None], _head_avg(MXU_TILE, 1.0 / HEAD_DIM),
                w_br_attn[0].astype(BF16), w_br_dn[0].astype(BF16), w_out[0].astype(BF16),
                g_norm2[0].reshape(1, d), w_mlp1[0].astype(BF16), w_mlp2[0].astype(BF16), tm=TOKEN_TILE)
    return out.astype(out_dtype)
```

```python
import functools

import numpy as np
import jax
import jax.numpy as jnp
from jax import lax
from jax.experimental import pallas as pl
from jax.experimental.pallas import tpu as pltpu

F32 = jnp.float32
BF16 = jnp.bfloat16

D_MODEL = 1024
GRID_W = 64
HEAD_DIM = 64
ATTN_Q_HEADS = 8
ATTN_KV_HEADS = 2
ATTN_GROUP = ATTN_Q_HEADS // ATTN_KV_HEADS
WINDOW = 128
ATTN_BLOCK = 128
ROPE_BASE = 10000.0
DN_HEADS = 8
CONV_W = 3
CHUNK = 64
N_DIR = 2
D_FF = 4 * D_MODEL
EPS = 1e-6
NEG_INF = -1e30
LOG2_E = float(np.log2(np.e))

ATTN_WIDTH = ATTN_Q_HEADS * HEAD_DIM
KV_WIDTH = ATTN_KV_HEADS * HEAD_DIM
DN_WIDTH = DN_HEADS * HEAD_DIM
LANES = 128
MXU_TILE = 256
HEADS_PER_TILE = MXU_TILE // HEAD_DIM
DN_SUB = 16
VMEM_LIMIT = 56 * 1024 * 1024

TOKEN_TILE = 512
ADA_COL_TILE = 1536
MLP_FF_CHUNK = 1024
ATTN_PREP_ROWS = 256
DN_CHUNKS_PER_STEP = 4
SCAN_TOKEN_TILE = 256
SCAN_BATCH_ROWS = 4

_OFF_AQ = 0
_OFF_AK = _OFF_AQ + ATTN_WIDTH
_OFF_AV = _OFF_AK + KV_WIDTH
_OFF_DQ = _OFF_AV + KV_WIDTH
_OFF_DK = _OFF_DQ + DN_WIDTH
_OFF_DV = _OFF_DK + DN_WIDTH
_OFF_DZ = _OFF_DV + DN_WIDTH
_OFF_DA = _OFF_DZ + DN_WIDTH
_OFF_DB = _OFF_DA + N_DIR * DN_HEADS
_OFF_GA = _OFF_DB + N_DIR * DN_HEADS
_OFF_GD = _OFF_GA + D_MODEL
_IN_WIDTH = _OFF_GD + D_MODEL


def _sigmoid(x):
    return 0.5 * jnp.tanh(0.5 * x) + 0.5


def _dot(a, b):
    return jnp.dot(a, b, preferred_element_type=F32)


def _dot_nt(a, b):
    return lax.dot_general(a, b, (((1,), (1,)), ((), ())), preferred_element_type=F32)


def _dot_tn(a, b):
    return lax.dot_general(a, b, (((0,), (0,)), ((), ())), preferred_element_type=F32)


def _cparams(n_axes):
    return pltpu.CompilerParams(dimension_semantics=("arbitrary",) * n_axes, vmem_limit_bytes=VMEM_LIMIT)


def _ada_kernel(c_ref, w_ref, b_ref, o_ref):
    c = c_ref[...]
    s = c * _sigmoid(c)
    o_ref[...] = _dot(s.astype(BF16), w_ref[...].astype(BF16)) + b_ref[...]


def _ada(cc, w_ada, b_ada):
    rows, d = cc.shape
    n = w_ada.shape[1]
    tn = ADA_COL_TILE
    return pl.pallas_call(
        _ada_kernel,
        grid=(n // tn,),
        in_specs=[pl.BlockSpec((rows, d), lambda j: (0, 0)),
                  pl.BlockSpec((d, tn), lambda j: (0, j)),
                  pl.BlockSpec((1, tn), lambda j: (0, j))],
        out_specs=pl.BlockSpec((rows, tn), lambda j: (0, j)),
        out_shape=jax.ShapeDtypeStruct((rows, n), F32),
        compiler_params=_cparams(1),
        name="ada",
    )(cc, w_ada, b_ada.reshape(1, n))


def _inproj_kernel(x_ref, xprev_ref, xnext_ref, mod_ref, g_ref, wt_ref, wabt_ref, cw_ref, hsum_ref, *out_refs,
                   segs, tm, halo):
    w_refs = (wt_ref, wabt_ref)
    i = pl.program_id(1)
    last = pl.num_programs(1) - 1
    mod = mod_ref[0]
    scale = g_ref[...] * (1.0 + mod[1:2])

    def norm_mod(v):
        ms = jnp.mean(v * v, axis=-1, keepdims=True)
        return (v * lax.rsqrt(ms + EPS) * scale + mod[0:1]).astype(BF16)

    h = norm_mod(x_ref[0])
    h_ext = jnp.concatenate([h, norm_mod(jnp.concatenate([xprev_ref[0], xnext_ref[0]], axis=0))], axis=0)
    keep_prev = (i > 0).astype(F32)
    keep_next = (i < last).astype(F32)
    rows = lax.broadcasted_iota(jnp.int32, (tm, MXU_TILE), 0)
    hsum = hsum_ref[...]
    def plain_piece(o_ref, src, start, lo, width):
        o_ref[0, :, lo:lo + width] = _dot_nt(h, w_refs[src][start + lo:start + lo + width, :]).astype(o_ref.dtype)

    def conv_piece(o_ref, src, start, lo, kind):
        conv_col, do_norm, is_q = kind
        w_rows = w_refs[src][start + lo:start + lo + MXU_TILE, :]
        p_ext = _dot_nt(h_ext, w_rows)
        p, p_halo = p_ext[0:tm], p_ext[tm:tm + 2 * halo]
        yield
        p_prev = jnp.where(rows == 0, p_halo[halo - 1:halo] * keep_prev, pltpu.roll(p, 1, 0))
        p_next = jnp.where(rows == tm - 1, p_halo[halo:halo + 1] * keep_next, pltpu.roll(p, tm - 1, 0))
        cw = cw_ref[:, conv_col + lo:conv_col + lo + MXU_TILE]
        y = p_prev * cw[0:1] + p * cw[1:2] + p_next * cw[2:3]
        y = y * _sigmoid(y)
        if do_norm:
            y = y * lax.rsqrt(_dot((y * y).astype(BF16), hsum) + EPS)
            if is_q:
                y = y * (HEAD_DIM ** -0.5)
        o_ref[0, :, lo:lo + MXU_TILE] = y.astype(o_ref.dtype)

    plain, conv = [], []
    for o_ref, (src, start, size, kind) in zip(out_refs, segs):
        if kind is None:
            plain += [functools.partial(plain_piece, o_ref, src, start, lo, min(MXU_TILE, size - lo))
                      for lo in range(0, size, MXU_TILE)]
        else:
            conv += [functools.partial(conv_piece, o_ref, src, start, lo, kind) for lo in range(0, size, MXU_TILE)]
    pending = None
    while plain or conv or pending is not None:
        started = conv.pop(0)() if conv else None
        if started is not None:
            next(started)
        if pending is not None:
            for _ in pending:
                pass
        pending = started
        for _ in range(-(-len(plain) // (len(conv) + 1)) if plain else 0):
            plain.pop(0)()


def _inproj(x, mod3, mod_row, g_norm, w_t, wab_t, conv_w, hsum, segs, dtypes, tm):
    b, t, d = x.shape
    halo = 8
    r = tm // halo
    nblk = t // halo
    if mod_row is None:
        mod_map = lambda bi, i: (bi, 0, 0)
    else:
        mod_map = lambda bi, i: (mod_row, 0, 0)
    const2 = lambda bi, i: (0, 0)
    out_shape = [jax.ShapeDtypeStruct((b, t, size), dt) for (_, _, size, _), dt in zip(segs, dtypes)]
    out_specs = [pl.BlockSpec((1, tm, size), lambda bi, i: (bi, i, 0)) for (_, _, size, _) in segs]
    return pl.pallas_call(
        functools.partial(_inproj_kernel, segs=segs, tm=tm, halo=halo),
        grid=(b, t // tm),
        in_specs=[pl.BlockSpec((1, tm, d), lambda bi, i: (bi, i, 0)),
                  pl.BlockSpec((1, halo, d), lambda bi, i: (bi, jnp.maximum(i * r - 1, 0), 0)),
                  pl.BlockSpec((1, halo, d), lambda bi, i: (bi, jnp.minimum((i + 1) * r, nblk - 1), 0)),
                  pl.BlockSpec((1, 6, d), mod_map),
                  pl.BlockSpec((1, d), const2),
                  _resident(w_t.shape),
                  _resident(wab_t.shape),
                  _resident(conv_w.shape),
                  _resident(hsum.shape)],
        out_specs=out_specs,
        out_shape=out_shape,
        compiler_params=_cparams(2),
        name="inproj",
    )(x, x, x, mod3, g_norm.reshape(1, d), w_t, wab_t, conv_w, hsum)


def _rope(x, cos, sin, lane):
    swapped = jnp.where((lane % 32) < 16, pltpu.roll(x, LANES - 16, 1), pltpu.roll(x, 16, 1))
    return x * cos + swapped * sin


def _attn_kernel(a_ref, kvc_ref, cos_ref, sin_ref, qg_ref, kg_ref, sink_ref, havg_ref,
                 o_ref, k_s, vt_s, *, seq, ctx, qb):
    n = pl.program_id(1)
    havg = havg_ref[...]
    prep_rows = ATTN_PREP_ROWS
    ctx_blocks = ctx // ATTN_BLOCK

    @pl.when(n == 0)
    def _prep():
        kg = kg_ref[...]
        kc = kvc_ref[0, :, 0:KV_WIDTH].astype(F32)
        ms = _dot((kc * kc).astype(BF16), havg)
        k_s[0:ctx, :] = (kc * lax.rsqrt(ms + EPS) * kg).astype(BF16)
        vc = kvc_ref[0, :, KV_WIDTH:2 * KV_WIDTH].astype(F32)
        for j in range(ctx_blocks):
            vt_s[j] = vc[j * ATTN_BLOCK:(j + 1) * ATTN_BLOCK, :].T.astype(BF16)
        lane = lax.broadcasted_iota(jnp.int32, (prep_rows, LANES), 1)
        for r0 in range(0, seq, prep_rows):
            kx = a_ref[0, r0:r0 + prep_rows, _OFF_AK:_OFF_AK + KV_WIDTH].astype(F32)
            ms = _dot((kx * kx).astype(BF16), havg)
            kx = kx * lax.rsqrt(ms + EPS) * kg
            kx = _rope(kx, cos_ref[r0:r0 + prep_rows, :], sin_ref[r0:r0 + prep_rows, :], lane)
            k_s[ctx + r0:ctx + r0 + prep_rows, :] = kx.astype(BF16)
            vx = a_ref[0, r0:r0 + prep_rows, _OFF_AV:_OFF_AV + KV_WIDTH].astype(F32)
            for j in range(prep_rows // ATTN_BLOCK):
                vt_s[ctx_blocks + r0 // ATTN_BLOCK + j] = vx[j * ATTN_BLOCK:(j + 1) * ATTN_BLOCK, :].T.astype(BF16)

    lane = lax.broadcasted_iota(jnp.int32, (ATTN_BLOCK, LANES), 1)
    qg = qg_ref[...]
    n_band = 3
    n_win = n_band * ATTN_BLOCK
    n_keys = n_win + ctx
    key_row = lax.broadcasted_iota(jnp.int32, (n_win, ATTN_BLOCK), 0)
    q_lane = lax.broadcasted_iota(jnp.int32, (n_win, ATTN_BLOCK), 1)
    zeros = jnp.zeros((HEAD_DIM, ATTN_BLOCK), BF16)
    groups = range(ATTN_KV_HEADS)
    sink_rows = [jnp.concatenate([sink_ref[g * ATTN_GROUP + r:g * ATTN_GROUP + r + 1, :] for r in range(ATTN_GROUP)],
                                 axis=1) * LOG2_E for g in groups]

    units = []
    k_alls, vt_alls, valids, rhss = [], [], [], {}
    for sb in range(qb):
        nq = n * qb + sb
        q0 = pl.multiple_of(nq * ATTN_BLOCK, ATTN_BLOCK)
        cos = cos_ref[pl.ds(q0, ATTN_BLOCK), :]
        sin = sin_ref[pl.ds(q0, ATTN_BLOCK), :]
        qts = []
        for j in range(ATTN_WIDTH // LANES):
            qx = a_ref[0, pl.ds(q0, ATTN_BLOCK), j * LANES:(j + 1) * LANES].astype(F32)
            ms = _dot((qx * qx).astype(BF16), havg)
            qx = _rope(qx * lax.rsqrt(ms + EPS) * qg, cos, sin, lane) * (HEAD_DIM ** -0.5 * LOG2_E)
            qts.append(qx.T.astype(BF16))
        blk0 = jnp.clip(nq - 1, 0, seq // ATTN_BLOCK - n_band)
        start = pl.multiple_of(blk0 * ATTN_BLOCK, ATTN_BLOCK)
        valid = jnp.abs(q0 + q_lane - (start + key_row)) <= WINDOW
        valids.append(jnp.concatenate([valid] * ATTN_GROUP, axis=1))
        k_alls.append(jnp.concatenate([k_s[pl.ds(ctx + start, n_win), :], k_s[0:ctx, :]], axis=0))
        vt_loc = vt_s[pl.ds(ctx_blocks + blk0, n_band)]
        vt_alls.append(jnp.concatenate([vt_loc[j] for j in range(n_band)] + [vt_s[j] for j in range(ctx_blocks)],
                                       axis=1))
        for g in groups:
            cols = []
            for r in range(ATTN_GROUP):
                h = g * ATTN_GROUP + r
                piece = qts[h // 2][(h % 2) * HEAD_DIM:(h % 2 + 1) * HEAD_DIM, :]
                cols.append(jnp.concatenate([piece, zeros] if g == 0 else [zeros, piece], axis=0))
            rhss[(sb, g)] = jnp.concatenate(cols, axis=1)
            units.append((sb, g))
        yield

    ss = []
    for sb, g in units:
        s = _dot(k_alls[sb], rhss[(sb, g)])
        ss.append(jnp.concatenate([jnp.where(valids[sb], s[0:n_win], NEG_INF), s[n_win:n_keys]], axis=0))
        yield
    p_bfs, invs = [], []
    for s, (sb, g) in zip(ss, units):
        m = jnp.maximum(jnp.max(s, axis=0, keepdims=True), sink_rows[g])
        p = jnp.exp2(s - m)
        invs.append(1.0 / (jnp.sum(p, axis=0, keepdims=True) + jnp.exp2(sink_rows[g] - m)))
        p_bfs.append(p.astype(BF16))
        yield
    pieces = {sb: [] for sb in range(qb)}
    for (sb, g), p_bf, inv in zip(units, p_bfs, invs):
        vt_g = vt_alls[sb][g * HEAD_DIM:(g + 1) * HEAD_DIM, :]
        for pair in range(ATTN_GROUP // 2):
            lanes = slice(pair * MXU_TILE, (pair + 1) * MXU_TILE)
            ot = _dot(vt_g, p_bf[:, lanes]) * inv[:, lanes]
            pieces[sb] += [ot[:, 0:ATTN_BLOCK], ot[:, ATTN_BLOCK:2 * ATTN_BLOCK]]
        yield
    for sb in range(qb):
        outs = [jnp.concatenate([pieces[sb][2 * j], pieces[sb][2 * j + 1]], axis=0).T
                for j in range(ATTN_Q_HEADS // 2)]
        o_ref[0, sb * ATTN_BLOCK:(sb + 1) * ATTN_BLOCK, :] = jnp.concatenate(outs, axis=1).astype(o_ref.dtype)


def _run_parts(parts, b, name):
    steps = parts[0]["steps"]
    assert all(p["steps"] == steps for p in parts)
    counts = [(len(p["args"]), len(p["out_shape"]), len(p["scratch_shapes"])) for p in parts]
    n_in = sum(c[0] for c in counts)
    n_out = sum(c[1] for c in counts)

    def kern(*refs):
        i_pos, o_pos, s_pos = 0, n_in, n_in + n_out
        bodies = []
        for p, (ci, co, cs) in zip(parts, counts):
            bodies.append(p["kernel"](*refs[i_pos:i_pos + ci], *refs[o_pos:o_pos + co], *refs[s_pos:s_pos + cs]))
            i_pos, o_pos, s_pos = i_pos + ci, o_pos + co, s_pos + cs
        while bodies:
            for body in list(bodies):
                try:
                    next(body)
                except StopIteration:
                    bodies.remove(body)

    outs = pl.pallas_call(
        kern,
        grid=(b, steps),
        in_specs=[s for p in parts for s in p["in_specs"]],
        out_specs=[s for p in parts for s in p["out_specs"]],
        out_shape=[s for p in parts for s in p["out_shape"]],
        scratch_shapes=[s for p in parts for s in p["scratch_shapes"]],
        compiler_params=_cparams(2),
        name=name,
    )(*[a for p in parts for a in p["args"]])
    res, pos = [], 0
    for _, co, _ in counts:
        res.append(outs[pos:pos + co])
        pos += co
    return res


def _attn_parts(a_lat, kv_ctx, cos, sin, qg, kg, sink, havg, qb):
    b, s, wa = a_lat.shape
    ctx = kv_ctx.shape[1]
    nb = s // ATTN_BLOCK
    const2 = lambda bi, n: (0, 0)
    return dict(
        kernel=functools.partial(_attn_kernel, seq=s, ctx=ctx, qb=qb),
        steps=nb // qb,
        in_specs=[pl.BlockSpec((1, s, wa), lambda bi, n: (bi, 0, 0)),
                  pl.BlockSpec((1, ctx, 2 * KV_WIDTH), lambda bi, n: (bi, 0, 0)),
                  pl.BlockSpec((s, LANES), const2),
                  pl.BlockSpec((s, LANES), const2),
                  pl.BlockSpec((1, LANES), const2),
                  pl.BlockSpec((1, LANES), const2),
                  pl.BlockSpec((ATTN_Q_HEADS, LANES), const2),
                  pl.BlockSpec((LANES, LANES), const2)],
        out_specs=[pl.BlockSpec((1, qb * ATTN_BLOCK, ATTN_WIDTH), lambda bi, n: (bi, n, 0))],
        out_shape=[jax.ShapeDtypeStruct((b, s, ATTN_WIDTH), BF16)],
        scratch_shapes=[pltpu.VMEM((ctx + s, KV_WIDTH), BF16),
                        pltpu.VMEM(((ctx + s) // ATTN_BLOCK, KV_WIDTH, ATTN_BLOCK), BF16)],
        args=[a_lat, kv_ctx, cos, sin, qg, kg, sink, havg])


def _half_mask():
    return lax.broadcasted_iota(jnp.int32, (CHUNK, LANES), 1) < HEAD_DIM


def _bdiag(x_lane, lo_half):
    zeros = jnp.zeros((CHUNK, LANES), x_lane.dtype)
    blocks = []
    for h in range(HEADS_PER_TILE):
        col = (h * HEAD_DIM) // LANES
        piece = x_lane[:, col * LANES:(col + 1) * LANES]
        piece = jnp.where(lo_half if (h * HEAD_DIM) % LANES == 0 else jnp.logical_not(lo_half), piece, zeros)
        blocks.append(jnp.concatenate([piece if c == col else zeros for c in range(MXU_TILE // LANES)], axis=1))
    return jnp.concatenate(blocks, axis=0)


def _unit_tri_inverses(a_list, eye_l, sub_mask, lo_half):
    ads = [jnp.where(sub_mask, a, 0.0) for a in a_list]
    ys = [jnp.where(sub_mask, 0.0, a) for a in a_list]
    ps = [eye_l - ad for ad in ads]
    pws = [ad.astype(BF16) for ad in ads]
    n_levels = int(np.log2(DN_SUB))
    c1, c2, c3 = CHUNK, 2 * CHUNK, 3 * CHUNK
    for level in range(n_levels):
        rhss = [_bdiag(pw, lo_half) for pw in pws]
        if level == 0:
            boths = [_dot(jnp.concatenate([pw, y.astype(BF16)], axis=0), rhs) for pw, y, rhs in zip(pws, ys, rhss)]
            pws = [both[0:c1].astype(BF16) for both in boths]
            ys = [y - both[c1:c2] for y, both in zip(ys, boths)]
        elif level < n_levels - 1:
            boths = [_dot(jnp.concatenate([pw, p.astype(BF16), y.astype(BF16)], axis=0), rhs)
                     for pw, p, y, rhs in zip(pws, ps, ys, rhss)]
            pws = [both[0:c1].astype(BF16) for both in boths]
            ps = [p + both[c1:c2] for p, both in zip(ps, boths)]
            ys = [y + both[c2:c3] for y, both in zip(ys, boths)]
        else:
            boths = [_dot(jnp.concatenate([p.astype(BF16), y.astype(BF16)], axis=0), rhs)
                     for p, y, rhs in zip(ps, ys, rhss)]
            ps = [p + both[0:c1] for p, both in zip(ps, boths)]
            ys = [y + both[c1:c2] for y, both in zip(ys, boths)]
        yield
    assert CHUNK // DN_SUB == 4
    b_bfs = [y.astype(BF16) for y in ys]
    boths = [_dot(jnp.concatenate([b, p.astype(BF16)], axis=0), _bdiag(b, lo_half)) for b, p in zip(b_bfs, ps)]
    yield
    zs = [p - both[c1:c2] for p, both in zip(ps, boths)]
    return [z + _dot(z.astype(BF16), _bdiag(both[0:c1].astype(BF16), lo_half)) for z, both in zip(zs, boths)]


def _dn_factors(abs_, d, arow, dtrow, exp2_ref, tri_ref, eye_t):
    def hi_lo(v):
        hi = v.astype(BF16)
        return hi, (v - hi.astype(F32)).astype(BF16)

    lane = lax.broadcasted_iota(jnp.int32, abs_[0].shape, 1)
    is_g = (lane % 32) < 16
    xhls = []
    for ab in abs_:
        z = ab + dtrow
        softplus = jnp.maximum(z, 0.0) + jnp.log(1.0 + jnp.exp(-jnp.abs(z)))
        x = jnp.where(is_g, -arow * softplus, _sigmoid(ab))
        x = jnp.where(is_g, _dot(tri_ref[d], jnp.concatenate(hi_lo(x), axis=0)), x)
        x_hi, x_lo = hi_lo(x)
        xhls.append(jnp.where(lane < 32, x_hi, x_lo)[:, 0:64])
    y = _dot(jnp.concatenate(xhls, axis=0), exp2_ref[d])
    out = []
    for c in range(len(abs_)):
        gi = y[c * CHUNK:(c + 1) * CHUNK, 0:DN_WIDTH]
        be = y[c * CHUNK:(c + 1) * CHUNK, DN_WIDTH:2 * DN_WIDTH]
        gj = jnp.sum(gi * eye_t, axis=0, keepdims=True)
        out.append((be, gi, jnp.broadcast_to(gj, gi.shape)))
    return out


def _block_mask():
    return (lax.broadcasted_iota(jnp.int32, (MXU_TILE, MXU_TILE), 0) // HEAD_DIM
            == lax.broadcasted_iota(jnp.int32, (MXU_TILE, MXU_TILE), 1) // HEAD_DIM)


def _dnprep_kernel(*refs, cb, want_o):
    if want_o:
        q_ref, k_ref, v_ref, ab_ref = refs[:4]
        refs = refs[4:]
    else:
        q_ref = None
        k_ref, v_ref, ab_ref = refs[:3]
        refs = refs[3:]
    arow_ref, dtrow_ref, exp2_ref, tri_ref, w_o, uv_o, kt_o, dl_o = refs[:8]
    qd_o, in_o = refs[8:10] if want_o else (None, None)
    n_tiles = DN_WIDTH // MXU_TILE
    row = lax.broadcasted_iota(jnp.int32, (CHUNK, MXU_TILE), 0)
    colj = lax.broadcasted_iota(jnp.int32, (CHUNK, MXU_TILE), 1) % HEAD_DIM
    eye_l = (row == colj).astype(F32)
    eye_t = jnp.concatenate([eye_l] * n_tiles, axis=1)
    lo_half = _half_mask()
    arow = arow_ref[...]
    dtrow = dtrow_ref[...]

    fac = {}
    shared = {}
    abs_ = [ab_ref[0, c * CHUNK:(c + 1) * CHUNK, :] for c in range(cb)]
    for d in range(N_DIR):
        for c, f in enumerate(_dn_factors(abs_, d, arow, dtrow, exp2_ref, tri_ref, eye_t)):
            fac[(c, d)] = f
    yield
    for c in range(cb):
        rows = slice(c * CHUNK, (c + 1) * CHUNK)
        for g in range(n_tiles):
            lanes = slice(g * MXU_TILE, (g + 1) * MXU_TILE)
            k_l = k_ref[0, rows, lanes]
            kbd = _bdiag(k_l, lo_half)
            if want_o:
                kq = _dot_nt(jnp.concatenate([k_l, q_ref[0, rows, lanes]], axis=0), kbd)
                shared[(c, g)] = (kq[0:CHUNK], kq[CHUNK:2 * CHUNK])
            else:
                shared[(c, g)] = (_dot_nt(k_l, kbd), None)
        yield

    units = [(c, g, d) for c in range(cb) for g in range(n_tiles) for d in range(N_DIR)]
    decs = []
    for c, g, d in units:
        lanes = slice(g * MXU_TILE, (g + 1) * MXU_TILE)
        be, gi, gj = fac[(c, d)]
        lower = (row > colj) if d == 0 else (row < colj)
        decs.append(jnp.where(lower, jnp.exp(jnp.where(lower, gi[:, lanes] - gj[:, lanes], 0.0)), 0.0))
    a_list = [fac[(c, d)][0][:, g * MXU_TILE:(g + 1) * MXU_TILE] * dec * shared[(c, g)][0]
              for (c, g, d), dec in zip(units, decs)]
    yield
    tinvs = yield from _unit_tri_inverses(a_list, eye_l, (row // DN_SUB) == (colj // DN_SUB), lo_half)

    for idx, ((c, g, d), dec, tinv) in enumerate(zip(units, decs, tinvs)):
        if idx % (n_tiles * N_DIR) == 0:
            yield
        rows = slice(c * CHUNK, (c + 1) * CHUNK)
        lanes = slice(g * MXU_TILE, (g + 1) * MXU_TILE)
        be, gi, _ = fac[(c, d)]
        be, gi = be[:, lanes], gi[:, lanes]
        last = CHUNK - 1 if d == 0 else 0
        e_g = jnp.exp(gi)
        gl_row = gi[last:last + 1, :]
        kf = k_ref[0, rows, lanes].astype(F32)
        vf = v_ref[0, rows, lanes].astype(F32)
        rhs = jnp.concatenate([_bdiag((be * e_g * kf).astype(BF16), lo_half), _bdiag((be * vf).astype(BF16), lo_half)],
                              axis=1)
        wu = _dot(tinv.astype(BF16), rhs)
        w_o[0, d, rows, lanes] = wu[:, 0:MXU_TILE].astype(BF16)
        uv_o[0, d, rows, lanes] = wu[:, MXU_TILE:2 * MXU_TILE].astype(BF16)
        kt_o[0, d, rows, lanes] = (jnp.exp(gl_row - gi) * kf).astype(BF16)
        dl_o[0, d, c, :, lanes] = jnp.exp(gl_row)
        if want_o:
            qd_o[0, d, rows, lanes] = (e_g * q_ref[0, rows, lanes].astype(F32)).astype(BF16)
            in_o[0, d, rows, lanes] = ((dec + eye_l) * shared[(c, g)][1]).astype(BF16)


def _dnprep_parts(q, k, v, ab, arow, dtrow, exp2, tri, cb):
    want_o = q is not None
    b, t, w = k.shape
    tb = cb * CHUNK
    tok = lambda bi, i: (bi, i, 0)
    const2 = lambda bi, i: (0, 0)
    const3 = lambda bi, i: (0, 0, 0)
    dir_tok = lambda bi, i: (bi, 0, i, 0)
    data = ([q] if want_o else []) + [k, v]
    in_specs = ([pl.BlockSpec((1, tb, w), tok)] * len(data) + [pl.BlockSpec((1, tb, LANES), tok),
                pl.BlockSpec((1, LANES), const2), pl.BlockSpec((1, LANES), const2),
                pl.BlockSpec(exp2.shape, const3), pl.BlockSpec(tri.shape, const3)])
    big = lambda dt: jax.ShapeDtypeStruct((b, N_DIR, t, w), dt)
    big_spec = pl.BlockSpec((1, N_DIR, tb, w), dir_tok)
    out_shape = [big(BF16), big(BF16), big(BF16), jax.ShapeDtypeStruct((b, N_DIR, t // CHUNK, 1, w), F32)]
    out_specs = [big_spec, big_spec, big_spec, pl.BlockSpec((1, N_DIR, cb, 1, w), lambda bi, i: (bi, 0, i, 0, 0))]
    if want_o:
        out_shape += [big(BF16), big(BF16)]
        out_specs += [big_spec, big_spec]
    return dict(kernel=functools.partial(_dnprep_kernel, cb=cb, want_o=want_o), steps=t // tb, in_specs=in_specs,
                out_specs=out_specs, out_shape=out_shape, scratch_shapes=[],
                args=data + [ab, arow, dtrow, exp2, tri])


def _dnscan_kernel(*refs, n_chunk, bb, want_o, have_s0, want_s):
    n_in = 6 if want_o else 4
    dir_refs = [refs[0:n_in], refs[n_in:2 * n_in]]
    pos = 2 * n_in
    s0_ref = refs[pos] if have_s0 else None
    pos += int(have_s0)
    o_refs = refs[pos:pos + N_DIR] if want_o else None
    pos += N_DIR if want_o else 0
    sout_ref = refs[pos] if want_s else None
    pos += int(want_s)
    s_scr = refs[pos]
    n_tiles = DN_WIDTH // MXU_TILE
    i = pl.program_id(1)

    @pl.when(i == 0)
    def _init():
        if have_s0:
            s_scr[...] = s0_ref[...]
        else:
            s_scr[...] = jnp.zeros_like(s_scr)

    bmask = _block_mask()
    lo_half = _half_mask()
    chains = [(bi, d, g) for bi in range(bb) for d in range(N_DIR) for g in range(n_tiles)]

    def body(j, carry):
        cidx = (j, n_chunk - 1 - j)
        r0s = [pl.multiple_of(cidx[d] * CHUNK, CHUNK) for d in range(N_DIR)]
        s_olds, r1s = [], []
        for bi, d, g in chains:
            lanes = slice(g * MXU_TILE, (g + 1) * MXU_TILE)
            w = dir_refs[d][0][bi, 0, pl.ds(r0s[d], CHUNK), lanes]
            if want_o:
                w = jnp.concatenate([w, dir_refs[d][4][bi, 0, pl.ds(r0s[d], CHUNK), lanes]], axis=0)
            s_old = s_scr[bi, d * n_tiles + g]
            s_olds.append(s_old)
            r1s.append(_dot(w, s_old.astype(BF16)))
        u_bfs = []
        for (bi, d, g), r1 in zip(chains, r1s):
            lanes = slice(g * MXU_TILE, (g + 1) * MXU_TILE)
            u_bfs.append((dir_refs[d][1][bi, 0, pl.ds(r0s[d], CHUNK), lanes] - r1[0:CHUNK]).astype(BF16))
        for (bi, d, g), r1, u_bf, s_old in zip(chains, r1s, u_bfs, s_olds):
            lanes = slice(g * MXU_TILE, (g + 1) * MXU_TILE)
            kt = dir_refs[d][2][bi, 0, pl.ds(r0s[d], CHUNK), lanes]
            ds = jnp.where(bmask, _dot_tn(kt, u_bf), 0.0)
            dl = dir_refs[d][3][bi, 0, cidx[d]][:, lanes]
            s_scr[bi, d * n_tiles + g] = s_old * dl + ds
            if want_o:
                intra = dir_refs[d][5][bi, 0, pl.ds(r0s[d], CHUNK), lanes]
                o = r1[CHUNK:2 * CHUNK] + _dot(intra, _bdiag(u_bf, lo_half))
                o_refs[d][bi, pl.ds(r0s[d], CHUNK), lanes] = o.astype(BF16)
        return carry

    lax.fori_loop(0, n_chunk, body, 0)

    if want_s:
        @pl.when(i == pl.num_programs(1) - 1)
        def _fin():
            sout_ref[...] = s_scr[...]


def _dnscan(prep, s0, tb, bb, want_s):
    want_o = len(prep) == 6
    b, _, t, w = prep[0].shape
    n_t = t // tb
    n_chunk = tb // CHUNK
    n_chain = N_DIR * (w // MXU_TILE)

    def specs(d):
        blk = (lambda bi, i: i) if d == 0 else (lambda bi, i: n_t - 1 - i)
        big = pl.BlockSpec((bb, 1, tb, w), lambda bi, i: (bi, d, blk(bi, i), 0))
        dl = pl.BlockSpec((bb, 1, n_chunk, 1, w), lambda bi, i: (bi, d, blk(bi, i), 0, 0))
        return [big, big, big, dl] + ([big, big] if want_o else [])

    in_specs = specs(0) + specs(1)
    args = list(prep) + list(prep)
    state_spec = pl.BlockSpec((bb, n_chain, MXU_TILE, MXU_TILE), lambda bi, i: (bi, 0, 0, 0))
    if s0 is not None:
        in_specs.append(state_spec)
        args.append(s0)
    out_shape, out_specs = [], []
    if want_o:
        out_shape += [jax.ShapeDtypeStruct((b, t, w), BF16)] * N_DIR
        out_specs += [pl.BlockSpec((bb, tb, w), lambda bi, i: (bi, i, 0)),
                      pl.BlockSpec((bb, tb, w), lambda bi, i: (bi, n_t - 1 - i, 0))]
    if want_s:
        out_shape.append(jax.ShapeDtypeStruct((b, n_chain, MXU_TILE, MXU_TILE), F32))
        out_specs.append(state_spec)
    return pl.pallas_call(
        functools.partial(_dnscan_kernel, n_chunk=n_chunk, bb=bb, want_o=want_o, have_s0=s0 is not None,
                          want_s=want_s),
        grid=(b // bb, n_t),
        in_specs=in_specs,
        out_specs=out_specs,
        out_shape=out_shape,
        scratch_shapes=[pltpu.VMEM((bb, n_chain, MXU_TILE, MXU_TILE), F32)],
        compiler_params=_cparams(2),
        name="dnscan",
    )(*args)


def _tail_kernel(x_ref, mod_ref, ya_ref, odf_ref, odb_ref, z_ref, gate_ref, dng_ref, havg_ref, wba_ref, wbd_ref,
                 wo_ref, gn2_ref, w1_ref, w2_ref, o_ref, *, ff_chunk):
    havg = havg_ref[...]
    dng = dng_ref[...]
    yd_parts = []
    for j in range(DN_WIDTH // MXU_TILE):
        sl = slice(j * MXU_TILE, (j + 1) * MXU_TILE)
        od = odf_ref[0, :, sl].astype(F32) + odb_ref[0, :, sl].astype(F32)
        ms = _dot((od * od).astype(BF16), havg)
        z = z_ref[0, :, sl].astype(F32)
        yd_parts.append((od * lax.rsqrt(ms + EPS) * dng * (z * _sigmoid(z))).astype(BF16))
    yd = jnp.concatenate(yd_parts, axis=1)
    ga = gate_ref[0, :, 0:D_MODEL].astype(F32)
    gd = gate_ref[0, :, D_MODEL:2 * D_MODEL].astype(F32)
    y = _sigmoid(ga) * _dot(ya_ref[0], wba_ref[...]) + _sigmoid(gd) * _dot(yd, wbd_ref[...])
    br = _dot(y.astype(BF16), wo_ref[...])
    mod = mod_ref[0]
    out1 = x_ref[0] + mod[2:3] * br
    ms2 = jnp.mean(out1 * out1, axis=-1, keepdims=True)
    hm = (out1 * lax.rsqrt(ms2 + EPS) * (gn2_ref[...] * (1.0 + mod[4:5])) + mod[3:4]).astype(BF16)
    acc = None
    for j in range(D_FF // ff_chunk):
        a = jnp.maximum(_dot(hm, w1_ref[:, j * ff_chunk:(j + 1) * ff_chunk]), 0.0)
        part = _dot((a * a).astype(BF16), w2_ref[j * ff_chunk:(j + 1) * ff_chunk, :])
        acc = part if acc is None else acc + part
    o_ref[0] = out1 + mod[5:6] * acc


def _resident(shape):
    return pl.BlockSpec(shape, lambda bi, i: (0,) * len(shape), pipeline_mode=pl.Buffered(1))


def _tail(x, mod3, y_attn, o_df, o_db, z, gates, dng, havg, wba, wbd, wo, gn2, w1, w2, tm):
    b, t, d = x.shape
    tok = lambda bi, i: (bi, i, 0)
    return pl.pallas_call(
        functools.partial(_tail_kernel, ff_chunk=MLP_FF_CHUNK),
        grid=(b, t // tm),
        in_specs=[pl.BlockSpec((1, tm, d), tok),
                  pl.BlockSpec((1, 6, d), lambda bi, i: (bi, 0, 0)),
                  pl.BlockSpec((1, tm, ATTN_WIDTH), tok),
                  pl.BlockSpec((1, tm, DN_WIDTH), tok),
                  pl.BlockSpec((1, tm, DN_WIDTH), tok),
                  pl.BlockSpec((1, tm, DN_WIDTH), tok),
                  pl.BlockSpec((1, tm, 2 * d), tok),
                  _resident((1, MXU_TILE)),
                  _resident((MXU_TILE, MXU_TILE)),
                  _resident(wba.shape),
                  _resident(wbd.shape),
                  _resident(wo.shape),
                  _resident((1, d)),
                  _resident(w1.shape),
                  _resident(w2.shape)],
        out_specs=pl.BlockSpec((1, tm, d), tok),
        out_shape=jax.ShapeDtypeStruct((b, t, d), F32),
        compiler_params=_cparams(2),
        name="tail",
    )(x, mod3, y_attn, o_df, o_db, z, gates, dng, havg, wba, wbd, wo, gn2, w1, w2)


def _head_avg(n, scale):
    idx = np.arange(n) // HEAD_DIM
    return jnp.asarray((idx[:, None] == idx[None, :]).astype(np.float32) * scale, BF16)


def _dn_expand_matrix():
    n = N_DIR * DN_HEADS
    m = np.zeros((N_DIR, 4 * n, 2 * DN_WIDTH), np.float32)
    for d in range(N_DIR):
        for part in range(2):
            for h in range(DN_HEADS):
                idx = d * DN_HEADS + h
                m[d, part * 2 * n + idx, h * HEAD_DIM:(h + 1) * HEAD_DIM] = 1.0
                m[d, part * 2 * n + n + idx, DN_WIDTH + h * HEAD_DIM:DN_WIDTH + (h + 1) * HEAD_DIM] = 1.0
    return jnp.asarray(m, BF16)


def _tri_matrices():
    i = np.arange(CHUNK)
    low = (i[:, None] >= i[None, :]).astype(np.float32)
    up = (i[:, None] <= i[None, :]).astype(np.float32)
    return jnp.asarray(np.stack([np.concatenate([low, low], axis=1), np.concatenate([up, up], axis=1)]), BF16)


def _rope_tables(seq):
    half = HEAD_DIM // 2
    n_freq = half // 2
    freqs = ROPE_BASE ** (-jnp.arange(n_freq, dtype=F32) / n_freq)
    pos = jnp.arange(seq)
    ang_r = (pos // GRID_W).astype(F32)[:, None] * freqs
    ang_c = (pos % GRID_W).astype(F32)[:, None] * freqs
    cos = jnp.concatenate([jnp.cos(ang_r)] * 2 + [jnp.cos(ang_c)] * 2, axis=1)
    sin = jnp.concatenate([-jnp.sin(ang_r), jnp.sin(ang_r), -jnp.sin(ang_c), jnp.sin(ang_c)], axis=1)
    reps = LANES // HEAD_DIM
    return jnp.tile(cos, (1, reps)), jnp.tile(sin, (1, reps))


def _pad_cols(w, n):
    return jnp.pad(w, ((0, 0), (0, n - w.shape[1])))


def kernel(x, c, ctx, c_ctx, w_ada, b_ada, g_norm1, w_in, q_norm_g, k_norm_g, attn_sink, conv_w, a_log, dt_bias,
           dn_norm_g, w_br_attn, w_br_dn, w_out, g_norm2, w_mlp1, w_mlp2):
    depth = w_ada.shape[0]
    assert depth == 1, "single-layer trunk only"
    b, s, d = x.shape
    n_ctx = ctx.shape[1]
    assert d == D_MODEL and w_in.shape[-1] == _IN_WIDTH
    assert s >= 3 * ATTN_BLOCK and s % ATTN_BLOCK == 0 and s % CHUNK == 0 and n_ctx % CHUNK == 0
    out_dtype = x.dtype
    w_in_t = jnp.swapaxes(w_in[0], 0, 1).astype(BF16)

    mod_rows = 16
    cc = jnp.concatenate([c.astype(F32), c_ctx.astype(F32)[None], jnp.zeros((mod_rows - b - 1, d), F32)], axis=0)
    mod = _ada(cc, w_ada[0], b_ada[0])
    mod3 = mod.reshape(mod_rows, 6, d)

    ab_rows = w_in_t[_OFF_DA:_OFF_GA]
    wab_t = jnp.concatenate([ab_rows, ab_rows, jnp.zeros((LANES - 2 * (_OFF_GA - _OFF_DA), d), BF16)], axis=0)
    hsum = _head_avg(MXU_TILE, 1.0)
    segs_lat = ((0, 0, _OFF_DQ, None),
                (0, _OFF_DQ, DN_WIDTH, (0, True, True)), (0, _OFF_DK, DN_WIDTH, (DN_WIDTH, True, False)),
                (0, _OFF_DV, DN_WIDTH, (2 * DN_WIDTH, False, False)),
                (0, _OFF_DZ, DN_WIDTH, None), (0, _OFF_GA, 2 * D_MODEL, None), (1, 0, LANES, None))
    a_lat, q_d, k_d, v_d, z_lat, gates, ab_lat = _inproj(x, mod3, None, g_norm1[0], w_in_t, wab_t, conv_w[0], hsum,
                                                         segs_lat, (BF16, BF16, BF16, BF16, BF16, BF16, F32),
                                                         tm=2 * TOKEN_TILE)
    segs_ctx = ((0, _OFF_AK, 2 * KV_WIDTH, None),
                (0, _OFF_DK, DN_WIDTH, (0, True, False)), (0, _OFF_DV, DN_WIDTH, (DN_WIDTH, False, False)),
                (1, 0, LANES, None))
    kv_ctx, k_dc, v_dc, ab_ctx = _inproj(ctx, mod3, b, g_norm1[0], w_in_t, wab_t, conv_w[0][:, DN_WIDTH:], hsum,
                                         segs_ctx, (BF16, BF16, BF16, F32), tm=n_ctx)

    cos, sin = _rope_tables(s)
    reps = LANES // HEAD_DIM
    n_gate = N_DIR * DN_HEADS
    arow = _pad_cols(jnp.tile(jnp.concatenate([jnp.exp(a_log[0]).reshape(1, n_gate), jnp.zeros((1, n_gate), F32)],
                                              axis=1), (1, 2)), LANES)
    dtrow = _pad_cols(jnp.tile(jnp.concatenate([dt_bias[0].reshape(1, n_gate), jnp.zeros((1, n_gate), F32)],
                                               axis=1), (1, 2)), LANES)
    exp2, tri = _dn_expand_matrix(), _tri_matrices()
    dn_cb = DN_CHUNKS_PER_STEP
    qb = (s // ATTN_BLOCK) // (s // (dn_cb * CHUNK))
    attn_parts = _attn_parts(a_lat, kv_ctx, cos, sin,
                             jnp.tile(q_norm_g[0].astype(F32), reps)[None],
                             jnp.tile(k_norm_g[0].astype(F32), reps)[None],
                             jnp.broadcast_to(attn_sink[0].astype(F32)[:, None], (ATTN_Q_HEADS, LANES)),
                             _head_avg(LANES, 1.0 / HEAD_DIM), qb)
    (y_attn,), prep_lat = _run_parts(
        [attn_parts, _dnprep_parts(q_d, k_d, v_d, ab_lat, arow, dtrow, exp2, tri, dn_cb)], b, "attn_dnprep")
    (prep_ctx,) = _run_parts([_dnprep_parts(None, k_dc, v_dc, ab_ctx, arow, dtrow, exp2, tri, dn_cb)], b, "dnprep")

    bb = max(q for q in range(1, SCAN_BATCH_ROWS + 1) if b % q == 0)
    (s_ctx,) = _dnscan(prep_ctx, None, tb=n_ctx, bb=bb, want_s=True)
    o_df, o_db = _dnscan(prep_lat, s_ctx, tb=SCAN_TOKEN_TILE, bb=bb, want_s=False)

    out = _tail(x, mod3, y_attn, o_df, o_db, z_lat, gates,
                jnp.tile(dn_norm_g[0].astype(F32), HEADS_PER_TILE)[None], _head_avg(MXU_TILE, 1.0 / HEAD_DIM),
                w_br_attn[0].astype(BF16), w_br_dn[0].astype(BF16), w_out[0].astype(BF16),
                g_norm2[0].reshape(1, d), w_mlp1[0].astype(BF16), w_mlp2[0].astype(BF16), tm=TOKEN_TILE)
    return out.astype(out_dtype)
```

```python
import functools

import numpy as np
import jax
import jax.numpy as jnp
from jax import lax
from jax.experimental import pallas as pl
from jax.experimental.pallas import tpu as pltpu

F32 = jnp.float32
BF16 = jnp.bfloat16

D_MODEL = 1024
GRID_W = 64
HEAD_DIM = 64
ATTN_Q_HEADS = 8
ATTN_KV_HEADS = 2
ATTN_GROUP = ATTN_Q_HEADS // ATTN_KV_HEADS
WINDOW = 128
ATTN_BLOCK = 128
ROPE_BASE = 10000.0
DN_HEADS = 8
CONV_W = 3
CHUNK = 64
N_DIR = 2
D_FF = 4 * D_MODEL
EPS = 1e-6
NEG_INF = -1e30
LOG2_E = float(np.log2(np.e))

ATTN_WIDTH = ATTN_Q_HEADS * HEAD_DIM
KV_WIDTH = ATTN_KV_HEADS * HEAD_DIM
DN_WIDTH = DN_HEADS * HEAD_DIM
LANES = 128
MXU_TILE = 256
HEADS_PER_TILE = MXU_TILE // HEAD_DIM
DN_SUB = 16
VMEM_LIMIT = 56 * 1024 * 1024

TOKEN_TILE = 512
ADA_COL_TILE = 1536
MLP_FF_CHUNK = 1024
ATTN_PREP_ROWS = 256
DN_CHUNKS_PER_STEP = 4
SCAN_TOKEN_TILE = 256
SCAN_BATCH_ROWS = 4

_OFF_AQ = 0
_OFF_AK = _OFF_AQ + ATTN_WIDTH
_OFF_AV = _OFF_AK + KV_WIDTH
_OFF_DQ = _OFF_AV + KV_WIDTH
_OFF_DK = _OFF_DQ + DN_WIDTH
_OFF_DV = _OFF_DK + DN_WIDTH
_OFF_DZ = _OFF_DV + DN_WIDTH
_OFF_DA = _OFF_DZ + DN_WIDTH
_OFF_DB = _OFF_DA + N_DIR * DN_HEADS
_OFF_GA = _OFF_DB + N_DIR * DN_HEADS
_OFF_GD = _OFF_GA + D_MODEL
_IN_WIDTH = _OFF_GD + D_MODEL


def _sigmoid(x):
    return 0.5 * jnp.tanh(0.5 * x) + 0.5


def _dot(a, b):
    return jnp.dot(a, b, preferred_element_type=F32)


def _dot_nt(a, b):
    return lax.dot_general(a, b, (((1,), (1,)), ((), ())), preferred_element_type=F32)


def _dot_tn(a, b):
    return lax.dot_general(a, b, (((0,), (0,)), ((), ())), preferred_element_type=F32)


def _cparams(n_axes):
    return pltpu.CompilerParams(dimension_semantics=("arbitrary",) * n_axes, vmem_limit_bytes=VMEM_LIMIT)


def _ada_kernel(c_ref, w_ref, b_ref, o_ref):
    c = c_ref[...]
    s = c * _sigmoid(c)
    o_ref[...] = _dot(s.astype(BF16), w_ref[...].astype(BF16)) + b_ref[...]


def _ada(cc, w_ada, b_ada):
    rows, d = cc.shape
    n = w_ada.shape[1]
    tn = ADA_COL_TILE
    return pl.pallas_call(
        _ada_kernel,
        grid=(n // tn,),
        in_specs=[pl.BlockSpec((rows, d), lambda j: (0, 0)),
                  pl.BlockSpec((d, tn), lambda j: (0, j)),
                  pl.BlockSpec((1, tn), lambda j: (0, j))],
        out_specs=pl.BlockSpec((rows, tn), lambda j: (0, j)),
        out_shape=jax.ShapeDtypeStruct((rows, n), F32),
        compiler_params=_cparams(1),
        name="ada",
    )(cc, w_ada, b_ada.reshape(1, n))


def _inproj_kernel(x_ref, xprev_ref, xnext_ref, mod_ref, g_ref, wt_ref, wabt_ref, cw_ref, hsum_ref, *out_refs,
                   segs, tm, halo):
    w_refs = (wt_ref, wabt_ref)
    i = pl.program_id(1)
    last = pl.num_programs(1) - 1
    mod = mod_ref[0]
    scale = g_ref[...] * (1.0 + mod[1:2])

    def norm_mod(v):
        ms = jnp.mean(v * v, axis=-1, keepdims=True)
        return (v * lax.rsqrt(ms + EPS) * scale + mod[0:1]).astype(BF16)

    h = norm_mod(x_ref[0])
    h_ext = jnp.concatenate([h, norm_mod(jnp.concatenate([xprev_ref[0], xnext_ref[0]], axis=0))], axis=0)
    keep_prev = (i > 0).astype(F32)
    keep_next = (i < last).astype(F32)
    rows = lax.broadcasted_iota(jnp.int32, (tm, MXU_TILE), 0)
    hsum = hsum_ref[...]
    def plain_piece(o_ref, src, start, lo, width):
        o_ref[0, :, lo:lo + width] = _dot_nt(h, w_refs[src][start + lo:start + lo + width, :]).astype(o_ref.dtype)

    def conv_piece(o_ref, src, start, lo, kind):
        conv_col, do_norm, is_q = kind
        w_rows = w_refs[src][start + lo:start + lo + MXU_TILE, :]
        p_ext = _dot_nt(h_ext, w_rows)
        p, p_halo = p_ext[0:tm], p_ext[tm:tm + 2 * halo]
        yield
        p_prev = jnp.where(rows == 0, p_halo[halo - 1:halo] * keep_prev, pltpu.roll(p, 1, 0))
        p_next = jnp.where(rows == tm - 1, p_halo[halo:halo + 1] * keep_next, pltpu.roll(p, tm - 1, 0))
        cw = cw_ref[:, conv_col + lo:conv_col + lo + MXU_TILE]
        y = p_prev * cw[0:1] + p * cw[1:2] + p_next * cw[2:3]
        y = y * _sigmoid(y)
        if do_norm:
            y = y * lax.rsqrt(_dot((y * y).astype(BF16), hsum) + EPS)
            if is_q:
                y = y * (HEAD_DIM ** -0.5)
        o_ref[0, :, lo:lo + MXU_TILE] = y.astype(o_ref.dtype)

    plain, conv = [], []
    for o_ref, (src, start, size, kind) in zip(out_refs, segs):
        if kind is None:
            plain += [functools.partial(plain_piece, o_ref, src, start, lo, min(MXU_TILE, size - lo))
                      for lo in range(0, size, MXU_TILE)]
        else:
            conv += [functools.partial(conv_piece, o_ref, src, start, lo, kind) for lo in range(0, size, MXU_TILE)]
    pending = None
    while plain or conv or pending is not None:
        started = conv.pop(0)() if conv else None
        if started is not None:
            next(started)
        if pending is not None:
            for _ in pending:
                pass
        pending = started
        for _ in range(-(-len(plain) // (len(conv) + 1)) if plain else 0):
            plain.pop(0)()


def _inproj(x, mod3, mod_row, g_norm, w_t, wab_t, conv_w, hsum, segs, dtypes, tm):
    b, t, d = x.shape
    halo = 8
    r = tm // halo
    nblk = t // halo
    if mod_row is None:
        mod_map = lambda bi, i: (bi, 0, 0)
    else:
        mod_map = lambda bi, i: (mod_row, 0, 0)
    const2 = lambda bi, i: (0, 0)
    out_shape = [jax.ShapeDtypeStruct((b, t, size), dt) for (_, _, size, _), dt in zip(segs, dtypes)]
    out_specs = [pl.BlockSpec((1, tm, size), lambda bi, i: (bi, i, 0)) for (_, _, size, _) in segs]
    return pl.pallas_call(
        functools.partial(_inproj_kernel, segs=segs, tm=tm, halo=halo),
        grid=(b, t // tm),
        in_specs=[pl.BlockSpec((1, tm, d), lambda bi, i: (bi, i, 0)),
                  pl.BlockSpec((1, halo, d), lambda bi, i: (bi, jnp.maximum(i * r - 1, 0), 0)),
                  pl.BlockSpec((1, halo, d), lambda bi, i: (bi, jnp.minimum((i + 1) * r, nblk - 1), 0)),
                  pl.BlockSpec((1, 6, d), mod_map),
                  pl.BlockSpec((1, d), const2),
                  _resident(w_t.shape),
                  _resident(wab_t.shape),
                  _resident(conv_w.shape),
                  _resident(hsum.shape)],
        out_specs=out_specs,
        out_shape=out_shape,
        compiler_params=_cparams(2),
        name="inproj",
    )(x, x, x, mod3, g_norm.reshape(1, d), w_t, wab_t, conv_w, hsum)


def _rope(x, cos, sin, lane):
    swapped = jnp.where((lane % 32) < 16, pltpu.roll(x, LANES - 16, 1), pltpu.roll(x, 16, 1))
    return x * cos + swapped * sin


def _attn_kernel(a_ref, kvc_ref, cos_ref, sin_ref, qg_ref, kg_ref, sink_ref, havg_ref,
                 o_ref, k_s, vt_s, *, seq, ctx, qb):
    n = pl.program_id(1)
    havg = havg_ref[...]
    prep_rows = ATTN_PREP_ROWS
    ctx_blocks = ctx // ATTN_BLOCK

    @pl.when(n == 0)
    def _prep():
        kg = kg_ref[...]
        kc = kvc_ref[0, :, 0:KV_WIDTH].astype(F32)
        ms = _dot((kc * kc).astype(BF16), havg)
        k_s[0:ctx, :] = (kc * lax.rsqrt(ms + EPS) * kg).astype(BF16)
        vc = kvc_ref[0, :, KV_WIDTH:2 * KV_WIDTH].astype(F32)
        for j in range(ctx_blocks):
            vt_s[j] = vc[j * ATTN_BLOCK:(j + 1) * ATTN_BLOCK, :].T.astype(BF16)
        lane = lax.broadcasted_iota(jnp.int32, (prep_rows, LANES), 1)
        for r0 in range(0, seq, prep_rows):
            kx = a_ref[0, r0:r0 + prep_rows, _OFF_AK:_OFF_AK + KV_WIDTH].astype(F32)
            ms = _dot((kx * kx).astype(BF16), havg)
            kx = kx * lax.rsqrt(ms + EPS) * kg
            kx = _rope(kx, cos_ref[r0:r0 + prep_rows, :], sin_ref[r0:r0 + prep_rows, :], lane)
            k_s[ctx + r0:ctx + r0 + prep_rows, :] = kx.astype(BF16)
            vx = a_ref[0, r0:r0 + prep_rows, _OFF_AV:_OFF_AV + KV_WIDTH].astype(F32)
            for j in range(prep_rows // ATTN_BLOCK):
                vt_s[ctx_blocks + r0 // ATTN_BLOCK + j] = vx[j * ATTN_BLOCK:(j + 1) * ATTN_BLOCK, :].T.astype(BF16)

    lane = lax.broadcasted_iota(jnp.int32, (ATTN_BLOCK, LANES), 1)
    qg = qg_ref[...]
    n_band = 3
    n_win = n_band * ATTN_BLOCK
    n_keys = n_win + ctx
    key_row = lax.broadcasted_iota(jnp.int32, (n_win, ATTN_BLOCK), 0)
    q_lane = lax.broadcasted_iota(jnp.int32, (n_win, ATTN_BLOCK), 1)
    zeros = jnp.zeros((HEAD_DIM, ATTN_BLOCK), BF16)
    groups = range(ATTN_KV_HEADS)
    sink_rows = [jnp.concatenate([sink_ref[g * ATTN_GROUP + r:g * ATTN_GROUP + r + 1, :] for r in range(ATTN_GROUP)],
                                 axis=1) * LOG2_E for g in groups]

    units = []
    k_alls, vt_alls, valids, rhss = [], [], [], {}
    for sb in range(qb):
        nq = n * qb + sb
        q0 = pl.multiple_of(nq * ATTN_BLOCK, ATTN_BLOCK)
        cos = cos_ref[pl.ds(q0, ATTN_BLOCK), :]
        sin = sin_ref[pl.ds(q0, ATTN_BLOCK), :]
        qts = []
        for j in range(ATTN_WIDTH // LANES):
            qx = a_ref[0, pl.ds(q0, ATTN_BLOCK), j * LANES:(j + 1) * LANES].astype(F32)
            ms = _dot((qx * qx).astype(BF16), havg)
            qx = _rope(qx * lax.rsqrt(ms + EPS) * qg, cos, sin, lane) * (HEAD_DIM ** -0.5 * LOG2_E)
            qts.append(qx.T.astype(BF16))
        blk0 = jnp.clip(nq - 1, 0, seq // ATTN_BLOCK - n_band)
        start = pl.multiple_of(blk0 * ATTN_BLOCK, ATTN_BLOCK)
        valid = jnp.abs(q0 + q_lane - (start + key_row)) <= WINDOW
        valids.append(jnp.concatenate([valid] * ATTN_GROUP, axis=1))
        k_alls.append(jnp.concatenate([k_s[pl.ds(ctx + start, n_win), :], k_s[0:ctx, :]], axis=0))
        vt_loc = vt_s[pl.ds(ctx_blocks + blk0, n_band)]
        vt_alls.append(jnp.concatenate([vt_loc[j] for j in range(n_band)] + [vt_s[j] for j in range(ctx_blocks)],
                                       axis=1))
        for g in groups:
            cols = []
            for r in range(ATTN_GROUP):
                h = g * ATTN_GROUP + r
                piece = qts[h // 2][(h % 2) * HEAD_DIM:(h % 2 + 1) * HEAD_DIM, :]
                cols.append(jnp.concatenate([piece, zeros] if g == 0 else [zeros, piece], axis=0))
            rhss[(sb, g)] = jnp.concatenate(cols, axis=1)
            units.append((sb, g))
        yield

    ss = []
    for sb, g in units:
        s = _dot(k_alls[sb], rhss[(sb, g)])
        ss.append(jnp.concatenate([jnp.where(valids[sb], s[0:n_win], NEG_INF), s[n_win:n_keys]], axis=0))
        yield
    p_bfs, invs = [], []
    for s, (sb, g) in zip(ss, units):
        m = jnp.maximum(jnp.max(s, axis=0, keepdims=True), sink_rows[g])
        p = jnp.exp2(s - m)
        invs.append(1.0 / (jnp.sum(p, axis=0, keepdims=True) + jnp.exp2(sink_rows[g] - m)))
        p_bfs.append(p.astype(BF16))
        yield
    pieces = {sb: [] for sb in range(qb)}
    for (sb, g), p_bf, inv in zip(units, p_bfs, invs):
        vt_g = vt_alls[sb][g * HEAD_DIM:(g + 1) * HEAD_DIM, :]
        for pair in range(ATTN_GROUP // 2):
            lanes = slice(pair * MXU_TILE, (pair + 1) * MXU_TILE)
            ot = _dot(vt_g, p_bf[:, lanes]) * inv[:, lanes]
            pieces[sb] += [ot[:, 0:ATTN_BLOCK], ot[:, ATTN_BLOCK:2 * ATTN_BLOCK]]
        yield
    for sb in range(qb):
        outs = [jnp.concatenate([pieces[sb][2 * j], pieces[sb][2 * j + 1]], axis=0).T
                for j in range(ATTN_Q_HEADS // 2)]
        o_ref[0, sb * ATTN_BLOCK:(sb + 1) * ATTN_BLOCK, :] = jnp.concatenate(outs, axis=1).astype(o_ref.dtype)


def _run_parts(parts, b, name):
    steps = parts[0]["steps"]
    assert all(p["steps"] == steps for p in parts)
    counts = [(len(p["args"]), len(p["out_shape"]), len(p["scratch_shapes"])) for p in parts]
    n_in = sum(c[0] for c in counts)
    n_out = sum(c[1] for c in counts)

    def kern(*refs):
        i_pos, o_pos, s_pos = 0, n_in, n_in + n_out
        bodies = []
        for p, (ci, co, cs) in zip(parts, counts):
            bodies.append(p["kernel"](*refs[i_pos:i_pos + ci], *refs[o_pos:o_pos + co], *refs[s_pos:s_pos + cs]))
            i_pos, o_pos, s_pos = i_pos + ci, o_pos + co, s_pos + cs
        while bodies:
            for body in list(bodies):
                try:
                    next(body)
                except StopIteration:
                    bodies.remove(body)

    outs = pl.pallas_call(
        kern,
        grid=(b, steps),
        in_specs=[s for p in parts for s in p["in_specs"]],
        out_specs=[s for p in parts for s in p["out_specs"]],
        out_shape=[s for p in parts for s in p["out_shape"]],
        scratch_shapes=[s for p in parts for s in p["scratch_shapes"]],
        compiler_params=_cparams(2),
        name=name,
    )(*[a for p in parts for a in p["args"]])
    res, pos = [], 0
    for _, co, _ in counts:
        res.append(outs[pos:pos + co])
        pos += co
    return res


def _attn_parts(a_lat, kv_ctx, cos, sin, qg, kg, sink, havg, qb):
    b, s, wa = a_lat.shape
    ctx = kv_ctx.shape[1]
    nb = s // ATTN_BLOCK
    const2 = lambda bi, n: (0, 0)
    return dict(
        kernel=functools.partial(_attn_kernel, seq=s, ctx=ctx, qb=qb),
        steps=nb // qb,
        in_specs=[pl.BlockSpec((1, s, wa), lambda bi, n: (bi, 0, 0)),
                  pl.BlockSpec((1, ctx, 2 * KV_WIDTH), lambda bi, n: (bi, 0, 0)),
                  pl.BlockSpec((s, LANES), const2),
                  pl.BlockSpec((s, LANES), const2),
                  pl.BlockSpec((1, LANES), const2),
                  pl.BlockSpec((1, LANES), const2),
                  pl.BlockSpec((ATTN_Q_HEADS, LANES), const2),
                  pl.BlockSpec((LANES, LANES), const2)],
        out_specs=[pl.BlockSpec((1, qb * ATTN_BLOCK, ATTN_WIDTH), lambda bi, n: (bi, n, 0))],
        out_shape=[jax.ShapeDtypeStruct((b, s, ATTN_WIDTH), BF16)],
        scratch_shapes=[pltpu.VMEM((ctx + s, KV_WIDTH), BF16),
                        pltpu.VMEM(((ctx + s) // ATTN_BLOCK, KV_WIDTH, ATTN_BLOCK), BF16)],
        args=[a_lat, kv_ctx, cos, sin, qg, kg, sink, havg])


def _half_mask():
    return lax.broadcasted_iota(jnp.int32, (CHUNK, LANES), 1) < HEAD_DIM


def _bdiag_pair(x_pair, lo_half):
    zeros = jnp.zeros_like(x_pair)
    return jnp.concatenate([jnp.where(lo_half, x_pair, zeros), jnp.where(lo_half, zeros, x_pair)], axis=0)


def _bdiag(x_lane, lo_half):
    zeros = jnp.zeros((CHUNK, LANES), x_lane.dtype)
    blocks = []
    for h in range(HEADS_PER_TILE):
        col = (h * HEAD_DIM) // LANES
        piece = x_lane[:, col * LANES:(col + 1) * LANES]
        piece = jnp.where(lo_half if (h * HEAD_DIM) % LANES == 0 else jnp.logical_not(lo_half), piece, zeros)
        blocks.append(jnp.concatenate([piece if c == col else zeros for c in range(MXU_TILE // LANES)], axis=1))
    return jnp.concatenate(blocks, axis=0)


def _unit_tri_inverses(a_list, eye_l, sub_mask, lo_half):
    ads = [jnp.where(sub_mask, a, 0.0) for a in a_list]
    ys = [jnp.where(sub_mask, 0.0, a) for a in a_list]
    ps = [eye_l - ad for ad in ads]
    pws = [ad.astype(BF16) for ad in ads]
    n_levels = int(np.log2(DN_SUB))
    c1, c2, c3 = CHUNK, 2 * CHUNK, 3 * CHUNK
    for level in range(n_levels):
        rhss = [_bdiag(pw, lo_half) for pw in pws]
        if level == 0:
            boths = [_dot(jnp.concatenate([pw, y.astype(BF16)], axis=0), rhs) for pw, y, rhs in zip(pws, ys, rhss)]
            pws = [both[0:c1].astype(BF16) for both in boths]
            ys = [y - both[c1:c2] for y, both in zip(ys, boths)]
        elif level < n_levels - 1:
            boths = [_dot(jnp.concatenate([pw, p.astype(BF16), y.astype(BF16)], axis=0), rhs)
                     for pw, p, y, rhs in zip(pws, ps, ys, rhss)]
            pws = [both[0:c1].astype(BF16) for both in boths]
            ps = [p + both[c1:c2] for p, both in zip(ps, boths)]
            ys = [y + both[c2:c3] for y, both in zip(ys, boths)]
        else:
            boths = [_dot(jnp.concatenate([p.astype(BF16), y.astype(BF16)], axis=0), rhs)
                     for p, y, rhs in zip(ps, ys, rhss)]
            ps = [p + both[0:c1] for p, both in zip(ps, boths)]
            ys = [y + both[c1:c2] for y, both in zip(ys, boths)]
        yield
    assert CHUNK // DN_SUB == 4
    b_bfs = [y.astype(BF16) for y in ys]
    boths = [_dot(jnp.concatenate([b, p.astype(BF16)], axis=0), _bdiag(b, lo_half)) for b, p in zip(b_bfs, ps)]
    yield
    zs = [p - both[c1:c2] for p, both in zip(ps, boths)]
    return [z + _dot(z.astype(BF16), _bdiag(both[0:c1].astype(BF16), lo_half)) for z, both in zip(zs, boths)]


def _dn_factors(abs_, d, arow, dtrow, exp2_ref, tri_ref, eye_t):
    def hi_lo(v):
        hi = v.astype(BF16)
        return hi, (v - hi.astype(F32)).astype(BF16)

    lane = lax.broadcasted_iota(jnp.int32, abs_[0].shape, 1)
    is_g = (lane % 32) < 16
    xhls = []
    for ab in abs_:
        z = ab + dtrow
        softplus = jnp.maximum(z, 0.0) + jnp.log(1.0 + jnp.exp(-jnp.abs(z)))
        x = jnp.where(is_g, -arow * softplus, _sigmoid(ab))
        x = jnp.where(is_g, _dot(tri_ref[d], jnp.concatenate(hi_lo(x), axis=0)), x)
        x_hi, x_lo = hi_lo(x)
        xhls.append(jnp.where(lane < 32, x_hi, x_lo)[:, 0:64])
    y = _dot(jnp.concatenate(xhls, axis=0), exp2_ref[d])
    out = []
    for c in range(len(abs_)):
        gi = y[c * CHUNK:(c + 1) * CHUNK, 0:DN_WIDTH]
        be = y[c * CHUNK:(c + 1) * CHUNK, DN_WIDTH:2 * DN_WIDTH]
        gj = jnp.sum(gi * eye_t, axis=0, keepdims=True)
        out.append((be, gi, jnp.broadcast_to(gj, gi.shape)))
    return out


def _block_mask():
    return (lax.broadcasted_iota(jnp.int32, (MXU_TILE, MXU_TILE), 0) // HEAD_DIM
            == lax.broadcasted_iota(jnp.int32, (MXU_TILE, MXU_TILE), 1) // HEAD_DIM)


def _dnprep_kernel(*refs, cb, want_o):
    if want_o:
        q_ref, k_ref, v_ref, ab_ref = refs[:4]
        refs = refs[4:]
    else:
        q_ref = None
        k_ref, v_ref, ab_ref = refs[:3]
        refs = refs[3:]
    arow_ref, dtrow_ref, exp2_ref, tri_ref, w_o, uv_o, kt_o, dl_o = refs[:8]
    qd_o, in_o = refs[8:10] if want_o else (None, None)
    n_tiles = DN_WIDTH // MXU_TILE
    row = lax.broadcasted_iota(jnp.int32, (CHUNK, MXU_TILE), 0)
    colj = lax.broadcasted_iota(jnp.int32, (CHUNK, MXU_TILE), 1) % HEAD_DIM
    eye_l = (row == colj).astype(F32)
    eye_t = jnp.concatenate([eye_l] * n_tiles, axis=1)
    lo_half = _half_mask()
    arow = arow_ref[...]
    dtrow = dtrow_ref[...]

    fac = {}
    shared = {}
    abs_ = [ab_ref[0, c * CHUNK:(c + 1) * CHUNK, :] for c in range(cb)]
    for d in range(N_DIR):
        for c, f in enumerate(_dn_factors(abs_, d, arow, dtrow, exp2_ref, tri_ref, eye_t)):
            fac[(c, d)] = f
    yield
    for c in range(cb):
        rows = slice(c * CHUNK, (c + 1) * CHUNK)
        for g in range(n_tiles):
            lanes = slice(g * MXU_TILE, (g + 1) * MXU_TILE)
            k_l = k_ref[0, rows, lanes]
            kbd = _bdiag(k_l, lo_half)
            if want_o:
                kq = _dot_nt(jnp.concatenate([k_l, q_ref[0, rows, lanes]], axis=0), kbd)
                shared[(c, g)] = (kq[0:CHUNK], kq[CHUNK:2 * CHUNK])
            else:
                shared[(c, g)] = (_dot_nt(k_l, kbd), None)
        yield

    units = [(c, g, d) for c in range(cb) for g in range(n_tiles) for d in range(N_DIR)]
    decs = []
    for c, g, d in units:
        lanes = slice(g * MXU_TILE, (g + 1) * MXU_TILE)
        be, gi, gj = fac[(c, d)]
        lower = (row > colj) if d == 0 else (row < colj)
        decs.append(jnp.where(lower, jnp.exp(jnp.where(lower, gi[:, lanes] - gj[:, lanes], 0.0)), 0.0))
    a_list = [fac[(c, d)][0][:, g * MXU_TILE:(g + 1) * MXU_TILE] * dec * shared[(c, g)][0]
              for (c, g, d), dec in zip(units, decs)]
    yield
    tinvs = yield from _unit_tri_inverses(a_list, eye_l, (row // DN_SUB) == (colj // DN_SUB), lo_half)

    for idx, ((c, g, d), dec, tinv) in enumerate(zip(units, decs, tinvs)):
        if idx % (n_tiles * N_DIR) == 0:
            yield
        rows = slice(c * CHUNK, (c + 1) * CHUNK)
        lanes = slice(g * MXU_TILE, (g + 1) * MXU_TILE)
        be, gi, _ = fac[(c, d)]
        be, gi = be[:, lanes], gi[:, lanes]
        last = CHUNK - 1 if d == 0 else 0
        e_g = jnp.exp(gi)
        gl_row = gi[last:last + 1, :]
        kf = k_ref[0, rows, lanes].astype(F32)
        vf = v_ref[0, rows, lanes].astype(F32)
        ck = (be * e_g * kf).astype(BF16)
        bv = (be * vf).astype(BF16)
        tinv_bf = tinv.astype(BF16)
        w_parts, uv_parts = [], []
        for pr in range(MXU_TILE // LANES):
            pl_ = slice(pr * LANES, (pr + 1) * LANES)
            rhs = jnp.concatenate([_bdiag_pair(ck[:, pl_], lo_half), _bdiag_pair(bv[:, pl_], lo_half)], axis=1)
            wu = _dot(tinv_bf[:, pl_], rhs)
            w_parts.append(wu[:, 0:LANES])
            uv_parts.append(wu[:, LANES:2 * LANES])
        w_o[0, d, rows, lanes] = jnp.concatenate(w_parts, axis=1).astype(BF16)
        uv_o[0, d, rows, lanes] = jnp.concatenate(uv_parts, axis=1).astype(BF16)
        kt_o[0, d, rows, lanes] = (jnp.exp(gl_row - gi) * kf).astype(BF16)
        dl_o[0, d, c, :, lanes] = jnp.exp(gl_row)
        if want_o:
            qd_o[0, d, rows, lanes] = (e_g * q_ref[0, rows, lanes].astype(F32)).astype(BF16)
            in_o[0, d, rows, lanes] = ((dec + eye_l) * shared[(c, g)][1]).astype(BF16)


def _dnprep_parts(q, k, v, ab, arow, dtrow, exp2, tri, cb):
    want_o = q is not None
    b, t, w = k.shape
    tb = cb * CHUNK
    tok = lambda bi, i: (bi, i, 0)
    const2 = lambda bi, i: (0, 0)
    const3 = lambda bi, i: (0, 0, 0)
    dir_tok = lambda bi, i: (bi, 0, i, 0)
    data = ([q] if want_o else []) + [k, v]
    in_specs = ([pl.BlockSpec((1, tb, w), tok)] * len(data) + [pl.BlockSpec((1, tb, LANES), tok),
                pl.BlockSpec((1, LANES), const2), pl.BlockSpec((1, LANES), const2),
                pl.BlockSpec(exp2.shape, const3), pl.BlockSpec(tri.shape, const3)])
    big = lambda dt: jax.ShapeDtypeStruct((b, N_DIR, t, w), dt)
    big_spec = pl.BlockSpec((1, N_DIR, tb, w), dir_tok)
    out_shape = [big(BF16), big(BF16), big(BF16), jax.ShapeDtypeStruct((b, N_DIR, t // CHUNK, 1, w), F32)]
    out_specs = [big_spec, big_spec, big_spec, pl.BlockSpec((1, N_DIR, cb, 1, w), lambda bi, i: (bi, 0, i, 0, 0))]
    if want_o:
        out_shape += [big(BF16), big(BF16)]
        out_specs += [big_spec, big_spec]
    return dict(kernel=functools.partial(_dnprep_kernel, cb=cb, want_o=want_o), steps=t // tb, in_specs=in_specs,
                out_specs=out_specs, out_shape=out_shape, scratch_shapes=[],
                args=data + [ab, arow, dtrow, exp2, tri])


def _dnscan_kernel(*refs, n_chunk, bb, want_o, have_s0, want_s):
    n_in = 6 if want_o else 4
    dir_refs = [refs[0:n_in], refs[n_in:2 * n_in]]
    pos = 2 * n_in
    s0_ref = refs[pos] if have_s0 else None
    pos += int(have_s0)
    o_refs = refs[pos:pos + N_DIR] if want_o else None
    pos += N_DIR if want_o else 0
    sout_ref = refs[pos] if want_s else None
    pos += int(want_s)
    s_scr = refs[pos]
    n_tiles = DN_WIDTH // MXU_TILE
    i = pl.program_id(1)

    @pl.when(i == 0)
    def _init():
        if have_s0:
            s_scr[...] = s0_ref[...]
        else:
            s_scr[...] = jnp.zeros_like(s_scr)

    bmask = _block_mask()
    lo_half = _half_mask()
    chains = [(bi, d, g) for bi in range(bb) for d in range(N_DIR) for g in range(n_tiles)]

    def body(j, carry):
        cidx = (j, n_chunk - 1 - j)
        r0s = [pl.multiple_of(cidx[d] * CHUNK, CHUNK) for d in range(N_DIR)]
        s_olds, r1s = [], []
        for bi, d, g in chains:
            lanes = slice(g * MXU_TILE, (g + 1) * MXU_TILE)
            w = dir_refs[d][0][bi, 0, pl.ds(r0s[d], CHUNK), lanes]
            if want_o:
                w = jnp.concatenate([w, dir_refs[d][4][bi, 0, pl.ds(r0s[d], CHUNK), lanes]], axis=0)
            s_old = s_scr[bi, d * n_tiles + g]
            s_olds.append(s_old)
            r1s.append(_dot(w, s_old.astype(BF16)))
        u_bfs = []
        for (bi, d, g), r1 in zip(chains, r1s):
            lanes = slice(g * MXU_TILE, (g + 1) * MXU_TILE)
            u_bfs.append((dir_refs[d][1][bi, 0, pl.ds(r0s[d], CHUNK), lanes] - r1[0:CHUNK]).astype(BF16))
        for (bi, d, g), r1, u_bf, s_old in zip(chains, r1s, u_bfs, s_olds):
            lanes = slice(g * MXU_TILE, (g + 1) * MXU_TILE)
            kt = dir_refs[d][2][bi, 0, pl.ds(r0s[d], CHUNK), lanes]
            ds = jnp.where(bmask, _dot_tn(kt, u_bf), 0.0)
            dl = dir_refs[d][3][bi, 0, cidx[d]][:, lanes]
            s_scr[bi, d * n_tiles + g] = s_old * dl + ds
            if want_o:
                intra = dir_refs[d][5][bi, 0, pl.ds(r0s[d], CHUNK), lanes]
                o = r1[CHUNK:2 * CHUNK] + _dot(intra, _bdiag(u_bf, lo_half))
                o_refs[d][bi, pl.ds(r0s[d], CHUNK), lanes] = o.astype(BF16)
        return carry

    lax.fori_loop(0, n_chunk, body, 0)

    if want_s:
        @pl.when(i == pl.num_programs(1) - 1)
        def _fin():
            sout_ref[...] = s_scr[...]


def _dnscan(prep, s0, tb, bb, want_s):
    want_o = len(prep) == 6
    b, _, t, w = prep[0].shape
    n_t = t // tb
    n_chunk = tb // CHUNK
    n_chain = N_DIR * (w // MXU_TILE)

    def specs(d):
        blk = (lambda bi, i: i) if d == 0 else (lambda bi, i: n_t - 1 - i)
        big = pl.BlockSpec((bb, 1, tb, w), lambda bi, i: (bi, d, blk(bi, i), 0))
        dl = pl.BlockSpec((bb, 1, n_chunk, 1, w), lambda bi, i: (bi, d, blk(bi, i), 0, 0))
        return [big, big, big, dl] + ([big, big] if want_o else [])

    in_specs = specs(0) + specs(1)
    args = list(prep) + list(prep)
    state_spec = pl.BlockSpec((bb, n_chain, MXU_TILE, MXU_TILE), lambda bi, i: (bi, 0, 0, 0))
    if s0 is not None:
        in_specs.append(state_spec)
        args.append(s0)
    out_shape, out_specs = [], []
    if want_o:
        out_shape += [jax.ShapeDtypeStruct((b, t, w), BF16)] * N_DIR
        out_specs += [pl.BlockSpec((bb, tb, w), lambda bi, i: (bi, i, 0)),
                      pl.BlockSpec((bb, tb, w), lambda bi, i: (bi, n_t - 1 - i, 0))]
    if want_s:
        out_shape.append(jax.ShapeDtypeStruct((b, n_chain, MXU_TILE, MXU_TILE), F32))
        out_specs.append(state_spec)
    return pl.pallas_call(
        functools.partial(_dnscan_kernel, n_chunk=n_chunk, bb=bb, want_o=want_o, have_s0=s0 is not None,
                          want_s=want_s),
        grid=(b // bb, n_t),
        in_specs=in_specs,
        out_specs=out_specs,
        out_shape=out_shape,
        scratch_shapes=[pltpu.VMEM((bb, n_chain, MXU_TILE, MXU_TILE), F32)],
        compiler_params=_cparams(2),
        name="dnscan",
    )(*args)


def _tail_kernel(x_ref, mod_ref, ya_ref, odf_ref, odb_ref, z_ref, gate_ref, dng_ref, havg_ref, wba_ref, wbd_ref,
                 wo_ref, gn2_ref, w1_ref, w2_ref, o_ref, *, ff_chunk):
    havg = havg_ref[...]
    dng = dng_ref[...]
    yd_parts = []
    for j in range(DN_WIDTH // MXU_TILE):
        sl = slice(j * MXU_TILE, (j + 1) * MXU_TILE)
        od = odf_ref[0, :, sl].astype(F32) + odb_ref[0, :, sl].astype(F32)
        ms = _dot((od * od).astype(BF16), havg)
        z = z_ref[0, :, sl].astype(F32)
        yd_parts.append((od * lax.rsqrt(ms + EPS) * dng * (z * _sigmoid(z))).astype(BF16))
    yd = jnp.concatenate(yd_parts, axis=1)
    ga = gate_ref[0, :, 0:D_MODEL].astype(F32)
    gd = gate_ref[0, :, D_MODEL:2 * D_MODEL].astype(F32)
    y = _sigmoid(ga) * _dot(ya_ref[0], wba_ref[...]) + _sigmoid(gd) * _dot(yd, wbd_ref[...])
    br = _dot(y.astype(BF16), wo_ref[...])
    mod = mod_ref[0]
    out1 = x_ref[0] + mod[2:3] * br
    ms2 = jnp.mean(out1 * out1, axis=-1, keepdims=True)
    hm = (out1 * lax.rsqrt(ms2 + EPS) * (gn2_ref[...] * (1.0 + mod[4:5])) + mod[3:4]).astype(BF16)
    acc = None
    for j in range(D_FF // ff_chunk):
        a = jnp.maximum(_dot(hm, w1_ref[:, j * ff_chunk:(j + 1) * ff_chunk]), 0.0)
        part = _dot((a * a).astype(BF16), w2_ref[j * ff_chunk:(j + 1) * ff_chunk, :])
        acc = part if acc is None else acc + part
    o_ref[0] = out1 + mod[5:6] * acc


def _resident(shape):
    return pl.BlockSpec(shape, lambda bi, i: (0,) * len(shape), pipeline_mode=pl.Buffered(1))


def _tail(x, mod3, y_attn, o_df, o_db, z, gates, dng, havg, wba, wbd, wo, gn2, w1, w2, tm):
    b, t, d = x.shape
    tok = lambda bi, i: (bi, i, 0)
    return pl.pallas_call(
        functools.partial(_tail_kernel, ff_chunk=MLP_FF_CHUNK),
        grid=(b, t // tm),
        in_specs=[pl.BlockSpec((1, tm, d), tok),
                  pl.BlockSpec((1, 6, d), lambda bi, i: (bi, 0, 0)),
                  pl.BlockSpec((1, tm, ATTN_WIDTH), tok),
                  pl.BlockSpec((1, tm, DN_WIDTH), tok),
                  pl.BlockSpec((1, tm, DN_WIDTH), tok),
                  pl.BlockSpec((1, tm, DN_WIDTH), tok),
                  pl.BlockSpec((1, tm, 2 * d), tok),
                  _resident((1, MXU_TILE)),
                  _resident((MXU_TILE, MXU_TILE)),
                  _resident(wba.shape),
                  _resident(wbd.shape),
                  _resident(wo.shape),
                  _resident((1, d)),
                  _resident(w1.shape),
                  _resident(w2.shape)],
        out_specs=pl.BlockSpec((1, tm, d), tok),
        out_shape=jax.ShapeDtypeStruct((b, t, d), F32),
        compiler_params=_cparams(2),
        name="tail",
    )(x, mod3, y_attn, o_df, o_db, z, gates, dng, havg, wba, wbd, wo, gn2, w1, w2)


def _head_avg(n, scale):
    idx = np.arange(n) // HEAD_DIM
    return jnp.asarray((idx[:, None] == idx[None, :]).astype(np.float32) * scale, BF16)


def _dn_expand_matrix():
    n = N_DIR * DN_HEADS
    m = np.zeros((N_DIR, 4 * n, 2 * DN_WIDTH), np.float32)
    for d in range(N_DIR):
        for part in range(2):
            for h in range(DN_HEADS):
                idx = d * DN_HEADS + h
                m[d, part * 2 * n + idx, h * HEAD_DIM:(h + 1) * HEAD_DIM] = 1.0
                m[d, part * 2 * n + n + idx, DN_WIDTH + h * HEAD_DIM:DN_WIDTH + (h + 1) * HEAD_DIM] = 1.0
    return jnp.asarray(m, BF16)


def _tri_matrices():
    i = np.arange(CHUNK)
    low = (i[:, None] >= i[None, :]).astype(np.float32)
    up = (i[:, None] <= i[None, :]).astype(np.float32)
    return jnp.asarray(np.stack([np.concatenate([low, low], axis=1), np.concatenate([up, up], axis=1)]), BF16)


def _rope_tables(seq):
    half = HEAD_DIM // 2
    n_freq = half // 2
    freqs = ROPE_BASE ** (-jnp.arange(n_freq, dtype=F32) / n_freq)
    pos = jnp.arange(seq)
    ang_r = (pos // GRID_W).astype(F32)[:, None] * freqs
    ang_c = (pos % GRID_W).astype(F32)[:, None] * freqs
    cos = jnp.concatenate([jnp.cos(ang_r)] * 2 + [jnp.cos(ang_c)] * 2, axis=1)
    sin = jnp.concatenate([-jnp.sin(ang_r), jnp.sin(ang_r), -jnp.sin(ang_c), jnp.sin(ang_c)], axis=1)
    reps = LANES // HEAD_DIM
    return jnp.tile(cos, (1, reps)), jnp.tile(sin, (1, reps))


def _pad_cols(w, n):
    return jnp.pad(w, ((0, 0), (0, n - w.shape[1])))


def kernel(x, c, ctx, c_ctx, w_ada, b_ada, g_norm1, w_in, q_norm_g, k_norm_g, attn_sink, conv_w, a_log, dt_bias,
           dn_norm_g, w_br_attn, w_br_dn, w_out, g_norm2, w_mlp1, w_mlp2):
    depth = w_ada.shape[0]
    assert depth == 1, "single-layer trunk only"
    b, s, d = x.shape
    n_ctx = ctx.shape[1]
    assert d == D_MODEL and w_in.shape[-1] == _IN_WIDTH
    assert s >= 3 * ATTN_BLOCK and s % ATTN_BLOCK == 0 and s % CHUNK == 0 and n_ctx % CHUNK == 0
    out_dtype = x.dtype
    w_in_t = jnp.swapaxes(w_in[0], 0, 1).astype(BF16)

    mod_rows = 16
    cc = jnp.concatenate([c.astype(F32), c_ctx.astype(F32)[None], jnp.zeros((mod_rows - b - 1, d), F32)], axis=0)
    mod = _ada(cc, w_ada[0], b_ada[0])
    mod3 = mod.reshape(mod_rows, 6, d)

    ab_rows = w_in_t[_OFF_DA:_OFF_GA]
    wab_t = jnp.concatenate([ab_rows, ab_rows, jnp.zeros((LANES - 2 * (_OFF_GA - _OFF_DA), d), BF16)], axis=0)
    hsum = _head_avg(MXU_TILE, 1.0)
    segs_lat = ((0, 0, _OFF_DQ, None),
                (0, _OFF_DQ, DN_WIDTH, (0, True, True)), (0, _OFF_DK, DN_WIDTH, (DN_WIDTH, True, False)),
                (0, _OFF_DV, DN_WIDTH, (2 * DN_WIDTH, False, False)),
                (0, _OFF_DZ, DN_WIDTH, None), (0, _OFF_GA, 2 * D_MODEL, None), (1, 0, LANES, None))
    a_lat, q_d, k_d, v_d, z_lat, gates, ab_lat = _inproj(x, mod3, None, g_norm1[0], w_in_t, wab_t, conv_w[0], hsum,
                                                         segs_lat, (BF16, BF16, BF16, BF16, BF16, BF16, F32),
                                                         tm=2 * TOKEN_TILE)
    segs_ctx = ((0, _OFF_AK, 2 * KV_WIDTH, None),
                (0, _OFF_DK, DN_WIDTH, (0, True, False)), (0, _OFF_DV, DN_WIDTH, (DN_WIDTH, False, False)),
                (1, 0, LANES, None))
    kv_ctx, k_dc, v_dc, ab_ctx = _inproj(ctx, mod3, b, g_norm1[0], w_in_t, wab_t, conv_w[0][:, DN_WIDTH:], hsum,
                                         segs_ctx, (BF16, BF16, BF16, F32), tm=n_ctx)

    cos, sin = _rope_tables(s)
    reps = LANES // HEAD_DIM
    n_gate = N_DIR * DN_HEADS
    arow = _pad_cols(jnp.tile(jnp.concatenate([jnp.exp(a_log[0]).reshape(1, n_gate), jnp.zeros((1, n_gate), F32)],
                                              axis=1), (1, 2)), LANES)
    dtrow = _pad_cols(jnp.tile(jnp.concatenate([dt_bias[0].reshape(1, n_gate), jnp.zeros((1, n_gate), F32)],
                                               axis=1), (1, 2)), LANES)
    exp2, tri = _dn_expand_matrix(), _tri_matrices()
    dn_cb = DN_CHUNKS_PER_STEP
    qb = (s // ATTN_BLOCK) // (s // (dn_cb * CHUNK))
    attn_parts = _attn_parts(a_lat, kv_ctx, cos, sin,
                             jnp.tile(q_norm_g[0].astype(F32), reps)[None],
                             jnp.tile(k_norm_g[0].astype(F32), reps)[None],
                             jnp.broadcast_to(attn_sink[0].astype(F32)[:, None], (ATTN_Q_HEADS, LANES)),
                             _head_avg(LANES, 1.0 / HEAD_DIM), qb)
    (y_attn,), prep_lat = _run_parts(
        [attn_parts, _dnprep_parts(q_d, k_d, v_d, ab_lat, arow, dtrow, exp2, tri, dn_cb)], b, "attn_dnprep")
    (prep_ctx,) = _run_parts([_dnprep_parts(None, k_dc, v_dc, ab_ctx, arow, dtrow, exp2, tri, dn_cb)], b, "dnprep")

    bb = max(q for q in range(1, SCAN_BATCH_ROWS + 1) if b % q == 0)
    (s_ctx,) = _dnscan(prep_ctx, None, tb=n_ctx, bb=bb, want_s=True)
    o_df, o_db = _dnscan(prep_lat, s_ctx, tb=SCAN_TOKEN_TILE, bb=bb, want_s=False)

    out = _tail(x, mod3, y_attn, o_df, o_db, z_lat, gates,
                jnp.tile(dn_norm_g[0].astype(F32), HEADS_PER_TILE)[None], _head_avg(MXU_TILE, 1.0 / HEAD_DIM),
                w_br_attn[0].astype(BF16), w_br_dn[0].astype(BF16), w_out[0].astype(BF16),
                g_norm2[0].reshape(1, d), w_mlp1[0].astype(BF16), w_mlp2[0].astype(BF16), tm=TOKEN_TILE)
    return out.astype(out_dtype)
```

```python
import functools

import numpy as np
import jax
import jax.numpy as jnp
from jax import lax
from jax.experimental import pallas as pl
from jax.experimental.pallas import tpu as pltpu

F32 = jnp.float32
BF16 = jnp.bfloat16

D_MODEL = 1024
GRID_W = 64
HEAD_DIM = 64
ATTN_Q_HEADS = 8
ATTN_KV_HEADS = 2
ATTN_GROUP = ATTN_Q_HEADS // ATTN_KV_HEADS
WINDOW = 128
ATTN_BLOCK = 128
ROPE_BASE = 10000.0
DN_HEADS = 8
CONV_W = 3
CHUNK = 64
N_DIR = 2
D_FF = 4 * D_MODEL
EPS = 1e-6
NEG_INF = -1e30
LOG2_E = float(np.log2(np.e))

ATTN_WIDTH = ATTN_Q_HEADS * HEAD_DIM
KV_WIDTH = ATTN_KV_HEADS * HEAD_DIM
DN_WIDTH = DN_HEADS * HEAD_DIM
LANES = 128
MXU_TILE = 256
HEADS_PER_TILE = MXU_TILE // HEAD_DIM
DN_SUB = 16
VMEM_LIMIT = 56 * 1024 * 1024

INPROJ_TOKEN_TILE = 1024
TAIL_TOKEN_TILE = 512
ADA_COL_TILE = 1536
MLP_FF_CHUNK = 1024
ATTN_PREP_ROWS = 256
DN_CHUNKS_PER_STEP = 4
SCAN_TOKEN_TILE = 256
SCAN_BATCH_ROWS = 4

_OFF_AQ = 0
_OFF_AK = _OFF_AQ + ATTN_WIDTH
_OFF_AV = _OFF_AK + KV_WIDTH
_OFF_DQ = _OFF_AV + KV_WIDTH
_OFF_DK = _OFF_DQ + DN_WIDTH
_OFF_DV = _OFF_DK + DN_WIDTH
_OFF_DZ = _OFF_DV + DN_WIDTH
_OFF_DA = _OFF_DZ + DN_WIDTH
_OFF_DB = _OFF_DA + N_DIR * DN_HEADS
_OFF_GA = _OFF_DB + N_DIR * DN_HEADS
_OFF_GD = _OFF_GA + D_MODEL
_IN_WIDTH = _OFF_GD + D_MODEL


def _sigmoid(x):
    return 0.5 * jnp.tanh(0.5 * x) + 0.5


def _dot(a, b):
    return jnp.dot(a, b, preferred_element_type=F32)


def _dot_nt(a, b):
    return lax.dot_general(a, b, (((1,), (1,)), ((), ())), preferred_element_type=F32)


def _dot_tn(a, b):
    return lax.dot_general(a, b, (((0,), (0,)), ((), ())), preferred_element_type=F32)


def _cparams(n_axes):
    return pltpu.CompilerParams(dimension_semantics=("arbitrary",) * n_axes, vmem_limit_bytes=VMEM_LIMIT)


def _ada_kernel(c_ref, w_ref, b_ref, o_ref):
    c = c_ref[...]
    s = c * _sigmoid(c)
    o_ref[...] = _dot(s.astype(BF16), w_ref[...].astype(BF16)) + b_ref[...]


def _ada(cc, w_ada, b_ada):
    rows, d = cc.shape
    n = w_ada.shape[1]
    tn = ADA_COL_TILE
    return pl.pallas_call(
        _ada_kernel,
        grid=(n // tn,),
        in_specs=[pl.BlockSpec((rows, d), lambda j: (0, 0)),
                  pl.BlockSpec((d, tn), lambda j: (0, j)),
                  pl.BlockSpec((1, tn), lambda j: (0, j))],
        out_specs=pl.BlockSpec((rows, tn), lambda j: (0, j)),
        out_shape=jax.ShapeDtypeStruct((rows, n), F32),
        compiler_params=_cparams(1),
        name="ada",
    )(cc, w_ada, b_ada.reshape(1, n))


def _inproj_kernel(x_ref, xprev_ref, xnext_ref, mod_ref, g_ref, wt_ref, wabt_ref, cw_ref, hsum_ref, *out_refs,
                   segs, tm, halo):
    w_refs = (wt_ref, wabt_ref)
    i = pl.program_id(1)
    last = pl.num_programs(1) - 1
    mod = mod_ref[0]
    scale = g_ref[...] * (1.0 + mod[1:2])

    def norm_mod(v):
        ms = jnp.mean(v * v, axis=-1, keepdims=True)
        return (v * lax.rsqrt(ms + EPS) * scale + mod[0:1]).astype(BF16)

    h = norm_mod(x_ref[0])
    h_ext = jnp.concatenate([h, norm_mod(jnp.concatenate([xprev_ref[0], xnext_ref[0]], axis=0))], axis=0)
    keep_prev = (i > 0).astype(F32)
    keep_next = (i < last).astype(F32)
    rows = lax.broadcasted_iota(jnp.int32, (tm, MXU_TILE), 0)
    hsum = hsum_ref[...]
    def plain_piece(o_ref, src, start, lo, width):
        o_ref[0, :, lo:lo + width] = _dot_nt(h, w_refs[src][start + lo:start + lo + width, :]).astype(o_ref.dtype)

    def conv_piece(o_ref, src, start, lo, kind):
        conv_col, do_norm, is_q = kind
        w_rows = w_refs[src][start + lo:start + lo + MXU_TILE, :]
        p_ext = _dot_nt(h_ext, w_rows)
        p, p_halo = p_ext[0:tm], p_ext[tm:tm + 2 * halo]
        yield
        p_prev = jnp.where(rows == 0, p_halo[halo - 1:halo] * keep_prev, pltpu.roll(p, 1, 0))
        p_next = jnp.where(rows == tm - 1, p_halo[halo:halo + 1] * keep_next, pltpu.roll(p, tm - 1, 0))
        cw = cw_ref[:, conv_col + lo:conv_col + lo + MXU_TILE]
        y = p_prev * cw[0:1] + p * cw[1:2] + p_next * cw[2:3]
        y = y * _sigmoid(y)
        if do_norm:
            y = y * lax.rsqrt(_dot((y * y).astype(BF16), hsum) + EPS)
            if is_q:
                y = y * (HEAD_DIM ** -0.5)
        o_ref[0, :, lo:lo + MXU_TILE] = y.astype(o_ref.dtype)

    plain, conv = [], []
    for o_ref, (src, start, size, kind) in zip(out_refs, segs):
        if kind is None:
            plain += [functools.partial(plain_piece, o_ref, src, start, lo, min(MXU_TILE, size - lo))
                      for lo in range(0, size, MXU_TILE)]
        else:
            conv += [functools.partial(conv_piece, o_ref, src, start, lo, kind) for lo in range(0, size, MXU_TILE)]
    pending = None
    while plain or conv or pending is not None:
        started = conv.pop(0)() if conv else None
        if started is not None:
            next(started)
        if pending is not None:
            for _ in pending:
                pass
        pending = started
        for _ in range(-(-len(plain) // (len(conv) + 1)) if plain else 0):
            plain.pop(0)()


def _inproj(x, mod3, mod_row, g_norm, w_t, wab_t, conv_w, hsum, segs, dtypes, tm):
    b, t, d = x.shape
    halo = 8
    r = tm // halo
    nblk = t // halo
    if mod_row is None:
        mod_map = lambda bi, i: (bi, 0, 0)
    else:
        mod_map = lambda bi, i: (mod_row, 0, 0)
    const2 = lambda bi, i: (0, 0)
    out_shape = [jax.ShapeDtypeStruct((b, t, size), dt) for (_, _, size, _), dt in zip(segs, dtypes)]
    out_specs = [pl.BlockSpec((1, tm, size), lambda bi, i: (bi, i, 0)) for (_, _, size, _) in segs]
    return pl.pallas_call(
        functools.partial(_inproj_kernel, segs=segs, tm=tm, halo=halo),
        grid=(b, t // tm),
        in_specs=[pl.BlockSpec((1, tm, d), lambda bi, i: (bi, i, 0)),
                  pl.BlockSpec((1, halo, d), lambda bi, i: (bi, jnp.maximum(i * r - 1, 0), 0)),
                  pl.BlockSpec((1, halo, d), lambda bi, i: (bi, jnp.minimum((i + 1) * r, nblk - 1), 0)),
                  pl.BlockSpec((1, 6, d), mod_map),
                  pl.BlockSpec((1, d), const2),
                  _resident(w_t.shape),
                  _resident(wab_t.shape),
                  _resident(conv_w.shape),
                  _resident(hsum.shape)],
        out_specs=out_specs,
        out_shape=out_shape,
        compiler_params=_cparams(2),
        name="inproj",
    )(x, x, x, mod3, g_norm.reshape(1, d), w_t, wab_t, conv_w, hsum)


def _rope(x, cos, sin, lane):
    swapped = jnp.where((lane % 32) < 16, pltpu.roll(x, LANES - 16, 1), pltpu.roll(x, 16, 1))
    return x * cos + swapped * sin


def _attn_kernel(a_ref, kvc_ref, cos_ref, sin_ref, qg_ref, kg_ref, sink_ref, havg_ref,
                 o_ref, k_s, vt_s, *, seq, ctx, qb):
    n = pl.program_id(1)
    havg = havg_ref[...]
    prep_rows = ATTN_PREP_ROWS
    ctx_blocks = ctx // ATTN_BLOCK

    @pl.when(n == 0)
    def _prep():
        kg = kg_ref[...]
        kc = kvc_ref[0, :, 0:KV_WIDTH].astype(F32)
        ms = _dot((kc * kc).astype(BF16), havg)
        k_s[0:ctx, :] = (kc * lax.rsqrt(ms + EPS) * kg).astype(BF16)
        vc = kvc_ref[0, :, KV_WIDTH:2 * KV_WIDTH].astype(F32)
        for j in range(ctx_blocks):
            vt_s[j] = vc[j * ATTN_BLOCK:(j + 1) * ATTN_BLOCK, :].T.astype(BF16)
        lane = lax.broadcasted_iota(jnp.int32, (prep_rows, LANES), 1)
        for r0 in range(0, seq, prep_rows):
            kx = a_ref[0, r0:r0 + prep_rows, _OFF_AK:_OFF_AK + KV_WIDTH].astype(F32)
            ms = _dot((kx * kx).astype(BF16), havg)
            kx = kx * lax.rsqrt(ms + EPS) * kg
            kx = _rope(kx, cos_ref[r0:r0 + prep_rows, :], sin_ref[r0:r0 + prep_rows, :], lane)
            k_s[ctx + r0:ctx + r0 + prep_rows, :] = kx.astype(BF16)
            vx = a_ref[0, r0:r0 + prep_rows, _OFF_AV:_OFF_AV + KV_WIDTH].astype(F32)
            for j in range(prep_rows // ATTN_BLOCK):
                vt_s[ctx_blocks + r0 // ATTN_BLOCK + j] = vx[j * ATTN_BLOCK:(j + 1) * ATTN_BLOCK, :].T.astype(BF16)

    lane = lax.broadcasted_iota(jnp.int32, (ATTN_BLOCK, LANES), 1)
    qg = qg_ref[...]
    n_band = 3
    n_win = n_band * ATTN_BLOCK
    n_keys = n_win + ctx
    key_row = lax.broadcasted_iota(jnp.int32, (n_win, ATTN_BLOCK), 0)
    q_lane = lax.broadcasted_iota(jnp.int32, (n_win, ATTN_BLOCK), 1)
    zeros = jnp.zeros((HEAD_DIM, ATTN_BLOCK), BF16)
    groups = range(ATTN_KV_HEADS)
    sink_rows = [jnp.concatenate([sink_ref[g * ATTN_GROUP + r:g * ATTN_GROUP + r + 1, :] for r in range(ATTN_GROUP)],
                                 axis=1) * LOG2_E for g in groups]

    units = []
    k_alls, vt_alls, valids, rhss = [], [], [], {}
    for sb in range(qb):
        nq = n * qb + sb
        q0 = pl.multiple_of(nq * ATTN_BLOCK, ATTN_BLOCK)
        cos = cos_ref[pl.ds(q0, ATTN_BLOCK), :]
        sin = sin_ref[pl.ds(q0, ATTN_BLOCK), :]
        qts = []
        for j in range(ATTN_WIDTH // LANES):
            qx = a_ref[0, pl.ds(q0, ATTN_BLOCK), j * LANES:(j + 1) * LANES].astype(F32)
            ms = _dot((qx * qx).astype(BF16), havg)
            qx = _rope(qx * lax.rsqrt(ms + EPS) * qg, cos, sin, lane) * (HEAD_DIM ** -0.5 * LOG2_E)
            qts.append(qx.T.astype(BF16))
        blk0 = jnp.clip(nq - 1, 0, seq // ATTN_BLOCK - n_band)
        start = pl.multiple_of(blk0 * ATTN_BLOCK, ATTN_BLOCK)
        valid = jnp.abs(q0 + q_lane - (start + key_row)) <= WINDOW
        valids.append(jnp.concatenate([valid] * ATTN_GROUP, axis=1))
        k_alls.append(jnp.concatenate([k_s[pl.ds(ctx + start, n_win), :], k_s[0:ctx, :]], axis=0))
        vt_loc = vt_s[pl.ds(ctx_blocks + blk0, n_band)]
        vt_alls.append(jnp.concatenate([vt_loc[j] for j in range(n_band)] + [vt_s[j] for j in range(ctx_blocks)],
                                       axis=1))
        for g in groups:
            cols = []
            for r in range(ATTN_GROUP):
                h = g * ATTN_GROUP + r
                piece = qts[h // 2][(h % 2) * HEAD_DIM:(h % 2 + 1) * HEAD_DIM, :]
                cols.append(jnp.concatenate([piece, zeros] if g == 0 else [zeros, piece], axis=0))
            rhss[(sb, g)] = jnp.concatenate(cols, axis=1)
            units.append((sb, g))
        yield

    ss = []
    for sb, g in units:
        s = _dot(k_alls[sb], rhss[(sb, g)])
        ss.append(jnp.concatenate([jnp.where(valids[sb], s[0:n_win], NEG_INF), s[n_win:n_keys]], axis=0))
        yield
    p_bfs, invs = [], []
    for s, (sb, g) in zip(ss, units):
        m = jnp.maximum(jnp.max(s, axis=0, keepdims=True), sink_rows[g])
        p = jnp.exp2(s - m)
        invs.append(1.0 / (jnp.sum(p, axis=0, keepdims=True) + jnp.exp2(sink_rows[g] - m)))
        p_bfs.append(p.astype(BF16))
        yield
    pieces = {sb: [] for sb in range(qb)}
    for (sb, g), p_bf, inv in zip(units, p_bfs, invs):
        vt_g = vt_alls[sb][g * HEAD_DIM:(g + 1) * HEAD_DIM, :]
        for pair in range(ATTN_GROUP // 2):
            lanes = slice(pair * MXU_TILE, (pair + 1) * MXU_TILE)
            ot = _dot(vt_g, p_bf[:, lanes]) * inv[:, lanes]
            pieces[sb] += [ot[:, 0:ATTN_BLOCK], ot[:, ATTN_BLOCK:2 * ATTN_BLOCK]]
        yield
    for sb in range(qb):
        outs = [jnp.concatenate([pieces[sb][2 * j], pieces[sb][2 * j + 1]], axis=0).T
                for j in range(ATTN_Q_HEADS // 2)]
        o_ref[0, sb * ATTN_BLOCK:(sb + 1) * ATTN_BLOCK, :] = jnp.concatenate(outs, axis=1).astype(o_ref.dtype)


def _run_parts(parts, b, name):
    steps = parts[0]["steps"]
    assert all(p["steps"] == steps for p in parts)
    counts = [(len(p["args"]), len(p["out_shape"]), len(p["scratch_shapes"])) for p in parts]
    n_in = sum(c[0] for c in counts)
    n_out = sum(c[1] for c in counts)

    def kern(*refs):
        i_pos, o_pos, s_pos = 0, n_in, n_in + n_out
        bodies = []
        for p, (ci, co, cs) in zip(parts, counts):
            bodies.append(p["kernel"](*refs[i_pos:i_pos + ci], *refs[o_pos:o_pos + co], *refs[s_pos:s_pos + cs]))
            i_pos, o_pos, s_pos = i_pos + ci, o_pos + co, s_pos + cs
        while bodies:
            for body in list(bodies):
                try:
                    next(body)
                except StopIteration:
                    bodies.remove(body)

    outs = pl.pallas_call(
        kern,
        grid=(b, steps),
        in_specs=[s for p in parts for s in p["in_specs"]],
        out_specs=[s for p in parts for s in p["out_specs"]],
        out_shape=[s for p in parts for s in p["out_shape"]],
        scratch_shapes=[s for p in parts for s in p["scratch_shapes"]],
        compiler_params=_cparams(2),
        name=name,
    )(*[a for p in parts for a in p["args"]])
    res, pos = [], 0
    for _, co, _ in counts:
        res.append(outs[pos:pos + co])
        pos += co
    return res


def _attn_parts(a_lat, kv_ctx, cos, sin, qg, kg, sink, havg, qb):
    b, s, wa = a_lat.shape
    ctx = kv_ctx.shape[1]
    nb = s // ATTN_BLOCK
    const2 = lambda bi, n: (0, 0)
    return dict(
        kernel=functools.partial(_attn_kernel, seq=s, ctx=ctx, qb=qb),
        steps=nb // qb,
        in_specs=[pl.BlockSpec((1, s, wa), lambda bi, n: (bi, 0, 0)),
                  pl.BlockSpec((1, ctx, 2 * KV_WIDTH), lambda bi, n: (bi, 0, 0)),
                  pl.BlockSpec((s, LANES), const2),
                  pl.BlockSpec((s, LANES), const2),
                  pl.BlockSpec((1, LANES), const2),
                  pl.BlockSpec((1, LANES), const2),
                  pl.BlockSpec((ATTN_Q_HEADS, LANES), const2),
                  pl.BlockSpec((LANES, LANES), const2)],
        out_specs=[pl.BlockSpec((1, qb * ATTN_BLOCK, ATTN_WIDTH), lambda bi, n: (bi, n, 0))],
        out_shape=[jax.ShapeDtypeStruct((b, s, ATTN_WIDTH), BF16)],
        scratch_shapes=[pltpu.VMEM((ctx + s, KV_WIDTH), BF16),
                        pltpu.VMEM(((ctx + s) // ATTN_BLOCK, KV_WIDTH, ATTN_BLOCK), BF16)],
        args=[a_lat, kv_ctx, cos, sin, qg, kg, sink, havg])


def _half_mask():
    return lax.broadcasted_iota(jnp.int32, (CHUNK, LANES), 1) < HEAD_DIM


def _bdiag_pair(x_pair, lo_half):
    zeros = jnp.zeros_like(x_pair)
    return jnp.concatenate([jnp.where(lo_half, x_pair, zeros), jnp.where(lo_half, zeros, x_pair)], axis=0)


def _bdiag(x_lane, lo_half):
    zeros = jnp.zeros((CHUNK, LANES), x_lane.dtype)
    blocks = []
    for h in range(HEADS_PER_TILE):
        col = (h * HEAD_DIM) // LANES
        piece = x_lane[:, col * LANES:(col + 1) * LANES]
        piece = jnp.where(lo_half if (h * HEAD_DIM) % LANES == 0 else jnp.logical_not(lo_half), piece, zeros)
        blocks.append(jnp.concatenate([piece if c == col else zeros for c in range(MXU_TILE // LANES)], axis=1))
    return jnp.concatenate(blocks, axis=0)


def _unit_tri_inverses(a_list, eye_l, sub_mask, lo_half):
    ads = [jnp.where(sub_mask, a, 0.0) for a in a_list]
    ys = [jnp.where(sub_mask, 0.0, a) for a in a_list]
    ps = [eye_l - ad for ad in ads]
    pws = [ad.astype(BF16) for ad in ads]
    n_levels = int(np.log2(DN_SUB))
    c1, c2, c3 = CHUNK, 2 * CHUNK, 3 * CHUNK
    for level in range(n_levels):
        rhss = [_bdiag(pw, lo_half) for pw in pws]
        if level == 0:
            boths = [_dot(jnp.concatenate([pw, y.astype(BF16)], axis=0), rhs) for pw, y, rhs in zip(pws, ys, rhss)]
            pws = [both[0:c1].astype(BF16) for both in boths]
            ys = [y - both[c1:c2] for y, both in zip(ys, boths)]
        elif level < n_levels - 1:
            boths = [_dot(jnp.concatenate([pw, p.astype(BF16), y.astype(BF16)], axis=0), rhs)
                     for pw, p, y, rhs in zip(pws, ps, ys, rhss)]
            pws = [both[0:c1].astype(BF16) for both in boths]
            ps = [p + both[c1:c2] for p, both in zip(ps, boths)]
            ys = [y + both[c2:c3] for y, both in zip(ys, boths)]
        else:
            boths = [_dot(jnp.concatenate([p.astype(BF16), y.astype(BF16)], axis=0), rhs)
                     for p, y, rhs in zip(ps, ys, rhss)]
            ps = [p + both[0:c1] for p, both in zip(ps, boths)]
            ys = [y + both[c1:c2] for y, both in zip(ys, boths)]
        yield
    assert CHUNK // DN_SUB == 4
    b_bfs = [y.astype(BF16) for y in ys]
    boths = [_dot(jnp.concatenate([b, p.astype(BF16)], axis=0), _bdiag(b, lo_half)) for b, p in zip(b_bfs, ps)]
    yield
    zs = [p - both[c1:c2] for p, both in zip(ps, boths)]
    return [z + _dot(z.astype(BF16), _bdiag(both[0:c1].astype(BF16), lo_half)) for z, both in zip(zs, boths)]


def _dn_factors(abs_, d, arow, dtrow, exp2_ref, tri_ref, eye_t):
    def hi_lo(v):
        hi = v.astype(BF16)
        return hi, (v - hi.astype(F32)).astype(BF16)

    lane = lax.broadcasted_iota(jnp.int32, abs_[0].shape, 1)
    is_g = (lane % 32) < 16
    xhls = []
    for ab in abs_:
        z = ab + dtrow
        softplus = jnp.maximum(z, 0.0) + jnp.log(1.0 + jnp.exp(-jnp.abs(z)))
        x = jnp.where(is_g, -arow * softplus, _sigmoid(ab))
        x = jnp.where(is_g, _dot(tri_ref[d], jnp.concatenate(hi_lo(x), axis=0)), x)
        x_hi, x_lo = hi_lo(x)
        xhls.append(jnp.where(lane < 32, x_hi, x_lo)[:, 0:64])
    y = _dot(jnp.concatenate(xhls, axis=0), exp2_ref[d])
    out = []
    for c in range(len(abs_)):
        gi = y[c * CHUNK:(c + 1) * CHUNK, 0:DN_WIDTH]
        be = y[c * CHUNK:(c + 1) * CHUNK, DN_WIDTH:2 * DN_WIDTH]
        gj = jnp.sum(gi * eye_t, axis=0, keepdims=True)
        out.append((be, gi, jnp.broadcast_to(gj, gi.shape)))
    return out


def _block_mask():
    return (lax.broadcasted_iota(jnp.int32, (MXU_TILE, MXU_TILE), 0) // HEAD_DIM
            == lax.broadcasted_iota(jnp.int32, (MXU_TILE, MXU_TILE), 1) // HEAD_DIM)


def _dnprep_kernel(*refs, cb, want_o):
    if want_o:
        q_ref, k_ref, v_ref, ab_ref = refs[:4]
        refs = refs[4:]
    else:
        q_ref = None
        k_ref, v_ref, ab_ref = refs[:3]
        refs = refs[3:]
    arow_ref, dtrow_ref, exp2_ref, tri_ref, w_o, uv_o, kt_o, dl_o = refs[:8]
    qd_o, in_o = refs[8:10] if want_o else (None, None)
    n_tiles = DN_WIDTH // MXU_TILE
    row = lax.broadcasted_iota(jnp.int32, (CHUNK, MXU_TILE), 0)
    colj = lax.broadcasted_iota(jnp.int32, (CHUNK, MXU_TILE), 1) % HEAD_DIM
    eye_l = (row == colj).astype(F32)
    eye_t = jnp.concatenate([eye_l] * n_tiles, axis=1)
    lo_half = _half_mask()
    arow = arow_ref[...]
    dtrow = dtrow_ref[...]

    fac = {}
    shared = {}
    abs_ = [ab_ref[0, c * CHUNK:(c + 1) * CHUNK, :] for c in range(cb)]
    for d in range(N_DIR):
        for c, f in enumerate(_dn_factors(abs_, d, arow, dtrow, exp2_ref, tri_ref, eye_t)):
            fac[(c, d)] = f
    yield
    for c in range(cb):
        rows = slice(c * CHUNK, (c + 1) * CHUNK)
        for g in range(n_tiles):
            lanes = slice(g * MXU_TILE, (g + 1) * MXU_TILE)
            k_l = k_ref[0, rows, lanes]
            kbd = _bdiag(k_l, lo_half)
            if want_o:
                kq = _dot_nt(jnp.concatenate([k_l, q_ref[0, rows, lanes]], axis=0), kbd)
                shared[(c, g)] = (kq[0:CHUNK], kq[CHUNK:2 * CHUNK])
            else:
                shared[(c, g)] = (_dot_nt(k_l, kbd), None)
        yield

    units = [(c, g, d) for c in range(cb) for g in range(n_tiles) for d in range(N_DIR)]
    decs = []
    for c, g, d in units:
        lanes = slice(g * MXU_TILE, (g + 1) * MXU_TILE)
        be, gi, gj = fac[(c, d)]
        lower = (row > colj) if d == 0 else (row < colj)
        decs.append(jnp.where(lower, jnp.exp(jnp.where(lower, gi[:, lanes] - gj[:, lanes], 0.0)), 0.0))
    a_list = [fac[(c, d)][0][:, g * MXU_TILE:(g + 1) * MXU_TILE] * dec * shared[(c, g)][0]
              for (c, g, d), dec in zip(units, decs)]
    yield
    tinvs = yield from _unit_tri_inverses(a_list, eye_l, (row // DN_SUB) == (colj // DN_SUB), lo_half)

    for idx, ((c, g, d), dec, tinv) in enumerate(zip(units, decs, tinvs)):
        if idx % (n_tiles * N_DIR) == 0:
            yield
        rows = slice(c * CHUNK, (c + 1) * CHUNK)
        lanes = slice(g * MXU_TILE, (g + 1) * MXU_TILE)
        be, gi, _ = fac[(c, d)]
        be, gi = be[:, lanes], gi[:, lanes]
        last = CHUNK - 1 if d == 0 else 0
        e_g = jnp.exp(gi)
        gl_row = gi[last:last + 1, :]
        kf = k_ref[0, rows, lanes].astype(F32)
        vf = v_ref[0, rows, lanes].astype(F32)
        ck = (be * e_g * kf).astype(BF16)
        bv = (be * vf).astype(BF16)
        tinv_bf = tinv.astype(BF16)
        w_parts, uv_parts = [], []
        for pr in range(MXU_TILE // LANES):
            pl_ = slice(pr * LANES, (pr + 1) * LANES)
            rhs = jnp.concatenate([_bdiag_pair(ck[:, pl_], lo_half), _bdiag_pair(bv[:, pl_], lo_half)], axis=1)
            wu = _dot(tinv_bf[:, pl_], rhs)
            w_parts.append(wu[:, 0:LANES])
            uv_parts.append(wu[:, LANES:2 * LANES])
        w_o[0, d, rows, lanes] = jnp.concatenate(w_parts, axis=1).astype(BF16)
        uv_o[0, d, rows, lanes] = jnp.concatenate(uv_parts, axis=1).astype(BF16)
        kt_o[0, d, rows, lanes] = (jnp.exp(gl_row - gi) * kf).astype(BF16)
        dl_o[0, d, c, :, lanes] = jnp.exp(gl_row)
        if want_o:
            qd_o[0, d, rows, lanes] = (e_g * q_ref[0, rows, lanes].astype(F32)).astype(BF16)
            in_o[0, d, rows, lanes] = ((dec + eye_l) * shared[(c, g)][1]).astype(BF16)


def _dnprep_parts(q, k, v, ab, arow, dtrow, exp2, tri, cb):
    want_o = q is not None
    b, t, w = k.shape
    tb = cb * CHUNK
    tok = lambda bi, i: (bi, i, 0)
    const2 = lambda bi, i: (0, 0)
    const3 = lambda bi, i: (0, 0, 0)
    dir_tok = lambda bi, i: (bi, 0, i, 0)
    data = ([q] if want_o else []) + [k, v]
    in_specs = ([pl.BlockSpec((1, tb, w), tok)] * len(data) + [pl.BlockSpec((1, tb, LANES), tok),
                pl.BlockSpec((1, LANES), const2), pl.BlockSpec((1, LANES), const2),
                pl.BlockSpec(exp2.shape, const3), pl.BlockSpec(tri.shape, const3)])
    big = lambda dt: jax.ShapeDtypeStruct((b, N_DIR, t, w), dt)
    big_spec = pl.BlockSpec((1, N_DIR, tb, w), dir_tok)
    out_shape = [big(BF16), big(BF16), big(BF16), jax.ShapeDtypeStruct((b, N_DIR, t // CHUNK, 1, w), F32)]
    out_specs = [big_spec, big_spec, big_spec, pl.BlockSpec((1, N_DIR, cb, 1, w), lambda bi, i: (bi, 0, i, 0, 0))]
    if want_o:
        out_shape += [big(BF16), big(BF16)]
        out_specs += [big_spec, big_spec]
    return dict(kernel=functools.partial(_dnprep_kernel, cb=cb, want_o=want_o), steps=t // tb, in_specs=in_specs,
                out_specs=out_specs, out_shape=out_shape, scratch_shapes=[],
                args=data + [ab, arow, dtrow, exp2, tri])


def _dnscan_kernel(*refs, n_chunk, bb, want_o, have_s0, want_s):
    n_in = 6 if want_o else 4
    dir_refs = [refs[0:n_in], refs[n_in:2 * n_in]]
    pos = 2 * n_in
    s0_ref = refs[pos] if have_s0 else None
    pos += int(have_s0)
    o_refs = refs[pos:pos + N_DIR] if want_o else None
    pos += N_DIR if want_o else 0
    sout_ref = refs[pos] if want_s else None
    pos += int(want_s)
    s_scr = refs[pos]
    n_tiles = DN_WIDTH // MXU_TILE
    i = pl.program_id(1)

    @pl.when(i == 0)
    def _init():
        if have_s0:
            s_scr[...] = s0_ref[...]
        else:
            s_scr[...] = jnp.zeros_like(s_scr)

    bmask = _block_mask()
    lo_half = _half_mask()
    chains = [(bi, d, g) for bi in range(bb) for d in range(N_DIR) for g in range(n_tiles)]

    def body(j, carry):
        cidx = (j, n_chunk - 1 - j)
        r0s = [pl.multiple_of(cidx[d] * CHUNK, CHUNK) for d in range(N_DIR)]
        s_olds, r1s = [], []
        for bi, d, g in chains:
            lanes = slice(g * MXU_TILE, (g + 1) * MXU_TILE)
            w = dir_refs[d][0][bi, 0, pl.ds(r0s[d], CHUNK), lanes]
            if want_o:
                w = jnp.concatenate([w, dir_refs[d][4][bi, 0, pl.ds(r0s[d], CHUNK), lanes]], axis=0)
            s_old = s_scr[bi, d * n_tiles + g]
            s_olds.append(s_old)
            r1s.append(_dot(w, s_old.astype(BF16)))
        u_bfs = []
        for (bi, d, g), r1 in zip(chains, r1s):
            lanes = slice(g * MXU_TILE, (g + 1) * MXU_TILE)
            u_bfs.append((dir_refs[d][1][bi, 0, pl.ds(r0s[d], CHUNK), lanes] - r1[0:CHUNK]).astype(BF16))
        for (bi, d, g), r1, u_bf, s_old in zip(chains, r1s, u_bfs, s_olds):
            lanes = slice(g * MXU_TILE, (g + 1) * MXU_TILE)
            kt = dir_refs[d][2][bi, 0, pl.ds(r0s[d], CHUNK), lanes]
            ds = jnp.where(bmask, _dot_tn(kt, u_bf), 0.0)
            dl = dir_refs[d][3][bi, 0, cidx[d]][:, lanes]
            s_scr[bi, d * n_tiles + g] = s_old * dl + ds
            if want_o:
                intra = dir_refs[d][5][bi, 0, pl.ds(r0s[d], CHUNK), lanes]
                o = r1[CHUNK:2 * CHUNK] + _dot(intra, _bdiag(u_bf, lo_half))
                o_refs[d][bi, pl.ds(r0s[d], CHUNK), lanes] = o.astype(BF16)
        return carry

    lax.fori_loop(0, n_chunk, body, 0)

    if want_s:
        @pl.when(i == pl.num_programs(1) - 1)
        def _fin():
            sout_ref[...] = s_scr[...]


def _dnscan(prep, s0, tb, bb, want_s):
    want_o = len(prep) == 6
    b, _, t, w = prep[0].shape
    n_t = t // tb
    n_chunk = tb // CHUNK
    n_chain = N_DIR * (w // MXU_TILE)

    def specs(d):
        blk = (lambda bi, i: i) if d == 0 else (lambda bi, i: n_t - 1 - i)
        big = pl.BlockSpec((bb, 1, tb, w), lambda bi, i: (bi, d, blk(bi, i), 0))
        dl = pl.BlockSpec((bb, 1, n_chunk, 1, w), lambda bi, i: (bi, d, blk(bi, i), 0, 0))
        return [big, big, big, dl] + ([big, big] if want_o else [])

    in_specs = specs(0) + specs(1)
    args = list(prep) + list(prep)
    state_spec = pl.BlockSpec((bb, n_chain, MXU_TILE, MXU_TILE), lambda bi, i: (bi, 0, 0, 0))
    if s0 is not None:
        in_specs.append(state_spec)
        args.append(s0)
    out_shape, out_specs = [], []
    if want_o:
        out_shape += [jax.ShapeDtypeStruct((b, t, w), BF16)] * N_DIR
        out_specs += [pl.BlockSpec((bb, tb, w), lambda bi, i: (bi, i, 0)),
                      pl.BlockSpec((bb, tb, w), lambda bi, i: (bi, n_t - 1 - i, 0))]
    if want_s:
        out_shape.append(jax.ShapeDtypeStruct((b, n_chain, MXU_TILE, MXU_TILE), F32))
        out_specs.append(state_spec)
    return pl.pallas_call(
        functools.partial(_dnscan_kernel, n_chunk=n_chunk, bb=bb, want_o=want_o, have_s0=s0 is not None,
                          want_s=want_s),
        grid=(b // bb, n_t),
        in_specs=in_specs,
        out_specs=out_specs,
        out_shape=out_shape,
        scratch_shapes=[pltpu.VMEM((bb, n_chain, MXU_TILE, MXU_TILE), F32)],
        compiler_params=_cparams(2),
        name="dnscan",
    )(*args)


def _tail_kernel(x_ref, mod_ref, ya_ref, odf_ref, odb_ref, z_ref, gate_ref, dng_ref, havg_ref, wba_ref, wbd_ref,
                 wo_ref, gn2_ref, w1_ref, w2_ref, o_ref, *, ff_chunk):
    havg = havg_ref[...]
    dng = dng_ref[...]
    yd_parts = []
    for j in range(DN_WIDTH // MXU_TILE):
        sl = slice(j * MXU_TILE, (j + 1) * MXU_TILE)
        od = odf_ref[0, :, sl].astype(F32) + odb_ref[0, :, sl].astype(F32)
        ms = _dot((od * od).astype(BF16), havg)
        z = z_ref[0, :, sl].astype(F32)
        yd_parts.append((od * lax.rsqrt(ms + EPS) * dng * (z * _sigmoid(z))).astype(BF16))
    yd = jnp.concatenate(yd_parts, axis=1)
    ga = gate_ref[0, :, 0:D_MODEL].astype(F32)
    gd = gate_ref[0, :, D_MODEL:2 * D_MODEL].astype(F32)
    y = _sigmoid(ga) * _dot(ya_ref[0], wba_ref[...]) + _sigmoid(gd) * _dot(yd, wbd_ref[...])
    br = _dot(y.astype(BF16), wo_ref[...])
    mod = mod_ref[0]
    out1 = x_ref[0] + mod[2:3] * br
    ms2 = jnp.mean(out1 * out1, axis=-1, keepdims=True)
    hm = (out1 * lax.rsqrt(ms2 + EPS) * (gn2_ref[...] * (1.0 + mod[4:5])) + mod[3:4]).astype(BF16)
    acc = None
    for j in range(D_FF // ff_chunk):
        a = jnp.maximum(_dot(hm, w1_ref[:, j * ff_chunk:(j + 1) * ff_chunk]), 0.0)
        part = _dot((a * a).astype(BF16), w2_ref[j * ff_chunk:(j + 1) * ff_chunk, :])
        acc = part if acc is None else acc + part
    o_ref[0] = out1 + mod[5:6] * acc


def _resident(shape):
    return pl.BlockSpec(shape, lambda bi, i: (0,) * len(shape), pipeline_mode=pl.Buffered(1))


def _tail(x, mod3, y_attn, o_df, o_db, z, gates, dng, havg, wba, wbd, wo, gn2, w1, w2, tm):
    b, t, d = x.shape
    tok = lambda bi, i: (bi, i, 0)
    return pl.pallas_call(
        functools.partial(_tail_kernel, ff_chunk=MLP_FF_CHUNK),
        grid=(b, t // tm),
        in_specs=[pl.BlockSpec((1, tm, d), tok),
                  pl.BlockSpec((1, 6, d), lambda bi, i: (bi, 0, 0)),
                  pl.BlockSpec((1, tm, ATTN_WIDTH), tok),
                  pl.BlockSpec((1, tm, DN_WIDTH), tok),
                  pl.BlockSpec((1, tm, DN_WIDTH), tok),
                  pl.BlockSpec((1, tm, DN_WIDTH), tok),
                  pl.BlockSpec((1, tm, 2 * d), tok),
                  _resident((1, MXU_TILE)),
                  _resident((MXU_TILE, MXU_TILE)),
                  _resident(wba.shape),
                  _resident(wbd.shape),
                  _resident(wo.shape),
                  _resident((1, d)),
                  _resident(w1.shape),
                  _resident(w2.shape)],
        out_specs=pl.BlockSpec((1, tm, d), tok),
        out_shape=jax.ShapeDtypeStruct((b, t, d), F32),
        compiler_params=_cparams(2),
        name="tail",
    )(x, mod3, y_attn, o_df, o_db, z, gates, dng, havg, wba, wbd, wo, gn2, w1, w2)


def _head_avg(n, scale):
    idx = np.arange(n) // HEAD_DIM
    return jnp.asarray((idx[:, None] == idx[None, :]).astype(np.float32) * scale, BF16)


def _dn_expand_matrix():
    n = N_DIR * DN_HEADS
    m = np.zeros((N_DIR, 4 * n, 2 * DN_WIDTH), np.float32)
    for d in range(N_DIR):
        for part in range(2):
            for h in range(DN_HEADS):
                idx = d * DN_HEADS + h
                m[d, part * 2 * n + idx, h * HEAD_DIM:(h + 1) * HEAD_DIM] = 1.0
                m[d, part * 2 * n + n + idx, DN_WIDTH + h * HEAD_DIM:DN_WIDTH + (h + 1) * HEAD_DIM] = 1.0
    return jnp.asarray(m, BF16)


def _tri_matrices():
    i = np.arange(CHUNK)
    low = (i[:, None] >= i[None, :]).astype(np.float32)
    up = (i[:, None] <= i[None, :]).astype(np.float32)
    return jnp.asarray(np.stack([np.concatenate([low, low], axis=1), np.concatenate([up, up], axis=1)]), BF16)


def _rope_tables(seq):
    half = HEAD_DIM // 2
    n_freq = half // 2
    freqs = ROPE_BASE ** (-jnp.arange(n_freq, dtype=F32) / n_freq)
    pos = jnp.arange(seq)
    ang_r = (pos // GRID_W).astype(F32)[:, None] * freqs
    ang_c = (pos % GRID_W).astype(F32)[:, None] * freqs
    cos = jnp.concatenate([jnp.cos(ang_r)] * 2 + [jnp.cos(ang_c)] * 2, axis=1)
    sin = jnp.concatenate([-jnp.sin(ang_r), jnp.sin(ang_r), -jnp.sin(ang_c), jnp.sin(ang_c)], axis=1)
    reps = LANES // HEAD_DIM
    return jnp.tile(cos, (1, reps)), jnp.tile(sin, (1, reps))


def _pad_cols(w, n):
    return jnp.pad(w, ((0, 0), (0, n - w.shape[1])))


def kernel(x, c, ctx, c_ctx, w_ada, b_ada, g_norm1, w_in, q_norm_g, k_norm_g, attn_sink, conv_w, a_log, dt_bias,
           dn_norm_g, w_br_attn, w_br_dn, w_out, g_norm2, w_mlp1, w_mlp2):
    depth = w_ada.shape[0]
    assert depth == 1, "single-layer trunk only"
    b, s, d = x.shape
    n_ctx = ctx.shape[1]
    assert d == D_MODEL and w_in.shape[-1] == _IN_WIDTH
    assert s >= 3 * ATTN_BLOCK and s % ATTN_BLOCK == 0 and s % CHUNK == 0 and n_ctx % CHUNK == 0
    out_dtype = x.dtype
    w_in_t = jnp.swapaxes(w_in[0], 0, 1).astype(BF16)

    mod_rows = -(-(b + 1) // 16) * 16
    cc = jnp.concatenate([c.astype(F32), c_ctx.astype(F32)[None], jnp.zeros((mod_rows - b - 1, d), F32)], axis=0)
    mod = _ada(cc, w_ada[0], b_ada[0])
    mod3 = mod.reshape(mod_rows, 6, d)

    ab_rows = w_in_t[_OFF_DA:_OFF_GA]
    wab_t = jnp.concatenate([ab_rows, ab_rows, jnp.zeros((LANES - 2 * (_OFF_GA - _OFF_DA), d), BF16)], axis=0)
    hsum = _head_avg(MXU_TILE, 1.0)
    segs_lat = ((0, 0, _OFF_DQ, None),
                (0, _OFF_DQ, DN_WIDTH, (0, True, True)), (0, _OFF_DK, DN_WIDTH, (DN_WIDTH, True, False)),
                (0, _OFF_DV, DN_WIDTH, (2 * DN_WIDTH, False, False)),
                (0, _OFF_DZ, DN_WIDTH, None), (0, _OFF_GA, 2 * D_MODEL, None), (1, 0, LANES, None))
    a_lat, q_d, k_d, v_d, z_lat, gates, ab_lat = _inproj(x, mod3, None, g_norm1[0], w_in_t, wab_t, conv_w[0], hsum,
                                                         segs_lat, (BF16, BF16, BF16, BF16, BF16, BF16, F32),
                                                         tm=min(INPROJ_TOKEN_TILE, s))
    segs_ctx = ((0, _OFF_AK, 2 * KV_WIDTH, None),
                (0, _OFF_DK, DN_WIDTH, (0, True, False)), (0, _OFF_DV, DN_WIDTH, (DN_WIDTH, False, False)),
                (1, 0, LANES, None))
    kv_ctx, k_dc, v_dc, ab_ctx = _inproj(ctx, mod3, b, g_norm1[0], w_in_t, wab_t, conv_w[0][:, DN_WIDTH:], hsum,
                                         segs_ctx, (BF16, BF16, BF16, F32), tm=n_ctx)

    cos, sin = _rope_tables(s)
    reps = LANES // HEAD_DIM
    n_gate = N_DIR * DN_HEADS
    arow = _pad_cols(jnp.tile(jnp.concatenate([jnp.exp(a_log[0]).reshape(1, n_gate), jnp.zeros((1, n_gate), F32)],
                                              axis=1), (1, 2)), LANES)
    dtrow = _pad_cols(jnp.tile(jnp.concatenate([dt_bias[0].reshape(1, n_gate), jnp.zeros((1, n_gate), F32)],
                                               axis=1), (1, 2)), LANES)
    exp2, tri = _dn_expand_matrix(), _tri_matrices()
    dn_cb = DN_CHUNKS_PER_STEP
    qb = (s // ATTN_BLOCK) // (s // (dn_cb * CHUNK))
    attn_parts = _attn_parts(a_lat, kv_ctx, cos, sin,
                             jnp.tile(q_norm_g[0].astype(F32), reps)[None],
                             jnp.tile(k_norm_g[0].astype(F32), reps)[None],
                             jnp.broadcast_to(attn_sink[0].astype(F32)[:, None], (ATTN_Q_HEADS, LANES)),
                             _head_avg(LANES, 1.0 / HEAD_DIM), qb)
    (y_attn,), prep_lat = _run_parts(
        [attn_parts, _dnprep_parts(q_d, k_d, v_d, ab_lat, arow, dtrow, exp2, tri, dn_cb)], b, "attn_dnprep")
    (prep_ctx,) = _run_parts([_dnprep_parts(None, k_dc, v_dc, ab_ctx, arow, dtrow, exp2, tri, dn_cb)], b, "dnprep")

    bb = max(q for q in range(1, SCAN_BATCH_ROWS + 1) if b % q == 0)
    (s_ctx,) = _dnscan(prep_ctx, None, tb=n_ctx, bb=bb, want_s=True)
    o_df, o_db = _dnscan(prep_lat, s_ctx, tb=SCAN_TOKEN_TILE, bb=bb, want_s=False)

    out = _tail(x, mod3, y_attn, o_df, o_db, z_lat, gates,
                jnp.tile(dn_norm_g[0].astype(F32), HEADS_PER_TILE)[None], _head_avg(MXU_TILE, 1.0 / HEAD_DIM),
                w_br_attn[0].astype(BF16), w_br_dn[0].astype(BF16), w_out[0].astype(BF16),
                g_norm2[0].reshape(1, d), w_mlp1[0].astype(BF16), w_mlp2[0].astype(BF16), tm=min(TAIL_TOKEN_TILE, s))
    return out.astype(out_dtype)
```

```python
import functools

import numpy as np
import jax
import jax.numpy as jnp
from jax import lax
from jax.experimental import pallas as pl
from jax.experimental.pallas import tpu as pltpu

F32 = jnp.float32
BF16 = jnp.bfloat16

D_MODEL = 1024
GRID_W = 64
HEAD_DIM = 64
ATTN_Q_HEADS = 8
ATTN_KV_HEADS = 2
ATTN_GROUP = ATTN_Q_HEADS // ATTN_KV_HEADS
WINDOW = 128
ATTN_BLOCK = 128
ROPE_BASE = 10000.0
DN_HEADS = 8
CONV_W = 3
CHUNK = 64
N_DIR = 2
D_FF = 4 * D_MODEL
EPS = 1e-6
NEG_INF = -1e30
LOG2_E = float(np.log2(np.e))

ATTN_WIDTH = ATTN_Q_HEADS * HEAD_DIM
KV_WIDTH = ATTN_KV_HEADS * HEAD_DIM
DN_WIDTH = DN_HEADS * HEAD_DIM
LANES = 128
MXU_TILE = 256
HEADS_PER_TILE = MXU_TILE // HEAD_DIM
DN_SUB = 16
VMEM_LIMIT = 56 * 1024 * 1024

INPROJ_TOKEN_TILE = 1024
TAIL_TOKEN_TILE = 512
ADA_COL_TILE = 1536
MLP_FF_CHUNK = 1024
ATTN_PREP_ROWS = 256
DN_CHUNKS_PER_STEP = 4
SCAN_TOKEN_TILE = 256
SCAN_BATCH_ROWS = 4

_OFF_AQ = 0
_OFF_AK = _OFF_AQ + ATTN_WIDTH
_OFF_AV = _OFF_AK + KV_WIDTH
_OFF_DQ = _OFF_AV + KV_WIDTH
_OFF_DK = _OFF_DQ + DN_WIDTH
_OFF_DV = _OFF_DK + DN_WIDTH
_OFF_DZ = _OFF_DV + DN_WIDTH
_OFF_DA = _OFF_DZ + DN_WIDTH
_OFF_DB = _OFF_DA + N_DIR * DN_HEADS
_OFF_GA = _OFF_DB + N_DIR * DN_HEADS
_OFF_GD = _OFF_GA + D_MODEL
_IN_WIDTH = _OFF_GD + D_MODEL


def _sigmoid(x):
    return 0.5 * jnp.tanh(0.5 * x) + 0.5


def _dot(a, b):
    return jnp.dot(a, b, preferred_element_type=F32)


def _dot_nt(a, b):
    return lax.dot_general(a, b, (((1,), (1,)), ((), ())), preferred_element_type=F32)


def _cparams(n_axes):
    return pltpu.CompilerParams(dimension_semantics=("arbitrary",) * n_axes, vmem_limit_bytes=VMEM_LIMIT)


def _ada_kernel(c_ref, w_ref, b_ref, o_ref):
    c = c_ref[...]
    s = c * _sigmoid(c)
    o_ref[...] = _dot(s.astype(BF16), w_ref[...].astype(BF16)) + b_ref[...]


def _ada(cc, w_ada, b_ada):
    rows, d = cc.shape
    n = w_ada.shape[1]
    tn = ADA_COL_TILE
    return pl.pallas_call(
        _ada_kernel,
        grid=(n // tn,),
        in_specs=[pl.BlockSpec((rows, d), lambda j: (0, 0)),
                  pl.BlockSpec((d, tn), lambda j: (0, j)),
                  pl.BlockSpec((1, tn), lambda j: (0, j))],
        out_specs=pl.BlockSpec((rows, tn), lambda j: (0, j)),
        out_shape=jax.ShapeDtypeStruct((rows, n), F32),
        compiler_params=_cparams(1),
        name="ada",
    )(cc, w_ada, b_ada.reshape(1, n))


def _inproj_kernel(x_ref, xprev_ref, xnext_ref, mod_ref, g_ref, wt_ref, wabt_ref, cw_ref, hsum_ref, *out_refs,
                   segs, tm, halo):
    w_refs = (wt_ref, wabt_ref)
    i = pl.program_id(1)
    last = pl.num_programs(1) - 1
    mod = mod_ref[0]
    scale = g_ref[...] * (1.0 + mod[1:2])

    def norm_mod(v):
        ms = jnp.mean(v * v, axis=-1, keepdims=True)
        return (v * lax.rsqrt(ms + EPS) * scale + mod[0:1]).astype(BF16)

    h = norm_mod(x_ref[0])
    h_ext = jnp.concatenate([h, norm_mod(jnp.concatenate([xprev_ref[0], xnext_ref[0]], axis=0))], axis=0)
    keep_prev = (i > 0).astype(F32)
    keep_next = (i < last).astype(F32)
    rows = lax.broadcasted_iota(jnp.int32, (tm, MXU_TILE), 0)
    hsum = hsum_ref[...]
    def plain_piece(o_ref, src, start, lo, width):
        o_ref[0, :, lo:lo + width] = _dot_nt(h, w_refs[src][start + lo:start + lo + width, :]).astype(o_ref.dtype)

    def conv_piece(o_ref, src, start, lo, kind):
        conv_col, do_norm, is_q = kind
        w_rows = w_refs[src][start + lo:start + lo + MXU_TILE, :]
        p_ext = _dot_nt(h_ext, w_rows)
        p, p_halo = p_ext[0:tm], p_ext[tm:tm + 2 * halo]
        yield
        p_prev = jnp.where(rows == 0, p_halo[halo - 1:halo] * keep_prev, pltpu.roll(p, 1, 0))
        p_next = jnp.where(rows == tm - 1, p_halo[halo:halo + 1] * keep_next, pltpu.roll(p, tm - 1, 0))
        cw = cw_ref[:, conv_col + lo:conv_col + lo + MXU_TILE]
        y = p_prev * cw[0:1] + p * cw[1:2] + p_next * cw[2:3]
        y = y * _sigmoid(y)
        if do_norm:
            y = y * lax.rsqrt(_dot((y * y).astype(BF16), hsum) + EPS)
            if is_q:
                y = y * (HEAD_DIM ** -0.5)
        o_ref[0, :, lo:lo + MXU_TILE] = y.astype(o_ref.dtype)

    plain, conv = [], []
    for o_ref, (src, start, size, kind) in zip(out_refs, segs):
        if kind is None:
            plain += [functools.partial(plain_piece, o_ref, src, start, lo, min(MXU_TILE, size - lo))
                      for lo in range(0, size, MXU_TILE)]
        else:
            conv += [functools.partial(conv_piece, o_ref, src, start, lo, kind) for lo in range(0, size, MXU_TILE)]
    pending = None
    while plain or conv or pending is not None:
        started = conv.pop(0)() if conv else None
        if started is not None:
            next(started)
        if pending is not None:
            for _ in pending:
                pass
        pending = started
        for _ in range(-(-len(plain) // (len(conv) + 1)) if plain else 0):
            plain.pop(0)()


def _inproj(x, mod3, mod_row, g_norm, w_t, wab_t, conv_w, hsum, segs, dtypes, tm):
    b, t, d = x.shape
    halo = 8
    r = tm // halo
    nblk = t // halo
    if mod_row is None:
        mod_map = lambda bi, i: (bi, 0, 0)
    else:
        mod_map = lambda bi, i: (mod_row, 0, 0)
    const2 = lambda bi, i: (0, 0)
    out_shape = [jax.ShapeDtypeStruct((b, t, size), dt) for (_, _, size, _), dt in zip(segs, dtypes)]
    out_specs = [pl.BlockSpec((1, tm, size), lambda bi, i: (bi, i, 0)) for (_, _, size, _) in segs]
    return pl.pallas_call(
        functools.partial(_inproj_kernel, segs=segs, tm=tm, halo=halo),
        grid=(b, t // tm),
        in_specs=[pl.BlockSpec((1, tm, d), lambda bi, i: (bi, i, 0)),
                  pl.BlockSpec((1, halo, d), lambda bi, i: (bi, jnp.maximum(i * r - 1, 0), 0)),
                  pl.BlockSpec((1, halo, d), lambda bi, i: (bi, jnp.minimum((i + 1) * r, nblk - 1), 0)),
                  pl.BlockSpec((1, 6, d), mod_map),
                  pl.BlockSpec((1, d), const2),
                  _resident(w_t.shape),
                  _resident(wab_t.shape),
                  _resident(conv_w.shape),
                  _resident(hsum.shape)],
        out_specs=out_specs,
        out_shape=out_shape,
        compiler_params=_cparams(2),
        name="inproj",
    )(x, x, x, mod3, g_norm.reshape(1, d), w_t, wab_t, conv_w, hsum)


def _rope(x, cos, sin, lane):
    swapped = jnp.where((lane % 32) < 16, pltpu.roll(x, LANES - 16, 1), pltpu.roll(x, 16, 1))
    return x * cos + swapped * sin


def _attn_kernel(a_ref, kvc_ref, cos_ref, sin_ref, qg_ref, kg_ref, sink_ref, havg_ref,
                 o_ref, k_s, vt_s, *, seq, ctx, qb):
    n = pl.program_id(1)
    havg = havg_ref[...]
    prep_rows = ATTN_PREP_ROWS
    ctx_blocks = ctx // ATTN_BLOCK

    @pl.when(n == 0)
    def _prep():
        kg = kg_ref[...]
        kc = kvc_ref[0, :, 0:KV_WIDTH].astype(F32)
        ms = _dot((kc * kc).astype(BF16), havg)
        k_s[0:ctx, :] = (kc * lax.rsqrt(ms + EPS) * kg).astype(BF16)
        vc = kvc_ref[0, :, KV_WIDTH:2 * KV_WIDTH].astype(F32)
        for j in range(ctx_blocks):
            vt_s[j] = vc[j * ATTN_BLOCK:(j + 1) * ATTN_BLOCK, :].T.astype(BF16)
        lane = lax.broadcasted_iota(jnp.int32, (prep_rows, LANES), 1)
        for r0 in range(0, seq, prep_rows):
            kx = a_ref[0, r0:r0 + prep_rows, _OFF_AK:_OFF_AK + KV_WIDTH].astype(F32)
            ms = _dot((kx * kx).astype(BF16), havg)
            kx = kx * lax.rsqrt(ms + EPS) * kg
            kx = _rope(kx, cos_ref[r0:r0 + prep_rows, :], sin_ref[r0:r0 + prep_rows, :], lane)
            k_s[ctx + r0:ctx + r0 + prep_rows, :] = kx.astype(BF16)
            vx = a_ref[0, r0:r0 + prep_rows, _OFF_AV:_OFF_AV + KV_WIDTH].astype(F32)
            for j in range(prep_rows // ATTN_BLOCK):
                vt_s[ctx_blocks + r0 // ATTN_BLOCK + j] = vx[j * ATTN_BLOCK:(j + 1) * ATTN_BLOCK, :].T.astype(BF16)

    lane = lax.broadcasted_iota(jnp.int32, (ATTN_BLOCK, LANES), 1)
    qg = qg_ref[...]
    n_band = 3
    n_win = n_band * ATTN_BLOCK
    n_keys = n_win + ctx
    key_row = lax.broadcasted_iota(jnp.int32, (n_win, ATTN_BLOCK), 0)
    q_lane = lax.broadcasted_iota(jnp.int32, (n_win, ATTN_BLOCK), 1)
    zeros = jnp.zeros((HEAD_DIM, ATTN_BLOCK), BF16)
    groups = range(ATTN_KV_HEADS)
    sink_rows = [jnp.concatenate([sink_ref[g * ATTN_GROUP + r:g * ATTN_GROUP + r + 1, :] for r in range(ATTN_GROUP)],
                                 axis=1) * LOG2_E for g in groups]

    units = []
    k_alls, vt_alls, valids, rhss = [], [], [], {}
    for sb in range(qb):
        nq = n * qb + sb
        q0 = pl.multiple_of(nq * ATTN_BLOCK, ATTN_BLOCK)
        cos = cos_ref[pl.ds(q0, ATTN_BLOCK), :]
        sin = sin_ref[pl.ds(q0, ATTN_BLOCK), :]
        qts = []
        for j in range(ATTN_WIDTH // LANES):
            qx = a_ref[0, pl.ds(q0, ATTN_BLOCK), j * LANES:(j + 1) * LANES].astype(F32)
            ms = _dot((qx * qx).astype(BF16), havg)
            qx = _rope(qx * lax.rsqrt(ms + EPS) * qg, cos, sin, lane) * (HEAD_DIM ** -0.5 * LOG2_E)
            qts.append(qx.T.astype(BF16))
        blk0 = jnp.clip(nq - 1, 0, seq // ATTN_BLOCK - n_band)
        start = pl.multiple_of(blk0 * ATTN_BLOCK, ATTN_BLOCK)
        valid = jnp.abs(q0 + q_lane - (start + key_row)) <= WINDOW
        valids.append(jnp.concatenate([valid] * ATTN_GROUP, axis=1))
        k_alls.append(jnp.concatenate([k_s[pl.ds(ctx + start, n_win), :], k_s[0:ctx, :]], axis=0))
        vt_loc = vt_s[pl.ds(ctx_blocks + blk0, n_band)]
        vt_alls.append(jnp.concatenate([vt_loc[j] for j in range(n_band)] + [vt_s[j] for j in range(ctx_blocks)],
                                       axis=1))
        for g in groups:
            cols = []
            for r in range(ATTN_GROUP):
                h = g * ATTN_GROUP + r
                piece = qts[h // 2][(h % 2) * HEAD_DIM:(h % 2 + 1) * HEAD_DIM, :]
                cols.append(jnp.concatenate([piece, zeros] if g == 0 else [zeros, piece], axis=0))
            rhss[(sb, g)] = jnp.concatenate(cols, axis=1)
            units.append((sb, g))
        yield

    ss = []
    for sb, g in units:
        s = _dot(k_alls[sb], rhss[(sb, g)])
        ss.append(jnp.concatenate([jnp.where(valids[sb], s[0:n_win], NEG_INF), s[n_win:n_keys]], axis=0))
        yield
    p_bfs, invs = [], []
    for s, (sb, g) in zip(ss, units):
        m = jnp.maximum(jnp.max(s, axis=0, keepdims=True), sink_rows[g])
        p = jnp.exp2(s - m)
        invs.append(1.0 / (jnp.sum(p, axis=0, keepdims=True) + jnp.exp2(sink_rows[g] - m)))
        p_bfs.append(p.astype(BF16))
        yield
    pieces = {sb: [] for sb in range(qb)}
    for (sb, g), p_bf, inv in zip(units, p_bfs, invs):
        vt_g = vt_alls[sb][g * HEAD_DIM:(g + 1) * HEAD_DIM, :]
        for pair in range(ATTN_GROUP // 2):
            lanes = slice(pair * MXU_TILE, (pair + 1) * MXU_TILE)
            ot = _dot(vt_g, p_bf[:, lanes]) * inv[:, lanes]
            pieces[sb] += [ot[:, 0:ATTN_BLOCK], ot[:, ATTN_BLOCK:2 * ATTN_BLOCK]]
        yield
    for sb in range(qb):
        outs = [jnp.concatenate([pieces[sb][2 * j], pieces[sb][2 * j + 1]], axis=0).T
                for j in range(ATTN_Q_HEADS // 2)]
        o_ref[0, sb * ATTN_BLOCK:(sb + 1) * ATTN_BLOCK, :] = jnp.concatenate(outs, axis=1).astype(o_ref.dtype)


def _run_parts(parts, b, name):
    steps = parts[0]["steps"]
    assert all(p["steps"] == steps for p in parts)
    counts = [(len(p["args"]), len(p["out_shape"]), len(p["scratch_shapes"])) for p in parts]
    n_in = sum(c[0] for c in counts)
    n_out = sum(c[1] for c in counts)

    def kern(*refs):
        i_pos, o_pos, s_pos = 0, n_in, n_in + n_out
        bodies = []
        for p, (ci, co, cs) in zip(parts, counts):
            bodies.append(p["kernel"](*refs[i_pos:i_pos + ci], *refs[o_pos:o_pos + co], *refs[s_pos:s_pos + cs]))
            i_pos, o_pos, s_pos = i_pos + ci, o_pos + co, s_pos + cs
        while bodies:
            for body in list(bodies):
                try:
                    next(body)
                except StopIteration:
                    bodies.remove(body)

    outs = pl.pallas_call(
        kern,
        grid=(b, steps),
        in_specs=[s for p in parts for s in p["in_specs"]],
        out_specs=[s for p in parts for s in p["out_specs"]],
        out_shape=[s for p in parts for s in p["out_shape"]],
        scratch_shapes=[s for p in parts for s in p["scratch_shapes"]],
        compiler_params=_cparams(2),
        name=name,
    )(*[a for p in parts for a in p["args"]])
    res, pos = [], 0
    for _, co, _ in counts:
        res.append(outs[pos:pos + co])
        pos += co
    return res


def _attn_parts(a_lat, kv_ctx, cos, sin, qg, kg, sink, havg, qb):
    b, s, wa = a_lat.shape
    ctx = kv_ctx.shape[1]
    nb = s // ATTN_BLOCK
    const2 = lambda bi, n: (0, 0)
    return dict(
        kernel=functools.partial(_attn_kernel, seq=s, ctx=ctx, qb=qb),
        steps=nb // qb,
        in_specs=[pl.BlockSpec((1, s, wa), lambda bi, n: (bi, 0, 0)),
                  pl.BlockSpec((1, ctx, 2 * KV_WIDTH), lambda bi, n: (bi, 0, 0)),
                  pl.BlockSpec((s, LANES), const2),
                  pl.BlockSpec((s, LANES), const2),
                  pl.BlockSpec((1, LANES), const2),
                  pl.BlockSpec((1, LANES), const2),
                  pl.BlockSpec((ATTN_Q_HEADS, LANES), const2),
                  pl.BlockSpec((LANES, LANES), const2)],
        out_specs=[pl.BlockSpec((1, qb * ATTN_BLOCK, ATTN_WIDTH), lambda bi, n: (bi, n, 0))],
        out_shape=[jax.ShapeDtypeStruct((b, s, ATTN_WIDTH), BF16)],
        scratch_shapes=[pltpu.VMEM((ctx + s, KV_WIDTH), BF16),
                        pltpu.VMEM(((ctx + s) // ATTN_BLOCK, KV_WIDTH, ATTN_BLOCK), BF16)],
        args=[a_lat, kv_ctx, cos, sin, qg, kg, sink, havg])


def _half_mask():
    return lax.broadcasted_iota(jnp.int32, (CHUNK, LANES), 1) < HEAD_DIM


def _bdiag_pair(x_pair, lo_half):
    zeros = jnp.zeros_like(x_pair)
    return jnp.concatenate([jnp.where(lo_half, x_pair, zeros), jnp.where(lo_half, zeros, x_pair)], axis=0)


def _bdiag(x_lane, lo_half):
    zeros = jnp.zeros((CHUNK, LANES), x_lane.dtype)
    blocks = []
    for h in range(HEADS_PER_TILE):
        col = (h * HEAD_DIM) // LANES
        piece = x_lane[:, col * LANES:(col + 1) * LANES]
        piece = jnp.where(lo_half if (h * HEAD_DIM) % LANES == 0 else jnp.logical_not(lo_half), piece, zeros)
        blocks.append(jnp.concatenate([piece if c == col else zeros for c in range(MXU_TILE // LANES)], axis=1))
    return jnp.concatenate(blocks, axis=0)


def _unit_tri_inverses(a_list, eye_l, sub_mask, lo_half):
    ads = [jnp.where(sub_mask, a, 0.0) for a in a_list]
    ys = [jnp.where(sub_mask, 0.0, a) for a in a_list]
    ps = [eye_l - ad for ad in ads]
    pws = [ad.astype(BF16) for ad in ads]
    n_levels = int(np.log2(DN_SUB))
    c1, c2, c3 = CHUNK, 2 * CHUNK, 3 * CHUNK
    for level in range(n_levels):
        rhss = [_bdiag(pw, lo_half) for pw in pws]
        if level == 0:
            boths = [_dot(jnp.concatenate([pw, y.astype(BF16)], axis=0), rhs) for pw, y, rhs in zip(pws, ys, rhss)]
            pws = [both[0:c1].astype(BF16) for both in boths]
            ys = [y - both[c1:c2] for y, both in zip(ys, boths)]
        elif level < n_levels - 1:
            boths = [_dot(jnp.concatenate([pw, p.astype(BF16), y.astype(BF16)], axis=0), rhs)
                     for pw, p, y, rhs in zip(pws, ps, ys, rhss)]
            pws = [both[0:c1].astype(BF16) for both in boths]
            ps = [p + both[c1:c2] for p, both in zip(ps, boths)]
            ys = [y + both[c2:c3] for y, both in zip(ys, boths)]
        else:
            boths = [_dot(jnp.concatenate([p.astype(BF16), y.astype(BF16)], axis=0), rhs)
                     for p, y, rhs in zip(ps, ys, rhss)]
            ps = [p + both[0:c1] for p, both in zip(ps, boths)]
            ys = [y + both[c1:c2] for y, both in zip(ys, boths)]
        yield
    assert CHUNK // DN_SUB == 4
    b_bfs = [y.astype(BF16) for y in ys]
    boths = [_dot(jnp.concatenate([b, p.astype(BF16)], axis=0), _bdiag(b, lo_half)) for b, p in zip(b_bfs, ps)]
    yield
    zs = [p - both[c1:c2] for p, both in zip(ps, boths)]
    return [z + _dot(z.astype(BF16), _bdiag(both[0:c1].astype(BF16), lo_half)) for z, both in zip(zs, boths)]


def _dn_factors(abs_, d, arow, dtrow, exp2_ref, tri_ref, eye_t):
    def hi_lo(v):
        hi = v.astype(BF16)
        return hi, (v - hi.astype(F32)).astype(BF16)

    lane = lax.broadcasted_iota(jnp.int32, abs_[0].shape, 1)
    is_g = (lane % 32) < 16
    xhls = []
    for ab in abs_:
        z = ab + dtrow
        softplus = jnp.maximum(z, 0.0) + jnp.log(1.0 + jnp.exp(-jnp.abs(z)))
        x = jnp.where(is_g, -arow * softplus, _sigmoid(ab))
        x = jnp.where(is_g, _dot(tri_ref[d], jnp.concatenate(hi_lo(x), axis=0)), x)
        x_hi, x_lo = hi_lo(x)
        xhls.append(jnp.where(lane < 32, x_hi, x_lo)[:, 0:64])
    y = _dot(jnp.concatenate(xhls, axis=0), exp2_ref[d])
    out = []
    for c in range(len(abs_)):
        gi = y[c * CHUNK:(c + 1) * CHUNK, 0:DN_WIDTH]
        be = y[c * CHUNK:(c + 1) * CHUNK, DN_WIDTH:2 * DN_WIDTH]
        gj = jnp.sum(gi * eye_t, axis=0, keepdims=True)
        out.append((be, gi, jnp.broadcast_to(gj, gi.shape)))
    return out


def _head_transpose(x_lane, lo_half):
    parts = []
    for pr in range(MXU_TILE // LANES):
        xt = _bdiag_pair(x_lane[:, pr * LANES:(pr + 1) * LANES].astype(F32), lo_half).T
        parts.append(xt[0:CHUNK] + xt[CHUNK:2 * CHUNK])
    return jnp.concatenate(parts, axis=1)


def _dnprep_kernel(*refs, cb, want_o):
    if want_o:
        q_ref, k_ref, v_ref, ab_ref = refs[:4]
        refs = refs[4:]
    else:
        q_ref = None
        k_ref, v_ref, ab_ref = refs[:3]
        refs = refs[3:]
    arow_ref, dtrow_ref, exp2_ref, tri_ref, w_o, uv_o, kt_o, dl_o = refs[:8]
    qd_o, in_o = refs[8:10] if want_o else (None, None)
    n_tiles = DN_WIDTH // MXU_TILE
    row = lax.broadcasted_iota(jnp.int32, (CHUNK, MXU_TILE), 0)
    colj = lax.broadcasted_iota(jnp.int32, (CHUNK, MXU_TILE), 1) % HEAD_DIM
    eye_l = (row == colj).astype(F32)
    eye_t = jnp.concatenate([eye_l] * n_tiles, axis=1)
    lo_half = _half_mask()
    arow = arow_ref[...]
    dtrow = dtrow_ref[...]

    fac = {}
    shared = {}
    abs_ = [ab_ref[0, c * CHUNK:(c + 1) * CHUNK, :] for c in range(cb)]
    for d in range(N_DIR):
        for c, f in enumerate(_dn_factors(abs_, d, arow, dtrow, exp2_ref, tri_ref, eye_t)):
            fac[(c, d)] = f
    yield
    for c in range(cb):
        rows = slice(c * CHUNK, (c + 1) * CHUNK)
        for g in range(n_tiles):
            lanes = slice(g * MXU_TILE, (g + 1) * MXU_TILE)
            k_l = k_ref[0, rows, lanes]
            kbd = _bdiag(k_l, lo_half)
            if want_o:
                kq = _dot_nt(jnp.concatenate([k_l, q_ref[0, rows, lanes]], axis=0), kbd)
                shared[(c, g)] = (kq[0:CHUNK], kq[CHUNK:2 * CHUNK])
            else:
                shared[(c, g)] = (_dot_nt(k_l, kbd), None)
        yield

    units = [(c, g, d) for c in range(cb) for g in range(n_tiles) for d in range(N_DIR)]
    decs = []
    for c, g, d in units:
        lanes = slice(g * MXU_TILE, (g + 1) * MXU_TILE)
        be, gi, gj = fac[(c, d)]
        lower = (row > colj) if d == 0 else (row < colj)
        decs.append(jnp.where(lower, jnp.exp(jnp.where(lower, gi[:, lanes] - gj[:, lanes], 0.0)), 0.0))
    a_list = [fac[(c, d)][0][:, g * MXU_TILE:(g + 1) * MXU_TILE] * dec * shared[(c, g)][0]
              for (c, g, d), dec in zip(units, decs)]
    yield
    tinvs = yield from _unit_tri_inverses(a_list, eye_l, (row // DN_SUB) == (colj // DN_SUB), lo_half)

    for idx, ((c, g, d), dec, tinv) in enumerate(zip(units, decs, tinvs)):
        if idx % (n_tiles * N_DIR) == 0:
            yield
        rows = slice(c * CHUNK, (c + 1) * CHUNK)
        lanes = slice(g * MXU_TILE, (g + 1) * MXU_TILE)
        be, gi, _ = fac[(c, d)]
        be, gi = be[:, lanes], gi[:, lanes]
        last = CHUNK - 1 if d == 0 else 0
        e_g = jnp.exp(gi)
        gl_row = gi[last:last + 1, :]
        kf = k_ref[0, rows, lanes].astype(F32)
        vf = v_ref[0, rows, lanes].astype(F32)
        ck = (be * e_g * kf).astype(BF16)
        bv = (be * vf).astype(BF16)
        tinv_bf = tinv.astype(BF16)
        w_parts, uv_parts = [], []
        for pr in range(MXU_TILE // LANES):
            pl_ = slice(pr * LANES, (pr + 1) * LANES)
            rhs = jnp.concatenate([_bdiag_pair(ck[:, pl_], lo_half), _bdiag_pair(bv[:, pl_], lo_half)], axis=1)
            wu = _dot(tinv_bf[:, pl_], rhs)
            w_parts.append(wu[:, 0:LANES])
            uv_parts.append(wu[:, LANES:2 * LANES])
        w_o[0, d, rows, lanes] = jnp.concatenate(w_parts, axis=1).astype(BF16)
        uv_o[0, d, rows, lanes] = jnp.concatenate(uv_parts, axis=1).astype(BF16)
        kt_o[0, d, rows, lanes] = (jnp.exp(gl_row - gi) * kf).astype(BF16)
        dl_o[0, d, c, :, lanes] = jnp.exp(gl_row)
        if want_o:
            qd_o[0, d, rows, lanes] = (e_g * q_ref[0, rows, lanes].astype(F32)).astype(BF16)
            in_o[0, d, rows, lanes] = ((dec + eye_l) * shared[(c, g)][1]).astype(BF16)


def _dnprep_parts(q, k, v, ab, arow, dtrow, exp2, tri, cb):
    want_o = q is not None
    b, t, w = k.shape
    tb = cb * CHUNK
    tok = lambda bi, i: (bi, i, 0)
    const2 = lambda bi, i: (0, 0)
    const3 = lambda bi, i: (0, 0, 0)
    dir_tok = lambda bi, i: (bi, 0, i, 0)
    data = ([q] if want_o else []) + [k, v]
    in_specs = ([pl.BlockSpec((1, tb, w), tok)] * len(data) + [pl.BlockSpec((1, tb, LANES), tok),
                pl.BlockSpec((1, LANES), const2), pl.BlockSpec((1, LANES), const2),
                pl.BlockSpec(exp2.shape, const3), pl.BlockSpec(tri.shape, const3)])
    big = lambda dt: jax.ShapeDtypeStruct((b, N_DIR, t, w), dt)
    big_spec = pl.BlockSpec((1, N_DIR, tb, w), dir_tok)
    out_shape = [big(BF16), big(BF16), big(BF16), jax.ShapeDtypeStruct((b, N_DIR, t // CHUNK, 1, w), F32)]
    out_specs = [big_spec, big_spec, big_spec, pl.BlockSpec((1, N_DIR, cb, 1, w), lambda bi, i: (bi, 0, i, 0, 0))]
    if want_o:
        out_shape += [big(BF16), big(BF16)]
        out_specs += [big_spec, big_spec]
    return dict(kernel=functools.partial(_dnprep_kernel, cb=cb, want_o=want_o), steps=t // tb, in_specs=in_specs,
                out_specs=out_specs, out_shape=out_shape, scratch_shapes=[],
                args=data + [ab, arow, dtrow, exp2, tri])


def _dnscan_kernel(*refs, n_chunk, bb, want_o, have_s0, want_s):
    n_in = 6 if want_o else 4
    dir_refs = [refs[0:n_in], refs[n_in:2 * n_in]]
    pos = 2 * n_in
    s0_ref = refs[pos] if have_s0 else None
    pos += int(have_s0)
    o_refs = refs[pos:pos + N_DIR] if want_o else None
    pos += N_DIR if want_o else 0
    sout_ref = refs[pos] if want_s else None
    pos += int(want_s)
    s_scr = refs[pos]
    n_tiles = DN_WIDTH // MXU_TILE
    i = pl.program_id(1)

    @pl.when(i == 0)
    def _init():
        if have_s0:
            s_scr[...] = s0_ref[...]
        else:
            s_scr[...] = jnp.zeros_like(s_scr)

    lo_half = _half_mask()
    chains = [(bi, d, g) for bi in range(bb) for d in range(N_DIR) for g in range(n_tiles)]

    def body(j, carry):
        cidx = (j, n_chunk - 1 - j)
        r0s = [pl.multiple_of(cidx[d] * CHUNK, CHUNK) for d in range(N_DIR)]
        s_olds, r1s = [], []
        for bi, d, g in chains:
            lanes = slice(g * MXU_TILE, (g + 1) * MXU_TILE)
            w = dir_refs[d][0][bi, 0, pl.ds(r0s[d], CHUNK), lanes]
            if want_o:
                w = jnp.concatenate([w, dir_refs[d][4][bi, 0, pl.ds(r0s[d], CHUNK), lanes]], axis=0)
            s_old = s_scr[bi, d * n_tiles + g]
            s_olds.append(s_old)
            r1s.append(_dot(w, _bdiag(s_old.astype(BF16), lo_half)))
        u_bfs = []
        for (bi, d, g), r1 in zip(chains, r1s):
            lanes = slice(g * MXU_TILE, (g + 1) * MXU_TILE)
            u_bfs.append((dir_refs[d][1][bi, 0, pl.ds(r0s[d], CHUNK), lanes] - r1[0:CHUNK]).astype(BF16))
        for (bi, d, g), r1, u_bf, s_old in zip(chains, r1s, u_bfs, s_olds):
            lanes = slice(g * MXU_TILE, (g + 1) * MXU_TILE)
            lhs = _head_transpose(dir_refs[d][2][bi, 0, pl.ds(r0s[d], CHUNK), lanes], lo_half).astype(BF16)
            if want_o:
                lhs = jnp.concatenate([dir_refs[d][5][bi, 0, pl.ds(r0s[d], CHUNK), lanes], lhs], axis=0)
            r2 = _dot(lhs, _bdiag(u_bf, lo_half))
            dl = dir_refs[d][3][bi, 0, cidx[d]][:, lanes]
            s_scr[bi, d * n_tiles + g] = s_old * dl + r2[r2.shape[0] - CHUNK:, :]
            if want_o:
                o_refs[d][bi, pl.ds(r0s[d], CHUNK), lanes] = (r1[CHUNK:2 * CHUNK] + r2[0:CHUNK]).astype(BF16)
        return carry

    lax.fori_loop(0, n_chunk, body, 0)

    if want_s:
        @pl.when(i == pl.num_programs(1) - 1)
        def _fin():
            sout_ref[...] = s_scr[...]


def _dnscan(prep, s0, tb, bb, want_s):
    want_o = len(prep) == 6
    b, _, t, w = prep[0].shape
    n_t = t // tb
    n_chunk = tb // CHUNK
    n_chain = N_DIR * (w // MXU_TILE)

    def specs(d):
        blk = (lambda bi, i: i) if d == 0 else (lambda bi, i: n_t - 1 - i)
        big = pl.BlockSpec((bb, 1, tb, w), lambda bi, i: (bi, d, blk(bi, i), 0))
        dl = pl.BlockSpec((bb, 1, n_chunk, 1, w), lambda bi, i: (bi, d, blk(bi, i), 0, 0))
        return [big, big, big, dl] + ([big, big] if want_o else [])

    in_specs = specs(0) + specs(1)
    args = list(prep) + list(prep)
    state_spec = pl.BlockSpec((bb, n_chain, CHUNK, MXU_TILE), lambda bi, i: (bi, 0, 0, 0))
    if s0 is not None:
        in_specs.append(state_spec)
        args.append(s0)
    out_shape, out_specs = [], []
    if want_o:
        out_shape += [jax.ShapeDtypeStruct((b, t, w), BF16)] * N_DIR
        out_specs += [pl.BlockSpec((bb, tb, w), lambda bi, i: (bi, i, 0)),
                      pl.BlockSpec((bb, tb, w), lambda bi, i: (bi, n_t - 1 - i, 0))]
    if want_s:
        out_shape.append(jax.ShapeDtypeStruct((b, n_chain, CHUNK, MXU_TILE), F32))
        out_specs.append(state_spec)
    return pl.pallas_call(
        functools.partial(_dnscan_kernel, n_chunk=n_chunk, bb=bb, want_o=want_o, have_s0=s0 is not None,
                          want_s=want_s),
        grid=(b // bb, n_t),
        in_specs=in_specs,
        out_specs=out_specs,
        out_shape=out_shape,
        scratch_shapes=[pltpu.VMEM((bb, n_chain, CHUNK, MXU_TILE), F32)],
        compiler_params=_cparams(2),
        name="dnscan",
    )(*args)


def _tail_kernel(x_ref, mod_ref, ya_ref, odf_ref, odb_ref, z_ref, gate_ref, dng_ref, havg_ref, wba_ref, wbd_ref,
                 wo_ref, gn2_ref, w1_ref, w2_ref, o_ref, *, ff_chunk):
    havg = havg_ref[...]
    dng = dng_ref[...]
    yd_parts = []
    for j in range(DN_WIDTH // MXU_TILE):
        sl = slice(j * MXU_TILE, (j + 1) * MXU_TILE)
        od = odf_ref[0, :, sl].astype(F32) + odb_ref[0, :, sl].astype(F32)
        ms = _dot((od * od).astype(BF16), havg)
        z = z_ref[0, :, sl].astype(F32)
        yd_parts.append((od * lax.rsqrt(ms + EPS) * dng * (z * _sigmoid(z))).astype(BF16))
    yd = jnp.concatenate(yd_parts, axis=1)
    ga = gate_ref[0, :, 0:D_MODEL].astype(F32)
    gd = gate_ref[0, :, D_MODEL:2 * D_MODEL].astype(F32)
    y = _sigmoid(ga) * _dot(ya_ref[0], wba_ref[...]) + _sigmoid(gd) * _dot(yd, wbd_ref[...])
    br = _dot(y.astype(BF16), wo_ref[...])
    mod = mod_ref[0]
    out1 = x_ref[0] + mod[2:3] * br
    ms2 = jnp.mean(out1 * out1, axis=-1, keepdims=True)
    hm = (out1 * lax.rsqrt(ms2 + EPS) * (gn2_ref[...] * (1.0 + mod[4:5])) + mod[3:4]).astype(BF16)
    acc = None
    for j in range(D_FF // ff_chunk):
        a = jnp.maximum(_dot(hm, w1_ref[:, j * ff_chunk:(j + 1) * ff_chunk]), 0.0)
        part = _dot((a * a).astype(BF16), w2_ref[j * ff_chunk:(j + 1) * ff_chunk, :])
        acc = part if acc is None else acc + part
    o_ref[0] = out1 + mod[5:6] * acc


def _resident(shape):
    return pl.BlockSpec(shape, lambda bi, i: (0,) * len(shape), pipeline_mode=pl.Buffered(1))


def _tail(x, mod3, y_attn, o_df, o_db, z, gates, dng, havg, wba, wbd, wo, gn2, w1, w2, tm):
    b, t, d = x.shape
    tok = lambda bi, i: (bi, i, 0)
    return pl.pallas_call(
        functools.partial(_tail_kernel, ff_chunk=MLP_FF_CHUNK),
        grid=(b, t // tm),
        in_specs=[pl.BlockSpec((1, tm, d), tok),
                  pl.BlockSpec((1, 6, d), lambda bi, i: (bi, 0, 0)),
                  pl.BlockSpec((1, tm, ATTN_WIDTH), tok),
                  pl.BlockSpec((1, tm, DN_WIDTH), tok),
                  pl.BlockSpec((1, tm, DN_WIDTH), tok),
                  pl.BlockSpec((1, tm, DN_WIDTH), tok),
                  pl.BlockSpec((1, tm, 2 * d), tok),
                  _resident((1, MXU_TILE)),
                  _resident((MXU_TILE, MXU_TILE)),
                  _resident(wba.shape),
                  _resident(wbd.shape),
                  _resident(wo.shape),
                  _resident((1, d)),
                  _resident(w1.shape),
                  _resident(w2.shape)],
        out_specs=pl.BlockSpec((1, tm, d), tok),
        out_shape=jax.ShapeDtypeStruct((b, t, d), F32),
        compiler_params=_cparams(2),
        name="tail",
    )(x, mod3, y_attn, o_df, o_db, z, gates, dng, havg, wba, wbd, wo, gn2, w1, w2)


def _head_avg(n, scale):
    idx = np.arange(n) // HEAD_DIM
    return jnp.asarray((idx[:, None] == idx[None, :]).astype(np.float32) * scale, BF16)


def _dn_expand_matrix():
    n = N_DIR * DN_HEADS
    m = np.zeros((N_DIR, 4 * n, 2 * DN_WIDTH), np.float32)
    for d in range(N_DIR):
        for part in range(2):
            for h in range(DN_HEADS):
                idx = d * DN_HEADS + h
                m[d, part * 2 * n + idx, h * HEAD_DIM:(h + 1) * HEAD_DIM] = 1.0
                m[d, part * 2 * n + n + idx, DN_WIDTH + h * HEAD_DIM:DN_WIDTH + (h + 1) * HEAD_DIM] = 1.0
    return jnp.asarray(m, BF16)


def _tri_matrices():
    i = np.arange(CHUNK)
    low = (i[:, None] >= i[None, :]).astype(np.float32)
    up = (i[:, None] <= i[None, :]).astype(np.float32)
    return jnp.asarray(np.stack([np.concatenate([low, low], axis=1), np.concatenate([up, up], axis=1)]), BF16)


def _rope_tables(seq):
    half = HEAD_DIM // 2
    n_freq = half // 2
    freqs = ROPE_BASE ** (-jnp.arange(n_freq, dtype=F32) / n_freq)
    pos = jnp.arange(seq)
    ang_r = (pos // GRID_W).astype(F32)[:, None] * freqs
    ang_c = (pos % GRID_W).astype(F32)[:, None] * freqs
    cos = jnp.concatenate([jnp.cos(ang_r)] * 2 + [jnp.cos(ang_c)] * 2, axis=1)
    sin = jnp.concatenate([-jnp.sin(ang_r), jnp.sin(ang_r), -jnp.sin(ang_c), jnp.sin(ang_c)], axis=1)
    reps = LANES // HEAD_DIM
    return jnp.tile(cos, (1, reps)), jnp.tile(sin, (1, reps))


def _pad_cols(w, n):
    return jnp.pad(w, ((0, 0), (0, n - w.shape[1])))


def kernel(x, c, ctx, c_ctx, w_ada, b_ada, g_norm1, w_in, q_norm_g, k_norm_g, attn_sink, conv_w, a_log, dt_bias,
           dn_norm_g, w_br_attn, w_br_dn, w_out, g_norm2, w_mlp1, w_mlp2):
    depth = w_ada.shape[0]
    assert depth == 1, "single-layer trunk only"
    b, s, d = x.shape
    n_ctx = ctx.shape[1]
    assert d == D_MODEL and w_in.shape[-1] == _IN_WIDTH
    assert s >= 3 * ATTN_BLOCK and s % ATTN_BLOCK == 0 and s % CHUNK == 0 and n_ctx % CHUNK == 0
    out_dtype = x.dtype
    w_in_t = jnp.swapaxes(w_in[0], 0, 1).astype(BF16)

    mod_rows = -(-(b + 1) // 16) * 16
    cc = jnp.concatenate([c.astype(F32), c_ctx.astype(F32)[None], jnp.zeros((mod_rows - b - 1, d), F32)], axis=0)
    mod = _ada(cc, w_ada[0], b_ada[0])
    mod3 = mod.reshape(mod_rows, 6, d)

    ab_rows = w_in_t[_OFF_DA:_OFF_GA]
    wab_t = jnp.concatenate([ab_rows, ab_rows, jnp.zeros((LANES - 2 * (_OFF_GA - _OFF_DA), d), BF16)], axis=0)
    hsum = _head_avg(MXU_TILE, 1.0)
    segs_lat = ((0, 0, _OFF_DQ, None),
                (0, _OFF_DQ, DN_WIDTH, (0, True, True)), (0, _OFF_DK, DN_WIDTH, (DN_WIDTH, True, False)),
                (0, _OFF_DV, DN_WIDTH, (2 * DN_WIDTH, False, False)),
                (0, _OFF_DZ, DN_WIDTH, None), (0, _OFF_GA, 2 * D_MODEL, None), (1, 0, LANES, None))
    a_lat, q_d, k_d, v_d, z_lat, gates, ab_lat = _inproj(x, mod3, None, g_norm1[0], w_in_t, wab_t, conv_w[0], hsum,
                                                         segs_lat, (BF16, BF16, BF16, BF16, BF16, BF16, F32),
                                                         tm=min(INPROJ_TOKEN_TILE, s))
    segs_ctx = ((0, _OFF_AK, 2 * KV_WIDTH, None),
                (0, _OFF_DK, DN_WIDTH, (0, True, False)), (0, _OFF_DV, DN_WIDTH, (DN_WIDTH, False, False)),
                (1, 0, LANES, None))
    kv_ctx, k_dc, v_dc, ab_ctx = _inproj(ctx, mod3, b, g_norm1[0], w_in_t, wab_t, conv_w[0][:, DN_WIDTH:], hsum,
                                         segs_ctx, (BF16, BF16, BF16, F32), tm=n_ctx)

    cos, sin = _rope_tables(s)
    reps = LANES // HEAD_DIM
    n_gate = N_DIR * DN_HEADS
    arow = _pad_cols(jnp.tile(jnp.concatenate([jnp.exp(a_log[0]).reshape(1, n_gate), jnp.zeros((1, n_gate), F32)],
                                              axis=1), (1, 2)), LANES)
    dtrow = _pad_cols(jnp.tile(jnp.concatenate([dt_bias[0].reshape(1, n_gate), jnp.zeros((1, n_gate), F32)],
                                               axis=1), (1, 2)), LANES)
    exp2, tri = _dn_expand_matrix(), _tri_matrices()
    dn_cb = DN_CHUNKS_PER_STEP
    qb = (s // ATTN_BLOCK) // (s // (dn_cb * CHUNK))
    attn_parts = _attn_parts(a_lat, kv_ctx, cos, sin,
                             jnp.tile(q_norm_g[0].astype(F32), reps)[None],
                             jnp.tile(k_norm_g[0].astype(F32), reps)[None],
                             jnp.broadcast_to(attn_sink[0].astype(F32)[:, None], (ATTN_Q_HEADS, LANES)),
                             _head_avg(LANES, 1.0 / HEAD_DIM), qb)
    (y_attn,), prep_lat = _run_parts(
        [attn_parts, _dnprep_parts(q_d, k_d, v_d, ab_lat, arow, dtrow, exp2, tri, dn_cb)], b, "attn_dnprep")
    (prep_ctx,) = _run_parts([_dnprep_parts(None, k_dc, v_dc, ab_ctx, arow, dtrow, exp2, tri, dn_cb)], b, "dnprep")

    bb = max(q for q in range(1, SCAN_BATCH_ROWS + 1) if b % q == 0)
    (s_ctx,) = _dnscan(prep_ctx, None, tb=n_ctx, bb=bb, want_s=True)
    o_df, o_db = _dnscan(prep_lat, s_ctx, tb=SCAN_TOKEN_TILE, bb=bb, want_s=False)

    out = _tail(x, mod3, y_attn, o_df, o_db, z_lat, gates,
                jnp.tile(dn_norm_g[0].astype(F32), HEADS_PER_TILE)[None], _head_avg(MXU_TILE, 1.0 / HEAD_DIM),
                w_br_attn[0].astype(BF16), w_br_dn[0].astype(BF16), w_out[0].astype(BF16),
                g_norm2[0].reshape(1, d), w_mlp1[0].astype(BF16), w_mlp2[0].astype(BF16), tm=min(TAIL_TOKEN_TILE, s))
    return out.astype(out_dtype)
```

```python
import functools

import numpy as np
import jax
import jax.numpy as jnp
from jax import lax
from jax.experimental import pallas as pl
from jax.experimental.pallas import tpu as pltpu

F32 = jnp.float32
BF16 = jnp.bfloat16

D_MODEL = 1024
GRID_W = 64
HEAD_DIM = 64
ATTN_Q_HEADS = 8
ATTN_KV_HEADS = 2
ATTN_GROUP = ATTN_Q_HEADS // ATTN_KV_HEADS
WINDOW = 128
ATTN_BLOCK = 128
ROPE_BASE = 10000.0
DN_HEADS = 8
CONV_W = 3
CHUNK = 64
N_DIR = 2
D_FF = 4 * D_MODEL
EPS = 1e-6
NEG_INF = -1e30
LOG2_E = float(np.log2(np.e))

ATTN_WIDTH = ATTN_Q_HEADS * HEAD_DIM
KV_WIDTH = ATTN_KV_HEADS * HEAD_DIM
DN_WIDTH = DN_HEADS * HEAD_DIM
LANES = 128
MXU_TILE = 256
HEADS_PER_TILE = MXU_TILE // HEAD_DIM
DN_SUB = 16
VMEM_LIMIT = 56 * 1024 * 1024

INPROJ_TOKEN_TILE = 1024
TAIL_TOKEN_TILE = 512
ADA_COL_TILE = 1536
MLP_FF_CHUNK = 1024
ATTN_PREP_ROWS = 256
DN_CHUNKS_PER_STEP = 8
SCAN_TOKEN_TILE = 256
SCAN_BATCH_ROWS = 4

_OFF_AQ = 0
_OFF_AK = _OFF_AQ + ATTN_WIDTH
_OFF_AV = _OFF_AK + KV_WIDTH
_OFF_DQ = _OFF_AV + KV_WIDTH
_OFF_DK = _OFF_DQ + DN_WIDTH
_OFF_DV = _OFF_DK + DN_WIDTH
_OFF_DZ = _OFF_DV + DN_WIDTH
_OFF_DA = _OFF_DZ + DN_WIDTH
_OFF_DB = _OFF_DA + N_DIR * DN_HEADS
_OFF_GA = _OFF_DB + N_DIR * DN_HEADS
_OFF_GD = _OFF_GA + D_MODEL
_IN_WIDTH = _OFF_GD + D_MODEL


def _sigmoid(x):
    return 0.5 * jnp.tanh(0.5 * x) + 0.5


def _dot(a, b):
    return jnp.dot(a, b, preferred_element_type=F32)


def _dot_nt(a, b):
    return lax.dot_general(a, b, (((1,), (1,)), ((), ())), preferred_element_type=F32)


def _cparams(n_axes):
    return pltpu.CompilerParams(dimension_semantics=("arbitrary",) * n_axes, vmem_limit_bytes=VMEM_LIMIT)


def _ada_kernel(c_ref, w_ref, b_ref, o_ref):
    c = c_ref[...]
    s = c * _sigmoid(c)
    o_ref[...] = _dot(s.astype(BF16), w_ref[...].astype(BF16)) + b_ref[...]


def _ada(cc, w_ada, b_ada):
    rows, d = cc.shape
    n = w_ada.shape[1]
    tn = ADA_COL_TILE
    return pl.pallas_call(
        _ada_kernel,
        grid=(n // tn,),
        in_specs=[pl.BlockSpec((rows, d), lambda j: (0, 0)),
                  pl.BlockSpec((d, tn), lambda j: (0, j)),
                  pl.BlockSpec((1, tn), lambda j: (0, j))],
        out_specs=pl.BlockSpec((rows, tn), lambda j: (0, j)),
        out_shape=jax.ShapeDtypeStruct((rows, n), F32),
        compiler_params=_cparams(1),
        name="ada",
    )(cc, w_ada, b_ada.reshape(1, n))


def _inproj_kernel(x_ref, xprev_ref, xnext_ref, mod_ref, g_ref, wt_ref, wabt_ref, cw_ref, hsum_ref, *out_refs,
                   segs, tm, halo):
    w_refs = (wt_ref, wabt_ref)
    i = pl.program_id(1)
    last = pl.num_programs(1) - 1
    mod = mod_ref[0]
    scale = g_ref[...] * (1.0 + mod[1:2])

    def norm_mod(v):
        ms = jnp.mean(v * v, axis=-1, keepdims=True)
        return (v * lax.rsqrt(ms + EPS) * scale + mod[0:1]).astype(BF16)

    h = norm_mod(x_ref[0])
    h_ext = jnp.concatenate([h, norm_mod(jnp.concatenate([xprev_ref[0], xnext_ref[0]], axis=0))], axis=0)
    keep_prev = (i > 0).astype(F32)
    keep_next = (i < last).astype(F32)
    rows = lax.broadcasted_iota(jnp.int32, (tm, MXU_TILE), 0)
    hsum = hsum_ref[...]
    def plain_piece(o_ref, src, start, lo, width):
        o_ref[0, :, lo:lo + width] = _dot_nt(h, w_refs[src][start + lo:start + lo + width, :]).astype(o_ref.dtype)

    def conv_piece(o_ref, src, start, lo, kind):
        conv_col, do_norm, is_q = kind
        w_rows = w_refs[src][start + lo:start + lo + MXU_TILE, :]
        p_ext = _dot_nt(h_ext, w_rows)
        p, p_halo = p_ext[0:tm], p_ext[tm:tm + 2 * halo]
        yield
        p_prev = jnp.where(rows == 0, p_halo[halo - 1:halo] * keep_prev, pltpu.roll(p, 1, 0))
        p_next = jnp.where(rows == tm - 1, p_halo[halo:halo + 1] * keep_next, pltpu.roll(p, tm - 1, 0))
        cw = cw_ref[:, conv_col + lo:conv_col + lo + MXU_TILE]
        y = p_prev * cw[0:1] + p * cw[1:2] + p_next * cw[2:3]
        y = y * _sigmoid(y)
        if do_norm:
            y = y * lax.rsqrt(_dot((y * y).astype(BF16), hsum) + EPS)
            if is_q:
                y = y * (HEAD_DIM ** -0.5)
        o_ref[0, :, lo:lo + MXU_TILE] = y.astype(o_ref.dtype)

    plain, conv = [], []
    for o_ref, (src, start, size, kind) in zip(out_refs, segs):
        if kind is None:
            plain += [functools.partial(plain_piece, o_ref, src, start, lo, min(MXU_TILE, size - lo))
                      for lo in range(0, size, MXU_TILE)]
        else:
            conv += [functools.partial(conv_piece, o_ref, src, start, lo, kind) for lo in range(0, size, MXU_TILE)]
    pending = None
    while plain or conv or pending is not None:
        started = conv.pop(0)() if conv else None
        if started is not None:
            next(started)
        if pending is not None:
            for _ in pending:
                pass
        pending = started
        for _ in range(-(-len(plain) // (len(conv) + 1)) if plain else 0):
            plain.pop(0)()


def _inproj(x, mod3, mod_row, g_norm, w_t, wab_t, conv_w, hsum, segs, dtypes, tm):
    b, t, d = x.shape
    halo = 8
    r = tm // halo
    nblk = t // halo
    if mod_row is None:
        mod_map = lambda bi, i: (bi, 0, 0)
    else:
        mod_map = lambda bi, i: (mod_row, 0, 0)
    const2 = lambda bi, i: (0, 0)
    out_shape = [jax.ShapeDtypeStruct((b, t, size), dt) for (_, _, size, _), dt in zip(segs, dtypes)]
    out_specs = [pl.BlockSpec((1, tm, size), lambda bi, i: (bi, i, 0)) for (_, _, size, _) in segs]
    return pl.pallas_call(
        functools.partial(_inproj_kernel, segs=segs, tm=tm, halo=halo),
        grid=(b, t // tm),
        in_specs=[pl.BlockSpec((1, tm, d), lambda bi, i: (bi, i, 0)),
                  pl.BlockSpec((1, halo, d), lambda bi, i: (bi, jnp.maximum(i * r - 1, 0), 0)),
                  pl.BlockSpec((1, halo, d), lambda bi, i: (bi, jnp.minimum((i + 1) * r, nblk - 1), 0)),
                  pl.BlockSpec((1, 6, d), mod_map),
                  pl.BlockSpec((1, d), const2),
                  _resident(w_t.shape),
                  _resident(wab_t.shape),
                  _resident(conv_w.shape),
                  _resident(hsum.shape)],
        out_specs=out_specs,
        out_shape=out_shape,
        compiler_params=_cparams(2),
        name="inproj",
    )(x, x, x, mod3, g_norm.reshape(1, d), w_t, wab_t, conv_w, hsum)


def _rope(x, cos, sin, lane):
    swapped = jnp.where((lane % 32) < 16, pltpu.roll(x, LANES - 16, 1), pltpu.roll(x, 16, 1))
    return x * cos + swapped * sin


def _attn_kernel(a_ref, kvc_ref, cos_ref, sin_ref, qg_ref, kg_ref, sink_ref, havg_ref,
                 o_ref, k_s, vt_s, *, seq, ctx, qb):
    n = pl.program_id(1)
    havg = havg_ref[...]
    prep_rows = ATTN_PREP_ROWS
    ctx_blocks = ctx // ATTN_BLOCK

    @pl.when(n == 0)
    def _prep():
        kg = kg_ref[...]
        kc = kvc_ref[0, :, 0:KV_WIDTH].astype(F32)
        ms = _dot((kc * kc).astype(BF16), havg)
        k_s[0:ctx, :] = (kc * lax.rsqrt(ms + EPS) * kg).astype(BF16)
        vc = kvc_ref[0, :, KV_WIDTH:2 * KV_WIDTH].astype(F32)
        for j in range(ctx_blocks):
            vt_s[j] = vc[j * ATTN_BLOCK:(j + 1) * ATTN_BLOCK, :].T.astype(BF16)
        lane = lax.broadcasted_iota(jnp.int32, (prep_rows, LANES), 1)
        for r0 in range(0, seq, prep_rows):
            kx = a_ref[0, r0:r0 + prep_rows, _OFF_AK:_OFF_AK + KV_WIDTH].astype(F32)
            ms = _dot((kx * kx).astype(BF16), havg)
            kx = kx * lax.rsqrt(ms + EPS) * kg
            kx = _rope(kx, cos_ref[r0:r0 + prep_rows, :], sin_ref[r0:r0 + prep_rows, :], lane)
            k_s[ctx + r0:ctx + r0 + prep_rows, :] = kx.astype(BF16)
            vx = a_ref[0, r0:r0 + prep_rows, _OFF_AV:_OFF_AV + KV_WIDTH].astype(F32)
            for j in range(prep_rows // ATTN_BLOCK):
                vt_s[ctx_blocks + r0 // ATTN_BLOCK + j] = vx[j * ATTN_BLOCK:(j + 1) * ATTN_BLOCK, :].T.astype(BF16)

    lane = lax.broadcasted_iota(jnp.int32, (ATTN_BLOCK, LANES), 1)
    qg = qg_ref[...]
    n_band = 3
    n_win = n_band * ATTN_BLOCK
    n_keys = n_win + ctx
    key_row = lax.broadcasted_iota(jnp.int32, (n_win, ATTN_BLOCK), 0)
    q_lane = lax.broadcasted_iota(jnp.int32, (n_win, ATTN_BLOCK), 1)
    zeros = jnp.zeros((HEAD_DIM, ATTN_BLOCK), BF16)
    groups = range(ATTN_KV_HEADS)
    sink_rows = [jnp.concatenate([sink_ref[g * ATTN_GROUP + r:g * ATTN_GROUP + r + 1, :] for r in range(ATTN_GROUP)],
                                 axis=1) * LOG2_E for g in groups]

    units = []
    k_alls, vt_alls, valids, rhss = [], [], [], {}
    for sb in range(qb):
        nq = n * qb + sb
        q0 = pl.multiple_of(nq * ATTN_BLOCK, ATTN_BLOCK)
        cos = cos_ref[pl.ds(q0, ATTN_BLOCK), :]
        sin = sin_ref[pl.ds(q0, ATTN_BLOCK), :]
        qts = []
        for j in range(ATTN_WIDTH // LANES):
            qx = a_ref[0, pl.ds(q0, ATTN_BLOCK), j * LANES:(j + 1) * LANES].astype(F32)
            ms = _dot((qx * qx).astype(BF16), havg)
            qx = _rope(qx * lax.rsqrt(ms + EPS) * qg, cos, sin, lane) * (HEAD_DIM ** -0.5 * LOG2_E)
            qts.append(qx.T.astype(BF16))
        blk0 = jnp.clip(nq - 1, 0, seq // ATTN_BLOCK - n_band)
        start = pl.multiple_of(blk0 * ATTN_BLOCK, ATTN_BLOCK)
        valid = jnp.abs(q0 + q_lane - (start + key_row)) <= WINDOW
        valids.append(jnp.concatenate([valid] * ATTN_GROUP, axis=1))
        k_alls.append(jnp.concatenate([k_s[pl.ds(ctx + start, n_win), :], k_s[0:ctx, :]], axis=0))
        vt_loc = vt_s[pl.ds(ctx_blocks + blk0, n_band)]
        vt_alls.append(jnp.concatenate([vt_loc[j] for j in range(n_band)] + [vt_s[j] for j in range(ctx_blocks)],
                                       axis=1))
        for g in groups:
            cols = []
            for r in range(ATTN_GROUP):
                h = g * ATTN_GROUP + r
                piece = qts[h // 2][(h % 2) * HEAD_DIM:(h % 2 + 1) * HEAD_DIM, :]
                cols.append(jnp.concatenate([piece, zeros] if g == 0 else [zeros, piece], axis=0))
            rhss[(sb, g)] = jnp.concatenate(cols, axis=1)
            units.append((sb, g))
        yield

    ss = []
    for sb, g in units:
        s = _dot(k_alls[sb], rhss[(sb, g)])
        ss.append(jnp.concatenate([jnp.where(valids[sb], s[0:n_win], NEG_INF), s[n_win:n_keys]], axis=0))
        yield
    p_bfs, invs = [], []
    for s, (sb, g) in zip(ss, units):
        m = jnp.maximum(jnp.max(s, axis=0, keepdims=True), sink_rows[g])
        p = jnp.exp2(s - m)
        invs.append(1.0 / (jnp.sum(p, axis=0, keepdims=True) + jnp.exp2(sink_rows[g] - m)))
        p_bfs.append(p.astype(BF16))
        yield
    pieces = {sb: [] for sb in range(qb)}
    for (sb, g), p_bf, inv in zip(units, p_bfs, invs):
        vt_g = vt_alls[sb][g * HEAD_DIM:(g + 1) * HEAD_DIM, :]
        for pair in range(ATTN_GROUP // 2):
            lanes = slice(pair * MXU_TILE, (pair + 1) * MXU_TILE)
            ot = _dot(vt_g, p_bf[:, lanes]) * inv[:, lanes]
            pieces[sb] += [ot[:, 0:ATTN_BLOCK], ot[:, ATTN_BLOCK:2 * ATTN_BLOCK]]
        yield
    for sb in range(qb):
        outs = [jnp.concatenate([pieces[sb][2 * j], pieces[sb][2 * j + 1]], axis=0).T
                for j in range(ATTN_Q_HEADS // 2)]
        o_ref[0, sb * ATTN_BLOCK:(sb + 1) * ATTN_BLOCK, :] = jnp.concatenate(outs, axis=1).astype(o_ref.dtype)


def _run_parts(parts, b, name):
    steps = parts[0]["steps"]
    assert all(p["steps"] == steps for p in parts)
    counts = [(len(p["args"]), len(p["out_shape"]), len(p["scratch_shapes"])) for p in parts]
    n_in = sum(c[0] for c in counts)
    n_out = sum(c[1] for c in counts)

    def kern(*refs):
        i_pos, o_pos, s_pos = 0, n_in, n_in + n_out
        bodies = []
        for p, (ci, co, cs) in zip(parts, counts):
            bodies.append(p["kernel"](*refs[i_pos:i_pos + ci], *refs[o_pos:o_pos + co], *refs[s_pos:s_pos + cs]))
            i_pos, o_pos, s_pos = i_pos + ci, o_pos + co, s_pos + cs
        while bodies:
            for body in list(bodies):
                try:
                    next(body)
                except StopIteration:
                    bodies.remove(body)

    outs = pl.pallas_call(
        kern,
        grid=(b, steps),
        in_specs=[s for p in parts for s in p["in_specs"]],
        out_specs=[s for p in parts for s in p["out_specs"]],
        out_shape=[s for p in parts for s in p["out_shape"]],
        scratch_shapes=[s for p in parts for s in p["scratch_shapes"]],
        compiler_params=_cparams(2),
        name=name,
    )(*[a for p in parts for a in p["args"]])
    res, pos = [], 0
    for _, co, _ in counts:
        res.append(outs[pos:pos + co])
        pos += co
    return res


def _attn_parts(a_lat, kv_ctx, cos, sin, qg, kg, sink, havg, qb):
    b, s, wa = a_lat.shape
    ctx = kv_ctx.shape[1]
    nb = s // ATTN_BLOCK
    const2 = lambda bi, n: (0, 0)
    return dict(
        kernel=functools.partial(_attn_kernel, seq=s, ctx=ctx, qb=qb),
        steps=nb // qb,
        in_specs=[pl.BlockSpec((1, s, wa), lambda bi, n: (bi, 0, 0)),
                  pl.BlockSpec((1, ctx, 2 * KV_WIDTH), lambda bi, n: (bi, 0, 0)),
                  pl.BlockSpec((s, LANES), const2),
                  pl.BlockSpec((s, LANES), const2),
                  pl.BlockSpec((1, LANES), const2),
                  pl.BlockSpec((1, LANES), const2),
                  pl.BlockSpec((ATTN_Q_HEADS, LANES), const2),
                  pl.BlockSpec((LANES, LANES), const2)],
        out_specs=[pl.BlockSpec((1, qb * ATTN_BLOCK, ATTN_WIDTH), lambda bi, n: (bi, n, 0))],
        out_shape=[jax.ShapeDtypeStruct((b, s, ATTN_WIDTH), BF16)],
        scratch_shapes=[pltpu.VMEM((ctx + s, KV_WIDTH), BF16),
                        pltpu.VMEM(((ctx + s) // ATTN_BLOCK, KV_WIDTH, ATTN_BLOCK), BF16)],
        args=[a_lat, kv_ctx, cos, sin, qg, kg, sink, havg])


def _half_mask():
    return lax.broadcasted_iota(jnp.int32, (CHUNK, LANES), 1) < HEAD_DIM


def _bdiag_pair(x_pair, lo_half):
    zeros = jnp.zeros_like(x_pair)
    return jnp.concatenate([jnp.where(lo_half, x_pair, zeros), jnp.where(lo_half, zeros, x_pair)], axis=0)


def _bdiag(x_lane, lo_half):
    zeros = jnp.zeros((CHUNK, LANES), x_lane.dtype)
    blocks = []
    for h in range(HEADS_PER_TILE):
        col = (h * HEAD_DIM) // LANES
        piece = x_lane[:, col * LANES:(col + 1) * LANES]
        piece = jnp.where(lo_half if (h * HEAD_DIM) % LANES == 0 else jnp.logical_not(lo_half), piece, zeros)
        blocks.append(jnp.concatenate([piece if c == col else zeros for c in range(MXU_TILE // LANES)], axis=1))
    return jnp.concatenate(blocks, axis=0)


def _unit_tri_inverses(a_list, eye_l, sub_mask, lo_half):
    ads = [jnp.where(sub_mask, a, 0.0) for a in a_list]
    ys = [jnp.where(sub_mask, 0.0, a) for a in a_list]
    ps = [eye_l - ad for ad in ads]
    pws = [ad.astype(BF16) for ad in ads]
    n_levels = int(np.log2(DN_SUB))
    c1, c2, c3 = CHUNK, 2 * CHUNK, 3 * CHUNK
    for level in range(n_levels):
        rhss = [_bdiag(pw, lo_half) for pw in pws]
        if level == 0:
            boths = [_dot(jnp.concatenate([pw, y.astype(BF16)], axis=0), rhs) for pw, y, rhs in zip(pws, ys, rhss)]
            pws = [both[0:c1].astype(BF16) for both in boths]
            ys = [y - both[c1:c2] for y, both in zip(ys, boths)]
        elif level < n_levels - 1:
            boths = [_dot(jnp.concatenate([pw, p.astype(BF16), y.astype(BF16)], axis=0), rhs)
                     for pw, p, y, rhs in zip(pws, ps, ys, rhss)]
            pws = [both[0:c1].astype(BF16) for both in boths]
            ps = [p + both[c1:c2] for p, both in zip(ps, boths)]
            ys = [y + both[c2:c3] for y, both in zip(ys, boths)]
        else:
            boths = [_dot(jnp.concatenate([p.astype(BF16), y.astype(BF16)], axis=0), rhs)
                     for p, y, rhs in zip(ps, ys, rhss)]
            ps = [p + both[0:c1] for p, both in zip(ps, boths)]
            ys = [y + both[c1:c2] for y, both in zip(ys, boths)]
        yield
    assert CHUNK // DN_SUB == 4
    b_bfs = [y.astype(BF16) for y in ys]
    boths = [_dot(jnp.concatenate([b, p.astype(BF16)], axis=0), _bdiag(b, lo_half)) for b, p in zip(b_bfs, ps)]
    yield
    zs = [p - both[c1:c2] for p, both in zip(ps, boths)]
    return [z + _dot(z.astype(BF16), _bdiag(both[0:c1].astype(BF16), lo_half)) for z, both in zip(zs, boths)]


def _dn_factors(abs_, d, arow, dtrow, exp2_ref, tri_ref, eye_t):
    def hi_lo(v):
        hi = v.astype(BF16)
        return hi, (v - hi.astype(F32)).astype(BF16)

    lane = lax.broadcasted_iota(jnp.int32, abs_[0].shape, 1)
    is_g = (lane % 32) < 16
    xhls = []
    for ab in abs_:
        z = ab + dtrow
        softplus = jnp.maximum(z, 0.0) + jnp.log(1.0 + jnp.exp(-jnp.abs(z)))
        x = jnp.where(is_g, -arow * softplus, _sigmoid(ab))
        x = jnp.where(is_g, _dot(tri_ref[d], jnp.concatenate(hi_lo(x), axis=0)), x)
        x_hi, x_lo = hi_lo(x)
        xhls.append(jnp.where(lane < 32, x_hi, x_lo)[:, 0:64])
    y = _dot(jnp.concatenate(xhls, axis=0), exp2_ref[d])
    out = []
    for c in range(len(abs_)):
        gi = y[c * CHUNK:(c + 1) * CHUNK, 0:DN_WIDTH]
        be = y[c * CHUNK:(c + 1) * CHUNK, DN_WIDTH:2 * DN_WIDTH]
        gj = jnp.sum(gi * eye_t, axis=0, keepdims=True)
        out.append((be, gi, jnp.broadcast_to(gj, gi.shape)))
    return out


def _head_transpose(x_lane, lo_half):
    parts = []
    for pr in range(MXU_TILE // LANES):
        xt = _bdiag_pair(x_lane[:, pr * LANES:(pr + 1) * LANES].astype(F32), lo_half).T
        parts.append(xt[0:CHUNK] + xt[CHUNK:2 * CHUNK])
    return jnp.concatenate(parts, axis=1)


def _dnprep_kernel(*refs, cb, want_o):
    if want_o:
        q_ref, k_ref, v_ref, ab_ref = refs[:4]
        refs = refs[4:]
    else:
        q_ref = None
        k_ref, v_ref, ab_ref = refs[:3]
        refs = refs[3:]
    arow_ref, dtrow_ref, exp2_ref, tri_ref, w_o, uv_o, kt_o, dl_o = refs[:8]
    qd_o, in_o = refs[8:10] if want_o else (None, None)
    n_tiles = DN_WIDTH // MXU_TILE
    row = lax.broadcasted_iota(jnp.int32, (CHUNK, MXU_TILE), 0)
    colj = lax.broadcasted_iota(jnp.int32, (CHUNK, MXU_TILE), 1) % HEAD_DIM
    eye_l = (row == colj).astype(F32)
    eye_t = jnp.concatenate([eye_l] * n_tiles, axis=1)
    lo_half = _half_mask()
    arow = arow_ref[...]
    dtrow = dtrow_ref[...]

    fac = {}
    shared = {}
    abs_ = [ab_ref[0, c * CHUNK:(c + 1) * CHUNK, :] for c in range(cb)]
    for d in range(N_DIR):
        for c, f in enumerate(_dn_factors(abs_, d, arow, dtrow, exp2_ref, tri_ref, eye_t)):
            fac[(c, d)] = f
    yield
    for c in range(cb):
        rows = slice(c * CHUNK, (c + 1) * CHUNK)
        for g in range(n_tiles):
            lanes = slice(g * MXU_TILE, (g + 1) * MXU_TILE)
            k_l = k_ref[0, rows, lanes]
            kbd = _bdiag(k_l, lo_half)
            if want_o:
                kq = _dot_nt(jnp.concatenate([k_l, q_ref[0, rows, lanes]], axis=0), kbd)
                shared[(c, g)] = (kq[0:CHUNK], kq[CHUNK:2 * CHUNK])
            else:
                shared[(c, g)] = (_dot_nt(k_l, kbd), None)
        yield

    units = [(c, g, d) for c in range(cb) for g in range(n_tiles) for d in range(N_DIR)]
    decs = []
    for c, g, d in units:
        lanes = slice(g * MXU_TILE, (g + 1) * MXU_TILE)
        be, gi, gj = fac[(c, d)]
        lower = (row > colj) if d == 0 else (row < colj)
        decs.append(jnp.where(lower, jnp.exp(jnp.where(lower, gi[:, lanes] - gj[:, lanes], 0.0)), 0.0))
    a_list = [fac[(c, d)][0][:, g * MXU_TILE:(g + 1) * MXU_TILE] * dec * shared[(c, g)][0]
              for (c, g, d), dec in zip(units, decs)]
    yield
    tinvs = yield from _unit_tri_inverses(a_list, eye_l, (row // DN_SUB) == (colj // DN_SUB), lo_half)

    for idx, ((c, g, d), dec, tinv) in enumerate(zip(units, decs, tinvs)):
        if idx % (n_tiles * N_DIR) == 0:
            yield
        rows = slice(c * CHUNK, (c + 1) * CHUNK)
        lanes = slice(g * MXU_TILE, (g + 1) * MXU_TILE)
        be, gi, _ = fac[(c, d)]
        be, gi = be[:, lanes], gi[:, lanes]
        last = CHUNK - 1 if d == 0 else 0
        e_g = jnp.exp(gi)
        gl_row = gi[last:last + 1, :]
        kf = k_ref[0, rows, lanes].astype(F32)
        vf = v_ref[0, rows, lanes].astype(F32)
        ck = (be * e_g * kf).astype(BF16)
        bv = (be * vf).astype(BF16)
        tinv_bf = tinv.astype(BF16)
        w_parts, uv_parts = [], []
        for pr in range(MXU_TILE // LANES):
            pl_ = slice(pr * LANES, (pr + 1) * LANES)
            rhs = jnp.concatenate([_bdiag_pair(ck[:, pl_], lo_half), _bdiag_pair(bv[:, pl_], lo_half)], axis=1)
            wu = _dot(tinv_bf[:, pl_], rhs)
            w_parts.append(wu[:, 0:LANES])
            uv_parts.append(wu[:, LANES:2 * LANES])
        w_o[0, d, rows, lanes] = jnp.concatenate(w_parts, axis=1).astype(BF16)
        uv_o[0, d, rows, lanes] = jnp.concatenate(uv_parts, axis=1).astype(BF16)
        kt_o[0, d, rows, lanes] = (jnp.exp(gl_row - gi) * kf).astype(BF16)
        dl_o[0, d, c, :, lanes] = jnp.exp(gl_row)
        if want_o:
            qd_o[0, d, rows, lanes] = (e_g * q_ref[0, rows, lanes].astype(F32)).astype(BF16)
            in_o[0, d, rows, lanes] = ((dec + eye_l) * shared[(c, g)][1]).astype(BF16)


def _dnprep_parts(q, k, v, ab, arow, dtrow, exp2, tri, cb):
    want_o = q is not None
    b, t, w = k.shape
    tb = cb * CHUNK
    tok = lambda bi, i: (bi, i, 0)
    const2 = lambda bi, i: (0, 0)
    const3 = lambda bi, i: (0, 0, 0)
    dir_tok = lambda bi, i: (bi, 0, i, 0)
    data = ([q] if want_o else []) + [k, v]
    in_specs = ([pl.BlockSpec((1, tb, w), tok)] * len(data) + [pl.BlockSpec((1, tb, LANES), tok),
                pl.BlockSpec((1, LANES), const2), pl.BlockSpec((1, LANES), const2),
                pl.BlockSpec(exp2.shape, const3), pl.BlockSpec(tri.shape, const3)])
    big = lambda dt: jax.ShapeDtypeStruct((b, N_DIR, t, w), dt)
    big_spec = pl.BlockSpec((1, N_DIR, tb, w), dir_tok)
    out_shape = [big(BF16), big(BF16), big(BF16), jax.ShapeDtypeStruct((b, N_DIR, t // CHUNK, 1, w), F32)]
    out_specs = [big_spec, big_spec, big_spec, pl.BlockSpec((1, N_DIR, cb, 1, w), lambda bi, i: (bi, 0, i, 0, 0))]
    if want_o:
        out_shape += [big(BF16), big(BF16)]
        out_specs += [big_spec, big_spec]
    return dict(kernel=functools.partial(_dnprep_kernel, cb=cb, want_o=want_o), steps=t // tb, in_specs=in_specs,
                out_specs=out_specs, out_shape=out_shape, scratch_shapes=[],
                args=data + [ab, arow, dtrow, exp2, tri])


def _dnscan_kernel(*refs, n_chunk, bb, want_o, have_s0, want_s):
    n_in = 6 if want_o else 4
    dir_refs = [refs[0:n_in], refs[n_in:2 * n_in]]
    pos = 2 * n_in
    s0_ref = refs[pos] if have_s0 else None
    pos += int(have_s0)
    o_refs = refs[pos:pos + N_DIR] if want_o else None
    pos += N_DIR if want_o else 0
    sout_ref = refs[pos] if want_s else None
    pos += int(want_s)
    s_scr = refs[pos]
    n_tiles = DN_WIDTH // MXU_TILE
    i = pl.program_id(1)

    @pl.when(i == 0)
    def _init():
        if have_s0:
            s_scr[...] = s0_ref[...]
        else:
            s_scr[...] = jnp.zeros_like(s_scr)

    lo_half = _half_mask()
    chains = [(bi, d, g) for bi in range(bb) for d in range(N_DIR) for g in range(n_tiles)]

    def body(j, carry):
        cidx = (j, n_chunk - 1 - j)
        r0s = [pl.multiple_of(cidx[d] * CHUNK, CHUNK) for d in range(N_DIR)]
        s_olds, r1s = [], []
        for bi, d, g in chains:
            lanes = slice(g * MXU_TILE, (g + 1) * MXU_TILE)
            w = dir_refs[d][0][bi, 0, pl.ds(r0s[d], CHUNK), lanes]
            if want_o:
                w = jnp.concatenate([w, dir_refs[d][4][bi, 0, pl.ds(r0s[d], CHUNK), lanes]], axis=0)
            s_old = s_scr[bi, d * n_tiles + g]
            s_olds.append(s_old)
            r1s.append(_dot(w, _bdiag(s_old.astype(BF16), lo_half)))
        u_bfs = []
        for (bi, d, g), r1 in zip(chains, r1s):
            lanes = slice(g * MXU_TILE, (g + 1) * MXU_TILE)
            u_bfs.append((dir_refs[d][1][bi, 0, pl.ds(r0s[d], CHUNK), lanes] - r1[0:CHUNK]).astype(BF16))
        for (bi, d, g), r1, u_bf, s_old in zip(chains, r1s, u_bfs, s_olds):
            lanes = slice(g * MXU_TILE, (g + 1) * MXU_TILE)
            lhs = _head_transpose(dir_refs[d][2][bi, 0, pl.ds(r0s[d], CHUNK), lanes], lo_half).astype(BF16)
            if want_o:
                lhs = jnp.concatenate([dir_refs[d][5][bi, 0, pl.ds(r0s[d], CHUNK), lanes], lhs], axis=0)
            r2 = _dot(lhs, _bdiag(u_bf, lo_half))
            dl = dir_refs[d][3][bi, 0, cidx[d]][:, lanes]
            s_scr[bi, d * n_tiles + g] = s_old * dl + r2[r2.shape[0] - CHUNK:, :]
            if want_o:
                o_refs[d][bi, pl.ds(r0s[d], CHUNK), lanes] = (r1[CHUNK:2 * CHUNK] + r2[0:CHUNK]).astype(BF16)
        return carry

    lax.fori_loop(0, n_chunk, body, 0)

    if want_s:
        @pl.when(i == pl.num_programs(1) - 1)
        def _fin():
            sout_ref[...] = s_scr[...]


def _dnscan(prep, s0, tb, bb, want_s):
    want_o = len(prep) == 6
    b, _, t, w = prep[0].shape
    n_t = t // tb
    n_chunk = tb // CHUNK
    n_chain = N_DIR * (w // MXU_TILE)

    def specs(d):
        blk = (lambda bi, i: i) if d == 0 else (lambda bi, i: n_t - 1 - i)
        big = pl.BlockSpec((bb, 1, tb, w), lambda bi, i: (bi, d, blk(bi, i), 0))
        dl = pl.BlockSpec((bb, 1, n_chunk, 1, w), lambda bi, i: (bi, d, blk(bi, i), 0, 0))
        return [big, big, big, dl] + ([big, big] if want_o else [])

    in_specs = specs(0) + specs(1)
    args = list(prep) + list(prep)
    state_spec = pl.BlockSpec((bb, n_chain, CHUNK, MXU_TILE), lambda bi, i: (bi, 0, 0, 0))
    if s0 is not None:
        in_specs.append(state_spec)
        args.append(s0)
    out_shape, out_specs = [], []
    if want_o:
        out_shape += [jax.ShapeDtypeStruct((b, t, w), BF16)] * N_DIR
        out_specs += [pl.BlockSpec((bb, tb, w), lambda bi, i: (bi, i, 0)),
                      pl.BlockSpec((bb, tb, w), lambda bi, i: (bi, n_t - 1 - i, 0))]
    if want_s:
        out_shape.append(jax.ShapeDtypeStruct((b, n_chain, CHUNK, MXU_TILE), F32))
        out_specs.append(state_spec)
    return pl.pallas_call(
        functools.partial(_dnscan_kernel, n_chunk=n_chunk, bb=bb, want_o=want_o, have_s0=s0 is not None,
                          want_s=want_s),
        grid=(b // bb, n_t),
        in_specs=in_specs,
        out_specs=out_specs,
        out_shape=out_shape,
        scratch_shapes=[pltpu.VMEM((bb, n_chain, CHUNK, MXU_TILE), F32)],
        compiler_params=_cparams(2),
        name="dnscan",
    )(*args)


def _tail_kernel(x_ref, mod_ref, ya_ref, odf_ref, odb_ref, z_ref, gate_ref, dng_ref, havg_ref, wba_ref, wbd_ref,
                 wo_ref, gn2_ref, w1_ref, w2_ref, o_ref, *, ff_chunk):
    havg = havg_ref[...]
    dng = dng_ref[...]
    yd_parts = []
    for j in range(DN_WIDTH // MXU_TILE):
        sl = slice(j * MXU_TILE, (j + 1) * MXU_TILE)
        od = odf_ref[0, :, sl].astype(F32) + odb_ref[0, :, sl].astype(F32)
        ms = _dot((od * od).astype(BF16), havg)
        z = z_ref[0, :, sl].astype(F32)
        yd_parts.append((od * lax.rsqrt(ms + EPS) * dng * (z * _sigmoid(z))).astype(BF16))
    yd = jnp.concatenate(yd_parts, axis=1)
    ga = gate_ref[0, :, 0:D_MODEL].astype(F32)
    gd = gate_ref[0, :, D_MODEL:2 * D_MODEL].astype(F32)
    y = _sigmoid(ga) * _dot(ya_ref[0], wba_ref[...]) + _sigmoid(gd) * _dot(yd, wbd_ref[...])
    br = _dot(y.astype(BF16), wo_ref[...])
    mod = mod_ref[0]
    out1 = x_ref[0] + mod[2:3] * br
    ms2 = jnp.mean(out1 * out1, axis=-1, keepdims=True)
    hm = (out1 * lax.rsqrt(ms2 + EPS) * (gn2_ref[...] * (1.0 + mod[4:5])) + mod[3:4]).astype(BF16)
    acc = None
    for j in range(D_FF // ff_chunk):
        a = jnp.maximum(_dot(hm, w1_ref[:, j * ff_chunk:(j + 1) * ff_chunk]), 0.0)
        part = _dot((a * a).astype(BF16), w2_ref[j * ff_chunk:(j + 1) * ff_chunk, :])
        acc = part if acc is None else acc + part
    o_ref[0] = out1 + mod[5:6] * acc


def _resident(shape):
    return pl.BlockSpec(shape, lambda bi, i: (0,) * len(shape), pipeline_mode=pl.Buffered(1))


def _tail(x, mod3, y_attn, o_df, o_db, z, gates, dng, havg, wba, wbd, wo, gn2, w1, w2, tm):
    b, t, d = x.shape
    tok = lambda bi, i: (bi, i, 0)
    return pl.pallas_call(
        functools.partial(_tail_kernel, ff_chunk=MLP_FF_CHUNK),
        grid=(b, t // tm),
        in_specs=[pl.BlockSpec((1, tm, d), tok),
                  pl.BlockSpec((1, 6, d), lambda bi, i: (bi, 0, 0)),
                  pl.BlockSpec((1, tm, ATTN_WIDTH), tok),
                  pl.BlockSpec((1, tm, DN_WIDTH), tok),
                  pl.BlockSpec((1, tm, DN_WIDTH), tok),
                  pl.BlockSpec((1, tm, DN_WIDTH), tok),
                  pl.BlockSpec((1, tm, 2 * d), tok),
                  _resident((1, MXU_TILE)),
                  _resident((MXU_TILE, MXU_TILE)),
                  _resident(wba.shape),
                  _resident(wbd.shape),
                  _resident(wo.shape),
                  _resident((1, d)),
                  _resident(w1.shape),
                  _resident(w2.shape)],
        out_specs=pl.BlockSpec((1, tm, d), tok),
        out_shape=jax.ShapeDtypeStruct((b, t, d), F32),
        compiler_params=_cparams(2),
        name="tail",
    )(x, mod3, y_attn, o_df, o_db, z, gates, dng, havg, wba, wbd, wo, gn2, w1, w2)


def _head_avg(n, scale):
    idx = np.arange(n) // HEAD_DIM
    return jnp.asarray((idx[:, None] == idx[None, :]).astype(np.float32) * scale, BF16)


def _dn_expand_matrix():
    n = N_DIR * DN_HEADS
    m = np.zeros((N_DIR, 4 * n, 2 * DN_WIDTH), np.float32)
    for d in range(N_DIR):
        for part in range(2):
            for h in range(DN_HEADS):
                idx = d * DN_HEADS + h
                m[d, part * 2 * n + idx, h * HEAD_DIM:(h + 1) * HEAD_DIM] = 1.0
                m[d, part * 2 * n + n + idx, DN_WIDTH + h * HEAD_DIM:DN_WIDTH + (h + 1) * HEAD_DIM] = 1.0
    return jnp.asarray(m, BF16)


def _tri_matrices():
    i = np.arange(CHUNK)
    low = (i[:, None] >= i[None, :]).astype(np.float32)
    up = (i[:, None] <= i[None, :]).astype(np.float32)
    return jnp.asarray(np.stack([np.concatenate([low, low], axis=1), np.concatenate([up, up], axis=1)]), BF16)


def _rope_tables(seq):
    half = HEAD_DIM // 2
    n_freq = half // 2
    freqs = ROPE_BASE ** (-jnp.arange(n_freq, dtype=F32) / n_freq)
    pos = jnp.arange(seq)
    ang_r = (pos // GRID_W).astype(F32)[:, None] * freqs
    ang_c = (pos % GRID_W).astype(F32)[:, None] * freqs
    cos = jnp.concatenate([jnp.cos(ang_r)] * 2 + [jnp.cos(ang_c)] * 2, axis=1)
    sin = jnp.concatenate([-jnp.sin(ang_r), jnp.sin(ang_r), -jnp.sin(ang_c), jnp.sin(ang_c)], axis=1)
    reps = LANES // HEAD_DIM
    return jnp.tile(cos, (1, reps)), jnp.tile(sin, (1, reps))


def _pad_cols(w, n):
    return jnp.pad(w, ((0, 0), (0, n - w.shape[1])))


def kernel(x, c, ctx, c_ctx, w_ada, b_ada, g_norm1, w_in, q_norm_g, k_norm_g, attn_sink, conv_w, a_log, dt_bias,
           dn_norm_g, w_br_attn, w_br_dn, w_out, g_norm2, w_mlp1, w_mlp2):
    depth = w_ada.shape[0]
    assert depth == 1, "single-layer trunk only"
    b, s, d = x.shape
    n_ctx = ctx.shape[1]
    assert d == D_MODEL and w_in.shape[-1] == _IN_WIDTH
    assert s >= 3 * ATTN_BLOCK and s % ATTN_BLOCK == 0 and s % CHUNK == 0 and n_ctx % CHUNK == 0
    out_dtype = x.dtype
    w_in_t = jnp.swapaxes(w_in[0], 0, 1).astype(BF16)

    mod_rows = -(-(b + 1) // 16) * 16
    cc = jnp.concatenate([c.astype(F32), c_ctx.astype(F32)[None], jnp.zeros((mod_rows - b - 1, d), F32)], axis=0)
    mod = _ada(cc, w_ada[0], b_ada[0])
    mod3 = mod.reshape(mod_rows, 6, d)

    ab_rows = w_in_t[_OFF_DA:_OFF_GA]
    wab_t = jnp.concatenate([ab_rows, ab_rows, jnp.zeros((LANES - 2 * (_OFF_GA - _OFF_DA), d), BF16)], axis=0)
    hsum = _head_avg(MXU_TILE, 1.0)
    segs_lat = ((0, 0, _OFF_DQ, None),
                (0, _OFF_DQ, DN_WIDTH, (0, True, True)), (0, _OFF_DK, DN_WIDTH, (DN_WIDTH, True, False)),
                (0, _OFF_DV, DN_WIDTH, (2 * DN_WIDTH, False, False)),
                (0, _OFF_DZ, DN_WIDTH, None), (0, _OFF_GA, 2 * D_MODEL, None), (1, 0, LANES, None))
    a_lat, q_d, k_d, v_d, z_lat, gates, ab_lat = _inproj(x, mod3, None, g_norm1[0], w_in_t, wab_t, conv_w[0], hsum,
                                                         segs_lat, (BF16, BF16, BF16, BF16, BF16, BF16, F32),
                                                         tm=min(INPROJ_TOKEN_TILE, s))
    segs_ctx = ((0, _OFF_AK, 2 * KV_WIDTH, None),
                (0, _OFF_DK, DN_WIDTH, (0, True, False)), (0, _OFF_DV, DN_WIDTH, (DN_WIDTH, False, False)),
                (1, 0, LANES, None))
    kv_ctx, k_dc, v_dc, ab_ctx = _inproj(ctx, mod3, b, g_norm1[0], w_in_t, wab_t, conv_w[0][:, DN_WIDTH:], hsum,
                                         segs_ctx, (BF16, BF16, BF16, F32), tm=n_ctx)

    cos, sin = _rope_tables(s)
    reps = LANES // HEAD_DIM
    n_gate = N_DIR * DN_HEADS
    arow = _pad_cols(jnp.tile(jnp.concatenate([jnp.exp(a_log[0]).reshape(1, n_gate), jnp.zeros((1, n_gate), F32)],
                                              axis=1), (1, 2)), LANES)
    dtrow = _pad_cols(jnp.tile(jnp.concatenate([dt_bias[0].reshape(1, n_gate), jnp.zeros((1, n_gate), F32)],
                                               axis=1), (1, 2)), LANES)
    exp2, tri = _dn_expand_matrix(), _tri_matrices()
    dn_cb = DN_CHUNKS_PER_STEP
    qb = (s // ATTN_BLOCK) // (s // (dn_cb * CHUNK))
    attn_parts = _attn_parts(a_lat, kv_ctx, cos, sin,
                             jnp.tile(q_norm_g[0].astype(F32), reps)[None],
                             jnp.tile(k_norm_g[0].astype(F32), reps)[None],
                             jnp.broadcast_to(attn_sink[0].astype(F32)[:, None], (ATTN_Q_HEADS, LANES)),
                             _head_avg(LANES, 1.0 / HEAD_DIM), qb)
    (y_attn,), prep_lat = _run_parts(
        [attn_parts, _dnprep_parts(q_d, k_d, v_d, ab_lat, arow, dtrow, exp2, tri, dn_cb)], b, "attn_dnprep")
    (prep_ctx,) = _run_parts(
        [_dnprep_parts(None, k_dc, v_dc, ab_ctx, arow, dtrow, exp2, tri, min(dn_cb, n_ctx // CHUNK))], b, "dnprep")

    bb = max(q for q in range(1, SCAN_BATCH_ROWS + 1) if b % q == 0)
    (s_ctx,) = _dnscan(prep_ctx, None, tb=n_ctx, bb=bb, want_s=True)
    o_df, o_db = _dnscan(prep_lat, s_ctx, tb=SCAN_TOKEN_TILE, bb=bb, want_s=False)

    out = _tail(x, mod3, y_attn, o_df, o_db, z_lat, gates,
                jnp.tile(dn_norm_g[0].astype(F32), HEADS_PER_TILE)[None], _head_avg(MXU_TILE, 1.0 / HEAD_DIM),
                w_br_attn[0].astype(BF16), w_br_dn[0].astype(BF16), w_out[0].astype(BF16),
                g_norm2[0].reshape(1, d), w_mlp1[0].astype(BF16), w_mlp2[0].astype(BF16), tm=min(TAIL_TOKEN_TILE, s))
    return out.astype(out_dtype)
```

```python
import functools

import numpy as np
import jax
import jax.numpy as jnp
from jax import lax
from jax.experimental import pallas as pl
from jax.experimental.pallas import tpu as pltpu

F32 = jnp.float32
BF16 = jnp.bfloat16

D_MODEL = 1024
GRID_W = 64
HEAD_DIM = 64
ATTN_Q_HEADS = 8
ATTN_KV_HEADS = 2
ATTN_GROUP = ATTN_Q_HEADS // ATTN_KV_HEADS
WINDOW = 128
ATTN_BLOCK = 128
ROPE_BASE = 10000.0
DN_HEADS = 8
CONV_W = 3
CHUNK = 64
N_DIR = 2
D_FF = 4 * D_MODEL
EPS = 1e-6
NEG_INF = -1e30
LOG2_E = float(np.log2(np.e))

ATTN_WIDTH = ATTN_Q_HEADS * HEAD_DIM
KV_WIDTH = ATTN_KV_HEADS * HEAD_DIM
DN_WIDTH = DN_HEADS * HEAD_DIM
LANES = 128
MXU_TILE = 256
HEADS_PER_TILE = MXU_TILE // HEAD_DIM
DN_SUB = 16
VMEM_LIMIT = 56 * 1024 * 1024

INPROJ_TOKEN_TILE = 1024
TAIL_TOKEN_TILE = 512
ADA_COL_TILE = 1536
MLP_FF_CHUNK = 1024
ATTN_PREP_ROWS = 256
DN_CHUNKS_PER_STEP = 8
SCAN_TOKEN_TILE = 256
SCAN_BATCH_ROWS = 4

_OFF_AQ = 0
_OFF_AK = _OFF_AQ + ATTN_WIDTH
_OFF_AV = _OFF_AK + KV_WIDTH
_OFF_DQ = _OFF_AV + KV_WIDTH
_OFF_DK = _OFF_DQ + DN_WIDTH
_OFF_DV = _OFF_DK + DN_WIDTH
_OFF_DZ = _OFF_DV + DN_WIDTH
_OFF_DA = _OFF_DZ + DN_WIDTH
_OFF_DB = _OFF_DA + N_DIR * DN_HEADS
_OFF_GA = _OFF_DB + N_DIR * DN_HEADS
_OFF_GD = _OFF_GA + D_MODEL
_IN_WIDTH = _OFF_GD + D_MODEL


def _sigmoid(x):
    return 0.5 * jnp.tanh(0.5 * x) + 0.5


def _dot(a, b):
    return jnp.dot(a, b, preferred_element_type=F32)


def _dot_nt(a, b):
    return lax.dot_general(a, b, (((1,), (1,)), ((), ())), preferred_element_type=F32)


def _cparams(n_axes):
    return pltpu.CompilerParams(dimension_semantics=("arbitrary",) * n_axes, vmem_limit_bytes=VMEM_LIMIT)


def _ada_kernel(c_ref, w_ref, b_ref, o_ref):
    c = c_ref[...]
    s = c * _sigmoid(c)
    o_ref[...] = _dot(s.astype(BF16), w_ref[...].astype(BF16)) + b_ref[...]


def _ada(cc, w_ada, b_ada):
    rows, d = cc.shape
    n = w_ada.shape[1]
    tn = ADA_COL_TILE
    return pl.pallas_call(
        _ada_kernel,
        grid=(n // tn,),
        in_specs=[pl.BlockSpec((rows, d), lambda j: (0, 0)),
                  pl.BlockSpec((d, tn), lambda j: (0, j)),
                  pl.BlockSpec((1, tn), lambda j: (0, j))],
        out_specs=pl.BlockSpec((rows, tn), lambda j: (0, j)),
        out_shape=jax.ShapeDtypeStruct((rows, n), F32),
        compiler_params=_cparams(1),
        name="ada",
    )(cc, w_ada, b_ada.reshape(1, n))


def _inproj_kernel(x_ref, xprev_ref, xnext_ref, mod_ref, g_ref, wt_ref, wabt_ref, cw_ref, hsum_ref, *out_refs,
                   segs, tm, halo):
    w_refs = (wt_ref, wabt_ref)
    i = pl.program_id(1)
    last = pl.num_programs(1) - 1
    mod = mod_ref[0]
    scale = g_ref[...] * (1.0 + mod[1:2])

    def norm_mod(v):
        ms = jnp.mean(v * v, axis=-1, keepdims=True)
        return (v * lax.rsqrt(ms + EPS) * scale + mod[0:1]).astype(BF16)

    h = norm_mod(x_ref[0])
    h_ext = jnp.concatenate([h, norm_mod(jnp.concatenate([xprev_ref[0], xnext_ref[0]], axis=0))], axis=0)
    keep_prev = (i > 0).astype(F32)
    keep_next = (i < last).astype(F32)
    rows = lax.broadcasted_iota(jnp.int32, (tm, MXU_TILE), 0)
    hsum = hsum_ref[...]
    def plain_piece(o_ref, src, start, lo, width):
        o_ref[0, :, lo:lo + width] = _dot_nt(h, w_refs[src][start + lo:start + lo + width, :]).astype(o_ref.dtype)

    def conv_piece(o_ref, src, start, lo, kind):
        conv_col, do_norm, is_q = kind
        w_rows = w_refs[src][start + lo:start + lo + MXU_TILE, :]
        p_ext = _dot_nt(h_ext, w_rows)
        p, p_halo = p_ext[0:tm], p_ext[tm:tm + 2 * halo]
        yield
        p_prev = jnp.where(rows == 0, p_halo[halo - 1:halo] * keep_prev, pltpu.roll(p, 1, 0))
        p_next = jnp.where(rows == tm - 1, p_halo[halo:halo + 1] * keep_next, pltpu.roll(p, tm - 1, 0))
        cw = cw_ref[:, conv_col + lo:conv_col + lo + MXU_TILE]
        y = p_prev * cw[0:1] + p * cw[1:2] + p_next * cw[2:3]
        y = y * _sigmoid(y)
        if do_norm:
            y = y * lax.rsqrt(_dot((y * y).astype(BF16), hsum) + EPS)
            if is_q:
                y = y * (HEAD_DIM ** -0.5)
        o_ref[0, :, lo:lo + MXU_TILE] = y.astype(o_ref.dtype)

    plain, conv = [], []
    for o_ref, (src, start, size, kind) in zip(out_refs, segs):
        if kind is None:
            plain += [functools.partial(plain_piece, o_ref, src, start, lo, min(MXU_TILE, size - lo))
                      for lo in range(0, size, MXU_TILE)]
        else:
            conv += [functools.partial(conv_piece, o_ref, src, start, lo, kind) for lo in range(0, size, MXU_TILE)]
    pending = None
    while plain or conv or pending is not None:
        started = conv.pop(0)() if conv else None
        if started is not None:
            next(started)
        if pending is not None:
            for _ in pending:
                pass
        pending = started
        for _ in range(-(-len(plain) // (len(conv) + 1)) if plain else 0):
            plain.pop(0)()


def _inproj(x, mod3, mod_row, g_norm, w_t, wab_t, conv_w, hsum, segs, dtypes, tm):
    b, t, d = x.shape
    halo = 8
    r = tm // halo
    nblk = t // halo
    if mod_row is None:
        mod_map = lambda bi, i: (bi, 0, 0)
    else:
        mod_map = lambda bi, i: (mod_row, 0, 0)
    const2 = lambda bi, i: (0, 0)
    out_shape = [jax.ShapeDtypeStruct((b, t, size), dt) for (_, _, size, _), dt in zip(segs, dtypes)]
    out_specs = [pl.BlockSpec((1, tm, size), lambda bi, i: (bi, i, 0)) for (_, _, size, _) in segs]
    return pl.pallas_call(
        functools.partial(_inproj_kernel, segs=segs, tm=tm, halo=halo),
        grid=(b, t // tm),
        in_specs=[pl.BlockSpec((1, tm, d), lambda bi, i: (bi, i, 0)),
                  pl.BlockSpec((1, halo, d), lambda bi, i: (bi, jnp.maximum(i * r - 1, 0), 0)),
                  pl.BlockSpec((1, halo, d), lambda bi, i: (bi, jnp.minimum((i + 1) * r, nblk - 1), 0)),
                  pl.BlockSpec((1, 6, d), mod_map),
                  pl.BlockSpec((1, d), const2),
                  _resident(w_t.shape),
                  _resident(wab_t.shape),
                  _resident(conv_w.shape),
                  _resident(hsum.shape)],
        out_specs=out_specs,
        out_shape=out_shape,
        compiler_params=_cparams(2),
        name="inproj",
    )(x, x, x, mod3, g_norm.reshape(1, d), w_t, wab_t, conv_w, hsum)


def _rope(x, cos, sin, lane):
    swapped = jnp.where((lane % 32) < 16, pltpu.roll(x, LANES - 16, 1), pltpu.roll(x, 16, 1))
    return x * cos + swapped * sin


def _attn_kernel(a_ref, kvc_ref, cos_ref, sin_ref, qg_ref, kg_ref, sink_ref, havg_ref,
                 o_ref, k_s, vt_s, *, seq, ctx, qb):
    n = pl.program_id(1)
    havg = havg_ref[...]
    prep_rows = ATTN_PREP_ROWS
    ctx_blocks = ctx // ATTN_BLOCK

    @pl.when(n == 0)
    def _prep():
        kg = kg_ref[...]
        kc = kvc_ref[0, :, 0:KV_WIDTH].astype(F32)
        ms = _dot((kc * kc).astype(BF16), havg)
        k_s[0:ctx, :] = (kc * lax.rsqrt(ms + EPS) * kg).astype(BF16)
        vc = kvc_ref[0, :, KV_WIDTH:2 * KV_WIDTH].astype(F32)
        for j in range(ctx_blocks):
            vt_s[j] = vc[j * ATTN_BLOCK:(j + 1) * ATTN_BLOCK, :].T.astype(BF16)
        lane = lax.broadcasted_iota(jnp.int32, (prep_rows, LANES), 1)
        for r0 in range(0, seq, prep_rows):
            kx = a_ref[0, r0:r0 + prep_rows, _OFF_AK:_OFF_AK + KV_WIDTH].astype(F32)
            ms = _dot((kx * kx).astype(BF16), havg)
            kx = kx * lax.rsqrt(ms + EPS) * kg
            kx = _rope(kx, cos_ref[r0:r0 + prep_rows, :], sin_ref[r0:r0 + prep_rows, :], lane)
            k_s[ctx + r0:ctx + r0 + prep_rows, :] = kx.astype(BF16)
            vx = a_ref[0, r0:r0 + prep_rows, _OFF_AV:_OFF_AV + KV_WIDTH].astype(F32)
            for j in range(prep_rows // ATTN_BLOCK):
                vt_s[ctx_blocks + r0 // ATTN_BLOCK + j] = vx[j * ATTN_BLOCK:(j + 1) * ATTN_BLOCK, :].T.astype(BF16)

    lane = lax.broadcasted_iota(jnp.int32, (ATTN_BLOCK, LANES), 1)
    qg = qg_ref[...]
    n_band = 3
    n_win = n_band * ATTN_BLOCK
    n_keys = n_win + ctx
    key_row = lax.broadcasted_iota(jnp.int32, (n_win, ATTN_BLOCK), 0)
    q_lane = lax.broadcasted_iota(jnp.int32, (n_win, ATTN_BLOCK), 1)
    zeros = jnp.zeros((HEAD_DIM, ATTN_BLOCK), BF16)
    groups = range(ATTN_KV_HEADS)
    sink_rows = [jnp.concatenate([sink_ref[g * ATTN_GROUP + r:g * ATTN_GROUP + r + 1, :] for r in range(ATTN_GROUP)],
                                 axis=1) * LOG2_E for g in groups]

    units = []
    k_alls, vt_alls, valids, rhss = [], [], [], {}
    for sb in range(qb):
        nq = n * qb + sb
        q0 = pl.multiple_of(nq * ATTN_BLOCK, ATTN_BLOCK)
        cos = cos_ref[pl.ds(q0, ATTN_BLOCK), :]
        sin = sin_ref[pl.ds(q0, ATTN_BLOCK), :]
        qts = []
        for j in range(ATTN_WIDTH // LANES):
            qx = a_ref[0, pl.ds(q0, ATTN_BLOCK), j * LANES:(j + 1) * LANES].astype(F32)
            ms = _dot((qx * qx).astype(BF16), havg)
            qx = _rope(qx * lax.rsqrt(ms + EPS) * qg, cos, sin, lane) * (HEAD_DIM ** -0.5 * LOG2_E)
            qts.append(qx.T.astype(BF16))
        blk0 = jnp.clip(nq - 1, 0, seq // ATTN_BLOCK - n_band)
        start = pl.multiple_of(blk0 * ATTN_BLOCK, ATTN_BLOCK)
        valid = jnp.abs(q0 + q_lane - (start + key_row)) <= WINDOW
        valids.append(jnp.concatenate([valid] * ATTN_GROUP, axis=1))
        k_alls.append(jnp.concatenate([k_s[pl.ds(ctx + start, n_win), :], k_s[0:ctx, :]], axis=0))
        vt_loc = vt_s[pl.ds(ctx_blocks + blk0, n_band)]
        vt_alls.append(jnp.concatenate([vt_loc[j] for j in range(n_band)] + [vt_s[j] for j in range(ctx_blocks)],
                                       axis=1))
        for g in groups:
            cols = []
            for r in range(ATTN_GROUP):
                h = g * ATTN_GROUP + r
                piece = qts[h // 2][(h % 2) * HEAD_DIM:(h % 2 + 1) * HEAD_DIM, :]
                cols.append(jnp.concatenate([piece, zeros] if g == 0 else [zeros, piece], axis=0))
            rhss[(sb, g)] = jnp.concatenate(cols, axis=1)
            units.append((sb, g))
        yield

    ss = []
    for sb, g in units:
        s = _dot(k_alls[sb], rhss[(sb, g)])
        ss.append(jnp.concatenate([jnp.where(valids[sb], s[0:n_win], NEG_INF), s[n_win:n_keys]], axis=0))
        yield
    p_bfs, invs = [], []
    for s, (sb, g) in zip(ss, units):
        m = jnp.maximum(jnp.max(s, axis=0, keepdims=True), sink_rows[g])
        p = jnp.exp2(s - m)
        invs.append(1.0 / (jnp.sum(p, axis=0, keepdims=True) + jnp.exp2(sink_rows[g] - m)))
        p_bfs.append(p.astype(BF16))
        yield
    pieces = {sb: [] for sb in range(qb)}
    for (sb, g), p_bf, inv in zip(units, p_bfs, invs):
        vt_g = vt_alls[sb][g * HEAD_DIM:(g + 1) * HEAD_DIM, :]
        for pair in range(ATTN_GROUP // 2):
            lanes = slice(pair * MXU_TILE, (pair + 1) * MXU_TILE)
            ot = _dot(vt_g, p_bf[:, lanes]) * inv[:, lanes]
            pieces[sb] += [ot[:, 0:ATTN_BLOCK], ot[:, ATTN_BLOCK:2 * ATTN_BLOCK]]
        yield
    for sb in range(qb):
        outs = [jnp.concatenate([pieces[sb][2 * j], pieces[sb][2 * j + 1]], axis=0).T
                for j in range(ATTN_Q_HEADS // 2)]
        o_ref[0, sb * ATTN_BLOCK:(sb + 1) * ATTN_BLOCK, :] = jnp.concatenate(outs, axis=1).astype(o_ref.dtype)


def _run_parts(parts, b, name):
    steps = parts[0]["steps"]
    assert all(p["steps"] == steps for p in parts)
    counts = [(len(p["args"]), len(p["out_shape"]), len(p["scratch_shapes"])) for p in parts]
    n_in = sum(c[0] for c in counts)
    n_out = sum(c[1] for c in counts)

    def kern(*refs):
        i_pos, o_pos, s_pos = 0, n_in, n_in + n_out
        bodies = []
        for p, (ci, co, cs) in zip(parts, counts):
            bodies.append(p["kernel"](*refs[i_pos:i_pos + ci], *refs[o_pos:o_pos + co], *refs[s_pos:s_pos + cs]))
            i_pos, o_pos, s_pos = i_pos + ci, o_pos + co, s_pos + cs
        while bodies:
            for body in list(bodies):
                try:
                    next(body)
                except StopIteration:
                    bodies.remove(body)

    outs = pl.pallas_call(
        kern,
        grid=(b, steps),
        in_specs=[s for p in parts for s in p["in_specs"]],
        out_specs=[s for p in parts for s in p["out_specs"]],
        out_shape=[s for p in parts for s in p["out_shape"]],
        scratch_shapes=[s for p in parts for s in p["scratch_shapes"]],
        compiler_params=_cparams(2),
        name=name,
    )(*[a for p in parts for a in p["args"]])
    res, pos = [], 0
    for _, co, _ in counts:
        res.append(outs[pos:pos + co])
        pos += co
    return res


def _attn_parts(a_lat, kv_ctx, cos, sin, qg, kg, sink, havg, qb):
    b, s, wa = a_lat.shape
    ctx = kv_ctx.shape[1]
    nb = s // ATTN_BLOCK
    const2 = lambda bi, n: (0, 0)
    return dict(
        kernel=functools.partial(_attn_kernel, seq=s, ctx=ctx, qb=qb),
        steps=nb // qb,
        in_specs=[pl.BlockSpec((1, s, wa), lambda bi, n: (bi, 0, 0)),
                  pl.BlockSpec((1, ctx, 2 * KV_WIDTH), lambda bi, n: (bi, 0, 0)),
                  pl.BlockSpec((s, LANES), const2),
                  pl.BlockSpec((s, LANES), const2),
                  pl.BlockSpec((1, LANES), const2),
                  pl.BlockSpec((1, LANES), const2),
                  pl.BlockSpec((ATTN_Q_HEADS, LANES), const2),
                  pl.BlockSpec((LANES, LANES), const2)],
        out_specs=[pl.BlockSpec((1, qb * ATTN_BLOCK, ATTN_WIDTH), lambda bi, n: (bi, n, 0))],
        out_shape=[jax.ShapeDtypeStruct((b, s, ATTN_WIDTH), BF16)],
        scratch_shapes=[pltpu.VMEM((ctx + s, KV_WIDTH), BF16),
                        pltpu.VMEM(((ctx + s) // ATTN_BLOCK, KV_WIDTH, ATTN_BLOCK), BF16)],
        args=[a_lat, kv_ctx, cos, sin, qg, kg, sink, havg])


def _half_mask():
    return lax.broadcasted_iota(jnp.int32, (CHUNK, LANES), 1) < HEAD_DIM


def _bdiag_pair(x_pair, lo_half):
    zeros = jnp.zeros_like(x_pair)
    return jnp.concatenate([jnp.where(lo_half, x_pair, zeros), jnp.where(lo_half, zeros, x_pair)], axis=0)


def _bdiag(x_lane, lo_half):
    zeros = jnp.zeros((CHUNK, LANES), x_lane.dtype)
    blocks = []
    for h in range(HEADS_PER_TILE):
        col = (h * HEAD_DIM) // LANES
        piece = x_lane[:, col * LANES:(col + 1) * LANES]
        piece = jnp.where(lo_half if (h * HEAD_DIM) % LANES == 0 else jnp.logical_not(lo_half), piece, zeros)
        blocks.append(jnp.concatenate([piece if c == col else zeros for c in range(MXU_TILE // LANES)], axis=1))
    return jnp.concatenate(blocks, axis=0)


def _unit_tri_inverses(a_list, eye_l, sub_mask, lo_half):
    ads = [jnp.where(sub_mask, a, 0.0) for a in a_list]
    ys = [jnp.where(sub_mask, 0.0, a) for a in a_list]
    ps = [eye_l - ad for ad in ads]
    pws = [ad.astype(BF16) for ad in ads]
    n_levels = int(np.log2(DN_SUB))
    c1, c2, c3 = CHUNK, 2 * CHUNK, 3 * CHUNK
    for level in range(n_levels):
        rhss = [_bdiag(pw, lo_half) for pw in pws]
        if level == 0:
            boths = [_dot(jnp.concatenate([pw, y.astype(BF16)], axis=0), rhs) for pw, y, rhs in zip(pws, ys, rhss)]
            pws = [both[0:c1].astype(BF16) for both in boths]
            ys = [y - both[c1:c2] for y, both in zip(ys, boths)]
        elif level < n_levels - 1:
            boths = [_dot(jnp.concatenate([pw, p.astype(BF16), y.astype(BF16)], axis=0), rhs)
                     for pw, p, y, rhs in zip(pws, ps, ys, rhss)]
            pws = [both[0:c1].astype(BF16) for both in boths]
            ps = [p + both[c1:c2] for p, both in zip(ps, boths)]
            ys = [y + both[c2:c3] for y, both in zip(ys, boths)]
        else:
            boths = [_dot(jnp.concatenate([p.astype(BF16), y.astype(BF16)], axis=0), rhs)
                     for p, y, rhs in zip(ps, ys, rhss)]
            ps = [p + both[0:c1] for p, both in zip(ps, boths)]
            ys = [y + both[c1:c2] for y, both in zip(ys, boths)]
        yield
    assert CHUNK // DN_SUB == 4
    b_bfs = [y.astype(BF16) for y in ys]
    boths = [_dot(jnp.concatenate([b, p.astype(BF16)], axis=0), _bdiag(b, lo_half)) for b, p in zip(b_bfs, ps)]
    yield
    zs = [p - both[c1:c2] for p, both in zip(ps, boths)]
    return [z + _dot(z.astype(BF16), _bdiag(both[0:c1].astype(BF16), lo_half)) for z, both in zip(zs, boths)]


def _dn_factors(abs_, d, arow, dtrow, exp2_ref, tri_ref, eye_t):
    def hi_lo(v):
        hi = v.astype(BF16)
        return hi, (v - hi.astype(F32)).astype(BF16)

    lane = lax.broadcasted_iota(jnp.int32, abs_[0].shape, 1)
    is_g = (lane % 32) < 16
    xhls = []
    for ab in abs_:
        z = ab + dtrow
        softplus = jnp.maximum(z, 0.0) + jnp.log(1.0 + jnp.exp(-jnp.abs(z)))
        x = jnp.where(is_g, -arow * softplus, _sigmoid(ab))
        x = jnp.where(is_g, _dot(tri_ref[d], jnp.concatenate(hi_lo(x), axis=0)), x)
        x_hi, x_lo = hi_lo(x)
        xhls.append(jnp.where(lane < 32, x_hi, x_lo)[:, 0:64])
    y = _dot(jnp.concatenate(xhls, axis=0), exp2_ref[d])
    out = []
    for c in range(len(abs_)):
        gi = y[c * CHUNK:(c + 1) * CHUNK, 0:DN_WIDTH]
        be = y[c * CHUNK:(c + 1) * CHUNK, DN_WIDTH:2 * DN_WIDTH]
        gj = jnp.sum(gi * eye_t, axis=0, keepdims=True)
        out.append((be, gi, jnp.broadcast_to(gj, gi.shape)))
    return out


def _head_transpose(x_lane, lo_half):
    parts = []
    for pr in range(MXU_TILE // LANES):
        xt = _bdiag_pair(x_lane[:, pr * LANES:(pr + 1) * LANES].astype(F32), lo_half).T
        parts.append(xt[0:CHUNK] + xt[CHUNK:2 * CHUNK])
    return jnp.concatenate(parts, axis=1)


def _dnprep_kernel(*refs, cb, want_o):
    if want_o:
        q_ref, k_ref, v_ref, ab_ref = refs[:4]
        refs = refs[4:]
    else:
        q_ref = None
        k_ref, v_ref, ab_ref = refs[:3]
        refs = refs[3:]
    arow_ref, dtrow_ref, exp2_ref, tri_ref, w_o, uv_o, kt_o, dl_o = refs[:8]
    qd_o, in_o = refs[8:10] if want_o else (None, None)
    n_tiles = DN_WIDTH // MXU_TILE
    row = lax.broadcasted_iota(jnp.int32, (CHUNK, MXU_TILE), 0)
    colj = lax.broadcasted_iota(jnp.int32, (CHUNK, MXU_TILE), 1) % HEAD_DIM
    eye_l = (row == colj).astype(F32)
    eye_t = jnp.concatenate([eye_l] * n_tiles, axis=1)
    lo_half = _half_mask()
    arow = arow_ref[...]
    dtrow = dtrow_ref[...]

    fac = {}
    shared = {}
    abs_ = [ab_ref[0, c * CHUNK:(c + 1) * CHUNK, :] for c in range(cb)]
    for d in range(N_DIR):
        for c, f in enumerate(_dn_factors(abs_, d, arow, dtrow, exp2_ref, tri_ref, eye_t)):
            fac[(c, d)] = f
    yield
    for c in range(cb):
        rows = slice(c * CHUNK, (c + 1) * CHUNK)
        for g in range(n_tiles):
            lanes = slice(g * MXU_TILE, (g + 1) * MXU_TILE)
            k_l = k_ref[0, rows, lanes]
            kbd = _bdiag(k_l, lo_half)
            if want_o:
                kq = _dot_nt(jnp.concatenate([k_l, q_ref[0, rows, lanes]], axis=0), kbd)
                shared[(c, g)] = (kq[0:CHUNK], kq[CHUNK:2 * CHUNK])
            else:
                shared[(c, g)] = (_dot_nt(k_l, kbd), None)
        yield

    units = [(c, g, d) for c in range(cb) for g in range(n_tiles) for d in range(N_DIR)]
    decs = []
    for c, g, d in units:
        lanes = slice(g * MXU_TILE, (g + 1) * MXU_TILE)
        be, gi, gj = fac[(c, d)]
        lower = (row > colj) if d == 0 else (row < colj)
        decs.append(jnp.where(lower, jnp.exp(jnp.where(lower, gi[:, lanes] - gj[:, lanes], 0.0)), 0.0))
    a_list = [fac[(c, d)][0][:, g * MXU_TILE:(g + 1) * MXU_TILE] * dec * shared[(c, g)][0]
              for (c, g, d), dec in zip(units, decs)]
    yield
    tinvs = yield from _unit_tri_inverses(a_list, eye_l, (row // DN_SUB) == (colj // DN_SUB), lo_half)

    for idx, ((c, g, d), dec, tinv) in enumerate(zip(units, decs, tinvs)):
        if idx % (n_tiles * N_DIR) == 0:
            yield
        rows = slice(c * CHUNK, (c + 1) * CHUNK)
        lanes = slice(g * MXU_TILE, (g + 1) * MXU_TILE)
        be, gi, _ = fac[(c, d)]
        be, gi = be[:, lanes], gi[:, lanes]
        last = CHUNK - 1 if d == 0 else 0
        e_g = jnp.exp(gi)
        gl_row = gi[last:last + 1, :]
        kf = k_ref[0, rows, lanes].astype(F32)
        vf = v_ref[0, rows, lanes].astype(F32)
        ck = (be * e_g * kf).astype(BF16)
        bv = (be * vf).astype(BF16)
        tinv_bf = tinv.astype(BF16)
        w_parts, uv_parts = [], []
        for pr in range(MXU_TILE // LANES):
            pl_ = slice(pr * LANES, (pr + 1) * LANES)
            rhs = jnp.concatenate([_bdiag_pair(ck[:, pl_], lo_half), _bdiag_pair(bv[:, pl_], lo_half)], axis=1)
            wu = _dot(tinv_bf[:, pl_], rhs)
            w_parts.append(wu[:, 0:LANES])
            uv_parts.append(wu[:, LANES:2 * LANES])
        w_o[0, d, rows, lanes] = jnp.concatenate(w_parts, axis=1).astype(BF16)
        uv_o[0, d, rows, lanes] = jnp.concatenate(uv_parts, axis=1).astype(BF16)
        kt_o[0, d, rows, lanes] = (jnp.exp(gl_row - gi) * kf).astype(BF16)
        dl_o[0, d, c, :, lanes] = jnp.exp(gl_row)
        if want_o:
            qd_o[0, d, rows, lanes] = (e_g * q_ref[0, rows, lanes].astype(F32)).astype(BF16)
            in_o[0, d, rows, lanes] = ((dec + eye_l) * shared[(c, g)][1]).astype(BF16)


def _dnprep_parts(q, k, v, ab, arow, dtrow, exp2, tri, cb):
    want_o = q is not None
    b, t, w = k.shape
    tb = cb * CHUNK
    tok = lambda bi, i: (bi, i, 0)
    const2 = lambda bi, i: (0, 0)
    const3 = lambda bi, i: (0, 0, 0)
    dir_tok = lambda bi, i: (bi, 0, i, 0)
    data = ([q] if want_o else []) + [k, v]
    in_specs = ([pl.BlockSpec((1, tb, w), tok)] * len(data) + [pl.BlockSpec((1, tb, LANES), tok),
                pl.BlockSpec((1, LANES), const2), pl.BlockSpec((1, LANES), const2),
                pl.BlockSpec(exp2.shape, const3), pl.BlockSpec(tri.shape, const3)])
    big = lambda dt: jax.ShapeDtypeStruct((b, N_DIR, t, w), dt)
    big_spec = pl.BlockSpec((1, N_DIR, tb, w), dir_tok)
    out_shape = [big(BF16), big(BF16), big(BF16), jax.ShapeDtypeStruct((b, N_DIR, t // CHUNK, 1, w), F32)]
    out_specs = [big_spec, big_spec, big_spec, pl.BlockSpec((1, N_DIR, cb, 1, w), lambda bi, i: (bi, 0, i, 0, 0))]
    if want_o:
        out_shape += [big(BF16), big(BF16)]
        out_specs += [big_spec, big_spec]
    return dict(kernel=functools.partial(_dnprep_kernel, cb=cb, want_o=want_o), steps=t // tb, in_specs=in_specs,
                out_specs=out_specs, out_shape=out_shape, scratch_shapes=[],
                args=data + [ab, arow, dtrow, exp2, tri])


def _dnscan_kernel(*refs, n_chunk, bb, want_o, have_s0, want_s):
    n_in = 6 if want_o else 4
    dir_refs = [refs[0:n_in], refs[n_in:2 * n_in]]
    pos = 2 * n_in
    s0_ref = refs[pos] if have_s0 else None
    pos += int(have_s0)
    o_refs = refs[pos:pos + N_DIR] if want_o else None
    pos += N_DIR if want_o else 0
    sout_ref = refs[pos] if want_s else None
    pos += int(want_s)
    s_scr = refs[pos]
    n_tiles = DN_WIDTH // MXU_TILE
    i = pl.program_id(1)

    @pl.when(i == 0)
    def _init():
        if have_s0:
            s_scr[...] = s0_ref[...]
        else:
            s_scr[...] = jnp.zeros_like(s_scr)

    lo_half = _half_mask()
    chains = [(bi, d, g) for bi in range(bb) for d in range(N_DIR) for g in range(n_tiles)]

    def body(j, carry):
        cidx = (j, n_chunk - 1 - j)
        r0s = [pl.multiple_of(cidx[d] * CHUNK, CHUNK) for d in range(N_DIR)]
        s_olds, r1s = [], []
        for bi, d, g in chains:
            lanes = slice(g * MXU_TILE, (g + 1) * MXU_TILE)
            w = dir_refs[d][0][bi, 0, pl.ds(r0s[d], CHUNK), lanes]
            if want_o:
                w = jnp.concatenate([w, dir_refs[d][4][bi, 0, pl.ds(r0s[d], CHUNK), lanes]], axis=0)
            s_old = s_scr[bi, d * n_tiles + g]
            s_olds.append(s_old)
            r1s.append(_dot(w, _bdiag(s_old.astype(BF16), lo_half)))
        u_bfs = []
        for (bi, d, g), r1 in zip(chains, r1s):
            lanes = slice(g * MXU_TILE, (g + 1) * MXU_TILE)
            u_bfs.append((dir_refs[d][1][bi, 0, pl.ds(r0s[d], CHUNK), lanes] - r1[0:CHUNK]).astype(BF16))
        for (bi, d, g), r1, u_bf, s_old in zip(chains, r1s, u_bfs, s_olds):
            lanes = slice(g * MXU_TILE, (g + 1) * MXU_TILE)
            lhs = _head_transpose(dir_refs[d][2][bi, 0, pl.ds(r0s[d], CHUNK), lanes], lo_half).astype(BF16)
            if want_o:
                lhs = jnp.concatenate([dir_refs[d][5][bi, 0, pl.ds(r0s[d], CHUNK), lanes], lhs], axis=0)
            r2 = _dot(lhs, _bdiag(u_bf, lo_half))
            dl = dir_refs[d][3][bi, 0, cidx[d]][:, lanes]
            s_scr[bi, d * n_tiles + g] = s_old * dl + r2[r2.shape[0] - CHUNK:, :]
            if want_o:
                o_refs[d][bi, pl.ds(r0s[d], CHUNK), lanes] = (r1[CHUNK:2 * CHUNK] + r2[0:CHUNK]).astype(BF16)
        return carry

    lax.fori_loop(0, n_chunk, body, 0)

    if want_s:
        @pl.when(i == pl.num_programs(1) - 1)
        def _fin():
            sout_ref[...] = s_scr[...]


def _dnscan(prep, s0, tb, bb, want_s):
    want_o = len(prep) == 6
    b, _, t, w = prep[0].shape
    n_t = t // tb
    n_chunk = tb // CHUNK
    n_chain = N_DIR * (w // MXU_TILE)

    def specs(d):
        blk = (lambda bi, i: i) if d == 0 else (lambda bi, i: n_t - 1 - i)
        big = pl.BlockSpec((bb, 1, tb, w), lambda bi, i: (bi, d, blk(bi, i), 0))
        dl = pl.BlockSpec((bb, 1, n_chunk, 1, w), lambda bi, i: (bi, d, blk(bi, i), 0, 0))
        return [big, big, big, dl] + ([big, big] if want_o else [])

    in_specs = specs(0) + specs(1)
    args = list(prep) + list(prep)
    state_spec = pl.BlockSpec((bb, n_chain, CHUNK, MXU_TILE), lambda bi, i: (bi, 0, 0, 0))
    if s0 is not None:
        in_specs.append(state_spec)
        args.append(s0)
    out_shape, out_specs = [], []
    if want_o:
        out_shape += [jax.ShapeDtypeStruct((b, t, w), BF16)] * N_DIR
        out_specs += [pl.BlockSpec((bb, tb, w), lambda bi, i: (bi, i, 0)),
                      pl.BlockSpec((bb, tb, w), lambda bi, i: (bi, n_t - 1 - i, 0))]
    if want_s:
        out_shape.append(jax.ShapeDtypeStruct((b, n_chain, CHUNK, MXU_TILE), F32))
        out_specs.append(state_spec)
    return pl.pallas_call(
        functools.partial(_dnscan_kernel, n_chunk=n_chunk, bb=bb, want_o=want_o, have_s0=s0 is not None,
                          want_s=want_s),
        grid=(b // bb, n_t),
        in_specs=in_specs,
        out_specs=out_specs,
        out_shape=out_shape,
        scratch_shapes=[pltpu.VMEM((bb, n_chain, CHUNK, MXU_TILE), F32)],
        compiler_params=_cparams(2),
        name="dnscan",
    )(*args)


def _tail_kernel(x_ref, mod_ref, ya_ref, odf_ref, odb_ref, z_ref, gate_ref, dng_ref, havg_ref, wba_ref, wbd_ref,
                 wo_ref, gn2_ref, w1_ref, w2_ref, o_ref, *, ff_chunk):
    havg = havg_ref[...]
    dng = dng_ref[...]
    yd_parts = []
    for j in range(DN_WIDTH // MXU_TILE):
        sl = slice(j * MXU_TILE, (j + 1) * MXU_TILE)
        od = odf_ref[0, :, sl].astype(F32) + odb_ref[0, :, sl].astype(F32)
        ms = _dot((od * od).astype(BF16), havg)
        z = z_ref[0, :, sl].astype(F32)
        yd_parts.append((od * lax.rsqrt(ms + EPS) * dng * (z * _sigmoid(z))).astype(BF16))
    yd = jnp.concatenate(yd_parts, axis=1)
    ga = gate_ref[0, :, 0:D_MODEL].astype(F32)
    gd = gate_ref[0, :, D_MODEL:2 * D_MODEL].astype(F32)
    y = _sigmoid(ga) * _dot(ya_ref[0], wba_ref[...]) + _sigmoid(gd) * _dot(yd, wbd_ref[...])
    br = _dot(y.astype(BF16), wo_ref[...])
    mod = mod_ref[0]
    out1 = x_ref[0] + mod[2:3] * br
    ms2 = jnp.mean(out1 * out1, axis=-1, keepdims=True)
    hm = (out1 * lax.rsqrt(ms2 + EPS) * (gn2_ref[...] * (1.0 + mod[4:5])) + mod[3:4]).astype(BF16)
    acc = None
    for j in range(D_FF // ff_chunk):
        a = jnp.maximum(_dot(hm, w1_ref[:, j * ff_chunk:(j + 1) * ff_chunk]), 0.0)
        part = _dot((a * a).astype(BF16), w2_ref[j * ff_chunk:(j + 1) * ff_chunk, :])
        acc = part if acc is None else acc + part
    o_ref[0] = out1 + mod[5:6] * acc


def _resident(shape):
    return pl.BlockSpec(shape, lambda bi, i: (0,) * len(shape), pipeline_mode=pl.Buffered(1))


def _tail(x, mod3, y_attn, o_df, o_db, z, gates, dng, havg, wba, wbd, wo, gn2, w1, w2, tm):
    b, t, d = x.shape
    tok = lambda bi, i: (bi, i, 0)
    return pl.pallas_call(
        functools.partial(_tail_kernel, ff_chunk=MLP_FF_CHUNK),
        grid=(b, t // tm),
        in_specs=[pl.BlockSpec((1, tm, d), tok),
                  pl.BlockSpec((1, 6, d), lambda bi, i: (bi, 0, 0)),
                  pl.BlockSpec((1, tm, ATTN_WIDTH), tok),
                  pl.BlockSpec((1, tm, DN_WIDTH), tok),
                  pl.BlockSpec((1, tm, DN_WIDTH), tok),
                  pl.BlockSpec((1, tm, DN_WIDTH), tok),
                  pl.BlockSpec((1, tm, 2 * d), tok),
                  _resident((1, MXU_TILE)),
                  _resident((MXU_TILE, MXU_TILE)),
                  _resident(wba.shape),
                  _resident(wbd.shape),
                  _resident(wo.shape),
                  _resident((1, d)),
                  _resident(w1.shape),
                  _resident(w2.shape)],
        out_specs=pl.BlockSpec((1, tm, d), tok),
        out_shape=jax.ShapeDtypeStruct((b, t, d), F32),
        compiler_params=pltpu.CompilerParams(
            dimension_semantics=("arbitrary",) * 2, vmem_limit_bytes=VMEM_LIMIT,
            allow_input_fusion=[i in (9, 10, 11, 13, 14) for i in range(15)]),
        name="tail",
    )(x, mod3, y_attn, o_df, o_db, z, gates, dng, havg, wba, wbd, wo, gn2, w1, w2)


def _head_avg(n, scale):
    idx = np.arange(n) // HEAD_DIM
    return jnp.asarray((idx[:, None] == idx[None, :]).astype(np.float32) * scale, BF16)


def _dn_expand_matrix():
    n = N_DIR * DN_HEADS
    m = np.zeros((N_DIR, 4 * n, 2 * DN_WIDTH), np.float32)
    for d in range(N_DIR):
        for part in range(2):
            for h in range(DN_HEADS):
                idx = d * DN_HEADS + h
                m[d, part * 2 * n + idx, h * HEAD_DIM:(h + 1) * HEAD_DIM] = 1.0
                m[d, part * 2 * n + n + idx, DN_WIDTH + h * HEAD_DIM:DN_WIDTH + (h + 1) * HEAD_DIM] = 1.0
    return jnp.asarray(m, BF16)


def _tri_matrices():
    i = np.arange(CHUNK)
    low = (i[:, None] >= i[None, :]).astype(np.float32)
    up = (i[:, None] <= i[None, :]).astype(np.float32)
    return jnp.asarray(np.stack([np.concatenate([low, low], axis=1), np.concatenate([up, up], axis=1)]), BF16)


def _rope_tables(seq):
    half = HEAD_DIM // 2
    n_freq = half // 2
    freqs = ROPE_BASE ** (-jnp.arange(n_freq, dtype=F32) / n_freq)
    pos = jnp.arange(seq)
    ang_r = (pos // GRID_W).astype(F32)[:, None] * freqs
    ang_c = (pos % GRID_W).astype(F32)[:, None] * freqs
    cos = jnp.concatenate([jnp.cos(ang_r)] * 2 + [jnp.cos(ang_c)] * 2, axis=1)
    sin = jnp.concatenate([-jnp.sin(ang_r), jnp.sin(ang_r), -jnp.sin(ang_c), jnp.sin(ang_c)], axis=1)
    reps = LANES // HEAD_DIM
    return jnp.tile(cos, (1, reps)), jnp.tile(sin, (1, reps))


def _pad_cols(w, n):
    return jnp.pad(w, ((0, 0), (0, n - w.shape[1])))


def kernel(x, c, ctx, c_ctx, w_ada, b_ada, g_norm1, w_in, q_norm_g, k_norm_g, attn_sink, conv_w, a_log, dt_bias,
           dn_norm_g, w_br_attn, w_br_dn, w_out, g_norm2, w_mlp1, w_mlp2):
    depth = w_ada.shape[0]
    assert depth == 1, "single-layer trunk only"
    b, s, d = x.shape
    n_ctx = ctx.shape[1]
    assert d == D_MODEL and w_in.shape[-1] == _IN_WIDTH
    assert s >= 3 * ATTN_BLOCK and s % ATTN_BLOCK == 0 and s % CHUNK == 0 and n_ctx % CHUNK == 0
    out_dtype = x.dtype
    w_in_t = jnp.swapaxes(w_in[0], 0, 1).astype(BF16)

    mod_rows = -(-(b + 1) // 16) * 16
    cc = jnp.concatenate([c.astype(F32), c_ctx.astype(F32)[None], jnp.zeros((mod_rows - b - 1, d), F32)], axis=0)
    mod = _ada(cc, w_ada[0], b_ada[0])
    mod3 = mod.reshape(mod_rows, 6, d)

    ab_rows = w_in_t[_OFF_DA:_OFF_GA]
    wab_t = jnp.concatenate([ab_rows, ab_rows, jnp.zeros((LANES - 2 * (_OFF_GA - _OFF_DA), d), BF16)], axis=0)
    hsum = _head_avg(MXU_TILE, 1.0)
    segs_lat = ((0, 0, _OFF_DQ, None),
                (0, _OFF_DQ, DN_WIDTH, (0, True, True)), (0, _OFF_DK, DN_WIDTH, (DN_WIDTH, True, False)),
                (0, _OFF_DV, DN_WIDTH, (2 * DN_WIDTH, False, False)),
                (0, _OFF_DZ, DN_WIDTH, None), (0, _OFF_GA, 2 * D_MODEL, None), (1, 0, LANES, None))
    a_lat, q_d, k_d, v_d, z_lat, gates, ab_lat = _inproj(x, mod3, None, g_norm1[0], w_in_t, wab_t, conv_w[0], hsum,
                                                         segs_lat, (BF16, BF16, BF16, BF16, BF16, BF16, F32),
                                                         tm=min(INPROJ_TOKEN_TILE, s))
    segs_ctx = ((0, _OFF_AK, 2 * KV_WIDTH, None),
                (0, _OFF_DK, DN_WIDTH, (0, True, False)), (0, _OFF_DV, DN_WIDTH, (DN_WIDTH, False, False)),
                (1, 0, LANES, None))
    kv_ctx, k_dc, v_dc, ab_ctx = _inproj(ctx, mod3, b, g_norm1[0], w_in_t, wab_t, conv_w[0][:, DN_WIDTH:], hsum,
                                         segs_ctx, (BF16, BF16, BF16, F32), tm=n_ctx)

    cos, sin = _rope_tables(s)
    reps = LANES // HEAD_DIM
    n_gate = N_DIR * DN_HEADS
    arow = _pad_cols(jnp.tile(jnp.concatenate([jnp.exp(a_log[0]).reshape(1, n_gate), jnp.zeros((1, n_gate), F32)],
                                              axis=1), (1, 2)), LANES)
    dtrow = _pad_cols(jnp.tile(jnp.concatenate([dt_bias[0].reshape(1, n_gate), jnp.zeros((1, n_gate), F32)],
                                               axis=1), (1, 2)), LANES)
    exp2, tri = _dn_expand_matrix(), _tri_matrices()
    dn_cb = DN_CHUNKS_PER_STEP
    qb = (s // ATTN_BLOCK) // (s // (dn_cb * CHUNK))
    attn_parts = _attn_parts(a_lat, kv_ctx, cos, sin,
                             jnp.tile(q_norm_g[0].astype(F32), reps)[None],
                             jnp.tile(k_norm_g[0].astype(F32), reps)[None],
                             jnp.broadcast_to(attn_sink[0].astype(F32)[:, None], (ATTN_Q_HEADS, LANES)),
                             _head_avg(LANES, 1.0 / HEAD_DIM), qb)
    (y_attn,), prep_lat = _run_parts(
        [attn_parts, _dnprep_parts(q_d, k_d, v_d, ab_lat, arow, dtrow, exp2, tri, dn_cb)], b, "attn_dnprep")
    (prep_ctx,) = _run_parts(
        [_dnprep_parts(None, k_dc, v_dc, ab_ctx, arow, dtrow, exp2, tri, min(dn_cb, n_ctx // CHUNK))], b, "dnprep")

    bb = max(q for q in range(1, SCAN_BATCH_ROWS + 1) if b % q == 0)
    (s_ctx,) = _dnscan(prep_ctx, None, tb=n_ctx, bb=bb, want_s=True)
    o_df, o_db = _dnscan(prep_lat, s_ctx, tb=SCAN_TOKEN_TILE, bb=bb, want_s=False)

    out = _tail(x, mod3, y_attn, o_df, o_db, z_lat, gates,
                jnp.tile(dn_norm_g[0].astype(F32), HEADS_PER_TILE)[None], _head_avg(MXU_TILE, 1.0 / HEAD_DIM),
                w_br_attn[0].astype(BF16), w_br_dn[0].astype(BF16), w_out[0].astype(BF16),
                g_norm2[0].reshape(1, d), w_mlp1[0].astype(BF16), w_mlp2[0].astype(BF16), tm=min(TAIL_TOKEN_TILE, s))
    return out.astype(out_dtype)
```
